```python
import math
import jax, jax.numpy as jnp
from jax import lax
import numpy as np

D_MODEL = 1024
BATCH = 2
SEQ = 16384
DEPTH = 2

HEAD_DIM = 64
EPS = 1e-6
S5_WIDTH = 256
S5_GROUP = 16
S5_GROUPS = S5_WIDTH // S5_GROUP
S5_STATE = 64
CONV_WIDTH = D_MODEL - S5_WIDTH
CONV_HEADS = CONV_WIDTH // HEAD_DIM
CONV_K = 3
IN_EVEN = S5_WIDTH + 3 * CONV_WIDTH
MOBA_HEADS = 4
NSA_HEADS = 12
NSA_KV_HEADS = 2
NSA_GROUP = NSA_HEADS // NSA_KV_HEADS
MOBA_W = MOBA_HEADS * HEAD_DIM
NSA_W = NSA_HEADS * HEAD_DIM
KV_W = NSA_KV_HEADS * HEAD_DIM
IN_ODD = 3 * MOBA_W + NSA_W + 6 * KV_W + 3 * NSA_HEADS
MOBA_BLOCK = 256
MOBA_TOPK = 3
CMP_BLOCK = 32
CMP_STRIDE = 16
CMP_HIDDEN = 256
SEL_BLOCK = 64
SEL_TOPK = 8
WINDOW = 512
Q_BLOCK = 64
FORCED_SCORE = 1e4
N_GROUPS = 4
EXPERTS_PER_GROUP = 4
N_EXPERTS = N_GROUPS * EXPERTS_PER_GROUP
EXPERT_TOPK = 2
EXPERT_FF = 256

kernel_name = 'hybrid_s5_conv_moba_nsa_hmoe'


def rms_norm(x, g):
    xf = x.astype(jnp.float32)
    y = xf * lax.rsqrt(jnp.mean(xf * xf, axis=-1, keepdims=True) + EPS)
    return (y * g.astype(jnp.float32)).astype(x.dtype)


def masked_softmax(s, mask):
    s = jnp.where(mask, s.astype(jnp.float32), -jnp.inf)
    m = jnp.max(s, axis=-1, keepdims=True)
    m = jnp.where(jnp.isfinite(m), m, 0.0)
    p = jnp.exp(s - m)
    return p / jnp.maximum(jnp.sum(p, axis=-1, keepdims=True), 1e-30)


def gather_blocks(blocks, idx):
    return jax.vmap(jax.vmap(lambda b, i: b[i]))(blocks, idx)


def s5_mixer(u, lam_re, lam_im, log_dt, b_re, b_im, c_re, c_im, d, w_glu):
    B, L, _ = u.shape
    uf = u.astype(jnp.float32).reshape(B, L, S5_GROUPS, S5_GROUP)
    lr = lam_re.astype(jnp.float32)
    li = lam_im.astype(jnp.float32)
    dt = jnp.exp(log_dt.astype(jnp.float32))[:, None]
    mag = jnp.exp(lr * dt)
    a_re = mag * jnp.cos(li * dt)
    a_im = mag * jnp.sin(li * dt)
    den = lr * lr + li * li
    f_re = ((a_re - 1.0) * lr + a_im * li) / den
    f_im = (a_im * lr - (a_re - 1.0) * li) / den
    br = b_re.astype(jnp.float32)
    bi = b_im.astype(jnp.float32)
    bb_re = f_re[..., None] * br - f_im[..., None] * bi
    bb_im = f_re[..., None] * bi + f_im[..., None] * br
    bu_re = jnp.einsum('blgh,gph->blgp', uf, bb_re)
    bu_im = jnp.einsum('blgh,gph->blgp', uf, bb_im)
    ar = jnp.broadcast_to(a_re, bu_re.shape)
    ai = jnp.broadcast_to(a_im, bu_re.shape)

    def combine(e1, e2):
        a1r, a1i, b1r, b1i = e1
        a2r, a2i, b2r, b2i = e2
        return (a2r * a1r - a2i * a1i, a2r * a1i + a2i * a1r,
                a2r * b1r - a2i * b1i + b2r, a2r * b1i + a2i * b1r + b2i)

    _, _, xr, xi = lax.associative_scan(combine, (ar, ai, bu_re, bu_im), axis=1)
    y = (jnp.einsum('blgp,ghp->blgh', xr, c_re.astype(jnp.float32))
         - jnp.einsum('blgp,ghp->blgh', xi, c_im.astype(jnp.float32))
         + d.astype(jnp.float32).reshape(S5_GROUPS, S5_GROUP) * uf)
    y = jax.nn.gelu(y.reshape(B, L, S5_WIDTH))
    y = y * jax.nn.sigmoid(y @ w_glu.astype(jnp.float32))
    return y.astype(u.dtype)


def short_conv_mixer(xc, gb, gc, conv_w, conv_b):
    z = gc * xc
    w = conv_w.astype(z.dtype)[:, None, :]
    y = lax.conv_general_dilated(z, w, window_strides=(1,), padding=[(CONV_K - 1, 0)],
                                 dimension_numbers=('NWC', 'WIO', 'NWC'),
                                 feature_group_count=CONV_WIDTH) + conv_b
    return gb * y


def even_mixer(h, w_in, w_out, lam_re, lam_im, log_dt, b_re, b_im, c_re, c_im, d, w_glu, conv_w, conv_b):
    proj = h @ w_in
    u, xc, gb, gc = jnp.split(proj, [S5_WIDTH, S5_WIDTH + CONV_WIDTH, S5_WIDTH + 2 * CONV_WIDTH], axis=-1)
    y_a = s5_mixer(u, lam_re, lam_im, log_dt, b_re, b_im, c_re, c_im, d, w_glu)
    y_b = short_conv_mixer(xc, gb, gc, conv_w, conv_b)
    return jnp.concatenate([y_a, y_b.astype(y_a.dtype)], axis=-1) @ w_out


def compress_blocks(k, pe, w1, w2):
    B, L, H, hd = k.shape
    c = k.reshape(B, L // CMP_STRIDE, CMP_STRIDE, H, hd)
    blocks = jnp.concatenate([c[:, :-1], c[:, 1:]], axis=2) + pe[:, None, :]
    flat = blocks.transpose(0, 1, 3, 2, 4).reshape(B, -1, H, CMP_BLOCK * hd)
    return jax.nn.gelu(flat @ w1) @ w2


def odd_mixer(h, w_in, w_out, moba_q_norm, moba_k_norm, nsa_q_norm, nsa_kcmp_norm, nsa_ksel_norm,
              nsa_kwin_norm, cmp_pe_k, cmp_w1_k, cmp_w2_k, cmp_pe_v, cmp_w1_v, cmp_w2_v):
    B, L, _ = h.shape
    hd = HEAD_DIM
    Lp = -(-L // MOBA_BLOCK) * MOBA_BLOCK
    hp = jnp.pad(h, ((0, 0), (0, Lp - L), (0, 0)))
    proj = hp @ w_in
    sizes = [MOBA_W] * 3 + [NSA_W] + [KV_W] * 6 + [3 * NSA_HEADS]
    qm, km, vm, qd, kc, vc, ks, vs, kw, vw, gl = jnp.split(proj, np.cumsum(sizes)[:-1].tolist(), axis=-1)
    qm = rms_norm(qm.reshape(B, Lp, MOBA_HEADS, hd), moba_q_norm)
    km = rms_norm(km.reshape(B, Lp, MOBA_HEADS, hd), moba_k_norm)
    vm = vm.reshape(B, Lp, MOBA_HEADS, hd)
    qd = rms_norm(qd.reshape(B, Lp, NSA_KV_HEADS, NSA_GROUP, hd), nsa_q_norm)
    kv_shape = (B, Lp, NSA_KV_HEADS, hd)
    k_cmp = rms_norm(compress_blocks(kc.reshape(kv_shape), cmp_pe_k, cmp_w1_k, cmp_w2_k), nsa_kcmp_norm)
    v_cmp = compress_blocks(vc.reshape(kv_shape), cmp_pe_v, cmp_w1_v, cmp_w2_v)
    n_cmp = k_cmp.shape[1]
    n_sel = Lp // SEL_BLOCK
    k_sel = rms_norm(ks.reshape(kv_shape), nsa_ksel_norm).reshape(
        B, n_sel, SEL_BLOCK, NSA_KV_HEADS, hd).transpose(0, 3, 1, 2, 4)
    v_sel = vs.reshape(B, n_sel, SEL_BLOCK, NSA_KV_HEADS, hd).transpose(0, 3, 1, 2, 4)
    pad_w = ((0, 0), (WINDOW, 0), (0, 0), (0, 0))
    k_win = jnp.pad(rms_norm(kw.reshape(kv_shape), nsa_kwin_norm), pad_w)
    v_win = jnp.pad(vw.reshape(kv_shape), pad_w)
    gates = jax.nn.sigmoid(gl.astype(jnp.float32)).reshape(B, Lp, 3, NSA_KV_HEADS, NSA_GROUP)
    n_mb = Lp // MOBA_BLOCK
    km_blocks = km.reshape(B, n_mb, MOBA_BLOCK, MOBA_HEADS, hd).transpose(0, 3, 1, 2, 4)
    vm_blocks = vm.reshape(B, n_mb, MOBA_BLOCK, MOBA_HEADS, hd).transpose(0, 3, 1, 2, 4)
    km_mean = jnp.mean(km_blocks.astype(jnp.float32), axis=3)
    k_moba = min(MOBA_TOPK, n_mb)
    k_nsa = min(SEL_TOPK, n_sel)
    scale = hd ** -0.5
    cmp_end = jnp.arange(n_cmp) * CMP_STRIDE + CMP_BLOCK - 1
    blk_ids = jnp.arange(n_sel)
    mb_ids = jnp.arange(n_mb)

    def block(qi):
        s = qi * Q_BLOCK
        t = s + jnp.arange(Q_BLOCK)
        q = lax.dynamic_slice_in_dim(qm, s, Q_BLOCK, axis=1)
        cur = s // MOBA_BLOCK
        k_own = lax.dynamic_slice_in_dim(km, cur * MOBA_BLOCK, MOBA_BLOCK, axis=1)
        v_own = lax.dynamic_slice_in_dim(vm, cur * MOBA_BLOCK, MOBA_BLOCK, axis=1)
        own_pos = cur * MOBA_BLOCK + jnp.arange(MOBA_BLOCK)
        s_own = jnp.einsum('bqhd,bkhd->bhqk', q, k_own) * scale
        m_own = jnp.broadcast_to(own_pos[None, :] <= t[:, None], s_own.shape)
        gate = jnp.einsum('bqhd,bhnd->bhqn', q.astype(jnp.float32), km_mean)
        gate = jnp.where(mb_ids < cur, gate, -jnp.inf)
        g_val, g_idx = lax.top_k(gate, k_moba)
        k_g = gather_blocks(km_blocks, g_idx)
        v_g = gather_blocks(vm_blocks, g_idx)
        s_g = jnp.einsum('bqhd,bhqnsd->bhqns', q, k_g).reshape(
            B, MOBA_HEADS, Q_BLOCK, k_moba * MOBA_BLOCK) * scale
        m_g = jnp.broadcast_to(jnp.isfinite(g_val)[..., None],
                               (B, MOBA_HEADS, Q_BLOCK, k_moba, MOBA_BLOCK)).reshape(s_g.shape)
        p = masked_softmax(jnp.concatenate([s_own, s_g], axis=-1), jnp.concatenate([m_own, m_g], axis=-1))
        o_moba = (jnp.einsum('bhqk,bkhd->bqhd', p[..., :MOBA_BLOCK], v_own)
                  + jnp.einsum('bhqm,bhqmd->bqhd', p[..., MOBA_BLOCK:],
                               v_g.reshape(B, MOBA_HEADS, Q_BLOCK, k_moba * MOBA_BLOCK, hd)))
        qn = lax.dynamic_slice_in_dim(qd, s, Q_BLOCK, axis=1)
        s_c = jnp.einsum('bqhgd,bnhd->bhgqn', qn, k_cmp) * scale
        p_c = masked_softmax(s_c, cmp_end[None, :] <= t[:, None])
        o_c = jnp.einsum('bhgqn,bnhd->bqhgd', p_c, v_cmp)
        imp = jnp.pad(p_c.sum(axis=2), ((0, 0), (0, 0), (0, 0), (0, 1)))
        r = imp.reshape(B, NSA_KV_HEADS, Q_BLOCK, n_sel, SEL_BLOCK // CMP_STRIDE)
        p_slc = r.sum(-1) + jnp.pad(r[..., :-1, -1], ((0, 0), (0, 0), (0, 0), (1, 0)))
        cur_s = (t // SEL_BLOCK)[:, None]
        forced = (blk_ids == 0) | (blk_ids == cur_s) | (blk_ids == cur_s - 1)
        score = jnp.where(forced, FORCED_SCORE, jnp.where(blk_ids < cur_s, p_slc, -jnp.inf))
        s_val, s_idx = lax.top_k(score, k_nsa)
        k_b = gather_blocks(k_sel, s_idx)
        v_b = gather_blocks(v_sel, s_idx)
        pos = s_idx[..., None] * SEL_BLOCK + jnp.arange(SEL_BLOCK)
        m_s = (jnp.isfinite(s_val)[..., None] & (pos <= t[:, None, None])).reshape(
            B, NSA_KV_HEADS, Q_BLOCK, k_nsa * SEL_BLOCK)[:, :, None]
        s_s = jnp.einsum('bqhgd,bhqnsd->bhgqns', qn, k_b).reshape(
            B, NSA_KV_HEADS, NSA_GROUP, Q_BLOCK, k_nsa * SEL_BLOCK) * scale
        o_s = jnp.einsum('bhgqm,bhqmd->bqhgd', masked_softmax(s_s, m_s),
                         v_b.reshape(B, NSA_KV_HEADS, Q_BLOCK, k_nsa * SEL_BLOCK, hd))
        k_w = lax.dynamic_slice_in_dim(k_win, s, Q_BLOCK + WINDOW, axis=1)
        v_w = lax.dynamic_slice_in_dim(v_win, s, Q_BLOCK + WINDOW, axis=1)
        wpos = s - WINDOW + jnp.arange(Q_BLOCK + WINDOW)
        m_w = (wpos[None, :] <= t[:, None]) & (wpos[None, :] > t[:, None] - WINDOW) & (wpos[None, :] >= 0)
        s_w = jnp.einsum('bqhgd,bkhd->bhgqk', qn, k_w) * scale
        o_w = jnp.einsum('bhgqk,bkhd->bqhgd', masked_softmax(s_w, m_w), v_w)
        g = lax.dynamic_slice_in_dim(gates, s, Q_BLOCK, axis=1)
        o_nsa = (g[:, :, 0][..., None] * o_c + g[:, :, 1][..., None] * o_s
                 + g[:, :, 2][..., None] * o_w)
        o = jnp.concatenate([o_moba.reshape(B, Q_BLOCK, MOBA_W),
                             o_nsa.reshape(B, Q_BLOCK, NSA_W)], axis=-1)
        return o.astype(h.dtype)

    out = lax.map(block, jnp.arange(Lp // Q_BLOCK))
    out = out.transpose(1, 0, 2, 3).reshape(B, Lp, D_MODEL)[:, :L]
    return out @ w_out


def hier_moe(h, w_group, b_group, w_expert, b_expert, w_gate, w_up, w_down):
    B, L, D = h.shape
    xt = h.reshape(B * L, D)
    gl = (xt @ w_group).astype(jnp.float32) + b_group.astype(jnp.float32)
    gp = jax.nn.softmax(gl, axis=-1)
    gi = jnp.argmax(gl, axis=-1)
    gw = jnp.take_along_axis(gp, gi[:, None], axis=-1)
    el = ((xt @ w_expert).astype(jnp.float32) + b_expert.astype(jnp.float32)).reshape(
        B * L, N_GROUPS, EXPERTS_PER_GROUP)
    el = jnp.take_along_axis(el, gi[:, None, None], axis=1)[:, 0]
    ev, ei = lax.top_k(jax.nn.softmax(el, axis=-1), EXPERT_TOPK)
    ev = ev / jnp.sum(ev, axis=-1, keepdims=True)
    eidx = gi[:, None] * EXPERTS_PER_GROUP + ei
    gate = jnp.sum(jax.nn.one_hot(eidx, N_EXPERTS, dtype=jnp.float32) * (gw * ev)[..., None], axis=1)
    h1 = jnp.einsum('td,edf->tef', xt, w_gate)
    h3 = jnp.einsum('td,edf->tef', xt, w_up)
    act = jax.nn.silu(h1) * h3 * gate[..., None].astype(h1.dtype)
    y = jnp.einsum('tef,efd->td', act, w_down)
    return y.reshape(B, L, D).astype(h.dtype)


def setup_inputs(seed: int = 0) -> dict:
    key = jax.random.key(seed)
    ks = iter(jax.random.split(key, 64))
    NE = (DEPTH + 1) // 2
    NO = DEPTH // 2

    def nrm(shape, scale):
        return jax.random.normal(next(ks), shape, jnp.float32) * scale

    def gain(shape):
        return 1.0 + nrm(shape, 0.01)

    x = nrm((BATCH, SEQ, D_MODEL), 1.0)
    ev_norm_mix = gain((NE, D_MODEL))
    ev_w_in = nrm((NE, D_MODEL, IN_EVEN), D_MODEL ** -0.5)
    ev_w_out = nrm((NE, D_MODEL, D_MODEL), D_MODEL ** -0.5)
    s5_lam_re = -0.5 * jnp.exp(nrm((NE, S5_GROUPS, S5_STATE), 0.1))
    s5_lam_im = math.pi * jnp.arange(S5_STATE, dtype=jnp.float32) + nrm((NE, S5_GROUPS, S5_STATE), 0.01)
    s5_log_dt = jax.random.uniform(next(ks), (NE, S5_GROUPS), jnp.float32, math.log(1e-3), math.log(1e-1))
    s5_b_re = nrm((NE, S5_GROUPS, S5_STATE, S5_GROUP), (2 * S5_GROUP) ** -0.5)
    s5_b_im = nrm((NE, S5_GROUPS, S5_STATE, S5_GROUP), (2 * S5_GROUP) ** -0.5)
    s5_c_re = nrm((NE, S5_GROUPS, S5_GROUP, S5_STATE), S5_STATE ** -0.5)
    s5_c_im = nrm((NE, S5_GROUPS, S5_GROUP, S5_STATE), S5_STATE ** -0.5)
    s5_d = nrm((NE, S5_WIDTH), 0.5)
    s5_w_glu = nrm((NE, S5_WIDTH, S5_WIDTH), S5_WIDTH ** -0.5)
    conv_w = nrm((NE, CONV_K, CONV_WIDTH), CONV_K ** -0.5)
    conv_b = nrm((NE, CONV_WIDTH), 0.01)
    od_norm_mix = gain((NO, D_MODEL))
    od_w_in = nrm((NO, D_MODEL, IN_ODD), D_MODEL ** -0.5)
    od_w_out = nrm((NO, D_MODEL, D_MODEL), D_MODEL ** -0.5)
    moba_q_norm = gain((NO, HEAD_DIM))
    moba_k_norm = gain((NO, HEAD_DIM))
    nsa_q_norm = gain((NO, HEAD_DIM))
    nsa_kcmp_norm = gain((NO, HEAD_DIM))
    nsa_ksel_norm = gain((NO, HEAD_DIM))
    nsa_kwin_norm = gain((NO, HEAD_DIM))
    cmp_pe_k = nrm((NO, CMP_BLOCK, HEAD_DIM), 0.02)
    cmp_w1_k = nrm((NO, CMP_BLOCK * HEAD_DIM, CMP_HIDDEN), (CMP_BLOCK * HEAD_DIM) ** -0.5)
    cmp_w2_k = nrm((NO, CMP_HIDDEN, HEAD_DIM), CMP_HIDDEN ** -0.5)
    cmp_pe_v = nrm((NO, CMP_BLOCK, HEAD_DIM), 0.02)
    cmp_w1_v = nrm((NO, CMP_BLOCK * HEAD_DIM, CMP_HIDDEN), (CMP_BLOCK * HEAD_DIM) ** -0.5)
    cmp_w2_v = nrm((NO, CMP_HIDDEN, HEAD_DIM), CMP_HIDDEN ** -0.5)
    moe_norm = gain((DEPTH, D_MODEL))
    moe_w_group = nrm((DEPTH, D_MODEL, N_GROUPS), D_MODEL ** -0.5)
    moe_b_group = nrm((DEPTH, N_GROUPS), 0.01)
    moe_w_expert = nrm((DEPTH, D_MODEL, N_EXPERTS), D_MODEL ** -0.5)
    moe_b_expert = nrm((DEPTH, N_EXPERTS), 0.01)
    moe_w_gate = nrm((DEPTH, N_EXPERTS, D_MODEL, EXPERT_FF), D_MODEL ** -0.5)
    moe_w_up = nrm((DEPTH, N_EXPERTS, D_MODEL, EXPERT_FF), D_MODEL ** -0.5)
    moe_w_down = nrm((DEPTH, N_EXPERTS, EXPERT_FF, D_MODEL), EXPERT_FF ** -0.5)
    return {'x': x, 'ev_norm_mix': ev_norm_mix, 'ev_w_in': ev_w_in, 'ev_w_out': ev_w_out,
            's5_lam_re': s5_lam_re, 's5_lam_im': s5_lam_im, 's5_log_dt': s5_log_dt,
            's5_b_re': s5_b_re, 's5_b_im': s5_b_im, 's5_c_re': s5_c_re, 's5_c_im': s5_c_im,
            's5_d': s5_d, 's5_w_glu': s5_w_glu, 'conv_w': conv_w, 'conv_b': conv_b,
            'od_norm_mix': od_norm_mix, 'od_w_in': od_w_in, 'od_w_out': od_w_out,
            'moba_q_norm': moba_q_norm, 'moba_k_norm': moba_k_norm, 'nsa_q_norm': nsa_q_norm,
            'nsa_kcmp_norm': nsa_kcmp_norm, 'nsa_ksel_norm': nsa_ksel_norm, 'nsa_kwin_norm': nsa_kwin_norm,
            'cmp_pe_k': cmp_pe_k, 'cmp_w1_k': cmp_w1_k, 'cmp_w2_k': cmp_w2_k,
            'cmp_pe_v': cmp_pe_v, 'cmp_w1_v': cmp_w1_v, 'cmp_w2_v': cmp_w2_v,
            'moe_norm': moe_norm, 'moe_w_group': moe_w_group, 'moe_b_group': moe_b_group,
            'moe_w_expert': moe_w_expert, 'moe_b_expert': moe_b_expert,
            'moe_w_gate': moe_w_gate, 'moe_w_up': moe_w_up, 'moe_w_down': moe_w_down}


def reference(x, ev_norm_mix, ev_w_in, ev_w_out, s5_lam_re, s5_lam_im, s5_log_dt, s5_b_re, s5_b_im,
              s5_c_re, s5_c_im, s5_d, s5_w_glu, conv_w, conv_b, od_norm_mix, od_w_in, od_w_out,
              moba_q_norm, moba_k_norm, nsa_q_norm, nsa_kcmp_norm, nsa_ksel_norm, nsa_kwin_norm,
              cmp_pe_k, cmp_w1_k, cmp_w2_k, cmp_pe_v, cmp_w1_v, cmp_w2_v,
              moe_norm, moe_w_group, moe_b_group, moe_w_expert, moe_b_expert,
              moe_w_gate, moe_w_up, moe_w_down):
    h = x
    for layer in range(DEPTH):
        i = layer // 2
        if layer % 2 == 0:
            h = h + even_mixer(rms_norm(h, ev_norm_mix[i]), ev_w_in[i], ev_w_out[i],
                               s5_lam_re[i], s5_lam_im[i], s5_log_dt[i], s5_b_re[i], s5_b_im[i],
                               s5_c_re[i], s5_c_im[i], s5_d[i], s5_w_glu[i], conv_w[i], conv_b[i])
        else:
            h = h + odd_mixer(rms_norm(h, od_norm_mix[i]), od_w_in[i], od_w_out[i],
                              moba_q_norm[i], moba_k_norm[i], nsa_q_norm[i], nsa_kcmp_norm[i],
                              nsa_ksel_norm[i], nsa_kwin_norm[i], cmp_pe_k[i], cmp_w1_k[i], cmp_w2_k[i],
                              cmp_pe_v[i], cmp_w1_v[i], cmp_w2_v[i])
        h = h + hier_moe(rms_norm(h, moe_norm[layer]), moe_w_group[layer], moe_b_group[layer],
                         moe_w_expert[layer], moe_b_expert[layer], moe_w_gate[layer],
                         moe_w_up[layer], moe_w_down[layer])
    return h
```

```python
import functools
import math

import jax
import jax.numpy as jnp
import numpy as np
from jax import lax
from jax.experimental import pallas as pl
from jax.experimental.pallas import tpu as pltpu

D_MODEL = 1024
HEAD_DIM = 64
EPS = 1e-6
S5_WIDTH = 256
S5_GROUP = 16
S5_GROUPS = 16
S5_STATE = 64
S5_CHUNK = 16
CONV_WIDTH = 768
CONV_K = 3
MOBA_HEADS = 4
NSA_HEADS = 12
NSA_KV_HEADS = 2
NSA_GROUP = 6
MOBA_W = 256
NSA_W = 768
KV_W = 128
MOBA_BLOCK = 256
MOBA_TOPK = 3
CMP_BLOCK = 32
CMP_STRIDE = 16
CMP_HIDDEN = 256
SEL_BLOCK = 64
SEL_TOPK = 8
WINDOW = 512
N_GROUPS = 4
EXPERTS_PER_GROUP = 4
N_EXPERTS = 16
EXPERT_FF = 256

VMEM_LIMIT_BYTES = 56 * 1024 * 1024
NEG = -float(2 ** 30)
F32 = jnp.float32
BF16 = jnp.bfloat16


def _params(*semantics):
    return pltpu.CompilerParams(dimension_semantics=semantics, vmem_limit_bytes=VMEM_LIMIT_BYTES)


def _dot(a, b):
    return jnp.dot(a, b, preferred_element_type=F32)


def _dot_nt(a, b):
    return lax.dot_general(a, b, (((1,), (1,)), ((), ())), preferred_element_type=F32)


def _split(x):
    hi = x.astype(BF16)
    lo = (x - hi.astype(F32)).astype(BF16)
    return hi, lo


def _dot_x2(x, w):
    hi, lo = _split(x)
    return _dot(hi, w) + _dot(lo, w)


def _dot_x3(x, w_hi, w_lo):
    hi, lo = _split(x)
    return _dot(hi, w_hi) + (_dot(hi, w_lo) + _dot(lo, w_hi))


def _rms(x, gain):
    return x * lax.rsqrt(jnp.mean(x * x, axis=-1, keepdims=True) + EPS) * gain


def _gelu(x):
    return 0.5 * x * (1.0 + jnp.tanh(math.sqrt(2.0 / math.pi) * (x + 0.044715 * (x * x * x))))


def _sigmoid(x):
    return 1.0 / (1.0 + jnp.exp(-x))


def _full(shape):
    n = len(shape)
    return pl.BlockSpec(shape, lambda *_: (0,) * n)


def _even_in_kernel(tiles_per_seq, x_ref, g_ref, w_ref, cw_ref, cb_ref, u_ref, yb_ref, carry_ref):
    i = pl.program_id(0)
    xn = _rms(x_ref[...], g_ref[...]).astype(BF16)
    u_ref[...] = _dot(xn, w_ref[:, 0:S5_WIDTH])
    o = S5_WIDTH
    xc = _dot(xn, w_ref[:, o:o + CONV_WIDTH])
    gb = _dot(xn, w_ref[:, o + CONV_WIDTH:o + 2 * CONV_WIDTH])
    gc = _dot(xn, w_ref[:, o + 2 * CONV_WIDTH:o + 3 * CONV_WIDTH])
    z = gc * xc
    tm = z.shape[0]

    @pl.when(i % tiles_per_seq == 0)
    def _():
        carry_ref[...] = jnp.zeros_like(carry_ref)

    row = lax.broadcasted_iota(jnp.int32, z.shape, 0)
    prev1 = carry_ref[7:8, :]
    prev2 = carry_ref[6:7, :]
    z1 = jnp.where(row == 0, prev1, pltpu.roll(z, 1, 0))
    z2 = jnp.where(row == 0, prev2, jnp.where(row == 1, prev1, pltpu.roll(z, 2, 0)))
    y = cw_ref[0:1, :] * z2 + cw_ref[1:2, :] * z1 + cw_ref[2:3, :] * z + cb_ref[...]
    yb_ref[...] = (gb * y).astype(BF16)
    carry_ref[...] = z[tm - 8:tm, :]


def _even_in(x2, gain, w_in, conv_w, conv_b, seq, tm=512):
    t = x2.shape[0]
    n_in = w_in.shape[1]
    return pl.pallas_call(
        functools.partial(_even_in_kernel, seq // tm),
        grid=(t // tm,),
        in_specs=[pl.BlockSpec((tm, D_MODEL), lambda i: (i, 0)), _full((1, D_MODEL)),
                  _full((D_MODEL, n_in)), _full((CONV_K, CONV_WIDTH)), _full((1, CONV_WIDTH))],
        out_specs=[pl.BlockSpec((tm, S5_WIDTH), lambda i: (i, 0)),
                   pl.BlockSpec((tm, CONV_WIDTH), lambda i: (i, 0))],
        out_shape=[jax.ShapeDtypeStruct((t, S5_WIDTH), F32), jax.ShapeDtypeStruct((t, CONV_WIDTH), BF16)],
        scratch_shapes=[pltpu.VMEM((8, CONV_WIDTH), F32)],
        compiler_params=_params("arbitrary"),
        name="even_in",
    )(x2, gain, w_in, conv_w, conv_b)


def _s5_weights(lam_re, lam_im, log_dt, b_re, b_im, c_re, c_im):
    g, p, hg, ck = S5_GROUPS, S5_STATE, S5_GROUP, S5_CHUNK
    lr, li = lam_re.astype(F32), lam_im.astype(F32)
    dt = jnp.exp(log_dt.astype(F32))[:, None]
    mag = jnp.exp(lr * dt)
    a_re, a_im = mag * jnp.cos(li * dt), mag * jnp.sin(li * dt)
    den = lr * lr + li * li
    f_re = ((a_re - 1.0) * lr + a_im * li) / den
    f_im = (a_im * lr - (a_re - 1.0) * li) / den
    br, bi = b_re.astype(F32), b_im.astype(F32)
    bb_re = f_re[..., None] * br - f_im[..., None] * bi
    bb_im = f_re[..., None] * bi + f_im[..., None] * br
    pw_re, pw_im = [jnp.ones_like(a_re)], [jnp.zeros_like(a_im)]
    for _ in range(ck):
        r, m = pw_re[-1], pw_im[-1]
        pw_re.append(r * a_re - m * a_im)
        pw_im.append(r * a_im + m * a_re)
    pw_re, pw_im = jnp.stack(pw_re), jnp.stack(pw_im)
    cr, ci = c_re.astype(F32), c_im.astype(F32)
    rev_re, rev_im = pw_re[ck - 1::-1][:ck], pw_im[ck - 1::-1][:ck]
    ws_re = rev_re[:, :, :, None] * bb_re[None] - rev_im[:, :, :, None] * bb_im[None]
    ws_im = rev_re[:, :, :, None] * bb_im[None] + rev_im[:, :, :, None] * bb_re[None]
    ws_re = ws_re.transpose(1, 0, 3, 2).reshape(g, ck * hg, p)
    ws_im = ws_im.transpose(1, 0, 3, 2).reshape(g, ck * hg, p)
    ca_re = cr[None] * pw_re[1:, :, None, :] - ci[None] * pw_im[1:, :, None, :]
    ca_im = cr[None] * pw_im[1:, :, None, :] + ci[None] * pw_re[1:, :, None, :]
    wc_re = ca_re.transpose(1, 3, 0, 2).reshape(g, p, ck * hg)
    wc_im = (-ca_im).transpose(1, 3, 0, 2).reshape(g, p, ck * hg)
    cb_re = jnp.einsum('ghp,kgp,gpj->kghj', cr, pw_re[:ck], bb_re) - jnp.einsum('ghp,kgp,gpj->kghj', cr, pw_im[:ck], bb_im) \
        - jnp.einsum('ghp,kgp,gpj->kghj', ci, pw_re[:ck], bb_im) - jnp.einsum('ghp,kgp,gpj->kghj', ci, pw_im[:ck], bb_re)
    lag = np.arange(ck)[None, :] - np.arange(ck)[:, None]
    tz = cb_re[np.clip(lag, 0, ck - 1)]
    tz = jnp.where((lag >= 0)[:, :, None, None, None], tz, 0.0)
    tz = tz.transpose(2, 0, 4, 1, 3).reshape(g, ck * hg, ck * hg)

    def pair_blockdiag(m):
        m = m.reshape(g // 2, 2, m.shape[1], m.shape[2])
        z = jnp.zeros_like(m[:, 0])
        return jnp.concatenate([jnp.concatenate([m[:, 0], z], axis=2), jnp.concatenate([z, m[:, 1]], axis=2)], axis=1)

    ws = jnp.concatenate([pair_blockdiag(ws_re), pair_blockdiag(ws_im)], axis=2)
    wc = jnp.concatenate([pair_blockdiag(wc_re), pair_blockdiag(wc_im)], axis=1)
    tzp = pair_blockdiag(tz)
    a16_re = pw_re[ck].reshape(g // 2, 1, 2 * p)
    a16_im = pw_im[ck].reshape(g // 2, 1, 2 * p)
    return ws.astype(BF16), wc.astype(BF16), tzp.astype(BF16), a16_re, a16_im


def _s5_state_kernel(u_ref, ws_ref, s_ref):
    s_ref[0, 0] = _dot(u_ref[0, 0], ws_ref[0])


def _s5_scan_kernel(s_ref, are_ref, aim_ref, xprev_ref, st_ref):
    @pl.when(pl.program_id(0) == 0)
    def _():
        st_ref[...] = jnp.zeros_like(st_ref)

    a_re, a_im = are_ref[...], aim_ref[...]
    n = s_ref.shape[0]
    half = a_re.shape[1]

    def body(c, carry):
        xr, xi = carry
        xprev_ref[c, :, 0:half] = xr
        xprev_ref[c, :, half:2 * half] = xi
        s = s_ref[c]
        nr = a_re * xr - a_im * xi + s[:, 0:half]
        ni = a_re * xi + a_im * xr + s[:, half:2 * half]
        return nr, ni

    xr, xi = lax.fori_loop(0, n, body, (st_ref[:, 0:half], st_ref[:, half:2 * half]), unroll=8)
    st_ref[:, 0:half] = xr
    st_ref[:, half:2 * half] = xi


def _s5_out_kernel(u_ref, xp_ref, tz_ref, wc_ref, y_ref):
    y_ref[0, 0] = _dot(u_ref[0, 0], tz_ref[0]) + _dot(xp_ref[0, 0].astype(BF16), wc_ref[0])


def _s5_mixer_pre(u, batch, seq, lam_re, lam_im, log_dt, b_re, b_im, c_re, c_im):
    ws, wc, tz, a16_re, a16_im = _s5_weights(lam_re, lam_im, log_dt, b_re, b_im, c_re, c_im)
    ck, hg = S5_CHUNK, S5_GROUP
    nc = seq // ck
    npair = S5_GROUPS // 2
    pw = 2 * ck * hg
    sw = 4 * S5_STATE
    up = u.astype(BF16).reshape(batch, nc, ck, npair, 2, hg).transpose(0, 3, 1, 4, 2, 5).reshape(batch, npair, nc, pw)
    s = pl.pallas_call(
        _s5_state_kernel,
        grid=(batch, npair),
        in_specs=[pl.BlockSpec((1, 1, nc, pw), lambda b, p: (b, p, 0, 0)), pl.BlockSpec((1, pw, sw), lambda b, p: (p, 0, 0))],
        out_specs=pl.BlockSpec((1, 1, nc, sw), lambda b, p: (b, p, 0, 0)),
        out_shape=jax.ShapeDtypeStruct((batch, npair, nc, sw), F32),
        compiler_params=_params("arbitrary", "arbitrary"),
        name="s5_state",
    )(up, ws)
    rows = batch * npair
    s_t = s.transpose(2, 0, 1, 3).reshape(nc, rows, sw)
    are = jnp.broadcast_to(a16_re.reshape(1, npair, 2 * S5_STATE), (batch, npair, 2 * S5_STATE)).reshape(rows, 2 * S5_STATE)
    aim = jnp.broadcast_to(a16_im.reshape(1, npair, 2 * S5_STATE), (batch, npair, 2 * S5_STATE)).reshape(rows, 2 * S5_STATE)
    tc = min(nc, 256)
    xprev = pl.pallas_call(
        _s5_scan_kernel,
        grid=(nc // tc,),
        in_specs=[pl.BlockSpec((tc, rows, sw), lambda i: (i, 0, 0)), _full((rows, 2 * S5_STATE)), _full((rows, 2 * S5_STATE))],
        out_specs=pl.BlockSpec((tc, rows, sw), lambda i: (i, 0, 0)),
        out_shape=jax.ShapeDtypeStruct((nc, rows, sw), F32),
        scratch_shapes=[pltpu.VMEM((rows, sw), F32)],
        compiler_params=_params("arbitrary"),
        name="s5_scan",
    )(s_t, are, aim)
    xp = xprev.reshape(nc, batch, npair, sw).transpose(1, 2, 0, 3)
    y = pl.pallas_call(
        _s5_out_kernel,
        grid=(batch, npair),
        in_specs=[pl.BlockSpec((1, 1, nc, pw), lambda b, p: (b, p, 0, 0)), pl.BlockSpec((1, 1, nc, sw), lambda b, p: (b, p, 0, 0)),
                  pl.BlockSpec((1, pw, pw), lambda b, p: (p, 0, 0)), pl.BlockSpec((1, sw, pw), lambda b, p: (p, 0, 0))],
        out_specs=pl.BlockSpec((1, 1, nc, pw), lambda b, p: (b, p, 0, 0)),
        out_shape=jax.ShapeDtypeStruct((batch, npair, nc, pw), F32),
        compiler_params=_params("arbitrary", "arbitrary"),
        name="s5_out",
    )(up, xp, tz, wc)
    return y.reshape(batch, npair, nc, 2, ck, hg).transpose(0, 2, 4, 1, 3, 5).reshape(batch * seq, S5_WIDTH)


def _even_out_kernel(ypre_ref, u_ref, yb_ref, x_ref, d_ref, wglu_ref, wout_ref, o_ref):
    y = _gelu(ypre_ref[...] + d_ref[...] * u_ref[...])
    y = y * _sigmoid(_dot(y.astype(BF16), wglu_ref[...]))
    o_ref[...] = (x_ref[...] + _dot(y.astype(BF16), wout_ref[0:S5_WIDTH, :])
                  + _dot(yb_ref[...], wout_ref[S5_WIDTH:D_MODEL, :]))


def _even_out(ypre, u, yb, x2, d, w_glu, w_out, tm=512):
    t = x2.shape[0]
    row = lambda w: pl.BlockSpec((tm, w), lambda i: (i, 0))
    return pl.pallas_call(
        _even_out_kernel,
        grid=(t // tm,),
        in_specs=[row(S5_WIDTH), row(S5_WIDTH), row(CONV_WIDTH), row(D_MODEL), _full((1, S5_WIDTH)),
                  _full((S5_WIDTH, S5_WIDTH)), _full((D_MODEL, D_MODEL))],
        out_specs=row(D_MODEL),
        out_shape=jax.ShapeDtypeStruct((t, D_MODEL), F32),
        compiler_params=_params("arbitrary"),
        name="even_out",
    )(ypre, u, yb, x2, d, w_glu, w_out)


def _first_max(v, lane, width):
    m = jnp.max(v, axis=-1, keepdims=True)
    idx = jnp.min(jnp.where(v == m, lane, width), axis=-1, keepdims=True)
    return m, idx


def _moe_kernel(h_ref, g_ref, wr_hi_ref, wr_lo_ref, br_ref, wg_ref, wu_ref, wd_ref, o_ref, xn_ref, gate_ref):
    e = pl.program_id(1)

    @pl.when(e == 0)
    def _():
        h = h_ref[...]
        xn = _rms(h, g_ref[...])
        xn_ref[...] = xn.astype(BF16)
        o_ref[...] = h
        logits = _dot_x3(xn, wr_hi_ref[...], wr_lo_ref[...]) + br_ref[...]
        lane = lax.broadcasted_iota(jnp.int32, logits.shape, 1)
        width = logits.shape[1]
        is_g = lane < N_GROUPS
        gl = jnp.where(is_g, logits, -jnp.inf)
        gm, gi = _first_max(gl, lane, width)
        gw = 1.0 / jnp.sum(jnp.where(is_g, jnp.exp(gl - gm), 0.0), axis=-1, keepdims=True)
        lo = N_GROUPS + gi * EXPERTS_PER_GROUP
        in_grp = (lane >= lo) & (lane < lo + EXPERTS_PER_GROUP)
        el = jnp.where(in_grp, logits, -jnp.inf)
        m1, i1 = _first_max(el, lane, width)
        m2, i2 = _first_max(jnp.where(lane == i1, -jnp.inf, el), lane, width)
        p2 = jnp.exp(m2 - m1)
        w1 = gw / (1.0 + p2)
        w2 = gw * p2 / (1.0 + p2)
        gate_ref[...] = jnp.where(lane == i1, w1, 0.0) + jnp.where(lane == i2, w2, 0.0)

    xn = xn_ref[...]
    lane = lax.broadcasted_iota(jnp.int32, gate_ref.shape, 1)
    ge = jnp.sum(jnp.where(lane == e + N_GROUPS, gate_ref[...], 0.0), axis=-1, keepdims=True)
    h1 = _dot(xn, wg_ref[0])
    h3 = _dot(xn, wu_ref[0])
    act = (h1 * _sigmoid(h1)) * h3 * ge
    o_ref[...] += _dot(act.astype(BF16), wd_ref[0])


def _moe(h2, gain, w_group, b_group, w_expert, b_expert, w_gate, w_up, w_down, tm=512):
    t = h2.shape[0]
    rw = 128
    wr = jnp.zeros((D_MODEL, rw), F32).at[:, 0:N_GROUPS].set(w_group).at[:, N_GROUPS:N_GROUPS + N_EXPERTS].set(w_expert)
    br = jnp.zeros((1, rw), F32).at[0, 0:N_GROUPS].set(b_group).at[0, N_GROUPS:N_GROUPS + N_EXPERTS].set(b_expert)
    wr_hi = wr.astype(BF16)
    wr_lo = (wr - wr_hi.astype(F32)).astype(BF16)
    return pl.pallas_call(
        _moe_kernel,
        grid=(t // tm, N_EXPERTS),
        in_specs=[pl.BlockSpec((tm, D_MODEL), lambda i, e: (i, 0)), _full((1, D_MODEL)),
                  _full((D_MODEL, rw)), _full((D_MODEL, rw)), _full((1, rw)),
                  pl.BlockSpec((1, D_MODEL, EXPERT_FF), lambda i, e: (e, 0, 0)),
                  pl.BlockSpec((1, D_MODEL, EXPERT_FF), lambda i, e: (e, 0, 0)),
                  pl.BlockSpec((1, EXPERT_FF, D_MODEL), lambda i, e: (e, 0, 0))],
        out_specs=pl.BlockSpec((tm, D_MODEL), lambda i, e: (i, 0)),
        out_shape=jax.ShapeDtypeStruct((t, D_MODEL), F32),
        scratch_shapes=[pltpu.VMEM((tm, D_MODEL), BF16), pltpu.VMEM((tm, rw), F32)],
        compiler_params=_params("arbitrary", "arbitrary"),
        name="moe",
    )(h2, gain, wr_hi, wr_lo, br, w_gate.astype(BF16), w_up.astype(BF16), w_down.astype(BF16))


ODD_SPLITS = (MOBA_W, MOBA_W, MOBA_W, NSA_W, KV_W, KV_W, KV_W, KV_W, KV_W, KV_W, 128)
ODD_IN_PAD = sum(ODD_SPLITS)


def _head_rms(x, hsum, gain):
    w = x.shape[1]
    ss = jnp.concatenate([_dot_x2(x[:, o:o + hsum.shape[0]] * x[:, o:o + hsum.shape[0]], hsum)
                          for o in range(0, w, hsum.shape[0])], axis=1) if w > hsum.shape[0] else _dot_x2(x * x, hsum)
    return x * lax.rsqrt(ss * (1.0 / HEAD_DIM) + EPS) * gain


def _odd_in_kernel(x_ref, g_ref, w_ref, hsum_ref, gq_ref, gk_ref, gnq_ref, gks_ref, gkw_ref,
                   qm_ref, km_ref, kmean_ref, vm_ref, qd_ref, kc_ref, vc_ref, ks_ref, vs_ref, kw_ref, vw_ref, gt_ref):
    xn = _rms(x_ref[...], g_ref[...]).astype(BF16)
    offs = np.cumsum((0,) + ODD_SPLITS)
    col = lambda j: _dot(xn, w_ref[:, int(offs[j]):int(offs[j + 1])])
    hsum = hsum_ref[...]
    hsum128 = hsum_ref[0:128, 0:128]
    qm_ref[...] = _head_rms(col(0), hsum, gq_ref[...])
    km = _head_rms(col(1), hsum, gk_ref[...])
    km_ref[...] = km.astype(BF16)
    tm = km.shape[0]
    for j in range(tm // MOBA_BLOCK):
        kmean_ref[0, j:j + 1, :] = jnp.mean(km[j * MOBA_BLOCK:(j + 1) * MOBA_BLOCK, :], axis=0, keepdims=True)
    vm_ref[...] = col(2).astype(BF16)
    qd_ref[...] = (_head_rms(col(3), hsum, gnq_ref[...]) * (HEAD_DIM ** -0.5)).astype(BF16)
    kc_ref[...] = col(4).astype(BF16)
    vc_ref[...] = col(5).astype(BF16)
    ks_ref[...] = _head_rms(col(6), hsum128, gks_ref[...]).astype(BF16)
    vs_ref[...] = col(7).astype(BF16)
    kw_ref[...] = _head_rms(col(8), hsum128, gkw_ref[...]).astype(BF16)
    vw_ref[...] = col(9).astype(BF16)
    gt_ref[...] = _sigmoid(col(10))


def _odd_in(h2, gain, w_in, moba_q_norm, moba_k_norm, nsa_q_norm, nsa_ksel_norm, nsa_kwin_norm, tm=512):
    t = h2.shape[0]
    w = jnp.pad(w_in, ((0, 0), (0, ODD_IN_PAD - w_in.shape[1]))).astype(BF16)
    hsum = jnp.asarray(np.kron(np.eye(MOBA_W // HEAD_DIM), np.ones((HEAD_DIM, HEAD_DIM))), BF16)
    tile = lambda g, width: jnp.tile(g.astype(F32), width // HEAD_DIM).reshape(1, width)
    row = lambda width: pl.BlockSpec((tm, width), lambda i: (i, 0))
    nmb = tm // MOBA_BLOCK
    widths = (MOBA_W, MOBA_W, MOBA_W, NSA_W, KV_W, KV_W, KV_W, KV_W, KV_W, KV_W, 128)
    dtypes = (F32, BF16, BF16, BF16, BF16, BF16, BF16, BF16, BF16, BF16, F32)
    out_specs = [row(widths[0]), row(widths[1]), pl.BlockSpec((1, nmb, MOBA_W), lambda i: (i, 0, 0))] + [row(wd) for wd in widths[2:]]
    out_shape = ([jax.ShapeDtypeStruct((t, widths[0]), dtypes[0]), jax.ShapeDtypeStruct((t, widths[1]), dtypes[1]),
                  jax.ShapeDtypeStruct((t // tm, nmb, MOBA_W), F32)]
                 + [jax.ShapeDtypeStruct((t, wd), dt) for wd, dt in zip(widths[2:], dtypes[2:])])
    return pl.pallas_call(
        _odd_in_kernel,
        grid=(t // tm,),
        in_specs=[row(D_MODEL), _full((1, D_MODEL)), _full((D_MODEL, ODD_IN_PAD)), _full((MOBA_W, MOBA_W)),
                  _full((1, MOBA_W)), _full((1, MOBA_W)), _full((1, NSA_W)), _full((1, KV_W)), _full((1, KV_W))],
        out_specs=out_specs,
        out_shape=out_shape,
        compiler_params=_params("arbitrary"),
        name="odd_in",
    )(h2, gain, w, hsum, tile(moba_q_norm, MOBA_W), tile(moba_k_norm, MOBA_W), tile(nsa_q_norm, NSA_W),
      tile(nsa_ksel_norm, KV_W), tile(nsa_kwin_norm, KV_W))


def _compress_kernel(c_ref, w1_ref, w2_ref, pe_ref, g_ref, o_ref):
    kind = pl.program_id(0)
    c = c_ref[0, 0, 0]
    half = c.shape[1]
    n16 = c.shape[0]
    first = _dot(c, w1_ref[0, 0:half, :])
    second = _dot(c, w1_ref[0, half:2 * half, :])
    peb = _dot(pe_ref[0], w1_ref[0])[0:1, :]
    hid = _gelu(first + pltpu.roll(second, n16 - 1, 0) + peb)
    out = _dot(hid.astype(BF16), w2_ref[0])
    o_ref[0, 0, 0] = jnp.where(kind == 0, _rms(out, g_ref[...]), out).astype(BF16)


def _compress(kc, vc, batch, seq, pe_k, w1_k, w2_k, pe_v, w1_v, w2_v, kcmp_norm):
    n16 = seq // CMP_STRIDE
    half = CMP_STRIDE * HEAD_DIM

    def flat(x):
        return x.reshape(batch, n16, CMP_STRIDE, NSA_KV_HEADS, HEAD_DIM).transpose(0, 3, 1, 2, 4).reshape(
            batch, NSA_KV_HEADS, n16, half)

    c = jnp.stack([flat(kc), flat(vc)])
    w1 = jnp.stack([w1_k, w1_v]).astype(BF16)
    w2 = jnp.stack([w2_k, w2_v]).astype(BF16)
    pe = jnp.stack([pe_k, pe_v]).reshape(2, 1, 2 * half)
    pe = jnp.broadcast_to(pe, (2, 8, 2 * half)).astype(BF16)
    return pl.pallas_call(
        _compress_kernel,
        grid=(2, batch, NSA_KV_HEADS),
        in_specs=[pl.BlockSpec((1, 1, 1, n16, half), lambda k, b, h: (k, b, h, 0, 0)),
                  pl.BlockSpec((1, 2 * half, CMP_HIDDEN), lambda k, b, h: (k, 0, 0)),
                  pl.BlockSpec((1, CMP_HIDDEN, HEAD_DIM), lambda k, b, h: (k, 0, 0)),
                  pl.BlockSpec((1, 8, 2 * half), lambda k, b, h: (k, 0, 0)),
                  _full((1, HEAD_DIM))],
        out_specs=pl.BlockSpec((1, 1, 1, n16, HEAD_DIM), lambda k, b, h: (k, b, h, 0, 0)),
        out_shape=jax.ShapeDtypeStruct((2, batch, NSA_KV_HEADS, n16, HEAD_DIM), BF16),
        compiler_params=_params("arbitrary", "arbitrary", "arbitrary"),
        name="nsa_compress",
    )(c, w1, w2, pe, kcmp_norm.astype(F32).reshape(1, HEAD_DIM))


M_INIT = -1e30


def _softmax_init(m_ref, l_ref, acc_ref):
    m_ref[...] = jnp.full(m_ref.shape, M_INIT, F32)
    l_ref[...] = jnp.zeros(l_ref.shape, F32)
    acc_ref[...] = jnp.zeros(acc_ref.shape, F32)


def _softmax_step(s, v, m_ref, l_ref, acc_ref):
    m_old = m_ref[...]
    m_new = jnp.maximum(m_old, jnp.max(s, axis=-1, keepdims=True))
    alpha = jnp.exp(m_old - m_new)
    p = jnp.exp(s - m_new)
    l_ref[...] = alpha * l_ref[...] + jnp.sum(p, axis=-1, keepdims=True)
    acc_ref[...] = alpha * acc_ref[...] + _dot(p.astype(BF16), v)
    m_ref[...] = m_new


def _pick_top(score, lane, width, k):
    sel = jnp.zeros(score.shape, jnp.bool_)
    for _ in range(k):
        m, idx = _first_max(score, lane, width)
        hit = (lane == idx) & (m > -jnp.inf)
        sel = sel | hit
        score = jnp.where(lane == idx, -jnp.inf, score)
    return sel


MOBA_LANES = 64


def _moba_kernel(q_ref, ka_ref, v_ref, kmean_ref, o_ref, qa_ref, m_ref, l_ref, acc_ref):
    i = pl.program_id(2)
    q = q_ref[0, 0]
    q_hi, q_lo = _split(q)
    km_hi, km_lo = _split(kmean_ref[0, 0])
    gate = _dot_nt(q_hi, km_hi) + (_dot_nt(q_hi, km_lo) + _dot_nt(q_lo, km_hi))
    blk = lax.broadcasted_iota(jnp.int32, gate.shape, 1)
    sel = _pick_top(jnp.where(blk < i, gate, -jnp.inf), blk, MOBA_LANES, MOBA_TOPK)
    bias = jnp.where(sel | (blk == i), 0.0, NEG)
    qa_ref[...] = jnp.concatenate([q * (HEAD_DIM ** -0.5), bias], axis=1).astype(BF16)
    _softmax_init(m_ref, l_ref, acc_ref)

    def body(j, carry):
        start = pl.multiple_of(j * MOBA_BLOCK, MOBA_BLOCK)
        s = _dot_nt(qa_ref[...], ka_ref[0, 0, pl.ds(start, MOBA_BLOCK), :])
        _softmax_step(s, v_ref[0, 0, pl.ds(start, MOBA_BLOCK), :], m_ref, l_ref, acc_ref)
        return carry

    lax.fori_loop(0, i, body, 0)
    start = pl.multiple_of(i * MOBA_BLOCK, MOBA_BLOCK)
    s = _dot_nt(qa_ref[...], ka_ref[0, 0, pl.ds(start, MOBA_BLOCK), :])
    qpos = lax.broadcasted_iota(jnp.int32, s.shape, 0)
    kpos = lax.broadcasted_iota(jnp.int32, s.shape, 1)
    _softmax_step(jnp.where(kpos <= qpos, s, NEG), v_ref[0, 0, pl.ds(start, MOBA_BLOCK), :], m_ref, l_ref, acc_ref)
    o_ref[0, 0] = (acc_ref[...] / l_ref[...]).astype(BF16)


def _moba(qm, km, kmean, vm, batch, seq):
    nmb = seq // MOBA_BLOCK
    assert nmb <= MOBA_LANES
    heads = lambda x: x.reshape(batch, seq, MOBA_HEADS, HEAD_DIM).transpose(0, 2, 1, 3)
    onehot = jnp.asarray(np.eye(MOBA_LANES)[np.arange(seq) // MOBA_BLOCK], BF16)
    ka = jnp.concatenate([heads(km), jnp.broadcast_to(onehot, (batch, MOBA_HEADS, seq, MOBA_LANES))], axis=-1)
    kmean = kmean.reshape(batch, nmb, MOBA_HEADS, HEAD_DIM).transpose(0, 2, 1, 3)
    kmean = jnp.pad(kmean, ((0, 0), (0, 0), (0, MOBA_LANES - nmb), (0, 0)))
    o = pl.pallas_call(
        _moba_kernel,
        grid=(batch, MOBA_HEADS, nmb),
        in_specs=[pl.BlockSpec((1, 1, MOBA_BLOCK, HEAD_DIM), lambda b, h, i: (b, h, i, 0)),
                  pl.BlockSpec((1, 1, seq, HEAD_DIM + MOBA_LANES), lambda b, h, i: (b, h, 0, 0)),
                  pl.BlockSpec((1, 1, seq, HEAD_DIM), lambda b, h, i: (b, h, 0, 0)),
                  pl.BlockSpec((1, 1, MOBA_LANES, HEAD_DIM), lambda b, h, i: (b, h, 0, 0))],
        out_specs=pl.BlockSpec((1, 1, MOBA_BLOCK, HEAD_DIM), lambda b, h, i: (b, h, i, 0)),
        out_shape=jax.ShapeDtypeStruct((batch, MOBA_HEADS, seq, HEAD_DIM), BF16),
        scratch_shapes=[pltpu.VMEM((MOBA_BLOCK, HEAD_DIM + MOBA_LANES), BF16), pltpu.VMEM((MOBA_BLOCK, 1), F32),
                        pltpu.VMEM((MOBA_BLOCK, 1), F32), pltpu.VMEM((MOBA_BLOCK, HEAD_DIM), F32)],
        compiler_params=_params("arbitrary", "arbitrary", "arbitrary"),
        name="moba",
    )(heads(qm), ka, heads(vm), kmean)
    return o.transpose(0, 2, 1, 3).reshape(batch * seq, MOBA_W)


NSA_TQ = 128
NSA_TK = 512
SEL_LANES = 64
SUPER_KEYS = SEL_LANES * SEL_BLOCK


def _nsa_kernel(n_super, q_ref, kc_ref, vc_ref, ka_ref, vp_ref, kw_ref, gt_ref, e_ref, band_ref, o_ref,
                qa_ref, m_ref, l_ref, acc_ref):
    qi = pl.program_id(2)
    tq = NSA_TQ
    s0 = qi * tq
    q = q_ref[0, 0, 0]

    sc = _dot_nt(q, kc_ref[0, 0])
    rq = lax.broadcasted_iota(jnp.int32, sc.shape, 0) & (tq - 1)
    n = lax.broadcasted_iota(jnp.int32, sc.shape, 1)
    vis = n * CMP_STRIDE + (CMP_BLOCK - 1) <= s0 + rq
    sc = jnp.where(vis, sc, NEG)
    pc = jnp.where(vis, jnp.exp(sc - jnp.max(sc, axis=-1, keepdims=True)), 0.0)
    pc = pc / jnp.maximum(jnp.sum(pc, axis=-1, keepdims=True), 1e-30)
    o_c = _dot(pc.astype(BF16), vc_ref[0, 0])
    imp = pc[0:tq]
    for g in range(1, NSA_GROUP):
        imp = imp + pc[g * tq:(g + 1) * tq]
    pslc = _dot_x2(imp, band_ref[...])

    nb = pslc.shape[1]
    blk = lax.broadcasted_iota(jnp.int32, pslc.shape, 1)
    cur = (s0 + lax.broadcasted_iota(jnp.int32, pslc.shape, 0)) // SEL_BLOCK
    c0 = s0 // SEL_BLOCK
    elig = (blk >= 1) & (blk <= cur - 2)
    sel = _pick_top(jnp.where(elig, pslc, -jnp.inf), blk, nb, SEL_TOPK - 3)
    sel = sel | (((blk == 0) | (blk == cur - 1)) & (blk < c0))
    bias = jnp.where(sel, 0.0, NEG).astype(BF16)
    for st in range(n_super):
        b = bias[:, st * SEL_LANES:(st + 1) * SEL_LANES]
        qa_ref[st] = jnp.concatenate([q, jnp.concatenate([b] * NSA_GROUP, axis=0)], axis=1)

    _softmax_init(m_ref, l_ref, acc_ref)

    def body(kt, carry):
        start = pl.multiple_of(kt * NSA_TK, NSA_TK)
        s = _dot_nt(qa_ref[kt // (SUPER_KEYS // NSA_TK)], ka_ref[0, 0, pl.ds(start, NSA_TK), :])
        _softmax_step(s, vp_ref[0, 0, pl.ds(start, NSA_TK), :], m_ref, l_ref, acc_ref)
        return carry

    lax.fori_loop(0, (s0 + NSA_TK - 1) // NSA_TK, body, 0)
    d0 = pl.multiple_of(s0, tq)
    s = _dot_nt(q, ka_ref[0, 0, pl.ds(d0, tq), :][:, 0:HEAD_DIM])
    rqd = lax.broadcasted_iota(jnp.int32, s.shape, 0) & (tq - 1)
    col = lax.broadcasted_iota(jnp.int32, s.shape, 1)
    _softmax_step(jnp.where(col <= rqd, s, NEG), vp_ref[0, 0, pl.ds(d0, tq), :], m_ref, l_ref, acc_ref)
    o_s = acc_ref[:, 0:HEAD_DIM] / l_ref[...]

    wlen = WINDOW + tq
    w0 = pl.multiple_of(jnp.maximum(s0 - WINDOW, 0), tq)
    sw = _dot_nt(q, kw_ref[0, 0, pl.ds(w0, wlen), :])
    kabs = w0 + lax.broadcasted_iota(jnp.int32, sw.shape, 1)
    t = s0 + (lax.broadcasted_iota(jnp.int32, sw.shape, 0) & (tq - 1))
    sw = jnp.where((kabs <= t) & (kabs > t - WINDOW), sw, NEG)
    pw = jnp.exp(sw - jnp.max(sw, axis=-1, keepdims=True))
    o_w = _dot(pw.astype(BF16), vp_ref[0, 0, pl.ds(w0, wlen), :])[:, HEAD_DIM:2 * HEAD_DIM] / jnp.sum(pw, axis=-1, keepdims=True)

    w = NSA_GROUP * HEAD_DIM
    gexp = _dot_x2(gt_ref[...], e_ref[0])
    wide = lambda x: jnp.concatenate([x[g * tq:(g + 1) * tq] for g in range(NSA_GROUP)], axis=1)
    o_ref[...] = (gexp[:, 0:w] * wide(o_c) + gexp[:, w:2 * w] * wide(o_s) + gexp[:, 2 * w:3 * w] * wide(o_w)).astype(BF16)


def _nsa(qd, kcmp, vcmp, ks, vs, kw, vw, gates, batch, seq):
    tq = NSA_TQ
    nq = seq // tq
    nb = seq // SEL_BLOCK
    n16 = seq // CMP_STRIDE
    assert seq % SUPER_KEYS == 0
    n_super = seq // SUPER_KEYS
    rows = NSA_GROUP * tq
    heads = lambda x: x.reshape(batch, seq, NSA_KV_HEADS, HEAD_DIM).transpose(0, 2, 1, 3)
    qs = qd.reshape(batch, nq, tq, NSA_KV_HEADS, NSA_GROUP, HEAD_DIM).transpose(0, 3, 1, 4, 2, 5).reshape(
        batch, NSA_KV_HEADS, nq, rows, HEAD_DIM)
    onehot = jnp.asarray(np.eye(SEL_LANES)[(np.arange(seq) // SEL_BLOCK) % SEL_LANES], BF16)
    ka = jnp.concatenate([heads(ks), jnp.broadcast_to(onehot, (batch, NSA_KV_HEADS, seq, SEL_LANES))], axis=-1)
    vp = jnp.concatenate([heads(vs), heads(vw)], axis=-1)
    e = np.zeros((NSA_KV_HEADS, 128, 3 * NSA_GROUP * HEAD_DIM), np.float32)
    for br in range(3):
        for hk in range(NSA_KV_HEADS):
            for g in range(NSA_GROUP):
                c = (br * NSA_GROUP + g) * HEAD_DIM
                e[hk, br * NSA_HEADS + hk * NSA_GROUP + g, c:c + HEAD_DIM] = 1.0
    nn, jj = np.arange(n16)[:, None], np.arange(nb)[None, :]
    band = ((nn >= 4 * jj - 1) & (nn <= 4 * jj + 3)).astype(np.float32)
    resident = lambda width: pl.BlockSpec((1, 1, seq, width), lambda b, h, i: (b, h, 0, 0))
    w = NSA_GROUP * HEAD_DIM
    return pl.pallas_call(
        functools.partial(_nsa_kernel, n_super),
        grid=(batch, NSA_KV_HEADS, nq),
        in_specs=[pl.BlockSpec((1, 1, 1, rows, HEAD_DIM), lambda b, h, i: (b, h, i, 0, 0)),
                  pl.BlockSpec((1, 1, n16, HEAD_DIM), lambda b, h, i: (b, h, 0, 0)),
                  pl.BlockSpec((1, 1, n16, HEAD_DIM), lambda b, h, i: (b, h, 0, 0)),
                  resident(HEAD_DIM + SEL_LANES), resident(2 * HEAD_DIM), resident(HEAD_DIM),
                  pl.BlockSpec((tq, 128), lambda b, h, i: (b * nq + i, 0)),
                  pl.BlockSpec((1, 128, 3 * w), lambda b, h, i: (h, 0, 0)),
                  pl.BlockSpec((n16, nb), lambda b, h, i: (0, 0))],
        out_specs=pl.BlockSpec((tq, w), lambda b, h, i: (b * nq + i, h)),
        out_shape=jax.ShapeDtypeStruct((batch * seq, NSA_W), BF16),
        scratch_shapes=[pltpu.VMEM((n_super, rows, HEAD_DIM + SEL_LANES), BF16), pltpu.VMEM((rows, 1), F32),
                        pltpu.VMEM((rows, 1), F32), pltpu.VMEM((rows, 2 * HEAD_DIM), F32)],
        compiler_params=_params("arbitrary", "arbitrary", "arbitrary"),
        name="nsa",
    )(qs, kcmp, vcmp, ka, vp, heads(kw), gates, jnp.asarray(e, BF16), jnp.asarray(band, BF16))


def _odd_out_kernel(om_ref, on_ref, h_ref, w_ref, o_ref):
    o_ref[...] = h_ref[...] + _dot(om_ref[...], w_ref[0:MOBA_W, :]) + _dot(on_ref[...], w_ref[MOBA_W:D_MODEL, :])


def _odd_out(o_moba, o_nsa, h2, w_out, tm=512):
    t = h2.shape[0]
    row = lambda w: pl.BlockSpec((tm, w), lambda i: (i, 0))
    return pl.pallas_call(
        _odd_out_kernel,
        grid=(t // tm,),
        in_specs=[row(MOBA_W), row(NSA_W), row(D_MODEL), _full((D_MODEL, D_MODEL))],
        out_specs=row(D_MODEL),
        out_shape=jax.ShapeDtypeStruct((t, D_MODEL), F32),
        compiler_params=_params("arbitrary"),
        name="odd_out",
    )(o_moba, o_nsa, h2, w_out)


def _even_layer(h2, batch, seq, norm, w_in, w_out, lam_re, lam_im, log_dt, b_re, b_im, c_re, c_im, d, w_glu, conv_w, conv_b):
    u, yb = _even_in(h2, norm.reshape(1, D_MODEL), w_in.astype(BF16), conv_w, conv_b.reshape(1, CONV_WIDTH), seq)
    ypre = _s5_mixer_pre(u, batch, seq, lam_re, lam_im, log_dt, b_re, b_im, c_re, c_im)
    return _even_out(ypre, u, yb, h2, d.reshape(1, S5_WIDTH), w_glu.astype(BF16), w_out.astype(BF16))


def _odd_layer(h2, batch, seq, norm, w_in, w_out, moba_q_norm, moba_k_norm, nsa_q_norm, nsa_kcmp_norm, nsa_ksel_norm,
               nsa_kwin_norm, cmp_pe_k, cmp_w1_k, cmp_w2_k, cmp_pe_v, cmp_w1_v, cmp_w2_v):
    qm, km, kmean, vm, qd, kc, vc, ks, vs, kw, vw, gates = _odd_in(
        h2, norm.reshape(1, D_MODEL), w_in, moba_q_norm, moba_k_norm, nsa_q_norm, nsa_ksel_norm, nsa_kwin_norm)
    cmp = _compress(kc, vc, batch, seq, cmp_pe_k, cmp_w1_k, cmp_w2_k, cmp_pe_v, cmp_w1_v, cmp_w2_v, nsa_kcmp_norm)
    o_moba = _moba(qm, km, kmean, vm, batch, seq)
    o_nsa = _nsa(qd, cmp[0], cmp[1], ks, vs, kw, vw, gates, batch, seq)
    return _odd_out(o_moba, o_nsa, h2, w_out.astype(BF16))


def kernel(x, ev_norm_mix, ev_w_in, ev_w_out, s5_lam_re, s5_lam_im, s5_log_dt, s5_b_re, s5_b_im, s5_c_re, s5_c_im, s5_d, s5_w_glu, conv_w, conv_b, od_norm_mix, od_w_in, od_w_out, moba_q_norm, moba_k_norm, nsa_q_norm, nsa_kcmp_norm, nsa_ksel_norm, nsa_kwin_norm, cmp_pe_k, cmp_w1_k, cmp_w2_k, cmp_pe_v, cmp_w1_v, cmp_w2_v, moe_norm, moe_w_group, moe_b_group, moe_w_expert, moe_b_expert, moe_w_gate, moe_w_up, moe_w_down):
    batch, seq, _ = x.shape
    depth = moe_norm.shape[0]
    h = x.reshape(batch * seq, D_MODEL)
    for layer in range(depth):
        i = layer // 2
        if layer % 2 == 0:
            h = _even_layer(h, batch, seq, ev_norm_mix[i], ev_w_in[i], ev_w_out[i], s5_lam_re[i], s5_lam_im[i], s5_log_dt[i],
                            s5_b_re[i], s5_b_im[i], s5_c_re[i], s5_c_im[i], s5_d[i], s5_w_glu[i], conv_w[i], conv_b[i])
        else:
            h = _odd_layer(h, batch, seq, od_norm_mix[i], od_w_in[i], od_w_out[i], moba_q_norm[i], moba_k_norm[i],
                           nsa_q_norm[i], nsa_kcmp_norm[i], nsa_ksel_norm[i], nsa_kwin_norm[i], cmp_pe_k[i], cmp_w1_k[i],
                           cmp_w2_k[i], cmp_pe_v[i], cmp_w1_v[i], cmp_w2_v[i])
        h = _moe(h, moe_norm[layer].reshape(1, D_MODEL), moe_w_group[layer], moe_b_group[layer], moe_w_expert[layer],
                 moe_b_expert[layer], moe_w_gate[layer], moe_w_up[layer], moe_w_down[layer])
    return h.reshape(batch, seq, D_MODEL)
```

```python
import functools
import math

import jax
import jax.numpy as jnp
import numpy as np
from jax import lax
from jax.experimental import pallas as pl
from jax.experimental.pallas import tpu as pltpu

D_MODEL = 1024
HEAD_DIM = 64
EPS = 1e-6
S5_WIDTH = 256
S5_GROUP = 16
S5_GROUPS = 16
S5_STATE = 64
S5_CHUNK = 16
CONV_WIDTH = 768
CONV_K = 3
MOBA_HEADS = 4
NSA_HEADS = 12
NSA_KV_HEADS = 2
NSA_GROUP = 6
MOBA_W = 256
NSA_W = 768
KV_W = 128
MOBA_BLOCK = 256
MOBA_TOPK = 3
CMP_BLOCK = 32
CMP_STRIDE = 16
CMP_HIDDEN = 256
SEL_BLOCK = 64
SEL_TOPK = 8
WINDOW = 512
N_GROUPS = 4
EXPERTS_PER_GROUP = 4
N_EXPERTS = 16
EXPERT_FF = 256

VMEM_LIMIT_BYTES = 56 * 1024 * 1024
NEG = -float(2 ** 30)
F32 = jnp.float32
BF16 = jnp.bfloat16


def _params(*semantics):
    return pltpu.CompilerParams(dimension_semantics=semantics, vmem_limit_bytes=VMEM_LIMIT_BYTES)


def _dot(a, b):
    return jnp.dot(a, b, preferred_element_type=F32)


def _dot_nt(a, b):
    return lax.dot_general(a, b, (((1,), (1,)), ((), ())), preferred_element_type=F32)


def _split(x):
    hi = x.astype(BF16)
    lo = (x - hi.astype(F32)).astype(BF16)
    return hi, lo


def _dot_x2(x, w):
    hi, lo = _split(x)
    return _dot(hi, w) + _dot(lo, w)


def _dot_x3(x, w_hi, w_lo):
    hi, lo = _split(x)
    return _dot(hi, w_hi) + (_dot(hi, w_lo) + _dot(lo, w_hi))


def _rms(x, gain):
    return x * lax.rsqrt(jnp.mean(x * x, axis=-1, keepdims=True) + EPS) * gain


def _gelu(x):
    return 0.5 * x * (1.0 + jnp.tanh(math.sqrt(2.0 / math.pi) * (x + 0.044715 * (x * x * x))))


def _sigmoid(x):
    return 1.0 / (1.0 + jnp.exp(-x))


def _full(shape):
    n = len(shape)
    return pl.BlockSpec(shape, lambda *_: (0,) * n)


def _even_in_kernel(tiles_per_seq, x_ref, g_ref, w_ref, cw_ref, cb_ref, u_ref, yb_ref, carry_ref):
    i = pl.program_id(0)
    xn = _rms(x_ref[...], g_ref[...]).astype(BF16)
    u_ref[...] = _dot(xn, w_ref[:, 0:S5_WIDTH])
    o = S5_WIDTH
    xc = _dot(xn, w_ref[:, o:o + CONV_WIDTH])
    gb = _dot(xn, w_ref[:, o + CONV_WIDTH:o + 2 * CONV_WIDTH])
    gc = _dot(xn, w_ref[:, o + 2 * CONV_WIDTH:o + 3 * CONV_WIDTH])
    z = gc * xc
    tm = z.shape[0]

    @pl.when(i % tiles_per_seq == 0)
    def _():
        carry_ref[...] = jnp.zeros_like(carry_ref)

    row = lax.broadcasted_iota(jnp.int32, z.shape, 0)
    prev1 = carry_ref[7:8, :]
    prev2 = carry_ref[6:7, :]
    z1 = jnp.where(row == 0, prev1, pltpu.roll(z, 1, 0))
    z2 = jnp.where(row == 0, prev2, jnp.where(row == 1, prev1, pltpu.roll(z, 2, 0)))
    y = cw_ref[0:1, :] * z2 + cw_ref[1:2, :] * z1 + cw_ref[2:3, :] * z + cb_ref[...]
    yb_ref[...] = (gb * y).astype(BF16)
    carry_ref[...] = z[tm - 8:tm, :]


def _even_in(x2, gain, w_in, conv_w, conv_b, seq, tm=512):
    t = x2.shape[0]
    n_in = w_in.shape[1]
    return pl.pallas_call(
        functools.partial(_even_in_kernel, seq // tm),
        grid=(t // tm,),
        in_specs=[pl.BlockSpec((tm, D_MODEL), lambda i: (i, 0)), _full((1, D_MODEL)),
                  _full((D_MODEL, n_in)), _full((CONV_K, CONV_WIDTH)), _full((1, CONV_WIDTH))],
        out_specs=[pl.BlockSpec((tm, S5_WIDTH), lambda i: (i, 0)),
                   pl.BlockSpec((tm, CONV_WIDTH), lambda i: (i, 0))],
        out_shape=[jax.ShapeDtypeStruct((t, S5_WIDTH), F32), jax.ShapeDtypeStruct((t, CONV_WIDTH), BF16)],
        scratch_shapes=[pltpu.VMEM((8, CONV_WIDTH), F32)],
        compiler_params=_params("arbitrary"),
        name="even_in",
    )(x2, gain, w_in, conv_w, conv_b)


def _s5_weights(lam_re, lam_im, log_dt, b_re, b_im, c_re, c_im):
    g, p, hg, ck = S5_GROUPS, S5_STATE, S5_GROUP, S5_CHUNK
    lr, li = lam_re.astype(F32), lam_im.astype(F32)
    dt = jnp.exp(log_dt.astype(F32))[:, None]
    mag = jnp.exp(lr * dt)
    a_re, a_im = mag * jnp.cos(li * dt), mag * jnp.sin(li * dt)
    den = lr * lr + li * li
    f_re = ((a_re - 1.0) * lr + a_im * li) / den
    f_im = (a_im * lr - (a_re - 1.0) * li) / den
    br, bi = b_re.astype(F32), b_im.astype(F32)
    bb_re = f_re[..., None] * br - f_im[..., None] * bi
    bb_im = f_re[..., None] * bi + f_im[..., None] * br
    pw_re, pw_im = [jnp.ones_like(a_re)], [jnp.zeros_like(a_im)]
    for _ in range(ck):
        r, m = pw_re[-1], pw_im[-1]
        pw_re.append(r * a_re - m * a_im)
        pw_im.append(r * a_im + m * a_re)
    pw_re, pw_im = jnp.stack(pw_re), jnp.stack(pw_im)
    cr, ci = c_re.astype(F32), c_im.astype(F32)
    rev_re, rev_im = pw_re[ck - 1::-1][:ck], pw_im[ck - 1::-1][:ck]
    ws_re = rev_re[:, :, :, None] * bb_re[None] - rev_im[:, :, :, None] * bb_im[None]
    ws_im = rev_re[:, :, :, None] * bb_im[None] + rev_im[:, :, :, None] * bb_re[None]
    ws_re = ws_re.transpose(1, 0, 3, 2).reshape(g, ck * hg, p)
    ws_im = ws_im.transpose(1, 0, 3, 2).reshape(g, ck * hg, p)
    ca_re = cr[None] * pw_re[1:, :, None, :] - ci[None] * pw_im[1:, :, None, :]
    ca_im = cr[None] * pw_im[1:, :, None, :] + ci[None] * pw_re[1:, :, None, :]
    wc_re = ca_re.transpose(1, 3, 0, 2).reshape(g, p, ck * hg)
    wc_im = (-ca_im).transpose(1, 3, 0, 2).reshape(g, p, ck * hg)
    cb_re = jnp.einsum('ghp,kgp,gpj->kghj', cr, pw_re[:ck], bb_re) - jnp.einsum('ghp,kgp,gpj->kghj', cr, pw_im[:ck], bb_im) \
        - jnp.einsum('ghp,kgp,gpj->kghj', ci, pw_re[:ck], bb_im) - jnp.einsum('ghp,kgp,gpj->kghj', ci, pw_im[:ck], bb_re)
    lag = np.arange(ck)[None, :] - np.arange(ck)[:, None]
    tz = cb_re[np.clip(lag, 0, ck - 1)]
    tz = jnp.where((lag >= 0)[:, :, None, None, None], tz, 0.0)
    tz = tz.transpose(2, 0, 4, 1, 3).reshape(g, ck * hg, ck * hg)

    def pair_blockdiag(m):
        m = m.reshape(g // 2, 2, m.shape[1], m.shape[2])
        z = jnp.zeros_like(m[:, 0])
        return jnp.concatenate([jnp.concatenate([m[:, 0], z], axis=2), jnp.concatenate([z, m[:, 1]], axis=2)], axis=1)

    ws = jnp.concatenate([pair_blockdiag(ws_re), pair_blockdiag(ws_im)], axis=2)
    wc = jnp.concatenate([pair_blockdiag(wc_re), pair_blockdiag(wc_im)], axis=1)
    tzp = pair_blockdiag(tz)
    a16_re = pw_re[ck].reshape(g // 2, 1, 2 * p)
    a16_im = pw_im[ck].reshape(g // 2, 1, 2 * p)
    return ws.astype(BF16), wc.astype(BF16), tzp.astype(BF16), a16_re, a16_im


def _s5_state_kernel(u_ref, ws_ref, s_ref):
    s_ref[0, 0] = _dot(u_ref[0, 0], ws_ref[0])


def _s5_scan_kernel(s_ref, are_ref, aim_ref, xprev_ref, st_ref):
    @pl.when(pl.program_id(0) == 0)
    def _():
        st_ref[...] = jnp.zeros_like(st_ref)

    a_re, a_im = are_ref[...], aim_ref[...]
    n = s_ref.shape[0]
    half = a_re.shape[1]

    def body(c, carry):
        xr, xi = carry
        xprev_ref[c, :, 0:half] = xr
        xprev_ref[c, :, half:2 * half] = xi
        s = s_ref[c]
        nr = a_re * xr - a_im * xi + s[:, 0:half]
        ni = a_re * xi + a_im * xr + s[:, half:2 * half]
        return nr, ni

    xr, xi = lax.fori_loop(0, n, body, (st_ref[:, 0:half], st_ref[:, half:2 * half]), unroll=8)
    st_ref[:, 0:half] = xr
    st_ref[:, half:2 * half] = xi


def _s5_out_kernel(u_ref, xp_ref, tz_ref, wc_ref, y_ref):
    y_ref[0, 0] = _dot(u_ref[0, 0], tz_ref[0]) + _dot(xp_ref[0, 0].astype(BF16), wc_ref[0])


def _s5_mixer_pre(u, batch, seq, lam_re, lam_im, log_dt, b_re, b_im, c_re, c_im):
    ws, wc, tz, a16_re, a16_im = _s5_weights(lam_re, lam_im, log_dt, b_re, b_im, c_re, c_im)
    ck, hg = S5_CHUNK, S5_GROUP
    nc = seq // ck
    npair = S5_GROUPS // 2
    pw = 2 * ck * hg
    sw = 4 * S5_STATE
    up = u.astype(BF16).reshape(batch, nc, ck, npair, 2, hg).transpose(0, 3, 1, 4, 2, 5).reshape(batch, npair, nc, pw)
    s = pl.pallas_call(
        _s5_state_kernel,
        grid=(batch, npair),
        in_specs=[pl.BlockSpec((1, 1, nc, pw), lambda b, p: (b, p, 0, 0)), pl.BlockSpec((1, pw, sw), lambda b, p: (p, 0, 0))],
        out_specs=pl.BlockSpec((1, 1, nc, sw), lambda b, p: (b, p, 0, 0)),
        out_shape=jax.ShapeDtypeStruct((batch, npair, nc, sw), F32),
        compiler_params=_params("arbitrary", "arbitrary"),
        name="s5_state",
    )(up, ws)
    rows = batch * npair
    s_t = s.transpose(2, 0, 1, 3).reshape(nc, rows, sw)
    are = jnp.broadcast_to(a16_re.reshape(1, npair, 2 * S5_STATE), (batch, npair, 2 * S5_STATE)).reshape(rows, 2 * S5_STATE)
    aim = jnp.broadcast_to(a16_im.reshape(1, npair, 2 * S5_STATE), (batch, npair, 2 * S5_STATE)).reshape(rows, 2 * S5_STATE)
    tc = min(nc, 256)
    xprev = pl.pallas_call(
        _s5_scan_kernel,
        grid=(nc // tc,),
        in_specs=[pl.BlockSpec((tc, rows, sw), lambda i: (i, 0, 0)), _full((rows, 2 * S5_STATE)), _full((rows, 2 * S5_STATE))],
        out_specs=pl.BlockSpec((tc, rows, sw), lambda i: (i, 0, 0)),
        out_shape=jax.ShapeDtypeStruct((nc, rows, sw), F32),
        scratch_shapes=[pltpu.VMEM((rows, sw), F32)],
        compiler_params=_params("arbitrary"),
        name="s5_scan",
    )(s_t, are, aim)
    xp = xprev.reshape(nc, batch, npair, sw).transpose(1, 2, 0, 3)
    y = pl.pallas_call(
        _s5_out_kernel,
        grid=(batch, npair),
        in_specs=[pl.BlockSpec((1, 1, nc, pw), lambda b, p: (b, p, 0, 0)), pl.BlockSpec((1, 1, nc, sw), lambda b, p: (b, p, 0, 0)),
                  pl.BlockSpec((1, pw, pw), lambda b, p: (p, 0, 0)), pl.BlockSpec((1, sw, pw), lambda b, p: (p, 0, 0))],
        out_specs=pl.BlockSpec((1, 1, nc, pw), lambda b, p: (b, p, 0, 0)),
        out_shape=jax.ShapeDtypeStruct((batch, npair, nc, pw), F32),
        compiler_params=_params("arbitrary", "arbitrary"),
        name="s5_out",
    )(up, xp, tz, wc)
    return y.reshape(batch, npair, nc, 2, ck, hg).transpose(0, 2, 4, 1, 3, 5).reshape(batch * seq, S5_WIDTH)


def _even_out_kernel(ypre_ref, u_ref, yb_ref, x_ref, d_ref, wglu_ref, wout_ref, o_ref):
    y = _gelu(ypre_ref[...] + d_ref[...] * u_ref[...])
    y = y * _sigmoid(_dot(y.astype(BF16), wglu_ref[...]))
    o_ref[...] = (x_ref[...] + _dot(y.astype(BF16), wout_ref[0:S5_WIDTH, :])
                  + _dot(yb_ref[...], wout_ref[S5_WIDTH:D_MODEL, :]))


def _even_out(ypre, u, yb, x2, d, w_glu, w_out, tm=512):
    t = x2.shape[0]
    row = lambda w: pl.BlockSpec((tm, w), lambda i: (i, 0))
    return pl.pallas_call(
        _even_out_kernel,
        grid=(t // tm,),
        in_specs=[row(S5_WIDTH), row(S5_WIDTH), row(CONV_WIDTH), row(D_MODEL), _full((1, S5_WIDTH)),
                  _full((S5_WIDTH, S5_WIDTH)), _full((D_MODEL, D_MODEL))],
        out_specs=row(D_MODEL),
        out_shape=jax.ShapeDtypeStruct((t, D_MODEL), F32),
        compiler_params=_params("arbitrary"),
        name="even_out",
    )(ypre, u, yb, x2, d, w_glu, w_out)


def _first_max(v, lane, width):
    m = jnp.max(v, axis=-1, keepdims=True)
    idx = jnp.min(jnp.where(v == m, lane, width), axis=-1, keepdims=True)
    return m, idx


def _moe_kernel(h_ref, g_ref, wr_hi_ref, wr_lo_ref, br_ref, wg_ref, wu_ref, wd_ref, o_ref, xn_ref, gate_ref):
    e = pl.program_id(1)

    @pl.when(e == 0)
    def _():
        h = h_ref[...]
        xn = _rms(h, g_ref[...])
        xn_ref[...] = xn.astype(BF16)
        o_ref[...] = h
        logits = _dot_x3(xn, wr_hi_ref[...], wr_lo_ref[...]) + br_ref[...]
        lane = lax.broadcasted_iota(jnp.int32, logits.shape, 1)
        width = logits.shape[1]
        is_g = lane < N_GROUPS
        gl = jnp.where(is_g, logits, -jnp.inf)
        gm, gi = _first_max(gl, lane, width)
        gw = 1.0 / jnp.sum(jnp.where(is_g, jnp.exp(gl - gm), 0.0), axis=-1, keepdims=True)
        lo = N_GROUPS + gi * EXPERTS_PER_GROUP
        in_grp = (lane >= lo) & (lane < lo + EXPERTS_PER_GROUP)
        el = jnp.where(in_grp, logits, -jnp.inf)
        m1, i1 = _first_max(el, lane, width)
        m2, i2 = _first_max(jnp.where(lane == i1, -jnp.inf, el), lane, width)
        p2 = jnp.exp(m2 - m1)
        w1 = gw / (1.0 + p2)
        w2 = gw * p2 / (1.0 + p2)
        gate_ref[...] = jnp.where(lane == i1, w1, 0.0) + jnp.where(lane == i2, w2, 0.0)

    xn = xn_ref[...]
    lane = lax.broadcasted_iota(jnp.int32, gate_ref.shape, 1)
    ge = jnp.sum(jnp.where(lane == e + N_GROUPS, gate_ref[...], 0.0), axis=-1, keepdims=True)
    h1 = _dot(xn, wg_ref[0])
    h3 = _dot(xn, wu_ref[0])
    act = (h1 * _sigmoid(h1)) * h3 * ge
    o_ref[...] += _dot(act.astype(BF16), wd_ref[0])


def _moe(h2, gain, w_group, b_group, w_expert, b_expert, w_gate, w_up, w_down, tm=512):
    t = h2.shape[0]
    rw = 128
    wr = jnp.zeros((D_MODEL, rw), F32).at[:, 0:N_GROUPS].set(w_group).at[:, N_GROUPS:N_GROUPS + N_EXPERTS].set(w_expert)
    br = jnp.zeros((1, rw), F32).at[0, 0:N_GROUPS].set(b_group).at[0, N_GROUPS:N_GROUPS + N_EXPERTS].set(b_expert)
    wr_hi = wr.astype(BF16)
    wr_lo = (wr - wr_hi.astype(F32)).astype(BF16)
    return pl.pallas_call(
        _moe_kernel,
        grid=(t // tm, N_EXPERTS),
        in_specs=[pl.BlockSpec((tm, D_MODEL), lambda i, e: (i, 0)), _full((1, D_MODEL)),
                  _full((D_MODEL, rw)), _full((D_MODEL, rw)), _full((1, rw)),
                  pl.BlockSpec((1, D_MODEL, EXPERT_FF), lambda i, e: (e, 0, 0)),
                  pl.BlockSpec((1, D_MODEL, EXPERT_FF), lambda i, e: (e, 0, 0)),
                  pl.BlockSpec((1, EXPERT_FF, D_MODEL), lambda i, e: (e, 0, 0))],
        out_specs=pl.BlockSpec((tm, D_MODEL), lambda i, e: (i, 0)),
        out_shape=jax.ShapeDtypeStruct((t, D_MODEL), F32),
        scratch_shapes=[pltpu.VMEM((tm, D_MODEL), BF16), pltpu.VMEM((tm, rw), F32)],
        compiler_params=_params("arbitrary", "arbitrary"),
        name="moe",
    )(h2, gain, wr_hi, wr_lo, br, w_gate.astype(BF16), w_up.astype(BF16), w_down.astype(BF16))


ODD_SPLITS = (MOBA_W, MOBA_W, MOBA_W, NSA_W, KV_W, KV_W, KV_W, KV_W, KV_W, KV_W, 128)
ODD_IN_PAD = sum(ODD_SPLITS)


def _head_rms(x, hsum, gain):
    w = x.shape[1]
    ss = jnp.concatenate([_dot_x2(x[:, o:o + hsum.shape[0]] * x[:, o:o + hsum.shape[0]], hsum)
                          for o in range(0, w, hsum.shape[0])], axis=1) if w > hsum.shape[0] else _dot_x2(x * x, hsum)
    return x * lax.rsqrt(ss * (1.0 / HEAD_DIM) + EPS) * gain


def _odd_in_kernel(x_ref, g_ref, w_ref, hsum_ref, gq_ref, gk_ref, gnq_ref, gks_ref, gkw_ref,
                   qm_ref, km_ref, kmean_ref, vm_ref, qd_ref, kc_ref, vc_ref, ks_ref, vs_ref, kw_ref, vw_ref, gt_ref):
    xn = _rms(x_ref[...], g_ref[...]).astype(BF16)
    offs = np.cumsum((0,) + ODD_SPLITS)
    col = lambda j: _dot(xn, w_ref[:, int(offs[j]):int(offs[j + 1])])
    hsum = hsum_ref[...]
    hsum128 = hsum_ref[0:128, 0:128]
    qm_ref[...] = _head_rms(col(0), hsum, gq_ref[...])
    km = _head_rms(col(1), hsum, gk_ref[...])
    km_ref[...] = km.astype(BF16)
    tm = km.shape[0]
    for j in range(tm // MOBA_BLOCK):
        kmean_ref[0, j:j + 1, :] = jnp.mean(km[j * MOBA_BLOCK:(j + 1) * MOBA_BLOCK, :], axis=0, keepdims=True)
    vm_ref[...] = col(2).astype(BF16)
    qd_ref[...] = (_head_rms(col(3), hsum, gnq_ref[...]) * (HEAD_DIM ** -0.5)).astype(BF16)
    kc_ref[...] = col(4).astype(BF16)
    vc_ref[...] = col(5).astype(BF16)
    ks_ref[...] = _head_rms(col(6), hsum128, gks_ref[...]).astype(BF16)
    vs_ref[...] = col(7).astype(BF16)
    kw_ref[...] = _head_rms(col(8), hsum128, gkw_ref[...]).astype(BF16)
    vw_ref[...] = col(9).astype(BF16)
    gt_ref[...] = _sigmoid(col(10))


def _odd_in(h2, gain, w_in, moba_q_norm, moba_k_norm, nsa_q_norm, nsa_ksel_norm, nsa_kwin_norm, tm=512):
    t = h2.shape[0]
    w = jnp.pad(w_in, ((0, 0), (0, ODD_IN_PAD - w_in.shape[1]))).astype(BF16)
    hsum = jnp.asarray(np.kron(np.eye(MOBA_W // HEAD_DIM), np.ones((HEAD_DIM, HEAD_DIM))), BF16)
    tile = lambda g, width: jnp.tile(g.astype(F32), width // HEAD_DIM).reshape(1, width)
    row = lambda width: pl.BlockSpec((tm, width), lambda i: (i, 0))
    nmb = tm // MOBA_BLOCK
    widths = (MOBA_W, MOBA_W, MOBA_W, NSA_W, KV_W, KV_W, KV_W, KV_W, KV_W, KV_W, 128)
    dtypes = (F32, BF16, BF16, BF16, BF16, BF16, BF16, BF16, BF16, BF16, F32)
    out_specs = [row(widths[0]), row(widths[1]), pl.BlockSpec((1, nmb, MOBA_W), lambda i: (i, 0, 0))] + [row(wd) for wd in widths[2:]]
    out_shape = ([jax.ShapeDtypeStruct((t, widths[0]), dtypes[0]), jax.ShapeDtypeStruct((t, widths[1]), dtypes[1]),
                  jax.ShapeDtypeStruct((t // tm, nmb, MOBA_W), F32)]
                 + [jax.ShapeDtypeStruct((t, wd), dt) for wd, dt in zip(widths[2:], dtypes[2:])])
    return pl.pallas_call(
        _odd_in_kernel,
        grid=(t // tm,),
        in_specs=[row(D_MODEL), _full((1, D_MODEL)), _full((D_MODEL, ODD_IN_PAD)), _full((MOBA_W, MOBA_W)),
                  _full((1, MOBA_W)), _full((1, MOBA_W)), _full((1, NSA_W)), _full((1, KV_W)), _full((1, KV_W))],
        out_specs=out_specs,
        out_shape=out_shape,
        compiler_params=_params("arbitrary"),
        name="odd_in",
    )(h2, gain, w, hsum, tile(moba_q_norm, MOBA_W), tile(moba_k_norm, MOBA_W), tile(nsa_q_norm, NSA_W),
      tile(nsa_ksel_norm, KV_W), tile(nsa_kwin_norm, KV_W))


def _compress_kernel(c_ref, w1_ref, w2_ref, pe_ref, g_ref, o_ref):
    kind = pl.program_id(0)
    c = c_ref[0, 0, 0]
    half = c.shape[1]
    n16 = c.shape[0]
    first = _dot(c, w1_ref[0, 0:half, :])
    second = _dot(c, w1_ref[0, half:2 * half, :])
    peb = _dot(pe_ref[0], w1_ref[0])[0:1, :]
    hid = _gelu(first + pltpu.roll(second, n16 - 1, 0) + peb)
    out = _dot(hid.astype(BF16), w2_ref[0])
    o_ref[0, 0, 0] = jnp.where(kind == 0, _rms(out, g_ref[...]), out).astype(BF16)


def _compress(kc, vc, batch, seq, pe_k, w1_k, w2_k, pe_v, w1_v, w2_v, kcmp_norm):
    n16 = seq // CMP_STRIDE
    half = CMP_STRIDE * HEAD_DIM

    def flat(x):
        return x.reshape(batch, n16, CMP_STRIDE, NSA_KV_HEADS, HEAD_DIM).transpose(0, 3, 1, 2, 4).reshape(
            batch, NSA_KV_HEADS, n16, half)

    c = jnp.stack([flat(kc), flat(vc)])
    w1 = jnp.stack([w1_k, w1_v]).astype(BF16)
    w2 = jnp.stack([w2_k, w2_v]).astype(BF16)
    pe = jnp.stack([pe_k, pe_v]).reshape(2, 1, 2 * half)
    pe = jnp.broadcast_to(pe, (2, 8, 2 * half)).astype(BF16)
    return pl.pallas_call(
        _compress_kernel,
        grid=(2, batch, NSA_KV_HEADS),
        in_specs=[pl.BlockSpec((1, 1, 1, n16, half), lambda k, b, h: (k, b, h, 0, 0)),
                  pl.BlockSpec((1, 2 * half, CMP_HIDDEN), lambda k, b, h: (k, 0, 0)),
                  pl.BlockSpec((1, CMP_HIDDEN, HEAD_DIM), lambda k, b, h: (k, 0, 0)),
                  pl.BlockSpec((1, 8, 2 * half), lambda k, b, h: (k, 0, 0)),
                  _full((1, HEAD_DIM))],
        out_specs=pl.BlockSpec((1, 1, 1, n16, HEAD_DIM), lambda k, b, h: (k, b, h, 0, 0)),
        out_shape=jax.ShapeDtypeStruct((2, batch, NSA_KV_HEADS, n16, HEAD_DIM), BF16),
        compiler_params=_params("arbitrary", "arbitrary", "arbitrary"),
        name="nsa_compress",
    )(c, w1, w2, pe, kcmp_norm.astype(F32).reshape(1, HEAD_DIM))


M_INIT = -1e30


LANES = 128


def _ones_column(v):
    pad = jnp.zeros(v.shape[:-1] + (LANES - HEAD_DIM,), v.dtype).at[..., 0].set(1)
    return jnp.concatenate([v, pad], axis=-1)


def _softmax_init(m_ref, acc_ref):
    m_ref[...] = jnp.full(m_ref.shape, M_INIT, F32)
    acc_ref[...] = jnp.zeros(acc_ref.shape, F32)


def _softmax_step(s, v_aug, m_ref, acc_ref):
    m_old = m_ref[...]
    m_new = jnp.maximum(m_old, jnp.max(s, axis=-1, keepdims=True))
    alpha = jnp.exp(m_old - m_new)
    p = jnp.exp(s - jnp.tile(m_new, (1, s.shape[1] // LANES)))
    acc_ref[...] = alpha * acc_ref[...] + _dot(p.astype(BF16), v_aug)
    m_ref[...] = m_new


def _softmax_result(acc_ref):
    acc = acc_ref[...]
    return acc[:, 0:HEAD_DIM] / acc[:, HEAD_DIM:HEAD_DIM + 1]


def _pick_top(score, lane, width, k):
    sel = jnp.zeros(score.shape, jnp.bool_)
    for _ in range(k):
        m, idx = _first_max(score, lane, width)
        hit = (lane == idx) & (m > -jnp.inf)
        sel = sel | hit
        score = jnp.where(lane == idx, -jnp.inf, score)
    return sel


MOBA_LANES = 64
MOBA_TK = 1024


def _moba_kernel(q_ref, ka_ref, va_ref, kmean_ref, o_ref, qa_ref, m_ref, acc_ref):
    i = pl.program_id(2)
    q = q_ref[0, 0]
    q_hi, q_lo = _split(q)
    km_hi, km_lo = _split(kmean_ref[0, 0])
    gate = _dot_nt(q_hi, km_hi) + (_dot_nt(q_hi, km_lo) + _dot_nt(q_lo, km_hi))
    blk = lax.broadcasted_iota(jnp.int32, gate.shape, 1)
    sel = _pick_top(jnp.where(blk < i, gate, -jnp.inf), blk, MOBA_LANES, MOBA_TOPK)
    qs = (q * (HEAD_DIM ** -0.5)).astype(BF16)
    qa_ref[...] = jnp.concatenate([qs, jnp.where(sel, 0.0, NEG).astype(BF16)], axis=1)
    _softmax_init(m_ref, acc_ref)

    def body(j, carry):
        start = pl.multiple_of(j * MOBA_TK, MOBA_TK)
        s = _dot_nt(qa_ref[...], ka_ref[0, 0, pl.ds(start, MOBA_TK), :])
        _softmax_step(s, va_ref[0, 0, pl.ds(start, MOBA_TK), :], m_ref, acc_ref)
        return carry

    lax.fori_loop(0, (i * MOBA_BLOCK + MOBA_TK - 1) // MOBA_TK, body, 0)
    start = pl.multiple_of(i * MOBA_BLOCK, MOBA_BLOCK)
    s = _dot_nt(qs, ka_ref[0, 0, pl.ds(start, MOBA_BLOCK), :][:, 0:HEAD_DIM])
    qpos = lax.broadcasted_iota(jnp.int32, s.shape, 0)
    kpos = lax.broadcasted_iota(jnp.int32, s.shape, 1)
    _softmax_step(jnp.where(kpos <= qpos, s, NEG), va_ref[0, 0, pl.ds(start, MOBA_BLOCK), :], m_ref, acc_ref)
    o_ref[0, 0] = _softmax_result(acc_ref).astype(BF16)


def _moba(qm, km, kmean, vm, batch, seq):
    nmb = seq // MOBA_BLOCK
    assert nmb <= MOBA_LANES and seq % MOBA_TK == 0
    heads = lambda x: x.reshape(batch, seq, MOBA_HEADS, HEAD_DIM).transpose(0, 2, 1, 3)
    onehot = jnp.asarray(np.eye(MOBA_LANES)[np.arange(seq) // MOBA_BLOCK], BF16)
    ka = jnp.concatenate([heads(km), jnp.broadcast_to(onehot, (batch, MOBA_HEADS, seq, MOBA_LANES))], axis=-1)
    kmean = kmean.reshape(batch, nmb, MOBA_HEADS, HEAD_DIM).transpose(0, 2, 1, 3)
    kmean = jnp.pad(kmean, ((0, 0), (0, 0), (0, MOBA_LANES - nmb), (0, 0)))
    o = pl.pallas_call(
        _moba_kernel,
        grid=(batch, MOBA_HEADS, nmb),
        in_specs=[pl.BlockSpec((1, 1, MOBA_BLOCK, HEAD_DIM), lambda b, h, i: (b, h, i, 0)),
                  pl.BlockSpec((1, 1, seq, LANES), lambda b, h, i: (b, h, 0, 0)),
                  pl.BlockSpec((1, 1, seq, LANES), lambda b, h, i: (b, h, 0, 0)),
                  pl.BlockSpec((1, 1, MOBA_LANES, HEAD_DIM), lambda b, h, i: (b, h, 0, 0))],
        out_specs=pl.BlockSpec((1, 1, MOBA_BLOCK, HEAD_DIM), lambda b, h, i: (b, h, i, 0)),
        out_shape=jax.ShapeDtypeStruct((batch, MOBA_HEADS, seq, HEAD_DIM), BF16),
        scratch_shapes=[pltpu.VMEM((MOBA_BLOCK, LANES), BF16), pltpu.VMEM((MOBA_BLOCK, LANES), F32),
                        pltpu.VMEM((MOBA_BLOCK, LANES), F32)],
        compiler_params=_params("arbitrary", "arbitrary", "arbitrary"),
        name="moba",
    )(heads(qm), ka, _ones_column(heads(vm)), kmean)
    return o.transpose(0, 2, 1, 3).reshape(batch * seq, MOBA_W)


NSA_TQ = 128
NSA_TK = 1024
SEL_LANES = 64
SUPER_KEYS = SEL_LANES * SEL_BLOCK


def _nsa_kernel(n_super, q_ref, kc_ref, vc_ref, ka_ref, va_ref, kvw_ref, gt_ref, e_ref, band_ref, o_ref,
                qa_ref, m_ref, acc_ref):
    qi = pl.program_id(2)
    tq = NSA_TQ
    s0 = qi * tq
    q = q_ref[0, 0, 0]

    sc = _dot_nt(q, kc_ref[0, 0])
    rq = lax.broadcasted_iota(jnp.int32, sc.shape, 0) & (tq - 1)
    n = lax.broadcasted_iota(jnp.int32, sc.shape, 1)
    vis = n * CMP_STRIDE + (CMP_BLOCK - 1) <= s0 + rq
    sc = jnp.where(vis, sc, NEG)
    pc = jnp.where(vis, jnp.exp(sc - jnp.max(sc, axis=-1, keepdims=True)), 0.0)
    pc = pc / jnp.maximum(jnp.sum(pc, axis=-1, keepdims=True), 1e-30)
    o_c = _dot(pc.astype(BF16), vc_ref[0, 0])
    imp = pc[0:tq]
    for g in range(1, NSA_GROUP):
        imp = imp + pc[g * tq:(g + 1) * tq]
    pslc = _dot_x2(imp, band_ref[...])

    nb = pslc.shape[1]
    blk = lax.broadcasted_iota(jnp.int32, pslc.shape, 1)
    cur = (s0 + lax.broadcasted_iota(jnp.int32, pslc.shape, 0)) // SEL_BLOCK
    c0 = s0 // SEL_BLOCK
    elig = (blk >= 1) & (blk <= cur - 2)
    sel = _pick_top(jnp.where(elig, pslc, -jnp.inf), blk, nb, SEL_TOPK - 3)
    sel = sel | (((blk == 0) | (blk == cur - 1)) & (blk < c0))
    bias = jnp.where(sel, 0.0, NEG).astype(BF16)
    for st in range(n_super):
        b = bias[:, st * SEL_LANES:(st + 1) * SEL_LANES]
        qa_ref[st] = jnp.concatenate([q, jnp.concatenate([b] * NSA_GROUP, axis=0)], axis=1)

    _softmax_init(m_ref, acc_ref)

    def body(kt, carry):
        start = pl.multiple_of(kt * NSA_TK, NSA_TK)
        s = _dot_nt(qa_ref[kt // (SUPER_KEYS // NSA_TK)], ka_ref[0, 0, pl.ds(start, NSA_TK), :])
        _softmax_step(s, va_ref[0, 0, pl.ds(start, NSA_TK), :], m_ref, acc_ref)
        return carry

    lax.fori_loop(0, (s0 + NSA_TK - 1) // NSA_TK, body, 0)
    d0 = pl.multiple_of(s0, tq)
    s = _dot_nt(q, ka_ref[0, 0, pl.ds(d0, tq), :][:, 0:HEAD_DIM])
    rqd = lax.broadcasted_iota(jnp.int32, s.shape, 0) & (tq - 1)
    col = lax.broadcasted_iota(jnp.int32, s.shape, 1)
    _softmax_step(jnp.where(col <= rqd, s, NEG), va_ref[0, 0, pl.ds(d0, tq), :], m_ref, acc_ref)
    o_s = _softmax_result(acc_ref)

    wlen = WINDOW + tq
    w0 = pl.multiple_of(jnp.maximum(s0 - WINDOW, 0), tq)
    kvw = kvw_ref[0, 0, pl.ds(w0, wlen), :]
    sw = _dot_nt(q, kvw[:, 0:HEAD_DIM])
    kabs = w0 + lax.broadcasted_iota(jnp.int32, sw.shape, 1)
    t = s0 + (lax.broadcasted_iota(jnp.int32, sw.shape, 0) & (tq - 1))
    sw = jnp.where((kabs <= t) & (kabs > t - WINDOW), sw, NEG)
    pw = jnp.exp(sw - jnp.max(sw, axis=-1, keepdims=True))
    o_w = _dot(pw.astype(BF16), kvw)[:, HEAD_DIM:2 * HEAD_DIM] / jnp.sum(pw, axis=-1, keepdims=True)

    w = NSA_GROUP * HEAD_DIM
    gexp = _dot_x2(gt_ref[...], e_ref[0])
    wide = lambda x: jnp.concatenate([x[g * tq:(g + 1) * tq] for g in range(NSA_GROUP)], axis=1)
    o_ref[...] = (gexp[:, 0:w] * wide(o_c) + gexp[:, w:2 * w] * wide(o_s) + gexp[:, 2 * w:3 * w] * wide(o_w)).astype(BF16)


def _nsa(qd, kcmp, vcmp, ks, vs, kw, vw, gates, batch, seq):
    tq = NSA_TQ
    nq = seq // tq
    nb = seq // SEL_BLOCK
    n16 = seq // CMP_STRIDE
    assert seq % SUPER_KEYS == 0
    n_super = seq // SUPER_KEYS
    rows = NSA_GROUP * tq
    heads = lambda x: x.reshape(batch, seq, NSA_KV_HEADS, HEAD_DIM).transpose(0, 2, 1, 3)
    qs = qd.reshape(batch, nq, tq, NSA_KV_HEADS, NSA_GROUP, HEAD_DIM).transpose(0, 3, 1, 4, 2, 5).reshape(
        batch, NSA_KV_HEADS, nq, rows, HEAD_DIM)
    onehot = jnp.asarray(np.eye(SEL_LANES)[(np.arange(seq) // SEL_BLOCK) % SEL_LANES], BF16)
    ka = jnp.concatenate([heads(ks), jnp.broadcast_to(onehot, (batch, NSA_KV_HEADS, seq, SEL_LANES))], axis=-1)
    kvw = jnp.concatenate([heads(kw), heads(vw)], axis=-1)
    e = np.zeros((NSA_KV_HEADS, 128, 3 * NSA_GROUP * HEAD_DIM), np.float32)
    for br in range(3):
        for hk in range(NSA_KV_HEADS):
            for g in range(NSA_GROUP):
                c = (br * NSA_GROUP + g) * HEAD_DIM
                e[hk, br * NSA_HEADS + hk * NSA_GROUP + g, c:c + HEAD_DIM] = 1.0
    nn, jj = np.arange(n16)[:, None], np.arange(nb)[None, :]
    band = ((nn >= 4 * jj - 1) & (nn <= 4 * jj + 3)).astype(np.float32)
    resident = lambda width: pl.BlockSpec((1, 1, seq, width), lambda b, h, i: (b, h, 0, 0))
    w = NSA_GROUP * HEAD_DIM
    return pl.pallas_call(
        functools.partial(_nsa_kernel, n_super),
        grid=(batch, NSA_KV_HEADS, nq),
        in_specs=[pl.BlockSpec((1, 1, 1, rows, HEAD_DIM), lambda b, h, i: (b, h, i, 0, 0)),
                  pl.BlockSpec((1, 1, n16, HEAD_DIM), lambda b, h, i: (b, h, 0, 0)),
                  pl.BlockSpec((1, 1, n16, HEAD_DIM), lambda b, h, i: (b, h, 0, 0)),
                  resident(LANES), resident(LANES), resident(LANES),
                  pl.BlockSpec((tq, 128), lambda b, h, i: (b * nq + i, 0)),
                  pl.BlockSpec((1, 128, 3 * w), lambda b, h, i: (h, 0, 0)),
                  pl.BlockSpec((n16, nb), lambda b, h, i: (0, 0))],
        out_specs=pl.BlockSpec((tq, w), lambda b, h, i: (b * nq + i, h)),
        out_shape=jax.ShapeDtypeStruct((batch * seq, NSA_W), BF16),
        scratch_shapes=[pltpu.VMEM((n_super, rows, LANES), BF16), pltpu.VMEM((rows, LANES), F32),
                        pltpu.VMEM((rows, LANES), F32)],
        compiler_params=_params("arbitrary", "arbitrary", "arbitrary"),
        name="nsa",
    )(qs, kcmp, vcmp, ka, _ones_column(heads(vs)), kvw, gates, jnp.asarray(e, BF16), jnp.asarray(band, BF16))


def _odd_out_kernel(om_ref, on_ref, h_ref, w_ref, o_ref):
    o_ref[...] = h_ref[...] + _dot(om_ref[...], w_ref[0:MOBA_W, :]) + _dot(on_ref[...], w_ref[MOBA_W:D_MODEL, :])


def _odd_out(o_moba, o_nsa, h2, w_out, tm=512):
    t = h2.shape[0]
    row = lambda w: pl.BlockSpec((tm, w), lambda i: (i, 0))
    return pl.pallas_call(
        _odd_out_kernel,
        grid=(t // tm,),
        in_specs=[row(MOBA_W), row(NSA_W), row(D_MODEL), _full((D_MODEL, D_MODEL))],
        out_specs=row(D_MODEL),
        out_shape=jax.ShapeDtypeStruct((t, D_MODEL), F32),
        compiler_params=_params("arbitrary"),
        name="odd_out",
    )(o_moba, o_nsa, h2, w_out)


def _even_layer(h2, batch, seq, norm, w_in, w_out, lam_re, lam_im, log_dt, b_re, b_im, c_re, c_im, d, w_glu, conv_w, conv_b):
    u, yb = _even_in(h2, norm.reshape(1, D_MODEL), w_in.astype(BF16), conv_w, conv_b.reshape(1, CONV_WIDTH), seq)
    ypre = _s5_mixer_pre(u, batch, seq, lam_re, lam_im, log_dt, b_re, b_im, c_re, c_im)
    return _even_out(ypre, u, yb, h2, d.reshape(1, S5_WIDTH), w_glu.astype(BF16), w_out.astype(BF16))


def _odd_layer(h2, batch, seq, norm, w_in, w_out, moba_q_norm, moba_k_norm, nsa_q_norm, nsa_kcmp_norm, nsa_ksel_norm,
               nsa_kwin_norm, cmp_pe_k, cmp_w1_k, cmp_w2_k, cmp_pe_v, cmp_w1_v, cmp_w2_v):
    qm, km, kmean, vm, qd, kc, vc, ks, vs, kw, vw, gates = _odd_in(
        h2, norm.reshape(1, D_MODEL), w_in, moba_q_norm, moba_k_norm, nsa_q_norm, nsa_ksel_norm, nsa_kwin_norm)
    cmp = _compress(kc, vc, batch, seq, cmp_pe_k, cmp_w1_k, cmp_w2_k, cmp_pe_v, cmp_w1_v, cmp_w2_v, nsa_kcmp_norm)
    o_moba = _moba(qm, km, kmean, vm, batch, seq)
    o_nsa = _nsa(qd, cmp[0], cmp[1], ks, vs, kw, vw, gates, batch, seq)
    return _odd_out(o_moba, o_nsa, h2, w_out.astype(BF16))


def kernel(x, ev_norm_mix, ev_w_in, ev_w_out, s5_lam_re, s5_lam_im, s5_log_dt, s5_b_re, s5_b_im, s5_c_re, s5_c_im, s5_d, s5_w_glu, conv_w, conv_b, od_norm_mix, od_w_in, od_w_out, moba_q_norm, moba_k_norm, nsa_q_norm, nsa_kcmp_norm, nsa_ksel_norm, nsa_kwin_norm, cmp_pe_k, cmp_w1_k, cmp_w2_k, cmp_pe_v, cmp_w1_v, cmp_w2_v, moe_norm, moe_w_group, moe_b_group, moe_w_expert, moe_b_expert, moe_w_gate, moe_w_up, moe_w_down):
    batch, seq, _ = x.shape
    depth = moe_norm.shape[0]
    h = x.reshape(batch * seq, D_MODEL)
    for layer in range(depth):
        i = layer // 2
        if layer % 2 == 0:
            h = _even_layer(h, batch, seq, ev_norm_mix[i], ev_w_in[i], ev_w_out[i], s5_lam_re[i], s5_lam_im[i], s5_log_dt[i],
                            s5_b_re[i], s5_b_im[i], s5_c_re[i], s5_c_im[i], s5_d[i], s5_w_glu[i], conv_w[i], conv_b[i])
        else:
            h = _odd_layer(h, batch, seq, od_norm_mix[i], od_w_in[i], od_w_out[i], moba_q_norm[i], moba_k_norm[i],
                           nsa_q_norm[i], nsa_kcmp_norm[i], nsa_ksel_norm[i], nsa_kwin_norm[i], cmp_pe_k[i], cmp_w1_k[i],
                           cmp_w2_k[i], cmp_pe_v[i], cmp_w1_v[i], cmp_w2_v[i])
        h = _moe(h, moe_norm[layer].reshape(1, D_MODEL), moe_w_group[layer], moe_b_group[layer], moe_w_expert[layer],
                 moe_b_expert[layer], moe_w_gate[layer], moe_w_up[layer], moe_w_down[layer])
    return h.reshape(batch, seq, D_MODEL)
```

```python
import functools
import math

import jax
import jax.numpy as jnp
import numpy as np
from jax import lax
from jax.experimental import pallas as pl
from jax.experimental.pallas import tpu as pltpu

D_MODEL = 1024
HEAD_DIM = 64
EPS = 1e-6
S5_WIDTH = 256
S5_GROUP = 16
S5_GROUPS = 16
S5_STATE = 64
S5_CHUNK = 16
CONV_WIDTH = 768
CONV_K = 3
MOBA_HEADS = 4
NSA_HEADS = 12
NSA_KV_HEADS = 2
NSA_GROUP = 6
MOBA_W = 256
NSA_W = 768
KV_W = 128
MOBA_BLOCK = 256
MOBA_TOPK = 3
CMP_BLOCK = 32
CMP_STRIDE = 16
CMP_HIDDEN = 256
SEL_BLOCK = 64
SEL_TOPK = 8
WINDOW = 512
N_GROUPS = 4
EXPERTS_PER_GROUP = 4
N_EXPERTS = 16
EXPERT_FF = 256

VMEM_LIMIT_BYTES = 56 * 1024 * 1024
NEG = -float(2 ** 30)
F32 = jnp.float32
BF16 = jnp.bfloat16


def _params(*semantics):
    return pltpu.CompilerParams(dimension_semantics=semantics, vmem_limit_bytes=VMEM_LIMIT_BYTES)


def _dot(a, b):
    return jnp.dot(a, b, preferred_element_type=F32)


def _dot_nt(a, b):
    return lax.dot_general(a, b, (((1,), (1,)), ((), ())), preferred_element_type=F32)


def _split(x):
    hi = x.astype(BF16)
    lo = (x - hi.astype(F32)).astype(BF16)
    return hi, lo


def _dot_x2(x, w):
    hi, lo = _split(x)
    return _dot(hi, w) + _dot(lo, w)


def _dot_x3(x, w_hi, w_lo):
    hi, lo = _split(x)
    return _dot(hi, w_hi) + (_dot(hi, w_lo) + _dot(lo, w_hi))


def _rms(x, gain):
    return x * lax.rsqrt(jnp.mean(x * x, axis=-1, keepdims=True) + EPS) * gain


def _gelu(x):
    return 0.5 * x * (1.0 + jnp.tanh(math.sqrt(2.0 / math.pi) * (x + 0.044715 * (x * x * x))))


def _sigmoid(x):
    return 1.0 / (1.0 + jnp.exp(-x))


def _full(shape):
    n = len(shape)
    return pl.BlockSpec(shape, lambda *_: (0,) * n)


def _even_in_kernel(tiles_per_seq, x_ref, g_ref, w_ref, cw_ref, cb_ref, u_ref, yb_ref, carry_ref):
    i = pl.program_id(0)
    xn = _rms(x_ref[...], g_ref[...]).astype(BF16)
    u_ref[...] = _dot(xn, w_ref[:, 0:S5_WIDTH])
    o = S5_WIDTH
    xc = _dot(xn, w_ref[:, o:o + CONV_WIDTH])
    gb = _dot(xn, w_ref[:, o + CONV_WIDTH:o + 2 * CONV_WIDTH])
    gc = _dot(xn, w_ref[:, o + 2 * CONV_WIDTH:o + 3 * CONV_WIDTH])
    z = gc * xc
    tm = z.shape[0]

    @pl.when(i % tiles_per_seq == 0)
    def _():
        carry_ref[...] = jnp.zeros_like(carry_ref)

    row = lax.broadcasted_iota(jnp.int32, z.shape, 0)
    prev1 = carry_ref[7:8, :]
    prev2 = carry_ref[6:7, :]
    z1 = jnp.where(row == 0, prev1, pltpu.roll(z, 1, 0))
    z2 = jnp.where(row == 0, prev2, jnp.where(row == 1, prev1, pltpu.roll(z, 2, 0)))
    y = cw_ref[0:1, :] * z2 + cw_ref[1:2, :] * z1 + cw_ref[2:3, :] * z + cb_ref[...]
    yb_ref[...] = (gb * y).astype(BF16)
    carry_ref[...] = z[tm - 8:tm, :]


def _even_in(x2, gain, w_in, conv_w, conv_b, seq, tm=512):
    t = x2.shape[0]
    n_in = w_in.shape[1]
    return pl.pallas_call(
        functools.partial(_even_in_kernel, seq // tm),
        grid=(t // tm,),
        in_specs=[pl.BlockSpec((tm, D_MODEL), lambda i: (i, 0)), _full((1, D_MODEL)),
                  _full((D_MODEL, n_in)), _full((CONV_K, CONV_WIDTH)), _full((1, CONV_WIDTH))],
        out_specs=[pl.BlockSpec((tm, S5_WIDTH), lambda i: (i, 0)),
                   pl.BlockSpec((tm, CONV_WIDTH), lambda i: (i, 0))],
        out_shape=[jax.ShapeDtypeStruct((t, S5_WIDTH), F32), jax.ShapeDtypeStruct((t, CONV_WIDTH), BF16)],
        scratch_shapes=[pltpu.VMEM((8, CONV_WIDTH), F32)],
        compiler_params=_params("arbitrary"),
        name="even_in",
    )(x2, gain, w_in, conv_w, conv_b)


def _s5_weights(lam_re, lam_im, log_dt, b_re, b_im, c_re, c_im):
    g, p, hg, ck = S5_GROUPS, S5_STATE, S5_GROUP, S5_CHUNK
    lr, li = lam_re.astype(F32), lam_im.astype(F32)
    dt = jnp.exp(log_dt.astype(F32))[:, None]
    mag = jnp.exp(lr * dt)
    a_re, a_im = mag * jnp.cos(li * dt), mag * jnp.sin(li * dt)
    den = lr * lr + li * li
    f_re = ((a_re - 1.0) * lr + a_im * li) / den
    f_im = (a_im * lr - (a_re - 1.0) * li) / den
    br, bi = b_re.astype(F32), b_im.astype(F32)
    bb_re = f_re[..., None] * br - f_im[..., None] * bi
    bb_im = f_re[..., None] * bi + f_im[..., None] * br
    pw_re, pw_im = [jnp.ones_like(a_re)], [jnp.zeros_like(a_im)]
    for _ in range(ck):
        r, m = pw_re[-1], pw_im[-1]
        pw_re.append(r * a_re - m * a_im)
        pw_im.append(r * a_im + m * a_re)
    pw_re, pw_im = jnp.stack(pw_re), jnp.stack(pw_im)
    cr, ci = c_re.astype(F32), c_im.astype(F32)
    rev_re, rev_im = pw_re[ck - 1::-1][:ck], pw_im[ck - 1::-1][:ck]
    ws_re = rev_re[:, :, :, None] * bb_re[None] - rev_im[:, :, :, None] * bb_im[None]
    ws_im = rev_re[:, :, :, None] * bb_im[None] + rev_im[:, :, :, None] * bb_re[None]
    ws_re = ws_re.transpose(1, 0, 3, 2).reshape(g, ck * hg, p)
    ws_im = ws_im.transpose(1, 0, 3, 2).reshape(g, ck * hg, p)
    ca_re = cr[None] * pw_re[1:, :, None, :] - ci[None] * pw_im[1:, :, None, :]
    ca_im = cr[None] * pw_im[1:, :, None, :] + ci[None] * pw_re[1:, :, None, :]
    wc_re = ca_re.transpose(1, 3, 0, 2).reshape(g, p, ck * hg)
    wc_im = (-ca_im).transpose(1, 3, 0, 2).reshape(g, p, ck * hg)
    cb_re = jnp.einsum('ghp,kgp,gpj->kghj', cr, pw_re[:ck], bb_re) - jnp.einsum('ghp,kgp,gpj->kghj', cr, pw_im[:ck], bb_im) \
        - jnp.einsum('ghp,kgp,gpj->kghj', ci, pw_re[:ck], bb_im) - jnp.einsum('ghp,kgp,gpj->kghj', ci, pw_im[:ck], bb_re)
    lag = np.arange(ck)[None, :] - np.arange(ck)[:, None]
    tz = cb_re[np.clip(lag, 0, ck - 1)]
    tz = jnp.where((lag >= 0)[:, :, None, None, None], tz, 0.0)
    tz = tz.transpose(2, 0, 4, 1, 3).reshape(g, ck * hg, ck * hg)

    def pair_blockdiag(m):
        m = m.reshape(g // 2, 2, m.shape[1], m.shape[2])
        z = jnp.zeros_like(m[:, 0])
        return jnp.concatenate([jnp.concatenate([m[:, 0], z], axis=2), jnp.concatenate([z, m[:, 1]], axis=2)], axis=1)

    ws = jnp.concatenate([pair_blockdiag(ws_re), pair_blockdiag(ws_im)], axis=2)
    wc = jnp.concatenate([pair_blockdiag(wc_re), pair_blockdiag(wc_im)], axis=1)
    tzp = pair_blockdiag(tz)
    a16_re = pw_re[ck].reshape(g // 2, 1, 2 * p)
    a16_im = pw_im[ck].reshape(g // 2, 1, 2 * p)
    return ws.astype(BF16), wc.astype(BF16), tzp.astype(BF16), a16_re, a16_im


def _s5_state_kernel(u_ref, ws_ref, s_ref):
    s_ref[0, 0] = _dot(u_ref[0, 0], ws_ref[0])


def _s5_scan_kernel(s_ref, are_ref, aim_ref, xprev_ref, st_ref):
    @pl.when(pl.program_id(0) == 0)
    def _():
        st_ref[...] = jnp.zeros_like(st_ref)

    a_re, a_im = are_ref[...], aim_ref[...]
    n = s_ref.shape[0]
    half = a_re.shape[1]

    def body(c, carry):
        xr, xi = carry
        xprev_ref[c, :, 0:half] = xr
        xprev_ref[c, :, half:2 * half] = xi
        s = s_ref[c]
        nr = a_re * xr - a_im * xi + s[:, 0:half]
        ni = a_re * xi + a_im * xr + s[:, half:2 * half]
        return nr, ni

    xr, xi = lax.fori_loop(0, n, body, (st_ref[:, 0:half], st_ref[:, half:2 * half]), unroll=8)
    st_ref[:, 0:half] = xr
    st_ref[:, half:2 * half] = xi


def _s5_out_kernel(u_ref, xp_ref, tz_ref, wc_ref, y_ref):
    y_ref[0, 0] = _dot(u_ref[0, 0], tz_ref[0]) + _dot(xp_ref[0, 0].astype(BF16), wc_ref[0])


def _s5_mixer_pre(u, batch, seq, lam_re, lam_im, log_dt, b_re, b_im, c_re, c_im):
    ws, wc, tz, a16_re, a16_im = _s5_weights(lam_re, lam_im, log_dt, b_re, b_im, c_re, c_im)
    ck, hg = S5_CHUNK, S5_GROUP
    nc = seq // ck
    npair = S5_GROUPS // 2
    pw = 2 * ck * hg
    sw = 4 * S5_STATE
    up = u.astype(BF16).reshape(batch, nc, ck, npair, 2, hg).transpose(0, 3, 1, 4, 2, 5).reshape(batch, npair, nc, pw)
    s = pl.pallas_call(
        _s5_state_kernel,
        grid=(batch, npair),
        in_specs=[pl.BlockSpec((1, 1, nc, pw), lambda b, p: (b, p, 0, 0)), pl.BlockSpec((1, pw, sw), lambda b, p: (p, 0, 0))],
        out_specs=pl.BlockSpec((1, 1, nc, sw), lambda b, p: (b, p, 0, 0)),
        out_shape=jax.ShapeDtypeStruct((batch, npair, nc, sw), F32),
        compiler_params=_params("arbitrary", "arbitrary"),
        name="s5_state",
    )(up, ws)
    rows = batch * npair
    s_t = s.transpose(2, 0, 1, 3).reshape(nc, rows, sw)
    are = jnp.broadcast_to(a16_re.reshape(1, npair, 2 * S5_STATE), (batch, npair, 2 * S5_STATE)).reshape(rows, 2 * S5_STATE)
    aim = jnp.broadcast_to(a16_im.reshape(1, npair, 2 * S5_STATE), (batch, npair, 2 * S5_STATE)).reshape(rows, 2 * S5_STATE)
    tc = min(nc, 256)
    xprev = pl.pallas_call(
        _s5_scan_kernel,
        grid=(nc // tc,),
        in_specs=[pl.BlockSpec((tc, rows, sw), lambda i: (i, 0, 0)), _full((rows, 2 * S5_STATE)), _full((rows, 2 * S5_STATE))],
        out_specs=pl.BlockSpec((tc, rows, sw), lambda i: (i, 0, 0)),
        out_shape=jax.ShapeDtypeStruct((nc, rows, sw), F32),
        scratch_shapes=[pltpu.VMEM((rows, sw), F32)],
        compiler_params=_params("arbitrary"),
        name="s5_scan",
    )(s_t, are, aim)
    xp = xprev.reshape(nc, batch, npair, sw).transpose(1, 2, 0, 3)
    y = pl.pallas_call(
        _s5_out_kernel,
        grid=(batch, npair),
        in_specs=[pl.BlockSpec((1, 1, nc, pw), lambda b, p: (b, p, 0, 0)), pl.BlockSpec((1, 1, nc, sw), lambda b, p: (b, p, 0, 0)),
                  pl.BlockSpec((1, pw, pw), lambda b, p: (p, 0, 0)), pl.BlockSpec((1, sw, pw), lambda b, p: (p, 0, 0))],
        out_specs=pl.BlockSpec((1, 1, nc, pw), lambda b, p: (b, p, 0, 0)),
        out_shape=jax.ShapeDtypeStruct((batch, npair, nc, pw), F32),
        compiler_params=_params("arbitrary", "arbitrary"),
        name="s5_out",
    )(up, xp, tz, wc)
    return y.reshape(batch, npair, nc, 2, ck, hg).transpose(0, 2, 4, 1, 3, 5).reshape(batch * seq, S5_WIDTH)


def _even_out_kernel(ypre_ref, u_ref, yb_ref, x_ref, d_ref, wglu_ref, wout_ref, o_ref):
    y = _gelu(ypre_ref[...] + d_ref[...] * u_ref[...])
    y = y * _sigmoid(_dot(y.astype(BF16), wglu_ref[...]))
    o_ref[...] = (x_ref[...] + _dot(y.astype(BF16), wout_ref[0:S5_WIDTH, :])
                  + _dot(yb_ref[...], wout_ref[S5_WIDTH:D_MODEL, :]))


def _even_out(ypre, u, yb, x2, d, w_glu, w_out, tm=512):
    t = x2.shape[0]
    row = lambda w: pl.BlockSpec((tm, w), lambda i: (i, 0))
    return pl.pallas_call(
        _even_out_kernel,
        grid=(t // tm,),
        in_specs=[row(S5_WIDTH), row(S5_WIDTH), row(CONV_WIDTH), row(D_MODEL), _full((1, S5_WIDTH)),
                  _full((S5_WIDTH, S5_WIDTH)), _full((D_MODEL, D_MODEL))],
        out_specs=row(D_MODEL),
        out_shape=jax.ShapeDtypeStruct((t, D_MODEL), F32),
        compiler_params=_params("arbitrary"),
        name="even_out",
    )(ypre, u, yb, x2, d, w_glu, w_out)


def _first_max(v, pos, width, axis=-1):
    m = jnp.max(v, axis=axis, keepdims=True)
    idx = jnp.min(jnp.where(v == m, pos, width), axis=axis, keepdims=True)
    return m, idx


def _moe_kernel(h_ref, g_ref, wr_hi_ref, wr_lo_ref, br_ref, wg_ref, wu_ref, wd_ref, o_ref, xn_ref, gate_ref):
    e = pl.program_id(1)

    @pl.when(e == 0)
    def _():
        h = h_ref[...]
        xn = _rms(h, g_ref[...])
        xn_ref[...] = xn.astype(BF16)
        o_ref[...] = h
        logits = _dot_x3(xn, wr_hi_ref[...], wr_lo_ref[...]) + br_ref[...]
        lane = lax.broadcasted_iota(jnp.int32, logits.shape, 1)
        width = logits.shape[1]
        is_g = lane < N_GROUPS
        gl = jnp.where(is_g, logits, -jnp.inf)
        gm, gi = _first_max(gl, lane, width)
        gw = 1.0 / jnp.sum(jnp.where(is_g, jnp.exp(gl - gm), 0.0), axis=-1, keepdims=True)
        lo = N_GROUPS + gi * EXPERTS_PER_GROUP
        in_grp = (lane >= lo) & (lane < lo + EXPERTS_PER_GROUP)
        el = jnp.where(in_grp, logits, -jnp.inf)
        m1, i1 = _first_max(el, lane, width)
        m2, i2 = _first_max(jnp.where(lane == i1, -jnp.inf, el), lane, width)
        p2 = jnp.exp(m2 - m1)
        w1 = gw / (1.0 + p2)
        w2 = gw * p2 / (1.0 + p2)
        gate_ref[...] = jnp.where(lane == i1, w1, 0.0) + jnp.where(lane == i2, w2, 0.0)

    xn = xn_ref[...]
    lane = lax.broadcasted_iota(jnp.int32, gate_ref.shape, 1)
    ge = jnp.sum(jnp.where(lane == e + N_GROUPS, gate_ref[...], 0.0), axis=-1, keepdims=True)
    h1 = _dot(xn, wg_ref[0])
    h3 = _dot(xn, wu_ref[0])
    act = (h1 * _sigmoid(h1)) * h3 * ge
    o_ref[...] += _dot(act.astype(BF16), wd_ref[0])


def _moe(h2, gain, w_group, b_group, w_expert, b_expert, w_gate, w_up, w_down, tm=512):
    t = h2.shape[0]
    rw = 128
    wr = jnp.zeros((D_MODEL, rw), F32).at[:, 0:N_GROUPS].set(w_group).at[:, N_GROUPS:N_GROUPS + N_EXPERTS].set(w_expert)
    br = jnp.zeros((1, rw), F32).at[0, 0:N_GROUPS].set(b_group).at[0, N_GROUPS:N_GROUPS + N_EXPERTS].set(b_expert)
    wr_hi = wr.astype(BF16)
    wr_lo = (wr - wr_hi.astype(F32)).astype(BF16)
    return pl.pallas_call(
        _moe_kernel,
        grid=(t // tm, N_EXPERTS),
        in_specs=[pl.BlockSpec((tm, D_MODEL), lambda i, e: (i, 0)), _full((1, D_MODEL)),
                  _full((D_MODEL, rw)), _full((D_MODEL, rw)), _full((1, rw)),
                  pl.BlockSpec((1, D_MODEL, EXPERT_FF), lambda i, e: (e, 0, 0)),
                  pl.BlockSpec((1, D_MODEL, EXPERT_FF), lambda i, e: (e, 0, 0)),
                  pl.BlockSpec((1, EXPERT_FF, D_MODEL), lambda i, e: (e, 0, 0))],
        out_specs=pl.BlockSpec((tm, D_MODEL), lambda i, e: (i, 0)),
        out_shape=jax.ShapeDtypeStruct((t, D_MODEL), F32),
        scratch_shapes=[pltpu.VMEM((tm, D_MODEL), BF16), pltpu.VMEM((tm, rw), F32)],
        compiler_params=_params("arbitrary", "arbitrary"),
        name="moe",
    )(h2, gain, wr_hi, wr_lo, br, w_gate.astype(BF16), w_up.astype(BF16), w_down.astype(BF16))


ODD_SPLITS = (MOBA_W, MOBA_W, MOBA_W, NSA_W, KV_W, KV_W, KV_W, KV_W, KV_W, KV_W, 128)
ODD_IN_PAD = sum(ODD_SPLITS)


def _head_rms(x, hsum, gain):
    w = x.shape[1]
    ss = jnp.concatenate([_dot_x2(x[:, o:o + hsum.shape[0]] * x[:, o:o + hsum.shape[0]], hsum)
                          for o in range(0, w, hsum.shape[0])], axis=1) if w > hsum.shape[0] else _dot_x2(x * x, hsum)
    return x * lax.rsqrt(ss * (1.0 / HEAD_DIM) + EPS) * gain


def _odd_in_kernel(x_ref, g_ref, w_ref, hsum_ref, gq_ref, gk_ref, gnq_ref, gks_ref, gkw_ref,
                   qm_ref, km_ref, kmean_ref, vm_ref, qd_ref, kc_ref, vc_ref, ks_ref, vs_ref, kw_ref, vw_ref, gt_ref):
    xn = _rms(x_ref[...], g_ref[...]).astype(BF16)
    offs = np.cumsum((0,) + ODD_SPLITS)
    col = lambda j: _dot(xn, w_ref[:, int(offs[j]):int(offs[j + 1])])
    hsum = hsum_ref[...]
    hsum128 = hsum_ref[0:128, 0:128]
    qm_ref[...] = _head_rms(col(0), hsum, gq_ref[...])
    km = _head_rms(col(1), hsum, gk_ref[...])
    km_ref[...] = km.astype(BF16)
    tm = km.shape[0]
    for j in range(tm // MOBA_BLOCK):
        kmean_ref[0, j:j + 1, :] = jnp.mean(km[j * MOBA_BLOCK:(j + 1) * MOBA_BLOCK, :], axis=0, keepdims=True)
    vm_ref[...] = col(2).astype(BF16)
    qd_ref[...] = (_head_rms(col(3), hsum, gnq_ref[...]) * (HEAD_DIM ** -0.5)).astype(BF16)
    kc_ref[...] = col(4).astype(BF16)
    vc_ref[...] = col(5).astype(BF16)
    ks_ref[...] = _head_rms(col(6), hsum128, gks_ref[...]).astype(BF16)
    vs_ref[...] = col(7).astype(BF16)
    kw_ref[...] = _head_rms(col(8), hsum128, gkw_ref[...]).astype(BF16)
    vw_ref[...] = col(9).astype(BF16)
    gt_ref[...] = _sigmoid(col(10))


def _odd_in(h2, gain, w_in, moba_q_norm, moba_k_norm, nsa_q_norm, nsa_ksel_norm, nsa_kwin_norm, tm=512):
    t = h2.shape[0]
    w = jnp.pad(w_in, ((0, 0), (0, ODD_IN_PAD - w_in.shape[1]))).astype(BF16)
    hsum = jnp.asarray(np.kron(np.eye(MOBA_W // HEAD_DIM), np.ones((HEAD_DIM, HEAD_DIM))), BF16)
    tile = lambda g, width: jnp.tile(g.astype(F32), width // HEAD_DIM).reshape(1, width)
    row = lambda width: pl.BlockSpec((tm, width), lambda i: (i, 0))
    nmb = tm // MOBA_BLOCK
    widths = (MOBA_W, MOBA_W, MOBA_W, NSA_W, KV_W, KV_W, KV_W, KV_W, KV_W, KV_W, 128)
    dtypes = (F32, BF16, BF16, BF16, BF16, BF16, BF16, BF16, BF16, BF16, F32)
    out_specs = [row(widths[0]), row(widths[1]), pl.BlockSpec((1, nmb, MOBA_W), lambda i: (i, 0, 0))] + [row(wd) for wd in widths[2:]]
    out_shape = ([jax.ShapeDtypeStruct((t, widths[0]), dtypes[0]), jax.ShapeDtypeStruct((t, widths[1]), dtypes[1]),
                  jax.ShapeDtypeStruct((t // tm, nmb, MOBA_W), F32)]
                 + [jax.ShapeDtypeStruct((t, wd), dt) for wd, dt in zip(widths[2:], dtypes[2:])])
    return pl.pallas_call(
        _odd_in_kernel,
        grid=(t // tm,),
        in_specs=[row(D_MODEL), _full((1, D_MODEL)), _full((D_MODEL, ODD_IN_PAD)), _full((MOBA_W, MOBA_W)),
                  _full((1, MOBA_W)), _full((1, MOBA_W)), _full((1, NSA_W)), _full((1, KV_W)), _full((1, KV_W))],
        out_specs=out_specs,
        out_shape=out_shape,
        compiler_params=_params("arbitrary"),
        name="odd_in",
    )(h2, gain, w, hsum, tile(moba_q_norm, MOBA_W), tile(moba_k_norm, MOBA_W), tile(nsa_q_norm, NSA_W),
      tile(nsa_ksel_norm, KV_W), tile(nsa_kwin_norm, KV_W))


def _compress_kernel(c_ref, w1_ref, w2_ref, pe_ref, g_ref, o_ref):
    kind = pl.program_id(0)
    c = c_ref[0, 0, 0]
    half = c.shape[1]
    n16 = c.shape[0]
    first = _dot(c, w1_ref[0, 0:half, :])
    second = _dot(c, w1_ref[0, half:2 * half, :])
    peb = _dot(pe_ref[0], w1_ref[0])[0:1, :]
    hid = _gelu(first + pltpu.roll(second, n16 - 1, 0) + peb)
    out = _dot(hid.astype(BF16), w2_ref[0])
    o_ref[0, 0, 0] = jnp.where(kind == 0, _rms(out, g_ref[...]), out).astype(BF16)


def _compress(kc, vc, batch, seq, pe_k, w1_k, w2_k, pe_v, w1_v, w2_v, kcmp_norm):
    n16 = seq // CMP_STRIDE
    half = CMP_STRIDE * HEAD_DIM

    def flat(x):
        return x.reshape(batch, n16, CMP_STRIDE, NSA_KV_HEADS, HEAD_DIM).transpose(0, 3, 1, 2, 4).reshape(
            batch, NSA_KV_HEADS, n16, half)

    c = jnp.stack([flat(kc), flat(vc)])
    w1 = jnp.stack([w1_k, w1_v]).astype(BF16)
    w2 = jnp.stack([w2_k, w2_v]).astype(BF16)
    pe = jnp.stack([pe_k, pe_v]).reshape(2, 1, 2 * half)
    pe = jnp.broadcast_to(pe, (2, 8, 2 * half)).astype(BF16)
    return pl.pallas_call(
        _compress_kernel,
        grid=(2, batch, NSA_KV_HEADS),
        in_specs=[pl.BlockSpec((1, 1, 1, n16, half), lambda k, b, h: (k, b, h, 0, 0)),
                  pl.BlockSpec((1, 2 * half, CMP_HIDDEN), lambda k, b, h: (k, 0, 0)),
                  pl.BlockSpec((1, CMP_HIDDEN, HEAD_DIM), lambda k, b, h: (k, 0, 0)),
                  pl.BlockSpec((1, 8, 2 * half), lambda k, b, h: (k, 0, 0)),
                  _full((1, HEAD_DIM))],
        out_specs=pl.BlockSpec((1, 1, 1, n16, HEAD_DIM), lambda k, b, h: (k, b, h, 0, 0)),
        out_shape=jax.ShapeDtypeStruct((2, batch, NSA_KV_HEADS, n16, HEAD_DIM), BF16),
        compiler_params=_params("arbitrary", "arbitrary", "arbitrary"),
        name="nsa_compress",
    )(c, w1, w2, pe, kcmp_norm.astype(F32).reshape(1, HEAD_DIM))


M_INIT = -1e30


LANES = 128


def _ones_column(v):
    pad = jnp.zeros(v.shape[:-1] + (LANES - HEAD_DIM,), v.dtype).at[..., 0].set(1)
    return jnp.concatenate([v, pad], axis=-1)


def _softmax_init(m_ref, acc_ref):
    m_ref[...] = jnp.full(m_ref.shape, M_INIT, F32)
    acc_ref[...] = jnp.zeros(acc_ref.shape, F32)


def _softmax_step(s, v_aug, m_ref, acc_ref):
    m_old = m_ref[...]
    m_new = jnp.maximum(m_old, jnp.max(s, axis=-1, keepdims=True))
    alpha = jnp.exp(m_old - m_new)
    p = jnp.exp(s - jnp.tile(m_new, (1, s.shape[1] // LANES)))
    acc_ref[...] = alpha * acc_ref[...] + _dot(p.astype(BF16), v_aug)
    m_ref[...] = m_new


def _softmax_result(acc_ref):
    acc = acc_ref[...]
    return acc[:, 0:HEAD_DIM] * (1.0 / acc[:, HEAD_DIM:HEAD_DIM + 1])


def _pick_top(score, pos, width, k):
    sel = jnp.zeros(score.shape, jnp.bool_)
    for _ in range(k):
        m, idx = _first_max(score, pos, width, axis=0)
        hit = (pos == idx) & (m > -jnp.inf)
        sel = sel | hit
        score = jnp.where(pos == idx, -jnp.inf, score)
    return sel


MOBA_LANES = 64
MOBA_TK = 1024


def _moba_kernel(q_ref, ka_ref, va_ref, kmean_ref, o_ref, qa_ref, m_ref, acc_ref):
    i = pl.program_id(2)
    q = q_ref[0, 0]
    q_hi, q_lo = _split(q)
    km_hi, km_lo = _split(kmean_ref[0, 0])
    gate = _dot_nt(km_hi, q_hi) + (_dot_nt(km_lo, q_hi) + _dot_nt(km_hi, q_lo))
    blk = lax.broadcasted_iota(jnp.int32, gate.shape, 0)
    sel = _pick_top(jnp.where(blk < i, gate, -jnp.inf), blk, gate.shape[0], MOBA_TOPK)
    qs = (q * (HEAD_DIM ** -0.5)).astype(BF16)
    bias = jnp.where(sel, 0.0, NEG).T[:, 0:MOBA_LANES]
    qa_ref[...] = jnp.concatenate([qs, bias.astype(BF16)], axis=1)
    _softmax_init(m_ref, acc_ref)

    def body(j, carry):
        start = pl.multiple_of(j * MOBA_TK, MOBA_TK)
        s = _dot_nt(qa_ref[...], ka_ref[0, 0, pl.ds(start, MOBA_TK), :])
        _softmax_step(s, va_ref[0, 0, pl.ds(start, MOBA_TK), :], m_ref, acc_ref)
        return carry

    lax.fori_loop(0, (i * MOBA_BLOCK + MOBA_TK - 1) // MOBA_TK, body, 0)
    start = pl.multiple_of(i * MOBA_BLOCK, MOBA_BLOCK)
    s = _dot_nt(qs, ka_ref[0, 0, pl.ds(start, MOBA_BLOCK), :][:, 0:HEAD_DIM])
    qpos = lax.broadcasted_iota(jnp.int32, s.shape, 0)
    kpos = lax.broadcasted_iota(jnp.int32, s.shape, 1)
    _softmax_step(jnp.where(kpos <= qpos, s, NEG), va_ref[0, 0, pl.ds(start, MOBA_BLOCK), :], m_ref, acc_ref)
    o_ref[0, 0] = _softmax_result(acc_ref).astype(BF16)


def _moba(qm, km, kmean, vm, batch, seq):
    nmb = seq // MOBA_BLOCK
    assert nmb <= MOBA_LANES and seq % MOBA_TK == 0
    heads = lambda x: x.reshape(batch, seq, MOBA_HEADS, HEAD_DIM).transpose(0, 2, 1, 3)
    onehot = jnp.asarray(np.eye(MOBA_LANES)[np.arange(seq) // MOBA_BLOCK], BF16)
    ka = jnp.concatenate([heads(km), jnp.broadcast_to(onehot, (batch, MOBA_HEADS, seq, MOBA_LANES))], axis=-1)
    kmean = kmean.reshape(batch, nmb, MOBA_HEADS, HEAD_DIM).transpose(0, 2, 1, 3)
    kmean = jnp.pad(kmean, ((0, 0), (0, 0), (0, LANES - nmb), (0, 0)))
    o = pl.pallas_call(
        _moba_kernel,
        grid=(batch, MOBA_HEADS, nmb),
        in_specs=[pl.BlockSpec((1, 1, MOBA_BLOCK, HEAD_DIM), lambda b, h, i: (b, h, i, 0)),
                  pl.BlockSpec((1, 1, seq, LANES), lambda b, h, i: (b, h, 0, 0)),
                  pl.BlockSpec((1, 1, seq, LANES), lambda b, h, i: (b, h, 0, 0)),
                  pl.BlockSpec((1, 1, LANES, HEAD_DIM), lambda b, h, i: (b, h, 0, 0))],
        out_specs=pl.BlockSpec((1, 1, MOBA_BLOCK, HEAD_DIM), lambda b, h, i: (b, h, i, 0)),
        out_shape=jax.ShapeDtypeStruct((batch, MOBA_HEADS, seq, HEAD_DIM), BF16),
        scratch_shapes=[pltpu.VMEM((MOBA_BLOCK, LANES), BF16), pltpu.VMEM((MOBA_BLOCK, LANES), F32),
                        pltpu.VMEM((MOBA_BLOCK, LANES), F32)],
        compiler_params=_params("arbitrary", "arbitrary", "arbitrary"),
        name="moba",
    )(heads(qm), ka, _ones_column(heads(vm)), kmean)
    return o.transpose(0, 2, 1, 3).reshape(batch * seq, MOBA_W)


NSA_TQ = 128
NSA_TK = 1024
SEL_LANES = 64
SUPER_KEYS = SEL_LANES * SEL_BLOCK
CMP_WIDTH_STEP = 256


def _nsa_kernel(n_super, q_ref, kc_ref, vc_ref, ka_ref, va_ref, kvw_ref, gt_ref, e_ref, band_ref, o_ref,
                qa_ref, m_ref, acc_ref):
    qi = pl.program_id(2)
    tq = NSA_TQ
    s0 = qi * tq
    q = q_ref[0, 0, 0]

    def compressed(width):
        sc = _dot_nt(q, kc_ref[0, 0, 0:width, :])
        rq = lax.broadcasted_iota(jnp.int32, sc.shape, 0) & (tq - 1)
        n = lax.broadcasted_iota(jnp.int32, sc.shape, 1)
        vis = n * CMP_STRIDE + (CMP_BLOCK - 1) <= s0 + rq
        sc = jnp.where(vis, sc, NEG)
        pc = jnp.where(vis, jnp.exp(sc - jnp.max(sc, axis=-1, keepdims=True)), 0.0)
        pc = pc * (1.0 / jnp.maximum(jnp.sum(pc, axis=-1, keepdims=True), 1e-30))
        imp = pc[0:tq]
        for g in range(1, NSA_GROUP):
            imp = imp + pc[g * tq:(g + 1) * tq]
        return _dot(pc.astype(BF16), vc_ref[0, 0, 0:width, :]), _dot_x2(imp, band_ref[0:width, :])

    n16 = kc_ref.shape[2]
    widths = list(range(CMP_WIDTH_STEP, n16, CMP_WIDTH_STEP)) + [n16]
    o_c, pslc = lax.switch((s0 + tq - 1) // (CMP_WIDTH_STEP * CMP_STRIDE),
                           [functools.partial(compressed, wd) for wd in widths])

    wlen = WINDOW + tq

    def window(kvw, masked):
        sw = masked(_dot_nt(q, kvw[:, 0:HEAD_DIM]))
        pw = jnp.exp(sw - jnp.max(sw, axis=-1, keepdims=True))
        return _dot(pw.astype(BF16), kvw)[:, HEAD_DIM:2 * HEAD_DIM] * (1.0 / jnp.sum(pw, axis=-1, keepdims=True))

    def window_interior():
        def masked(sw):
            rq = lax.broadcasted_iota(jnp.int32, (sw.shape[0], tq), 0) & (tq - 1)
            c = lax.broadcasted_iota(jnp.int32, (sw.shape[0], tq), 1)
            return jnp.concatenate([jnp.where(c > rq, sw[:, 0:tq], NEG), sw[:, tq:WINDOW],
                                    jnp.where(c <= rq, sw[:, WINDOW:wlen], NEG)], axis=1)

        return window(kvw_ref[0, 0, pl.ds(pl.multiple_of(s0 - WINDOW, tq), wlen), :], masked)

    def window_start():
        def masked(sw):
            kabs = lax.broadcasted_iota(jnp.int32, sw.shape, 1)
            t = s0 + (lax.broadcasted_iota(jnp.int32, sw.shape, 0) & (tq - 1))
            return jnp.where((kabs <= t) & (kabs > t - WINDOW), sw, NEG)

        return window(kvw_ref[0, 0, 0:wlen, :], masked)

    o_w = lax.cond(s0 >= WINDOW, window_interior, window_start)

    pslc_t = pslc.T
    nb = pslc_t.shape[0]
    blk = lax.broadcasted_iota(jnp.int32, pslc_t.shape, 0)
    cur = (s0 + lax.broadcasted_iota(jnp.int32, pslc_t.shape, 1)) // SEL_BLOCK
    c0 = s0 // SEL_BLOCK
    elig = (blk >= 1) & (blk <= cur - 2)
    sel = _pick_top(jnp.where(elig, pslc_t, -jnp.inf), blk, nb, SEL_TOPK - 3)
    sel = sel | (((blk == 0) | (blk == cur - 1)) & (blk < c0))
    bias = jnp.where(sel, 0.0, NEG).T.astype(BF16)
    for st in range(n_super):
        b = bias[:, st * SEL_LANES:(st + 1) * SEL_LANES]
        qa_ref[st] = jnp.concatenate([q, jnp.concatenate([b] * NSA_GROUP, axis=0)], axis=1)

    _softmax_init(m_ref, acc_ref)

    def body(kt, carry):
        start = pl.multiple_of(kt * NSA_TK, NSA_TK)
        s = _dot_nt(qa_ref[kt // (SUPER_KEYS // NSA_TK)], ka_ref[0, 0, pl.ds(start, NSA_TK), :])
        _softmax_step(s, va_ref[0, 0, pl.ds(start, NSA_TK), :], m_ref, acc_ref)
        return carry

    lax.fori_loop(0, (s0 + NSA_TK - 1) // NSA_TK, body, 0)
    d0 = pl.multiple_of(s0, tq)
    s = _dot_nt(q, ka_ref[0, 0, pl.ds(d0, tq), :][:, 0:HEAD_DIM])
    rqd = lax.broadcasted_iota(jnp.int32, s.shape, 0) & (tq - 1)
    col = lax.broadcasted_iota(jnp.int32, s.shape, 1)
    _softmax_step(jnp.where(col <= rqd, s, NEG), va_ref[0, 0, pl.ds(d0, tq), :], m_ref, acc_ref)
    o_s = _softmax_result(acc_ref)

    w = NSA_GROUP * HEAD_DIM
    gexp = _dot_x2(gt_ref[...], e_ref[0])
    wide = lambda x: jnp.concatenate([x[g * tq:(g + 1) * tq] for g in range(NSA_GROUP)], axis=1)
    o_ref[...] = (gexp[:, 0:w] * wide(o_c) + gexp[:, w:2 * w] * wide(o_s) + gexp[:, 2 * w:3 * w] * wide(o_w)).astype(BF16)


def _nsa(qd, kcmp, vcmp, ks, vs, kw, vw, gates, batch, seq):
    tq = NSA_TQ
    nq = seq // tq
    nb = seq // SEL_BLOCK
    n16 = seq // CMP_STRIDE
    assert seq % SUPER_KEYS == 0
    n_super = seq // SUPER_KEYS
    rows = NSA_GROUP * tq
    heads = lambda x: x.reshape(batch, seq, NSA_KV_HEADS, HEAD_DIM).transpose(0, 2, 1, 3)
    qs = qd.reshape(batch, nq, tq, NSA_KV_HEADS, NSA_GROUP, HEAD_DIM).transpose(0, 3, 1, 4, 2, 5).reshape(
        batch, NSA_KV_HEADS, nq, rows, HEAD_DIM)
    onehot = jnp.asarray(np.eye(SEL_LANES)[(np.arange(seq) // SEL_BLOCK) % SEL_LANES], BF16)
    ka = jnp.concatenate([heads(ks), jnp.broadcast_to(onehot, (batch, NSA_KV_HEADS, seq, SEL_LANES))], axis=-1)
    kvw = jnp.concatenate([heads(kw), heads(vw)], axis=-1)
    e = np.zeros((NSA_KV_HEADS, 128, 3 * NSA_GROUP * HEAD_DIM), np.float32)
    for br in range(3):
        for hk in range(NSA_KV_HEADS):
            for g in range(NSA_GROUP):
                c = (br * NSA_GROUP + g) * HEAD_DIM
                e[hk, br * NSA_HEADS + hk * NSA_GROUP + g, c:c + HEAD_DIM] = 1.0
    nn, jj = np.arange(n16)[:, None], np.arange(nb)[None, :]
    band = ((nn >= 4 * jj - 1) & (nn <= 4 * jj + 3)).astype(np.float32)
    resident = lambda width: pl.BlockSpec((1, 1, seq, width), lambda b, h, i: (b, h, 0, 0))
    w = NSA_GROUP * HEAD_DIM
    return pl.pallas_call(
        functools.partial(_nsa_kernel, n_super),
        grid=(batch, NSA_KV_HEADS, nq),
        in_specs=[pl.BlockSpec((1, 1, 1, rows, HEAD_DIM), lambda b, h, i: (b, h, i, 0, 0)),
                  pl.BlockSpec((1, 1, n16, HEAD_DIM), lambda b, h, i: (b, h, 0, 0)),
                  pl.BlockSpec((1, 1, n16, HEAD_DIM), lambda b, h, i: (b, h, 0, 0)),
                  resident(LANES), resident(LANES), resident(LANES),
                  pl.BlockSpec((tq, 128), lambda b, h, i: (b * nq + i, 0)),
                  pl.BlockSpec((1, 128, 3 * w), lambda b, h, i: (h, 0, 0)),
                  pl.BlockSpec((n16, nb), lambda b, h, i: (0, 0))],
        out_specs=pl.BlockSpec((tq, w), lambda b, h, i: (b * nq + i, h)),
        out_shape=jax.ShapeDtypeStruct((batch * seq, NSA_W), BF16),
        scratch_shapes=[pltpu.VMEM((n_super, rows, LANES), BF16), pltpu.VMEM((rows, LANES), F32),
                        pltpu.VMEM((rows, LANES), F32)],
        compiler_params=_params("arbitrary", "arbitrary", "arbitrary"),
        name="nsa",
    )(qs, kcmp, vcmp, ka, _ones_column(heads(vs)), kvw, gates, jnp.asarray(e, BF16), jnp.asarray(band, BF16))


def _odd_out_kernel(om_ref, on_ref, h_ref, w_ref, o_ref):
    o_ref[...] = h_ref[...] + _dot(om_ref[...], w_ref[0:MOBA_W, :]) + _dot(on_ref[...], w_ref[MOBA_W:D_MODEL, :])


def _odd_out(o_moba, o_nsa, h2, w_out, tm=512):
    t = h2.shape[0]
    row = lambda w: pl.BlockSpec((tm, w), lambda i: (i, 0))
    return pl.pallas_call(
        _odd_out_kernel,
        grid=(t // tm,),
        in_specs=[row(MOBA_W), row(NSA_W), row(D_MODEL), _full((D_MODEL, D_MODEL))],
        out_specs=row(D_MODEL),
        out_shape=jax.ShapeDtypeStruct((t, D_MODEL), F32),
        compiler_params=_params("arbitrary"),
        name="odd_out",
    )(o_moba, o_nsa, h2, w_out)


def _even_layer(h2, batch, seq, norm, w_in, w_out, lam_re, lam_im, log_dt, b_re, b_im, c_re, c_im, d, w_glu, conv_w, conv_b):
    u, yb = _even_in(h2, norm.reshape(1, D_MODEL), w_in.astype(BF16), conv_w, conv_b.reshape(1, CONV_WIDTH), seq)
    ypre = _s5_mixer_pre(u, batch, seq, lam_re, lam_im, log_dt, b_re, b_im, c_re, c_im)
    return _even_out(ypre, u, yb, h2, d.reshape(1, S5_WIDTH), w_glu.astype(BF16), w_out.astype(BF16))


def _odd_layer(h2, batch, seq, norm, w_in, w_out, moba_q_norm, moba_k_norm, nsa_q_norm, nsa_kcmp_norm, nsa_ksel_norm,
               nsa_kwin_norm, cmp_pe_k, cmp_w1_k, cmp_w2_k, cmp_pe_v, cmp_w1_v, cmp_w2_v):
    qm, km, kmean, vm, qd, kc, vc, ks, vs, kw, vw, gates = _odd_in(
        h2, norm.reshape(1, D_MODEL), w_in, moba_q_norm, moba_k_norm, nsa_q_norm, nsa_ksel_norm, nsa_kwin_norm)
    cmp = _compress(kc, vc, batch, seq, cmp_pe_k, cmp_w1_k, cmp_w2_k, cmp_pe_v, cmp_w1_v, cmp_w2_v, nsa_kcmp_norm)
    o_moba = _moba(qm, km, kmean, vm, batch, seq)
    o_nsa = _nsa(qd, cmp[0], cmp[1], ks, vs, kw, vw, gates, batch, seq)
    return _odd_out(o_moba, o_nsa, h2, w_out.astype(BF16))


def kernel(x, ev_norm_mix, ev_w_in, ev_w_out, s5_lam_re, s5_lam_im, s5_log_dt, s5_b_re, s5_b_im, s5_c_re, s5_c_im, s5_d, s5_w_glu, conv_w, conv_b, od_norm_mix, od_w_in, od_w_out, moba_q_norm, moba_k_norm, nsa_q_norm, nsa_kcmp_norm, nsa_ksel_norm, nsa_kwin_norm, cmp_pe_k, cmp_w1_k, cmp_w2_k, cmp_pe_v, cmp_w1_v, cmp_w2_v, moe_norm, moe_w_group, moe_b_group, moe_w_expert, moe_b_expert, moe_w_gate, moe_w_up, moe_w_down):
    batch, seq, _ = x.shape
    depth = moe_norm.shape[0]
    h = x.reshape(batch * seq, D_MODEL)
    for layer in range(depth):
        i = layer // 2
        if layer % 2 == 0:
            h = _even_layer(h, batch, seq, ev_norm_mix[i], ev_w_in[i], ev_w_out[i], s5_lam_re[i], s5_lam_im[i], s5_log_dt[i],
                            s5_b_re[i], s5_b_im[i], s5_c_re[i], s5_c_im[i], s5_d[i], s5_w_glu[i], conv_w[i], conv_b[i])
        else:
            h = _odd_layer(h, batch, seq, od_norm_mix[i], od_w_in[i], od_w_out[i], moba_q_norm[i], moba_k_norm[i],
                           nsa_q_norm[i], nsa_kcmp_norm[i], nsa_ksel_norm[i], nsa_kwin_norm[i], cmp_pe_k[i], cmp_w1_k[i],
                           cmp_w2_k[i], cmp_pe_v[i], cmp_w1_v[i], cmp_w2_v[i])
        h = _moe(h, moe_norm[layer].reshape(1, D_MODEL), moe_w_group[layer], moe_b_group[layer], moe_w_expert[layer],
                 moe_b_expert[layer], moe_w_gate[layer], moe_w_up[layer], moe_w_down[layer])
    return h.reshape(batch, seq, D_MODEL)
```

```python
import functools
import math

import jax
import jax.numpy as jnp
import numpy as np
from jax import lax
from jax.experimental import pallas as pl
from jax.experimental.pallas import tpu as pltpu

D_MODEL = 1024
HEAD_DIM = 64
EPS = 1e-6
S5_WIDTH = 256
S5_GROUP = 16
S5_GROUPS = 16
S5_STATE = 64
S5_CHUNK = 16
CONV_WIDTH = 768
CONV_K = 3
MOBA_HEADS = 4
NSA_HEADS = 12
NSA_KV_HEADS = 2
NSA_GROUP = 6
MOBA_W = 256
NSA_W = 768
KV_W = 128
MOBA_BLOCK = 256
MOBA_TOPK = 3
CMP_BLOCK = 32
CMP_STRIDE = 16
CMP_HIDDEN = 256
SEL_BLOCK = 64
SEL_TOPK = 8
WINDOW = 512
N_GROUPS = 4
EXPERTS_PER_GROUP = 4
N_EXPERTS = 16
EXPERT_FF = 256

VMEM_LIMIT_BYTES = 56 * 1024 * 1024
NEG = -float(2 ** 30)
F32 = jnp.float32
BF16 = jnp.bfloat16


def _params(*semantics):
    return pltpu.CompilerParams(dimension_semantics=semantics, vmem_limit_bytes=VMEM_LIMIT_BYTES)


def _dot(a, b):
    return jnp.dot(a, b, preferred_element_type=F32)


def _dot_nt(a, b):
    return lax.dot_general(a, b, (((1,), (1,)), ((), ())), preferred_element_type=F32)


def _split(x):
    hi = x.astype(BF16)
    lo = (x - hi.astype(F32)).astype(BF16)
    return hi, lo


def _dot_x2(x, w):
    hi, lo = _split(x)
    return _dot(hi, w) + _dot(lo, w)


def _dot_x3(x, w_hi, w_lo):
    hi, lo = _split(x)
    return _dot(hi, w_hi) + (_dot(hi, w_lo) + _dot(lo, w_hi))


def _rms(x, gain):
    return x * lax.rsqrt(jnp.mean(x * x, axis=-1, keepdims=True) + EPS) * gain


def _gelu(x):
    return 0.5 * x * (1.0 + jnp.tanh(math.sqrt(2.0 / math.pi) * (x + 0.044715 * (x * x * x))))


def _sigmoid(x):
    return 1.0 / (1.0 + jnp.exp(-x))


def _full(shape):
    n = len(shape)
    return pl.BlockSpec(shape, lambda *_: (0,) * n)


def _even_in_kernel(tiles_per_seq, x_ref, g_ref, w_ref, cw_ref, cb_ref, u_ref, yb_ref, carry_ref):
    i = pl.program_id(0)
    xn = _rms(x_ref[...], g_ref[...]).astype(BF16)
    u_ref[...] = _dot(xn, w_ref[:, 0:S5_WIDTH])
    o = S5_WIDTH
    xc = _dot(xn, w_ref[:, o:o + CONV_WIDTH])
    gb = _dot(xn, w_ref[:, o + CONV_WIDTH:o + 2 * CONV_WIDTH])
    gc = _dot(xn, w_ref[:, o + 2 * CONV_WIDTH:o + 3 * CONV_WIDTH])
    z = gc * xc
    tm = z.shape[0]

    @pl.when(i % tiles_per_seq == 0)
    def _():
        carry_ref[...] = jnp.zeros_like(carry_ref)

    row = lax.broadcasted_iota(jnp.int32, z.shape, 0)
    prev1 = carry_ref[7:8, :]
    prev2 = carry_ref[6:7, :]
    z1 = jnp.where(row == 0, prev1, pltpu.roll(z, 1, 0))
    z2 = jnp.where(row == 0, prev2, jnp.where(row == 1, prev1, pltpu.roll(z, 2, 0)))
    y = cw_ref[0:1, :] * z2 + cw_ref[1:2, :] * z1 + cw_ref[2:3, :] * z + cb_ref[...]
    yb_ref[...] = (gb * y).astype(BF16)
    carry_ref[...] = z[tm - 8:tm, :]


def _even_in(x2, gain, w_in, conv_w, conv_b, seq, tm=512):
    t = x2.shape[0]
    n_in = w_in.shape[1]
    return pl.pallas_call(
        functools.partial(_even_in_kernel, seq // tm),
        grid=(t // tm,),
        in_specs=[pl.BlockSpec((tm, D_MODEL), lambda i: (i, 0)), _full((1, D_MODEL)),
                  _full((D_MODEL, n_in)), _full((CONV_K, CONV_WIDTH)), _full((1, CONV_WIDTH))],
        out_specs=[pl.BlockSpec((tm, S5_WIDTH), lambda i: (i, 0)),
                   pl.BlockSpec((tm, CONV_WIDTH), lambda i: (i, 0))],
        out_shape=[jax.ShapeDtypeStruct((t, S5_WIDTH), F32), jax.ShapeDtypeStruct((t, CONV_WIDTH), BF16)],
        scratch_shapes=[pltpu.VMEM((8, CONV_WIDTH), F32)],
        compiler_params=_params("arbitrary"),
        name="even_in",
    )(x2, gain, w_in, conv_w, conv_b)


def _s5_weights(lam_re, lam_im, log_dt, b_re, b_im, c_re, c_im):
    g, p, hg, ck = S5_GROUPS, S5_STATE, S5_GROUP, S5_CHUNK
    lr, li = lam_re.astype(F32), lam_im.astype(F32)
    dt = jnp.exp(log_dt.astype(F32))[:, None]
    mag = jnp.exp(lr * dt)
    a_re, a_im = mag * jnp.cos(li * dt), mag * jnp.sin(li * dt)
    den = lr * lr + li * li
    f_re = ((a_re - 1.0) * lr + a_im * li) / den
    f_im = (a_im * lr - (a_re - 1.0) * li) / den
    br, bi = b_re.astype(F32), b_im.astype(F32)
    bb_re = f_re[..., None] * br - f_im[..., None] * bi
    bb_im = f_re[..., None] * bi + f_im[..., None] * br
    pw_re, pw_im = [jnp.ones_like(a_re)], [jnp.zeros_like(a_im)]
    for _ in range(ck):
        r, m = pw_re[-1], pw_im[-1]
        pw_re.append(r * a_re - m * a_im)
        pw_im.append(r * a_im + m * a_re)
    pw_re, pw_im = jnp.stack(pw_re), jnp.stack(pw_im)
    cr, ci = c_re.astype(F32), c_im.astype(F32)
    rev_re, rev_im = pw_re[ck - 1::-1][:ck], pw_im[ck - 1::-1][:ck]
    ws_re = rev_re[:, :, :, None] * bb_re[None] - rev_im[:, :, :, None] * bb_im[None]
    ws_im = rev_re[:, :, :, None] * bb_im[None] + rev_im[:, :, :, None] * bb_re[None]
    ws_re = ws_re.transpose(1, 0, 3, 2).reshape(g, ck * hg, p)
    ws_im = ws_im.transpose(1, 0, 3, 2).reshape(g, ck * hg, p)
    ca_re = cr[None] * pw_re[1:, :, None, :] - ci[None] * pw_im[1:, :, None, :]
    ca_im = cr[None] * pw_im[1:, :, None, :] + ci[None] * pw_re[1:, :, None, :]
    wc_re = ca_re.transpose(1, 3, 0, 2).reshape(g, p, ck * hg)
    wc_im = (-ca_im).transpose(1, 3, 0, 2).reshape(g, p, ck * hg)
    cb_re = jnp.einsum('ghp,kgp,gpj->kghj', cr, pw_re[:ck], bb_re) - jnp.einsum('ghp,kgp,gpj->kghj', cr, pw_im[:ck], bb_im) \
        - jnp.einsum('ghp,kgp,gpj->kghj', ci, pw_re[:ck], bb_im) - jnp.einsum('ghp,kgp,gpj->kghj', ci, pw_im[:ck], bb_re)
    lag = np.arange(ck)[None, :] - np.arange(ck)[:, None]
    tz = cb_re[np.clip(lag, 0, ck - 1)]
    tz = jnp.where((lag >= 0)[:, :, None, None, None], tz, 0.0)
    tz = tz.transpose(2, 0, 4, 1, 3).reshape(g, ck * hg, ck * hg)

    def pair_blockdiag(m):
        m = m.reshape(g // 2, 2, m.shape[1], m.shape[2])
        z = jnp.zeros_like(m[:, 0])
        return jnp.concatenate([jnp.concatenate([m[:, 0], z], axis=2), jnp.concatenate([z, m[:, 1]], axis=2)], axis=1)

    ws = jnp.concatenate([pair_blockdiag(ws_re), pair_blockdiag(ws_im)], axis=2)
    wc = jnp.concatenate([pair_blockdiag(wc_re), pair_blockdiag(wc_im)], axis=1)
    tzp = pair_blockdiag(tz)
    a16_re = pw_re[ck].reshape(g // 2, 1, 2 * p)
    a16_im = pw_im[ck].reshape(g // 2, 1, 2 * p)
    return ws.astype(BF16), wc.astype(BF16), tzp.astype(BF16), a16_re, a16_im


def _s5_state_kernel(u_ref, ws_ref, s_ref):
    s_ref[0, 0] = _dot(u_ref[0, 0], ws_ref[0])


def _s5_scan_kernel(s_ref, are_ref, aim_ref, xprev_ref, st_ref):
    @pl.when(pl.program_id(0) == 0)
    def _():
        st_ref[...] = jnp.zeros_like(st_ref)

    a_re, a_im = are_ref[...], aim_ref[...]
    n = s_ref.shape[0]
    half = a_re.shape[1]

    def body(c, carry):
        xr, xi = carry
        xprev_ref[c, :, 0:half] = xr
        xprev_ref[c, :, half:2 * half] = xi
        s = s_ref[c]
        nr = a_re * xr - a_im * xi + s[:, 0:half]
        ni = a_re * xi + a_im * xr + s[:, half:2 * half]
        return nr, ni

    xr, xi = lax.fori_loop(0, n, body, (st_ref[:, 0:half], st_ref[:, half:2 * half]), unroll=8)
    st_ref[:, 0:half] = xr
    st_ref[:, half:2 * half] = xi


def _s5_out_kernel(u_ref, xp_ref, tz_ref, wc_ref, y_ref):
    y_ref[0, 0] = _dot(u_ref[0, 0], tz_ref[0]) + _dot(xp_ref[0, 0].astype(BF16), wc_ref[0])


def _s5_mixer_pre(u, batch, seq, lam_re, lam_im, log_dt, b_re, b_im, c_re, c_im):
    ws, wc, tz, a16_re, a16_im = _s5_weights(lam_re, lam_im, log_dt, b_re, b_im, c_re, c_im)
    ck, hg = S5_CHUNK, S5_GROUP
    nc = seq // ck
    npair = S5_GROUPS // 2
    pw = 2 * ck * hg
    sw = 4 * S5_STATE
    up = u.astype(BF16).reshape(batch, nc, ck, npair, 2, hg).transpose(0, 3, 1, 4, 2, 5).reshape(batch, npair, nc, pw)
    s = pl.pallas_call(
        _s5_state_kernel,
        grid=(batch, npair),
        in_specs=[pl.BlockSpec((1, 1, nc, pw), lambda b, p: (b, p, 0, 0)), pl.BlockSpec((1, pw, sw), lambda b, p: (p, 0, 0))],
        out_specs=pl.BlockSpec((1, 1, nc, sw), lambda b, p: (b, p, 0, 0)),
        out_shape=jax.ShapeDtypeStruct((batch, npair, nc, sw), F32),
        compiler_params=_params("arbitrary", "arbitrary"),
        name="s5_state",
    )(up, ws)
    rows = batch * npair
    s_t = s.transpose(2, 0, 1, 3).reshape(nc, rows, sw)
    are = jnp.broadcast_to(a16_re.reshape(1, npair, 2 * S5_STATE), (batch, npair, 2 * S5_STATE)).reshape(rows, 2 * S5_STATE)
    aim = jnp.broadcast_to(a16_im.reshape(1, npair, 2 * S5_STATE), (batch, npair, 2 * S5_STATE)).reshape(rows, 2 * S5_STATE)
    tc = min(nc, 256)
    xprev = pl.pallas_call(
        _s5_scan_kernel,
        grid=(nc // tc,),
        in_specs=[pl.BlockSpec((tc, rows, sw), lambda i: (i, 0, 0)), _full((rows, 2 * S5_STATE)), _full((rows, 2 * S5_STATE))],
        out_specs=pl.BlockSpec((tc, rows, sw), lambda i: (i, 0, 0)),
        out_shape=jax.ShapeDtypeStruct((nc, rows, sw), F32),
        scratch_shapes=[pltpu.VMEM((rows, sw), F32)],
        compiler_params=_params("arbitrary"),
        name="s5_scan",
    )(s_t, are, aim)
    xp = xprev.reshape(nc, batch, npair, sw).transpose(1, 2, 0, 3)
    y = pl.pallas_call(
        _s5_out_kernel,
        grid=(batch, npair),
        in_specs=[pl.BlockSpec((1, 1, nc, pw), lambda b, p: (b, p, 0, 0)), pl.BlockSpec((1, 1, nc, sw), lambda b, p: (b, p, 0, 0)),
                  pl.BlockSpec((1, pw, pw), lambda b, p: (p, 0, 0)), pl.BlockSpec((1, sw, pw), lambda b, p: (p, 0, 0))],
        out_specs=pl.BlockSpec((1, 1, nc, pw), lambda b, p: (b, p, 0, 0)),
        out_shape=jax.ShapeDtypeStruct((batch, npair, nc, pw), F32),
        compiler_params=_params("arbitrary", "arbitrary"),
        name="s5_out",
    )(up, xp, tz, wc)
    return y.reshape(batch, npair, nc, 2, ck, hg).transpose(0, 2, 4, 1, 3, 5).reshape(batch * seq, S5_WIDTH)


def _even_out_kernel(ypre_ref, u_ref, yb_ref, x_ref, d_ref, wglu_ref, wout_ref, o_ref):
    y = _gelu(ypre_ref[...] + d_ref[...] * u_ref[...])
    y = y * _sigmoid(_dot(y.astype(BF16), wglu_ref[...]))
    o_ref[...] = (x_ref[...] + _dot(y.astype(BF16), wout_ref[0:S5_WIDTH, :])
                  + _dot(yb_ref[...], wout_ref[S5_WIDTH:D_MODEL, :]))


def _even_out(ypre, u, yb, x2, d, w_glu, w_out, tm=512):
    t = x2.shape[0]
    row = lambda w: pl.BlockSpec((tm, w), lambda i: (i, 0))
    return pl.pallas_call(
        _even_out_kernel,
        grid=(t // tm,),
        in_specs=[row(S5_WIDTH), row(S5_WIDTH), row(CONV_WIDTH), row(D_MODEL), _full((1, S5_WIDTH)),
                  _full((S5_WIDTH, S5_WIDTH)), _full((D_MODEL, D_MODEL))],
        out_specs=row(D_MODEL),
        out_shape=jax.ShapeDtypeStruct((t, D_MODEL), F32),
        compiler_params=_params("arbitrary"),
        name="even_out",
    )(ypre, u, yb, x2, d, w_glu, w_out)


def _first_max(v, pos, width, axis=-1):
    m = jnp.max(v, axis=axis, keepdims=True)
    idx = jnp.min(jnp.where(v == m, pos, width), axis=axis, keepdims=True)
    return m, idx


def _moe_kernel(h_ref, g_ref, wr_hi_ref, wr_lo_ref, br_ref, wg_ref, wu_ref, wd_ref, o_ref, xn_ref, gate_ref):
    e = pl.program_id(1)

    @pl.when(e == 0)
    def _():
        h = h_ref[...]
        xn = _rms(h, g_ref[...])
        xn_ref[...] = xn.astype(BF16)
        o_ref[...] = h
        logits = _dot_x3(xn, wr_hi_ref[...], wr_lo_ref[...]) + br_ref[...]
        lane = lax.broadcasted_iota(jnp.int32, logits.shape, 1)
        width = logits.shape[1]
        is_g = lane < N_GROUPS
        gl = jnp.where(is_g, logits, -jnp.inf)
        gm, gi = _first_max(gl, lane, width)
        gw = 1.0 / jnp.sum(jnp.where(is_g, jnp.exp(gl - gm), 0.0), axis=-1, keepdims=True)
        lo = N_GROUPS + gi * EXPERTS_PER_GROUP
        in_grp = (lane >= lo) & (lane < lo + EXPERTS_PER_GROUP)
        el = jnp.where(in_grp, logits, -jnp.inf)
        m1, i1 = _first_max(el, lane, width)
        m2, i2 = _first_max(jnp.where(lane == i1, -jnp.inf, el), lane, width)
        p2 = jnp.exp(m2 - m1)
        w1 = gw / (1.0 + p2)
        w2 = gw * p2 / (1.0 + p2)
        gate_ref[...] = jnp.where(lane == i1, w1, 0.0) + jnp.where(lane == i2, w2, 0.0)

    xn = xn_ref[...]
    lane = lax.broadcasted_iota(jnp.int32, gate_ref.shape, 1)
    gate = gate_ref[...]
    y = None
    for j in range(EXPERTS_PER_GROUP):
        ge = jnp.sum(jnp.where(lane == e * EXPERTS_PER_GROUP + (j + N_GROUPS), gate, 0.0), axis=-1, keepdims=True)
        h1 = _dot(xn, wg_ref[j])
        h3 = _dot(xn, wu_ref[j])
        act = (h1 * _sigmoid(h1)) * h3 * ge
        yj = _dot(act.astype(BF16), wd_ref[j])
        y = yj if y is None else y + yj
    o_ref[...] += y


def _moe(h2, gain, w_group, b_group, w_expert, b_expert, w_gate, w_up, w_down, tm=1024):
    t = h2.shape[0]
    rw = 128
    wr = jnp.zeros((D_MODEL, rw), F32).at[:, 0:N_GROUPS].set(w_group).at[:, N_GROUPS:N_GROUPS + N_EXPERTS].set(w_expert)
    br = jnp.zeros((1, rw), F32).at[0, 0:N_GROUPS].set(b_group).at[0, N_GROUPS:N_GROUPS + N_EXPERTS].set(b_expert)
    wr_hi = wr.astype(BF16)
    wr_lo = (wr - wr_hi.astype(F32)).astype(BF16)
    return pl.pallas_call(
        _moe_kernel,
        grid=(t // tm, N_GROUPS),
        in_specs=[pl.BlockSpec((tm, D_MODEL), lambda i, e: (i, 0)), _full((1, D_MODEL)),
                  _full((D_MODEL, rw)), _full((D_MODEL, rw)), _full((1, rw)),
                  pl.BlockSpec((EXPERTS_PER_GROUP, D_MODEL, EXPERT_FF), lambda i, e: (e, 0, 0)),
                  pl.BlockSpec((EXPERTS_PER_GROUP, D_MODEL, EXPERT_FF), lambda i, e: (e, 0, 0)),
                  pl.BlockSpec((EXPERTS_PER_GROUP, EXPERT_FF, D_MODEL), lambda i, e: (e, 0, 0))],
        out_specs=pl.BlockSpec((tm, D_MODEL), lambda i, e: (i, 0)),
        out_shape=jax.ShapeDtypeStruct((t, D_MODEL), F32),
        scratch_shapes=[pltpu.VMEM((tm, D_MODEL), BF16), pltpu.VMEM((tm, rw), F32)],
        compiler_params=_params("arbitrary", "arbitrary"),
        name="moe",
    )(h2, gain, wr_hi, wr_lo, br, w_gate.astype(BF16), w_up.astype(BF16), w_down.astype(BF16))


ODD_SPLITS = (MOBA_W, MOBA_W, MOBA_W, NSA_W, KV_W, KV_W, KV_W, KV_W, KV_W, KV_W, 128)
ODD_IN_PAD = sum(ODD_SPLITS)


def _head_rms(x, hsum, gain):
    w = x.shape[1]
    ss = jnp.concatenate([_dot_x2(x[:, o:o + hsum.shape[0]] * x[:, o:o + hsum.shape[0]], hsum)
                          for o in range(0, w, hsum.shape[0])], axis=1) if w > hsum.shape[0] else _dot_x2(x * x, hsum)
    return x * lax.rsqrt(ss * (1.0 / HEAD_DIM) + EPS) * gain


def _odd_in_kernel(tiles_per_seq, x_ref, g_ref, w_ref, hsum_ref, gq_ref, gk_ref, gnq_ref, gks_ref, gkw_ref,
                   qm_ref, kam_ref, kmean_ref, vam_ref, qs_ref, kc_ref, vc_ref, kas_ref, vas_ref, kvw_ref, gt_ref):
    xn = _rms(x_ref[...], g_ref[...]).astype(BF16)
    offs = np.cumsum((0,) + ODD_SPLITS)
    col = lambda j: _dot(xn, w_ref[:, int(offs[j]):int(offs[j + 1])])
    head = lambda x, h: x[:, h * HEAD_DIM:(h + 1) * HEAD_DIM]
    hsum = hsum_ref[...]
    hsum128 = hsum_ref[0:128, 0:128]
    tm = x_ref.shape[0]
    pos = (pl.program_id(0) % tiles_per_seq) * tm + lax.broadcasted_iota(jnp.int32, (tm, HEAD_DIM), 0)
    lane = lax.broadcasted_iota(jnp.int32, (tm, HEAD_DIM), 1)
    ones_col = (lane == 0).astype(BF16)

    qm = _head_rms(col(0), hsum, gq_ref[...])
    km = _head_rms(col(1), hsum, gk_ref[...])
    for j in range(tm // MOBA_BLOCK):
        kmean_ref[0, j:j + 1, :] = jnp.mean(km[j * MOBA_BLOCK:(j + 1) * MOBA_BLOCK, :], axis=0, keepdims=True)
    km = km.astype(BF16)
    vm = col(2).astype(BF16)
    moba_id = (lane == pos // MOBA_BLOCK).astype(BF16)
    for h in range(MOBA_HEADS):
        qm_ref[0, h] = head(qm, h)
        kam_ref[0, h] = jnp.concatenate([head(km, h), moba_id], axis=1)
        vam_ref[0, h] = jnp.concatenate([head(vm, h), ones_col], axis=1)

    qd = (_head_rms(col(3), hsum, gnq_ref[...]) * (HEAD_DIM ** -0.5)).astype(BF16)
    for hk in range(NSA_KV_HEADS):
        for j in range(tm // NSA_TQ):
            for g in range(NSA_GROUP):
                qs_ref[0, hk, j, g * NSA_TQ:(g + 1) * NSA_TQ, :] = head(qd, hk * NSA_GROUP + g)[j * NSA_TQ:(j + 1) * NSA_TQ, :]
    kc_ref[...] = col(4).astype(BF16)
    vc_ref[...] = col(5).astype(BF16)
    ks = _head_rms(col(6), hsum128, gks_ref[...]).astype(BF16)
    vs = col(7).astype(BF16)
    kw = _head_rms(col(8), hsum128, gkw_ref[...]).astype(BF16)
    vw = col(9).astype(BF16)
    sel_id = (lane == (pos // SEL_BLOCK) % SEL_LANES).astype(BF16)
    for hk in range(NSA_KV_HEADS):
        kas_ref[0, hk] = jnp.concatenate([head(ks, hk), sel_id], axis=1)
        vas_ref[0, hk] = jnp.concatenate([head(vs, hk), ones_col], axis=1)
        kvw_ref[0, hk] = jnp.concatenate([head(kw, hk), head(vw, hk)], axis=1)
    gt_ref[...] = _sigmoid(col(10))


def _odd_in(h2, batch, seq, gain, w_in, moba_q_norm, moba_k_norm, nsa_q_norm, nsa_ksel_norm, nsa_kwin_norm, tm=512):
    t = h2.shape[0]
    assert MOBA_LANES == HEAD_DIM and SEL_LANES == HEAD_DIM and seq % tm == 0 and tm % MOBA_BLOCK == 0
    tps = seq // tm
    w = jnp.pad(w_in, ((0, 0), (0, ODD_IN_PAD - w_in.shape[1]))).astype(BF16)
    hsum = jnp.asarray(np.kron(np.eye(MOBA_W // HEAD_DIM), np.ones((HEAD_DIM, HEAD_DIM))), BF16)
    tile = lambda g, width: jnp.tile(g.astype(F32), width // HEAD_DIM).reshape(1, width)
    row = lambda width: pl.BlockSpec((tm, width), lambda i: (i, 0))
    heads = lambda n, width: pl.BlockSpec((1, n, tm, width), lambda i: (i // tps, 0, i % tps, 0))
    nmb = tm // MOBA_BLOCK
    nqt = tm // NSA_TQ
    rows = NSA_GROUP * NSA_TQ
    sds = jax.ShapeDtypeStruct
    out_specs = [heads(MOBA_HEADS, HEAD_DIM), heads(MOBA_HEADS, LANES), pl.BlockSpec((1, nmb, MOBA_W), lambda i: (i, 0, 0)),
                 heads(MOBA_HEADS, LANES),
                 pl.BlockSpec((1, NSA_KV_HEADS, nqt, rows, HEAD_DIM), lambda i: (i // tps, 0, i % tps, 0, 0)),
                 row(KV_W), row(KV_W), heads(NSA_KV_HEADS, LANES), heads(NSA_KV_HEADS, LANES), heads(NSA_KV_HEADS, LANES),
                 row(128)]
    out_shape = [sds((batch, MOBA_HEADS, seq, HEAD_DIM), F32), sds((batch, MOBA_HEADS, seq, LANES), BF16),
                 sds((t // tm, nmb, MOBA_W), F32), sds((batch, MOBA_HEADS, seq, LANES), BF16),
                 sds((batch, NSA_KV_HEADS, seq // NSA_TQ, rows, HEAD_DIM), BF16),
                 sds((t, KV_W), BF16), sds((t, KV_W), BF16), sds((batch, NSA_KV_HEADS, seq, LANES), BF16),
                 sds((batch, NSA_KV_HEADS, seq, LANES), BF16), sds((batch, NSA_KV_HEADS, seq, LANES), BF16),
                 sds((t, 128), F32)]
    return pl.pallas_call(
        functools.partial(_odd_in_kernel, tps),
        grid=(t // tm,),
        in_specs=[row(D_MODEL), _full((1, D_MODEL)), _full((D_MODEL, ODD_IN_PAD)), _full((MOBA_W, MOBA_W)),
                  _full((1, MOBA_W)), _full((1, MOBA_W)), _full((1, NSA_W)), _full((1, KV_W)), _full((1, KV_W))],
        out_specs=out_specs,
        out_shape=out_shape,
        compiler_params=_params("arbitrary"),
        name="odd_in",
    )(h2, gain, w, hsum, tile(moba_q_norm, MOBA_W), tile(moba_k_norm, MOBA_W), tile(nsa_q_norm, NSA_W),
      tile(nsa_ksel_norm, KV_W), tile(nsa_kwin_norm, KV_W))


def _compress_kernel(c_ref, w1_ref, w2_ref, pe_ref, g_ref, o_ref):
    kind = pl.program_id(0)
    c = c_ref[0, 0, 0]
    half = c.shape[1]
    n16 = c.shape[0]
    first = _dot(c, w1_ref[0, 0:half, :])
    second = _dot(c, w1_ref[0, half:2 * half, :])
    peb = _dot(pe_ref[0], w1_ref[0])[0:1, :]
    hid = _gelu(first + pltpu.roll(second, n16 - 1, 0) + peb)
    out = _dot(hid.astype(BF16), w2_ref[0])
    o_ref[0, 0, 0] = jnp.where(kind == 0, _rms(out, g_ref[...]), out).astype(BF16)


def _compress(kc, vc, batch, seq, pe_k, w1_k, w2_k, pe_v, w1_v, w2_v, kcmp_norm):
    n16 = seq // CMP_STRIDE
    half = CMP_STRIDE * HEAD_DIM

    def flat(x):
        return x.reshape(batch, n16, CMP_STRIDE, NSA_KV_HEADS, HEAD_DIM).transpose(0, 3, 1, 2, 4).reshape(
            batch, NSA_KV_HEADS, n16, half)

    c = jnp.stack([flat(kc), flat(vc)])
    w1 = jnp.stack([w1_k, w1_v]).astype(BF16)
    w2 = jnp.stack([w2_k, w2_v]).astype(BF16)
    pe = jnp.stack([pe_k, pe_v]).reshape(2, 1, 2 * half)
    pe = jnp.broadcast_to(pe, (2, 8, 2 * half)).astype(BF16)
    return pl.pallas_call(
        _compress_kernel,
        grid=(2, batch, NSA_KV_HEADS),
        in_specs=[pl.BlockSpec((1, 1, 1, n16, half), lambda k, b, h: (k, b, h, 0, 0)),
                  pl.BlockSpec((1, 2 * half, CMP_HIDDEN), lambda k, b, h: (k, 0, 0)),
                  pl.BlockSpec((1, CMP_HIDDEN, HEAD_DIM), lambda k, b, h: (k, 0, 0)),
                  pl.BlockSpec((1, 8, 2 * half), lambda k, b, h: (k, 0, 0)),
                  _full((1, HEAD_DIM))],
        out_specs=pl.BlockSpec((1, 1, 1, n16, HEAD_DIM), lambda k, b, h: (k, b, h, 0, 0)),
        out_shape=jax.ShapeDtypeStruct((2, batch, NSA_KV_HEADS, n16, HEAD_DIM), BF16),
        compiler_params=_params("arbitrary", "arbitrary", "arbitrary"),
        name="nsa_compress",
    )(c, w1, w2, pe, kcmp_norm.astype(F32).reshape(1, HEAD_DIM))


M_INIT = -1e30


LANES = 128


def _softmax_init(m_ref, acc_ref):
    m_ref[...] = jnp.full(m_ref.shape, M_INIT, F32)
    acc_ref[...] = jnp.zeros(acc_ref.shape, F32)


def _softmax_step(s, v_aug, m_ref, acc_ref):
    m_old = m_ref[...]
    m_new = jnp.maximum(m_old, jnp.max(s, axis=-1, keepdims=True))
    alpha = jnp.exp(m_old - m_new)
    p = jnp.exp(s - jnp.tile(m_new, (1, s.shape[1] // LANES)))
    acc_ref[...] = alpha * acc_ref[...] + _dot(p.astype(BF16), v_aug)
    m_ref[...] = m_new


def _past_keys_loop(n_keys, tile, step):
    n_full = n_keys // tile

    def body(j, carry):
        step(pl.multiple_of(j * tile, tile), tile)
        return carry

    lax.fori_loop(0, n_full, body, 0)
    rest = n_keys - n_full * tile
    start = pl.multiple_of(n_full * tile, tile)

    @pl.when(rest > tile // 2)
    def _():
        step(start, tile)

    @pl.when((rest > 0) & (rest <= tile // 2))
    def _():
        step(start, tile // 2)


def _softmax_result(acc_ref):
    acc = acc_ref[...]
    return acc[:, 0:HEAD_DIM] * (1.0 / acc[:, HEAD_DIM:HEAD_DIM + 1])


def _pick_top(score, pos, width, k):
    sel = jnp.zeros(score.shape, jnp.bool_)
    for _ in range(k):
        m, idx = _first_max(score, pos, width, axis=0)
        hit = (pos == idx) & (m > -jnp.inf)
        sel = sel | hit
        score = jnp.where(pos == idx, -jnp.inf, score)
    return sel


MOBA_LANES = 64
MOBA_TK = 2048


def _moba_kernel(q_ref, ka_ref, va_ref, kmean_ref, o_ref, qa_ref, m_ref, acc_ref):
    i = pl.program_id(2)
    q = q_ref[0, 0]
    q_hi, q_lo = _split(q)
    km_hi, km_lo = _split(kmean_ref[0, 0])
    gate = _dot_nt(km_hi, q_hi) + (_dot_nt(km_lo, q_hi) + _dot_nt(km_hi, q_lo))
    blk = lax.broadcasted_iota(jnp.int32, gate.shape, 0)
    sel = _pick_top(jnp.where(blk < i, gate, -jnp.inf), blk, gate.shape[0], MOBA_TOPK)
    qs = (q * (HEAD_DIM ** -0.5)).astype(BF16)
    bias = jnp.where(sel, 0.0, NEG).T[:, 0:MOBA_LANES]
    qa_ref[...] = jnp.concatenate([qs, bias.astype(BF16)], axis=1)
    _softmax_init(m_ref, acc_ref)

    def step(start, size):
        s = _dot_nt(qa_ref[...], ka_ref[0, 0, pl.ds(start, size), :])
        _softmax_step(s, va_ref[0, 0, pl.ds(start, size), :], m_ref, acc_ref)

    _past_keys_loop(i * MOBA_BLOCK, MOBA_TK, step)
    start = pl.multiple_of(i * MOBA_BLOCK, MOBA_BLOCK)
    s = _dot_nt(qs, ka_ref[0, 0, pl.ds(start, MOBA_BLOCK), :][:, 0:HEAD_DIM])
    qpos = lax.broadcasted_iota(jnp.int32, s.shape, 0)
    kpos = lax.broadcasted_iota(jnp.int32, s.shape, 1)
    _softmax_step(jnp.where(kpos <= qpos, s, NEG), va_ref[0, 0, pl.ds(start, MOBA_BLOCK), :], m_ref, acc_ref)
    o_ref[0, 0] = _softmax_result(acc_ref).astype(BF16)


def _moba(qm, ka, kmean, va, batch, seq):
    nmb = seq // MOBA_BLOCK
    assert nmb <= MOBA_LANES and seq % MOBA_TK == 0
    kmean = kmean.reshape(batch, nmb, MOBA_HEADS, HEAD_DIM).transpose(0, 2, 1, 3)
    kmean = jnp.pad(kmean, ((0, 0), (0, 0), (0, LANES - nmb), (0, 0)))
    return pl.pallas_call(
        _moba_kernel,
        grid=(batch, MOBA_HEADS, nmb),
        in_specs=[pl.BlockSpec((1, 1, MOBA_BLOCK, HEAD_DIM), lambda b, h, i: (b, h, i, 0)),
                  pl.BlockSpec((1, 1, seq, LANES), lambda b, h, i: (b, h, 0, 0)),
                  pl.BlockSpec((1, 1, seq, LANES), lambda b, h, i: (b, h, 0, 0)),
                  pl.BlockSpec((1, 1, LANES, HEAD_DIM), lambda b, h, i: (b, h, 0, 0))],
        out_specs=pl.BlockSpec((1, 1, MOBA_BLOCK, HEAD_DIM), lambda b, h, i: (b, h, i, 0)),
        out_shape=jax.ShapeDtypeStruct((batch, MOBA_HEADS, seq, HEAD_DIM), BF16),
        scratch_shapes=[pltpu.VMEM((MOBA_BLOCK, LANES), BF16), pltpu.VMEM((MOBA_BLOCK, LANES), F32),
                        pltpu.VMEM((MOBA_BLOCK, LANES), F32)],
        compiler_params=_params("arbitrary", "arbitrary", "arbitrary"),
        name="moba",
    )(qm, ka, va, kmean)


NSA_TQ = 128
NSA_TK = 2048
SEL_LANES = 64
SUPER_KEYS = SEL_LANES * SEL_BLOCK
CMP_WIDTH_STEP = 256


def _nsa_kernel(n_super, q_ref, kc_ref, vc_ref, ka_ref, va_ref, kvw_ref, gt_ref, e_ref, band_ref, o_ref,
                qa_ref, m_ref, acc_ref):
    qi = pl.program_id(2)
    tq = NSA_TQ
    s0 = qi * tq
    q = q_ref[0, 0, 0]

    def compressed(width):
        sc = _dot_nt(q, kc_ref[0, 0, 0:width, :])
        rq = lax.broadcasted_iota(jnp.int32, sc.shape, 0) & (tq - 1)
        n = lax.broadcasted_iota(jnp.int32, sc.shape, 1)
        vis = n * CMP_STRIDE + (CMP_BLOCK - 1) <= s0 + rq
        sc = jnp.where(vis, sc, NEG)
        pc = jnp.where(vis, jnp.exp(sc - jnp.max(sc, axis=-1, keepdims=True)), 0.0)
        pc = pc * (1.0 / jnp.maximum(jnp.sum(pc, axis=-1, keepdims=True), 1e-30))
        imp = pc[0:tq]
        for g in range(1, NSA_GROUP):
            imp = imp + pc[g * tq:(g + 1) * tq]
        return _dot(pc.astype(BF16), vc_ref[0, 0, 0:width, :]), _dot_x2(imp, band_ref[0:width, :])

    n16 = kc_ref.shape[2]
    widths = list(range(CMP_WIDTH_STEP, n16, CMP_WIDTH_STEP)) + [n16]
    o_c, pslc = lax.switch((s0 + tq - 1) // (CMP_WIDTH_STEP * CMP_STRIDE),
                           [functools.partial(compressed, wd) for wd in widths])

    wlen = WINDOW + tq

    def window(kvw, masked):
        sw = masked(_dot_nt(q, kvw[:, 0:HEAD_DIM]))
        pw = jnp.exp(sw - jnp.max(sw, axis=-1, keepdims=True))
        return _dot(pw.astype(BF16), kvw)[:, HEAD_DIM:2 * HEAD_DIM] * (1.0 / jnp.sum(pw, axis=-1, keepdims=True))

    def window_interior():
        def masked(sw):
            rq = lax.broadcasted_iota(jnp.int32, (sw.shape[0], tq), 0) & (tq - 1)
            c = lax.broadcasted_iota(jnp.int32, (sw.shape[0], tq), 1)
            return jnp.concatenate([jnp.where(c > rq, sw[:, 0:tq], NEG), sw[:, tq:WINDOW],
                                    jnp.where(c <= rq, sw[:, WINDOW:wlen], NEG)], axis=1)

        return window(kvw_ref[0, 0, pl.ds(pl.multiple_of(s0 - WINDOW, tq), wlen), :], masked)

    def window_start():
        def masked(sw):
            kabs = lax.broadcasted_iota(jnp.int32, sw.shape, 1)
            t = s0 + (lax.broadcasted_iota(jnp.int32, sw.shape, 0) & (tq - 1))
            return jnp.where((kabs <= t) & (kabs > t - WINDOW), sw, NEG)

        return window(kvw_ref[0, 0, 0:wlen, :], masked)

    o_w = lax.cond(s0 >= WINDOW, window_interior, window_start)

    pslc_t = pslc.T
    nb = pslc_t.shape[0]
    blk = lax.broadcasted_iota(jnp.int32, pslc_t.shape, 0)
    cur = (s0 + lax.broadcasted_iota(jnp.int32, pslc_t.shape, 1)) // SEL_BLOCK
    c0 = s0 // SEL_BLOCK
    elig = (blk >= 1) & (blk <= cur - 2)
    sel = _pick_top(jnp.where(elig, pslc_t, -jnp.inf), blk, nb, SEL_TOPK - 3)
    sel = sel | (((blk == 0) | (blk == cur - 1)) & (blk < c0))
    bias = jnp.where(sel, 0.0, NEG).T.astype(BF16)
    for st in range(n_super):
        b = bias[:, st * SEL_LANES:(st + 1) * SEL_LANES]
        qa_ref[st] = jnp.concatenate([q, jnp.concatenate([b] * NSA_GROUP, axis=0)], axis=1)

    _softmax_init(m_ref, acc_ref)

    def step(start, size):
        s = _dot_nt(qa_ref[start // SUPER_KEYS], ka_ref[0, 0, pl.ds(start, size), :])
        _softmax_step(s, va_ref[0, 0, pl.ds(start, size), :], m_ref, acc_ref)

    _past_keys_loop(s0, NSA_TK, step)
    d0 = pl.multiple_of(s0, tq)
    s = _dot_nt(q, ka_ref[0, 0, pl.ds(d0, tq), :][:, 0:HEAD_DIM])
    rqd = lax.broadcasted_iota(jnp.int32, s.shape, 0) & (tq - 1)
    col = lax.broadcasted_iota(jnp.int32, s.shape, 1)
    _softmax_step(jnp.where(col <= rqd, s, NEG), va_ref[0, 0, pl.ds(d0, tq), :], m_ref, acc_ref)
    o_s = _softmax_result(acc_ref)

    w = NSA_GROUP * HEAD_DIM
    gexp = _dot_x2(gt_ref[...], e_ref[0])
    wide = lambda x: jnp.concatenate([x[g * tq:(g + 1) * tq] for g in range(NSA_GROUP)], axis=1)
    o_ref[...] = (gexp[:, 0:w] * wide(o_c) + gexp[:, w:2 * w] * wide(o_s) + gexp[:, 2 * w:3 * w] * wide(o_w)).astype(BF16)


def _nsa(qs, kcmp, vcmp, ka, va, kvw, gates, batch, seq):
    tq = NSA_TQ
    nq = seq // tq
    nb = seq // SEL_BLOCK
    n16 = seq // CMP_STRIDE
    assert seq % SUPER_KEYS == 0
    n_super = seq // SUPER_KEYS
    rows = NSA_GROUP * tq
    e = np.zeros((NSA_KV_HEADS, 128, 3 * NSA_GROUP * HEAD_DIM), np.float32)
    for br in range(3):
        for hk in range(NSA_KV_HEADS):
            for g in range(NSA_GROUP):
                c = (br * NSA_GROUP + g) * HEAD_DIM
                e[hk, br * NSA_HEADS + hk * NSA_GROUP + g, c:c + HEAD_DIM] = 1.0
    nn, jj = np.arange(n16)[:, None], np.arange(nb)[None, :]
    band = ((nn >= 4 * jj - 1) & (nn <= 4 * jj + 3)).astype(np.float32)
    resident = lambda width: pl.BlockSpec((1, 1, seq, width), lambda b, h, i: (b, h, 0, 0))
    w = NSA_GROUP * HEAD_DIM
    return pl.pallas_call(
        functools.partial(_nsa_kernel, n_super),
        grid=(batch, NSA_KV_HEADS, nq),
        in_specs=[pl.BlockSpec((1, 1, 1, rows, HEAD_DIM), lambda b, h, i: (b, h, i, 0, 0)),
                  pl.BlockSpec((1, 1, n16, HEAD_DIM), lambda b, h, i: (b, h, 0, 0)),
                  pl.BlockSpec((1, 1, n16, HEAD_DIM), lambda b, h, i: (b, h, 0, 0)),
                  resident(LANES), resident(LANES), resident(LANES),
                  pl.BlockSpec((tq, 128), lambda b, h, i: (b * nq + i, 0)),
                  pl.BlockSpec((1, 128, 3 * w), lambda b, h, i: (h, 0, 0)),
                  pl.BlockSpec((n16, nb), lambda b, h, i: (0, 0))],
        out_specs=pl.BlockSpec((tq, w), lambda b, h, i: (b * nq + i, h)),
        out_shape=jax.ShapeDtypeStruct((batch * seq, NSA_W), BF16),
        scratch_shapes=[pltpu.VMEM((n_super, rows, LANES), BF16), pltpu.VMEM((rows, LANES), F32),
                        pltpu.VMEM((rows, LANES), F32)],
        compiler_params=_params("arbitrary", "arbitrary", "arbitrary"),
        name="nsa",
    )(qs, kcmp, vcmp, ka, va, kvw, gates, jnp.asarray(e, BF16), jnp.asarray(band, BF16))


def _odd_out_kernel(om_ref, on_ref, h_ref, w_ref, o_ref):
    acc = h_ref[...] + _dot(on_ref[...], w_ref[MOBA_W:D_MODEL, :])
    for h in range(MOBA_HEADS):
        acc = acc + _dot(om_ref[0, h], w_ref[h * HEAD_DIM:(h + 1) * HEAD_DIM, :])
    o_ref[...] = acc


def _odd_out(o_moba, o_nsa, h2, w_out, seq, tm=512):
    t = h2.shape[0]
    tps = seq // tm
    row = lambda w: pl.BlockSpec((tm, w), lambda i: (i, 0))
    return pl.pallas_call(
        _odd_out_kernel,
        grid=(t // tm,),
        in_specs=[pl.BlockSpec((1, MOBA_HEADS, tm, HEAD_DIM), lambda i: (i // tps, 0, i % tps, 0)),
                  row(NSA_W), row(D_MODEL), _full((D_MODEL, D_MODEL))],
        out_specs=row(D_MODEL),
        out_shape=jax.ShapeDtypeStruct((t, D_MODEL), F32),
        compiler_params=_params("arbitrary"),
        name="odd_out",
    )(o_moba, o_nsa, h2, w_out)


def _even_layer(h2, batch, seq, norm, w_in, w_out, lam_re, lam_im, log_dt, b_re, b_im, c_re, c_im, d, w_glu, conv_w, conv_b):
    u, yb = _even_in(h2, norm.reshape(1, D_MODEL), w_in.astype(BF16), conv_w, conv_b.reshape(1, CONV_WIDTH), seq)
    ypre = _s5_mixer_pre(u, batch, seq, lam_re, lam_im, log_dt, b_re, b_im, c_re, c_im)
    return _even_out(ypre, u, yb, h2, d.reshape(1, S5_WIDTH), w_glu.astype(BF16), w_out.astype(BF16))


def _odd_layer(h2, batch, seq, norm, w_in, w_out, moba_q_norm, moba_k_norm, nsa_q_norm, nsa_kcmp_norm, nsa_ksel_norm,
               nsa_kwin_norm, cmp_pe_k, cmp_w1_k, cmp_w2_k, cmp_pe_v, cmp_w1_v, cmp_w2_v):
    qm, kam, kmean, vam, qs, kc, vc, kas, vas, kvw, gates = _odd_in(
        h2, batch, seq, norm.reshape(1, D_MODEL), w_in, moba_q_norm, moba_k_norm, nsa_q_norm, nsa_ksel_norm, nsa_kwin_norm)
    cmp = _compress(kc, vc, batch, seq, cmp_pe_k, cmp_w1_k, cmp_w2_k, cmp_pe_v, cmp_w1_v, cmp_w2_v, nsa_kcmp_norm)
    o_moba = _moba(qm, kam, kmean, vam, batch, seq)
    o_nsa = _nsa(qs, cmp[0], cmp[1], kas, vas, kvw, gates, batch, seq)
    return _odd_out(o_moba, o_nsa, h2, w_out.astype(BF16), seq)


def kernel(x, ev_norm_mix, ev_w_in, ev_w_out, s5_lam_re, s5_lam_im, s5_log_dt, s5_b_re, s5_b_im, s5_c_re, s5_c_im, s5_d, s5_w_glu, conv_w, conv_b, od_norm_mix, od_w_in, od_w_out, moba_q_norm, moba_k_norm, nsa_q_norm, nsa_kcmp_norm, nsa_ksel_norm, nsa_kwin_norm, cmp_pe_k, cmp_w1_k, cmp_w2_k, cmp_pe_v, cmp_w1_v, cmp_w2_v, moe_norm, moe_w_group, moe_b_group, moe_w_expert, moe_b_expert, moe_w_gate, moe_w_up, moe_w_down):
    batch, seq, _ = x.shape
    depth = moe_norm.shape[0]
    h = x.reshape(batch * seq, D_MODEL)
    for layer in range(depth):
        i = layer // 2
        if layer % 2 == 0:
            h = _even_layer(h, batch, seq, ev_norm_mix[i], ev_w_in[i], ev_w_out[i], s5_lam_re[i], s5_lam_im[i], s5_log_dt[i],
                            s5_b_re[i], s5_b_im[i], s5_c_re[i], s5_c_im[i], s5_d[i], s5_w_glu[i], conv_w[i], conv_b[i])
        else:
            h = _odd_layer(h, batch, seq, od_norm_mix[i], od_w_in[i], od_w_out[i], moba_q_norm[i], moba_k_norm[i],
                           nsa_q_norm[i], nsa_kcmp_norm[i], nsa_ksel_norm[i], nsa_kwin_norm[i], cmp_pe_k[i], cmp_w1_k[i],
                           cmp_w2_k[i], cmp_pe_v[i], cmp_w1_v[i], cmp_w2_v[i])
        h = _moe(h, moe_norm[layer].reshape(1, D_MODEL), moe_w_group[layer], moe_b_group[layer], moe_w_expert[layer],
                 moe_b_expert[layer], moe_w_gate[layer], moe_w_up[layer], moe_w_down[layer])
    return h.reshape(batch, seq, D_MODEL)
```

```python
import functools
import math

import jax
import jax.numpy as jnp
import numpy as np
from jax import lax
from jax.experimental import pallas as pl
from jax.experimental.pallas import tpu as pltpu

D_MODEL = 1024
HEAD_DIM = 64
EPS = 1e-6
S5_WIDTH = 256
S5_GROUP = 16
S5_GROUPS = 16
S5_STATE = 64
S5_CHUNK = 16
CONV_WIDTH = 768
CONV_K = 3
MOBA_HEADS = 4
NSA_HEADS = 12
NSA_KV_HEADS = 2
NSA_GROUP = 6
MOBA_W = 256
NSA_W = 768
KV_W = 128
MOBA_BLOCK = 256
MOBA_TOPK = 3
CMP_BLOCK = 32
CMP_STRIDE = 16
CMP_HIDDEN = 256
SEL_BLOCK = 64
SEL_TOPK = 8
WINDOW = 512
N_GROUPS = 4
EXPERTS_PER_GROUP = 4
N_EXPERTS = 16
EXPERT_FF = 256

VMEM_LIMIT_BYTES = 56 * 1024 * 1024
NEG = -float(2 ** 30)
F32 = jnp.float32
BF16 = jnp.bfloat16


def _params(*semantics):
    return pltpu.CompilerParams(dimension_semantics=semantics, vmem_limit_bytes=VMEM_LIMIT_BYTES)


def _dot(a, b):
    return jnp.dot(a, b, preferred_element_type=F32)


def _dot_nt(a, b):
    return lax.dot_general(a, b, (((1,), (1,)), ((), ())), preferred_element_type=F32)


def _split(x):
    hi = x.astype(BF16)
    lo = (x - hi.astype(F32)).astype(BF16)
    return hi, lo


def _dot_x2(x, w):
    hi, lo = _split(x)
    return _dot(hi, w) + _dot(lo, w)


def _dot_x3(x, w_hi, w_lo):
    hi, lo = _split(x)
    return _dot(hi, w_hi) + (_dot(hi, w_lo) + _dot(lo, w_hi))


def _rms(x, gain):
    return x * lax.rsqrt(jnp.mean(x * x, axis=-1, keepdims=True) + EPS) * gain


def _gelu(x):
    return 0.5 * x * (1.0 + jnp.tanh(math.sqrt(2.0 / math.pi) * (x + 0.044715 * (x * x * x))))


def _sigmoid(x):
    return 1.0 / (1.0 + jnp.exp(-x))


def _full(shape):
    n = len(shape)
    return pl.BlockSpec(shape, lambda *_: (0,) * n)


def _even_in_kernel(tiles_per_seq, x_ref, g_ref, w_ref, cw_ref, cb_ref, u_ref, ub_ref, yb_ref, carry_ref):
    i = pl.program_id(0)
    xn = _rms(x_ref[...], g_ref[...]).astype(BF16)
    u = _dot(xn, w_ref[:, 0:S5_WIDTH])
    u_ref[...] = u
    ub_ref[...] = u.astype(BF16)
    o = S5_WIDTH
    xc = _dot(xn, w_ref[:, o:o + CONV_WIDTH])
    gb = _dot(xn, w_ref[:, o + CONV_WIDTH:o + 2 * CONV_WIDTH])
    gc = _dot(xn, w_ref[:, o + 2 * CONV_WIDTH:o + 3 * CONV_WIDTH])
    z = gc * xc
    tm = z.shape[0]

    @pl.when(i % tiles_per_seq == 0)
    def _():
        carry_ref[...] = jnp.zeros_like(carry_ref)

    row = lax.broadcasted_iota(jnp.int32, z.shape, 0)
    prev1 = carry_ref[7:8, :]
    prev2 = carry_ref[6:7, :]
    z1 = jnp.where(row == 0, prev1, pltpu.roll(z, 1, 0))
    z2 = jnp.where(row == 0, prev2, jnp.where(row == 1, prev1, pltpu.roll(z, 2, 0)))
    y = cw_ref[0:1, :] * z2 + cw_ref[1:2, :] * z1 + cw_ref[2:3, :] * z + cb_ref[...]
    yb_ref[...] = (gb * y).astype(BF16)
    carry_ref[...] = z[tm - 8:tm, :]


def _even_in(x2, gain, w_in, conv_w, conv_b, seq, tm=512):
    t = x2.shape[0]
    n_in = w_in.shape[1]
    return pl.pallas_call(
        functools.partial(_even_in_kernel, seq // tm),
        grid=(t // tm,),
        in_specs=[pl.BlockSpec((tm, D_MODEL), lambda i: (i, 0)), _full((1, D_MODEL)),
                  _full((D_MODEL, n_in)), _full((CONV_K, CONV_WIDTH)), _full((1, CONV_WIDTH))],
        out_specs=[pl.BlockSpec((tm, S5_WIDTH), lambda i: (i, 0)), pl.BlockSpec((tm, S5_WIDTH), lambda i: (i, 0)),
                   pl.BlockSpec((tm, CONV_WIDTH), lambda i: (i, 0))],
        out_shape=[jax.ShapeDtypeStruct((t, S5_WIDTH), F32), jax.ShapeDtypeStruct((t, S5_WIDTH), BF16),
                   jax.ShapeDtypeStruct((t, CONV_WIDTH), BF16)],
        scratch_shapes=[pltpu.VMEM((8, CONV_WIDTH), F32)],
        compiler_params=_params("arbitrary"),
        name="even_in",
    )(x2, gain, w_in, conv_w, conv_b)


def _s5_weights(lam_re, lam_im, log_dt, b_re, b_im, c_re, c_im):
    g, p, hg, ck = S5_GROUPS, S5_STATE, S5_GROUP, S5_CHUNK
    lr, li = lam_re.astype(F32), lam_im.astype(F32)
    dt = jnp.exp(log_dt.astype(F32))[:, None]
    mag = jnp.exp(lr * dt)
    a_re, a_im = mag * jnp.cos(li * dt), mag * jnp.sin(li * dt)
    den = lr * lr + li * li
    f_re = ((a_re - 1.0) * lr + a_im * li) / den
    f_im = (a_im * lr - (a_re - 1.0) * li) / den
    br, bi = b_re.astype(F32), b_im.astype(F32)
    bb_re = f_re[..., None] * br - f_im[..., None] * bi
    bb_im = f_re[..., None] * bi + f_im[..., None] * br
    pw_re, pw_im = [jnp.ones_like(a_re)], [jnp.zeros_like(a_im)]
    for _ in range(ck):
        r, m = pw_re[-1], pw_im[-1]
        pw_re.append(r * a_re - m * a_im)
        pw_im.append(r * a_im + m * a_re)
    pw_re, pw_im = jnp.stack(pw_re), jnp.stack(pw_im)
    cr, ci = c_re.astype(F32), c_im.astype(F32)
    rev_re, rev_im = pw_re[ck - 1::-1][:ck], pw_im[ck - 1::-1][:ck]
    ws_re = rev_re[:, :, :, None] * bb_re[None] - rev_im[:, :, :, None] * bb_im[None]
    ws_im = rev_re[:, :, :, None] * bb_im[None] + rev_im[:, :, :, None] * bb_re[None]
    ca_re = cr[None] * pw_re[1:, :, None, :] - ci[None] * pw_im[1:, :, None, :]
    ca_im = cr[None] * pw_im[1:, :, None, :] + ci[None] * pw_re[1:, :, None, :]
    cb_re = jnp.einsum('ghp,kgp,gpj->kghj', cr, pw_re[:ck], bb_re) - jnp.einsum('ghp,kgp,gpj->kghj', cr, pw_im[:ck], bb_im) \
        - jnp.einsum('ghp,kgp,gpj->kghj', ci, pw_re[:ck], bb_im) - jnp.einsum('ghp,kgp,gpj->kghj', ci, pw_im[:ck], bb_re)
    lag = np.arange(ck)[None, :] - np.arange(ck)[:, None]
    tz = cb_re[np.clip(lag, 0, ck - 1)]
    tz = jnp.where((lag >= 0)[:, :, None, None, None], tz, 0.0)
    eye = jnp.eye(g, dtype=F32)
    cw = ck * g * hg
    ws = jnp.concatenate([jnp.einsum('sgpa,gk->sgakp', ws_re, eye).reshape(cw, g * p),
                          jnp.einsum('sgpa,gk->sgakp', ws_im, eye).reshape(cw, g * p)], axis=1)
    wc = jnp.concatenate([jnp.einsum('tghp,gk->gptkh', ca_re, eye).reshape(g * p, cw),
                          jnp.einsum('tghp,gk->gptkh', -ca_im, eye).reshape(g * p, cw)], axis=0)
    tzf = jnp.einsum('stgba,gk->sgatkb', tz, eye).reshape(cw, cw)
    return ws.astype(BF16), wc.astype(BF16), tzf.astype(BF16), pw_re[ck].reshape(1, g * p), pw_im[ck].reshape(1, g * p)


def _s5_state_kernel(u_ref, ws_ref, s_ref):
    s_ref[...] = _dot(u_ref[...], ws_ref[...])


def _s5_scan_kernel(s_ref, are_ref, aim_ref, xprev_ref, st_ref):
    @pl.when(pl.program_id(0) == 0)
    def _():
        st_ref[...] = jnp.zeros_like(st_ref)

    a_re, a_im = are_ref[...], aim_ref[...]
    nb, n = s_ref.shape[0], s_ref.shape[1]
    half = a_re.shape[1]

    def body(c, carry):
        out = []
        for b in range(nb):
            xr, xi = carry[2 * b], carry[2 * b + 1]
            xprev_ref[b, pl.ds(c, 1), 0:half] = xr
            xprev_ref[b, pl.ds(c, 1), half:2 * half] = xi
            s = s_ref[b, pl.ds(c, 1), :]
            out += [a_re * xr - a_im * xi + s[:, 0:half], a_re * xi + a_im * xr + s[:, half:2 * half]]
        return tuple(out)

    init = tuple(st_ref[b:b + 1, o:o + half] for b in range(nb) for o in (0, half))
    final = lax.fori_loop(0, n, body, init, unroll=4)
    for b in range(nb):
        st_ref[b:b + 1, 0:half] = final[2 * b]
        st_ref[b:b + 1, half:2 * half] = final[2 * b + 1]


def _s5_out_kernel(u_ref, xp_ref, tz_ref, wc_ref, y_ref):
    y_ref[...] = _dot(u_ref[...], tz_ref[...]) + _dot(xp_ref[...].astype(BF16), wc_ref[...])


def _s5_mixer_pre(ub, batch, seq, lam_re, lam_im, log_dt, b_re, b_im, c_re, c_im):
    ws, wc, tz, a16_re, a16_im = _s5_weights(lam_re, lam_im, log_dt, b_re, b_im, c_re, c_im)
    nc = seq // S5_CHUNK
    rows = batch * nc
    cw = S5_CHUNK * S5_WIDTH
    sw = 2 * S5_GROUPS * S5_STATE
    tr = min(rows, 512)
    tn = 512
    uc = ub.reshape(rows, cw)
    s = pl.pallas_call(
        _s5_state_kernel,
        grid=(sw // tn, rows // tr),
        in_specs=[pl.BlockSpec((tr, cw), lambda j, i: (i, 0)), pl.BlockSpec((cw, tn), lambda j, i: (0, j))],
        out_specs=pl.BlockSpec((tr, tn), lambda j, i: (i, j)),
        out_shape=jax.ShapeDtypeStruct((rows, sw), F32),
        compiler_params=_params("arbitrary", "arbitrary"),
        name="s5_state",
    )(uc, ws)
    tc = min(nc, 256)
    xprev = pl.pallas_call(
        _s5_scan_kernel,
        grid=(nc // tc,),
        in_specs=[pl.BlockSpec((batch, tc, sw), lambda i: (0, i, 0)), _full((1, sw // 2)), _full((1, sw // 2))],
        out_specs=pl.BlockSpec((batch, tc, sw), lambda i: (0, i, 0)),
        out_shape=jax.ShapeDtypeStruct((batch, nc, sw), F32),
        scratch_shapes=[pltpu.VMEM((batch, sw), F32)],
        compiler_params=_params("arbitrary"),
        name="s5_scan",
    )(s.reshape(batch, nc, sw), a16_re, a16_im)
    y = pl.pallas_call(
        _s5_out_kernel,
        grid=(cw // tn, rows // tr),
        in_specs=[pl.BlockSpec((tr, cw), lambda j, i: (i, 0)), pl.BlockSpec((tr, sw), lambda j, i: (i, 0)),
                  pl.BlockSpec((cw, tn), lambda j, i: (0, j)), pl.BlockSpec((sw, tn), lambda j, i: (0, j))],
        out_specs=pl.BlockSpec((tr, tn), lambda j, i: (i, j)),
        out_shape=jax.ShapeDtypeStruct((rows, cw), F32),
        compiler_params=_params("arbitrary", "arbitrary"),
        name="s5_out",
    )(uc, xprev.reshape(rows, sw), tz, wc)
    return y.reshape(batch * seq, S5_WIDTH)


def _even_out_kernel(ypre_ref, u_ref, yb_ref, x_ref, d_ref, wglu_ref, wout_ref, o_ref):
    y = _gelu(ypre_ref[...] + d_ref[...] * u_ref[...])
    y = y * _sigmoid(_dot(y.astype(BF16), wglu_ref[...]))
    o_ref[...] = (x_ref[...] + _dot(y.astype(BF16), wout_ref[0:S5_WIDTH, :])
                  + _dot(yb_ref[...], wout_ref[S5_WIDTH:D_MODEL, :]))


def _even_out(ypre, u, yb, x2, d, w_glu, w_out, tm=512):
    t = x2.shape[0]
    row = lambda w: pl.BlockSpec((tm, w), lambda i: (i, 0))
    return pl.pallas_call(
        _even_out_kernel,
        grid=(t // tm,),
        in_specs=[row(S5_WIDTH), row(S5_WIDTH), row(CONV_WIDTH), row(D_MODEL), _full((1, S5_WIDTH)),
                  _full((S5_WIDTH, S5_WIDTH)), _full((D_MODEL, D_MODEL))],
        out_specs=row(D_MODEL),
        out_shape=jax.ShapeDtypeStruct((t, D_MODEL), F32),
        compiler_params=_params("arbitrary"),
        name="even_out",
    )(ypre, u, yb, x2, d, w_glu, w_out)


def _first_max(v, pos, width, axis=-1):
    m = jnp.max(v, axis=axis, keepdims=True)
    idx = jnp.min(jnp.where(v == m, pos, width), axis=axis, keepdims=True)
    return m, idx


def _moe_kernel(h_ref, g_ref, wr_hi_ref, wr_lo_ref, br_ref, wg_ref, wu_ref, wd_ref, o_ref, xn_ref, gate_ref):
    e = pl.program_id(1)

    @pl.when(e == 0)
    def _():
        h = h_ref[...]
        xn = _rms(h, g_ref[...])
        xn_ref[...] = xn.astype(BF16)
        o_ref[...] = h
        logits = _dot_x3(xn, wr_hi_ref[...], wr_lo_ref[...]) + br_ref[...]
        lane = lax.broadcasted_iota(jnp.int32, logits.shape, 1)
        width = logits.shape[1]
        is_g = lane < N_GROUPS
        gl = jnp.where(is_g, logits, -jnp.inf)
        gm, gi = _first_max(gl, lane, width)
        gw = 1.0 / jnp.sum(jnp.where(is_g, jnp.exp(gl - gm), 0.0), axis=-1, keepdims=True)
        lo = N_GROUPS + gi * EXPERTS_PER_GROUP
        in_grp = (lane >= lo) & (lane < lo + EXPERTS_PER_GROUP)
        el = jnp.where(in_grp, logits, -jnp.inf)
        m1, i1 = _first_max(el, lane, width)
        m2, i2 = _first_max(jnp.where(lane == i1, -jnp.inf, el), lane, width)
        p2 = jnp.exp(m2 - m1)
        w1 = gw / (1.0 + p2)
        w2 = gw * p2 / (1.0 + p2)
        gate_ref[...] = jnp.where(lane == i1, w1, 0.0) + jnp.where(lane == i2, w2, 0.0)

    xn = xn_ref[...]
    lane = lax.broadcasted_iota(jnp.int32, gate_ref.shape, 1)
    gate = gate_ref[...]
    y = None
    for j in range(EXPERTS_PER_GROUP):
        ge = jnp.sum(jnp.where(lane == e * EXPERTS_PER_GROUP + (j + N_GROUPS), gate, 0.0), axis=-1, keepdims=True)
        h1 = _dot(xn, wg_ref[j])
        h3 = _dot(xn, wu_ref[j])
        act = (h1 * _sigmoid(h1)) * h3 * ge
        yj = _dot(act.astype(BF16), wd_ref[j])
        y = yj if y is None else y + yj
    o_ref[...] += y


def _moe(h2, gain, w_group, b_group, w_expert, b_expert, w_gate, w_up, w_down, tm=1024):
    t = h2.shape[0]
    rw = 128
    wr = jnp.zeros((D_MODEL, rw), F32).at[:, 0:N_GROUPS].set(w_group).at[:, N_GROUPS:N_GROUPS + N_EXPERTS].set(w_expert)
    br = jnp.zeros((1, rw), F32).at[0, 0:N_GROUPS].set(b_group).at[0, N_GROUPS:N_GROUPS + N_EXPERTS].set(b_expert)
    wr_hi = wr.astype(BF16)
    wr_lo = (wr - wr_hi.astype(F32)).astype(BF16)
    return pl.pallas_call(
        _moe_kernel,
        grid=(t // tm, N_GROUPS),
        in_specs=[pl.BlockSpec((tm, D_MODEL), lambda i, e: (i, 0)), _full((1, D_MODEL)),
                  _full((D_MODEL, rw)), _full((D_MODEL, rw)), _full((1, rw)),
                  pl.BlockSpec((EXPERTS_PER_GROUP, D_MODEL, EXPERT_FF), lambda i, e: (e, 0, 0)),
                  pl.BlockSpec((EXPERTS_PER_GROUP, D_MODEL, EXPERT_FF), lambda i, e: (e, 0, 0)),
                  pl.BlockSpec((EXPERTS_PER_GROUP, EXPERT_FF, D_MODEL), lambda i, e: (e, 0, 0))],
        out_specs=pl.BlockSpec((tm, D_MODEL), lambda i, e: (i, 0)),
        out_shape=jax.ShapeDtypeStruct((t, D_MODEL), F32),
        scratch_shapes=[pltpu.VMEM((tm, D_MODEL), BF16), pltpu.VMEM((tm, rw), F32)],
        compiler_params=_params("arbitrary", "arbitrary"),
        name="moe",
    )(h2, gain, wr_hi, wr_lo, br, w_gate.astype(BF16), w_up.astype(BF16), w_down.astype(BF16))


ODD_SPLITS = (MOBA_W, MOBA_W, MOBA_W, NSA_W, KV_W, KV_W, KV_W, KV_W, KV_W, KV_W, 128)
ODD_IN_PAD = sum(ODD_SPLITS)


def _head_rms(x, hsum, gain):
    w = x.shape[1]
    ss = jnp.concatenate([_dot_x2(x[:, o:o + hsum.shape[0]] * x[:, o:o + hsum.shape[0]], hsum)
                          for o in range(0, w, hsum.shape[0])], axis=1) if w > hsum.shape[0] else _dot_x2(x * x, hsum)
    return x * lax.rsqrt(ss * (1.0 / HEAD_DIM) + EPS) * gain


def _odd_in_kernel(tiles_per_seq, x_ref, g_ref, w_ref, hsum_ref, gq_ref, gk_ref, gnq_ref, gks_ref, gkw_ref,
                   qm_ref, kam_ref, kmean_ref, vam_ref, qs_ref, kc_ref, vc_ref, kas_ref, vas_ref, kvw_ref, gt_ref):
    xn = _rms(x_ref[...], g_ref[...]).astype(BF16)
    offs = np.cumsum((0,) + ODD_SPLITS)
    col = lambda j: _dot(xn, w_ref[:, int(offs[j]):int(offs[j + 1])])
    head = lambda x, h: x[:, h * HEAD_DIM:(h + 1) * HEAD_DIM]
    hsum = hsum_ref[...]
    hsum128 = hsum_ref[0:128, 0:128]
    tm = x_ref.shape[0]
    pos = (pl.program_id(0) % tiles_per_seq) * tm + lax.broadcasted_iota(jnp.int32, (tm, HEAD_DIM), 0)
    lane = lax.broadcasted_iota(jnp.int32, (tm, HEAD_DIM), 1)
    ones_col = (lane == 0).astype(BF16)

    qm = _head_rms(col(0), hsum, gq_ref[...])
    km = _head_rms(col(1), hsum, gk_ref[...])
    for j in range(tm // MOBA_BLOCK):
        kmean_ref[0, j:j + 1, :] = jnp.mean(km[j * MOBA_BLOCK:(j + 1) * MOBA_BLOCK, :], axis=0, keepdims=True)
    km = km.astype(BF16)
    vm = col(2).astype(BF16)
    moba_id = (lane == pos // MOBA_BLOCK).astype(BF16)
    for h in range(MOBA_HEADS):
        qm_ref[0, h] = head(qm, h)
        kam_ref[0, h] = jnp.concatenate([head(km, h), moba_id], axis=1)
        vam_ref[0, h] = jnp.concatenate([head(vm, h), ones_col], axis=1)

    qd = (_head_rms(col(3), hsum, gnq_ref[...]) * (HEAD_DIM ** -0.5)).astype(BF16)
    for hk in range(NSA_KV_HEADS):
        for j in range(tm // NSA_TQ):
            for g in range(NSA_GROUP):
                qs_ref[0, hk, j, g * NSA_TQ:(g + 1) * NSA_TQ, :] = head(qd, hk * NSA_GROUP + g)[j * NSA_TQ:(j + 1) * NSA_TQ, :]
    kc_ref[...] = col(4).astype(BF16)
    vc_ref[...] = col(5).astype(BF16)
    ks = _head_rms(col(6), hsum128, gks_ref[...]).astype(BF16)
    vs = col(7).astype(BF16)
    kw = _head_rms(col(8), hsum128, gkw_ref[...]).astype(BF16)
    vw = col(9).astype(BF16)
    sel_id = (lane == (pos // SEL_BLOCK) % SEL_LANES).astype(BF16)
    for hk in range(NSA_KV_HEADS):
        kas_ref[0, hk] = jnp.concatenate([head(ks, hk), sel_id], axis=1)
        vas_ref[0, hk] = jnp.concatenate([head(vs, hk), ones_col], axis=1)
        kvw_ref[0, hk] = jnp.concatenate([head(kw, hk), head(vw, hk)], axis=1)
    gt_ref[...] = _sigmoid(col(10))


def _odd_in(h2, batch, seq, gain, w_in, moba_q_norm, moba_k_norm, nsa_q_norm, nsa_ksel_norm, nsa_kwin_norm, tm=512):
    t = h2.shape[0]
    assert MOBA_LANES == HEAD_DIM and SEL_LANES == HEAD_DIM and seq % tm == 0 and tm % MOBA_BLOCK == 0
    tps = seq // tm
    w = jnp.pad(w_in, ((0, 0), (0, ODD_IN_PAD - w_in.shape[1]))).astype(BF16)
    hsum = jnp.asarray(np.kron(np.eye(MOBA_W // HEAD_DIM), np.ones((HEAD_DIM, HEAD_DIM))), BF16)
    tile = lambda g, width: jnp.tile(g.astype(F32), width // HEAD_DIM).reshape(1, width)
    row = lambda width: pl.BlockSpec((tm, width), lambda i: (i, 0))
    heads = lambda n, width: pl.BlockSpec((1, n, tm, width), lambda i: (i // tps, 0, i % tps, 0))
    nmb = tm // MOBA_BLOCK
    nqt = tm // NSA_TQ
    rows = NSA_GROUP * NSA_TQ
    sds = jax.ShapeDtypeStruct
    out_specs = [heads(MOBA_HEADS, HEAD_DIM), heads(MOBA_HEADS, LANES), pl.BlockSpec((1, nmb, MOBA_W), lambda i: (i, 0, 0)),
                 heads(MOBA_HEADS, LANES),
                 pl.BlockSpec((1, NSA_KV_HEADS, nqt, rows, HEAD_DIM), lambda i: (i // tps, 0, i % tps, 0, 0)),
                 row(KV_W), row(KV_W), heads(NSA_KV_HEADS, LANES), heads(NSA_KV_HEADS, LANES), heads(NSA_KV_HEADS, LANES),
                 row(128)]
    out_shape = [sds((batch, MOBA_HEADS, seq, HEAD_DIM), F32), sds((batch, MOBA_HEADS, seq, LANES), BF16),
                 sds((t // tm, nmb, MOBA_W), F32), sds((batch, MOBA_HEADS, seq, LANES), BF16),
                 sds((batch, NSA_KV_HEADS, seq // NSA_TQ, rows, HEAD_DIM), BF16),
                 sds((t, KV_W), BF16), sds((t, KV_W), BF16), sds((batch, NSA_KV_HEADS, seq, LANES), BF16),
                 sds((batch, NSA_KV_HEADS, seq, LANES), BF16), sds((batch, NSA_KV_HEADS, seq, LANES), BF16),
                 sds((t, 128), F32)]
    return pl.pallas_call(
        functools.partial(_odd_in_kernel, tps),
        grid=(t // tm,),
        in_specs=[row(D_MODEL), _full((1, D_MODEL)), _full((D_MODEL, ODD_IN_PAD)), _full((MOBA_W, MOBA_W)),
                  _full((1, MOBA_W)), _full((1, MOBA_W)), _full((1, NSA_W)), _full((1, KV_W)), _full((1, KV_W))],
        out_specs=out_specs,
        out_shape=out_shape,
        compiler_params=_params("arbitrary"),
        name="odd_in",
    )(h2, gain, w, hsum, tile(moba_q_norm, MOBA_W), tile(moba_k_norm, MOBA_W), tile(nsa_q_norm, NSA_W),
      tile(nsa_ksel_norm, KV_W), tile(nsa_kwin_norm, KV_W))


def _compress_kernel(c_ref, w1_ref, w2_ref, pe_ref, g_ref, o_ref):
    kind = pl.program_id(0)
    c = c_ref[0, 0, 0]
    half = c.shape[1]
    n16 = c.shape[0]
    first = _dot(c, w1_ref[0, 0:half, :])
    second = _dot(c, w1_ref[0, half:2 * half, :])
    peb = _dot(pe_ref[0], w1_ref[0])[0:1, :]
    hid = _gelu(first + pltpu.roll(second, n16 - 1, 0) + peb)
    out = _dot(hid.astype(BF16), w2_ref[0])
    o_ref[0, 0, 0] = jnp.where(kind == 0, _rms(out, g_ref[...]), out).astype(BF16)


def _compress(kc, vc, batch, seq, pe_k, w1_k, w2_k, pe_v, w1_v, w2_v, kcmp_norm):
    n16 = seq // CMP_STRIDE
    half = CMP_STRIDE * HEAD_DIM

    def flat(x):
        return x.reshape(batch, n16, CMP_STRIDE, NSA_KV_HEADS, HEAD_DIM).transpose(0, 3, 1, 2, 4).reshape(
            batch, NSA_KV_HEADS, n16, half)

    c = jnp.stack([flat(kc), flat(vc)])
    w1 = jnp.stack([w1_k, w1_v]).astype(BF16)
    w2 = jnp.stack([w2_k, w2_v]).astype(BF16)
    pe = jnp.stack([pe_k, pe_v]).reshape(2, 1, 2 * half)
    pe = jnp.broadcast_to(pe, (2, 8, 2 * half)).astype(BF16)
    return pl.pallas_call(
        _compress_kernel,
        grid=(2, batch, NSA_KV_HEADS),
        in_specs=[pl.BlockSpec((1, 1, 1, n16, half), lambda k, b, h: (k, b, h, 0, 0)),
                  pl.BlockSpec((1, 2 * half, CMP_HIDDEN), lambda k, b, h: (k, 0, 0)),
                  pl.BlockSpec((1, CMP_HIDDEN, HEAD_DIM), lambda k, b, h: (k, 0, 0)),
                  pl.BlockSpec((1, 8, 2 * half), lambda k, b, h: (k, 0, 0)),
                  _full((1, HEAD_DIM))],
        out_specs=pl.BlockSpec((1, 1, 1, n16, HEAD_DIM), lambda k, b, h: (k, b, h, 0, 0)),
        out_shape=jax.ShapeDtypeStruct((2, batch, NSA_KV_HEADS, n16, HEAD_DIM), BF16),
        compiler_params=_params("arbitrary", "arbitrary", "arbitrary"),
        name="nsa_compress",
    )(c, w1, w2, pe, kcmp_norm.astype(F32).reshape(1, HEAD_DIM))


M_INIT = -1e30


LANES = 128


def _softmax_init(m_ref, acc_ref):
    m_ref[...] = jnp.full(m_ref.shape, M_INIT, F32)
    acc_ref[...] = jnp.zeros(acc_ref.shape, F32)


def _softmax_step(s, v_aug, m_ref, acc_ref):
    m_old = m_ref[...]
    m_new = jnp.maximum(m_old, jnp.max(s, axis=-1, keepdims=True))
    alpha = jnp.exp(m_old - m_new)
    p = jnp.exp(s - jnp.tile(m_new, (1, s.shape[1] // LANES)))
    acc_ref[...] = alpha * acc_ref[...] + _dot(p.astype(BF16), v_aug)
    m_ref[...] = m_new


def _past_keys_loop(n_keys, tile, step):
    n_full = n_keys // tile

    def body(j, carry):
        step(pl.multiple_of(j * tile, tile), tile)
        return carry

    lax.fori_loop(0, n_full, body, 0)
    rest = n_keys - n_full * tile
    start = pl.multiple_of(n_full * tile, tile)

    @pl.when(rest > tile // 2)
    def _():
        step(start, tile)

    @pl.when((rest > 0) & (rest <= tile // 2))
    def _():
        step(start, tile // 2)


def _softmax_result(acc_ref):
    acc = acc_ref[...]
    return acc[:, 0:HEAD_DIM] * (1.0 / acc[:, HEAD_DIM:HEAD_DIM + 1])


def _pick_top(score, pos, width, k):
    sel = jnp.zeros(score.shape, jnp.bool_)
    for _ in range(k):
        m, idx = _first_max(score, pos, width, axis=0)
        hit = (pos == idx) & (m > -jnp.inf)
        sel = sel | hit
        score = jnp.where(pos == idx, -jnp.inf, score)
    return sel


MOBA_LANES = 64
MOBA_TK = 2048


def _moba_kernel(q_ref, ka_ref, va_ref, kmean_ref, o_ref, qa_ref, m_ref, acc_ref):
    i = pl.program_id(2)
    q = q_ref[0, 0]
    q_hi, q_lo = _split(q)
    km_hi, km_lo = _split(kmean_ref[0, 0])
    gate = _dot_nt(km_hi, q_hi) + (_dot_nt(km_lo, q_hi) + _dot_nt(km_hi, q_lo))
    blk = lax.broadcasted_iota(jnp.int32, gate.shape, 0)
    sel = _pick_top(jnp.where(blk < i, gate, -jnp.inf), blk, gate.shape[0], MOBA_TOPK)
    qs = (q * (HEAD_DIM ** -0.5)).astype(BF16)
    bias = jnp.where(sel, 0.0, NEG).T[:, 0:MOBA_LANES]
    qa_ref[...] = jnp.concatenate([qs, bias.astype(BF16)], axis=1)
    _softmax_init(m_ref, acc_ref)

    def step(start, size):
        s = _dot_nt(qa_ref[...], ka_ref[0, 0, pl.ds(start, size), :])
        _softmax_step(s, va_ref[0, 0, pl.ds(start, size), :], m_ref, acc_ref)

    _past_keys_loop(i * MOBA_BLOCK, MOBA_TK, step)
    start = pl.multiple_of(i * MOBA_BLOCK, MOBA_BLOCK)
    s = _dot_nt(qs, ka_ref[0, 0, pl.ds(start, MOBA_BLOCK), :][:, 0:HEAD_DIM])
    qpos = lax.broadcasted_iota(jnp.int32, s.shape, 0)
    kpos = lax.broadcasted_iota(jnp.int32, s.shape, 1)
    _softmax_step(jnp.where(kpos <= qpos, s, NEG), va_ref[0, 0, pl.ds(start, MOBA_BLOCK), :], m_ref, acc_ref)
    o_ref[0, 0] = _softmax_result(acc_ref).astype(BF16)


def _moba(qm, ka, kmean, va, batch, seq):
    nmb = seq // MOBA_BLOCK
    assert nmb <= MOBA_LANES and seq % MOBA_TK == 0
    kmean = kmean.reshape(batch, nmb, MOBA_HEADS, HEAD_DIM).transpose(0, 2, 1, 3)
    kmean = jnp.pad(kmean, ((0, 0), (0, 0), (0, LANES - nmb), (0, 0)))
    return pl.pallas_call(
        _moba_kernel,
        grid=(batch, MOBA_HEADS, nmb),
        in_specs=[pl.BlockSpec((1, 1, MOBA_BLOCK, HEAD_DIM), lambda b, h, i: (b, h, i, 0)),
                  pl.BlockSpec((1, 1, seq, LANES), lambda b, h, i: (b, h, 0, 0)),
                  pl.BlockSpec((1, 1, seq, LANES), lambda b, h, i: (b, h, 0, 0)),
                  pl.BlockSpec((1, 1, LANES, HEAD_DIM), lambda b, h, i: (b, h, 0, 0))],
        out_specs=pl.BlockSpec((1, 1, MOBA_BLOCK, HEAD_DIM), lambda b, h, i: (b, h, i, 0)),
        out_shape=jax.ShapeDtypeStruct((batch, MOBA_HEADS, seq, HEAD_DIM), BF16),
        scratch_shapes=[pltpu.VMEM((MOBA_BLOCK, LANES), BF16), pltpu.VMEM((MOBA_BLOCK, LANES), F32),
                        pltpu.VMEM((MOBA_BLOCK, LANES), F32)],
        compiler_params=_params("arbitrary", "arbitrary", "arbitrary"),
        name="moba",
    )(qm, ka, va, kmean)


NSA_TQ = 128
NSA_TK = 2048
SEL_LANES = 64
SUPER_KEYS = SEL_LANES * SEL_BLOCK
CMP_WIDTH_STEP = 256


def _nsa_kernel(n_super, q_ref, kc_ref, vc_ref, ka_ref, va_ref, kvw_ref, gt_ref, e_ref, band_ref, o_ref,
                qa_ref, m_ref, acc_ref):
    qi = pl.program_id(2)
    tq = NSA_TQ
    s0 = qi * tq
    q = q_ref[0, 0, 0]

    def compressed(width):
        sc = _dot_nt(q, kc_ref[0, 0, 0:width, :])
        rq = lax.broadcasted_iota(jnp.int32, sc.shape, 0) & (tq - 1)
        n = lax.broadcasted_iota(jnp.int32, sc.shape, 1)
        vis = n * CMP_STRIDE + (CMP_BLOCK - 1) <= s0 + rq
        sc = jnp.where(vis, sc, NEG)
        pc = jnp.where(vis, jnp.exp(sc - jnp.max(sc, axis=-1, keepdims=True)), 0.0)
        pc = pc * (1.0 / jnp.maximum(jnp.sum(pc, axis=-1, keepdims=True), 1e-30))
        imp = pc[0:tq]
        for g in range(1, NSA_GROUP):
            imp = imp + pc[g * tq:(g + 1) * tq]
        return _dot(pc.astype(BF16), vc_ref[0, 0, 0:width, :]), _dot_x2(imp, band_ref[0:width, :])

    n16 = kc_ref.shape[2]
    widths = list(range(CMP_WIDTH_STEP, n16, CMP_WIDTH_STEP)) + [n16]
    o_c, pslc = lax.switch((s0 + tq - 1) // (CMP_WIDTH_STEP * CMP_STRIDE),
                           [functools.partial(compressed, wd) for wd in widths])

    wlen = WINDOW + tq

    def window(kvw, masked):
        sw = masked(_dot_nt(q, kvw[:, 0:HEAD_DIM]))
        pw = jnp.exp(sw - jnp.max(sw, axis=-1, keepdims=True))
        return _dot(pw.astype(BF16), kvw)[:, HEAD_DIM:2 * HEAD_DIM] * (1.0 / jnp.sum(pw, axis=-1, keepdims=True))

    def window_interior():
        def masked(sw):
            rq = lax.broadcasted_iota(jnp.int32, (sw.shape[0], tq), 0) & (tq - 1)
            c = lax.broadcasted_iota(jnp.int32, (sw.shape[0], tq), 1)
            return jnp.concatenate([jnp.where(c > rq, sw[:, 0:tq], NEG), sw[:, tq:WINDOW],
                                    jnp.where(c <= rq, sw[:, WINDOW:wlen], NEG)], axis=1)

        return window(kvw_ref[0, 0, pl.ds(pl.multiple_of(s0 - WINDOW, tq), wlen), :], masked)

    def window_start():
        def masked(sw):
            kabs = lax.broadcasted_iota(jnp.int32, sw.shape, 1)
            t = s0 + (lax.broadcasted_iota(jnp.int32, sw.shape, 0) & (tq - 1))
            return jnp.where((kabs <= t) & (kabs > t - WINDOW), sw, NEG)

        return window(kvw_ref[0, 0, 0:wlen, :], masked)

    o_w = lax.cond(s0 >= WINDOW, window_interior, window_start)

    pslc_t = pslc.T
    nb = pslc_t.shape[0]
    blk = lax.broadcasted_iota(jnp.int32, pslc_t.shape, 0)
    cur = (s0 + lax.broadcasted_iota(jnp.int32, pslc_t.shape, 1)) // SEL_BLOCK
    c0 = s0 // SEL_BLOCK
    elig = (blk >= 1) & (blk <= cur - 2)
    sel = _pick_top(jnp.where(elig, pslc_t, -jnp.inf), blk, nb, SEL_TOPK - 3)
    sel = sel | (((blk == 0) | (blk == cur - 1)) & (blk < c0))
    bias = jnp.where(sel, 0.0, NEG).T.astype(BF16)
    for st in range(n_super):
        b = bias[:, st * SEL_LANES:(st + 1) * SEL_LANES]
        qa_ref[st] = jnp.concatenate([q, jnp.concatenate([b] * NSA_GROUP, axis=0)], axis=1)

    _softmax_init(m_ref, acc_ref)

    def step(start, size):
        s = _dot_nt(qa_ref[start // SUPER_KEYS], ka_ref[0, 0, pl.ds(start, size), :])
        _softmax_step(s, va_ref[0, 0, pl.ds(start, size), :], m_ref, acc_ref)

    _past_keys_loop(s0, NSA_TK, step)
    d0 = pl.multiple_of(s0, tq)
    s = _dot_nt(q, ka_ref[0, 0, pl.ds(d0, tq), :][:, 0:HEAD_DIM])
    rqd = lax.broadcasted_iota(jnp.int32, s.shape, 0) & (tq - 1)
    col = lax.broadcasted_iota(jnp.int32, s.shape, 1)
    _softmax_step(jnp.where(col <= rqd, s, NEG), va_ref[0, 0, pl.ds(d0, tq), :], m_ref, acc_ref)
    o_s = _softmax_result(acc_ref)

    w = NSA_GROUP * HEAD_DIM
    gexp = _dot_x2(gt_ref[...], e_ref[0])
    wide = lambda x: jnp.concatenate([x[g * tq:(g + 1) * tq] for g in range(NSA_GROUP)], axis=1)
    o_ref[...] = (gexp[:, 0:w] * wide(o_c) + gexp[:, w:2 * w] * wide(o_s) + gexp[:, 2 * w:3 * w] * wide(o_w)).astype(BF16)


def _nsa(qs, kcmp, vcmp, ka, va, kvw, gates, batch, seq):
    tq = NSA_TQ
    nq = seq // tq
    nb = seq // SEL_BLOCK
    n16 = seq // CMP_STRIDE
    assert seq % SUPER_KEYS == 0
    n_super = seq // SUPER_KEYS
    rows = NSA_GROUP * tq
    e = np.zeros((NSA_KV_HEADS, 128, 3 * NSA_GROUP * HEAD_DIM), np.float32)
    for br in range(3):
        for hk in range(NSA_KV_HEADS):
            for g in range(NSA_GROUP):
                c = (br * NSA_GROUP + g) * HEAD_DIM
                e[hk, br * NSA_HEADS + hk * NSA_GROUP + g, c:c + HEAD_DIM] = 1.0
    nn, jj = np.arange(n16)[:, None], np.arange(nb)[None, :]
    band = ((nn >= 4 * jj - 1) & (nn <= 4 * jj + 3)).astype(np.float32)
    resident = lambda width: pl.BlockSpec((1, 1, seq, width), lambda b, h, i: (b, h, 0, 0))
    w = NSA_GROUP * HEAD_DIM
    return pl.pallas_call(
        functools.partial(_nsa_kernel, n_super),
        grid=(batch, NSA_KV_HEADS, nq),
        in_specs=[pl.BlockSpec((1, 1, 1, rows, HEAD_DIM), lambda b, h, i: (b, h, i, 0, 0)),
                  pl.BlockSpec((1, 1, n16, HEAD_DIM), lambda b, h, i: (b, h, 0, 0)),
                  pl.BlockSpec((1, 1, n16, HEAD_DIM), lambda b, h, i: (b, h, 0, 0)),
                  resident(LANES), resident(LANES), resident(LANES),
                  pl.BlockSpec((tq, 128), lambda b, h, i: (b * nq + i, 0)),
                  pl.BlockSpec((1, 128, 3 * w), lambda b, h, i: (h, 0, 0)),
                  pl.BlockSpec((n16, nb), lambda b, h, i: (0, 0))],
        out_specs=pl.BlockSpec((tq, w), lambda b, h, i: (b * nq + i, h)),
        out_shape=jax.ShapeDtypeStruct((batch * seq, NSA_W), BF16),
        scratch_shapes=[pltpu.VMEM((n_super, rows, LANES), BF16), pltpu.VMEM((rows, LANES), F32),
                        pltpu.VMEM((rows, LANES), F32)],
        compiler_params=_params("arbitrary", "arbitrary", "arbitrary"),
        name="nsa",
    )(qs, kcmp, vcmp, ka, va, kvw, gates, jnp.asarray(e, BF16), jnp.asarray(band, BF16))


def _odd_out_kernel(om_ref, on_ref, h_ref, w_ref, o_ref):
    acc = h_ref[...] + _dot(on_ref[...], w_ref[MOBA_W:D_MODEL, :])
    for h in range(MOBA_HEADS):
        acc = acc + _dot(om_ref[0, h], w_ref[h * HEAD_DIM:(h + 1) * HEAD_DIM, :])
    o_ref[...] = acc


def _odd_out(o_moba, o_nsa, h2, w_out, seq, tm=512):
    t = h2.shape[0]
    tps = seq // tm
    row = lambda w: pl.BlockSpec((tm, w), lambda i: (i, 0))
    return pl.pallas_call(
        _odd_out_kernel,
        grid=(t // tm,),
        in_specs=[pl.BlockSpec((1, MOBA_HEADS, tm, HEAD_DIM), lambda i: (i // tps, 0, i % tps, 0)),
                  row(NSA_W), row(D_MODEL), _full((D_MODEL, D_MODEL))],
        out_specs=row(D_MODEL),
        out_shape=jax.ShapeDtypeStruct((t, D_MODEL), F32),
        compiler_params=_params("arbitrary"),
        name="odd_out",
    )(o_moba, o_nsa, h2, w_out)


def _even_layer(h2, batch, seq, norm, w_in, w_out, lam_re, lam_im, log_dt, b_re, b_im, c_re, c_im, d, w_glu, conv_w, conv_b):
    u, ub, yb = _even_in(h2, norm.reshape(1, D_MODEL), w_in.astype(BF16), conv_w, conv_b.reshape(1, CONV_WIDTH), seq)
    ypre = _s5_mixer_pre(ub, batch, seq, lam_re, lam_im, log_dt, b_re, b_im, c_re, c_im)
    return _even_out(ypre, u, yb, h2, d.reshape(1, S5_WIDTH), w_glu.astype(BF16), w_out.astype(BF16))


def _odd_layer(h2, batch, seq, norm, w_in, w_out, moba_q_norm, moba_k_norm, nsa_q_norm, nsa_kcmp_norm, nsa_ksel_norm,
               nsa_kwin_norm, cmp_pe_k, cmp_w1_k, cmp_w2_k, cmp_pe_v, cmp_w1_v, cmp_w2_v):
    qm, kam, kmean, vam, qs, kc, vc, kas, vas, kvw, gates = _odd_in(
        h2, batch, seq, norm.reshape(1, D_MODEL), w_in, moba_q_norm, moba_k_norm, nsa_q_norm, nsa_ksel_norm, nsa_kwin_norm)
    cmp = _compress(kc, vc, batch, seq, cmp_pe_k, cmp_w1_k, cmp_w2_k, cmp_pe_v, cmp_w1_v, cmp_w2_v, nsa_kcmp_norm)
    o_moba = _moba(qm, kam, kmean, vam, batch, seq)
    o_nsa = _nsa(qs, cmp[0], cmp[1], kas, vas, kvw, gates, batch, seq)
    return _odd_out(o_moba, o_nsa, h2, w_out.astype(BF16), seq)


def kernel(x, ev_norm_mix, ev_w_in, ev_w_out, s5_lam_re, s5_lam_im, s5_log_dt, s5_b_re, s5_b_im, s5_c_re, s5_c_im, s5_d, s5_w_glu, conv_w, conv_b, od_norm_mix, od_w_in, od_w_out, moba_q_norm, moba_k_norm, nsa_q_norm, nsa_kcmp_norm, nsa_ksel_norm, nsa_kwin_norm, cmp_pe_k, cmp_w1_k, cmp_w2_k, cmp_pe_v, cmp_w1_v, cmp_w2_v, moe_norm, moe_w_group, moe_b_group, moe_w_expert, moe_b_expert, moe_w_gate, moe_w_up, moe_w_down):
    batch, seq, _ = x.shape
    depth = moe_norm.shape[0]
    h = x.reshape(batch * seq, D_MODEL)
    for layer in range(depth):
        i = layer // 2
        if layer % 2 == 0:
            h = _even_layer(h, batch, seq, ev_norm_mix[i], ev_w_in[i], ev_w_out[i], s5_lam_re[i], s5_lam_im[i], s5_log_dt[i],
                            s5_b_re[i], s5_b_im[i], s5_c_re[i], s5_c_im[i], s5_d[i], s5_w_glu[i], conv_w[i], conv_b[i])
        else:
            h = _odd_layer(h, batch, seq, od_norm_mix[i], od_w_in[i], od_w_out[i], moba_q_norm[i], moba_k_norm[i],
                           nsa_q_norm[i], nsa_kcmp_norm[i], nsa_ksel_norm[i], nsa_kwin_norm[i], cmp_pe_k[i], cmp_w1_k[i],
                           cmp_w2_k[i], cmp_pe_v[i], cmp_w1_v[i], cmp_w2_v[i])
        h = _moe(h, moe_norm[layer].reshape(1, D_MODEL), moe_w_group[layer], moe_b_group[layer], moe_w_expert[layer],
                 moe_b_expert[layer], moe_w_gate[layer], moe_w_up[layer], moe_w_down[layer])
    return h.reshape(batch, seq, D_MODEL)
```

```python
import functools
import math

import jax
import jax.numpy as jnp
import numpy as np
from jax import lax
from jax.experimental import pallas as pl
from jax.experimental.pallas import tpu as pltpu

D_MODEL = 1024
HEAD_DIM = 64
EPS = 1e-6
S5_WIDTH = 256
S5_GROUP = 16
S5_GROUPS = 16
S5_STATE = 64
S5_CHUNK = 16
CONV_WIDTH = 768
CONV_K = 3
MOBA_HEADS = 4
NSA_HEADS = 12
NSA_KV_HEADS = 2
NSA_GROUP = 6
MOBA_W = 256
NSA_W = 768
KV_W = 128
MOBA_BLOCK = 256
MOBA_TOPK = 3
CMP_BLOCK = 32
CMP_STRIDE = 16
CMP_HIDDEN = 256
SEL_BLOCK = 64
SEL_TOPK = 8
WINDOW = 512
N_GROUPS = 4
EXPERTS_PER_GROUP = 4
N_EXPERTS = 16
EXPERT_FF = 256

VMEM_LIMIT_BYTES = 56 * 1024 * 1024
NEG = -float(2 ** 30)
F32 = jnp.float32
BF16 = jnp.bfloat16


def _params(*semantics):
    return pltpu.CompilerParams(dimension_semantics=semantics, vmem_limit_bytes=VMEM_LIMIT_BYTES)


def _dot(a, b):
    return jnp.dot(a, b, preferred_element_type=F32)


def _dot_nt(a, b):
    return lax.dot_general(a, b, (((1,), (1,)), ((), ())), preferred_element_type=F32)


def _split(x):
    hi = x.astype(BF16)
    lo = (x - hi.astype(F32)).astype(BF16)
    return hi, lo


def _dot_x2(x, w):
    hi, lo = _split(x)
    return _dot(hi, w) + _dot(lo, w)


def _dot_x3(x, w_hi, w_lo):
    hi, lo = _split(x)
    return _dot(hi, w_hi) + (_dot(hi, w_lo) + _dot(lo, w_hi))


def _rms(x, gain):
    return x * lax.rsqrt(jnp.mean(x * x, axis=-1, keepdims=True) + EPS) * gain


def _gelu(x):
    return 0.5 * x * (1.0 + jnp.tanh(math.sqrt(2.0 / math.pi) * (x + 0.044715 * (x * x * x))))


def _sigmoid(x):
    return 1.0 / (1.0 + jnp.exp(-x))


def _full(shape):
    n = len(shape)
    return pl.BlockSpec(shape, lambda *_: (0,) * n)


def _even_in_kernel(tiles_per_seq, x_ref, g_ref, w_ref, cw_ref, cb_ref, u_ref, ub_ref, yb_ref, carry_ref):
    i = pl.program_id(0)
    xn = _rms(x_ref[...], g_ref[...]).astype(BF16)
    u = _dot(xn, w_ref[:, 0:S5_WIDTH])
    u_ref[...] = u
    ub_ref[...] = u.astype(BF16)
    o = S5_WIDTH
    xc = _dot(xn, w_ref[:, o:o + CONV_WIDTH])
    gb = _dot(xn, w_ref[:, o + CONV_WIDTH:o + 2 * CONV_WIDTH])
    gc = _dot(xn, w_ref[:, o + 2 * CONV_WIDTH:o + 3 * CONV_WIDTH])
    z = gc * xc
    tm = z.shape[0]

    @pl.when(i % tiles_per_seq == 0)
    def _():
        carry_ref[...] = jnp.zeros_like(carry_ref)

    row = lax.broadcasted_iota(jnp.int32, z.shape, 0)
    prev1 = carry_ref[7:8, :]
    prev2 = carry_ref[6:7, :]
    z1 = jnp.where(row == 0, prev1, pltpu.roll(z, 1, 0))
    z2 = jnp.where(row == 0, prev2, jnp.where(row == 1, prev1, pltpu.roll(z, 2, 0)))
    y = cw_ref[0:1, :] * z2 + cw_ref[1:2, :] * z1 + cw_ref[2:3, :] * z + cb_ref[...]
    yb_ref[...] = (gb * y).astype(BF16)
    carry_ref[...] = z[tm - 8:tm, :]


def _even_in(x2, gain, w_in, conv_w, conv_b, seq, tm=512):
    t = x2.shape[0]
    n_in = w_in.shape[1]
    return pl.pallas_call(
        functools.partial(_even_in_kernel, seq // tm),
        grid=(t // tm,),
        in_specs=[pl.BlockSpec((tm, D_MODEL), lambda i: (i, 0)), _full((1, D_MODEL)),
                  _full((D_MODEL, n_in)), _full((CONV_K, CONV_WIDTH)), _full((1, CONV_WIDTH))],
        out_specs=[pl.BlockSpec((tm, S5_WIDTH), lambda i: (i, 0)), pl.BlockSpec((tm, S5_WIDTH), lambda i: (i, 0)),
                   pl.BlockSpec((tm, CONV_WIDTH), lambda i: (i, 0))],
        out_shape=[jax.ShapeDtypeStruct((t, S5_WIDTH), F32), jax.ShapeDtypeStruct((t, S5_WIDTH), BF16),
                   jax.ShapeDtypeStruct((t, CONV_WIDTH), BF16)],
        scratch_shapes=[pltpu.VMEM((8, CONV_WIDTH), F32)],
        compiler_params=_params("arbitrary"),
        name="even_in",
    )(x2, gain, w_in, conv_w, conv_b)


def _s5_weights(lam_re, lam_im, log_dt, b_re, b_im, c_re, c_im):
    g, p, hg, ck = S5_GROUPS, S5_STATE, S5_GROUP, S5_CHUNK
    lr, li = lam_re.astype(F32), lam_im.astype(F32)
    dt = jnp.exp(log_dt.astype(F32))[:, None]
    mag = jnp.exp(lr * dt)
    a_re, a_im = mag * jnp.cos(li * dt), mag * jnp.sin(li * dt)
    den = lr * lr + li * li
    f_re = ((a_re - 1.0) * lr + a_im * li) / den
    f_im = (a_im * lr - (a_re - 1.0) * li) / den
    br, bi = b_re.astype(F32), b_im.astype(F32)
    bb_re = f_re[..., None] * br - f_im[..., None] * bi
    bb_im = f_re[..., None] * bi + f_im[..., None] * br
    pw_re, pw_im = [jnp.ones_like(a_re)], [jnp.zeros_like(a_im)]
    for _ in range(ck):
        r, m = pw_re[-1], pw_im[-1]
        pw_re.append(r * a_re - m * a_im)
        pw_im.append(r * a_im + m * a_re)
    pw_re, pw_im = jnp.stack(pw_re), jnp.stack(pw_im)
    cr, ci = c_re.astype(F32), c_im.astype(F32)
    rev_re, rev_im = pw_re[ck - 1::-1][:ck], pw_im[ck - 1::-1][:ck]
    ws_re = rev_re[:, :, :, None] * bb_re[None] - rev_im[:, :, :, None] * bb_im[None]
    ws_im = rev_re[:, :, :, None] * bb_im[None] + rev_im[:, :, :, None] * bb_re[None]
    ca_re = cr[None] * pw_re[1:, :, None, :] - ci[None] * pw_im[1:, :, None, :]
    ca_im = cr[None] * pw_im[1:, :, None, :] + ci[None] * pw_re[1:, :, None, :]
    cb_re = jnp.einsum('ghp,kgp,gpj->kghj', cr, pw_re[:ck], bb_re) - jnp.einsum('ghp,kgp,gpj->kghj', cr, pw_im[:ck], bb_im) \
        - jnp.einsum('ghp,kgp,gpj->kghj', ci, pw_re[:ck], bb_im) - jnp.einsum('ghp,kgp,gpj->kghj', ci, pw_im[:ck], bb_re)
    lag = np.arange(ck)[None, :] - np.arange(ck)[:, None]
    tz = cb_re[np.clip(lag, 0, ck - 1)]
    tz = jnp.where((lag >= 0)[:, :, None, None, None], tz, 0.0)
    cw = ck * g * hg
    ws = jnp.stack([ws_re, ws_im]).transpose(1, 2, 4, 0, 3).reshape(cw, 2 * p)
    wc = jnp.stack([ca_re, -ca_im]).transpose(0, 2, 4, 1, 3).reshape(2 * g * p, ck * hg)
    tzc = tz.transpose(0, 2, 4, 1, 3).reshape(cw, ck * hg)
    return ws.astype(BF16), wc.astype(BF16), tzc.astype(BF16), pw_re[ck].reshape(1, g * p), pw_im[ck].reshape(1, g * p)


def _group_expand(compact, expand, row_shift, col_shift, col0):
    full = _dot(compact, expand)
    row = lax.broadcasted_iota(jnp.int32, full.shape, 0)
    col = col0 + lax.broadcasted_iota(jnp.int32, full.shape, 1)
    same = ((row >> row_shift) & (S5_GROUPS - 1)) == ((col >> col_shift) & (S5_GROUPS - 1))
    return jnp.where(same, full, 0.0).astype(BF16)


def _s5_state_kernel(u_ref, ws_ref, e_ref, s_ref, w_scr):
    @pl.when(pl.program_id(1) == 0)
    def _():
        w_scr[...] = _group_expand(ws_ref[...], e_ref[...], 4, 6, pl.program_id(0) * w_scr.shape[1])

    s_ref[...] = _dot(u_ref[...], w_scr[...])


def _s5_scan_kernel(s_ref, are_ref, aim_ref, xprev_ref, st_ref):
    @pl.when(pl.program_id(0) == 0)
    def _():
        st_ref[...] = jnp.zeros_like(st_ref)

    a_re, a_im = are_ref[...], aim_ref[...]
    nb, n = s_ref.shape[0], s_ref.shape[1]
    half = a_re.shape[1]

    def body(c, carry):
        out = []
        for b in range(nb):
            xr, xi = carry[2 * b], carry[2 * b + 1]
            xprev_ref[b, pl.ds(c, 1), 0:half] = xr
            xprev_ref[b, pl.ds(c, 1), half:2 * half] = xi
            s = s_ref[b, pl.ds(c, 1), :]
            out += [a_re * xr - a_im * xi + s[:, 0:half], a_re * xi + a_im * xr + s[:, half:2 * half]]
        return tuple(out)

    init = tuple(st_ref[b:b + 1, o:o + half] for b in range(nb) for o in (0, half))
    final = lax.fori_loop(0, n, body, init, unroll=4)
    for b in range(nb):
        st_ref[b:b + 1, 0:half] = final[2 * b]
        st_ref[b:b + 1, half:2 * half] = final[2 * b + 1]


def _s5_out_kernel(u_ref, xp_ref, tz_ref, wc_ref, e_ref, y_ref, tz_scr, wc_scr):
    @pl.when(pl.program_id(1) == 0)
    def _():
        col0 = pl.program_id(0) * tz_scr.shape[1]
        tz_scr[...] = _group_expand(tz_ref[...], e_ref[...], 4, 4, col0)
        wc_scr[...] = _group_expand(wc_ref[...], e_ref[...], 6, 4, col0)

    y_ref[...] = _dot(u_ref[...], tz_scr[...]) + _dot(xp_ref[...].astype(BF16), wc_scr[...])


def _expand_matrix(outer, inner):
    e = np.zeros((outer, inner, outer, S5_GROUPS, inner), np.float32)
    for x in range(outer):
        for y in range(inner):
            e[x, y, x, :, y] = 1.0
    return jnp.asarray(e.reshape(outer * inner, outer * S5_GROUPS * inner), BF16)


def _s5_mixer_pre(ub, batch, seq, lam_re, lam_im, log_dt, b_re, b_im, c_re, c_im):
    ws, wc, tz, a16_re, a16_im = _s5_weights(lam_re, lam_im, log_dt, b_re, b_im, c_re, c_im)
    nc = seq // S5_CHUNK
    rows = batch * nc
    cw = S5_CHUNK * S5_WIDTH
    sw = 2 * S5_GROUPS * S5_STATE
    tr = min(rows, 512)
    tn = 512
    uc = ub.reshape(rows, cw)
    e_state = _expand_matrix(2, S5_STATE)
    e_out = _expand_matrix(S5_CHUNK, S5_GROUP)
    s = pl.pallas_call(
        _s5_state_kernel,
        grid=(sw // tn, rows // tr),
        in_specs=[pl.BlockSpec((tr, cw), lambda j, i: (i, 0)), _full(ws.shape),
                  pl.BlockSpec((e_state.shape[0], tn), lambda j, i: (0, j))],
        out_specs=pl.BlockSpec((tr, tn), lambda j, i: (i, j)),
        out_shape=jax.ShapeDtypeStruct((rows, sw), F32),
        scratch_shapes=[pltpu.VMEM((cw, tn), BF16)],
        compiler_params=_params("arbitrary", "arbitrary"),
        name="s5_state",
    )(uc, ws, e_state)
    tc = min(nc, 256)
    xprev = pl.pallas_call(
        _s5_scan_kernel,
        grid=(nc // tc,),
        in_specs=[pl.BlockSpec((batch, tc, sw), lambda i: (0, i, 0)), _full((1, sw // 2)), _full((1, sw // 2))],
        out_specs=pl.BlockSpec((batch, tc, sw), lambda i: (0, i, 0)),
        out_shape=jax.ShapeDtypeStruct((batch, nc, sw), F32),
        scratch_shapes=[pltpu.VMEM((batch, sw), F32)],
        compiler_params=_params("arbitrary"),
        name="s5_scan",
    )(s.reshape(batch, nc, sw), a16_re, a16_im)
    y = pl.pallas_call(
        _s5_out_kernel,
        grid=(cw // tn, rows // tr),
        in_specs=[pl.BlockSpec((tr, cw), lambda j, i: (i, 0)), pl.BlockSpec((tr, sw), lambda j, i: (i, 0)),
                  _full(tz.shape), _full(wc.shape), pl.BlockSpec((e_out.shape[0], tn), lambda j, i: (0, j))],
        out_specs=pl.BlockSpec((tr, tn), lambda j, i: (i, j)),
        out_shape=jax.ShapeDtypeStruct((rows, cw), F32),
        scratch_shapes=[pltpu.VMEM((cw, tn), BF16), pltpu.VMEM((sw, tn), BF16)],
        compiler_params=_params("arbitrary", "arbitrary"),
        name="s5_out",
    )(uc, xprev.reshape(rows, sw), tz, wc, e_out)
    return y.reshape(batch * seq, S5_WIDTH)


def _even_out_kernel(ypre_ref, u_ref, yb_ref, x_ref, d_ref, wglu_ref, wout_ref, o_ref):
    y = _gelu(ypre_ref[...] + d_ref[...] * u_ref[...])
    y = y * _sigmoid(_dot(y.astype(BF16), wglu_ref[...]))
    o_ref[...] = (x_ref[...] + _dot(y.astype(BF16), wout_ref[0:S5_WIDTH, :])
                  + _dot(yb_ref[...], wout_ref[S5_WIDTH:D_MODEL, :]))


def _even_out(ypre, u, yb, x2, d, w_glu, w_out, tm=512):
    t = x2.shape[0]
    row = lambda w: pl.BlockSpec((tm, w), lambda i: (i, 0))
    return pl.pallas_call(
        _even_out_kernel,
        grid=(t // tm,),
        in_specs=[row(S5_WIDTH), row(S5_WIDTH), row(CONV_WIDTH), row(D_MODEL), _full((1, S5_WIDTH)),
                  _full((S5_WIDTH, S5_WIDTH)), _full((D_MODEL, D_MODEL))],
        out_specs=row(D_MODEL),
        out_shape=jax.ShapeDtypeStruct((t, D_MODEL), F32),
        compiler_params=_params("arbitrary"),
        name="even_out",
    )(ypre, u, yb, x2, d, w_glu, w_out)


def _first_max(v, pos, width, axis=-1):
    m = jnp.max(v, axis=axis, keepdims=True)
    idx = jnp.min(jnp.where(v == m, pos, width), axis=axis, keepdims=True)
    return m, idx


def _moe_kernel(h_ref, g_ref, wr_hi_ref, wr_lo_ref, br_ref, wg_ref, wu_ref, wd_ref, o_ref, xn_ref, gate_ref):
    e = pl.program_id(1)

    @pl.when(e == 0)
    def _():
        h = h_ref[...]
        xn = _rms(h, g_ref[...])
        xn_ref[...] = xn.astype(BF16)
        o_ref[...] = h
        logits = _dot_x3(xn, wr_hi_ref[...], wr_lo_ref[...]) + br_ref[...]
        lane = lax.broadcasted_iota(jnp.int32, logits.shape, 1)
        width = logits.shape[1]
        is_g = lane < N_GROUPS
        gl = jnp.where(is_g, logits, -jnp.inf)
        gm, gi = _first_max(gl, lane, width)
        gw = 1.0 / jnp.sum(jnp.where(is_g, jnp.exp(gl - gm), 0.0), axis=-1, keepdims=True)
        lo = N_GROUPS + gi * EXPERTS_PER_GROUP
        in_grp = (lane >= lo) & (lane < lo + EXPERTS_PER_GROUP)
        el = jnp.where(in_grp, logits, -jnp.inf)
        m1, i1 = _first_max(el, lane, width)
        m2, i2 = _first_max(jnp.where(lane == i1, -jnp.inf, el), lane, width)
        p2 = jnp.exp(m2 - m1)
        w1 = gw / (1.0 + p2)
        w2 = gw * p2 / (1.0 + p2)
        gate_ref[...] = jnp.where(lane == i1, w1, 0.0) + jnp.where(lane == i2, w2, 0.0)

    xn = xn_ref[...]
    lane = lax.broadcasted_iota(jnp.int32, gate_ref.shape, 1)
    gate = gate_ref[...]
    y = None
    for j in range(EXPERTS_PER_GROUP):
        ge = jnp.sum(jnp.where(lane == e * EXPERTS_PER_GROUP + (j + N_GROUPS), gate, 0.0), axis=-1, keepdims=True)
        h1 = _dot(xn, wg_ref[j])
        h3 = _dot(xn, wu_ref[j])
        act = (h1 * _sigmoid(h1)) * h3 * ge
        yj = _dot(act.astype(BF16), wd_ref[j])
        y = yj if y is None else y + yj
    o_ref[...] += y


def _moe(h2, gain, w_group, b_group, w_expert, b_expert, w_gate, w_up, w_down, tm=1024):
    t = h2.shape[0]
    rw = 128
    wr = jnp.zeros((D_MODEL, rw), F32).at[:, 0:N_GROUPS].set(w_group).at[:, N_GROUPS:N_GROUPS + N_EXPERTS].set(w_expert)
    br = jnp.zeros((1, rw), F32).at[0, 0:N_GROUPS].set(b_group).at[0, N_GROUPS:N_GROUPS + N_EXPERTS].set(b_expert)
    wr_hi = wr.astype(BF16)
    wr_lo = (wr - wr_hi.astype(F32)).astype(BF16)
    return pl.pallas_call(
        _moe_kernel,
        grid=(t // tm, N_GROUPS),
        in_specs=[pl.BlockSpec((tm, D_MODEL), lambda i, e: (i, 0)), _full((1, D_MODEL)),
                  _full((D_MODEL, rw)), _full((D_MODEL, rw)), _full((1, rw)),
                  pl.BlockSpec((EXPERTS_PER_GROUP, D_MODEL, EXPERT_FF), lambda i, e: (e, 0, 0)),
                  pl.BlockSpec((EXPERTS_PER_GROUP, D_MODEL, EXPERT_FF), lambda i, e: (e, 0, 0)),
                  pl.BlockSpec((EXPERTS_PER_GROUP, EXPERT_FF, D_MODEL), lambda i, e: (e, 0, 0))],
        out_specs=pl.BlockSpec((tm, D_MODEL), lambda i, e: (i, 0)),
        out_shape=jax.ShapeDtypeStruct((t, D_MODEL), F32),
        scratch_shapes=[pltpu.VMEM((tm, D_MODEL), BF16), pltpu.VMEM((tm, rw), F32)],
        compiler_params=_params("arbitrary", "arbitrary"),
        name="moe",
    )(h2, gain, wr_hi, wr_lo, br, w_gate.astype(BF16), w_up.astype(BF16), w_down.astype(BF16))


ODD_SPLITS = (MOBA_W, MOBA_W, MOBA_W, NSA_W, KV_W, KV_W, KV_W, KV_W, KV_W, KV_W, 128)
ODD_IN_PAD = sum(ODD_SPLITS)


def _head_rms(x, hsum, gain):
    w = x.shape[1]
    ss = jnp.concatenate([_dot_x2(x[:, o:o + hsum.shape[0]] * x[:, o:o + hsum.shape[0]], hsum)
                          for o in range(0, w, hsum.shape[0])], axis=1) if w > hsum.shape[0] else _dot_x2(x * x, hsum)
    return x * lax.rsqrt(ss * (1.0 / HEAD_DIM) + EPS) * gain


def _odd_in_kernel(tiles_per_seq, x_ref, g_ref, w_ref, hsum_ref, gq_ref, gk_ref, gnq_ref, gks_ref, gkw_ref,
                   qm_ref, kam_ref, kmean_ref, vam_ref, qs_ref, kc_ref, vc_ref, kas_ref, vas_ref, kvw_ref, gt_ref):
    xn = _rms(x_ref[...], g_ref[...]).astype(BF16)
    offs = np.cumsum((0,) + ODD_SPLITS)
    col = lambda j: _dot(xn, w_ref[:, int(offs[j]):int(offs[j + 1])])
    head = lambda x, h: x[:, h * HEAD_DIM:(h + 1) * HEAD_DIM]
    hsum = hsum_ref[...]
    hsum128 = hsum_ref[0:128, 0:128]
    tm = x_ref.shape[0]
    pos = (pl.program_id(0) % tiles_per_seq) * tm + lax.broadcasted_iota(jnp.int32, (tm, HEAD_DIM), 0)
    lane = lax.broadcasted_iota(jnp.int32, (tm, HEAD_DIM), 1)
    ones_col = (lane == 0).astype(BF16)

    qm = _head_rms(col(0), hsum, gq_ref[...])
    km = _head_rms(col(1), hsum, gk_ref[...])
    for j in range(tm // MOBA_BLOCK):
        kmean_ref[0, j:j + 1, :] = jnp.mean(km[j * MOBA_BLOCK:(j + 1) * MOBA_BLOCK, :], axis=0, keepdims=True)
    km = km.astype(BF16)
    vm = col(2).astype(BF16)
    moba_id = (lane == pos // MOBA_BLOCK).astype(BF16)
    for h in range(MOBA_HEADS):
        qm_ref[0, h] = head(qm, h)
        kam_ref[0, h] = jnp.concatenate([head(km, h), moba_id], axis=1)
        vam_ref[0, h] = jnp.concatenate([head(vm, h), ones_col], axis=1)

    qd = (_head_rms(col(3), hsum, gnq_ref[...]) * (HEAD_DIM ** -0.5)).astype(BF16)
    for hk in range(NSA_KV_HEADS):
        for j in range(tm // NSA_TQ):
            for g in range(NSA_GROUP):
                qs_ref[0, hk, j, g * NSA_TQ:(g + 1) * NSA_TQ, :] = head(qd, hk * NSA_GROUP + g)[j * NSA_TQ:(j + 1) * NSA_TQ, :]
    kc_ref[...] = col(4).astype(BF16)
    vc_ref[...] = col(5).astype(BF16)
    ks = _head_rms(col(6), hsum128, gks_ref[...]).astype(BF16)
    vs = col(7).astype(BF16)
    kw = _head_rms(col(8), hsum128, gkw_ref[...]).astype(BF16)
    vw = col(9).astype(BF16)
    sel_id = (lane == (pos // SEL_BLOCK) % SEL_LANES).astype(BF16)
    for hk in range(NSA_KV_HEADS):
        kas_ref[0, hk] = jnp.concatenate([head(ks, hk), sel_id], axis=1)
        vas_ref[0, hk] = jnp.concatenate([head(vs, hk), ones_col], axis=1)
        kvw_ref[0, hk] = jnp.concatenate([head(kw, hk), head(vw, hk)], axis=1)
    gt_ref[...] = _sigmoid(col(10))


def _odd_in(h2, batch, seq, gain, w_in, moba_q_norm, moba_k_norm, nsa_q_norm, nsa_ksel_norm, nsa_kwin_norm, tm=512):
    t = h2.shape[0]
    assert MOBA_LANES == HEAD_DIM and SEL_LANES == HEAD_DIM and seq % tm == 0 and tm % MOBA_BLOCK == 0
    tps = seq // tm
    w = jnp.pad(w_in, ((0, 0), (0, ODD_IN_PAD - w_in.shape[1]))).astype(BF16)
    hsum = jnp.asarray(np.kron(np.eye(MOBA_W // HEAD_DIM), np.ones((HEAD_DIM, HEAD_DIM))), BF16)
    tile = lambda g, width: jnp.tile(g.astype(F32), width // HEAD_DIM).reshape(1, width)
    row = lambda width: pl.BlockSpec((tm, width), lambda i: (i, 0))
    heads = lambda n, width: pl.BlockSpec((1, n, tm, width), lambda i: (i // tps, 0, i % tps, 0))
    nmb = tm // MOBA_BLOCK
    nqt = tm // NSA_TQ
    rows = NSA_GROUP * NSA_TQ
    sds = jax.ShapeDtypeStruct
    out_specs = [heads(MOBA_HEADS, HEAD_DIM), heads(MOBA_HEADS, LANES), pl.BlockSpec((1, nmb, MOBA_W), lambda i: (i, 0, 0)),
                 heads(MOBA_HEADS, LANES),
                 pl.BlockSpec((1, NSA_KV_HEADS, nqt, rows, HEAD_DIM), lambda i: (i // tps, 0, i % tps, 0, 0)),
                 row(KV_W), row(KV_W), heads(NSA_KV_HEADS, LANES), heads(NSA_KV_HEADS, LANES), heads(NSA_KV_HEADS, LANES),
                 row(128)]
    out_shape = [sds((batch, MOBA_HEADS, seq, HEAD_DIM), F32), sds((batch, MOBA_HEADS, seq, LANES), BF16),
                 sds((t // tm, nmb, MOBA_W), F32), sds((batch, MOBA_HEADS, seq, LANES), BF16),
                 sds((batch, NSA_KV_HEADS, seq // NSA_TQ, rows, HEAD_DIM), BF16),
                 sds((t, KV_W), BF16), sds((t, KV_W), BF16), sds((batch, NSA_KV_HEADS, seq, LANES), BF16),
                 sds((batch, NSA_KV_HEADS, seq, LANES), BF16), sds((batch, NSA_KV_HEADS, seq, LANES), BF16),
                 sds((t, 128), F32)]
    return pl.pallas_call(
        functools.partial(_odd_in_kernel, tps),
        grid=(t // tm,),
        in_specs=[row(D_MODEL), _full((1, D_MODEL)), _full((D_MODEL, ODD_IN_PAD)), _full((MOBA_W, MOBA_W)),
                  _full((1, MOBA_W)), _full((1, MOBA_W)), _full((1, NSA_W)), _full((1, KV_W)), _full((1, KV_W))],
        out_specs=out_specs,
        out_shape=out_shape,
        compiler_params=_params("arbitrary"),
        name="odd_in",
    )(h2, gain, w, hsum, tile(moba_q_norm, MOBA_W), tile(moba_k_norm, MOBA_W), tile(nsa_q_norm, NSA_W),
      tile(nsa_ksel_norm, KV_W), tile(nsa_kwin_norm, KV_W))


def _compress_kernel(c_ref, w1_ref, w2_ref, pe_ref, g_ref, o_ref):
    kind = pl.program_id(0)
    c = c_ref[0, 0, 0]
    half = c.shape[1]
    n16 = c.shape[0]
    first = _dot(c, w1_ref[0, 0:half, :])
    second = _dot(c, w1_ref[0, half:2 * half, :])
    peb = _dot(pe_ref[0], w1_ref[0])[0:1, :]
    hid = _gelu(first + pltpu.roll(second, n16 - 1, 0) + peb)
    out = _dot(hid.astype(BF16), w2_ref[0])
    o_ref[0, 0, 0] = jnp.where(kind == 0, _rms(out, g_ref[...]), out).astype(BF16)


def _compress(kc, vc, batch, seq, pe_k, w1_k, w2_k, pe_v, w1_v, w2_v, kcmp_norm):
    n16 = seq // CMP_STRIDE
    half = CMP_STRIDE * HEAD_DIM

    def flat(x):
        return x.reshape(batch, n16, CMP_STRIDE, NSA_KV_HEADS, HEAD_DIM).transpose(0, 3, 1, 2, 4).reshape(
            batch, NSA_KV_HEADS, n16, half)

    c = jnp.stack([flat(kc), flat(vc)])
    w1 = jnp.stack([w1_k, w1_v]).astype(BF16)
    w2 = jnp.stack([w2_k, w2_v]).astype(BF16)
    pe = jnp.stack([pe_k, pe_v]).reshape(2, 1, 2 * half)
    pe = jnp.broadcast_to(pe, (2, 8, 2 * half)).astype(BF16)
    return pl.pallas_call(
        _compress_kernel,
        grid=(2, batch, NSA_KV_HEADS),
        in_specs=[pl.BlockSpec((1, 1, 1, n16, half), lambda k, b, h: (k, b, h, 0, 0)),
                  pl.BlockSpec((1, 2 * half, CMP_HIDDEN), lambda k, b, h: (k, 0, 0)),
                  pl.BlockSpec((1, CMP_HIDDEN, HEAD_DIM), lambda k, b, h: (k, 0, 0)),
                  pl.BlockSpec((1, 8, 2 * half), lambda k, b, h: (k, 0, 0)),
                  _full((1, HEAD_DIM))],
        out_specs=pl.BlockSpec((1, 1, 1, n16, HEAD_DIM), lambda k, b, h: (k, b, h, 0, 0)),
        out_shape=jax.ShapeDtypeStruct((2, batch, NSA_KV_HEADS, n16, HEAD_DIM), BF16),
        compiler_params=_params("arbitrary", "arbitrary", "arbitrary"),
        name="nsa_compress",
    )(c, w1, w2, pe, kcmp_norm.astype(F32).reshape(1, HEAD_DIM))


M_INIT = -1e30


LANES = 128


def _softmax_init(m_ref, acc_ref):
    m_ref[...] = jnp.full(m_ref.shape, M_INIT, F32)
    acc_ref[...] = jnp.zeros(acc_ref.shape, F32)


def _softmax_step(s, v_aug, m_ref, acc_ref):
    m_old = m_ref[...]
    m_new = jnp.maximum(m_old, jnp.max(s, axis=-1, keepdims=True))
    alpha = jnp.exp(m_old - m_new)
    p = jnp.exp(s - jnp.tile(m_new, (1, s.shape[1] // LANES)))
    acc_ref[...] = alpha * acc_ref[...] + _dot(p.astype(BF16), v_aug)
    m_ref[...] = m_new


def _past_keys_loop(n_keys, tile, step):
    n_full = n_keys // tile

    def body(j, carry):
        step(pl.multiple_of(j * tile, tile), tile)
        return carry

    lax.fori_loop(0, n_full, body, 0)
    rest = n_keys - n_full * tile
    start = pl.multiple_of(n_full * tile, tile)

    @pl.when(rest > tile // 2)
    def _():
        step(start, tile)

    @pl.when((rest > 0) & (rest <= tile // 2))
    def _():
        step(start, tile // 2)


def _softmax_result(acc_ref):
    acc = acc_ref[...]
    return acc[:, 0:HEAD_DIM] * (1.0 / acc[:, HEAD_DIM:HEAD_DIM + 1])


def _pick_top(score, pos, width, k):
    sel = jnp.zeros(score.shape, jnp.bool_)
    for _ in range(k):
        m, idx = _first_max(score, pos, width, axis=0)
        hit = (pos == idx) & (m > -jnp.inf)
        sel = sel | hit
        score = jnp.where(pos == idx, -jnp.inf, score)
    return sel


MOBA_LANES = 64
MOBA_TK = 2048


def _moba_kernel(q_ref, ka_ref, va_ref, kmean_ref, o_ref, qa_ref, m_ref, acc_ref):
    i = pl.program_id(2)
    q = q_ref[0, 0]
    q_hi, q_lo = _split(q)
    km_hi, km_lo = _split(kmean_ref[0, 0])
    gate = _dot_nt(km_hi, q_hi) + (_dot_nt(km_lo, q_hi) + _dot_nt(km_hi, q_lo))
    blk = lax.broadcasted_iota(jnp.int32, gate.shape, 0)
    sel = _pick_top(jnp.where(blk < i, gate, -jnp.inf), blk, gate.shape[0], MOBA_TOPK)
    qs = (q * (HEAD_DIM ** -0.5)).astype(BF16)
    bias = jnp.where(sel, 0.0, NEG).T[:, 0:MOBA_LANES]
    qa_ref[...] = jnp.concatenate([qs, bias.astype(BF16)], axis=1)
    _softmax_init(m_ref, acc_ref)

    def step(start, size):
        s = _dot_nt(qa_ref[...], ka_ref[0, 0, pl.ds(start, size), :])
        _softmax_step(s, va_ref[0, 0, pl.ds(start, size), :], m_ref, acc_ref)

    _past_keys_loop(i * MOBA_BLOCK, MOBA_TK, step)
    start = pl.multiple_of(i * MOBA_BLOCK, MOBA_BLOCK)
    s = _dot_nt(qs, ka_ref[0, 0, pl.ds(start, MOBA_BLOCK), :][:, 0:HEAD_DIM])
    qpos = lax.broadcasted_iota(jnp.int32, s.shape, 0)
    kpos = lax.broadcasted_iota(jnp.int32, s.shape, 1)
    _softmax_step(jnp.where(kpos <= qpos, s, NEG), va_ref[0, 0, pl.ds(start, MOBA_BLOCK), :], m_ref, acc_ref)
    o_ref[0, 0] = _softmax_result(acc_ref).astype(BF16)


def _moba(qm, ka, kmean, va, batch, seq):
    nmb = seq // MOBA_BLOCK
    assert nmb <= MOBA_LANES and seq % MOBA_TK == 0
    kmean = kmean.reshape(batch, nmb, MOBA_HEADS, HEAD_DIM).transpose(0, 2, 1, 3)
    kmean = jnp.pad(kmean, ((0, 0), (0, 0), (0, LANES - nmb), (0, 0)))
    return pl.pallas_call(
        _moba_kernel,
        grid=(batch, MOBA_HEADS, nmb),
        in_specs=[pl.BlockSpec((1, 1, MOBA_BLOCK, HEAD_DIM), lambda b, h, i: (b, h, i, 0)),
                  pl.BlockSpec((1, 1, seq, LANES), lambda b, h, i: (b, h, 0, 0)),
                  pl.BlockSpec((1, 1, seq, LANES), lambda b, h, i: (b, h, 0, 0)),
                  pl.BlockSpec((1, 1, LANES, HEAD_DIM), lambda b, h, i: (b, h, 0, 0))],
        out_specs=pl.BlockSpec((1, 1, MOBA_BLOCK, HEAD_DIM), lambda b, h, i: (b, h, i, 0)),
        out_shape=jax.ShapeDtypeStruct((batch, MOBA_HEADS, seq, HEAD_DIM), BF16),
        scratch_shapes=[pltpu.VMEM((MOBA_BLOCK, LANES), BF16), pltpu.VMEM((MOBA_BLOCK, LANES), F32),
                        pltpu.VMEM((MOBA_BLOCK, LANES), F32)],
        compiler_params=_params("arbitrary", "arbitrary", "arbitrary"),
        name="moba",
    )(qm, ka, va, kmean)


NSA_TQ = 128
NSA_TK = 2048
SEL_LANES = 64
SUPER_KEYS = SEL_LANES * SEL_BLOCK
CMP_WIDTH_STEP = 256


def _nsa_kernel(n_super, q_ref, kc_ref, vc_ref, ka_ref, va_ref, kvw_ref, gt_ref, e_ref, band_ref, o_ref,
                qa_ref, m_ref, acc_ref):
    qi = pl.program_id(2)
    tq = NSA_TQ
    s0 = qi * tq
    q = q_ref[0, 0, 0]

    def compressed(width):
        sc = _dot_nt(q, kc_ref[0, 0, 0:width, :])
        rq = lax.broadcasted_iota(jnp.int32, sc.shape, 0) & (tq - 1)
        n = lax.broadcasted_iota(jnp.int32, sc.shape, 1)
        vis = n * CMP_STRIDE + (CMP_BLOCK - 1) <= s0 + rq
        sc = jnp.where(vis, sc, NEG)
        pc = jnp.where(vis, jnp.exp(sc - jnp.max(sc, axis=-1, keepdims=True)), 0.0)
        pc = pc * (1.0 / jnp.maximum(jnp.sum(pc, axis=-1, keepdims=True), 1e-30))
        imp = pc[0:tq]
        for g in range(1, NSA_GROUP):
            imp = imp + pc[g * tq:(g + 1) * tq]
        return _dot(pc.astype(BF16), vc_ref[0, 0, 0:width, :]), _dot_x2(imp, band_ref[0:width, :])

    n16 = kc_ref.shape[2]
    widths = list(range(CMP_WIDTH_STEP, n16, CMP_WIDTH_STEP)) + [n16]
    o_c, pslc = lax.switch((s0 + tq - 1) // (CMP_WIDTH_STEP * CMP_STRIDE),
                           [functools.partial(compressed, wd) for wd in widths])

    wlen = WINDOW + tq

    def window(kvw, masked):
        sw = masked(_dot_nt(q, kvw[:, 0:HEAD_DIM]))
        pw = jnp.exp(sw - jnp.max(sw, axis=-1, keepdims=True))
        return _dot(pw.astype(BF16), kvw)[:, HEAD_DIM:2 * HEAD_DIM] * (1.0 / jnp.sum(pw, axis=-1, keepdims=True))

    def window_interior():
        def masked(sw):
            rq = lax.broadcasted_iota(jnp.int32, (sw.shape[0], tq), 0) & (tq - 1)
            c = lax.broadcasted_iota(jnp.int32, (sw.shape[0], tq), 1)
            return jnp.concatenate([jnp.where(c > rq, sw[:, 0:tq], NEG), sw[:, tq:WINDOW],
                                    jnp.where(c <= rq, sw[:, WINDOW:wlen], NEG)], axis=1)

        return window(kvw_ref[0, 0, pl.ds(pl.multiple_of(s0 - WINDOW, tq), wlen), :], masked)

    def window_start():
        def masked(sw):
            kabs = lax.broadcasted_iota(jnp.int32, sw.shape, 1)
            t = s0 + (lax.broadcasted_iota(jnp.int32, sw.shape, 0) & (tq - 1))
            return jnp.where((kabs <= t) & (kabs > t - WINDOW), sw, NEG)

        return window(kvw_ref[0, 0, 0:wlen, :], masked)

    o_w = lax.cond(s0 >= WINDOW, window_interior, window_start)

    pslc_t = pslc.T
    nb = pslc_t.shape[0]
    blk = lax.broadcasted_iota(jnp.int32, pslc_t.shape, 0)
    cur = (s0 + lax.broadcasted_iota(jnp.int32, pslc_t.shape, 1)) // SEL_BLOCK
    c0 = s0 // SEL_BLOCK
    elig = (blk >= 1) & (blk <= cur - 2)
    sel = _pick_top(jnp.where(elig, pslc_t, -jnp.inf), blk, nb, SEL_TOPK - 3)
    sel = sel | (((blk == 0) | (blk == cur - 1)) & (blk < c0))
    bias = jnp.where(sel, 0.0, NEG).T.astype(BF16)
    for st in range(n_super):
        b = bias[:, st * SEL_LANES:(st + 1) * SEL_LANES]
        qa_ref[st] = jnp.concatenate([q, jnp.concatenate([b] * NSA_GROUP, axis=0)], axis=1)

    _softmax_init(m_ref, acc_ref)

    def step(start, size):
        s = _dot_nt(qa_ref[start // SUPER_KEYS], ka_ref[0, 0, pl.ds(start, size), :])
        _softmax_step(s, va_ref[0, 0, pl.ds(start, size), :], m_ref, acc_ref)

    _past_keys_loop(s0, NSA_TK, step)
    d0 = pl.multiple_of(s0, tq)
    s = _dot_nt(q, ka_ref[0, 0, pl.ds(d0, tq), :][:, 0:HEAD_DIM])
    rqd = lax.broadcasted_iota(jnp.int32, s.shape, 0) & (tq - 1)
    col = lax.broadcasted_iota(jnp.int32, s.shape, 1)
    _softmax_step(jnp.where(col <= rqd, s, NEG), va_ref[0, 0, pl.ds(d0, tq), :], m_ref, acc_ref)
    o_s = _softmax_result(acc_ref)

    w = NSA_GROUP * HEAD_DIM
    gexp = _dot_x2(gt_ref[...], e_ref[0])
    wide = lambda x: jnp.concatenate([x[g * tq:(g + 1) * tq] for g in range(NSA_GROUP)], axis=1)
    o_ref[...] = (gexp[:, 0:w] * wide(o_c) + gexp[:, w:2 * w] * wide(o_s) + gexp[:, 2 * w:3 * w] * wide(o_w)).astype(BF16)


def _nsa(qs, kcmp, vcmp, ka, va, kvw, gates, batch, seq):
    tq = NSA_TQ
    nq = seq // tq
    nb = seq // SEL_BLOCK
    n16 = seq // CMP_STRIDE
    assert seq % SUPER_KEYS == 0
    n_super = seq // SUPER_KEYS
    rows = NSA_GROUP * tq
    e = np.zeros((NSA_KV_HEADS, 128, 3 * NSA_GROUP * HEAD_DIM), np.float32)
    for br in range(3):
        for hk in range(NSA_KV_HEADS):
            for g in range(NSA_GROUP):
                c = (br * NSA_GROUP + g) * HEAD_DIM
                e[hk, br * NSA_HEADS + hk * NSA_GROUP + g, c:c + HEAD_DIM] = 1.0
    nn, jj = np.arange(n16)[:, None], np.arange(nb)[None, :]
    band = ((nn >= 4 * jj - 1) & (nn <= 4 * jj + 3)).astype(np.float32)
    resident = lambda width: pl.BlockSpec((1, 1, seq, width), lambda b, h, i: (b, h, 0, 0))
    w = NSA_GROUP * HEAD_DIM
    return pl.pallas_call(
        functools.partial(_nsa_kernel, n_super),
        grid=(batch, NSA_KV_HEADS, nq),
        in_specs=[pl.BlockSpec((1, 1, 1, rows, HEAD_DIM), lambda b, h, i: (b, h, i, 0, 0)),
                  pl.BlockSpec((1, 1, n16, HEAD_DIM), lambda b, h, i: (b, h, 0, 0)),
                  pl.BlockSpec((1, 1, n16, HEAD_DIM), lambda b, h, i: (b, h, 0, 0)),
                  resident(LANES), resident(LANES), resident(LANES),
                  pl.BlockSpec((tq, 128), lambda b, h, i: (b * nq + i, 0)),
                  pl.BlockSpec((1, 128, 3 * w), lambda b, h, i: (h, 0, 0)),
                  pl.BlockSpec((n16, nb), lambda b, h, i: (0, 0))],
        out_specs=pl.BlockSpec((tq, w), lambda b, h, i: (b * nq + i, h)),
        out_shape=jax.ShapeDtypeStruct((batch * seq, NSA_W), BF16),
        scratch_shapes=[pltpu.VMEM((n_super, rows, LANES), BF16), pltpu.VMEM((rows, LANES), F32),
                        pltpu.VMEM((rows, LANES), F32)],
        compiler_params=_params("arbitrary", "arbitrary", "arbitrary"),
        name="nsa",
    )(qs, kcmp, vcmp, ka, va, kvw, gates, jnp.asarray(e, BF16), jnp.asarray(band, BF16))


def _odd_out_kernel(om_ref, on_ref, h_ref, w_ref, o_ref):
    acc = h_ref[...] + _dot(on_ref[...], w_ref[MOBA_W:D_MODEL, :])
    for h in range(MOBA_HEADS):
        acc = acc + _dot(om_ref[0, h], w_ref[h * HEAD_DIM:(h + 1) * HEAD_DIM, :])
    o_ref[...] = acc


def _odd_out(o_moba, o_nsa, h2, w_out, seq, tm=512):
    t = h2.shape[0]
    tps = seq // tm
    row = lambda w: pl.BlockSpec((tm, w), lambda i: (i, 0))
    return pl.pallas_call(
        _odd_out_kernel,
        grid=(t // tm,),
        in_specs=[pl.BlockSpec((1, MOBA_HEADS, tm, HEAD_DIM), lambda i: (i // tps, 0, i % tps, 0)),
                  row(NSA_W), row(D_MODEL), _full((D_MODEL, D_MODEL))],
        out_specs=row(D_MODEL),
        out_shape=jax.ShapeDtypeStruct((t, D_MODEL), F32),
        compiler_params=_params("arbitrary"),
        name="odd_out",
    )(o_moba, o_nsa, h2, w_out)


def _even_layer(h2, batch, seq, norm, w_in, w_out, lam_re, lam_im, log_dt, b_re, b_im, c_re, c_im, d, w_glu, conv_w, conv_b):
    u, ub, yb = _even_in(h2, norm.reshape(1, D_MODEL), w_in.astype(BF16), conv_w, conv_b.reshape(1, CONV_WIDTH), seq)
    ypre = _s5_mixer_pre(ub, batch, seq, lam_re, lam_im, log_dt, b_re, b_im, c_re, c_im)
    return _even_out(ypre, u, yb, h2, d.reshape(1, S5_WIDTH), w_glu.astype(BF16), w_out.astype(BF16))


def _odd_layer(h2, batch, seq, norm, w_in, w_out, moba_q_norm, moba_k_norm, nsa_q_norm, nsa_kcmp_norm, nsa_ksel_norm,
               nsa_kwin_norm, cmp_pe_k, cmp_w1_k, cmp_w2_k, cmp_pe_v, cmp_w1_v, cmp_w2_v):
    qm, kam, kmean, vam, qs, kc, vc, kas, vas, kvw, gates = _odd_in(
        h2, batch, seq, norm.reshape(1, D_MODEL), w_in, moba_q_norm, moba_k_norm, nsa_q_norm, nsa_ksel_norm, nsa_kwin_norm)
    cmp = _compress(kc, vc, batch, seq, cmp_pe_k, cmp_w1_k, cmp_w2_k, cmp_pe_v, cmp_w1_v, cmp_w2_v, nsa_kcmp_norm)
    o_moba = _moba(qm, kam, kmean, vam, batch, seq)
    o_nsa = _nsa(qs, cmp[0], cmp[1], kas, vas, kvw, gates, batch, seq)
    return _odd_out(o_moba, o_nsa, h2, w_out.astype(BF16), seq)


def kernel(x, ev_norm_mix, ev_w_in, ev_w_out, s5_lam_re, s5_lam_im, s5_log_dt, s5_b_re, s5_b_im, s5_c_re, s5_c_im, s5_d, s5_w_glu, conv_w, conv_b, od_norm_mix, od_w_in, od_w_out, moba_q_norm, moba_k_norm, nsa_q_norm, nsa_kcmp_norm, nsa_ksel_norm, nsa_kwin_norm, cmp_pe_k, cmp_w1_k, cmp_w2_k, cmp_pe_v, cmp_w1_v, cmp_w2_v, moe_norm, moe_w_group, moe_b_group, moe_w_expert, moe_b_expert, moe_w_gate, moe_w_up, moe_w_down):
    batch, seq, _ = x.shape
    depth = moe_norm.shape[0]
    h = x.reshape(batch * seq, D_MODEL)
    for layer in range(depth):
        i = layer // 2
        if layer % 2 == 0:
            h = _even_layer(h, batch, seq, ev_norm_mix[i], ev_w_in[i], ev_w_out[i], s5_lam_re[i], s5_lam_im[i], s5_log_dt[i],
                            s5_b_re[i], s5_b_im[i], s5_c_re[i], s5_c_im[i], s5_d[i], s5_w_glu[i], conv_w[i], conv_b[i])
        else:
            h = _odd_layer(h, batch, seq, od_norm_mix[i], od_w_in[i], od_w_out[i], moba_q_norm[i], moba_k_norm[i],
                           nsa_q_norm[i], nsa_kcmp_norm[i], nsa_ksel_norm[i], nsa_kwin_norm[i], cmp_pe_k[i], cmp_w1_k[i],
                           cmp_w2_k[i], cmp_pe_v[i], cmp_w1_v[i], cmp_w2_v[i])
        h = _moe(h, moe_norm[layer].reshape(1, D_MODEL), moe_w_group[layer], moe_b_group[layer], moe_w_expert[layer],
                 moe_b_expert[layer], moe_w_gate[layer], moe_w_up[layer], moe_w_down[layer])
    return h.reshape(batch, seq, D_MODEL)
```

```python
import functools
import math

import jax
import jax.numpy as jnp
import numpy as np
from jax import lax
from jax.experimental import pallas as pl
from jax.experimental.pallas import tpu as pltpu

D_MODEL = 1024
HEAD_DIM = 64
EPS = 1e-6
S5_WIDTH = 256
S5_GROUP = 16
S5_GROUPS = 16
S5_STATE = 64
S5_CHUNK = 16
CONV_WIDTH = 768
CONV_K = 3
MOBA_HEADS = 4
NSA_HEADS = 12
NSA_KV_HEADS = 2
NSA_GROUP = 6
MOBA_W = 256
NSA_W = 768
KV_W = 128
MOBA_BLOCK = 256
MOBA_TOPK = 3
CMP_BLOCK = 32
CMP_STRIDE = 16
CMP_HIDDEN = 256
SEL_BLOCK = 64
SEL_TOPK = 8
WINDOW = 512
N_GROUPS = 4
EXPERTS_PER_GROUP = 4
N_EXPERTS = 16
EXPERT_FF = 256

VMEM_LIMIT_BYTES = 56 * 1024 * 1024
NEG = -float(2 ** 30)
F32 = jnp.float32
BF16 = jnp.bfloat16


def _params(*semantics):
    return pltpu.CompilerParams(dimension_semantics=semantics, vmem_limit_bytes=VMEM_LIMIT_BYTES)


def _dot(a, b):
    return jnp.dot(a, b, preferred_element_type=F32)


def _dot_nt(a, b):
    return lax.dot_general(a, b, (((1,), (1,)), ((), ())), preferred_element_type=F32)


def _split(x):
    hi = x.astype(BF16)
    lo = (x - hi.astype(F32)).astype(BF16)
    return hi, lo


def _dot_x2(x, w):
    hi, lo = _split(x)
    return _dot(hi, w) + _dot(lo, w)


def _dot_x3(x, w_hi, w_lo):
    hi, lo = _split(x)
    return _dot(hi, w_hi) + (_dot(hi, w_lo) + _dot(lo, w_hi))


def _rms(x, gain):
    return x * lax.rsqrt(jnp.mean(x * x, axis=-1, keepdims=True) + EPS) * gain


def _gelu(x):
    return 0.5 * x * (1.0 + jnp.tanh(math.sqrt(2.0 / math.pi) * (x + 0.044715 * (x * x * x))))


def _sigmoid(x):
    return 1.0 / (1.0 + jnp.exp(-x))


def _full(shape):
    n = len(shape)
    return pl.BlockSpec(shape, lambda *_: (0,) * n)


def _even_in_kernel(tiles_per_seq, x_ref, g_ref, w_ref, cw_ref, cb_ref, u_ref, ub_ref, yb_ref, carry_ref):
    i = pl.program_id(0)
    xn = _rms(x_ref[...], g_ref[...]).astype(BF16)
    u = _dot(xn, w_ref[:, 0:S5_WIDTH])
    u_ref[...] = u
    ub_ref[...] = u.astype(BF16)
    o = S5_WIDTH
    xc = _dot(xn, w_ref[:, o:o + CONV_WIDTH])
    gb = _dot(xn, w_ref[:, o + CONV_WIDTH:o + 2 * CONV_WIDTH])
    gc = _dot(xn, w_ref[:, o + 2 * CONV_WIDTH:o + 3 * CONV_WIDTH])
    z = gc * xc
    tm = z.shape[0]

    @pl.when(i % tiles_per_seq == 0)
    def _():
        carry_ref[...] = jnp.zeros_like(carry_ref)

    row = lax.broadcasted_iota(jnp.int32, z.shape, 0)
    prev1 = carry_ref[7:8, :]
    prev2 = carry_ref[6:7, :]
    z1 = jnp.where(row == 0, prev1, pltpu.roll(z, 1, 0))
    z2 = jnp.where(row == 0, prev2, jnp.where(row == 1, prev1, pltpu.roll(z, 2, 0)))
    y = cw_ref[0:1, :] * z2 + cw_ref[1:2, :] * z1 + cw_ref[2:3, :] * z + cb_ref[...]
    yb_ref[...] = (gb * y).astype(BF16)
    carry_ref[...] = z[tm - 8:tm, :]


def _even_in(x2, gain, w_in, conv_w, conv_b, seq, tm=512):
    t = x2.shape[0]
    n_in = w_in.shape[1]
    return pl.pallas_call(
        functools.partial(_even_in_kernel, seq // tm),
        grid=(t // tm,),
        in_specs=[pl.BlockSpec((tm, D_MODEL), lambda i: (i, 0)), _full((1, D_MODEL)),
                  _full((D_MODEL, n_in)), _full((CONV_K, CONV_WIDTH)), _full((1, CONV_WIDTH))],
        out_specs=[pl.BlockSpec((tm, S5_WIDTH), lambda i: (i, 0)), pl.BlockSpec((tm, S5_WIDTH), lambda i: (i, 0)),
                   pl.BlockSpec((tm, CONV_WIDTH), lambda i: (i, 0))],
        out_shape=[jax.ShapeDtypeStruct((t, S5_WIDTH), F32), jax.ShapeDtypeStruct((t, S5_WIDTH), BF16),
                   jax.ShapeDtypeStruct((t, CONV_WIDTH), BF16)],
        scratch_shapes=[pltpu.VMEM((8, CONV_WIDTH), F32)],
        compiler_params=_params("arbitrary"),
        name="even_in",
    )(x2, gain, w_in, conv_w, conv_b)


def _s5_weights(lam_re, lam_im, log_dt, b_re, b_im, c_re, c_im):
    g, p, hg, ck = S5_GROUPS, S5_STATE, S5_GROUP, S5_CHUNK
    lr, li = lam_re.astype(F32), lam_im.astype(F32)
    dt = jnp.exp(log_dt.astype(F32))[:, None]
    mag = jnp.exp(lr * dt)
    a_re, a_im = mag * jnp.cos(li * dt), mag * jnp.sin(li * dt)
    den = lr * lr + li * li
    f_re = ((a_re - 1.0) * lr + a_im * li) / den
    f_im = (a_im * lr - (a_re - 1.0) * li) / den
    br, bi = b_re.astype(F32), b_im.astype(F32)
    bb_re = f_re[..., None] * br - f_im[..., None] * bi
    bb_im = f_re[..., None] * bi + f_im[..., None] * br
    pw_re, pw_im = [jnp.ones_like(a_re)], [jnp.zeros_like(a_im)]
    for _ in range(ck):
        r, m = pw_re[-1], pw_im[-1]
        pw_re.append(r * a_re - m * a_im)
        pw_im.append(r * a_im + m * a_re)
    pw_re, pw_im = jnp.stack(pw_re), jnp.stack(pw_im)
    cr, ci = c_re.astype(F32), c_im.astype(F32)
    rev_re, rev_im = pw_re[ck - 1::-1][:ck], pw_im[ck - 1::-1][:ck]
    ws_re = rev_re[:, :, :, None] * bb_re[None] - rev_im[:, :, :, None] * bb_im[None]
    ws_im = rev_re[:, :, :, None] * bb_im[None] + rev_im[:, :, :, None] * bb_re[None]
    ca_re = cr[None] * pw_re[1:, :, None, :] - ci[None] * pw_im[1:, :, None, :]
    ca_im = cr[None] * pw_im[1:, :, None, :] + ci[None] * pw_re[1:, :, None, :]
    cb_re = jnp.einsum('ghp,kgp,gpj->kghj', cr, pw_re[:ck], bb_re) - jnp.einsum('ghp,kgp,gpj->kghj', cr, pw_im[:ck], bb_im) \
        - jnp.einsum('ghp,kgp,gpj->kghj', ci, pw_re[:ck], bb_im) - jnp.einsum('ghp,kgp,gpj->kghj', ci, pw_im[:ck], bb_re)
    lag = np.arange(ck)[None, :] - np.arange(ck)[:, None]
    tz = cb_re[np.clip(lag, 0, ck - 1)]
    tz = jnp.where((lag >= 0)[:, :, None, None, None], tz, 0.0)
    cw = ck * g * hg
    ws = jnp.stack([ws_re, ws_im]).transpose(1, 2, 4, 0, 3).reshape(cw, 2 * p)
    wc = jnp.stack([ca_re, -ca_im]).transpose(0, 2, 4, 1, 3).reshape(2 * g * p, ck * hg)
    tzc = tz.transpose(0, 2, 4, 1, 3).reshape(cw, ck * hg)
    return ws.astype(BF16), wc.astype(BF16), tzc.astype(BF16), pw_re[ck].reshape(1, g * p), pw_im[ck].reshape(1, g * p)


def _group_expand(compact, expand, row_shift, col_shift, col0):
    full = _dot(compact, expand)
    row = lax.broadcasted_iota(jnp.int32, full.shape, 0)
    col = col0 + lax.broadcasted_iota(jnp.int32, full.shape, 1)
    same = ((row >> row_shift) & (S5_GROUPS - 1)) == ((col >> col_shift) & (S5_GROUPS - 1))
    return jnp.where(same, full, 0.0).astype(BF16)


def _s5_state_kernel(u_ref, ws_ref, e_ref, s_ref, w_scr):
    @pl.when(pl.program_id(1) == 0)
    def _():
        w_scr[...] = _group_expand(ws_ref[...], e_ref[...], 4, 6, pl.program_id(0) * w_scr.shape[1])

    s_ref[...] = _dot(u_ref[...], w_scr[...])


def _s5_scan_kernel(s_ref, are_ref, aim_ref, xprev_ref, st_ref):
    @pl.when(pl.program_id(0) == 0)
    def _():
        st_ref[...] = jnp.zeros_like(st_ref)

    a_re, a_im = are_ref[...], aim_ref[...]
    nb, n = s_ref.shape[0], s_ref.shape[1]
    half = a_re.shape[1]

    def body(c, carry):
        out = []
        for b in range(nb):
            xr, xi = carry[2 * b], carry[2 * b + 1]
            xprev_ref[b, pl.ds(c, 1), 0:half] = xr
            xprev_ref[b, pl.ds(c, 1), half:2 * half] = xi
            s = s_ref[b, pl.ds(c, 1), :]
            out += [a_re * xr - a_im * xi + s[:, 0:half], a_re * xi + a_im * xr + s[:, half:2 * half]]
        return tuple(out)

    init = tuple(st_ref[b:b + 1, o:o + half] for b in range(nb) for o in (0, half))
    final = lax.fori_loop(0, n, body, init, unroll=4)
    for b in range(nb):
        st_ref[b:b + 1, 0:half] = final[2 * b]
        st_ref[b:b + 1, half:2 * half] = final[2 * b + 1]


def _s5_out_kernel(u_ref, xp_ref, tz_ref, wc_ref, e_ref, y_ref, tz_scr, wc_scr):
    @pl.when(pl.program_id(1) == 0)
    def _():
        col0 = pl.program_id(0) * tz_scr.shape[1]
        tz_scr[...] = _group_expand(tz_ref[...], e_ref[...], 4, 4, col0)
        wc_scr[...] = _group_expand(wc_ref[...], e_ref[...], 6, 4, col0)

    y_ref[...] = _dot(u_ref[...], tz_scr[...]) + _dot(xp_ref[...].astype(BF16), wc_scr[...])


def _expand_matrix(outer, inner):
    e = np.zeros((outer, inner, outer, S5_GROUPS, inner), np.float32)
    for x in range(outer):
        for y in range(inner):
            e[x, y, x, :, y] = 1.0
    return jnp.asarray(e.reshape(outer * inner, outer * S5_GROUPS * inner), BF16)


def _s5_mixer_pre(ub, batch, seq, lam_re, lam_im, log_dt, b_re, b_im, c_re, c_im):
    ws, wc, tz, a16_re, a16_im = _s5_weights(lam_re, lam_im, log_dt, b_re, b_im, c_re, c_im)
    nc = seq // S5_CHUNK
    rows = batch * nc
    cw = S5_CHUNK * S5_WIDTH
    sw = 2 * S5_GROUPS * S5_STATE
    tr = min(rows, 512)
    tn = 512
    uc = ub.reshape(rows, cw)
    e_state = _expand_matrix(2, S5_STATE)
    e_out = _expand_matrix(S5_CHUNK, S5_GROUP)
    s = pl.pallas_call(
        _s5_state_kernel,
        grid=(sw // tn, rows // tr),
        in_specs=[pl.BlockSpec((tr, cw), lambda j, i: (i, 0)), _full(ws.shape),
                  pl.BlockSpec((e_state.shape[0], tn), lambda j, i: (0, j))],
        out_specs=pl.BlockSpec((tr, tn), lambda j, i: (i, j)),
        out_shape=jax.ShapeDtypeStruct((rows, sw), F32),
        scratch_shapes=[pltpu.VMEM((cw, tn), BF16)],
        compiler_params=_params("arbitrary", "arbitrary"),
        name="s5_state",
    )(uc, ws, e_state)
    tc = min(nc, 256)
    xprev = pl.pallas_call(
        _s5_scan_kernel,
        grid=(nc // tc,),
        in_specs=[pl.BlockSpec((batch, tc, sw), lambda i: (0, i, 0)), _full((1, sw // 2)), _full((1, sw // 2))],
        out_specs=pl.BlockSpec((batch, tc, sw), lambda i: (0, i, 0)),
        out_shape=jax.ShapeDtypeStruct((batch, nc, sw), F32),
        scratch_shapes=[pltpu.VMEM((batch, sw), F32)],
        compiler_params=_params("arbitrary"),
        name="s5_scan",
    )(s.reshape(batch, nc, sw), a16_re, a16_im)
    y = pl.pallas_call(
        _s5_out_kernel,
        grid=(cw // tn, rows // tr),
        in_specs=[pl.BlockSpec((tr, cw), lambda j, i: (i, 0)), pl.BlockSpec((tr, sw), lambda j, i: (i, 0)),
                  _full(tz.shape), _full(wc.shape), pl.BlockSpec((e_out.shape[0], tn), lambda j, i: (0, j))],
        out_specs=pl.BlockSpec((tr, tn), lambda j, i: (i, j)),
        out_shape=jax.ShapeDtypeStruct((rows, cw), F32),
        scratch_shapes=[pltpu.VMEM((cw, tn), BF16), pltpu.VMEM((sw, tn), BF16)],
        compiler_params=_params("arbitrary", "arbitrary"),
        name="s5_out",
    )(uc, xprev.reshape(rows, sw), tz, wc, e_out)
    return y.reshape(batch * seq, S5_WIDTH)


def _even_out_kernel(ypre_ref, u_ref, yb_ref, x_ref, d_ref, wglu_ref, wout_ref, o_ref):
    y = _gelu(ypre_ref[...] + d_ref[...] * u_ref[...])
    y = y * _sigmoid(_dot(y.astype(BF16), wglu_ref[...]))
    o_ref[...] = (x_ref[...] + _dot(y.astype(BF16), wout_ref[0:S5_WIDTH, :])
                  + _dot(yb_ref[...], wout_ref[S5_WIDTH:D_MODEL, :]))


def _even_out(ypre, u, yb, x2, d, w_glu, w_out, tm=512):
    t = x2.shape[0]
    row = lambda w: pl.BlockSpec((tm, w), lambda i: (i, 0))
    return pl.pallas_call(
        _even_out_kernel,
        grid=(t // tm,),
        in_specs=[row(S5_WIDTH), row(S5_WIDTH), row(CONV_WIDTH), row(D_MODEL), _full((1, S5_WIDTH)),
                  _full((S5_WIDTH, S5_WIDTH)), _full((D_MODEL, D_MODEL))],
        out_specs=row(D_MODEL),
        out_shape=jax.ShapeDtypeStruct((t, D_MODEL), F32),
        compiler_params=_params("arbitrary"),
        name="even_out",
    )(ypre, u, yb, x2, d, w_glu, w_out)


def _first_max(v, pos, width, axis=-1):
    m = jnp.max(v, axis=axis, keepdims=True)
    idx = jnp.min(jnp.where(v == m, pos, width), axis=axis, keepdims=True)
    return m, idx


def _moe_kernel(h_ref, g_ref, wr_hi_ref, wr_lo_ref, br_ref, wg_ref, wu_ref, wd_ref, o_ref, xn_ref, gate_ref):
    e = pl.program_id(1)

    @pl.when(e == 0)
    def _():
        h = h_ref[...]
        xn = _rms(h, g_ref[...])
        xn_ref[...] = xn.astype(BF16)
        o_ref[...] = h
        logits = _dot_x3(xn, wr_hi_ref[...], wr_lo_ref[...]) + br_ref[...]
        lane = lax.broadcasted_iota(jnp.int32, logits.shape, 1)
        width = logits.shape[1]
        is_g = lane < N_GROUPS
        gl = jnp.where(is_g, logits, -jnp.inf)
        gm, gi = _first_max(gl, lane, width)
        gw = 1.0 / jnp.sum(jnp.where(is_g, jnp.exp(gl - gm), 0.0), axis=-1, keepdims=True)
        lo = N_GROUPS + gi * EXPERTS_PER_GROUP
        in_grp = (lane >= lo) & (lane < lo + EXPERTS_PER_GROUP)
        el = jnp.where(in_grp, logits, -jnp.inf)
        m1, i1 = _first_max(el, lane, width)
        m2, i2 = _first_max(jnp.where(lane == i1, -jnp.inf, el), lane, width)
        p2 = jnp.exp(m2 - m1)
        w1 = gw / (1.0 + p2)
        w2 = gw * p2 / (1.0 + p2)
        gate_ref[...] = jnp.where(lane == i1, w1, 0.0) + jnp.where(lane == i2, w2, 0.0)

    xn = xn_ref[...]
    lane = lax.broadcasted_iota(jnp.int32, gate_ref.shape, 1)
    gate = gate_ref[...]
    y = None
    for j in range(EXPERTS_PER_GROUP):
        ge = jnp.sum(jnp.where(lane == e * EXPERTS_PER_GROUP + (j + N_GROUPS), gate, 0.0), axis=-1, keepdims=True)
        h1 = _dot(xn, wg_ref[j])
        h3 = _dot(xn, wu_ref[j])
        act = (h1 * _sigmoid(h1)) * h3 * ge
        yj = _dot(act.astype(BF16), wd_ref[j])
        y = yj if y is None else y + yj
    o_ref[...] += y


def _moe(h2, gain, w_group, b_group, w_expert, b_expert, w_gate, w_up, w_down, tm=1024):
    t = h2.shape[0]
    rw = 128
    wr = jnp.zeros((D_MODEL, rw), F32).at[:, 0:N_GROUPS].set(w_group).at[:, N_GROUPS:N_GROUPS + N_EXPERTS].set(w_expert)
    br = jnp.zeros((1, rw), F32).at[0, 0:N_GROUPS].set(b_group).at[0, N_GROUPS:N_GROUPS + N_EXPERTS].set(b_expert)
    wr_hi = wr.astype(BF16)
    wr_lo = (wr - wr_hi.astype(F32)).astype(BF16)
    return pl.pallas_call(
        _moe_kernel,
        grid=(t // tm, N_GROUPS),
        in_specs=[pl.BlockSpec((tm, D_MODEL), lambda i, e: (i, 0)), _full((1, D_MODEL)),
                  _full((D_MODEL, rw)), _full((D_MODEL, rw)), _full((1, rw)),
                  pl.BlockSpec((EXPERTS_PER_GROUP, D_MODEL, EXPERT_FF), lambda i, e: (e, 0, 0)),
                  pl.BlockSpec((EXPERTS_PER_GROUP, D_MODEL, EXPERT_FF), lambda i, e: (e, 0, 0)),
                  pl.BlockSpec((EXPERTS_PER_GROUP, EXPERT_FF, D_MODEL), lambda i, e: (e, 0, 0))],
        out_specs=pl.BlockSpec((tm, D_MODEL), lambda i, e: (i, 0)),
        out_shape=jax.ShapeDtypeStruct((t, D_MODEL), F32),
        scratch_shapes=[pltpu.VMEM((tm, D_MODEL), BF16), pltpu.VMEM((tm, rw), F32)],
        compiler_params=_params("arbitrary", "arbitrary"),
        name="moe",
    )(h2, gain, wr_hi, wr_lo, br, w_gate.astype(BF16), w_up.astype(BF16), w_down.astype(BF16))


ODD_SPLITS = (MOBA_W, MOBA_W, MOBA_W, NSA_W, KV_W, KV_W, KV_W, KV_W, KV_W, KV_W, 128)
ODD_IN_PAD = sum(ODD_SPLITS)


def _head_rms(x, hsum, gain):
    w = x.shape[1]
    ss = jnp.concatenate([_dot_x2(x[:, o:o + hsum.shape[0]] * x[:, o:o + hsum.shape[0]], hsum)
                          for o in range(0, w, hsum.shape[0])], axis=1) if w > hsum.shape[0] else _dot_x2(x * x, hsum)
    return x * lax.rsqrt(ss * (1.0 / HEAD_DIM) + EPS) * gain


def _odd_in_kernel(tiles_per_seq, x_ref, g_ref, w_ref, hsum_ref, gq_ref, gk_ref, gnq_ref, gks_ref, gkw_ref,
                   qm_ref, kam_ref, kmean_ref, vam_ref, qs_ref, kc_ref, vc_ref, kas_ref, vas_ref, kvw_ref, gt_ref):
    xn = _rms(x_ref[...], g_ref[...]).astype(BF16)
    offs = np.cumsum((0,) + ODD_SPLITS)
    col = lambda j: _dot(xn, w_ref[:, int(offs[j]):int(offs[j + 1])])
    head = lambda x, h: x[:, h * HEAD_DIM:(h + 1) * HEAD_DIM]
    hsum = hsum_ref[...]
    hsum128 = hsum_ref[0:128, 0:128]
    tm = x_ref.shape[0]
    pos = (pl.program_id(0) % tiles_per_seq) * tm + lax.broadcasted_iota(jnp.int32, (tm, HEAD_DIM), 0)
    lane = lax.broadcasted_iota(jnp.int32, (tm, HEAD_DIM), 1)
    ones_col = (lane == 0).astype(BF16)

    qm = _head_rms(col(0), hsum, gq_ref[...])
    km = _head_rms(col(1), hsum, gk_ref[...])
    for j in range(tm // MOBA_BLOCK):
        kmean_ref[0, j:j + 1, :] = jnp.mean(km[j * MOBA_BLOCK:(j + 1) * MOBA_BLOCK, :], axis=0, keepdims=True)
    km = km.astype(BF16)
    vm = col(2).astype(BF16)
    moba_id = (lane == pos // MOBA_BLOCK).astype(BF16)
    for h in range(MOBA_HEADS):
        qm_ref[0, h] = head(qm, h)
        kam_ref[0, h] = jnp.concatenate([head(km, h), moba_id], axis=1)
        vam_ref[0, h] = jnp.concatenate([head(vm, h), ones_col], axis=1)

    qd = (_head_rms(col(3), hsum, gnq_ref[...]) * (HEAD_DIM ** -0.5)).astype(BF16)
    for hk in range(NSA_KV_HEADS):
        for j in range(tm // NSA_TQ):
            for g in range(NSA_GROUP):
                qs_ref[0, hk, j, g * NSA_TQ:(g + 1) * NSA_TQ, :] = head(qd, hk * NSA_GROUP + g)[j * NSA_TQ:(j + 1) * NSA_TQ, :]
    kc_ref[...] = col(4).astype(BF16)
    vc_ref[...] = col(5).astype(BF16)
    ks = _head_rms(col(6), hsum128, gks_ref[...]).astype(BF16)
    vs = col(7).astype(BF16)
    kw = _head_rms(col(8), hsum128, gkw_ref[...]).astype(BF16)
    vw = col(9).astype(BF16)
    sel_id = (lane == (pos // SEL_BLOCK) % SEL_LANES).astype(BF16)
    for hk in range(NSA_KV_HEADS):
        kas_ref[0, hk] = jnp.concatenate([head(ks, hk), sel_id], axis=1)
        vas_ref[0, hk] = jnp.concatenate([head(vs, hk), ones_col], axis=1)
        kvw_ref[0, hk] = jnp.concatenate([head(kw, hk), head(vw, hk)], axis=1)
    gt_ref[...] = _sigmoid(col(10))


def _odd_in(h2, batch, seq, gain, w_in, moba_q_norm, moba_k_norm, nsa_q_norm, nsa_ksel_norm, nsa_kwin_norm, tm=512):
    t = h2.shape[0]
    assert MOBA_LANES == HEAD_DIM and SEL_LANES == HEAD_DIM and seq % tm == 0 and tm % MOBA_BLOCK == 0
    tps = seq // tm
    w = jnp.pad(w_in, ((0, 0), (0, ODD_IN_PAD - w_in.shape[1]))).astype(BF16)
    hsum = jnp.asarray(np.kron(np.eye(MOBA_W // HEAD_DIM), np.ones((HEAD_DIM, HEAD_DIM))), BF16)
    tile = lambda g, width: jnp.tile(g.astype(F32), width // HEAD_DIM).reshape(1, width)
    row = lambda width: pl.BlockSpec((tm, width), lambda i: (i, 0))
    heads = lambda n, width: pl.BlockSpec((1, n, tm, width), lambda i: (i // tps, 0, i % tps, 0))
    nmb = tm // MOBA_BLOCK
    nqt = tm // NSA_TQ
    rows = NSA_GROUP * NSA_TQ
    sds = jax.ShapeDtypeStruct
    out_specs = [heads(MOBA_HEADS, HEAD_DIM), heads(MOBA_HEADS, LANES), pl.BlockSpec((1, nmb, MOBA_W), lambda i: (i, 0, 0)),
                 heads(MOBA_HEADS, LANES),
                 pl.BlockSpec((1, NSA_KV_HEADS, nqt, rows, HEAD_DIM), lambda i: (i // tps, 0, i % tps, 0, 0)),
                 row(KV_W), row(KV_W), heads(NSA_KV_HEADS, LANES), heads(NSA_KV_HEADS, LANES), heads(NSA_KV_HEADS, LANES),
                 row(128)]
    out_shape = [sds((batch, MOBA_HEADS, seq, HEAD_DIM), F32), sds((batch, MOBA_HEADS, seq, LANES), BF16),
                 sds((t // tm, nmb, MOBA_W), F32), sds((batch, MOBA_HEADS, seq, LANES), BF16),
                 sds((batch, NSA_KV_HEADS, seq // NSA_TQ, rows, HEAD_DIM), BF16),
                 sds((t, KV_W), BF16), sds((t, KV_W), BF16), sds((batch, NSA_KV_HEADS, seq, LANES), BF16),
                 sds((batch, NSA_KV_HEADS, seq, LANES), BF16), sds((batch, NSA_KV_HEADS, seq, LANES), BF16),
                 sds((t, 128), F32)]
    return pl.pallas_call(
        functools.partial(_odd_in_kernel, tps),
        grid=(t // tm,),
        in_specs=[row(D_MODEL), _full((1, D_MODEL)), _full((D_MODEL, ODD_IN_PAD)), _full((MOBA_W, MOBA_W)),
                  _full((1, MOBA_W)), _full((1, MOBA_W)), _full((1, NSA_W)), _full((1, KV_W)), _full((1, KV_W))],
        out_specs=out_specs,
        out_shape=out_shape,
        compiler_params=_params("arbitrary"),
        name="odd_in",
    )(h2, gain, w, hsum, tile(moba_q_norm, MOBA_W), tile(moba_k_norm, MOBA_W), tile(nsa_q_norm, NSA_W),
      tile(nsa_ksel_norm, KV_W), tile(nsa_kwin_norm, KV_W))


def _compress_kernel(c_ref, w1_ref, w2_ref, pe_ref, g_ref, o_ref):
    kind = pl.program_id(0)
    c = c_ref[0, 0, 0]
    half = c.shape[1]
    n16 = c.shape[0]
    first = _dot(c, w1_ref[0, 0:half, :])
    second = _dot(c, w1_ref[0, half:2 * half, :])
    peb = _dot(pe_ref[0], w1_ref[0])[0:1, :]
    hid = _gelu(first + pltpu.roll(second, n16 - 1, 0) + peb)
    out = _dot(hid.astype(BF16), w2_ref[0])
    o_ref[0, 0, 0] = jnp.where(kind == 0, _rms(out, g_ref[...]), out).astype(BF16)


def _compress(kc, vc, batch, seq, pe_k, w1_k, w2_k, pe_v, w1_v, w2_v, kcmp_norm):
    n16 = seq // CMP_STRIDE
    half = CMP_STRIDE * HEAD_DIM

    def flat(x):
        return x.reshape(batch, n16, CMP_STRIDE, NSA_KV_HEADS, HEAD_DIM).transpose(0, 3, 1, 2, 4).reshape(
            batch, NSA_KV_HEADS, n16, half)

    c = jnp.stack([flat(kc), flat(vc)])
    w1 = jnp.stack([w1_k, w1_v]).astype(BF16)
    w2 = jnp.stack([w2_k, w2_v]).astype(BF16)
    pe = jnp.stack([pe_k, pe_v]).reshape(2, 1, 2 * half)
    pe = jnp.broadcast_to(pe, (2, 8, 2 * half)).astype(BF16)
    return pl.pallas_call(
        _compress_kernel,
        grid=(2, batch, NSA_KV_HEADS),
        in_specs=[pl.BlockSpec((1, 1, 1, n16, half), lambda k, b, h: (k, b, h, 0, 0)),
                  pl.BlockSpec((1, 2 * half, CMP_HIDDEN), lambda k, b, h: (k, 0, 0)),
                  pl.BlockSpec((1, CMP_HIDDEN, HEAD_DIM), lambda k, b, h: (k, 0, 0)),
                  pl.BlockSpec((1, 8, 2 * half), lambda k, b, h: (k, 0, 0)),
                  _full((1, HEAD_DIM))],
        out_specs=pl.BlockSpec((1, 1, 1, n16, HEAD_DIM), lambda k, b, h: (k, b, h, 0, 0)),
        out_shape=jax.ShapeDtypeStruct((2, batch, NSA_KV_HEADS, n16, HEAD_DIM), BF16),
        compiler_params=_params("arbitrary", "arbitrary", "arbitrary"),
        name="nsa_compress",
    )(c, w1, w2, pe, kcmp_norm.astype(F32).reshape(1, HEAD_DIM))


M_INIT = -1e30


LANES = 128


def _softmax_init(m_ref, acc_ref):
    m_ref[...] = jnp.full(m_ref.shape, M_INIT, F32)
    acc_ref[...] = jnp.zeros(acc_ref.shape, F32)


def _softmax_step(s, v_aug, m_ref, acc_ref):
    m_old = m_ref[...]
    m_new = jnp.maximum(m_old, jnp.max(s, axis=-1, keepdims=True))
    alpha = jnp.exp(m_old - m_new)
    p = jnp.exp(s - jnp.tile(m_new, (1, s.shape[1] // LANES)))
    acc_ref[...] = alpha * acc_ref[...] + _dot(p.astype(BF16), v_aug)
    m_ref[...] = m_new


def _past_keys_loop(n_keys, tile, step):
    n_full = n_keys // tile

    def body(j, carry):
        step(pl.multiple_of(j * tile, tile), tile)
        return carry

    lax.fori_loop(0, n_full, body, 0)
    rest = n_keys - n_full * tile
    start = pl.multiple_of(n_full * tile, tile)

    @pl.when(rest > tile // 2)
    def _():
        step(start, tile)

    @pl.when((rest > 0) & (rest <= tile // 2))
    def _():
        step(start, tile // 2)


def _softmax_result(acc_ref):
    acc = acc_ref[...]
    return acc[:, 0:HEAD_DIM] * (1.0 / acc[:, HEAD_DIM:HEAD_DIM + 1])


def _pick_top(score, pos, width, k):
    sel = jnp.zeros(score.shape, jnp.bool_)
    for _ in range(k):
        m, idx = _first_max(score, pos, width, axis=0)
        hit = (pos == idx) & (m > -jnp.inf)
        sel = sel | hit
        score = jnp.where(pos == idx, -jnp.inf, score)
    return sel


MOBA_LANES = 64
MOBA_TK = 2048


MOBA_TQ = 2 * MOBA_BLOCK


def _moba_kernel(q_ref, ka_ref, va_ref, kmean_ref, o_ref, qa_ref, m_ref, acc_ref):
    i0 = pl.program_id(2) * (MOBA_TQ // MOBA_BLOCK)
    q = q_ref[0, 0]
    q_hi, q_lo = _split(q)
    km_hi, km_lo = _split(kmean_ref[0, 0])
    gate = _dot_nt(km_hi, q_hi) + (_dot_nt(km_lo, q_hi) + _dot_nt(km_hi, q_lo))
    blk = lax.broadcasted_iota(jnp.int32, gate.shape, 0)
    cur = i0 + lax.broadcasted_iota(jnp.int32, gate.shape, 1) // MOBA_BLOCK
    sel = _pick_top(jnp.where(blk < cur, gate, -jnp.inf), blk, gate.shape[0], MOBA_TOPK)
    qs = (q * (HEAD_DIM ** -0.5)).astype(BF16)
    past = jnp.where(sel & (blk < i0), 0.0, NEG).T[:, 0:MOBA_LANES]
    own = jnp.where((blk == cur) | (sel & (blk >= i0)), 0.0, NEG).T[:, 0:MOBA_LANES]
    qa_ref[0] = jnp.concatenate([qs, past.astype(BF16)], axis=1)
    qa_ref[1] = jnp.concatenate([qs, own.astype(BF16)], axis=1)
    _softmax_init(m_ref, acc_ref)

    def step(start, size):
        s = _dot_nt(qa_ref[0], ka_ref[0, 0, pl.ds(start, size), :])
        _softmax_step(s, va_ref[0, 0, pl.ds(start, size), :], m_ref, acc_ref)

    _past_keys_loop(i0 * MOBA_BLOCK, MOBA_TK, step)
    start = pl.multiple_of(i0 * MOBA_BLOCK, MOBA_TQ)
    s = _dot_nt(qa_ref[1], ka_ref[0, 0, pl.ds(start, MOBA_TQ), :])
    qpos = lax.broadcasted_iota(jnp.int32, s.shape, 0)
    kpos = lax.broadcasted_iota(jnp.int32, s.shape, 1)
    hidden = (qpos // MOBA_BLOCK == kpos // MOBA_BLOCK) & (kpos > qpos)
    _softmax_step(jnp.where(hidden, NEG, s), va_ref[0, 0, pl.ds(start, MOBA_TQ), :], m_ref, acc_ref)
    o_ref[0, 0] = _softmax_result(acc_ref).astype(BF16)


def _moba(qm, ka, kmean, va, batch, seq):
    nmb = seq // MOBA_BLOCK
    assert nmb <= MOBA_LANES and seq % MOBA_TK == 0 and seq % MOBA_TQ == 0
    kmean = kmean.reshape(batch, nmb, MOBA_HEADS, HEAD_DIM).transpose(0, 2, 1, 3)
    kmean = jnp.pad(kmean, ((0, 0), (0, 0), (0, LANES - nmb), (0, 0)))
    return pl.pallas_call(
        _moba_kernel,
        grid=(batch, MOBA_HEADS, seq // MOBA_TQ),
        in_specs=[pl.BlockSpec((1, 1, MOBA_TQ, HEAD_DIM), lambda b, h, i: (b, h, i, 0)),
                  pl.BlockSpec((1, 1, seq, LANES), lambda b, h, i: (b, h, 0, 0)),
                  pl.BlockSpec((1, 1, seq, LANES), lambda b, h, i: (b, h, 0, 0)),
                  pl.BlockSpec((1, 1, LANES, HEAD_DIM), lambda b, h, i: (b, h, 0, 0))],
        out_specs=pl.BlockSpec((1, 1, MOBA_TQ, HEAD_DIM), lambda b, h, i: (b, h, i, 0)),
        out_shape=jax.ShapeDtypeStruct((batch, MOBA_HEADS, seq, HEAD_DIM), BF16),
        scratch_shapes=[pltpu.VMEM((2, MOBA_TQ, LANES), BF16), pltpu.VMEM((MOBA_TQ, LANES), F32),
                        pltpu.VMEM((MOBA_TQ, LANES), F32)],
        compiler_params=_params("arbitrary", "arbitrary", "arbitrary"),
        name="moba",
    )(qm, ka, va, kmean)


NSA_TQ = 128
NSA_TK = 2048
SEL_LANES = 64
SUPER_KEYS = SEL_LANES * SEL_BLOCK
CMP_WIDTH_STEP = 256


def _nsa_kernel(n_super, q_ref, kc_ref, vc_ref, ka_ref, va_ref, kvw_ref, gt_ref, e_ref, band_ref, o_ref,
                qa_ref, m_ref, acc_ref):
    qi = pl.program_id(2)
    tq = NSA_TQ
    s0 = qi * tq
    q = q_ref[0, 0, 0]

    def compressed(width):
        sc = _dot_nt(q, kc_ref[0, 0, 0:width, :])
        rq = lax.broadcasted_iota(jnp.int32, sc.shape, 0) & (tq - 1)
        n = lax.broadcasted_iota(jnp.int32, sc.shape, 1)
        vis = n * CMP_STRIDE + (CMP_BLOCK - 1) <= s0 + rq
        sc = jnp.where(vis, sc, NEG)
        pc = jnp.where(vis, jnp.exp(sc - jnp.max(sc, axis=-1, keepdims=True)), 0.0)
        pc = pc * (1.0 / jnp.maximum(jnp.sum(pc, axis=-1, keepdims=True), 1e-30))
        imp = pc[0:tq]
        for g in range(1, NSA_GROUP):
            imp = imp + pc[g * tq:(g + 1) * tq]
        return _dot(pc.astype(BF16), vc_ref[0, 0, 0:width, :]), _dot_x2(imp, band_ref[0:width, :])

    n16 = kc_ref.shape[2]
    widths = list(range(CMP_WIDTH_STEP, n16, CMP_WIDTH_STEP)) + [n16]
    o_c, pslc = lax.switch((s0 + tq - 1) // (CMP_WIDTH_STEP * CMP_STRIDE),
                           [functools.partial(compressed, wd) for wd in widths])

    wlen = WINDOW + tq

    def window(kvw, masked):
        sw = masked(_dot_nt(q, kvw[:, 0:HEAD_DIM]))
        pw = jnp.exp(sw - jnp.max(sw, axis=-1, keepdims=True))
        return _dot(pw.astype(BF16), kvw)[:, HEAD_DIM:2 * HEAD_DIM] * (1.0 / jnp.sum(pw, axis=-1, keepdims=True))

    def window_interior():
        def masked(sw):
            rq = lax.broadcasted_iota(jnp.int32, (sw.shape[0], tq), 0) & (tq - 1)
            c = lax.broadcasted_iota(jnp.int32, (sw.shape[0], tq), 1)
            return jnp.concatenate([jnp.where(c > rq, sw[:, 0:tq], NEG), sw[:, tq:WINDOW],
                                    jnp.where(c <= rq, sw[:, WINDOW:wlen], NEG)], axis=1)

        return window(kvw_ref[0, 0, pl.ds(pl.multiple_of(s0 - WINDOW, tq), wlen), :], masked)

    def window_start():
        def masked(sw):
            kabs = lax.broadcasted_iota(jnp.int32, sw.shape, 1)
            t = s0 + (lax.broadcasted_iota(jnp.int32, sw.shape, 0) & (tq - 1))
            return jnp.where((kabs <= t) & (kabs > t - WINDOW), sw, NEG)

        return window(kvw_ref[0, 0, 0:wlen, :], masked)

    o_w = lax.cond(s0 >= WINDOW, window_interior, window_start)

    pslc_t = pslc.T
    nb = pslc_t.shape[0]
    blk = lax.broadcasted_iota(jnp.int32, pslc_t.shape, 0)
    cur = (s0 + lax.broadcasted_iota(jnp.int32, pslc_t.shape, 1)) // SEL_BLOCK
    c0 = s0 // SEL_BLOCK
    elig = (blk >= 1) & (blk <= cur - 2)
    sel = _pick_top(jnp.where(elig, pslc_t, -jnp.inf), blk, nb, SEL_TOPK - 3)
    sel = sel | (((blk == 0) | (blk == cur - 1)) & (blk < c0))
    bias = jnp.where(sel, 0.0, NEG).T.astype(BF16)
    for st in range(n_super):
        b = bias[:, st * SEL_LANES:(st + 1) * SEL_LANES]
        qa_ref[st] = jnp.concatenate([q, jnp.concatenate([b] * NSA_GROUP, axis=0)], axis=1)

    _softmax_init(m_ref, acc_ref)

    def step(start, size):
        s = _dot_nt(qa_ref[start // SUPER_KEYS], ka_ref[0, 0, pl.ds(start, size), :])
        _softmax_step(s, va_ref[0, 0, pl.ds(start, size), :], m_ref, acc_ref)

    _past_keys_loop(s0, NSA_TK, step)
    d0 = pl.multiple_of(s0, tq)
    s = _dot_nt(q, ka_ref[0, 0, pl.ds(d0, tq), :][:, 0:HEAD_DIM])
    rqd = lax.broadcasted_iota(jnp.int32, s.shape, 0) & (tq - 1)
    col = lax.broadcasted_iota(jnp.int32, s.shape, 1)
    _softmax_step(jnp.where(col <= rqd, s, NEG), va_ref[0, 0, pl.ds(d0, tq), :], m_ref, acc_ref)
    o_s = _softmax_result(acc_ref)

    w = NSA_GROUP * HEAD_DIM
    gexp = _dot_x2(gt_ref[...], e_ref[0])
    wide = lambda x: jnp.concatenate([x[g * tq:(g + 1) * tq] for g in range(NSA_GROUP)], axis=1)
    o_ref[...] = (gexp[:, 0:w] * wide(o_c) + gexp[:, w:2 * w] * wide(o_s) + gexp[:, 2 * w:3 * w] * wide(o_w)).astype(BF16)


def _nsa(qs, kcmp, vcmp, ka, va, kvw, gates, batch, seq):
    tq = NSA_TQ
    nq = seq // tq
    nb = seq // SEL_BLOCK
    n16 = seq // CMP_STRIDE
    assert seq % SUPER_KEYS == 0
    n_super = seq // SUPER_KEYS
    rows = NSA_GROUP * tq
    e = np.zeros((NSA_KV_HEADS, 128, 3 * NSA_GROUP * HEAD_DIM), np.float32)
    for br in range(3):
        for hk in range(NSA_KV_HEADS):
            for g in range(NSA_GROUP):
                c = (br * NSA_GROUP + g) * HEAD_DIM
                e[hk, br * NSA_HEADS + hk * NSA_GROUP + g, c:c + HEAD_DIM] = 1.0
    nn, jj = np.arange(n16)[:, None], np.arange(nb)[None, :]
    band = ((nn >= 4 * jj - 1) & (nn <= 4 * jj + 3)).astype(np.float32)
    resident = lambda width: pl.BlockSpec((1, 1, seq, width), lambda b, h, i: (b, h, 0, 0))
    w = NSA_GROUP * HEAD_DIM
    return pl.pallas_call(
        functools.partial(_nsa_kernel, n_super),
        grid=(batch, NSA_KV_HEADS, nq),
        in_specs=[pl.BlockSpec((1, 1, 1, rows, HEAD_DIM), lambda b, h, i: (b, h, i, 0, 0)),
                  pl.BlockSpec((1, 1, n16, HEAD_DIM), lambda b, h, i: (b, h, 0, 0)),
                  pl.BlockSpec((1, 1, n16, HEAD_DIM), lambda b, h, i: (b, h, 0, 0)),
                  resident(LANES), resident(LANES), resident(LANES),
                  pl.BlockSpec((tq, 128), lambda b, h, i: (b * nq + i, 0)),
                  pl.BlockSpec((1, 128, 3 * w), lambda b, h, i: (h, 0, 0)),
                  pl.BlockSpec((n16, nb), lambda b, h, i: (0, 0))],
        out_specs=pl.BlockSpec((tq, w), lambda b, h, i: (b * nq + i, h)),
        out_shape=jax.ShapeDtypeStruct((batch * seq, NSA_W), BF16),
        scratch_shapes=[pltpu.VMEM((n_super, rows, LANES), BF16), pltpu.VMEM((rows, LANES), F32),
                        pltpu.VMEM((rows, LANES), F32)],
        compiler_params=_params("arbitrary", "arbitrary", "arbitrary"),
        name="nsa",
    )(qs, kcmp, vcmp, ka, va, kvw, gates, jnp.asarray(e, BF16), jnp.asarray(band, BF16))


def _odd_out_kernel(om_ref, on_ref, h_ref, w_ref, o_ref):
    acc = h_ref[...] + _dot(on_ref[...], w_ref[MOBA_W:D_MODEL, :])
    for h in range(MOBA_HEADS):
        acc = acc + _dot(om_ref[0, h], w_ref[h * HEAD_DIM:(h + 1) * HEAD_DIM, :])
    o_ref[...] = acc


def _odd_out(o_moba, o_nsa, h2, w_out, seq, tm=512):
    t = h2.shape[0]
    tps = seq // tm
    row = lambda w: pl.BlockSpec((tm, w), lambda i: (i, 0))
    return pl.pallas_call(
        _odd_out_kernel,
        grid=(t // tm,),
        in_specs=[pl.BlockSpec((1, MOBA_HEADS, tm, HEAD_DIM), lambda i: (i // tps, 0, i % tps, 0)),
                  row(NSA_W), row(D_MODEL), _full((D_MODEL, D_MODEL))],
        out_specs=row(D_MODEL),
        out_shape=jax.ShapeDtypeStruct((t, D_MODEL), F32),
        compiler_params=_params("arbitrary"),
        name="odd_out",
    )(o_moba, o_nsa, h2, w_out)


def _even_layer(h2, batch, seq, norm, w_in, w_out, lam_re, lam_im, log_dt, b_re, b_im, c_re, c_im, d, w_glu, conv_w, conv_b):
    u, ub, yb = _even_in(h2, norm.reshape(1, D_MODEL), w_in.astype(BF16), conv_w, conv_b.reshape(1, CONV_WIDTH), seq)
    ypre = _s5_mixer_pre(ub, batch, seq, lam_re, lam_im, log_dt, b_re, b_im, c_re, c_im)
    return _even_out(ypre, u, yb, h2, d.reshape(1, S5_WIDTH), w_glu.astype(BF16), w_out.astype(BF16))


def _odd_layer(h2, batch, seq, norm, w_in, w_out, moba_q_norm, moba_k_norm, nsa_q_norm, nsa_kcmp_norm, nsa_ksel_norm,
               nsa_kwin_norm, cmp_pe_k, cmp_w1_k, cmp_w2_k, cmp_pe_v, cmp_w1_v, cmp_w2_v):
    qm, kam, kmean, vam, qs, kc, vc, kas, vas, kvw, gates = _odd_in(
        h2, batch, seq, norm.reshape(1, D_MODEL), w_in, moba_q_norm, moba_k_norm, nsa_q_norm, nsa_ksel_norm, nsa_kwin_norm)
    cmp = _compress(kc, vc, batch, seq, cmp_pe_k, cmp_w1_k, cmp_w2_k, cmp_pe_v, cmp_w1_v, cmp_w2_v, nsa_kcmp_norm)
    o_moba = _moba(qm, kam, kmean, vam, batch, seq)
    o_nsa = _nsa(qs, cmp[0], cmp[1], kas, vas, kvw, gates, batch, seq)
    return _odd_out(o_moba, o_nsa, h2, w_out.astype(BF16), seq)


def kernel(x, ev_norm_mix, ev_w_in, ev_w_out, s5_lam_re, s5_lam_im, s5_log_dt, s5_b_re, s5_b_im, s5_c_re, s5_c_im, s5_d, s5_w_glu, conv_w, conv_b, od_norm_mix, od_w_in, od_w_out, moba_q_norm, moba_k_norm, nsa_q_norm, nsa_kcmp_norm, nsa_ksel_norm, nsa_kwin_norm, cmp_pe_k, cmp_w1_k, cmp_w2_k, cmp_pe_v, cmp_w1_v, cmp_w2_v, moe_norm, moe_w_group, moe_b_group, moe_w_expert, moe_b_expert, moe_w_gate, moe_w_up, moe_w_down):
    batch, seq, _ = x.shape
    depth = moe_norm.shape[0]
    h = x.reshape(batch * seq, D_MODEL)
    for layer in range(depth):
        i = layer // 2
        if layer % 2 == 0:
            h = _even_layer(h, batch, seq, ev_norm_mix[i], ev_w_in[i], ev_w_out[i], s5_lam_re[i], s5_lam_im[i], s5_log_dt[i],
                            s5_b_re[i], s5_b_im[i], s5_c_re[i], s5_c_im[i], s5_d[i], s5_w_glu[i], conv_w[i], conv_b[i])
        else:
            h = _odd_layer(h, batch, seq, od_norm_mix[i], od_w_in[i], od_w_out[i], moba_q_norm[i], moba_k_norm[i],
                           nsa_q_norm[i], nsa_kcmp_norm[i], nsa_ksel_norm[i], nsa_kwin_norm[i], cmp_pe_k[i], cmp_w1_k[i],
                           cmp_w2_k[i], cmp_pe_v[i], cmp_w1_v[i], cmp_w2_v[i])
        h = _moe(h, moe_norm[layer].reshape(1, D_MODEL), moe_w_group[layer], moe_b_group[layer], moe_w_expert[layer],
                 moe_b_expert[layer], moe_w_gate[layer], moe_w_up[layer], moe_w_down[layer])
    return h.reshape(batch, seq, D_MODEL)
```

```python
import functools
import math

import jax
import jax.numpy as jnp
import numpy as np
from jax import lax
from jax.experimental import pallas as pl
from jax.experimental.pallas import tpu as pltpu

D_MODEL = 1024
HEAD_DIM = 64
EPS = 1e-6
S5_WIDTH = 256
S5_GROUP = 16
S5_GROUPS = 16
S5_STATE = 64
S5_CHUNK = 16
CONV_WIDTH = 768
CONV_K = 3
MOBA_HEADS = 4
NSA_HEADS = 12
NSA_KV_HEADS = 2
NSA_GROUP = 6
MOBA_W = 256
NSA_W = 768
KV_W = 128
MOBA_BLOCK = 256
MOBA_TOPK = 3
CMP_BLOCK = 32
CMP_STRIDE = 16
CMP_HIDDEN = 256
SEL_BLOCK = 64
SEL_TOPK = 8
WINDOW = 512
N_GROUPS = 4
EXPERTS_PER_GROUP = 4
N_EXPERTS = 16
EXPERT_FF = 256

VMEM_LIMIT_BYTES = 56 * 1024 * 1024
NEG = -float(2 ** 30)
F32 = jnp.float32
BF16 = jnp.bfloat16


def _params(*semantics):
    return pltpu.CompilerParams(dimension_semantics=semantics, vmem_limit_bytes=VMEM_LIMIT_BYTES)


def _dot(a, b):
    return jnp.dot(a, b, preferred_element_type=F32)


def _dot_nt(a, b):
    return lax.dot_general(a, b, (((1,), (1,)), ((), ())), preferred_element_type=F32)


def _split(x):
    hi = x.astype(BF16)
    lo = (x - hi.astype(F32)).astype(BF16)
    return hi, lo


def _dot_x2(x, w):
    hi, lo = _split(x)
    return _dot(hi, w) + _dot(lo, w)


def _dot_x3(x, w_hi, w_lo):
    hi, lo = _split(x)
    return _dot(hi, w_hi) + (_dot(hi, w_lo) + _dot(lo, w_hi))


def _rms(x, gain):
    return x * lax.rsqrt(jnp.mean(x * x, axis=-1, keepdims=True) + EPS) * gain


def _gelu(x):
    return 0.5 * x * (1.0 + jnp.tanh(math.sqrt(2.0 / math.pi) * (x + 0.044715 * (x * x * x))))


def _sigmoid(x):
    return 1.0 / (1.0 + jnp.exp(-x))


def _full(shape):
    n = len(shape)
    return pl.BlockSpec(shape, lambda *_: (0,) * n)


def _even_in_kernel(tiles_per_seq, x_ref, g_ref, w_ref, cw_ref, cb_ref, u_ref, ub_ref, yb_ref, carry_ref):
    i = pl.program_id(0)
    xn = _rms(x_ref[...], g_ref[...]).astype(BF16)
    u = _dot(xn, w_ref[:, 0:S5_WIDTH])
    u_ref[...] = u
    ub_ref[...] = u.astype(BF16)
    o = S5_WIDTH
    xc = _dot(xn, w_ref[:, o:o + CONV_WIDTH])
    gb = _dot(xn, w_ref[:, o + CONV_WIDTH:o + 2 * CONV_WIDTH])
    gc = _dot(xn, w_ref[:, o + 2 * CONV_WIDTH:o + 3 * CONV_WIDTH])
    z = gc * xc
    tm = z.shape[0]

    @pl.when(i % tiles_per_seq == 0)
    def _():
        carry_ref[...] = jnp.zeros_like(carry_ref)

    row = lax.broadcasted_iota(jnp.int32, z.shape, 0)
    prev1 = carry_ref[7:8, :]
    prev2 = carry_ref[6:7, :]
    z1 = jnp.where(row == 0, prev1, pltpu.roll(z, 1, 0))
    z2 = jnp.where(row == 0, prev2, jnp.where(row == 1, prev1, pltpu.roll(z, 2, 0)))
    y = cw_ref[0:1, :] * z2 + cw_ref[1:2, :] * z1 + cw_ref[2:3, :] * z + cb_ref[...]
    yb_ref[...] = (gb * y).astype(BF16)
    carry_ref[...] = z[tm - 8:tm, :]


def _even_in(x2, gain, w_in, conv_w, conv_b, seq, tm=512):
    t = x2.shape[0]
    n_in = w_in.shape[1]
    return pl.pallas_call(
        functools.partial(_even_in_kernel, seq // tm),
        grid=(t // tm,),
        in_specs=[pl.BlockSpec((tm, D_MODEL), lambda i: (i, 0)), _full((1, D_MODEL)),
                  _full((D_MODEL, n_in)), _full((CONV_K, CONV_WIDTH)), _full((1, CONV_WIDTH))],
        out_specs=[pl.BlockSpec((tm, S5_WIDTH), lambda i: (i, 0)), pl.BlockSpec((tm, S5_WIDTH), lambda i: (i, 0)),
                   pl.BlockSpec((tm, CONV_WIDTH), lambda i: (i, 0))],
        out_shape=[jax.ShapeDtypeStruct((t, S5_WIDTH), F32), jax.ShapeDtypeStruct((t, S5_WIDTH), BF16),
                   jax.ShapeDtypeStruct((t, CONV_WIDTH), BF16)],
        scratch_shapes=[pltpu.VMEM((8, CONV_WIDTH), F32)],
        compiler_params=_params("arbitrary"),
        name="even_in",
    )(x2, gain, w_in, conv_w, conv_b)


def _s5_weights(lam_re, lam_im, log_dt, b_re, b_im, c_re, c_im):
    g, p, hg, ck = S5_GROUPS, S5_STATE, S5_GROUP, S5_CHUNK
    lr, li = lam_re.astype(F32), lam_im.astype(F32)
    dt = jnp.exp(log_dt.astype(F32))[:, None]
    mag = jnp.exp(lr * dt)
    a_re, a_im = mag * jnp.cos(li * dt), mag * jnp.sin(li * dt)
    den = lr * lr + li * li
    f_re = ((a_re - 1.0) * lr + a_im * li) / den
    f_im = (a_im * lr - (a_re - 1.0) * li) / den
    br, bi = b_re.astype(F32), b_im.astype(F32)
    bb_re = f_re[..., None] * br - f_im[..., None] * bi
    bb_im = f_re[..., None] * bi + f_im[..., None] * br
    pw_re, pw_im = [jnp.ones_like(a_re)], [jnp.zeros_like(a_im)]
    for _ in range(ck):
        r, m = pw_re[-1], pw_im[-1]
        pw_re.append(r * a_re - m * a_im)
        pw_im.append(r * a_im + m * a_re)
    pw_re, pw_im = jnp.stack(pw_re), jnp.stack(pw_im)
    cr, ci = c_re.astype(F32), c_im.astype(F32)
    rev_re, rev_im = pw_re[ck - 1::-1][:ck], pw_im[ck - 1::-1][:ck]
    ws_re = rev_re[:, :, :, None] * bb_re[None] - rev_im[:, :, :, None] * bb_im[None]
    ws_im = rev_re[:, :, :, None] * bb_im[None] + rev_im[:, :, :, None] * bb_re[None]
    ca_re = cr[None] * pw_re[1:, :, None, :] - ci[None] * pw_im[1:, :, None, :]
    ca_im = cr[None] * pw_im[1:, :, None, :] + ci[None] * pw_re[1:, :, None, :]
    cb_re = jnp.einsum('ghp,kgp,gpj->kghj', cr, pw_re[:ck], bb_re) - jnp.einsum('ghp,kgp,gpj->kghj', cr, pw_im[:ck], bb_im) \
        - jnp.einsum('ghp,kgp,gpj->kghj', ci, pw_re[:ck], bb_im) - jnp.einsum('ghp,kgp,gpj->kghj', ci, pw_im[:ck], bb_re)
    lag = np.arange(ck)[None, :] - np.arange(ck)[:, None]
    tz = cb_re[np.clip(lag, 0, ck - 1)]
    tz = jnp.where((lag >= 0)[:, :, None, None, None], tz, 0.0)
    cw = ck * g * hg
    ws = jnp.stack([ws_re, ws_im]).transpose(1, 2, 4, 0, 3).reshape(cw, 2 * p)
    wc = jnp.stack([ca_re, -ca_im]).transpose(0, 2, 4, 1, 3).reshape(2 * g * p, ck * hg)
    tzc = tz.transpose(0, 2, 4, 1, 3).reshape(cw, ck * hg)
    return ws.astype(BF16), wc.astype(BF16), tzc.astype(BF16), pw_re[ck].reshape(1, g * p), pw_im[ck].reshape(1, g * p)


def _group_expand(compact, expand, row_shift, col_shift, col0):
    full = _dot(compact, expand)
    row = lax.broadcasted_iota(jnp.int32, full.shape, 0)
    col = col0 + lax.broadcasted_iota(jnp.int32, full.shape, 1)
    same = ((row >> row_shift) & (S5_GROUPS - 1)) == ((col >> col_shift) & (S5_GROUPS - 1))
    return jnp.where(same, full, 0.0).astype(BF16)


def _s5_state_kernel(u_ref, ws_ref, e_ref, s_ref, w_scr):
    @pl.when(pl.program_id(1) == 0)
    def _():
        w_scr[...] = _group_expand(ws_ref[...], e_ref[...], 4, 6, pl.program_id(0) * w_scr.shape[1])

    s_ref[...] = _dot(u_ref[...], w_scr[...])


def _s5_scan_kernel(s_ref, are_ref, aim_ref, xprev_ref, st_ref):
    @pl.when(pl.program_id(0) == 0)
    def _():
        st_ref[...] = jnp.zeros_like(st_ref)

    a_re, a_im = are_ref[...], aim_ref[...]
    nb, n = s_ref.shape[0], s_ref.shape[1]
    half = a_re.shape[1]

    def body(c, carry):
        out = []
        for b in range(nb):
            xr, xi = carry[2 * b], carry[2 * b + 1]
            xprev_ref[b, pl.ds(c, 1), 0:half] = xr
            xprev_ref[b, pl.ds(c, 1), half:2 * half] = xi
            s = s_ref[b, pl.ds(c, 1), :]
            out += [a_re * xr - a_im * xi + s[:, 0:half], a_re * xi + a_im * xr + s[:, half:2 * half]]
        return tuple(out)

    init = tuple(st_ref[b:b + 1, o:o + half] for b in range(nb) for o in (0, half))
    final = lax.fori_loop(0, n, body, init, unroll=4)
    for b in range(nb):
        st_ref[b:b + 1, 0:half] = final[2 * b]
        st_ref[b:b + 1, half:2 * half] = final[2 * b + 1]


def _s5_out_kernel(u_ref, xp_ref, tz_ref, wc_ref, e_ref, y_ref, tz_scr, wc_scr):
    @pl.when(pl.program_id(1) == 0)
    def _():
        col0 = pl.program_id(0) * tz_scr.shape[1]
        tz_scr[...] = _group_expand(tz_ref[...], e_ref[...], 4, 4, col0)
        wc_scr[...] = _group_expand(wc_ref[...], e_ref[...], 6, 4, col0)

    y_ref[...] = _dot(u_ref[...], tz_scr[...]) + _dot(xp_ref[...].astype(BF16), wc_scr[...])


def _expand_matrix(outer, inner):
    e = np.zeros((outer, inner, outer, S5_GROUPS, inner), np.float32)
    for x in range(outer):
        for y in range(inner):
            e[x, y, x, :, y] = 1.0
    return jnp.asarray(e.reshape(outer * inner, outer * S5_GROUPS * inner), BF16)


def _s5_mixer_pre(ub, batch, seq, lam_re, lam_im, log_dt, b_re, b_im, c_re, c_im):
    ws, wc, tz, a16_re, a16_im = _s5_weights(lam_re, lam_im, log_dt, b_re, b_im, c_re, c_im)
    nc = seq // S5_CHUNK
    rows = batch * nc
    cw = S5_CHUNK * S5_WIDTH
    sw = 2 * S5_GROUPS * S5_STATE
    tr = min(rows, 512)
    tn = 512
    uc = ub.reshape(rows, cw)
    e_state = _expand_matrix(2, S5_STATE)
    e_out = _expand_matrix(S5_CHUNK, S5_GROUP)
    s = pl.pallas_call(
        _s5_state_kernel,
        grid=(sw // tn, rows // tr),
        in_specs=[pl.BlockSpec((tr, cw), lambda j, i: (i, 0)), _full(ws.shape),
                  pl.BlockSpec((e_state.shape[0], tn), lambda j, i: (0, j))],
        out_specs=pl.BlockSpec((tr, tn), lambda j, i: (i, j)),
        out_shape=jax.ShapeDtypeStruct((rows, sw), F32),
        scratch_shapes=[pltpu.VMEM((cw, tn), BF16)],
        compiler_params=_params("arbitrary", "arbitrary"),
        name="s5_state",
    )(uc, ws, e_state)
    tc = min(nc, 256)
    xprev = pl.pallas_call(
        _s5_scan_kernel,
        grid=(nc // tc,),
        in_specs=[pl.BlockSpec((batch, tc, sw), lambda i: (0, i, 0)), _full((1, sw // 2)), _full((1, sw // 2))],
        out_specs=pl.BlockSpec((batch, tc, sw), lambda i: (0, i, 0)),
        out_shape=jax.ShapeDtypeStruct((batch, nc, sw), F32),
        scratch_shapes=[pltpu.VMEM((batch, sw), F32)],
        compiler_params=_params("arbitrary"),
        name="s5_scan",
    )(s.reshape(batch, nc, sw), a16_re, a16_im)
    y = pl.pallas_call(
        _s5_out_kernel,
        grid=(cw // tn, rows // tr),
        in_specs=[pl.BlockSpec((tr, cw), lambda j, i: (i, 0)), pl.BlockSpec((tr, sw), lambda j, i: (i, 0)),
                  _full(tz.shape), _full(wc.shape), pl.BlockSpec((e_out.shape[0], tn), lambda j, i: (0, j))],
        out_specs=pl.BlockSpec((tr, tn), lambda j, i: (i, j)),
        out_shape=jax.ShapeDtypeStruct((rows, cw), F32),
        scratch_shapes=[pltpu.VMEM((cw, tn), BF16), pltpu.VMEM((sw, tn), BF16)],
        compiler_params=_params("arbitrary", "arbitrary"),
        name="s5_out",
    )(uc, xprev.reshape(rows, sw), tz, wc, e_out)
    return y.reshape(batch * seq, S5_WIDTH)


def _even_out_kernel(ypre_ref, u_ref, yb_ref, x_ref, d_ref, wglu_ref, wout_ref, o_ref):
    y = _gelu(ypre_ref[...] + d_ref[...] * u_ref[...])
    y = y * _sigmoid(_dot(y.astype(BF16), wglu_ref[...]))
    o_ref[...] = (x_ref[...] + _dot(y.astype(BF16), wout_ref[0:S5_WIDTH, :])
                  + _dot(yb_ref[...], wout_ref[S5_WIDTH:D_MODEL, :]))


def _even_out(ypre, u, yb, x2, d, w_glu, w_out, tm=512):
    t = x2.shape[0]
    row = lambda w: pl.BlockSpec((tm, w), lambda i: (i, 0))
    return pl.pallas_call(
        _even_out_kernel,
        grid=(t // tm,),
        in_specs=[row(S5_WIDTH), row(S5_WIDTH), row(CONV_WIDTH), row(D_MODEL), _full((1, S5_WIDTH)),
                  _full((S5_WIDTH, S5_WIDTH)), _full((D_MODEL, D_MODEL))],
        out_specs=row(D_MODEL),
        out_shape=jax.ShapeDtypeStruct((t, D_MODEL), F32),
        compiler_params=_params("arbitrary"),
        name="even_out",
    )(ypre, u, yb, x2, d, w_glu, w_out)


def _first_max(v, pos, width, axis=-1):
    m = jnp.max(v, axis=axis, keepdims=True)
    idx = jnp.min(jnp.where(v == m, pos, width), axis=axis, keepdims=True)
    return m, idx


MOE_TM = 1024
MOE_ALIGN = 16
MOE_SLOTS = 1152
MOE_WIN = 320
ROUTER_LANES = 128


def _moe_router_kernel(h_ref, g_ref, wr_hi_ref, wr_lo_ref, br_ref, gate_ref, grp_ref, cnt_ref):
    xn = _rms(h_ref[...], g_ref[...])
    logits = _dot_x3(xn, wr_hi_ref[...], wr_lo_ref[...]) + br_ref[...]
    lane = lax.broadcasted_iota(jnp.int32, logits.shape, 1)
    width = logits.shape[1]
    is_g = lane < N_GROUPS
    gl = jnp.where(is_g, logits, -jnp.inf)
    gm, gi = _first_max(gl, lane, width)
    gw = 1.0 / jnp.sum(jnp.where(is_g, jnp.exp(gl - gm), 0.0), axis=-1, keepdims=True)
    lo = N_GROUPS + gi * EXPERTS_PER_GROUP
    in_grp = (lane >= lo) & (lane < lo + EXPERTS_PER_GROUP)
    el = jnp.where(in_grp, logits, -jnp.inf)
    m1, i1 = _first_max(el, lane, width)
    m2, i2 = _first_max(jnp.where(lane == i1, -jnp.inf, el), lane, width)
    p2 = jnp.exp(m2 - m1)
    w1 = gw / (1.0 + p2)
    w2 = gw * p2 / (1.0 + p2)
    gate_ref[...] = jnp.where(lane == i1, w1, 0.0) + jnp.where(lane == i2, w2, 0.0)
    grp = (lane == gi).astype(F32)
    grp_ref[...] = grp.astype(BF16)
    cnt_ref[0] = jnp.broadcast_to(jnp.sum(grp, axis=0, keepdims=True), cnt_ref.shape[1:])


def _moe_expert_kernel(base_ref, nwin_ref, h_ref, g_ref, gate_ref, grp_ref, wg_ref, wu_ref, wd_ref, o_ref,
                       xs_ref, gs_ref, ys_ref, pt_ref):
    i, g = pl.program_id(0), pl.program_id(1)
    tm = h_ref.shape[0]

    @pl.when((i == 0) & (g == 0))
    def _():
        xs_ref[...] = jnp.zeros_like(xs_ref)
        gs_ref[...] = jnp.zeros_like(gs_ref)
        ys_ref[...] = jnp.zeros_like(ys_ref)

    @pl.when(g == 0)
    def _():
        xn = _rms(h_ref[...], g_ref[...]).astype(BF16)
        grp = grp_ref[...]
        earlier = (lax.broadcasted_iota(jnp.int32, (tm, tm), 0) > lax.broadcasted_iota(jnp.int32, (tm, tm), 1)).astype(BF16)
        rank = _dot(earlier, grp)
        lane = lax.broadcasted_iota(jnp.int32, rank.shape, 1)
        for k in range(N_GROUPS):
            rank = rank + jnp.where(lane == k, base_ref[i * N_GROUPS + k].astype(F32), 0.0)
        slot = jnp.sum(grp.astype(F32) * rank, axis=-1, keepdims=True).astype(jnp.int32)
        pt = (lax.broadcasted_iota(jnp.int32, (tm, MOE_SLOTS), 1) == slot).astype(BF16)
        pt_ref[...] = pt
        gather = lambda x: lax.dot_general(pt, x, (((0,), (0,)), ((), ())), preferred_element_type=F32)
        xs_ref[0:MOE_SLOTS, :] = gather(xn).astype(BF16)
        gate_hi, gate_lo = _split(gate_ref[...])
        gs_ref[0:MOE_SLOTS, :] = gather(gate_hi) + gather(gate_lo)

    def window(w, carry):
        r0 = pl.multiple_of(base_ref[i * N_GROUPS + g] + w * MOE_WIN, MOE_ALIGN)
        x = xs_ref[pl.ds(r0, MOE_WIN), :]
        gate = gs_ref[pl.ds(r0, MOE_WIN), :]
        lane = lax.broadcasted_iota(jnp.int32, gate.shape, 1)
        y = None
        for j in range(EXPERTS_PER_GROUP):
            ge = jnp.sum(jnp.where(lane == g * EXPERTS_PER_GROUP + (j + N_GROUPS), gate, 0.0), axis=-1, keepdims=True)
            h1 = _dot(x, wg_ref[j])
            h3 = _dot(x, wu_ref[j])
            act = (h1 * _sigmoid(h1)) * h3 * ge
            yj = _dot(act.astype(BF16), wd_ref[j])
            y = yj if y is None else y + yj
        ys_ref[pl.ds(r0, MOE_WIN), :] = y.astype(BF16)
        return carry

    lax.fori_loop(0, nwin_ref[i * N_GROUPS + g], window, 0)

    @pl.when(g == N_GROUPS - 1)
    def _():
        o_ref[...] = h_ref[...] + _dot(pt_ref[...], ys_ref[0:MOE_SLOTS, :])


def _moe(h2, gain, w_group, b_group, w_expert, b_expert, w_gate, w_up, w_down):
    t = h2.shape[0]
    tm, rw = MOE_TM, ROUTER_LANES
    assert t % tm == 0 and MOE_SLOTS >= tm + N_GROUPS * (MOE_ALIGN - 1) and MOE_WIN % MOE_ALIGN == 0
    n_tiles = t // tm
    wr = jnp.zeros((D_MODEL, rw), F32).at[:, 0:N_GROUPS].set(w_group).at[:, N_GROUPS:N_GROUPS + N_EXPERTS].set(w_expert)
    br = jnp.zeros((1, rw), F32).at[0, 0:N_GROUPS].set(b_group).at[0, N_GROUPS:N_GROUPS + N_EXPERTS].set(b_expert)
    wr_hi = wr.astype(BF16)
    wr_lo = (wr - wr_hi.astype(F32)).astype(BF16)
    row = lambda width: pl.BlockSpec((tm, width), lambda i: (i, 0))
    gates, grp, cnt = pl.pallas_call(
        _moe_router_kernel,
        grid=(n_tiles,),
        in_specs=[row(D_MODEL), _full((1, D_MODEL)), _full((D_MODEL, rw)), _full((D_MODEL, rw)), _full((1, rw))],
        out_specs=[row(rw), row(rw), pl.BlockSpec((1, 8, rw), lambda i: (i, 0, 0))],
        out_shape=[jax.ShapeDtypeStruct((t, rw), F32), jax.ShapeDtypeStruct((t, rw), BF16),
                   jax.ShapeDtypeStruct((n_tiles, 8, rw), F32)],
        compiler_params=_params("arbitrary"),
        name="moe_router",
    )(h2, gain, wr_hi, wr_lo, br)
    n = cnt[:, 0, 0:N_GROUPS].astype(jnp.int32)
    padded = (n + (MOE_ALIGN - 1)) // MOE_ALIGN * MOE_ALIGN
    base = (jnp.cumsum(padded, axis=1) - padded).reshape(-1)
    nwin = ((padded + (MOE_WIN - 1)) // MOE_WIN).reshape(-1)
    tile = lambda width: pl.BlockSpec((tm, width), lambda i, g, *_: (i, 0))
    experts = lambda shape: pl.BlockSpec((EXPERTS_PER_GROUP,) + shape, lambda i, g, *_: (g, 0, 0))
    slots = MOE_SLOTS + MOE_WIN
    return pl.pallas_call(
        _moe_expert_kernel,
        grid_spec=pltpu.PrefetchScalarGridSpec(
            num_scalar_prefetch=2,
            grid=(n_tiles, N_GROUPS),
            in_specs=[tile(D_MODEL), pl.BlockSpec((1, D_MODEL), lambda i, g, *_: (0, 0)), tile(rw), tile(rw),
                      experts((D_MODEL, EXPERT_FF)), experts((D_MODEL, EXPERT_FF)), experts((EXPERT_FF, D_MODEL))],
            out_specs=tile(D_MODEL),
            scratch_shapes=[pltpu.VMEM((slots, D_MODEL), BF16), pltpu.VMEM((slots, rw), F32),
                            pltpu.VMEM((slots, D_MODEL), BF16), pltpu.VMEM((tm, MOE_SLOTS), BF16)]),
        out_shape=jax.ShapeDtypeStruct((t, D_MODEL), F32),
        compiler_params=_params("arbitrary", "arbitrary"),
        name="moe",
    )(base, nwin, h2, gain, gates, grp, w_gate.astype(BF16), w_up.astype(BF16), w_down.astype(BF16))


ODD_SPLITS = (MOBA_W, MOBA_W, MOBA_W, NSA_W, KV_W, KV_W, KV_W, KV_W, KV_W, KV_W, 128)
ODD_IN_PAD = sum(ODD_SPLITS)


def _head_rms(x, hsum, gain):
    w = x.shape[1]
    ss = jnp.concatenate([_dot_x2(x[:, o:o + hsum.shape[0]] * x[:, o:o + hsum.shape[0]], hsum)
                          for o in range(0, w, hsum.shape[0])], axis=1) if w > hsum.shape[0] else _dot_x2(x * x, hsum)
    return x * lax.rsqrt(ss * (1.0 / HEAD_DIM) + EPS) * gain


def _odd_in_kernel(tiles_per_seq, x_ref, g_ref, w_ref, hsum_ref, gq_ref, gk_ref, gnq_ref, gks_ref, gkw_ref,
                   qm_ref, kam_ref, kmean_ref, vam_ref, qs_ref, kc_ref, vc_ref, kas_ref, vas_ref, kvw_ref, gt_ref):
    xn = _rms(x_ref[...], g_ref[...]).astype(BF16)
    offs = np.cumsum((0,) + ODD_SPLITS)
    col = lambda j: _dot(xn, w_ref[:, int(offs[j]):int(offs[j + 1])])
    head = lambda x, h: x[:, h * HEAD_DIM:(h + 1) * HEAD_DIM]
    hsum = hsum_ref[...]
    hsum128 = hsum_ref[0:128, 0:128]
    tm = x_ref.shape[0]
    pos = (pl.program_id(0) % tiles_per_seq) * tm + lax.broadcasted_iota(jnp.int32, (tm, HEAD_DIM), 0)
    lane = lax.broadcasted_iota(jnp.int32, (tm, HEAD_DIM), 1)
    ones_col = (lane == 0).astype(BF16)

    qm = _head_rms(col(0), hsum, gq_ref[...])
    km = _head_rms(col(1), hsum, gk_ref[...])
    for j in range(tm // MOBA_BLOCK):
        kmean_ref[0, j:j + 1, :] = jnp.mean(km[j * MOBA_BLOCK:(j + 1) * MOBA_BLOCK, :], axis=0, keepdims=True)
    km = km.astype(BF16)
    vm = col(2).astype(BF16)
    moba_id = (lane == pos // MOBA_BLOCK).astype(BF16)
    for h in range(MOBA_HEADS):
        qm_ref[0, h] = head(qm, h)
        kam_ref[0, h] = jnp.concatenate([head(km, h), moba_id], axis=1)
        vam_ref[0, h] = jnp.concatenate([head(vm, h), ones_col], axis=1)

    qd = (_head_rms(col(3), hsum, gnq_ref[...]) * (HEAD_DIM ** -0.5)).astype(BF16)
    for hk in range(NSA_KV_HEADS):
        for j in range(tm // NSA_TQ):
            for g in range(NSA_GROUP):
                qs_ref[0, hk, j, g * NSA_TQ:(g + 1) * NSA_TQ, :] = head(qd, hk * NSA_GROUP + g)[j * NSA_TQ:(j + 1) * NSA_TQ, :]
    kc_ref[...] = col(4).astype(BF16)
    vc_ref[...] = col(5).astype(BF16)
    ks = _head_rms(col(6), hsum128, gks_ref[...]).astype(BF16)
    vs = col(7).astype(BF16)
    kw = _head_rms(col(8), hsum128, gkw_ref[...]).astype(BF16)
    vw = col(9).astype(BF16)
    sel_id = (lane == (pos // SEL_BLOCK) % SEL_LANES).astype(BF16)
    for hk in range(NSA_KV_HEADS):
        kas_ref[0, hk] = jnp.concatenate([head(ks, hk), sel_id], axis=1)
        vas_ref[0, hk] = jnp.concatenate([head(vs, hk), ones_col], axis=1)
        kvw_ref[0, hk] = jnp.concatenate([head(kw, hk), head(vw, hk)], axis=1)
    gt_ref[...] = _sigmoid(col(10))


def _odd_in(h2, batch, seq, gain, w_in, moba_q_norm, moba_k_norm, nsa_q_norm, nsa_ksel_norm, nsa_kwin_norm, tm=512):
    t = h2.shape[0]
    assert MOBA_LANES == HEAD_DIM and SEL_LANES == HEAD_DIM and seq % tm == 0 and tm % MOBA_BLOCK == 0
    tps = seq // tm
    w = jnp.pad(w_in, ((0, 0), (0, ODD_IN_PAD - w_in.shape[1]))).astype(BF16)
    hsum = jnp.asarray(np.kron(np.eye(MOBA_W // HEAD_DIM), np.ones((HEAD_DIM, HEAD_DIM))), BF16)
    tile = lambda g, width: jnp.tile(g.astype(F32), width // HEAD_DIM).reshape(1, width)
    row = lambda width: pl.BlockSpec((tm, width), lambda i: (i, 0))
    heads = lambda n, width: pl.BlockSpec((1, n, tm, width), lambda i: (i // tps, 0, i % tps, 0))
    nmb = tm // MOBA_BLOCK
    nqt = tm // NSA_TQ
    rows = NSA_GROUP * NSA_TQ
    sds = jax.ShapeDtypeStruct
    out_specs = [heads(MOBA_HEADS, HEAD_DIM), heads(MOBA_HEADS, LANES), pl.BlockSpec((1, nmb, MOBA_W), lambda i: (i, 0, 0)),
                 heads(MOBA_HEADS, LANES),
                 pl.BlockSpec((1, NSA_KV_HEADS, nqt, rows, HEAD_DIM), lambda i: (i // tps, 0, i % tps, 0, 0)),
                 row(KV_W), row(KV_W), heads(NSA_KV_HEADS, LANES), heads(NSA_KV_HEADS, LANES), heads(NSA_KV_HEADS, LANES),
                 row(128)]
    out_shape = [sds((batch, MOBA_HEADS, seq, HEAD_DIM), F32), sds((batch, MOBA_HEADS, seq, LANES), BF16),
                 sds((t // tm, nmb, MOBA_W), F32), sds((batch, MOBA_HEADS, seq, LANES), BF16),
                 sds((batch, NSA_KV_HEADS, seq // NSA_TQ, rows, HEAD_DIM), BF16),
                 sds((t, KV_W), BF16), sds((t, KV_W), BF16), sds((batch, NSA_KV_HEADS, seq, LANES), BF16),
                 sds((batch, NSA_KV_HEADS, seq, LANES), BF16), sds((batch, NSA_KV_HEADS, seq, LANES), BF16),
                 sds((t, 128), F32)]
    return pl.pallas_call(
        functools.partial(_odd_in_kernel, tps),
        grid=(t // tm,),
        in_specs=[row(D_MODEL), _full((1, D_MODEL)), _full((D_MODEL, ODD_IN_PAD)), _full((MOBA_W, MOBA_W)),
                  _full((1, MOBA_W)), _full((1, MOBA_W)), _full((1, NSA_W)), _full((1, KV_W)), _full((1, KV_W))],
        out_specs=out_specs,
        out_shape=out_shape,
        compiler_params=_params("arbitrary"),
        name="odd_in",
    )(h2, gain, w, hsum, tile(moba_q_norm, MOBA_W), tile(moba_k_norm, MOBA_W), tile(nsa_q_norm, NSA_W),
      tile(nsa_ksel_norm, KV_W), tile(nsa_kwin_norm, KV_W))


def _compress_kernel(c_ref, w1_ref, w2_ref, pe_ref, g_ref, o_ref):
    kind = pl.program_id(0)
    c = c_ref[0, 0, 0]
    half = c.shape[1]
    n16 = c.shape[0]
    first = _dot(c, w1_ref[0, 0:half, :])
    second = _dot(c, w1_ref[0, half:2 * half, :])
    peb = _dot(pe_ref[0], w1_ref[0])[0:1, :]
    hid = _gelu(first + pltpu.roll(second, n16 - 1, 0) + peb)
    out = _dot(hid.astype(BF16), w2_ref[0])
    o_ref[0, 0, 0] = jnp.where(kind == 0, _rms(out, g_ref[...]), out).astype(BF16)


def _compress(kc, vc, batch, seq, pe_k, w1_k, w2_k, pe_v, w1_v, w2_v, kcmp_norm):
    n16 = seq // CMP_STRIDE
    half = CMP_STRIDE * HEAD_DIM

    def flat(x):
        return x.reshape(batch, n16, CMP_STRIDE, NSA_KV_HEADS, HEAD_DIM).transpose(0, 3, 1, 2, 4).reshape(
            batch, NSA_KV_HEADS, n16, half)

    c = jnp.stack([flat(kc), flat(vc)])
    w1 = jnp.stack([w1_k, w1_v]).astype(BF16)
    w2 = jnp.stack([w2_k, w2_v]).astype(BF16)
    pe = jnp.stack([pe_k, pe_v]).reshape(2, 1, 2 * half)
    pe = jnp.broadcast_to(pe, (2, 8, 2 * half)).astype(BF16)
    return pl.pallas_call(
        _compress_kernel,
        grid=(2, batch, NSA_KV_HEADS),
        in_specs=[pl.BlockSpec((1, 1, 1, n16, half), lambda k, b, h: (k, b, h, 0, 0)),
                  pl.BlockSpec((1, 2 * half, CMP_HIDDEN), lambda k, b, h: (k, 0, 0)),
                  pl.BlockSpec((1, CMP_HIDDEN, HEAD_DIM), lambda k, b, h: (k, 0, 0)),
                  pl.BlockSpec((1, 8, 2 * half), lambda k, b, h: (k, 0, 0)),
                  _full((1, HEAD_DIM))],
        out_specs=pl.BlockSpec((1, 1, 1, n16, HEAD_DIM), lambda k, b, h: (k, b, h, 0, 0)),
        out_shape=jax.ShapeDtypeStruct((2, batch, NSA_KV_HEADS, n16, HEAD_DIM), BF16),
        compiler_params=_params("arbitrary", "arbitrary", "arbitrary"),
        name="nsa_compress",
    )(c, w1, w2, pe, kcmp_norm.astype(F32).reshape(1, HEAD_DIM))


M_INIT = -1e30


LANES = 128


def _softmax_init(m_ref, acc_ref):
    m_ref[...] = jnp.full(m_ref.shape, M_INIT, F32)
    acc_ref[...] = jnp.zeros(acc_ref.shape, F32)


def _softmax_step(s, v_aug, m_ref, acc_ref):
    m_old = m_ref[...]
    m_new = jnp.maximum(m_old, jnp.max(s, axis=-1, keepdims=True))
    alpha = jnp.exp(m_old - m_new)
    p = jnp.exp(s - jnp.tile(m_new, (1, s.shape[1] // LANES)))
    acc_ref[...] = alpha * acc_ref[...] + _dot(p.astype(BF16), v_aug)
    m_ref[...] = m_new


def _past_keys_loop(n_keys, tile, step):
    n_full = n_keys // tile

    def body(j, carry):
        step(pl.multiple_of(j * tile, tile), tile)
        return carry

    lax.fori_loop(0, n_full, body, 0)
    rest = n_keys - n_full * tile
    start = pl.multiple_of(n_full * tile, tile)

    @pl.when(rest > tile // 2)
    def _():
        step(start, tile)

    @pl.when((rest > 0) & (rest <= tile // 2))
    def _():
        step(start, tile // 2)


def _softmax_result(acc_ref):
    acc = acc_ref[...]
    return acc[:, 0:HEAD_DIM] * (1.0 / acc[:, HEAD_DIM:HEAD_DIM + 1])


def _pick_top(score, pos, width, k):
    sel = jnp.zeros(score.shape, jnp.bool_)
    for _ in range(k):
        m, idx = _first_max(score, pos, width, axis=0)
        hit = (pos == idx) & (m > -jnp.inf)
        sel = sel | hit
        score = jnp.where(pos == idx, -jnp.inf, score)
    return sel


MOBA_LANES = 64
MOBA_TK = 2048


MOBA_TQ = 4 * MOBA_BLOCK


def _moba_kernel(q_ref, ka_ref, va_ref, kmean_ref, o_ref, qa_ref, m_ref, acc_ref):
    i0 = pl.program_id(2) * (MOBA_TQ // MOBA_BLOCK)
    q = q_ref[0, 0]
    q_hi, q_lo = _split(q)
    km_hi, km_lo = _split(kmean_ref[0, 0])
    gate = _dot_nt(km_hi, q_hi) + (_dot_nt(km_lo, q_hi) + _dot_nt(km_hi, q_lo))
    blk = lax.broadcasted_iota(jnp.int32, gate.shape, 0)
    cur = i0 + lax.broadcasted_iota(jnp.int32, gate.shape, 1) // MOBA_BLOCK
    sel = _pick_top(jnp.where(blk < cur, gate, -jnp.inf), blk, gate.shape[0], MOBA_TOPK)
    qs = (q * (HEAD_DIM ** -0.5)).astype(BF16)
    past = jnp.where(sel & (blk < i0), 0.0, NEG).T[:, 0:MOBA_LANES]
    own = jnp.where((blk == cur) | (sel & (blk >= i0)), 0.0, NEG).T[:, 0:MOBA_LANES]
    qa_ref[0] = jnp.concatenate([qs, past.astype(BF16)], axis=1)
    qa_ref[1] = jnp.concatenate([qs, own.astype(BF16)], axis=1)
    _softmax_init(m_ref, acc_ref)

    def step(start, size):
        s = _dot_nt(qa_ref[0], ka_ref[0, 0, pl.ds(start, size), :])
        _softmax_step(s, va_ref[0, 0, pl.ds(start, size), :], m_ref, acc_ref)

    _past_keys_loop(i0 * MOBA_BLOCK, MOBA_TK, step)
    start = pl.multiple_of(i0 * MOBA_BLOCK, MOBA_TQ)
    s = _dot_nt(qa_ref[1], ka_ref[0, 0, pl.ds(start, MOBA_TQ), :])
    qpos = lax.broadcasted_iota(jnp.int32, s.shape, 0)
    kpos = lax.broadcasted_iota(jnp.int32, s.shape, 1)
    hidden = (qpos // MOBA_BLOCK == kpos // MOBA_BLOCK) & (kpos > qpos)
    _softmax_step(jnp.where(hidden, NEG, s), va_ref[0, 0, pl.ds(start, MOBA_TQ), :], m_ref, acc_ref)
    o_ref[0, 0] = _softmax_result(acc_ref).astype(BF16)


def _moba(qm, ka, kmean, va, batch, seq):
    nmb = seq // MOBA_BLOCK
    assert nmb <= MOBA_LANES and seq % MOBA_TK == 0 and seq % MOBA_TQ == 0
    kmean = kmean.reshape(batch, nmb, MOBA_HEADS, HEAD_DIM).transpose(0, 2, 1, 3)
    kmean = jnp.pad(kmean, ((0, 0), (0, 0), (0, LANES - nmb), (0, 0)))
    return pl.pallas_call(
        _moba_kernel,
        grid=(batch, MOBA_HEADS, seq // MOBA_TQ),
        in_specs=[pl.BlockSpec((1, 1, MOBA_TQ, HEAD_DIM), lambda b, h, i: (b, h, i, 0)),
                  pl.BlockSpec((1, 1, seq, LANES), lambda b, h, i: (b, h, 0, 0)),
                  pl.BlockSpec((1, 1, seq, LANES), lambda b, h, i: (b, h, 0, 0)),
                  pl.BlockSpec((1, 1, LANES, HEAD_DIM), lambda b, h, i: (b, h, 0, 0))],
        out_specs=pl.BlockSpec((1, 1, MOBA_TQ, HEAD_DIM), lambda b, h, i: (b, h, i, 0)),
        out_shape=jax.ShapeDtypeStruct((batch, MOBA_HEADS, seq, HEAD_DIM), BF16),
        scratch_shapes=[pltpu.VMEM((2, MOBA_TQ, LANES), BF16), pltpu.VMEM((MOBA_TQ, LANES), F32),
                        pltpu.VMEM((MOBA_TQ, LANES), F32)],
        compiler_params=_params("arbitrary", "arbitrary", "arbitrary"),
        name="moba",
    )(qm, ka, va, kmean)


NSA_TQ = 128
NSA_TK = 2048
SEL_LANES = 64
SUPER_KEYS = SEL_LANES * SEL_BLOCK
CMP_WIDTH_STEP = 256


def _nsa_kernel(n_super, q_ref, kc_ref, vc_ref, ka_ref, va_ref, kvw_ref, gt_ref, e_ref, band_ref, o_ref,
                qa_ref, m_ref, acc_ref):
    qi = pl.program_id(2)
    tq = NSA_TQ
    s0 = qi * tq
    q = q_ref[0, 0, 0]

    def compressed(width):
        sc = _dot_nt(q, kc_ref[0, 0, 0:width, :])
        rq = lax.broadcasted_iota(jnp.int32, sc.shape, 0) & (tq - 1)
        n = lax.broadcasted_iota(jnp.int32, sc.shape, 1)
        vis = n * CMP_STRIDE + (CMP_BLOCK - 1) <= s0 + rq
        sc = jnp.where(vis, sc, NEG)
        pc = jnp.where(vis, jnp.exp(sc - jnp.max(sc, axis=-1, keepdims=True)), 0.0)
        pc = pc * (1.0 / jnp.maximum(jnp.sum(pc, axis=-1, keepdims=True), 1e-30))
        imp = pc[0:tq]
        for g in range(1, NSA_GROUP):
            imp = imp + pc[g * tq:(g + 1) * tq]
        return _dot(pc.astype(BF16), vc_ref[0, 0, 0:width, :]), _dot_x2(imp, band_ref[0:width, :])

    n16 = kc_ref.shape[2]
    widths = list(range(CMP_WIDTH_STEP, n16, CMP_WIDTH_STEP)) + [n16]
    o_c, pslc = lax.switch((s0 + tq - 1) // (CMP_WIDTH_STEP * CMP_STRIDE),
                           [functools.partial(compressed, wd) for wd in widths])

    wlen = WINDOW + tq

    def window(kvw, masked):
        sw = masked(_dot_nt(q, kvw[:, 0:HEAD_DIM]))
        pw = jnp.exp(sw - jnp.max(sw, axis=-1, keepdims=True))
        return _dot(pw.astype(BF16), kvw)[:, HEAD_DIM:2 * HEAD_DIM] * (1.0 / jnp.sum(pw, axis=-1, keepdims=True))

    def window_interior():
        def masked(sw):
            rq = lax.broadcasted_iota(jnp.int32, (sw.shape[0], tq), 0) & (tq - 1)
            c = lax.broadcasted_iota(jnp.int32, (sw.shape[0], tq), 1)
            return jnp.concatenate([jnp.where(c > rq, sw[:, 0:tq], NEG), sw[:, tq:WINDOW],
                                    jnp.where(c <= rq, sw[:, WINDOW:wlen], NEG)], axis=1)

        return window(kvw_ref[0, 0, pl.ds(pl.multiple_of(s0 - WINDOW, tq), wlen), :], masked)

    def window_start():
        def masked(sw):
            kabs = lax.broadcasted_iota(jnp.int32, sw.shape, 1)
            t = s0 + (lax.broadcasted_iota(jnp.int32, sw.shape, 0) & (tq - 1))
            return jnp.where((kabs <= t) & (kabs > t - WINDOW), sw, NEG)

        return window(kvw_ref[0, 0, 0:wlen, :], masked)

    o_w = lax.cond(s0 >= WINDOW, window_interior, window_start)

    pslc_t = pslc.T
    nb = pslc_t.shape[0]
    blk = lax.broadcasted_iota(jnp.int32, pslc_t.shape, 0)
    cur = (s0 + lax.broadcasted_iota(jnp.int32, pslc_t.shape, 1)) // SEL_BLOCK
    c0 = s0 // SEL_BLOCK
    elig = (blk >= 1) & (blk <= cur - 2)
    sel = _pick_top(jnp.where(elig, pslc_t, -jnp.inf), blk, nb, SEL_TOPK - 3)
    sel = sel | (blk == 0) | (blk == cur - 1) | (blk == cur)
    past = jnp.where(sel & (blk < c0), 0.0, NEG).T.astype(BF16)
    own = jnp.where(sel & (blk >= c0), 0.0, NEG).T.astype(BF16)
    for v, bias in enumerate((past, own)):
        for st in range(n_super):
            b = bias[:, st * SEL_LANES:(st + 1) * SEL_LANES]
            qa_ref[v * n_super + st] = jnp.concatenate([q, jnp.concatenate([b] * NSA_GROUP, axis=0)], axis=1)

    _softmax_init(m_ref, acc_ref)

    def step(start, size):
        s = _dot_nt(qa_ref[start // SUPER_KEYS], ka_ref[0, 0, pl.ds(start, size), :])
        _softmax_step(s, va_ref[0, 0, pl.ds(start, size), :], m_ref, acc_ref)

    _past_keys_loop(s0, NSA_TK, step)
    d0 = pl.multiple_of(s0, tq)
    s = _dot_nt(qa_ref[n_super + s0 // SUPER_KEYS], ka_ref[0, 0, pl.ds(d0, tq), :])
    qpos = lax.broadcasted_iota(jnp.int32, s.shape, 0) & (tq - 1)
    kpos = lax.broadcasted_iota(jnp.int32, s.shape, 1)
    hidden = (qpos // SEL_BLOCK == kpos // SEL_BLOCK) & (kpos > qpos)
    _softmax_step(jnp.where(hidden, NEG, s), va_ref[0, 0, pl.ds(d0, tq), :], m_ref, acc_ref)
    o_s = _softmax_result(acc_ref)

    w = NSA_GROUP * HEAD_DIM
    gexp = _dot_x2(gt_ref[...], e_ref[0])
    wide = lambda x: jnp.concatenate([x[g * tq:(g + 1) * tq] for g in range(NSA_GROUP)], axis=1)
    o_ref[...] = (gexp[:, 0:w] * wide(o_c) + gexp[:, w:2 * w] * wide(o_s) + gexp[:, 2 * w:3 * w] * wide(o_w)).astype(BF16)


def _nsa(qs, kcmp, vcmp, ka, va, kvw, gates, batch, seq):
    tq = NSA_TQ
    nq = seq // tq
    nb = seq // SEL_BLOCK
    n16 = seq // CMP_STRIDE
    assert seq % SUPER_KEYS == 0
    n_super = seq // SUPER_KEYS
    rows = NSA_GROUP * tq
    e = np.zeros((NSA_KV_HEADS, 128, 3 * NSA_GROUP * HEAD_DIM), np.float32)
    for br in range(3):
        for hk in range(NSA_KV_HEADS):
            for g in range(NSA_GROUP):
                c = (br * NSA_GROUP + g) * HEAD_DIM
                e[hk, br * NSA_HEADS + hk * NSA_GROUP + g, c:c + HEAD_DIM] = 1.0
    nn, jj = np.arange(n16)[:, None], np.arange(nb)[None, :]
    band = ((nn >= 4 * jj - 1) & (nn <= 4 * jj + 3)).astype(np.float32)
    resident = lambda width: pl.BlockSpec((1, 1, seq, width), lambda b, h, i: (b, h, 0, 0))
    w = NSA_GROUP * HEAD_DIM
    return pl.pallas_call(
        functools.partial(_nsa_kernel, n_super),
        grid=(batch, NSA_KV_HEADS, nq),
        in_specs=[pl.BlockSpec((1, 1, 1, rows, HEAD_DIM), lambda b, h, i: (b, h, i, 0, 0)),
                  pl.BlockSpec((1, 1, n16, HEAD_DIM), lambda b, h, i: (b, h, 0, 0)),
                  pl.BlockSpec((1, 1, n16, HEAD_DIM), lambda b, h, i: (b, h, 0, 0)),
                  resident(LANES), resident(LANES), resident(LANES),
                  pl.BlockSpec((tq, 128), lambda b, h, i: (b * nq + i, 0)),
                  pl.BlockSpec((1, 128, 3 * w), lambda b, h, i: (h, 0, 0)),
                  pl.BlockSpec((n16, nb), lambda b, h, i: (0, 0))],
        out_specs=pl.BlockSpec((tq, w), lambda b, h, i: (b * nq + i, h)),
        out_shape=jax.ShapeDtypeStruct((batch * seq, NSA_W), BF16),
        scratch_shapes=[pltpu.VMEM((2 * n_super, rows, LANES), BF16), pltpu.VMEM((rows, LANES), F32),
                        pltpu.VMEM((rows, LANES), F32)],
        compiler_params=_params("arbitrary", "arbitrary", "arbitrary"),
        name="nsa",
    )(qs, kcmp, vcmp, ka, va, kvw, gates, jnp.asarray(e, BF16), jnp.asarray(band, BF16))


def _odd_out_kernel(om_ref, on_ref, h_ref, w_ref, o_ref):
    acc = h_ref[...] + _dot(on_ref[...], w_ref[MOBA_W:D_MODEL, :])
    for h in range(MOBA_HEADS):
        acc = acc + _dot(om_ref[0, h], w_ref[h * HEAD_DIM:(h + 1) * HEAD_DIM, :])
    o_ref[...] = acc


def _odd_out(o_moba, o_nsa, h2, w_out, seq, tm=512):
    t = h2.shape[0]
    tps = seq // tm
    row = lambda w: pl.BlockSpec((tm, w), lambda i: (i, 0))
    return pl.pallas_call(
        _odd_out_kernel,
        grid=(t // tm,),
        in_specs=[pl.BlockSpec((1, MOBA_HEADS, tm, HEAD_DIM), lambda i: (i // tps, 0, i % tps, 0)),
                  row(NSA_W), row(D_MODEL), _full((D_MODEL, D_MODEL))],
        out_specs=row(D_MODEL),
        out_shape=jax.ShapeDtypeStruct((t, D_MODEL), F32),
        compiler_params=_params("arbitrary"),
        name="odd_out",
    )(o_moba, o_nsa, h2, w_out)


def _even_layer(h2, batch, seq, norm, w_in, w_out, lam_re, lam_im, log_dt, b_re, b_im, c_re, c_im, d, w_glu, conv_w, conv_b):
    u, ub, yb = _even_in(h2, norm.reshape(1, D_MODEL), w_in.astype(BF16), conv_w, conv_b.reshape(1, CONV_WIDTH), seq)
    ypre = _s5_mixer_pre(ub, batch, seq, lam_re, lam_im, log_dt, b_re, b_im, c_re, c_im)
    return _even_out(ypre, u, yb, h2, d.reshape(1, S5_WIDTH), w_glu.astype(BF16), w_out.astype(BF16))


def _odd_layer(h2, batch, seq, norm, w_in, w_out, moba_q_norm, moba_k_norm, nsa_q_norm, nsa_kcmp_norm, nsa_ksel_norm,
               nsa_kwin_norm, cmp_pe_k, cmp_w1_k, cmp_w2_k, cmp_pe_v, cmp_w1_v, cmp_w2_v):
    qm, kam, kmean, vam, qs, kc, vc, kas, vas, kvw, gates = _odd_in(
        h2, batch, seq, norm.reshape(1, D_MODEL), w_in, moba_q_norm, moba_k_norm, nsa_q_norm, nsa_ksel_norm, nsa_kwin_norm)
    cmp = _compress(kc, vc, batch, seq, cmp_pe_k, cmp_w1_k, cmp_w2_k, cmp_pe_v, cmp_w1_v, cmp_w2_v, nsa_kcmp_norm)
    o_moba = _moba(qm, kam, kmean, vam, batch, seq)
    o_nsa = _nsa(qs, cmp[0], cmp[1], kas, vas, kvw, gates, batch, seq)
    return _odd_out(o_moba, o_nsa, h2, w_out.astype(BF16), seq)


def kernel(x, ev_norm_mix, ev_w_in, ev_w_out, s5_lam_re, s5_lam_im, s5_log_dt, s5_b_re, s5_b_im, s5_c_re, s5_c_im, s5_d, s5_w_glu, conv_w, conv_b, od_norm_mix, od_w_in, od_w_out, moba_q_norm, moba_k_norm, nsa_q_norm, nsa_kcmp_norm, nsa_ksel_norm, nsa_kwin_norm, cmp_pe_k, cmp_w1_k, cmp_w2_k, cmp_pe_v, cmp_w1_v, cmp_w2_v, moe_norm, moe_w_group, moe_b_group, moe_w_expert, moe_b_expert, moe_w_gate, moe_w_up, moe_w_down):
    batch, seq, _ = x.shape
    depth = moe_norm.shape[0]
    h = x.reshape(batch * seq, D_MODEL)
    for layer in range(depth):
        i = layer // 2
        if layer % 2 == 0:
            h = _even_layer(h, batch, seq, ev_norm_mix[i], ev_w_in[i], ev_w_out[i], s5_lam_re[i], s5_lam_im[i], s5_log_dt[i],
                            s5_b_re[i], s5_b_im[i], s5_c_re[i], s5_c_im[i], s5_d[i], s5_w_glu[i], conv_w[i], conv_b[i])
        else:
            h = _odd_layer(h, batch, seq, od_norm_mix[i], od_w_in[i], od_w_out[i], moba_q_norm[i], moba_k_norm[i],
                           nsa_q_norm[i], nsa_kcmp_norm[i], nsa_ksel_norm[i], nsa_kwin_norm[i], cmp_pe_k[i], cmp_w1_k[i],
                           cmp_w2_k[i], cmp_pe_v[i], cmp_w1_v[i], cmp_w2_v[i])
        h = _moe(h, moe_norm[layer].reshape(1, D_MODEL), moe_w_group[layer], moe_b_group[layer], moe_w_expert[layer],
                 moe_b_expert[layer], moe_w_gate[layer], moe_w_up[layer], moe_w_down[layer])
    return h.reshape(batch, seq, D_MODEL)
```

```python
import functools
import math

import jax
import jax.numpy as jnp
import numpy as np
from jax import lax
from jax.experimental import pallas as pl
from jax.experimental.pallas import tpu as pltpu

D_MODEL = 1024
HEAD_DIM = 64
EPS = 1e-6
S5_WIDTH = 256
S5_GROUP = 16
S5_GROUPS = 16
S5_STATE = 64
S5_CHUNK = 16
CONV_WIDTH = 768
CONV_K = 3
MOBA_HEADS = 4
NSA_HEADS = 12
NSA_KV_HEADS = 2
NSA_GROUP = 6
MOBA_W = 256
NSA_W = 768
KV_W = 128
MOBA_BLOCK = 256
MOBA_TOPK = 3
CMP_BLOCK = 32
CMP_STRIDE = 16
CMP_HIDDEN = 256
SEL_BLOCK = 64
SEL_TOPK = 8
WINDOW = 512
N_GROUPS = 4
EXPERTS_PER_GROUP = 4
N_EXPERTS = 16
EXPERT_FF = 256

VMEM_LIMIT_BYTES = 56 * 1024 * 1024
NEG = -float(2 ** 30)
F32 = jnp.float32
BF16 = jnp.bfloat16


def _params(*semantics):
    return pltpu.CompilerParams(dimension_semantics=semantics, vmem_limit_bytes=VMEM_LIMIT_BYTES)


def _dot(a, b):
    return jnp.dot(a, b, preferred_element_type=F32)


def _dot_nt(a, b):
    return lax.dot_general(a, b, (((1,), (1,)), ((), ())), preferred_element_type=F32)


def _split(x):
    hi = x.astype(BF16)
    lo = (x - hi.astype(F32)).astype(BF16)
    return hi, lo


def _dot_x2(x, w):
    hi, lo = _split(x)
    return _dot(hi, w) + _dot(lo, w)


def _dot_x3(x, w_hi, w_lo):
    hi, lo = _split(x)
    return _dot(hi, w_hi) + (_dot(hi, w_lo) + _dot(lo, w_hi))


def _rms(x, gain):
    return x * lax.rsqrt(jnp.mean(x * x, axis=-1, keepdims=True) + EPS) * gain


def _gelu(x):
    return 0.5 * x * (1.0 + jnp.tanh(math.sqrt(2.0 / math.pi) * (x + 0.044715 * (x * x * x))))


def _sigmoid(x):
    return 1.0 / (1.0 + jnp.exp(-x))


def _full(shape):
    n = len(shape)
    return pl.BlockSpec(shape, lambda *_: (0,) * n)


def _even_in_kernel(tiles_per_seq, x_ref, g_ref, w_ref, cw_ref, cb_ref, u_ref, ub_ref, yb_ref, carry_ref):
    i = pl.program_id(0)
    xn = _rms(x_ref[...], g_ref[...]).astype(BF16)
    u = _dot(xn, w_ref[:, 0:S5_WIDTH])
    u_ref[...] = u
    ub_ref[...] = u.astype(BF16)
    o = S5_WIDTH
    xc = _dot(xn, w_ref[:, o:o + CONV_WIDTH])
    gb = _dot(xn, w_ref[:, o + CONV_WIDTH:o + 2 * CONV_WIDTH])
    gc = _dot(xn, w_ref[:, o + 2 * CONV_WIDTH:o + 3 * CONV_WIDTH])
    z = gc * xc
    tm = z.shape[0]

    @pl.when(i % tiles_per_seq == 0)
    def _():
        carry_ref[...] = jnp.zeros_like(carry_ref)

    row = lax.broadcasted_iota(jnp.int32, z.shape, 0)
    prev1 = carry_ref[7:8, :]
    prev2 = carry_ref[6:7, :]
    z1 = jnp.where(row == 0, prev1, pltpu.roll(z, 1, 0))
    z2 = jnp.where(row == 0, prev2, jnp.where(row == 1, prev1, pltpu.roll(z, 2, 0)))
    y = cw_ref[0:1, :] * z2 + cw_ref[1:2, :] * z1 + cw_ref[2:3, :] * z + cb_ref[...]
    yb_ref[...] = (gb * y).astype(BF16)
    carry_ref[...] = z[tm - 8:tm, :]


def _even_in(x2, gain, w_in, conv_w, conv_b, seq, tm=512):
    t = x2.shape[0]
    n_in = w_in.shape[1]
    return pl.pallas_call(
        functools.partial(_even_in_kernel, seq // tm),
        grid=(t // tm,),
        in_specs=[pl.BlockSpec((tm, D_MODEL), lambda i: (i, 0)), _full((1, D_MODEL)),
                  _full((D_MODEL, n_in)), _full((CONV_K, CONV_WIDTH)), _full((1, CONV_WIDTH))],
        out_specs=[pl.BlockSpec((tm, S5_WIDTH), lambda i: (i, 0)), pl.BlockSpec((tm, S5_WIDTH), lambda i: (i, 0)),
                   pl.BlockSpec((tm, CONV_WIDTH), lambda i: (i, 0))],
        out_shape=[jax.ShapeDtypeStruct((t, S5_WIDTH), F32), jax.ShapeDtypeStruct((t, S5_WIDTH), BF16),
                   jax.ShapeDtypeStruct((t, CONV_WIDTH), BF16)],
        scratch_shapes=[pltpu.VMEM((8, CONV_WIDTH), F32)],
        compiler_params=_params("arbitrary"),
        name="even_in",
    )(x2, gain, w_in, conv_w, conv_b)


def _s5_weights(lam_re, lam_im, log_dt, b_re, b_im, c_re, c_im):
    g, p, hg, ck = S5_GROUPS, S5_STATE, S5_GROUP, S5_CHUNK
    lr, li = lam_re.astype(F32), lam_im.astype(F32)
    dt = jnp.exp(log_dt.astype(F32))[:, None]
    mag = jnp.exp(lr * dt)
    a_re, a_im = mag * jnp.cos(li * dt), mag * jnp.sin(li * dt)
    den = lr * lr + li * li
    f_re = ((a_re - 1.0) * lr + a_im * li) / den
    f_im = (a_im * lr - (a_re - 1.0) * li) / den
    br, bi = b_re.astype(F32), b_im.astype(F32)
    bb_re = f_re[..., None] * br - f_im[..., None] * bi
    bb_im = f_re[..., None] * bi + f_im[..., None] * br
    pw_re, pw_im = [jnp.ones_like(a_re)], [jnp.zeros_like(a_im)]
    for _ in range(ck):
        r, m = pw_re[-1], pw_im[-1]
        pw_re.append(r * a_re - m * a_im)
        pw_im.append(r * a_im + m * a_re)
    pw_re, pw_im = jnp.stack(pw_re), jnp.stack(pw_im)
    cr, ci = c_re.astype(F32), c_im.astype(F32)
    rev_re, rev_im = pw_re[ck - 1::-1][:ck], pw_im[ck - 1::-1][:ck]
    ws_re = rev_re[:, :, :, None] * bb_re[None] - rev_im[:, :, :, None] * bb_im[None]
    ws_im = rev_re[:, :, :, None] * bb_im[None] + rev_im[:, :, :, None] * bb_re[None]
    ca_re = cr[None] * pw_re[1:, :, None, :] - ci[None] * pw_im[1:, :, None, :]
    ca_im = cr[None] * pw_im[1:, :, None, :] + ci[None] * pw_re[1:, :, None, :]
    cb_re = jnp.einsum('ghp,kgp,gpj->kghj', cr, pw_re[:ck], bb_re) - jnp.einsum('ghp,kgp,gpj->kghj', cr, pw_im[:ck], bb_im) \
        - jnp.einsum('ghp,kgp,gpj->kghj', ci, pw_re[:ck], bb_im) - jnp.einsum('ghp,kgp,gpj->kghj', ci, pw_im[:ck], bb_re)
    lag = np.arange(ck)[None, :] - np.arange(ck)[:, None]
    tz = cb_re[np.clip(lag, 0, ck - 1)]
    tz = jnp.where((lag >= 0)[:, :, None, None, None], tz, 0.0)
    cw = ck * g * hg
    ws = jnp.stack([ws_re, ws_im]).transpose(1, 2, 4, 0, 3).reshape(cw, 2 * p)
    wc = jnp.stack([ca_re, -ca_im]).transpose(0, 2, 4, 1, 3).reshape(2 * g * p, ck * hg)
    tzc = tz.transpose(0, 2, 4, 1, 3).reshape(cw, ck * hg)
    return ws.astype(BF16), wc.astype(BF16), tzc.astype(BF16), pw_re[ck].reshape(1, g * p), pw_im[ck].reshape(1, g * p)


def _group_expand(compact, expand, row_shift, col_shift, col0):
    full = _dot(compact, expand)
    row = lax.broadcasted_iota(jnp.int32, full.shape, 0)
    col = col0 + lax.broadcasted_iota(jnp.int32, full.shape, 1)
    same = ((row >> row_shift) & (S5_GROUPS - 1)) == ((col >> col_shift) & (S5_GROUPS - 1))
    return jnp.where(same, full, 0.0).astype(BF16)


def _s5_state_kernel(u_ref, ws_ref, e_ref, s_ref, w_scr):
    @pl.when(pl.program_id(1) == 0)
    def _():
        w_scr[...] = _group_expand(ws_ref[...], e_ref[...], 4, 6, pl.program_id(0) * w_scr.shape[1])

    s_ref[...] = _dot(u_ref[...], w_scr[...])


def _s5_scan_kernel(s_ref, are_ref, aim_ref, xprev_ref, st_ref):
    @pl.when(pl.program_id(0) == 0)
    def _():
        st_ref[...] = jnp.zeros_like(st_ref)

    a_re, a_im = are_ref[...], aim_ref[...]
    nb, n = s_ref.shape[0], s_ref.shape[1]
    half = a_re.shape[1]

    def body(c, carry):
        out = []
        for b in range(nb):
            xr, xi = carry[2 * b], carry[2 * b + 1]
            xprev_ref[b, pl.ds(c, 1), 0:half] = xr
            xprev_ref[b, pl.ds(c, 1), half:2 * half] = xi
            s = s_ref[b, pl.ds(c, 1), :]
            out += [a_re * xr - a_im * xi + s[:, 0:half], a_re * xi + a_im * xr + s[:, half:2 * half]]
        return tuple(out)

    init = tuple(st_ref[b:b + 1, o:o + half] for b in range(nb) for o in (0, half))
    final = lax.fori_loop(0, n, body, init, unroll=4)
    for b in range(nb):
        st_ref[b:b + 1, 0:half] = final[2 * b]
        st_ref[b:b + 1, half:2 * half] = final[2 * b + 1]


def _s5_out_kernel(u_ref, xp_ref, tz_ref, wc_ref, e_ref, y_ref, tz_scr, wc_scr):
    @pl.when(pl.program_id(1) == 0)
    def _():
        col0 = pl.program_id(0) * tz_scr.shape[1]
        tz_scr[...] = _group_expand(tz_ref[...], e_ref[...], 4, 4, col0)
        wc_scr[...] = _group_expand(wc_ref[...], e_ref[...], 6, 4, col0)

    y_ref[...] = _dot(u_ref[...], tz_scr[...]) + _dot(xp_ref[...].astype(BF16), wc_scr[...])


def _expand_matrix(outer, inner):
    e = np.zeros((outer, inner, outer, S5_GROUPS, inner), np.float32)
    for x in range(outer):
        for y in range(inner):
            e[x, y, x, :, y] = 1.0
    return jnp.asarray(e.reshape(outer * inner, outer * S5_GROUPS * inner), BF16)


def _s5_mixer_pre(ub, batch, seq, lam_re, lam_im, log_dt, b_re, b_im, c_re, c_im):
    ws, wc, tz, a16_re, a16_im = _s5_weights(lam_re, lam_im, log_dt, b_re, b_im, c_re, c_im)
    nc = seq // S5_CHUNK
    rows = batch * nc
    cw = S5_CHUNK * S5_WIDTH
    sw = 2 * S5_GROUPS * S5_STATE
    tr = min(rows, 512)
    tn = 512
    uc = ub.reshape(rows, cw)
    e_state = _expand_matrix(2, S5_STATE)
    e_out = _expand_matrix(S5_CHUNK, S5_GROUP)
    s = pl.pallas_call(
        _s5_state_kernel,
        grid=(sw // tn, rows // tr),
        in_specs=[pl.BlockSpec((tr, cw), lambda j, i: (i, 0)), _full(ws.shape),
                  pl.BlockSpec((e_state.shape[0], tn), lambda j, i: (0, j))],
        out_specs=pl.BlockSpec((tr, tn), lambda j, i: (i, j)),
        out_shape=jax.ShapeDtypeStruct((rows, sw), F32),
        scratch_shapes=[pltpu.VMEM((cw, tn), BF16)],
        compiler_params=_params("arbitrary", "arbitrary"),
        name="s5_state",
    )(uc, ws, e_state)
    tc = min(nc, 256)
    xprev = pl.pallas_call(
        _s5_scan_kernel,
        grid=(nc // tc,),
        in_specs=[pl.BlockSpec((batch, tc, sw), lambda i: (0, i, 0)), _full((1, sw // 2)), _full((1, sw // 2))],
        out_specs=pl.BlockSpec((batch, tc, sw), lambda i: (0, i, 0)),
        out_shape=jax.ShapeDtypeStruct((batch, nc, sw), F32),
        scratch_shapes=[pltpu.VMEM((batch, sw), F32)],
        compiler_params=_params("arbitrary"),
        name="s5_scan",
    )(s.reshape(batch, nc, sw), a16_re, a16_im)
    y = pl.pallas_call(
        _s5_out_kernel,
        grid=(cw // tn, rows // tr),
        in_specs=[pl.BlockSpec((tr, cw), lambda j, i: (i, 0)), pl.BlockSpec((tr, sw), lambda j, i: (i, 0)),
                  _full(tz.shape), _full(wc.shape), pl.BlockSpec((e_out.shape[0], tn), lambda j, i: (0, j))],
        out_specs=pl.BlockSpec((tr, tn), lambda j, i: (i, j)),
        out_shape=jax.ShapeDtypeStruct((rows, cw), F32),
        scratch_shapes=[pltpu.VMEM((cw, tn), BF16), pltpu.VMEM((sw, tn), BF16)],
        compiler_params=_params("arbitrary", "arbitrary"),
        name="s5_out",
    )(uc, xprev.reshape(rows, sw), tz, wc, e_out)
    return y.reshape(batch * seq, S5_WIDTH)


def _even_out_kernel(ypre_ref, u_ref, yb_ref, x_ref, d_ref, wglu_ref, wout_ref, o_ref):
    y = _gelu(ypre_ref[...] + d_ref[...] * u_ref[...])
    y = y * _sigmoid(_dot(y.astype(BF16), wglu_ref[...]))
    o_ref[...] = (x_ref[...] + _dot(y.astype(BF16), wout_ref[0:S5_WIDTH, :])
                  + _dot(yb_ref[...], wout_ref[S5_WIDTH:D_MODEL, :]))


def _even_out(ypre, u, yb, x2, d, w_glu, w_out, tm=512):
    t = x2.shape[0]
    row = lambda w: pl.BlockSpec((tm, w), lambda i: (i, 0))
    return pl.pallas_call(
        _even_out_kernel,
        grid=(t // tm,),
        in_specs=[row(S5_WIDTH), row(S5_WIDTH), row(CONV_WIDTH), row(D_MODEL), _full((1, S5_WIDTH)),
                  _full((S5_WIDTH, S5_WIDTH)), _full((D_MODEL, D_MODEL))],
        out_specs=row(D_MODEL),
        out_shape=jax.ShapeDtypeStruct((t, D_MODEL), F32),
        compiler_params=_params("arbitrary"),
        name="even_out",
    )(ypre, u, yb, x2, d, w_glu, w_out)


def _first_max(v, pos, width, axis=-1):
    m = jnp.max(v, axis=axis, keepdims=True)
    idx = jnp.min(jnp.where(v == m, pos, width), axis=axis, keepdims=True)
    return m, idx


MOE_TM = 1024
MOE_ALIGN = 16
MOE_SLOTS = 1152
MOE_WIN = 320
ROUTER_LANES = 128


def _moe_router_kernel(h_ref, g_ref, wr_hi_ref, wr_lo_ref, br_ref, gate_ref, grp_ref, cnt_ref):
    xn = _rms(h_ref[...], g_ref[...])
    logits = _dot_x3(xn, wr_hi_ref[...], wr_lo_ref[...]) + br_ref[...]
    lane = lax.broadcasted_iota(jnp.int32, logits.shape, 1)
    width = logits.shape[1]
    is_g = lane < N_GROUPS
    gl = jnp.where(is_g, logits, -jnp.inf)
    gm, gi = _first_max(gl, lane, width)
    gw = 1.0 / jnp.sum(jnp.where(is_g, jnp.exp(gl - gm), 0.0), axis=-1, keepdims=True)
    lo = N_GROUPS + gi * EXPERTS_PER_GROUP
    in_grp = (lane >= lo) & (lane < lo + EXPERTS_PER_GROUP)
    el = jnp.where(in_grp, logits, -jnp.inf)
    m1, i1 = _first_max(el, lane, width)
    m2, i2 = _first_max(jnp.where(lane == i1, -jnp.inf, el), lane, width)
    p2 = jnp.exp(m2 - m1)
    w1 = gw / (1.0 + p2)
    w2 = gw * p2 / (1.0 + p2)
    gate_ref[...] = jnp.where(lane == i1, w1, 0.0) + jnp.where(lane == i2, w2, 0.0)
    grp = (lane == gi).astype(F32)
    grp_ref[...] = grp.astype(BF16)
    cnt_ref[0] = jnp.broadcast_to(jnp.sum(grp, axis=0, keepdims=True), cnt_ref.shape[1:])


def _moe_expert_kernel(base_ref, nwin_ref, h_ref, g_ref, gate_ref, grp_ref, wg_ref, wu_ref, wd_ref, o_ref,
                       xs_ref, gs_ref, ys_ref, pt_ref):
    i, g = pl.program_id(0), pl.program_id(1)
    tm = h_ref.shape[0]

    @pl.when((i == 0) & (g == 0))
    def _():
        xs_ref[...] = jnp.zeros_like(xs_ref)
        gs_ref[...] = jnp.zeros_like(gs_ref)
        ys_ref[...] = jnp.zeros_like(ys_ref)

    @pl.when(g == 0)
    def _():
        xn = _rms(h_ref[...], g_ref[...]).astype(BF16)
        grp = grp_ref[...]
        earlier = (lax.broadcasted_iota(jnp.int32, (tm, tm), 0) > lax.broadcasted_iota(jnp.int32, (tm, tm), 1)).astype(BF16)
        rank = _dot(earlier, grp)
        lane = lax.broadcasted_iota(jnp.int32, rank.shape, 1)
        for k in range(N_GROUPS):
            rank = rank + jnp.where(lane == k, base_ref[i * N_GROUPS + k].astype(F32), 0.0)
        slot = jnp.sum(grp.astype(F32) * rank, axis=-1, keepdims=True).astype(jnp.int32)
        pt = (lax.broadcasted_iota(jnp.int32, (tm, MOE_SLOTS), 1) == slot).astype(BF16)
        pt_ref[...] = pt
        gather = lambda x: lax.dot_general(pt, x, (((0,), (0,)), ((), ())), preferred_element_type=F32)
        xs_ref[0:MOE_SLOTS, :] = gather(xn).astype(BF16)
        gate_hi, gate_lo = _split(gate_ref[...])
        gs_ref[0:MOE_SLOTS, :] = gather(gate_hi) + gather(gate_lo)

    def window(w, carry):
        r0 = pl.multiple_of(base_ref[i * N_GROUPS + g] + w * MOE_WIN, MOE_ALIGN)
        x = xs_ref[pl.ds(r0, MOE_WIN), :]
        gate = gs_ref[pl.ds(r0, MOE_WIN), :]
        lane = lax.broadcasted_iota(jnp.int32, gate.shape, 1)
        y = None
        for j in range(EXPERTS_PER_GROUP):
            ge = jnp.sum(jnp.where(lane == g * EXPERTS_PER_GROUP + (j + N_GROUPS), gate, 0.0), axis=-1, keepdims=True)
            h1 = _dot(x, wg_ref[j])
            h3 = _dot(x, wu_ref[j])
            act = (h1 * _sigmoid(h1)) * h3 * ge
            yj = _dot(act.astype(BF16), wd_ref[j])
            y = yj if y is None else y + yj
        ys_ref[pl.ds(r0, MOE_WIN), :] = y.astype(BF16)
        return carry

    lax.fori_loop(0, nwin_ref[i * N_GROUPS + g], window, 0)

    @pl.when(g == N_GROUPS - 1)
    def _():
        o_ref[...] = h_ref[...] + _dot(pt_ref[...], ys_ref[0:MOE_SLOTS, :])


def _moe(h2, gain, w_group, b_group, w_expert, b_expert, w_gate, w_up, w_down):
    t = h2.shape[0]
    tm, rw = MOE_TM, ROUTER_LANES
    assert t % tm == 0 and MOE_SLOTS >= tm + N_GROUPS * (MOE_ALIGN - 1) and MOE_WIN % MOE_ALIGN == 0
    n_tiles = t // tm
    wr = jnp.zeros((D_MODEL, rw), F32).at[:, 0:N_GROUPS].set(w_group).at[:, N_GROUPS:N_GROUPS + N_EXPERTS].set(w_expert)
    br = jnp.zeros((1, rw), F32).at[0, 0:N_GROUPS].set(b_group).at[0, N_GROUPS:N_GROUPS + N_EXPERTS].set(b_expert)
    wr_hi = wr.astype(BF16)
    wr_lo = (wr - wr_hi.astype(F32)).astype(BF16)
    row = lambda width: pl.BlockSpec((tm, width), lambda i: (i, 0))
    gates, grp, cnt = pl.pallas_call(
        _moe_router_kernel,
        grid=(n_tiles,),
        in_specs=[row(D_MODEL), _full((1, D_MODEL)), _full((D_MODEL, rw)), _full((D_MODEL, rw)), _full((1, rw))],
        out_specs=[row(rw), row(rw), pl.BlockSpec((1, 8, rw), lambda i: (i, 0, 0))],
        out_shape=[jax.ShapeDtypeStruct((t, rw), F32), jax.ShapeDtypeStruct((t, rw), BF16),
                   jax.ShapeDtypeStruct((n_tiles, 8, rw), F32)],
        compiler_params=_params("arbitrary"),
        name="moe_router",
    )(h2, gain, wr_hi, wr_lo, br)
    n = cnt[:, 0, 0:N_GROUPS].astype(jnp.int32)
    padded = (n + (MOE_ALIGN - 1)) // MOE_ALIGN * MOE_ALIGN
    base = (jnp.cumsum(padded, axis=1) - padded).reshape(-1)
    nwin = ((padded + (MOE_WIN - 1)) // MOE_WIN).reshape(-1)
    tile = lambda width: pl.BlockSpec((tm, width), lambda i, g, *_: (i, 0))
    experts = lambda shape: pl.BlockSpec((EXPERTS_PER_GROUP,) + shape, lambda i, g, *_: (g, 0, 0))
    slots = MOE_SLOTS + MOE_WIN
    return pl.pallas_call(
        _moe_expert_kernel,
        grid_spec=pltpu.PrefetchScalarGridSpec(
            num_scalar_prefetch=2,
            grid=(n_tiles, N_GROUPS),
            in_specs=[tile(D_MODEL), pl.BlockSpec((1, D_MODEL), lambda i, g, *_: (0, 0)), tile(rw), tile(rw),
                      experts((D_MODEL, EXPERT_FF)), experts((D_MODEL, EXPERT_FF)), experts((EXPERT_FF, D_MODEL))],
            out_specs=tile(D_MODEL),
            scratch_shapes=[pltpu.VMEM((slots, D_MODEL), BF16), pltpu.VMEM((slots, rw), F32),
                            pltpu.VMEM((slots, D_MODEL), BF16), pltpu.VMEM((tm, MOE_SLOTS), BF16)]),
        out_shape=jax.ShapeDtypeStruct((t, D_MODEL), F32),
        compiler_params=_params("arbitrary", "arbitrary"),
        name="moe",
    )(base, nwin, h2, gain, gates, grp, w_gate.astype(BF16), w_up.astype(BF16), w_down.astype(BF16))


ODD_SPLITS = (MOBA_W, MOBA_W, MOBA_W, NSA_W, KV_W, KV_W, KV_W, KV_W, KV_W, KV_W, 128)
ODD_IN_PAD = sum(ODD_SPLITS)


def _head_rms(x, hsum, gain):
    w = x.shape[1]
    ss = jnp.concatenate([_dot_x2(x[:, o:o + hsum.shape[0]] * x[:, o:o + hsum.shape[0]], hsum)
                          for o in range(0, w, hsum.shape[0])], axis=1) if w > hsum.shape[0] else _dot_x2(x * x, hsum)
    return x * lax.rsqrt(ss * (1.0 / HEAD_DIM) + EPS) * gain


def _odd_in_kernel(tiles_per_seq, x_ref, g_ref, w_ref, hsum_ref, gq_ref, gk_ref, gnq_ref, gks_ref, gkw_ref,
                   qm_ref, kam_ref, kmean_ref, vam_ref, qs_ref, kc_ref, vc_ref, kas_ref, vas_ref, kvw_ref, gt_ref):
    xn = _rms(x_ref[...], g_ref[...]).astype(BF16)
    offs = np.cumsum((0,) + ODD_SPLITS)
    col = lambda j: _dot(xn, w_ref[:, int(offs[j]):int(offs[j + 1])])
    head = lambda x, h: x[:, h * HEAD_DIM:(h + 1) * HEAD_DIM]
    hsum = hsum_ref[...]
    hsum128 = hsum_ref[0:128, 0:128]
    tm = x_ref.shape[0]
    pos = (pl.program_id(0) % tiles_per_seq) * tm + lax.broadcasted_iota(jnp.int32, (tm, HEAD_DIM), 0)
    lane = lax.broadcasted_iota(jnp.int32, (tm, HEAD_DIM), 1)
    ones_col = (lane == 0).astype(BF16)

    qm = _head_rms(col(0), hsum, gq_ref[...])
    km = _head_rms(col(1), hsum, gk_ref[...])
    for j in range(tm // MOBA_BLOCK):
        kmean_ref[0, j:j + 1, :] = jnp.mean(km[j * MOBA_BLOCK:(j + 1) * MOBA_BLOCK, :], axis=0, keepdims=True)
    km = km.astype(BF16)
    vm = col(2).astype(BF16)
    moba_id = (lane == pos // MOBA_BLOCK).astype(BF16)
    for h in range(MOBA_HEADS):
        qm_ref[0, h] = head(qm, h)
        kam_ref[0, h] = jnp.concatenate([head(km, h), moba_id], axis=1)
        vam_ref[0, h] = jnp.concatenate([head(vm, h), ones_col], axis=1)

    qd = (_head_rms(col(3), hsum, gnq_ref[...]) * (HEAD_DIM ** -0.5)).astype(BF16)
    for hk in range(NSA_KV_HEADS):
        for j in range(tm // NSA_TQ):
            for g in range(NSA_GROUP):
                qs_ref[0, hk, j, g * NSA_TQ:(g + 1) * NSA_TQ, :] = head(qd, hk * NSA_GROUP + g)[j * NSA_TQ:(j + 1) * NSA_TQ, :]
    kc_ref[...] = col(4).astype(BF16)
    vc_ref[...] = col(5).astype(BF16)
    ks = _head_rms(col(6), hsum128, gks_ref[...]).astype(BF16)
    vs = col(7).astype(BF16)
    kw = _head_rms(col(8), hsum128, gkw_ref[...]).astype(BF16)
    vw = col(9).astype(BF16)
    sel_id = (lane == (pos // SEL_BLOCK) % SEL_LANES).astype(BF16)
    for hk in range(NSA_KV_HEADS):
        kas_ref[0, hk] = jnp.concatenate([head(ks, hk), sel_id], axis=1)
        vas_ref[0, hk] = jnp.concatenate([head(vs, hk), ones_col], axis=1)
        kvw_ref[0, hk] = jnp.concatenate([head(kw, hk), head(vw, hk)], axis=1)
    gt_ref[...] = _sigmoid(col(10))


def _odd_in(h2, batch, seq, gain, w_in, moba_q_norm, moba_k_norm, nsa_q_norm, nsa_ksel_norm, nsa_kwin_norm, tm=512):
    t = h2.shape[0]
    assert MOBA_LANES == HEAD_DIM and SEL_LANES == HEAD_DIM and seq % tm == 0 and tm % MOBA_BLOCK == 0
    tps = seq // tm
    w = jnp.pad(w_in, ((0, 0), (0, ODD_IN_PAD - w_in.shape[1]))).astype(BF16)
    hsum = jnp.asarray(np.kron(np.eye(MOBA_W // HEAD_DIM), np.ones((HEAD_DIM, HEAD_DIM))), BF16)
    tile = lambda g, width: jnp.tile(g.astype(F32), width // HEAD_DIM).reshape(1, width)
    row = lambda width: pl.BlockSpec((tm, width), lambda i: (i, 0))
    heads = lambda n, width: pl.BlockSpec((1, n, tm, width), lambda i: (i // tps, 0, i % tps, 0))
    nmb = tm // MOBA_BLOCK
    nqt = tm // NSA_TQ
    rows = NSA_GROUP * NSA_TQ
    sds = jax.ShapeDtypeStruct
    out_specs = [heads(MOBA_HEADS, HEAD_DIM), heads(MOBA_HEADS, LANES), pl.BlockSpec((1, nmb, MOBA_W), lambda i: (i, 0, 0)),
                 heads(MOBA_HEADS, LANES),
                 pl.BlockSpec((1, NSA_KV_HEADS, nqt, rows, HEAD_DIM), lambda i: (i // tps, 0, i % tps, 0, 0)),
                 row(KV_W), row(KV_W), heads(NSA_KV_HEADS, LANES), heads(NSA_KV_HEADS, LANES), heads(NSA_KV_HEADS, LANES),
                 row(128)]
    out_shape = [sds((batch, MOBA_HEADS, seq, HEAD_DIM), F32), sds((batch, MOBA_HEADS, seq, LANES), BF16),
                 sds((t // tm, nmb, MOBA_W), F32), sds((batch, MOBA_HEADS, seq, LANES), BF16),
                 sds((batch, NSA_KV_HEADS, seq // NSA_TQ, rows, HEAD_DIM), BF16),
                 sds((t, KV_W), BF16), sds((t, KV_W), BF16), sds((batch, NSA_KV_HEADS, seq, LANES), BF16),
                 sds((batch, NSA_KV_HEADS, seq, LANES), BF16), sds((batch, NSA_KV_HEADS, seq, LANES), BF16),
                 sds((t, 128), F32)]
    return pl.pallas_call(
        functools.partial(_odd_in_kernel, tps),
        grid=(t // tm,),
        in_specs=[row(D_MODEL), _full((1, D_MODEL)), _full((D_MODEL, ODD_IN_PAD)), _full((MOBA_W, MOBA_W)),
                  _full((1, MOBA_W)), _full((1, MOBA_W)), _full((1, NSA_W)), _full((1, KV_W)), _full((1, KV_W))],
        out_specs=out_specs,
        out_shape=out_shape,
        compiler_params=_params("arbitrary"),
        name="odd_in",
    )(h2, gain, w, hsum, tile(moba_q_norm, MOBA_W), tile(moba_k_norm, MOBA_W), tile(nsa_q_norm, NSA_W),
      tile(nsa_ksel_norm, KV_W), tile(nsa_kwin_norm, KV_W))


def _compress_kernel(c_ref, w1_ref, w2_ref, pe_ref, g_ref, o_ref):
    kind = pl.program_id(0)
    c = c_ref[0, 0, 0]
    half = c.shape[1]
    n16 = c.shape[0]
    first = _dot(c, w1_ref[0, 0:half, :])
    second = _dot(c, w1_ref[0, half:2 * half, :])
    peb = _dot(pe_ref[0], w1_ref[0])[0:1, :]
    hid = _gelu(first + pltpu.roll(second, n16 - 1, 0) + peb)
    out = _dot(hid.astype(BF16), w2_ref[0])
    o_ref[0, 0, 0] = jnp.where(kind == 0, _rms(out, g_ref[...]), out).astype(BF16)


def _compress(kc, vc, batch, seq, pe_k, w1_k, w2_k, pe_v, w1_v, w2_v, kcmp_norm):
    n16 = seq // CMP_STRIDE
    half = CMP_STRIDE * HEAD_DIM

    def flat(x):
        return x.reshape(batch, n16, CMP_STRIDE, NSA_KV_HEADS, HEAD_DIM).transpose(0, 3, 1, 2, 4).reshape(
            batch, NSA_KV_HEADS, n16, half)

    c = jnp.stack([flat(kc), flat(vc)])
    w1 = jnp.stack([w1_k, w1_v]).astype(BF16)
    w2 = jnp.stack([w2_k, w2_v]).astype(BF16)
    pe = jnp.stack([pe_k, pe_v]).reshape(2, 1, 2 * half)
    pe = jnp.broadcast_to(pe, (2, 8, 2 * half)).astype(BF16)
    return pl.pallas_call(
        _compress_kernel,
        grid=(2, batch, NSA_KV_HEADS),
        in_specs=[pl.BlockSpec((1, 1, 1, n16, half), lambda k, b, h: (k, b, h, 0, 0)),
                  pl.BlockSpec((1, 2 * half, CMP_HIDDEN), lambda k, b, h: (k, 0, 0)),
                  pl.BlockSpec((1, CMP_HIDDEN, HEAD_DIM), lambda k, b, h: (k, 0, 0)),
                  pl.BlockSpec((1, 8, 2 * half), lambda k, b, h: (k, 0, 0)),
                  _full((1, HEAD_DIM))],
        out_specs=pl.BlockSpec((1, 1, 1, n16, HEAD_DIM), lambda k, b, h: (k, b, h, 0, 0)),
        out_shape=jax.ShapeDtypeStruct((2, batch, NSA_KV_HEADS, n16, HEAD_DIM), BF16),
        compiler_params=_params("arbitrary", "arbitrary", "arbitrary"),
        name="nsa_compress",
    )(c, w1, w2, pe, kcmp_norm.astype(F32).reshape(1, HEAD_DIM))


M_INIT = -1e30


LANES = 128


def _softmax_init(m_ref, acc_ref):
    m_ref[...] = jnp.full(m_ref.shape, M_INIT, F32)
    acc_ref[...] = jnp.zeros(acc_ref.shape, F32)


def _softmax_step(s, v_aug, m_ref, acc_ref):
    m_old = m_ref[...]
    m_new = jnp.maximum(m_old, jnp.max(s, axis=-1, keepdims=True))
    alpha = jnp.exp(m_old - m_new)
    p = jnp.exp(s - jnp.tile(m_new, (1, s.shape[1] // LANES)))
    acc_ref[...] = alpha * acc_ref[...] + _dot(p.astype(BF16), v_aug)
    m_ref[...] = m_new


def _past_keys_loop(n_keys, tile, step, alongside=None):
    n_full = n_keys // tile

    def body(j, carry):
        step(pl.multiple_of(j * tile, tile), tile)
        return carry

    extra = None
    first = 0
    if alongside is not None:
        with_tile, alone = alongside

        def both():
            out = with_tile()
            step(0, tile)
            return out

        extra = lax.cond(n_full > 0, both, alone)
        first = 1
    lax.fori_loop(first, n_full, body, 0)
    rest = n_keys - n_full * tile
    start = pl.multiple_of(n_full * tile, tile)

    @pl.when(rest > tile // 2)
    def _():
        step(start, tile)

    @pl.when((rest > 0) & (rest <= tile // 2))
    def _():
        step(start, tile // 2)

    return extra


def _softmax_result(acc_ref):
    acc = acc_ref[...]
    return acc[:, 0:HEAD_DIM] * (1.0 / acc[:, HEAD_DIM:HEAD_DIM + 1])


def _pick_top(score, pos, width, k):
    sel = jnp.zeros(score.shape, jnp.bool_)
    for _ in range(k):
        m, idx = _first_max(score, pos, width, axis=0)
        hit = (pos == idx) & (m > -jnp.inf)
        sel = sel | hit
        score = jnp.where(pos == idx, -jnp.inf, score)
    return sel


MOBA_LANES = 64
MOBA_TK = 2048


MOBA_TQ = 4 * MOBA_BLOCK


def _moba_kernel(q_ref, ka_ref, va_ref, kmean_ref, o_ref, qa_ref, m_ref, acc_ref):
    i0 = pl.program_id(2) * (MOBA_TQ // MOBA_BLOCK)
    q = q_ref[0, 0]
    q_hi, q_lo = _split(q)
    km_hi, km_lo = _split(kmean_ref[0, 0])
    gate = _dot_nt(km_hi, q_hi) + (_dot_nt(km_lo, q_hi) + _dot_nt(km_hi, q_lo))
    blk = lax.broadcasted_iota(jnp.int32, gate.shape, 0)
    cur = i0 + lax.broadcasted_iota(jnp.int32, gate.shape, 1) // MOBA_BLOCK
    sel = _pick_top(jnp.where(blk < cur, gate, -jnp.inf), blk, gate.shape[0], MOBA_TOPK)
    qs = (q * (HEAD_DIM ** -0.5)).astype(BF16)
    past = jnp.where(sel & (blk < i0), 0.0, NEG).T[:, 0:MOBA_LANES]
    own = jnp.where((blk == cur) | (sel & (blk >= i0)), 0.0, NEG).T[:, 0:MOBA_LANES]
    qa_ref[0] = jnp.concatenate([qs, past.astype(BF16)], axis=1)
    qa_ref[1] = jnp.concatenate([qs, own.astype(BF16)], axis=1)
    _softmax_init(m_ref, acc_ref)

    def step(start, size):
        s = _dot_nt(qa_ref[0], ka_ref[0, 0, pl.ds(start, size), :])
        _softmax_step(s, va_ref[0, 0, pl.ds(start, size), :], m_ref, acc_ref)

    _past_keys_loop(i0 * MOBA_BLOCK, MOBA_TK, step)
    start = pl.multiple_of(i0 * MOBA_BLOCK, MOBA_TQ)
    s = _dot_nt(qa_ref[1], ka_ref[0, 0, pl.ds(start, MOBA_TQ), :])
    qpos = lax.broadcasted_iota(jnp.int32, s.shape, 0)
    kpos = lax.broadcasted_iota(jnp.int32, s.shape, 1)
    hidden = (qpos // MOBA_BLOCK == kpos // MOBA_BLOCK) & (kpos > qpos)
    _softmax_step(jnp.where(hidden, NEG, s), va_ref[0, 0, pl.ds(start, MOBA_TQ), :], m_ref, acc_ref)
    o_ref[0, 0] = _softmax_result(acc_ref).astype(BF16)


def _moba(qm, ka, kmean, va, batch, seq):
    nmb = seq // MOBA_BLOCK
    assert nmb <= MOBA_LANES and seq % MOBA_TK == 0 and seq % MOBA_TQ == 0
    kmean = kmean.reshape(batch, nmb, MOBA_HEADS, HEAD_DIM).transpose(0, 2, 1, 3)
    kmean = jnp.pad(kmean, ((0, 0), (0, 0), (0, LANES - nmb), (0, 0)))
    return pl.pallas_call(
        _moba_kernel,
        grid=(batch, MOBA_HEADS, seq // MOBA_TQ),
        in_specs=[pl.BlockSpec((1, 1, MOBA_TQ, HEAD_DIM), lambda b, h, i: (b, h, i, 0)),
                  pl.BlockSpec((1, 1, seq, LANES), lambda b, h, i: (b, h, 0, 0)),
                  pl.BlockSpec((1, 1, seq, LANES), lambda b, h, i: (b, h, 0, 0)),
                  pl.BlockSpec((1, 1, LANES, HEAD_DIM), lambda b, h, i: (b, h, 0, 0))],
        out_specs=pl.BlockSpec((1, 1, MOBA_TQ, HEAD_DIM), lambda b, h, i: (b, h, i, 0)),
        out_shape=jax.ShapeDtypeStruct((batch, MOBA_HEADS, seq, HEAD_DIM), BF16),
        scratch_shapes=[pltpu.VMEM((2, MOBA_TQ, LANES), BF16), pltpu.VMEM((MOBA_TQ, LANES), F32),
                        pltpu.VMEM((MOBA_TQ, LANES), F32)],
        compiler_params=_params("arbitrary", "arbitrary", "arbitrary"),
        name="moba",
    )(qm, ka, va, kmean)


NSA_TQ = 2 * SEL_BLOCK
NSA_TK = 2048
SEL_LANES = 64
SUPER_KEYS = SEL_LANES * SEL_BLOCK
CMP_WIDTH_STEP = 256


def _nsa_kernel(n_super, q_ref, kc_ref, vc_ref, ka_ref, va_ref, kvw_ref, gt_ref, e_ref, band_ref, o_ref,
                qa_ref, m_ref, acc_ref):
    qi = pl.program_id(2)
    tq = NSA_TQ
    s0 = qi * tq
    q = q_ref[0, 0, 0]

    def compressed(width):
        sc = _dot_nt(q, kc_ref[0, 0, 0:width, :])
        rq = lax.broadcasted_iota(jnp.int32, sc.shape, 0) & (tq - 1)
        n = lax.broadcasted_iota(jnp.int32, sc.shape, 1)
        vis = n * CMP_STRIDE + (CMP_BLOCK - 1) <= s0 + rq
        sc = jnp.where(vis, sc, NEG)
        pc = jnp.where(vis, jnp.exp(sc - jnp.max(sc, axis=-1, keepdims=True)), 0.0)
        pc = pc * (1.0 / jnp.maximum(jnp.sum(pc, axis=-1, keepdims=True), 1e-30))
        imp = pc[0:tq]
        for g in range(1, NSA_GROUP):
            imp = imp + pc[g * tq:(g + 1) * tq]
        return _dot(pc.astype(BF16), vc_ref[0, 0, 0:width, :]), _dot_x2(imp, band_ref[0:width, :])

    n16 = kc_ref.shape[2]
    widths = list(range(CMP_WIDTH_STEP, n16, CMP_WIDTH_STEP)) + [n16]
    o_c, pslc = lax.switch((s0 + tq - 1) // (CMP_WIDTH_STEP * CMP_STRIDE),
                           [functools.partial(compressed, wd) for wd in widths])

    wlen = WINDOW + tq

    def window(kvw, masked):
        sw = masked(_dot_nt(q, kvw[:, 0:HEAD_DIM]))
        pw = jnp.exp(sw - jnp.max(sw, axis=-1, keepdims=True))
        return _dot(pw.astype(BF16), kvw)[:, HEAD_DIM:2 * HEAD_DIM] * (1.0 / jnp.sum(pw, axis=-1, keepdims=True))

    def window_interior():
        def masked(sw):
            rq = lax.broadcasted_iota(jnp.int32, (sw.shape[0], tq), 0) & (tq - 1)
            c = lax.broadcasted_iota(jnp.int32, (sw.shape[0], tq), 1)
            return jnp.concatenate([jnp.where(c > rq, sw[:, 0:tq], NEG), sw[:, tq:WINDOW],
                                    jnp.where(c <= rq, sw[:, WINDOW:wlen], NEG)], axis=1)

        return window(kvw_ref[0, 0, pl.ds(pl.multiple_of(s0 - WINDOW, tq), wlen), :], masked)

    def window_start():
        def masked(sw):
            kabs = lax.broadcasted_iota(jnp.int32, sw.shape, 1)
            t = s0 + (lax.broadcasted_iota(jnp.int32, sw.shape, 0) & (tq - 1))
            return jnp.where((kabs <= t) & (kabs > t - WINDOW), sw, NEG)

        return window(kvw_ref[0, 0, 0:wlen, :], masked)

    pslc_t = pslc.T
    nb = pslc_t.shape[0]
    blk = lax.broadcasted_iota(jnp.int32, pslc_t.shape, 0)
    cur = (s0 + lax.broadcasted_iota(jnp.int32, pslc_t.shape, 1)) // SEL_BLOCK
    c0 = s0 // SEL_BLOCK
    elig = (blk >= 1) & (blk <= cur - 2)
    sel = _pick_top(jnp.where(elig, pslc_t, -jnp.inf), blk, nb, SEL_TOPK - 3)
    sel = sel | (blk == 0) | (blk == cur - 1)
    past = jnp.where(sel & (blk < c0), 0.0, NEG).T.astype(BF16)
    for st in range(n_super):
        b = past[:, st * SEL_LANES:(st + 1) * SEL_LANES]
        qa_ref[st] = jnp.concatenate([q, jnp.concatenate([b] * NSA_GROUP, axis=0)], axis=1)

    _softmax_init(m_ref, acc_ref)
    d0 = pl.multiple_of(s0, tq)
    s = _dot_nt(q, ka_ref[0, 0, pl.ds(d0, tq), :][:, 0:HEAD_DIM])
    qpos = lax.broadcasted_iota(jnp.int32, s.shape, 0) & (tq - 1)
    kpos = lax.broadcasted_iota(jnp.int32, s.shape, 1)
    _softmax_step(jnp.where(kpos <= qpos, s, NEG), va_ref[0, 0, pl.ds(d0, tq), :], m_ref, acc_ref)

    def step(start, size):
        s = _dot_nt(qa_ref[start // SUPER_KEYS], ka_ref[0, 0, pl.ds(start, size), :])
        _softmax_step(s, va_ref[0, 0, pl.ds(start, size), :], m_ref, acc_ref)

    o_w = _past_keys_loop(s0, NSA_TK, step, alongside=(
        window_interior, lambda: lax.cond(s0 >= WINDOW, window_interior, window_start)))
    o_s = _softmax_result(acc_ref)

    w = NSA_GROUP * HEAD_DIM
    gexp = _dot_x2(gt_ref[...], e_ref[0])
    wide = lambda x: jnp.concatenate([x[g * tq:(g + 1) * tq] for g in range(NSA_GROUP)], axis=1)
    o_ref[...] = (gexp[:, 0:w] * wide(o_c) + gexp[:, w:2 * w] * wide(o_s) + gexp[:, 2 * w:3 * w] * wide(o_w)).astype(BF16)


def _nsa(qs, kcmp, vcmp, ka, va, kvw, gates, batch, seq):
    tq = NSA_TQ
    nq = seq // tq
    nb = seq // SEL_BLOCK
    n16 = seq // CMP_STRIDE
    assert seq % SUPER_KEYS == 0 and tq == 2 * SEL_BLOCK and WINDOW % tq == 0 and NSA_TK >= WINDOW
    n_super = seq // SUPER_KEYS
    rows = NSA_GROUP * tq
    e = np.zeros((NSA_KV_HEADS, 128, 3 * NSA_GROUP * HEAD_DIM), np.float32)
    for br in range(3):
        for hk in range(NSA_KV_HEADS):
            for g in range(NSA_GROUP):
                c = (br * NSA_GROUP + g) * HEAD_DIM
                e[hk, br * NSA_HEADS + hk * NSA_GROUP + g, c:c + HEAD_DIM] = 1.0
    nn, jj = np.arange(n16)[:, None], np.arange(nb)[None, :]
    band = ((nn >= 4 * jj - 1) & (nn <= 4 * jj + 3)).astype(np.float32)
    resident = lambda width: pl.BlockSpec((1, 1, seq, width), lambda b, h, i: (b, h, 0, 0))
    w = NSA_GROUP * HEAD_DIM
    return pl.pallas_call(
        functools.partial(_nsa_kernel, n_super),
        grid=(batch, NSA_KV_HEADS, nq),
        in_specs=[pl.BlockSpec((1, 1, 1, rows, HEAD_DIM), lambda b, h, i: (b, h, i, 0, 0)),
                  pl.BlockSpec((1, 1, n16, HEAD_DIM), lambda b, h, i: (b, h, 0, 0)),
                  pl.BlockSpec((1, 1, n16, HEAD_DIM), lambda b, h, i: (b, h, 0, 0)),
                  resident(LANES), resident(LANES), resident(LANES),
                  pl.BlockSpec((tq, 128), lambda b, h, i: (b * nq + i, 0)),
                  pl.BlockSpec((1, 128, 3 * w), lambda b, h, i: (h, 0, 0)),
                  pl.BlockSpec((n16, nb), lambda b, h, i: (0, 0))],
        out_specs=pl.BlockSpec((tq, w), lambda b, h, i: (b * nq + i, h)),
        out_shape=jax.ShapeDtypeStruct((batch * seq, NSA_W), BF16),
        scratch_shapes=[pltpu.VMEM((n_super, rows, LANES), BF16), pltpu.VMEM((rows, LANES), F32),
                        pltpu.VMEM((rows, LANES), F32)],
        compiler_params=_params("arbitrary", "arbitrary", "arbitrary"),
        name="nsa",
    )(qs, kcmp, vcmp, ka, va, kvw, gates, jnp.asarray(e, BF16), jnp.asarray(band, BF16))


def _odd_out_kernel(om_ref, on_ref, h_ref, w_ref, o_ref):
    acc = h_ref[...] + _dot(on_ref[...], w_ref[MOBA_W:D_MODEL, :])
    for h in range(MOBA_HEADS):
        acc = acc + _dot(om_ref[0, h], w_ref[h * HEAD_DIM:(h + 1) * HEAD_DIM, :])
    o_ref[...] = acc


def _odd_out(o_moba, o_nsa, h2, w_out, seq, tm=512):
    t = h2.shape[0]
    tps = seq // tm
    row = lambda w: pl.BlockSpec((tm, w), lambda i: (i, 0))
    return pl.pallas_call(
        _odd_out_kernel,
        grid=(t // tm,),
        in_specs=[pl.BlockSpec((1, MOBA_HEADS, tm, HEAD_DIM), lambda i: (i // tps, 0, i % tps, 0)),
                  row(NSA_W), row(D_MODEL), _full((D_MODEL, D_MODEL))],
        out_specs=row(D_MODEL),
        out_shape=jax.ShapeDtypeStruct((t, D_MODEL), F32),
        compiler_params=_params("arbitrary"),
        name="odd_out",
    )(o_moba, o_nsa, h2, w_out)


def _even_layer(h2, batch, seq, norm, w_in, w_out, lam_re, lam_im, log_dt, b_re, b_im, c_re, c_im, d, w_glu, conv_w, conv_b):
    u, ub, yb = _even_in(h2, norm.reshape(1, D_MODEL), w_in.astype(BF16), conv_w, conv_b.reshape(1, CONV_WIDTH), seq)
    ypre = _s5_mixer_pre(ub, batch, seq, lam_re, lam_im, log_dt, b_re, b_im, c_re, c_im)
    return _even_out(ypre, u, yb, h2, d.reshape(1, S5_WIDTH), w_glu.astype(BF16), w_out.astype(BF16))


def _odd_layer(h2, batch, seq, norm, w_in, w_out, moba_q_norm, moba_k_norm, nsa_q_norm, nsa_kcmp_norm, nsa_ksel_norm,
               nsa_kwin_norm, cmp_pe_k, cmp_w1_k, cmp_w2_k, cmp_pe_v, cmp_w1_v, cmp_w2_v):
    qm, kam, kmean, vam, qs, kc, vc, kas, vas, kvw, gates = _odd_in(
        h2, batch, seq, norm.reshape(1, D_MODEL), w_in, moba_q_norm, moba_k_norm, nsa_q_norm, nsa_ksel_norm, nsa_kwin_norm)
    cmp = _compress(kc, vc, batch, seq, cmp_pe_k, cmp_w1_k, cmp_w2_k, cmp_pe_v, cmp_w1_v, cmp_w2_v, nsa_kcmp_norm)
    o_moba = _moba(qm, kam, kmean, vam, batch, seq)
    o_nsa = _nsa(qs, cmp[0], cmp[1], kas, vas, kvw, gates, batch, seq)
    return _odd_out(o_moba, o_nsa, h2, w_out.astype(BF16), seq)


def kernel(x, ev_norm_mix, ev_w_in, ev_w_out, s5_lam_re, s5_lam_im, s5_log_dt, s5_b_re, s5_b_im, s5_c_re, s5_c_im, s5_d, s5_w_glu, conv_w, conv_b, od_norm_mix, od_w_in, od_w_out, moba_q_norm, moba_k_norm, nsa_q_norm, nsa_kcmp_norm, nsa_ksel_norm, nsa_kwin_norm, cmp_pe_k, cmp_w1_k, cmp_w2_k, cmp_pe_v, cmp_w1_v, cmp_w2_v, moe_norm, moe_w_group, moe_b_group, moe_w_expert, moe_b_expert, moe_w_gate, moe_w_up, moe_w_down):
    batch, seq, _ = x.shape
    depth = moe_norm.shape[0]
    h = x.reshape(batch * seq, D_MODEL)
    for layer in range(depth):
        i = layer // 2
        if layer % 2 == 0:
            h = _even_layer(h, batch, seq, ev_norm_mix[i], ev_w_in[i], ev_w_out[i], s5_lam_re[i], s5_lam_im[i], s5_log_dt[i],
                            s5_b_re[i], s5_b_im[i], s5_c_re[i], s5_c_im[i], s5_d[i], s5_w_glu[i], conv_w[i], conv_b[i])
        else:
            h = _odd_layer(h, batch, seq, od_norm_mix[i], od_w_in[i], od_w_out[i], moba_q_norm[i], moba_k_norm[i],
                           nsa_q_norm[i], nsa_kcmp_norm[i], nsa_ksel_norm[i], nsa_kwin_norm[i], cmp_pe_k[i], cmp_w1_k[i],
                           cmp_w2_k[i], cmp_pe_v[i], cmp_w1_v[i], cmp_w2_v[i])
        h = _moe(h, moe_norm[layer].reshape(1, D_MODEL), moe_w_group[layer], moe_b_group[layer], moe_w_expert[layer],
                 moe_b_expert[layer], moe_w_gate[layer], moe_w_up[layer], moe_w_down[layer])
    return h.reshape(batch, seq, D_MODEL)
```

```python
import functools
import math

import jax
import jax.numpy as jnp
import numpy as np
from jax import lax
from jax.experimental import pallas as pl
from jax.experimental.pallas import tpu as pltpu

D_MODEL = 1024
HEAD_DIM = 64
EPS = 1e-6
S5_WIDTH = 256
S5_GROUP = 16
S5_GROUPS = 16
S5_STATE = 64
S5_CHUNK = 16
CONV_WIDTH = 768
CONV_K = 3
MOBA_HEADS = 4
NSA_HEADS = 12
NSA_KV_HEADS = 2
NSA_GROUP = 6
MOBA_W = 256
NSA_W = 768
KV_W = 128
MOBA_BLOCK = 256
MOBA_TOPK = 3
CMP_BLOCK = 32
CMP_STRIDE = 16
CMP_HIDDEN = 256
SEL_BLOCK = 64
SEL_TOPK = 8
WINDOW = 512
N_GROUPS = 4
EXPERTS_PER_GROUP = 4
N_EXPERTS = 16
EXPERT_FF = 256

VMEM_LIMIT_BYTES = 56 * 1024 * 1024
NEG = -float(2 ** 30)
F32 = jnp.float32
BF16 = jnp.bfloat16


def _params(*semantics):
    return pltpu.CompilerParams(dimension_semantics=semantics, vmem_limit_bytes=VMEM_LIMIT_BYTES)


def _dot(a, b):
    return jnp.dot(a, b, preferred_element_type=F32)


def _dot_nt(a, b):
    return lax.dot_general(a, b, (((1,), (1,)), ((), ())), preferred_element_type=F32)


def _split(x):
    hi = x.astype(BF16)
    lo = (x - hi.astype(F32)).astype(BF16)
    return hi, lo


def _dot_x2(x, w):
    hi, lo = _split(x)
    return _dot(hi, w) + _dot(lo, w)


def _dot_x3(x, w_hi, w_lo):
    hi, lo = _split(x)
    return _dot(hi, w_hi) + (_dot(hi, w_lo) + _dot(lo, w_hi))


def _rms(x, gain):
    return x * lax.rsqrt(jnp.mean(x * x, axis=-1, keepdims=True) + EPS) * gain


def _gelu(x):
    return 0.5 * x * (1.0 + jnp.tanh(math.sqrt(2.0 / math.pi) * (x + 0.044715 * (x * x * x))))


def _sigmoid(x):
    return 1.0 / (1.0 + jnp.exp(-x))


def _full(shape):
    n = len(shape)
    return pl.BlockSpec(shape, lambda *_: (0,) * n)


def _even_in_kernel(tiles_per_seq, x_ref, g_ref, w_ref, cw_ref, cb_ref, u_ref, ub_ref, yb_ref, carry_ref):
    i = pl.program_id(0)
    xn = _rms(x_ref[...], g_ref[...]).astype(BF16)
    u = _dot(xn, w_ref[:, 0:S5_WIDTH])
    u_ref[...] = u
    ub_ref[...] = u.astype(BF16)
    o = S5_WIDTH
    xc = _dot(xn, w_ref[:, o:o + CONV_WIDTH])
    gb = _dot(xn, w_ref[:, o + CONV_WIDTH:o + 2 * CONV_WIDTH])
    gc = _dot(xn, w_ref[:, o + 2 * CONV_WIDTH:o + 3 * CONV_WIDTH])
    z = gc * xc
    tm = z.shape[0]

    @pl.when(i % tiles_per_seq == 0)
    def _():
        carry_ref[...] = jnp.zeros_like(carry_ref)

    row = lax.broadcasted_iota(jnp.int32, z.shape, 0)
    prev1 = carry_ref[7:8, :]
    prev2 = carry_ref[6:7, :]
    z1 = jnp.where(row == 0, prev1, pltpu.roll(z, 1, 0))
    z2 = jnp.where(row == 0, prev2, jnp.where(row == 1, prev1, pltpu.roll(z, 2, 0)))
    y = cw_ref[0:1, :] * z2 + cw_ref[1:2, :] * z1 + cw_ref[2:3, :] * z + cb_ref[...]
    yb_ref[...] = (gb * y).astype(BF16)
    carry_ref[...] = z[tm - 8:tm, :]


def _even_in(x2, gain, w_in, conv_w, conv_b, seq, tm=512):
    t = x2.shape[0]
    n_in = w_in.shape[1]
    return pl.pallas_call(
        functools.partial(_even_in_kernel, seq // tm),
        grid=(t // tm,),
        in_specs=[pl.BlockSpec((tm, D_MODEL), lambda i: (i, 0)), _full((1, D_MODEL)),
                  _full((D_MODEL, n_in)), _full((CONV_K, CONV_WIDTH)), _full((1, CONV_WIDTH))],
        out_specs=[pl.BlockSpec((tm, S5_WIDTH), lambda i: (i, 0)), pl.BlockSpec((tm, S5_WIDTH), lambda i: (i, 0)),
                   pl.BlockSpec((tm, CONV_WIDTH), lambda i: (i, 0))],
        out_shape=[jax.ShapeDtypeStruct((t, S5_WIDTH), F32), jax.ShapeDtypeStruct((t, S5_WIDTH), BF16),
                   jax.ShapeDtypeStruct((t, CONV_WIDTH), BF16)],
        scratch_shapes=[pltpu.VMEM((8, CONV_WIDTH), F32)],
        compiler_params=_params("arbitrary"),
        name="even_in",
    )(x2, gain, w_in, conv_w, conv_b)


def _s5_weights(lam_re, lam_im, log_dt, b_re, b_im, c_re, c_im):
    g, p, hg, ck = S5_GROUPS, S5_STATE, S5_GROUP, S5_CHUNK
    lr, li = lam_re.astype(F32), lam_im.astype(F32)
    dt = jnp.exp(log_dt.astype(F32))[:, None]
    mag = jnp.exp(lr * dt)
    a_re, a_im = mag * jnp.cos(li * dt), mag * jnp.sin(li * dt)
    den = lr * lr + li * li
    f_re = ((a_re - 1.0) * lr + a_im * li) / den
    f_im = (a_im * lr - (a_re - 1.0) * li) / den
    br, bi = b_re.astype(F32), b_im.astype(F32)
    bb_re = f_re[..., None] * br - f_im[..., None] * bi
    bb_im = f_re[..., None] * bi + f_im[..., None] * br
    pw_re, pw_im = [jnp.ones_like(a_re)], [jnp.zeros_like(a_im)]
    for _ in range(ck):
        r, m = pw_re[-1], pw_im[-1]
        pw_re.append(r * a_re - m * a_im)
        pw_im.append(r * a_im + m * a_re)
    pw_re, pw_im = jnp.stack(pw_re), jnp.stack(pw_im)
    cr, ci = c_re.astype(F32), c_im.astype(F32)
    rev_re, rev_im = pw_re[ck - 1::-1][:ck], pw_im[ck - 1::-1][:ck]
    ws_re = rev_re[:, :, :, None] * bb_re[None] - rev_im[:, :, :, None] * bb_im[None]
    ws_im = rev_re[:, :, :, None] * bb_im[None] + rev_im[:, :, :, None] * bb_re[None]
    ca_re = cr[None] * pw_re[1:, :, None, :] - ci[None] * pw_im[1:, :, None, :]
    ca_im = cr[None] * pw_im[1:, :, None, :] + ci[None] * pw_re[1:, :, None, :]
    cb_re = jnp.einsum('ghp,kgp,gpj->kghj', cr, pw_re[:ck], bb_re) - jnp.einsum('ghp,kgp,gpj->kghj', cr, pw_im[:ck], bb_im) \
        - jnp.einsum('ghp,kgp,gpj->kghj', ci, pw_re[:ck], bb_im) - jnp.einsum('ghp,kgp,gpj->kghj', ci, pw_im[:ck], bb_re)
    lag = np.arange(ck)[None, :] - np.arange(ck)[:, None]
    tz = cb_re[np.clip(lag, 0, ck - 1)]
    tz = jnp.where((lag >= 0)[:, :, None, None, None], tz, 0.0)
    cw = ck * g * hg
    ws = jnp.stack([ws_re, ws_im]).transpose(1, 2, 4, 0, 3).reshape(cw, 2 * p)
    wc = jnp.stack([ca_re, -ca_im]).transpose(0, 2, 4, 1, 3).reshape(2 * g * p, ck * hg)
    tzc = tz.transpose(0, 2, 4, 1, 3).reshape(cw, ck * hg)
    return ws.astype(BF16), wc.astype(BF16), tzc.astype(BF16), pw_re[ck].reshape(1, g * p), pw_im[ck].reshape(1, g * p)


def _group_expand(compact, expand, row_shift, col_shift, col0):
    full = _dot(compact, expand)
    row = lax.broadcasted_iota(jnp.int32, full.shape, 0)
    col = col0 + lax.broadcasted_iota(jnp.int32, full.shape, 1)
    same = ((row >> row_shift) & (S5_GROUPS - 1)) == ((col >> col_shift) & (S5_GROUPS - 1))
    return jnp.where(same, full, 0.0).astype(BF16)


def _s5_state_kernel(u_ref, ws_ref, e_ref, s_ref, w_scr):
    @pl.when(pl.program_id(1) == 0)
    def _():
        w_scr[...] = _group_expand(ws_ref[...], e_ref[...], 4, 6, pl.program_id(0) * w_scr.shape[1])

    s_ref[...] = _dot(u_ref[...], w_scr[...])


def _s5_scan_kernel(s_ref, are_ref, aim_ref, xprev_ref, st_ref):
    @pl.when(pl.program_id(0) == 0)
    def _():
        st_ref[...] = jnp.zeros_like(st_ref)

    a_re, a_im = are_ref[...], aim_ref[...]
    nb, n = s_ref.shape[0], s_ref.shape[1]
    half = a_re.shape[1]

    def body(c, carry):
        out = []
        for b in range(nb):
            xr, xi = carry[2 * b], carry[2 * b + 1]
            xprev_ref[b, pl.ds(c, 1), 0:half] = xr
            xprev_ref[b, pl.ds(c, 1), half:2 * half] = xi
            s = s_ref[b, pl.ds(c, 1), :]
            out += [a_re * xr - a_im * xi + s[:, 0:half], a_re * xi + a_im * xr + s[:, half:2 * half]]
        return tuple(out)

    init = tuple(st_ref[b:b + 1, o:o + half] for b in range(nb) for o in (0, half))
    final = lax.fori_loop(0, n, body, init, unroll=4)
    for b in range(nb):
        st_ref[b:b + 1, 0:half] = final[2 * b]
        st_ref[b:b + 1, half:2 * half] = final[2 * b + 1]


def _s5_out_kernel(u_ref, xp_ref, tz_ref, wc_ref, e_ref, y_ref, tz_scr, wc_scr):
    @pl.when(pl.program_id(1) == 0)
    def _():
        col0 = pl.program_id(0) * tz_scr.shape[1]
        tz_scr[...] = _group_expand(tz_ref[...], e_ref[...], 4, 4, col0)
        wc_scr[...] = _group_expand(wc_ref[...], e_ref[...], 6, 4, col0)

    y_ref[...] = _dot(u_ref[...], tz_scr[...]) + _dot(xp_ref[...].astype(BF16), wc_scr[...])


def _expand_matrix(outer, inner):
    e = np.zeros((outer, inner, outer, S5_GROUPS, inner), np.float32)
    for x in range(outer):
        for y in range(inner):
            e[x, y, x, :, y] = 1.0
    return jnp.asarray(e.reshape(outer * inner, outer * S5_GROUPS * inner), BF16)


def _s5_mixer_pre(ub, batch, seq, lam_re, lam_im, log_dt, b_re, b_im, c_re, c_im):
    ws, wc, tz, a16_re, a16_im = _s5_weights(lam_re, lam_im, log_dt, b_re, b_im, c_re, c_im)
    nc = seq // S5_CHUNK
    rows = batch * nc
    cw = S5_CHUNK * S5_WIDTH
    sw = 2 * S5_GROUPS * S5_STATE
    tr = min(rows, 512)
    tn = 512
    uc = ub.reshape(rows, cw)
    e_state = _expand_matrix(2, S5_STATE)
    e_out = _expand_matrix(S5_CHUNK, S5_GROUP)
    s = pl.pallas_call(
        _s5_state_kernel,
        grid=(sw // tn, rows // tr),
        in_specs=[pl.BlockSpec((tr, cw), lambda j, i: (i, 0)), _full(ws.shape),
                  pl.BlockSpec((e_state.shape[0], tn), lambda j, i: (0, j))],
        out_specs=pl.BlockSpec((tr, tn), lambda j, i: (i, j)),
        out_shape=jax.ShapeDtypeStruct((rows, sw), F32),
        scratch_shapes=[pltpu.VMEM((cw, tn), BF16)],
        compiler_params=_params("arbitrary", "arbitrary"),
        name="s5_state",
    )(uc, ws, e_state)
    tc = min(nc, 256)
    xprev = pl.pallas_call(
        _s5_scan_kernel,
        grid=(nc // tc,),
        in_specs=[pl.BlockSpec((batch, tc, sw), lambda i: (0, i, 0)), _full((1, sw // 2)), _full((1, sw // 2))],
        out_specs=pl.BlockSpec((batch, tc, sw), lambda i: (0, i, 0)),
        out_shape=jax.ShapeDtypeStruct((batch, nc, sw), F32),
        scratch_shapes=[pltpu.VMEM((batch, sw), F32)],
        compiler_params=_params("arbitrary"),
        name="s5_scan",
    )(s.reshape(batch, nc, sw), a16_re, a16_im)
    y = pl.pallas_call(
        _s5_out_kernel,
        grid=(cw // tn, rows // tr),
        in_specs=[pl.BlockSpec((tr, cw), lambda j, i: (i, 0)), pl.BlockSpec((tr, sw), lambda j, i: (i, 0)),
                  _full(tz.shape), _full(wc.shape), pl.BlockSpec((e_out.shape[0], tn), lambda j, i: (0, j))],
        out_specs=pl.BlockSpec((tr, tn), lambda j, i: (i, j)),
        out_shape=jax.ShapeDtypeStruct((rows, cw), F32),
        scratch_shapes=[pltpu.VMEM((cw, tn), BF16), pltpu.VMEM((sw, tn), BF16)],
        compiler_params=_params("arbitrary", "arbitrary"),
        name="s5_out",
    )(uc, xprev.reshape(rows, sw), tz, wc, e_out)
    return y.reshape(batch * seq, S5_WIDTH)


def _even_out_kernel(ypre_ref, u_ref, yb_ref, x_ref, d_ref, wglu_ref, wout_ref, o_ref):
    y = _gelu(ypre_ref[...] + d_ref[...] * u_ref[...])
    y = y * _sigmoid(_dot(y.astype(BF16), wglu_ref[...]))
    o_ref[...] = (x_ref[...] + _dot(y.astype(BF16), wout_ref[0:S5_WIDTH, :])
                  + _dot(yb_ref[...], wout_ref[S5_WIDTH:D_MODEL, :]))


def _even_out(ypre, u, yb, x2, d, w_glu, w_out, tm=512):
    t = x2.shape[0]
    row = lambda w: pl.BlockSpec((tm, w), lambda i: (i, 0))
    return pl.pallas_call(
        _even_out_kernel,
        grid=(t // tm,),
        in_specs=[row(S5_WIDTH), row(S5_WIDTH), row(CONV_WIDTH), row(D_MODEL), _full((1, S5_WIDTH)),
                  _full((S5_WIDTH, S5_WIDTH)), _full((D_MODEL, D_MODEL))],
        out_specs=row(D_MODEL),
        out_shape=jax.ShapeDtypeStruct((t, D_MODEL), F32),
        compiler_params=_params("arbitrary"),
        name="even_out",
    )(ypre, u, yb, x2, d, w_glu, w_out)


def _first_max(v, pos, width, axis=-1):
    m = jnp.max(v, axis=axis, keepdims=True)
    idx = jnp.min(jnp.where(v == m, pos, width), axis=axis, keepdims=True)
    return m, idx


MOE_TM = 1024
MOE_ALIGN = 16
MOE_SLOTS = 1152
MOE_WIN = 320
ROUTER_LANES = 128


def _moe_router_kernel(h_ref, g_ref, wr_hi_ref, wr_lo_ref, br_ref, gate_ref, grp_ref, cnt_ref):
    xn = _rms(h_ref[...], g_ref[...])
    logits = _dot_x3(xn, wr_hi_ref[...], wr_lo_ref[...]) + br_ref[...]
    lane = lax.broadcasted_iota(jnp.int32, logits.shape, 1)
    width = logits.shape[1]
    is_g = lane < N_GROUPS
    gl = jnp.where(is_g, logits, -jnp.inf)
    gm, gi = _first_max(gl, lane, width)
    gw = 1.0 / jnp.sum(jnp.where(is_g, jnp.exp(gl - gm), 0.0), axis=-1, keepdims=True)
    lo = N_GROUPS + gi * EXPERTS_PER_GROUP
    in_grp = (lane >= lo) & (lane < lo + EXPERTS_PER_GROUP)
    el = jnp.where(in_grp, logits, -jnp.inf)
    m1, i1 = _first_max(el, lane, width)
    m2, i2 = _first_max(jnp.where(lane == i1, -jnp.inf, el), lane, width)
    p2 = jnp.exp(m2 - m1)
    w1 = gw / (1.0 + p2)
    w2 = gw * p2 / (1.0 + p2)
    gate_ref[...] = jnp.where(lane == i1, w1, 0.0) + jnp.where(lane == i2, w2, 0.0)
    grp = (lane == gi).astype(F32)
    grp_ref[...] = grp.astype(BF16)
    cnt_ref[0] = jnp.broadcast_to(jnp.sum(grp, axis=0, keepdims=True), cnt_ref.shape[1:])


def _moe_expert_kernel(base_ref, nwin_ref, h_ref, g_ref, gate_ref, grp_ref, wg_ref, wu_ref, wd_ref, o_ref,
                       xs_ref, gs_ref, ys_ref, pt_ref):
    i, g = pl.program_id(0), pl.program_id(1)
    tm = h_ref.shape[0]

    @pl.when((i == 0) & (g == 0))
    def _():
        xs_ref[...] = jnp.zeros_like(xs_ref)
        gs_ref[...] = jnp.zeros_like(gs_ref)
        ys_ref[...] = jnp.zeros_like(ys_ref)

    @pl.when(g == 0)
    def _():
        xn = _rms(h_ref[...], g_ref[...]).astype(BF16)
        grp = grp_ref[...]
        earlier = (lax.broadcasted_iota(jnp.int32, (tm, tm), 0) > lax.broadcasted_iota(jnp.int32, (tm, tm), 1)).astype(BF16)
        rank = _dot(earlier, grp)
        lane = lax.broadcasted_iota(jnp.int32, rank.shape, 1)
        for k in range(N_GROUPS):
            rank = rank + jnp.where(lane == k, base_ref[i * N_GROUPS + k].astype(F32), 0.0)
        slot = jnp.sum(grp.astype(F32) * rank, axis=-1, keepdims=True).astype(jnp.int32)
        pt = (lax.broadcasted_iota(jnp.int32, (tm, MOE_SLOTS), 1) == slot).astype(BF16)
        pt_ref[...] = pt
        gather = lambda x: lax.dot_general(pt, x, (((0,), (0,)), ((), ())), preferred_element_type=F32)
        xs_ref[0:MOE_SLOTS, :] = gather(xn).astype(BF16)
        gate_hi, gate_lo = _split(gate_ref[...])
        gs_ref[0:MOE_SLOTS, :] = gather(gate_hi) + gather(gate_lo)

    def window(w, carry):
        r0 = pl.multiple_of(base_ref[i * N_GROUPS + g] + w * MOE_WIN, MOE_ALIGN)
        x = xs_ref[pl.ds(r0, MOE_WIN), :]
        gate = gs_ref[pl.ds(r0, MOE_WIN), :]
        lane = lax.broadcasted_iota(jnp.int32, gate.shape, 1)
        y = None
        for j in range(EXPERTS_PER_GROUP):
            ge = jnp.sum(jnp.where(lane == g * EXPERTS_PER_GROUP + (j + N_GROUPS), gate, 0.0), axis=-1, keepdims=True)
            h1 = _dot(x, wg_ref[j])
            h3 = _dot(x, wu_ref[j])
            act = (h1 * _sigmoid(h1)) * h3 * ge
            yj = _dot(act.astype(BF16), wd_ref[j])
            y = yj if y is None else y + yj
        ys_ref[pl.ds(r0, MOE_WIN), :] = y.astype(BF16)
        return carry

    lax.fori_loop(0, nwin_ref[i * N_GROUPS + g], window, 0)

    @pl.when(g == N_GROUPS - 1)
    def _():
        o_ref[...] = h_ref[...] + _dot(pt_ref[...], ys_ref[0:MOE_SLOTS, :])


def _moe(h2, gain, w_group, b_group, w_expert, b_expert, w_gate, w_up, w_down):
    t = h2.shape[0]
    tm, rw = MOE_TM, ROUTER_LANES
    assert t % tm == 0 and MOE_SLOTS >= tm + N_GROUPS * (MOE_ALIGN - 1) and MOE_WIN % MOE_ALIGN == 0
    n_tiles = t // tm
    wr = jnp.zeros((D_MODEL, rw), F32).at[:, 0:N_GROUPS].set(w_group).at[:, N_GROUPS:N_GROUPS + N_EXPERTS].set(w_expert)
    br = jnp.zeros((1, rw), F32).at[0, 0:N_GROUPS].set(b_group).at[0, N_GROUPS:N_GROUPS + N_EXPERTS].set(b_expert)
    wr_hi = wr.astype(BF16)
    wr_lo = (wr - wr_hi.astype(F32)).astype(BF16)
    row = lambda width: pl.BlockSpec((tm, width), lambda i: (i, 0))
    gates, grp, cnt = pl.pallas_call(
        _moe_router_kernel,
        grid=(n_tiles,),
        in_specs=[row(D_MODEL), _full((1, D_MODEL)), _full((D_MODEL, rw)), _full((D_MODEL, rw)), _full((1, rw))],
        out_specs=[row(rw), row(rw), pl.BlockSpec((1, 8, rw), lambda i: (i, 0, 0))],
        out_shape=[jax.ShapeDtypeStruct((t, rw), F32), jax.ShapeDtypeStruct((t, rw), BF16),
                   jax.ShapeDtypeStruct((n_tiles, 8, rw), F32)],
        compiler_params=_params("arbitrary"),
        name="moe_router",
    )(h2, gain, wr_hi, wr_lo, br)
    n = cnt[:, 0, 0:N_GROUPS].astype(jnp.int32)
    padded = (n + (MOE_ALIGN - 1)) // MOE_ALIGN * MOE_ALIGN
    base = (jnp.cumsum(padded, axis=1) - padded).reshape(-1)
    nwin = ((padded + (MOE_WIN - 1)) // MOE_WIN).reshape(-1)
    tile = lambda width: pl.BlockSpec((tm, width), lambda i, g, *_: (i, 0))
    experts = lambda shape: pl.BlockSpec((EXPERTS_PER_GROUP,) + shape, lambda i, g, *_: (g, 0, 0))
    slots = MOE_SLOTS + MOE_WIN
    return pl.pallas_call(
        _moe_expert_kernel,
        grid_spec=pltpu.PrefetchScalarGridSpec(
            num_scalar_prefetch=2,
            grid=(n_tiles, N_GROUPS),
            in_specs=[tile(D_MODEL), pl.BlockSpec((1, D_MODEL), lambda i, g, *_: (0, 0)), tile(rw), tile(rw),
                      experts((D_MODEL, EXPERT_FF)), experts((D_MODEL, EXPERT_FF)), experts((EXPERT_FF, D_MODEL))],
            out_specs=tile(D_MODEL),
            scratch_shapes=[pltpu.VMEM((slots, D_MODEL), BF16), pltpu.VMEM((slots, rw), F32),
                            pltpu.VMEM((slots, D_MODEL), BF16), pltpu.VMEM((tm, MOE_SLOTS), BF16)]),
        out_shape=jax.ShapeDtypeStruct((t, D_MODEL), F32),
        compiler_params=_params("arbitrary", "arbitrary"),
        name="moe",
    )(base, nwin, h2, gain, gates, grp, w_gate.astype(BF16), w_up.astype(BF16), w_down.astype(BF16))


ODD_SPLITS = (MOBA_W, MOBA_W, MOBA_W, NSA_W, KV_W, KV_W, KV_W, KV_W, KV_W, KV_W, 128)
ODD_IN_PAD = sum(ODD_SPLITS)


def _head_rms(x, hsum, gain):
    w = x.shape[1]
    ss = jnp.concatenate([_dot_x2(x[:, o:o + hsum.shape[0]] * x[:, o:o + hsum.shape[0]], hsum)
                          for o in range(0, w, hsum.shape[0])], axis=1) if w > hsum.shape[0] else _dot_x2(x * x, hsum)
    return x * lax.rsqrt(ss * (1.0 / HEAD_DIM) + EPS) * gain


def _odd_in_kernel(tiles_per_seq, x_ref, g_ref, w_ref, hsum_ref, gq_ref, gk_ref, gnq_ref, gks_ref, gkw_ref,
                   qm_ref, kam_ref, kmean_ref, vam_ref, qs_ref, kvc_ref, kas_ref, vas_ref, kvw_ref, gt_ref):
    xn = _rms(x_ref[...], g_ref[...]).astype(BF16)
    offs = np.cumsum((0,) + ODD_SPLITS)
    col = lambda j: _dot(xn, w_ref[:, int(offs[j]):int(offs[j + 1])])
    head = lambda x, h: x[:, h * HEAD_DIM:(h + 1) * HEAD_DIM]
    hsum = hsum_ref[...]
    hsum128 = hsum_ref[0:128, 0:128]
    tm = x_ref.shape[0]
    pos = (pl.program_id(0) % tiles_per_seq) * tm + lax.broadcasted_iota(jnp.int32, (tm, HEAD_DIM), 0)
    lane = lax.broadcasted_iota(jnp.int32, (tm, HEAD_DIM), 1)
    ones_col = (lane == 0).astype(BF16)

    qm = _head_rms(col(0), hsum, gq_ref[...])
    km = _head_rms(col(1), hsum, gk_ref[...])
    for j in range(tm // MOBA_BLOCK):
        kmean_ref[0, j:j + 1, :] = jnp.mean(km[j * MOBA_BLOCK:(j + 1) * MOBA_BLOCK, :], axis=0, keepdims=True)
    km = km.astype(BF16)
    vm = col(2).astype(BF16)
    moba_id = (lane == pos // MOBA_BLOCK).astype(BF16)
    for h in range(MOBA_HEADS):
        qm_ref[0, h] = head(qm, h)
        kam_ref[0, h] = jnp.concatenate([head(km, h), moba_id], axis=1)
        vam_ref[0, h] = jnp.concatenate([head(vm, h), ones_col], axis=1)

    qd = (_head_rms(col(3), hsum, gnq_ref[...]) * (HEAD_DIM ** -0.5)).astype(BF16)
    for hk in range(NSA_KV_HEADS):
        for j in range(tm // NSA_TQ):
            for g in range(NSA_GROUP):
                qs_ref[0, hk, j, g * NSA_TQ:(g + 1) * NSA_TQ, :] = head(qd, hk * NSA_GROUP + g)[j * NSA_TQ:(j + 1) * NSA_TQ, :]
    kvc_ref[0] = col(4)
    kvc_ref[1] = col(5)
    ks = _head_rms(col(6), hsum128, gks_ref[...]).astype(BF16)
    vs = col(7).astype(BF16)
    kw = _head_rms(col(8), hsum128, gkw_ref[...]).astype(BF16)
    vw = col(9).astype(BF16)
    sel_id = (lane == (pos // SEL_BLOCK) % SEL_LANES).astype(BF16)
    for hk in range(NSA_KV_HEADS):
        kas_ref[0, hk] = jnp.concatenate([head(ks, hk), sel_id], axis=1)
        vas_ref[0, hk] = jnp.concatenate([head(vs, hk), ones_col], axis=1)
        kvw_ref[0, hk] = jnp.concatenate([head(kw, hk), head(vw, hk)], axis=1)
    gt_ref[...] = _sigmoid(col(10))


def _odd_in(h2, batch, seq, gain, w_in, moba_q_norm, moba_k_norm, nsa_q_norm, nsa_ksel_norm, nsa_kwin_norm, tm=512):
    t = h2.shape[0]
    assert MOBA_LANES == HEAD_DIM and SEL_LANES == HEAD_DIM and seq % tm == 0 and tm % MOBA_BLOCK == 0
    tps = seq // tm
    w = jnp.pad(w_in, ((0, 0), (0, ODD_IN_PAD - w_in.shape[1]))).astype(BF16)
    hsum = jnp.asarray(np.kron(np.eye(MOBA_W // HEAD_DIM), np.ones((HEAD_DIM, HEAD_DIM))), BF16)
    tile = lambda g, width: jnp.tile(g.astype(F32), width // HEAD_DIM).reshape(1, width)
    row = lambda width: pl.BlockSpec((tm, width), lambda i: (i, 0))
    heads = lambda n, width: pl.BlockSpec((1, n, tm, width), lambda i: (i // tps, 0, i % tps, 0))
    nmb = tm // MOBA_BLOCK
    nqt = tm // NSA_TQ
    rows = NSA_GROUP * NSA_TQ
    sds = jax.ShapeDtypeStruct
    out_specs = [heads(MOBA_HEADS, HEAD_DIM), heads(MOBA_HEADS, LANES), pl.BlockSpec((1, nmb, MOBA_W), lambda i: (i, 0, 0)),
                 heads(MOBA_HEADS, LANES),
                 pl.BlockSpec((1, NSA_KV_HEADS, nqt, rows, HEAD_DIM), lambda i: (i // tps, 0, i % tps, 0, 0)),
                 pl.BlockSpec((2, tm, KV_W), lambda i: (0, i, 0)),
                 heads(NSA_KV_HEADS, LANES), heads(NSA_KV_HEADS, LANES), heads(NSA_KV_HEADS, LANES),
                 row(128)]
    out_shape = [sds((batch, MOBA_HEADS, seq, HEAD_DIM), F32), sds((batch, MOBA_HEADS, seq, LANES), BF16),
                 sds((t // tm, nmb, MOBA_W), F32), sds((batch, MOBA_HEADS, seq, LANES), BF16),
                 sds((batch, NSA_KV_HEADS, seq // NSA_TQ, rows, HEAD_DIM), BF16),
                 sds((2, t, KV_W), F32), sds((batch, NSA_KV_HEADS, seq, LANES), BF16),
                 sds((batch, NSA_KV_HEADS, seq, LANES), BF16), sds((batch, NSA_KV_HEADS, seq, LANES), BF16),
                 sds((t, 128), F32)]
    return pl.pallas_call(
        functools.partial(_odd_in_kernel, tps),
        grid=(t // tm,),
        in_specs=[row(D_MODEL), _full((1, D_MODEL)), _full((D_MODEL, ODD_IN_PAD)), _full((MOBA_W, MOBA_W)),
                  _full((1, MOBA_W)), _full((1, MOBA_W)), _full((1, NSA_W)), _full((1, KV_W)), _full((1, KV_W))],
        out_specs=out_specs,
        out_shape=out_shape,
        compiler_params=_params("arbitrary"),
        name="odd_in",
    )(h2, gain, w, hsum, tile(moba_q_norm, MOBA_W), tile(moba_k_norm, MOBA_W), tile(nsa_q_norm, NSA_W),
      tile(nsa_ksel_norm, KV_W), tile(nsa_kwin_norm, KV_W))


def _compress_kernel(c_ref, w1_ref, w2_ref, pe_ref, g_ref, o_ref):
    kind = pl.program_id(0)
    n16 = c_ref.shape[1] // CMP_STRIDE
    half = CMP_STRIDE * HEAD_DIM
    peb = _dot(pe_ref[0], w1_ref[0])[0:1, :]
    xs = [c_ref[0, pl.ds(s, n16, stride=CMP_STRIDE), :].astype(BF16) for s in range(CMP_STRIDE)]
    for h in range(NSA_KV_HEADS):
        first = second = None
        for s in range(CMP_STRIDE):
            x = xs[s][:, h * HEAD_DIM:(h + 1) * HEAD_DIM]
            a = _dot(x, w1_ref[0, s * HEAD_DIM:(s + 1) * HEAD_DIM, :])
            b = _dot(x, w1_ref[0, half + s * HEAD_DIM:half + (s + 1) * HEAD_DIM, :])
            first = a if first is None else first + a
            second = b if second is None else second + b
        hid = _gelu(first + pltpu.roll(second, n16 - 1, 0) + peb)
        out = _dot(hid.astype(BF16), w2_ref[0])
        o_ref[0, 0, h] = jnp.where(kind == 0, _rms(out, g_ref[...]), out).astype(BF16)


def _compress(kvc, batch, seq, pe_k, w1_k, w2_k, pe_v, w1_v, w2_v, kcmp_norm):
    n16 = seq // CMP_STRIDE
    half = CMP_STRIDE * HEAD_DIM
    w1 = jnp.stack([w1_k, w1_v]).astype(BF16)
    w2 = jnp.stack([w2_k, w2_v]).astype(BF16)
    pe = jnp.stack([pe_k, pe_v]).reshape(2, 1, 2 * half)
    pe = jnp.broadcast_to(pe, (2, 8, 2 * half)).astype(BF16)
    return pl.pallas_call(
        _compress_kernel,
        grid=(2, batch),
        in_specs=[pl.BlockSpec((1, seq, KV_W), lambda k, b: (k, b, 0)),
                  pl.BlockSpec((1, 2 * half, CMP_HIDDEN), lambda k, b: (k, 0, 0)),
                  pl.BlockSpec((1, CMP_HIDDEN, HEAD_DIM), lambda k, b: (k, 0, 0)),
                  pl.BlockSpec((1, 8, 2 * half), lambda k, b: (k, 0, 0)),
                  _full((1, HEAD_DIM))],
        out_specs=pl.BlockSpec((1, 1, NSA_KV_HEADS, n16, HEAD_DIM), lambda k, b: (k, b, 0, 0, 0)),
        out_shape=jax.ShapeDtypeStruct((2, batch, NSA_KV_HEADS, n16, HEAD_DIM), BF16),
        compiler_params=_params("arbitrary", "arbitrary"),
        name="nsa_compress",
    )(kvc, w1, w2, pe, kcmp_norm.astype(F32).reshape(1, HEAD_DIM))


M_INIT = -1e30


LANES = 128


def _softmax_init(m_ref, acc_ref):
    m_ref[...] = jnp.full(m_ref.shape, M_INIT, F32)
    acc_ref[...] = jnp.zeros(acc_ref.shape, F32)


def _softmax_step(s, v_aug, m_ref, acc_ref):
    m_old = m_ref[...]
    m_new = jnp.maximum(m_old, jnp.max(s, axis=-1, keepdims=True))
    alpha = jnp.exp(m_old - m_new)
    p = jnp.exp(s - jnp.tile(m_new, (1, s.shape[1] // LANES)))
    acc_ref[...] = alpha * acc_ref[...] + _dot(p.astype(BF16), v_aug)
    m_ref[...] = m_new


def _past_keys_loop(n_keys, tile, step, riders=()):
    n_full = n_keys // tile

    def body(j, carry):
        step(pl.multiple_of(j * tile, tile), tile)
        return carry

    extras = []
    for k, (with_tile, alone) in enumerate(riders):
        def both(with_tile=with_tile, k=k):
            out = with_tile()
            step(k * tile, tile)
            return out

        extras.append(lax.cond(n_full > k, both, alone))
    lax.fori_loop(len(riders), n_full, body, 0)
    rest = n_keys - n_full * tile
    start = pl.multiple_of(n_full * tile, tile)

    @pl.when(rest > tile // 2)
    def _():
        step(start, tile)

    @pl.when((rest > 0) & (rest <= tile // 2))
    def _():
        step(start, tile // 2)

    return extras


def _softmax_result(acc_ref):
    acc = acc_ref[...]
    return acc[:, 0:HEAD_DIM] * (1.0 / acc[:, HEAD_DIM:HEAD_DIM + 1])


def _pick_top(score, pos, width, k):
    sel = jnp.zeros(score.shape, jnp.bool_)
    for _ in range(k):
        m, idx = _first_max(score, pos, width, axis=0)
        hit = (pos == idx) & (m > -jnp.inf)
        sel = sel | hit
        score = jnp.where(pos == idx, -jnp.inf, score)
    return sel


MOBA_LANES = 64
MOBA_TK = 2048


MOBA_TQ = 4 * MOBA_BLOCK


def _moba_kernel(q_ref, ka_ref, va_ref, kmean_ref, o_ref, qa_ref, m_ref, acc_ref):
    i0 = pl.program_id(2) * (MOBA_TQ // MOBA_BLOCK)
    q = q_ref[0, 0]
    q_hi, q_lo = _split(q)
    km_hi, km_lo = _split(kmean_ref[0, 0])
    gate = _dot_nt(km_hi, q_hi) + (_dot_nt(km_lo, q_hi) + _dot_nt(km_hi, q_lo))
    blk = lax.broadcasted_iota(jnp.int32, gate.shape, 0)
    cur = i0 + lax.broadcasted_iota(jnp.int32, gate.shape, 1) // MOBA_BLOCK
    sel = _pick_top(jnp.where(blk < cur, gate, -jnp.inf), blk, gate.shape[0], MOBA_TOPK)
    qs = (q * (HEAD_DIM ** -0.5)).astype(BF16)
    past = jnp.where(sel & (blk < i0), 0.0, NEG).T[:, 0:MOBA_LANES]
    own = jnp.where((blk == cur) | (sel & (blk >= i0)), 0.0, NEG).T[:, 0:MOBA_LANES]
    qa_ref[0] = jnp.concatenate([qs, past.astype(BF16)], axis=1)
    qa_ref[1] = jnp.concatenate([qs, own.astype(BF16)], axis=1)
    _softmax_init(m_ref, acc_ref)

    def step(start, size):
        s = _dot_nt(qa_ref[0], ka_ref[0, 0, pl.ds(start, size), :])
        _softmax_step(s, va_ref[0, 0, pl.ds(start, size), :], m_ref, acc_ref)

    _past_keys_loop(i0 * MOBA_BLOCK, MOBA_TK, step)
    start = pl.multiple_of(i0 * MOBA_BLOCK, MOBA_TQ)
    s = _dot_nt(qa_ref[1], ka_ref[0, 0, pl.ds(start, MOBA_TQ), :])
    qpos = lax.broadcasted_iota(jnp.int32, s.shape, 0)
    kpos = lax.broadcasted_iota(jnp.int32, s.shape, 1)
    hidden = (qpos // MOBA_BLOCK == kpos // MOBA_BLOCK) & (kpos > qpos)
    _softmax_step(jnp.where(hidden, NEG, s), va_ref[0, 0, pl.ds(start, MOBA_TQ), :], m_ref, acc_ref)
    o_ref[0, 0] = _softmax_result(acc_ref).astype(BF16)


def _moba(qm, ka, kmean, va, batch, seq):
    nmb = seq // MOBA_BLOCK
    assert nmb <= MOBA_LANES and seq % MOBA_TK == 0 and seq % MOBA_TQ == 0
    kmean = kmean.reshape(batch, nmb, MOBA_HEADS, HEAD_DIM).transpose(0, 2, 1, 3)
    kmean = jnp.pad(kmean, ((0, 0), (0, 0), (0, LANES - nmb), (0, 0)))
    return pl.pallas_call(
        _moba_kernel,
        grid=(batch, MOBA_HEADS, seq // MOBA_TQ),
        in_specs=[pl.BlockSpec((1, 1, MOBA_TQ, HEAD_DIM), lambda b, h, i: (b, h, i, 0)),
                  pl.BlockSpec((1, 1, seq, LANES), lambda b, h, i: (b, h, 0, 0)),
                  pl.BlockSpec((1, 1, seq, LANES), lambda b, h, i: (b, h, 0, 0)),
                  pl.BlockSpec((1, 1, LANES, HEAD_DIM), lambda b, h, i: (b, h, 0, 0))],
        out_specs=pl.BlockSpec((1, 1, MOBA_TQ, HEAD_DIM), lambda b, h, i: (b, h, i, 0)),
        out_shape=jax.ShapeDtypeStruct((batch, MOBA_HEADS, seq, HEAD_DIM), BF16),
        scratch_shapes=[pltpu.VMEM((2, MOBA_TQ, LANES), BF16), pltpu.VMEM((MOBA_TQ, LANES), F32),
                        pltpu.VMEM((MOBA_TQ, LANES), F32)],
        compiler_params=_params("arbitrary", "arbitrary", "arbitrary"),
        name="moba",
    )(qm, ka, va, kmean)


NSA_TQ = 2 * SEL_BLOCK
NSA_TK = 2048
SEL_LANES = 64
SUPER_KEYS = SEL_LANES * SEL_BLOCK
CMP_WIDTH_STEP = 128


def _nsa_kernel(n_super, q_ref, kc_ref, vc_ref, ka_ref, va_ref, kvw_ref, gt_ref, e_ref, band_ref, o_ref,
                qa_ref, m_ref, acc_ref):
    qi = pl.program_id(2)
    tq = NSA_TQ
    s0 = qi * tq
    q = q_ref[0, 0, 0]

    def compressed(width):
        sc = _dot_nt(q, kc_ref[0, 0, 0:width, :])
        rq = lax.broadcasted_iota(jnp.int32, sc.shape, 0) & (tq - 1)
        n = lax.broadcasted_iota(jnp.int32, sc.shape, 1)
        vis = n * CMP_STRIDE + (CMP_BLOCK - 1) <= s0 + rq
        sc = jnp.where(vis, sc, NEG)
        pc = jnp.where(vis, jnp.exp(sc - jnp.max(sc, axis=-1, keepdims=True)), 0.0)
        pc = pc * (1.0 / jnp.maximum(jnp.sum(pc, axis=-1, keepdims=True), 1e-30))
        imp = pc[0:tq]
        for g in range(1, NSA_GROUP):
            imp = imp + pc[g * tq:(g + 1) * tq]
        return _dot(pc.astype(BF16), vc_ref[0, 0, 0:width, :]), _dot_x2(imp, band_ref[0:width, :])

    n16 = kc_ref.shape[2]
    widths = list(range(CMP_WIDTH_STEP, n16, CMP_WIDTH_STEP)) + [n16]
    o_c, pslc = lax.switch((s0 + tq - 1) // (CMP_WIDTH_STEP * CMP_STRIDE),
                           [functools.partial(compressed, wd) for wd in widths])

    wlen = WINDOW + tq

    def window(kvw, masked):
        sw = masked(_dot_nt(q, kvw[:, 0:HEAD_DIM]))
        pw = jnp.exp(sw - jnp.max(sw, axis=-1, keepdims=True))
        return _dot(pw.astype(BF16), kvw)[:, HEAD_DIM:2 * HEAD_DIM] * (1.0 / jnp.sum(pw, axis=-1, keepdims=True))

    def window_interior():
        def masked(sw):
            rq = lax.broadcasted_iota(jnp.int32, (sw.shape[0], tq), 0) & (tq - 1)
            c = lax.broadcasted_iota(jnp.int32, (sw.shape[0], tq), 1)
            return jnp.concatenate([jnp.where(c > rq, sw[:, 0:tq], NEG), sw[:, tq:WINDOW],
                                    jnp.where(c <= rq, sw[:, WINDOW:wlen], NEG)], axis=1)

        return window(kvw_ref[0, 0, pl.ds(pl.multiple_of(s0 - WINDOW, tq), wlen), :], masked)

    def window_start():
        def masked(sw):
            kabs = lax.broadcasted_iota(jnp.int32, sw.shape, 1)
            t = s0 + (lax.broadcasted_iota(jnp.int32, sw.shape, 0) & (tq - 1))
            return jnp.where((kabs <= t) & (kabs > t - WINDOW), sw, NEG)

        return window(kvw_ref[0, 0, 0:wlen, :], masked)

    pslc_t = pslc.T
    nb = pslc_t.shape[0]
    blk = lax.broadcasted_iota(jnp.int32, pslc_t.shape, 0)
    cur = (s0 + lax.broadcasted_iota(jnp.int32, pslc_t.shape, 1)) // SEL_BLOCK
    c0 = s0 // SEL_BLOCK
    elig = (blk >= 1) & (blk <= cur - 2)
    sel = _pick_top(jnp.where(elig, pslc_t, -jnp.inf), blk, nb, SEL_TOPK - 3)
    sel = sel | (blk == 0) | (blk == cur - 1)
    past = jnp.where(sel & (blk < c0), 0.0, NEG).T.astype(BF16)
    for st in range(n_super):
        b = past[:, st * SEL_LANES:(st + 1) * SEL_LANES]
        qa_ref[st] = jnp.concatenate([q, jnp.concatenate([b] * NSA_GROUP, axis=0)], axis=1)

    _softmax_init(m_ref, acc_ref)
    d0 = pl.multiple_of(s0, tq)
    s = _dot_nt(q, ka_ref[0, 0, pl.ds(d0, tq), :][:, 0:HEAD_DIM])
    qpos = lax.broadcasted_iota(jnp.int32, s.shape, 0) & (tq - 1)
    kpos = lax.broadcasted_iota(jnp.int32, s.shape, 1)
    _softmax_step(jnp.where(kpos <= qpos, s, NEG), va_ref[0, 0, pl.ds(d0, tq), :], m_ref, acc_ref)

    def step(start, size):
        s = _dot_nt(qa_ref[start // SUPER_KEYS], ka_ref[0, 0, pl.ds(start, size), :])
        _softmax_step(s, va_ref[0, 0, pl.ds(start, size), :], m_ref, acc_ref)

    (o_w,) = _past_keys_loop(s0, NSA_TK, step, riders=(
        (window_interior, lambda: lax.cond(s0 >= WINDOW, window_interior, window_start)),))
    o_s = _softmax_result(acc_ref)

    w = NSA_GROUP * HEAD_DIM
    gexp = _dot_x2(gt_ref[...], e_ref[0])
    wide = lambda x: jnp.concatenate([x[g * tq:(g + 1) * tq] for g in range(NSA_GROUP)], axis=1)
    o_ref[...] = (gexp[:, 0:w] * wide(o_c) + gexp[:, w:2 * w] * wide(o_s) + gexp[:, 2 * w:3 * w] * wide(o_w)).astype(BF16)


def _nsa(qs, kcmp, vcmp, ka, va, kvw, gates, batch, seq):
    tq = NSA_TQ
    nq = seq // tq
    nb = seq // SEL_BLOCK
    n16 = seq // CMP_STRIDE
    assert seq % SUPER_KEYS == 0 and tq == 2 * SEL_BLOCK and WINDOW % tq == 0 and NSA_TK >= WINDOW
    n_super = seq // SUPER_KEYS
    rows = NSA_GROUP * tq
    e = np.zeros((NSA_KV_HEADS, 128, 3 * NSA_GROUP * HEAD_DIM), np.float32)
    for br in range(3):
        for hk in range(NSA_KV_HEADS):
            for g in range(NSA_GROUP):
                c = (br * NSA_GROUP + g) * HEAD_DIM
                e[hk, br * NSA_HEADS + hk * NSA_GROUP + g, c:c + HEAD_DIM] = 1.0
    nn, jj = np.arange(n16)[:, None], np.arange(nb)[None, :]
    band = ((nn >= 4 * jj - 1) & (nn <= 4 * jj + 3)).astype(np.float32)
    resident = lambda width: pl.BlockSpec((1, 1, seq, width), lambda b, h, i: (b, h, 0, 0))
    w = NSA_GROUP * HEAD_DIM
    return pl.pallas_call(
        functools.partial(_nsa_kernel, n_super),
        grid=(batch, NSA_KV_HEADS, nq),
        in_specs=[pl.BlockSpec((1, 1, 1, rows, HEAD_DIM), lambda b, h, i: (b, h, i, 0, 0)),
                  pl.BlockSpec((1, 1, n16, HEAD_DIM), lambda b, h, i: (b, h, 0, 0)),
                  pl.BlockSpec((1, 1, n16, HEAD_DIM), lambda b, h, i: (b, h, 0, 0)),
                  resident(LANES), resident(LANES), resident(LANES),
                  pl.BlockSpec((tq, 128), lambda b, h, i: (b * nq + i, 0)),
                  pl.BlockSpec((1, 128, 3 * w), lambda b, h, i: (h, 0, 0)),
                  pl.BlockSpec((n16, nb), lambda b, h, i: (0, 0))],
        out_specs=pl.BlockSpec((tq, w), lambda b, h, i: (b * nq + i, h)),
        out_shape=jax.ShapeDtypeStruct((batch * seq, NSA_W), BF16),
        scratch_shapes=[pltpu.VMEM((n_super, rows, LANES), BF16), pltpu.VMEM((rows, LANES), F32),
                        pltpu.VMEM((rows, LANES), F32)],
        compiler_params=_params("arbitrary", "arbitrary", "arbitrary"),
        name="nsa",
    )(qs, kcmp, vcmp, ka, va, kvw, gates, jnp.asarray(e, BF16), jnp.asarray(band, BF16))


def _odd_out_kernel(om_ref, on_ref, h_ref, w_ref, o_ref):
    acc = h_ref[...] + _dot(on_ref[...], w_ref[MOBA_W:D_MODEL, :])
    for h in range(MOBA_HEADS):
        acc = acc + _dot(om_ref[0, h], w_ref[h * HEAD_DIM:(h + 1) * HEAD_DIM, :])
    o_ref[...] = acc


def _odd_out(o_moba, o_nsa, h2, w_out, seq, tm=512):
    t = h2.shape[0]
    tps = seq // tm
    row = lambda w: pl.BlockSpec((tm, w), lambda i: (i, 0))
    return pl.pallas_call(
        _odd_out_kernel,
        grid=(t // tm,),
        in_specs=[pl.BlockSpec((1, MOBA_HEADS, tm, HEAD_DIM), lambda i: (i // tps, 0, i % tps, 0)),
                  row(NSA_W), row(D_MODEL), _full((D_MODEL, D_MODEL))],
        out_specs=row(D_MODEL),
        out_shape=jax.ShapeDtypeStruct((t, D_MODEL), F32),
        compiler_params=_params("arbitrary"),
        name="odd_out",
    )(o_moba, o_nsa, h2, w_out)


def _even_layer(h2, batch, seq, norm, w_in, w_out, lam_re, lam_im, log_dt, b_re, b_im, c_re, c_im, d, w_glu, conv_w, conv_b):
    u, ub, yb = _even_in(h2, norm.reshape(1, D_MODEL), w_in.astype(BF16), conv_w, conv_b.reshape(1, CONV_WIDTH), seq)
    ypre = _s5_mixer_pre(ub, batch, seq, lam_re, lam_im, log_dt, b_re, b_im, c_re, c_im)
    return _even_out(ypre, u, yb, h2, d.reshape(1, S5_WIDTH), w_glu.astype(BF16), w_out.astype(BF16))


def _odd_layer(h2, batch, seq, norm, w_in, w_out, moba_q_norm, moba_k_norm, nsa_q_norm, nsa_kcmp_norm, nsa_ksel_norm,
               nsa_kwin_norm, cmp_pe_k, cmp_w1_k, cmp_w2_k, cmp_pe_v, cmp_w1_v, cmp_w2_v):
    qm, kam, kmean, vam, qs, kvc, kas, vas, kvw, gates = _odd_in(
        h2, batch, seq, norm.reshape(1, D_MODEL), w_in, moba_q_norm, moba_k_norm, nsa_q_norm, nsa_ksel_norm, nsa_kwin_norm)
    cmp = _compress(kvc, batch, seq, cmp_pe_k, cmp_w1_k, cmp_w2_k, cmp_pe_v, cmp_w1_v, cmp_w2_v, nsa_kcmp_norm)
    o_moba = _moba(qm, kam, kmean, vam, batch, seq)
    o_nsa = _nsa(qs, cmp[0], cmp[1], kas, vas, kvw, gates, batch, seq)
    return _odd_out(o_moba, o_nsa, h2, w_out.astype(BF16), seq)


def kernel(x, ev_norm_mix, ev_w_in, ev_w_out, s5_lam_re, s5_lam_im, s5_log_dt, s5_b_re, s5_b_im, s5_c_re, s5_c_im, s5_d, s5_w_glu, conv_w, conv_b, od_norm_mix, od_w_in, od_w_out, moba_q_norm, moba_k_norm, nsa_q_norm, nsa_kcmp_norm, nsa_ksel_norm, nsa_kwin_norm, cmp_pe_k, cmp_w1_k, cmp_w2_k, cmp_pe_v, cmp_w1_v, cmp_w2_v, moe_norm, moe_w_group, moe_b_group, moe_w_expert, moe_b_expert, moe_w_gate, moe_w_up, moe_w_down):
    batch, seq, _ = x.shape
    depth = moe_norm.shape[0]
    h = x.reshape(batch * seq, D_MODEL)
    for layer in range(depth):
        i = layer // 2
        if layer % 2 == 0:
            h = _even_layer(h, batch, seq, ev_norm_mix[i], ev_w_in[i], ev_w_out[i], s5_lam_re[i], s5_lam_im[i], s5_log_dt[i],
                            s5_b_re[i], s5_b_im[i], s5_c_re[i], s5_c_im[i], s5_d[i], s5_w_glu[i], conv_w[i], conv_b[i])
        else:
            h = _odd_layer(h, batch, seq, od_norm_mix[i], od_w_in[i], od_w_out[i], moba_q_norm[i], moba_k_norm[i],
                           nsa_q_norm[i], nsa_kcmp_norm[i], nsa_ksel_norm[i], nsa_kwin_norm[i], cmp_pe_k[i], cmp_w1_k[i],
                           cmp_w2_k[i], cmp_pe_v[i], cmp_w1_v[i], cmp_w2_v[i])
        h = _moe(h, moe_norm[layer].reshape(1, D_MODEL), moe_w_group[layer], moe_b_group[layer], moe_w_expert[layer],
                 moe_b_expert[layer], moe_w_gate[layer], moe_w_up[layer], moe_w_down[layer])
    return h.reshape(batch, seq, D_MODEL)
```

```python
import functools
import math

import jax
import jax.numpy as jnp
import numpy as np
from jax import lax
from jax.experimental import pallas as pl
from jax.experimental.pallas import tpu as pltpu

D_MODEL = 1024
HEAD_DIM = 64
EPS = 1e-6
S5_WIDTH = 256
S5_GROUP = 16
S5_GROUPS = 16
S5_STATE = 64
S5_CHUNK = 16
CONV_WIDTH = 768
CONV_K = 3
MOBA_HEADS = 4
NSA_HEADS = 12
NSA_KV_HEADS = 2
NSA_GROUP = 6
MOBA_W = 256
NSA_W = 768
KV_W = 128
MOBA_BLOCK = 256
MOBA_TOPK = 3
CMP_BLOCK = 32
CMP_STRIDE = 16
CMP_HIDDEN = 256
SEL_BLOCK = 64
SEL_TOPK = 8
WINDOW = 512
N_GROUPS = 4
EXPERTS_PER_GROUP = 4
N_EXPERTS = 16
EXPERT_FF = 256

VMEM_LIMIT_BYTES = 56 * 1024 * 1024
NEG = -float(2 ** 100)
F32 = jnp.float32
BF16 = jnp.bfloat16


def _params(*semantics):
    return pltpu.CompilerParams(dimension_semantics=semantics, vmem_limit_bytes=VMEM_LIMIT_BYTES)


def _dot(a, b):
    return jnp.dot(a, b, preferred_element_type=F32)


def _dot_nt(a, b):
    return lax.dot_general(a, b, (((1,), (1,)), ((), ())), preferred_element_type=F32)


def _split(x):
    hi = x.astype(BF16)
    lo = (x - hi.astype(F32)).astype(BF16)
    return hi, lo


def _dot_x2(x, w):
    hi, lo = _split(x)
    return _dot(hi, w) + _dot(lo, w)


def _dot_x3(x, w_hi, w_lo):
    hi, lo = _split(x)
    return _dot(hi, w_hi) + (_dot(hi, w_lo) + _dot(lo, w_hi))


def _rms(x, gain):
    return x * lax.rsqrt(jnp.mean(x * x, axis=-1, keepdims=True) + EPS) * gain


def _gelu(x):
    return 0.5 * x * (1.0 + jnp.tanh(math.sqrt(2.0 / math.pi) * (x + 0.044715 * (x * x * x))))


def _sigmoid(x):
    return 1.0 / (1.0 + jnp.exp(-x))


def _full(shape):
    n = len(shape)
    return pl.BlockSpec(shape, lambda *_: (0,) * n)


def _even_in_kernel(tiles_per_seq, x_ref, g_ref, w_ref, cw_ref, cb_ref, u_ref, ub_ref, yb_ref, carry_ref):
    i = pl.program_id(0)
    xn = _rms(x_ref[...], g_ref[...]).astype(BF16)
    u = _dot(xn, w_ref[:, 0:S5_WIDTH])
    u_ref[...] = u
    ub_ref[...] = u.astype(BF16)
    o = S5_WIDTH
    xc = _dot(xn, w_ref[:, o:o + CONV_WIDTH])
    gb = _dot(xn, w_ref[:, o + CONV_WIDTH:o + 2 * CONV_WIDTH])
    gc = _dot(xn, w_ref[:, o + 2 * CONV_WIDTH:o + 3 * CONV_WIDTH])
    z = gc * xc
    tm = z.shape[0]

    @pl.when(i % tiles_per_seq == 0)
    def _():
        carry_ref[...] = jnp.zeros_like(carry_ref)

    row = lax.broadcasted_iota(jnp.int32, z.shape, 0)
    prev1 = carry_ref[7:8, :]
    prev2 = carry_ref[6:7, :]
    z1 = jnp.where(row == 0, prev1, pltpu.roll(z, 1, 0))
    z2 = jnp.where(row == 0, prev2, jnp.where(row == 1, prev1, pltpu.roll(z, 2, 0)))
    y = cw_ref[0:1, :] * z2 + cw_ref[1:2, :] * z1 + cw_ref[2:3, :] * z + cb_ref[...]
    yb_ref[...] = (gb * y).astype(BF16)
    carry_ref[...] = z[tm - 8:tm, :]


def _even_in(x2, gain, w_in, conv_w, conv_b, seq, tm=512):
    t = x2.shape[0]
    n_in = w_in.shape[1]
    return pl.pallas_call(
        functools.partial(_even_in_kernel, seq // tm),
        grid=(t // tm,),
        in_specs=[pl.BlockSpec((tm, D_MODEL), lambda i: (i, 0)), _full((1, D_MODEL)),
                  _full((D_MODEL, n_in)), _full((CONV_K, CONV_WIDTH)), _full((1, CONV_WIDTH))],
        out_specs=[pl.BlockSpec((tm, S5_WIDTH), lambda i: (i, 0)), pl.BlockSpec((tm, S5_WIDTH), lambda i: (i, 0)),
                   pl.BlockSpec((tm, CONV_WIDTH), lambda i: (i, 0))],
        out_shape=[jax.ShapeDtypeStruct((t, S5_WIDTH), F32), jax.ShapeDtypeStruct((t, S5_WIDTH), BF16),
                   jax.ShapeDtypeStruct((t, CONV_WIDTH), BF16)],
        scratch_shapes=[pltpu.VMEM((8, CONV_WIDTH), F32)],
        compiler_params=_params("arbitrary"),
        name="even_in",
    )(x2, gain, w_in, conv_w, conv_b)


def _s5_weights(lam_re, lam_im, log_dt, b_re, b_im, c_re, c_im):
    g, p, hg, ck = S5_GROUPS, S5_STATE, S5_GROUP, S5_CHUNK
    lr, li = lam_re.astype(F32), lam_im.astype(F32)
    dt = jnp.exp(log_dt.astype(F32))[:, None]
    mag = jnp.exp(lr * dt)
    a_re, a_im = mag * jnp.cos(li * dt), mag * jnp.sin(li * dt)
    den = lr * lr + li * li
    f_re = ((a_re - 1.0) * lr + a_im * li) / den
    f_im = (a_im * lr - (a_re - 1.0) * li) / den
    br, bi = b_re.astype(F32), b_im.astype(F32)
    bb_re = f_re[..., None] * br - f_im[..., None] * bi
    bb_im = f_re[..., None] * bi + f_im[..., None] * br
    pw_re, pw_im = [jnp.ones_like(a_re)], [jnp.zeros_like(a_im)]
    for _ in range(ck):
        r, m = pw_re[-1], pw_im[-1]
        pw_re.append(r * a_re - m * a_im)
        pw_im.append(r * a_im + m * a_re)
    pw_re, pw_im = jnp.stack(pw_re), jnp.stack(pw_im)
    cr, ci = c_re.astype(F32), c_im.astype(F32)
    rev_re, rev_im = pw_re[ck - 1::-1][:ck], pw_im[ck - 1::-1][:ck]
    ws_re = rev_re[:, :, :, None] * bb_re[None] - rev_im[:, :, :, None] * bb_im[None]
    ws_im = rev_re[:, :, :, None] * bb_im[None] + rev_im[:, :, :, None] * bb_re[None]
    ca_re = cr[None] * pw_re[1:, :, None, :] - ci[None] * pw_im[1:, :, None, :]
    ca_im = cr[None] * pw_im[1:, :, None, :] + ci[None] * pw_re[1:, :, None, :]
    cb_re = jnp.einsum('ghp,kgp,gpj->kghj', cr, pw_re[:ck], bb_re) - jnp.einsum('ghp,kgp,gpj->kghj', cr, pw_im[:ck], bb_im) \
        - jnp.einsum('ghp,kgp,gpj->kghj', ci, pw_re[:ck], bb_im) - jnp.einsum('ghp,kgp,gpj->kghj', ci, pw_im[:ck], bb_re)
    lag = np.arange(ck)[None, :] - np.arange(ck)[:, None]
    tz = cb_re[np.clip(lag, 0, ck - 1)]
    tz = jnp.where((lag >= 0)[:, :, None, None, None], tz, 0.0)
    cw = ck * g * hg
    ws = jnp.stack([ws_re, ws_im]).transpose(1, 2, 4, 0, 3).reshape(cw, 2 * p)
    wc = jnp.stack([ca_re, -ca_im]).transpose(0, 2, 4, 1, 3).reshape(2 * g * p, ck * hg)
    tzc = tz.transpose(0, 2, 4, 1, 3).reshape(cw, ck * hg)
    return ws.astype(BF16), wc.astype(BF16), tzc.astype(BF16), pw_re[ck].reshape(1, g * p), pw_im[ck].reshape(1, g * p)


def _group_expand(compact, expand, row_shift, col_shift, col0):
    full = _dot(compact, expand)
    row = lax.broadcasted_iota(jnp.int32, full.shape, 0)
    col = col0 + lax.broadcasted_iota(jnp.int32, full.shape, 1)
    same = ((row >> row_shift) & (S5_GROUPS - 1)) == ((col >> col_shift) & (S5_GROUPS - 1))
    return jnp.where(same, full, 0.0).astype(BF16)


def _s5_state_kernel(u_ref, ws_ref, e_ref, s_ref, w_scr):
    @pl.when(pl.program_id(1) == 0)
    def _():
        w_scr[...] = _group_expand(ws_ref[...], e_ref[...], 4, 6, pl.program_id(0) * w_scr.shape[1])

    s_ref[...] = _dot(u_ref[...], w_scr[...])


def _s5_scan_kernel(s_ref, are_ref, aim_ref, xprev_ref, st_ref):
    @pl.when(pl.program_id(0) == 0)
    def _():
        st_ref[...] = jnp.zeros_like(st_ref)

    a_re, a_im = are_ref[...], aim_ref[...]
    nb, n = s_ref.shape[0], s_ref.shape[1]
    half = a_re.shape[1]

    def body(c, carry):
        out = []
        for b in range(nb):
            xr, xi = carry[2 * b], carry[2 * b + 1]
            xprev_ref[b, pl.ds(c, 1), 0:half] = xr
            xprev_ref[b, pl.ds(c, 1), half:2 * half] = xi
            s = s_ref[b, pl.ds(c, 1), :]
            out += [a_re * xr - a_im * xi + s[:, 0:half], a_re * xi + a_im * xr + s[:, half:2 * half]]
        return tuple(out)

    init = tuple(st_ref[b:b + 1, o:o + half] for b in range(nb) for o in (0, half))
    final = lax.fori_loop(0, n, body, init, unroll=4)
    for b in range(nb):
        st_ref[b:b + 1, 0:half] = final[2 * b]
        st_ref[b:b + 1, half:2 * half] = final[2 * b + 1]


def _s5_out_kernel(u_ref, xp_ref, tz_ref, wc_ref, e_ref, y_ref, tz_scr, wc_scr):
    @pl.when(pl.program_id(1) == 0)
    def _():
        col0 = pl.program_id(0) * tz_scr.shape[1]
        tz_scr[...] = _group_expand(tz_ref[...], e_ref[...], 4, 4, col0)
        wc_scr[...] = _group_expand(wc_ref[...], e_ref[...], 6, 4, col0)

    y_ref[...] = _dot(u_ref[...], tz_scr[...]) + _dot(xp_ref[...].astype(BF16), wc_scr[...])


def _expand_matrix(outer, inner):
    e = np.zeros((outer, inner, outer, S5_GROUPS, inner), np.float32)
    for x in range(outer):
        for y in range(inner):
            e[x, y, x, :, y] = 1.0
    return jnp.asarray(e.reshape(outer * inner, outer * S5_GROUPS * inner), BF16)


def _s5_mixer_pre(ub, batch, seq, lam_re, lam_im, log_dt, b_re, b_im, c_re, c_im):
    ws, wc, tz, a16_re, a16_im = _s5_weights(lam_re, lam_im, log_dt, b_re, b_im, c_re, c_im)
    nc = seq // S5_CHUNK
    rows = batch * nc
    cw = S5_CHUNK * S5_WIDTH
    sw = 2 * S5_GROUPS * S5_STATE
    tr = min(rows, 512)
    tn = 512
    uc = ub.reshape(rows, cw)
    e_state = _expand_matrix(2, S5_STATE)
    e_out = _expand_matrix(S5_CHUNK, S5_GROUP)
    s = pl.pallas_call(
        _s5_state_kernel,
        grid=(sw // tn, rows // tr),
        in_specs=[pl.BlockSpec((tr, cw), lambda j, i: (i, 0)), _full(ws.shape),
                  pl.BlockSpec((e_state.shape[0], tn), lambda j, i: (0, j))],
        out_specs=pl.BlockSpec((tr, tn), lambda j, i: (i, j)),
        out_shape=jax.ShapeDtypeStruct((rows, sw), F32),
        scratch_shapes=[pltpu.VMEM((cw, tn), BF16)],
        compiler_params=_params("arbitrary", "arbitrary"),
        name="s5_state",
    )(uc, ws, e_state)
    tc = min(nc, 256)
    xprev = pl.pallas_call(
        _s5_scan_kernel,
        grid=(nc // tc,),
        in_specs=[pl.BlockSpec((batch, tc, sw), lambda i: (0, i, 0)), _full((1, sw // 2)), _full((1, sw // 2))],
        out_specs=pl.BlockSpec((batch, tc, sw), lambda i: (0, i, 0)),
        out_shape=jax.ShapeDtypeStruct((batch, nc, sw), F32),
        scratch_shapes=[pltpu.VMEM((batch, sw), F32)],
        compiler_params=_params("arbitrary"),
        name="s5_scan",
    )(s.reshape(batch, nc, sw), a16_re, a16_im)
    y = pl.pallas_call(
        _s5_out_kernel,
        grid=(cw // tn, rows // tr),
        in_specs=[pl.BlockSpec((tr, cw), lambda j, i: (i, 0)), pl.BlockSpec((tr, sw), lambda j, i: (i, 0)),
                  _full(tz.shape), _full(wc.shape), pl.BlockSpec((e_out.shape[0], tn), lambda j, i: (0, j))],
        out_specs=pl.BlockSpec((tr, tn), lambda j, i: (i, j)),
        out_shape=jax.ShapeDtypeStruct((rows, cw), F32),
        scratch_shapes=[pltpu.VMEM((cw, tn), BF16), pltpu.VMEM((sw, tn), BF16)],
        compiler_params=_params("arbitrary", "arbitrary"),
        name="s5_out",
    )(uc, xprev.reshape(rows, sw), tz, wc, e_out)
    return y.reshape(batch * seq, S5_WIDTH)


def _even_out_kernel(ypre_ref, u_ref, yb_ref, x_ref, d_ref, wglu_ref, wout_ref, o_ref):
    y = _gelu(ypre_ref[...] + d_ref[...] * u_ref[...])
    y = y * _sigmoid(_dot(y.astype(BF16), wglu_ref[...]))
    o_ref[...] = (x_ref[...] + _dot(y.astype(BF16), wout_ref[0:S5_WIDTH, :])
                  + _dot(yb_ref[...], wout_ref[S5_WIDTH:D_MODEL, :]))


def _even_out(ypre, u, yb, x2, d, w_glu, w_out, tm=512):
    t = x2.shape[0]
    row = lambda w: pl.BlockSpec((tm, w), lambda i: (i, 0))
    return pl.pallas_call(
        _even_out_kernel,
        grid=(t // tm,),
        in_specs=[row(S5_WIDTH), row(S5_WIDTH), row(CONV_WIDTH), row(D_MODEL), _full((1, S5_WIDTH)),
                  _full((S5_WIDTH, S5_WIDTH)), _full((D_MODEL, D_MODEL))],
        out_specs=row(D_MODEL),
        out_shape=jax.ShapeDtypeStruct((t, D_MODEL), F32),
        compiler_params=_params("arbitrary"),
        name="even_out",
    )(ypre, u, yb, x2, d, w_glu, w_out)


def _first_max(v, pos, width, axis=-1):
    m = jnp.max(v, axis=axis, keepdims=True)
    idx = jnp.min(jnp.where(v == m, pos, width), axis=axis, keepdims=True)
    return m, idx


MOE_TM = 1024
MOE_ALIGN = 16
MOE_SLOTS = 1152
MOE_WIN = 320
ROUTER_LANES = 128


def _moe_router_kernel(h_ref, g_ref, wr_hi_ref, wr_lo_ref, br_ref, gate_ref, grp_ref, cnt_ref):
    xn = _rms(h_ref[...], g_ref[...])
    logits = _dot_x3(xn, wr_hi_ref[...], wr_lo_ref[...]) + br_ref[...]
    lane = lax.broadcasted_iota(jnp.int32, logits.shape, 1)
    width = logits.shape[1]
    is_g = lane < N_GROUPS
    gl = jnp.where(is_g, logits, -jnp.inf)
    gm, gi = _first_max(gl, lane, width)
    gw = 1.0 / jnp.sum(jnp.where(is_g, jnp.exp(gl - gm), 0.0), axis=-1, keepdims=True)
    lo = N_GROUPS + gi * EXPERTS_PER_GROUP
    in_grp = (lane >= lo) & (lane < lo + EXPERTS_PER_GROUP)
    el = jnp.where(in_grp, logits, -jnp.inf)
    m1, i1 = _first_max(el, lane, width)
    m2, i2 = _first_max(jnp.where(lane == i1, -jnp.inf, el), lane, width)
    p2 = jnp.exp(m2 - m1)
    w1 = gw / (1.0 + p2)
    w2 = gw * p2 / (1.0 + p2)
    gate_ref[...] = jnp.where(lane == i1, w1, 0.0) + jnp.where(lane == i2, w2, 0.0)
    grp = (lane == gi).astype(F32)
    grp_ref[...] = grp.astype(BF16)
    cnt_ref[0] = jnp.broadcast_to(jnp.sum(grp, axis=0, keepdims=True), cnt_ref.shape[1:])


def _moe_expert_kernel(base_ref, nwin_ref, h_ref, g_ref, gate_ref, grp_ref, wg_ref, wu_ref, wd_ref, o_ref,
                       xs_ref, gs_ref, ys_ref, pt_ref):
    i, g = pl.program_id(0), pl.program_id(1)
    tm = h_ref.shape[0]

    @pl.when((i == 0) & (g == 0))
    def _():
        xs_ref[...] = jnp.zeros_like(xs_ref)
        gs_ref[...] = jnp.zeros_like(gs_ref)
        ys_ref[...] = jnp.zeros_like(ys_ref)

    @pl.when(g == 0)
    def _():
        xn = _rms(h_ref[...], g_ref[...]).astype(BF16)
        grp = grp_ref[...]
        earlier = (lax.broadcasted_iota(jnp.int32, (tm, tm), 0) > lax.broadcasted_iota(jnp.int32, (tm, tm), 1)).astype(BF16)
        rank = _dot(earlier, grp)
        lane = lax.broadcasted_iota(jnp.int32, rank.shape, 1)
        for k in range(N_GROUPS):
            rank = rank + jnp.where(lane == k, base_ref[i * N_GROUPS + k].astype(F32), 0.0)
        slot = jnp.sum(grp.astype(F32) * rank, axis=-1, keepdims=True).astype(jnp.int32)
        pt = (lax.broadcasted_iota(jnp.int32, (tm, MOE_SLOTS), 1) == slot).astype(BF16)
        pt_ref[...] = pt
        gather = lambda x: lax.dot_general(pt, x, (((0,), (0,)), ((), ())), preferred_element_type=F32)
        xs_ref[0:MOE_SLOTS, :] = gather(xn).astype(BF16)
        gate_hi, gate_lo = _split(gate_ref[...])
        gs_ref[0:MOE_SLOTS, :] = gather(gate_hi) + gather(gate_lo)

    def window(w, carry):
        r0 = pl.multiple_of(base_ref[i * N_GROUPS + g] + w * MOE_WIN, MOE_ALIGN)
        x = xs_ref[pl.ds(r0, MOE_WIN), :]
        gate = gs_ref[pl.ds(r0, MOE_WIN), :]
        lane = lax.broadcasted_iota(jnp.int32, gate.shape, 1)
        y = None
        for j in range(EXPERTS_PER_GROUP):
            ge = jnp.sum(jnp.where(lane == g * EXPERTS_PER_GROUP + (j + N_GROUPS), gate, 0.0), axis=-1, keepdims=True)
            h1 = _dot(x, wg_ref[j])
            h3 = _dot(x, wu_ref[j])
            act = (h1 * _sigmoid(h1)) * h3 * ge
            yj = _dot(act.astype(BF16), wd_ref[j])
            y = yj if y is None else y + yj
        ys_ref[pl.ds(r0, MOE_WIN), :] = y.astype(BF16)
        return carry

    lax.fori_loop(0, nwin_ref[i * N_GROUPS + g], window, 0)

    @pl.when(g == N_GROUPS - 1)
    def _():
        o_ref[...] = h_ref[...] + _dot(pt_ref[...], ys_ref[0:MOE_SLOTS, :])


def _moe(h2, gain, w_group, b_group, w_expert, b_expert, w_gate, w_up, w_down):
    t = h2.shape[0]
    tm, rw = MOE_TM, ROUTER_LANES
    assert t % tm == 0 and MOE_SLOTS >= tm + N_GROUPS * (MOE_ALIGN - 1) and MOE_WIN % MOE_ALIGN == 0
    n_tiles = t // tm
    wr = jnp.zeros((D_MODEL, rw), F32).at[:, 0:N_GROUPS].set(w_group).at[:, N_GROUPS:N_GROUPS + N_EXPERTS].set(w_expert)
    br = jnp.zeros((1, rw), F32).at[0, 0:N_GROUPS].set(b_group).at[0, N_GROUPS:N_GROUPS + N_EXPERTS].set(b_expert)
    wr_hi = wr.astype(BF16)
    wr_lo = (wr - wr_hi.astype(F32)).astype(BF16)
    row = lambda width: pl.BlockSpec((tm, width), lambda i: (i, 0))
    gates, grp, cnt = pl.pallas_call(
        _moe_router_kernel,
        grid=(n_tiles,),
        in_specs=[row(D_MODEL), _full((1, D_MODEL)), _full((D_MODEL, rw)), _full((D_MODEL, rw)), _full((1, rw))],
        out_specs=[row(rw), row(rw), pl.BlockSpec((1, 8, rw), lambda i: (i, 0, 0))],
        out_shape=[jax.ShapeDtypeStruct((t, rw), F32), jax.ShapeDtypeStruct((t, rw), BF16),
                   jax.ShapeDtypeStruct((n_tiles, 8, rw), F32)],
        compiler_params=_params("arbitrary"),
        name="moe_router",
    )(h2, gain, wr_hi, wr_lo, br)
    n = cnt[:, 0, 0:N_GROUPS].astype(jnp.int32)
    padded = (n + (MOE_ALIGN - 1)) // MOE_ALIGN * MOE_ALIGN
    base = (jnp.cumsum(padded, axis=1) - padded).reshape(-1)
    nwin = ((padded + (MOE_WIN - 1)) // MOE_WIN).reshape(-1)
    tile = lambda width: pl.BlockSpec((tm, width), lambda i, g, *_: (i, 0))
    experts = lambda shape: pl.BlockSpec((EXPERTS_PER_GROUP,) + shape, lambda i, g, *_: (g, 0, 0))
    slots = MOE_SLOTS + MOE_WIN
    return pl.pallas_call(
        _moe_expert_kernel,
        grid_spec=pltpu.PrefetchScalarGridSpec(
            num_scalar_prefetch=2,
            grid=(n_tiles, N_GROUPS),
            in_specs=[tile(D_MODEL), pl.BlockSpec((1, D_MODEL), lambda i, g, *_: (0, 0)), tile(rw), tile(rw),
                      experts((D_MODEL, EXPERT_FF)), experts((D_MODEL, EXPERT_FF)), experts((EXPERT_FF, D_MODEL))],
            out_specs=tile(D_MODEL),
            scratch_shapes=[pltpu.VMEM((slots, D_MODEL), BF16), pltpu.VMEM((slots, rw), F32),
                            pltpu.VMEM((slots, D_MODEL), BF16), pltpu.VMEM((tm, MOE_SLOTS), BF16)]),
        out_shape=jax.ShapeDtypeStruct((t, D_MODEL), F32),
        compiler_params=_params("arbitrary", "arbitrary"),
        name="moe",
    )(base, nwin, h2, gain, gates, grp, w_gate.astype(BF16), w_up.astype(BF16), w_down.astype(BF16))


ODD_SPLITS = (MOBA_W, MOBA_W, MOBA_W, NSA_W, KV_W, KV_W, KV_W, KV_W, KV_W, KV_W, 128)
ODD_IN_PAD = sum(ODD_SPLITS)


def _head_rms(x, hsum, gain):
    w = x.shape[1]
    ss = jnp.concatenate([_dot_x2(x[:, o:o + hsum.shape[0]] * x[:, o:o + hsum.shape[0]], hsum)
                          for o in range(0, w, hsum.shape[0])], axis=1) if w > hsum.shape[0] else _dot_x2(x * x, hsum)
    return x * lax.rsqrt(ss * (1.0 / HEAD_DIM) + EPS) * gain


def _odd_in_kernel(tiles_per_seq, x_ref, g_ref, w_ref, hsum_ref, gq_ref, gk_ref, gnq_ref, gks_ref, gkw_ref,
                   qm_ref, kam_ref, kmean_ref, vam_ref, qs_ref, kvc_ref, kas_ref, vas_ref, kvw_ref, gt_ref):
    xn = _rms(x_ref[...], g_ref[...]).astype(BF16)
    offs = np.cumsum((0,) + ODD_SPLITS)
    col = lambda j: _dot(xn, w_ref[:, int(offs[j]):int(offs[j + 1])])
    head = lambda x, h: x[:, h * HEAD_DIM:(h + 1) * HEAD_DIM]
    hsum = hsum_ref[...]
    hsum128 = hsum_ref[0:128, 0:128]
    tm = x_ref.shape[0]
    pos = (pl.program_id(0) % tiles_per_seq) * tm + lax.broadcasted_iota(jnp.int32, (tm, HEAD_DIM), 0)
    lane = lax.broadcasted_iota(jnp.int32, (tm, HEAD_DIM), 1)
    ones_col = (lane == 0).astype(BF16)

    qm = _head_rms(col(0), hsum, gq_ref[...])
    km = _head_rms(col(1), hsum, gk_ref[...])
    for j in range(tm // MOBA_BLOCK):
        kmean_ref[0, j:j + 1, :] = jnp.mean(km[j * MOBA_BLOCK:(j + 1) * MOBA_BLOCK, :], axis=0, keepdims=True)
    km = km.astype(BF16)
    vm = col(2).astype(BF16)
    moba_id = (lane == pos // MOBA_BLOCK).astype(BF16)
    for h in range(MOBA_HEADS):
        qm_ref[0, h] = head(qm, h)
        kam_ref[0, h] = jnp.concatenate([head(km, h), moba_id], axis=1)
        vam_ref[0, h] = jnp.concatenate([head(vm, h), ones_col], axis=1)

    qd = (_head_rms(col(3), hsum, gnq_ref[...]) * (HEAD_DIM ** -0.5)).astype(BF16)
    for hk in range(NSA_KV_HEADS):
        for j in range(tm // NSA_TQ):
            for g in range(NSA_GROUP):
                qs_ref[0, hk, j, g * NSA_TQ:(g + 1) * NSA_TQ, :] = head(qd, hk * NSA_GROUP + g)[j * NSA_TQ:(j + 1) * NSA_TQ, :]
    kvc_ref[0] = col(4)
    kvc_ref[1] = col(5)
    ks = _head_rms(col(6), hsum128, gks_ref[...]).astype(BF16)
    vs = col(7).astype(BF16)
    kw = _head_rms(col(8), hsum128, gkw_ref[...]).astype(BF16)
    vw = col(9).astype(BF16)
    sel_id = (lane == (pos // SEL_BLOCK) % SEL_LANES).astype(BF16)
    for hk in range(NSA_KV_HEADS):
        kas_ref[0, hk] = jnp.concatenate([head(ks, hk), sel_id], axis=1)
        vas_ref[0, hk] = jnp.concatenate([head(vs, hk), ones_col], axis=1)
        kvw_ref[0, hk] = jnp.concatenate([head(kw, hk), head(vw, hk)], axis=1)
    gt_ref[...] = _sigmoid(col(10))


def _odd_in(h2, batch, seq, gain, w_in, moba_q_norm, moba_k_norm, nsa_q_norm, nsa_ksel_norm, nsa_kwin_norm, tm=512):
    t = h2.shape[0]
    assert MOBA_LANES == HEAD_DIM and SEL_LANES == HEAD_DIM and seq % tm == 0 and tm % MOBA_BLOCK == 0
    tps = seq // tm
    w = jnp.pad(w_in, ((0, 0), (0, ODD_IN_PAD - w_in.shape[1]))).astype(BF16)
    hsum = jnp.asarray(np.kron(np.eye(MOBA_W // HEAD_DIM), np.ones((HEAD_DIM, HEAD_DIM))), BF16)
    tile = lambda g, width: jnp.tile(g.astype(F32), width // HEAD_DIM).reshape(1, width)
    row = lambda width: pl.BlockSpec((tm, width), lambda i: (i, 0))
    heads = lambda n, width: pl.BlockSpec((1, n, tm, width), lambda i: (i // tps, 0, i % tps, 0))
    nmb = tm // MOBA_BLOCK
    nqt = tm // NSA_TQ
    rows = NSA_GROUP * NSA_TQ
    sds = jax.ShapeDtypeStruct
    out_specs = [heads(MOBA_HEADS, HEAD_DIM), heads(MOBA_HEADS, LANES), pl.BlockSpec((1, nmb, MOBA_W), lambda i: (i, 0, 0)),
                 heads(MOBA_HEADS, LANES),
                 pl.BlockSpec((1, NSA_KV_HEADS, nqt, rows, HEAD_DIM), lambda i: (i // tps, 0, i % tps, 0, 0)),
                 pl.BlockSpec((2, tm, KV_W), lambda i: (0, i, 0)),
                 heads(NSA_KV_HEADS, LANES), heads(NSA_KV_HEADS, LANES), heads(NSA_KV_HEADS, LANES),
                 row(128)]
    out_shape = [sds((batch, MOBA_HEADS, seq, HEAD_DIM), F32), sds((batch, MOBA_HEADS, seq, LANES), BF16),
                 sds((t // tm, nmb, MOBA_W), F32), sds((batch, MOBA_HEADS, seq, LANES), BF16),
                 sds((batch, NSA_KV_HEADS, seq // NSA_TQ, rows, HEAD_DIM), BF16),
                 sds((2, t, KV_W), F32), sds((batch, NSA_KV_HEADS, seq, LANES), BF16),
                 sds((batch, NSA_KV_HEADS, seq, LANES), BF16), sds((batch, NSA_KV_HEADS, seq, LANES), BF16),
                 sds((t, 128), F32)]
    return pl.pallas_call(
        functools.partial(_odd_in_kernel, tps),
        grid=(t // tm,),
        in_specs=[row(D_MODEL), _full((1, D_MODEL)), _full((D_MODEL, ODD_IN_PAD)), _full((MOBA_W, MOBA_W)),
                  _full((1, MOBA_W)), _full((1, MOBA_W)), _full((1, NSA_W)), _full((1, KV_W)), _full((1, KV_W))],
        out_specs=out_specs,
        out_shape=out_shape,
        compiler_params=_params("arbitrary"),
        name="odd_in",
    )(h2, gain, w, hsum, tile(moba_q_norm, MOBA_W), tile(moba_k_norm, MOBA_W), tile(nsa_q_norm, NSA_W),
      tile(nsa_ksel_norm, KV_W), tile(nsa_kwin_norm, KV_W))


def _compress_kernel(c_ref, w1_ref, w2_ref, pe_ref, g_ref, o_ref):
    kind = pl.program_id(0)
    n16 = c_ref.shape[1] // CMP_STRIDE
    half = CMP_STRIDE * HEAD_DIM
    peb = _dot(pe_ref[0], w1_ref[0])[0:1, :]
    xs = [c_ref[0, pl.ds(s, n16, stride=CMP_STRIDE), :].astype(BF16) for s in range(CMP_STRIDE)]
    for h in range(NSA_KV_HEADS):
        first = second = None
        for s in range(CMP_STRIDE):
            x = xs[s][:, h * HEAD_DIM:(h + 1) * HEAD_DIM]
            a = _dot(x, w1_ref[0, s * HEAD_DIM:(s + 1) * HEAD_DIM, :])
            b = _dot(x, w1_ref[0, half + s * HEAD_DIM:half + (s + 1) * HEAD_DIM, :])
            first = a if first is None else first + a
            second = b if second is None else second + b
        hid = _gelu(first + pltpu.roll(second, n16 - 1, 0) + peb)
        out = _dot(hid.astype(BF16), w2_ref[0])
        o_ref[0, 0, h] = jnp.where(kind == 0, _rms(out, g_ref[...]), out).astype(BF16)


def _compress(kvc, batch, seq, pe_k, w1_k, w2_k, pe_v, w1_v, w2_v, kcmp_norm):
    n16 = seq // CMP_STRIDE
    half = CMP_STRIDE * HEAD_DIM
    w1 = jnp.stack([w1_k, w1_v]).astype(BF16)
    w2 = jnp.stack([w2_k, w2_v]).astype(BF16)
    pe = jnp.stack([pe_k, pe_v]).reshape(2, 1, 2 * half)
    pe = jnp.broadcast_to(pe, (2, 8, 2 * half)).astype(BF16)
    return pl.pallas_call(
        _compress_kernel,
        grid=(2, batch),
        in_specs=[pl.BlockSpec((1, seq, KV_W), lambda k, b: (k, b, 0)),
                  pl.BlockSpec((1, 2 * half, CMP_HIDDEN), lambda k, b: (k, 0, 0)),
                  pl.BlockSpec((1, CMP_HIDDEN, HEAD_DIM), lambda k, b: (k, 0, 0)),
                  pl.BlockSpec((1, 8, 2 * half), lambda k, b: (k, 0, 0)),
                  _full((1, HEAD_DIM))],
        out_specs=pl.BlockSpec((1, 1, NSA_KV_HEADS, n16, HEAD_DIM), lambda k, b: (k, b, 0, 0, 0)),
        out_shape=jax.ShapeDtypeStruct((2, batch, NSA_KV_HEADS, n16, HEAD_DIM), BF16),
        compiler_params=_params("arbitrary", "arbitrary"),
        name="nsa_compress",
    )(kvc, w1, w2, pe, kcmp_norm.astype(F32).reshape(1, HEAD_DIM))


M_INIT = -1e30


LANES = 128


def _softmax_init(m_ref, acc_ref):
    m_ref[...] = jnp.full(m_ref.shape, M_INIT, F32)
    acc_ref[...] = jnp.zeros(acc_ref.shape, F32)


def _softmax_step(s, v_aug, m_ref, acc_ref):
    m_old = m_ref[...]
    m_new = jnp.maximum(m_old, jnp.max(s, axis=-1, keepdims=True))
    alpha = jnp.exp(m_old - m_new)
    p = jnp.exp(s - jnp.tile(m_new, (1, s.shape[1] // LANES)))
    acc_ref[...] = alpha * acc_ref[...] + _dot(p.astype(BF16), v_aug)
    m_ref[...] = m_new


def _past_keys_loop(n_keys, tile, step, riders=(), parts=2):
    n_full = n_keys // tile

    def body(j, carry):
        step(pl.multiple_of(j * tile, tile), tile)
        return carry

    extras = []
    for k, (with_tile, alone) in enumerate(riders):
        def both(with_tile=with_tile, k=k):
            out = with_tile()
            step(k * tile, tile)
            return out

        extras.append(lax.cond(n_full > k, both, alone))
    lax.fori_loop(len(riders), n_full, body, 0)
    rest = n_keys - n_full * tile
    start = pl.multiple_of(n_full * tile, tile)
    part = tile // parts
    n_parts = (rest + part - 1) // part
    for k in range(1, parts + 1):
        @pl.when(n_parts == k)
        def _(k=k):
            done = 0
            for size in (tile, tile // 2, tile // 4):
                if size % part == 0 and k * part - done >= size:
                    step(pl.multiple_of(start + done, part), size)
                    done += size
            assert done == k * part

    return extras


def _softmax_result(acc_ref):
    acc = acc_ref[...]
    return acc[:, 0:HEAD_DIM] * (1.0 / acc[:, HEAD_DIM:HEAD_DIM + 1])


def _pick_top(score, pos, width, k):
    sel = jnp.zeros(score.shape, jnp.bool_)
    for _ in range(k):
        m, idx = _first_max(score, pos, width, axis=0)
        hit = (pos == idx) & (m > -jnp.inf)
        sel = sel | hit
        score = jnp.where(pos == idx, -jnp.inf, score)
    return sel


MOBA_LANES = 64
MOBA_TK = 2048


MOBA_TQ = 4 * MOBA_BLOCK


def _moba_kernel(q_ref, ka_ref, va_ref, kmean_ref, o_ref, qa_ref, m_ref, acc_ref):
    i0 = pl.program_id(2) * (MOBA_TQ // MOBA_BLOCK)
    q = q_ref[0, 0]
    q_hi, q_lo = _split(q)
    km_hi, km_lo = _split(kmean_ref[0, 0])
    gate = _dot_nt(km_hi, q_hi) + (_dot_nt(km_lo, q_hi) + _dot_nt(km_hi, q_lo))
    blk = lax.broadcasted_iota(jnp.int32, gate.shape, 0)
    cur = i0 + lax.broadcasted_iota(jnp.int32, gate.shape, 1) // MOBA_BLOCK
    sel = _pick_top(jnp.where(blk < cur, gate, -jnp.inf), blk, gate.shape[0], MOBA_TOPK)
    qs = (q * (HEAD_DIM ** -0.5)).astype(BF16)
    past = jnp.where(sel & (blk < i0), 0.0, NEG).T[:, 0:MOBA_LANES]
    own = jnp.where((blk == cur) | (sel & (blk >= i0)), 0.0, NEG).T[:, 0:MOBA_LANES]
    qa_ref[0] = jnp.concatenate([qs, past.astype(BF16)], axis=1)
    qa_ref[1] = jnp.concatenate([qs, own.astype(BF16)], axis=1)
    _softmax_init(m_ref, acc_ref)

    def step(start, size):
        s = _dot_nt(qa_ref[0], ka_ref[0, 0, pl.ds(start, size), :])
        _softmax_step(s, va_ref[0, 0, pl.ds(start, size), :], m_ref, acc_ref)

    _past_keys_loop(i0 * MOBA_BLOCK, MOBA_TK, step)
    start = pl.multiple_of(i0 * MOBA_BLOCK, MOBA_TQ)
    s = _dot_nt(qa_ref[1], ka_ref[0, 0, pl.ds(start, MOBA_TQ), :])
    qpos = lax.broadcasted_iota(jnp.int32, s.shape, 0)
    kpos = lax.broadcasted_iota(jnp.int32, s.shape, 1)
    hidden = (qpos // MOBA_BLOCK == kpos // MOBA_BLOCK) & (kpos > qpos)
    _softmax_step(jnp.where(hidden, NEG, s), va_ref[0, 0, pl.ds(start, MOBA_TQ), :], m_ref, acc_ref)
    o_ref[0, 0] = _softmax_result(acc_ref).astype(BF16)


def _moba(qm, ka, kmean, va, batch, seq):
    nmb = seq // MOBA_BLOCK
    assert nmb <= MOBA_LANES and seq % MOBA_TK == 0 and seq % MOBA_TQ == 0
    kmean = kmean.reshape(batch, nmb, MOBA_HEADS, HEAD_DIM).transpose(0, 2, 1, 3)
    kmean = jnp.pad(kmean, ((0, 0), (0, 0), (0, LANES - nmb), (0, 0)))
    return pl.pallas_call(
        _moba_kernel,
        grid=(batch, MOBA_HEADS, seq // MOBA_TQ),
        in_specs=[pl.BlockSpec((1, 1, MOBA_TQ, HEAD_DIM), lambda b, h, i: (b, h, i, 0)),
                  pl.BlockSpec((1, 1, seq, LANES), lambda b, h, i: (b, h, 0, 0)),
                  pl.BlockSpec((1, 1, seq, LANES), lambda b, h, i: (b, h, 0, 0)),
                  pl.BlockSpec((1, 1, LANES, HEAD_DIM), lambda b, h, i: (b, h, 0, 0))],
        out_specs=pl.BlockSpec((1, 1, MOBA_TQ, HEAD_DIM), lambda b, h, i: (b, h, i, 0)),
        out_shape=jax.ShapeDtypeStruct((batch, MOBA_HEADS, seq, HEAD_DIM), BF16),
        scratch_shapes=[pltpu.VMEM((2, MOBA_TQ, LANES), BF16), pltpu.VMEM((MOBA_TQ, LANES), F32),
                        pltpu.VMEM((MOBA_TQ, LANES), F32)],
        compiler_params=_params("arbitrary", "arbitrary", "arbitrary"),
        name="moba",
    )(qm, ka, va, kmean)


NSA_TQ = 2 * SEL_BLOCK
NSA_TK = 2048
SEL_LANES = 64
SUPER_KEYS = SEL_LANES * SEL_BLOCK
CMP_WIDTH_STEP = 256


def _nsa_kernel(n_super, q_ref, kc_ref, vc_ref, ka_ref, va_ref, kvw_ref, gt_ref, e_ref, band_ref, o_ref,
                qa_ref, m_ref, acc_ref):
    qi = pl.program_id(2)
    tq = NSA_TQ
    s0 = qi * tq
    q = q_ref[0, 0, 0]

    def compressed(width):
        sc = _dot_nt(q, kc_ref[0, 0, 0:width, :])
        rq = lax.broadcasted_iota(jnp.int32, sc.shape, 0) & (tq - 1)
        n = lax.broadcasted_iota(jnp.int32, sc.shape, 1)
        vis = n * CMP_STRIDE + (CMP_BLOCK - 1) <= s0 + rq
        sc = jnp.where(vis, sc, NEG)
        pc = jnp.where(vis, jnp.exp(sc - jnp.max(sc, axis=-1, keepdims=True)), 0.0)
        pc = pc * (1.0 / jnp.maximum(jnp.sum(pc, axis=-1, keepdims=True), 1e-30))
        imp = pc[0:tq]
        for g in range(1, NSA_GROUP):
            imp = imp + pc[g * tq:(g + 1) * tq]
        return _dot(pc.astype(BF16), vc_ref[0, 0, 0:width, :]), _dot_x2(imp, band_ref[0:width, :])

    n16 = kc_ref.shape[2]
    widths = list(range(CMP_WIDTH_STEP, n16, CMP_WIDTH_STEP)) + [n16]
    o_c, pslc = lax.switch((s0 + tq - 1) // (CMP_WIDTH_STEP * CMP_STRIDE),
                           [functools.partial(compressed, wd) for wd in widths])

    wlen = WINDOW + tq

    def window(kvw, masked):
        sw = masked(_dot_nt(q, kvw[:, 0:HEAD_DIM]))
        pw = jnp.exp(sw - jnp.max(sw, axis=-1, keepdims=True))
        return _dot(pw.astype(BF16), kvw)[:, HEAD_DIM:2 * HEAD_DIM] * (1.0 / jnp.sum(pw, axis=-1, keepdims=True))

    def window_interior():
        def masked(sw):
            rq = lax.broadcasted_iota(jnp.int32, (sw.shape[0], tq), 0) & (tq - 1)
            c = lax.broadcasted_iota(jnp.int32, (sw.shape[0], tq), 1)
            return jnp.concatenate([jnp.where(c > rq, sw[:, 0:tq], NEG), sw[:, tq:WINDOW],
                                    jnp.where(c <= rq, sw[:, WINDOW:wlen], NEG)], axis=1)

        return window(kvw_ref[0, 0, pl.ds(pl.multiple_of(s0 - WINDOW, tq), wlen), :], masked)

    def window_start():
        def masked(sw):
            kabs = lax.broadcasted_iota(jnp.int32, sw.shape, 1)
            t = s0 + (lax.broadcasted_iota(jnp.int32, sw.shape, 0) & (tq - 1))
            return jnp.where((kabs <= t) & (kabs > t - WINDOW), sw, NEG)

        return window(kvw_ref[0, 0, 0:wlen, :], masked)

    pslc_t = pslc.T
    nb = pslc_t.shape[0]
    blk = lax.broadcasted_iota(jnp.int32, pslc_t.shape, 0)
    cur = (s0 + lax.broadcasted_iota(jnp.int32, pslc_t.shape, 1)) // SEL_BLOCK
    c0 = s0 // SEL_BLOCK
    elig = (blk >= 1) & (blk <= cur - 2)
    sel = _pick_top(jnp.where(elig, pslc_t, -jnp.inf), blk, nb, SEL_TOPK - 3)
    sel = sel | (blk == 0) | (blk == cur - 1)
    past = jnp.where(sel & (blk < c0), 0.0, NEG).T.astype(BF16)
    for st in range(n_super):
        b = past[:, st * SEL_LANES:(st + 1) * SEL_LANES]
        qa_ref[st] = jnp.concatenate([q, jnp.concatenate([b] * NSA_GROUP, axis=0)], axis=1)

    _softmax_init(m_ref, acc_ref)
    d0 = pl.multiple_of(s0, tq)
    s = _dot_nt(q, ka_ref[0, 0, pl.ds(d0, tq), :][:, 0:HEAD_DIM])
    qpos = lax.broadcasted_iota(jnp.int32, s.shape, 0) & (tq - 1)
    kpos = lax.broadcasted_iota(jnp.int32, s.shape, 1)
    _softmax_step(jnp.where(kpos <= qpos, s, NEG), va_ref[0, 0, pl.ds(d0, tq), :], m_ref, acc_ref)

    def step(start, size):
        s = _dot_nt(qa_ref[start // SUPER_KEYS], ka_ref[0, 0, pl.ds(start, size), :])
        _softmax_step(s, va_ref[0, 0, pl.ds(start, size), :], m_ref, acc_ref)

    (o_w,) = _past_keys_loop(s0, NSA_TK, step, parts=4, riders=(
        (window_interior, lambda: lax.cond(s0 >= WINDOW, window_interior, window_start)),))
    o_s = _softmax_result(acc_ref)

    w = NSA_GROUP * HEAD_DIM
    gexp = _dot_x2(gt_ref[...], e_ref[0])
    wide = lambda x: jnp.concatenate([x[g * tq:(g + 1) * tq] for g in range(NSA_GROUP)], axis=1)
    o_ref[...] = (gexp[:, 0:w] * wide(o_c) + gexp[:, w:2 * w] * wide(o_s) + gexp[:, 2 * w:3 * w] * wide(o_w)).astype(BF16)


def _nsa(qs, kcmp, vcmp, ka, va, kvw, gates, batch, seq):
    tq = NSA_TQ
    nq = seq // tq
    nb = seq // SEL_BLOCK
    n16 = seq // CMP_STRIDE
    assert seq % SUPER_KEYS == 0 and tq == 2 * SEL_BLOCK and WINDOW % tq == 0 and NSA_TK >= WINDOW
    n_super = seq // SUPER_KEYS
    rows = NSA_GROUP * tq
    e = np.zeros((NSA_KV_HEADS, 128, 3 * NSA_GROUP * HEAD_DIM), np.float32)
    for br in range(3):
        for hk in range(NSA_KV_HEADS):
            for g in range(NSA_GROUP):
                c = (br * NSA_GROUP + g) * HEAD_DIM
                e[hk, br * NSA_HEADS + hk * NSA_GROUP + g, c:c + HEAD_DIM] = 1.0
    nn, jj = np.arange(n16)[:, None], np.arange(nb)[None, :]
    band = ((nn >= 4 * jj - 1) & (nn <= 4 * jj + 3)).astype(np.float32)
    resident = lambda width: pl.BlockSpec((1, 1, seq, width), lambda b, h, i: (b, h, 0, 0))
    w = NSA_GROUP * HEAD_DIM
    return pl.pallas_call(
        functools.partial(_nsa_kernel, n_super),
        grid=(batch, NSA_KV_HEADS, nq),
        in_specs=[pl.BlockSpec((1, 1, 1, rows, HEAD_DIM), lambda b, h, i: (b, h, i, 0, 0)),
                  pl.BlockSpec((1, 1, n16, HEAD_DIM), lambda b, h, i: (b, h, 0, 0)),
                  pl.BlockSpec((1, 1, n16, HEAD_DIM), lambda b, h, i: (b, h, 0, 0)),
                  resident(LANES), resident(LANES), resident(LANES),
                  pl.BlockSpec((tq, 128), lambda b, h, i: (b * nq + i, 0)),
                  pl.BlockSpec((1, 128, 3 * w), lambda b, h, i: (h, 0, 0)),
                  pl.BlockSpec((n16, nb), lambda b, h, i: (0, 0))],
        out_specs=pl.BlockSpec((tq, w), lambda b, h, i: (b * nq + i, h)),
        out_shape=jax.ShapeDtypeStruct((batch * seq, NSA_W), BF16),
        scratch_shapes=[pltpu.VMEM((n_super, rows, LANES), BF16), pltpu.VMEM((rows, LANES), F32),
                        pltpu.VMEM((rows, LANES), F32)],
        compiler_params=_params("arbitrary", "arbitrary", "arbitrary"),
        name="nsa",
    )(qs, kcmp, vcmp, ka, va, kvw, gates, jnp.asarray(e, BF16), jnp.asarray(band, BF16))


def _odd_out_kernel(om_ref, on_ref, h_ref, w_ref, o_ref):
    acc = h_ref[...] + _dot(on_ref[...], w_ref[MOBA_W:D_MODEL, :])
    for h in range(MOBA_HEADS):
        acc = acc + _dot(om_ref[0, h], w_ref[h * HEAD_DIM:(h + 1) * HEAD_DIM, :])
    o_ref[...] = acc


def _odd_out(o_moba, o_nsa, h2, w_out, seq, tm=512):
    t = h2.shape[0]
    tps = seq // tm
    row = lambda w: pl.BlockSpec((tm, w), lambda i: (i, 0))
    return pl.pallas_call(
        _odd_out_kernel,
        grid=(t // tm,),
        in_specs=[pl.BlockSpec((1, MOBA_HEADS, tm, HEAD_DIM), lambda i: (i // tps, 0, i % tps, 0)),
                  row(NSA_W), row(D_MODEL), _full((D_MODEL, D_MODEL))],
        out_specs=row(D_MODEL),
        out_shape=jax.ShapeDtypeStruct((t, D_MODEL), F32),
        compiler_params=_params("arbitrary"),
        name="odd_out",
    )(o_moba, o_nsa, h2, w_out)


def _even_layer(h2, batch, seq, norm, w_in, w_out, lam_re, lam_im, log_dt, b_re, b_im, c_re, c_im, d, w_glu, conv_w, conv_b):
    u, ub, yb = _even_in(h2, norm.reshape(1, D_MODEL), w_in.astype(BF16), conv_w, conv_b.reshape(1, CONV_WIDTH), seq)
    ypre = _s5_mixer_pre(ub, batch, seq, lam_re, lam_im, log_dt, b_re, b_im, c_re, c_im)
    return _even_out(ypre, u, yb, h2, d.reshape(1, S5_WIDTH), w_glu.astype(BF16), w_out.astype(BF16))


def _odd_layer(h2, batch, seq, norm, w_in, w_out, moba_q_norm, moba_k_norm, nsa_q_norm, nsa_kcmp_norm, nsa_ksel_norm,
               nsa_kwin_norm, cmp_pe_k, cmp_w1_k, cmp_w2_k, cmp_pe_v, cmp_w1_v, cmp_w2_v):
    qm, kam, kmean, vam, qs, kvc, kas, vas, kvw, gates = _odd_in(
        h2, batch, seq, norm.reshape(1, D_MODEL), w_in, moba_q_norm, moba_k_norm, nsa_q_norm, nsa_ksel_norm, nsa_kwin_norm)
    cmp = _compress(kvc, batch, seq, cmp_pe_k, cmp_w1_k, cmp_w2_k, cmp_pe_v, cmp_w1_v, cmp_w2_v, nsa_kcmp_norm)
    o_moba = _moba(qm, kam, kmean, vam, batch, seq)
    o_nsa = _nsa(qs, cmp[0], cmp[1], kas, vas, kvw, gates, batch, seq)
    return _odd_out(o_moba, o_nsa, h2, w_out.astype(BF16), seq)


def kernel(x, ev_norm_mix, ev_w_in, ev_w_out, s5_lam_re, s5_lam_im, s5_log_dt, s5_b_re, s5_b_im, s5_c_re, s5_c_im, s5_d, s5_w_glu, conv_w, conv_b, od_norm_mix, od_w_in, od_w_out, moba_q_norm, moba_k_norm, nsa_q_norm, nsa_kcmp_norm, nsa_ksel_norm, nsa_kwin_norm, cmp_pe_k, cmp_w1_k, cmp_w2_k, cmp_pe_v, cmp_w1_v, cmp_w2_v, moe_norm, moe_w_group, moe_b_group, moe_w_expert, moe_b_expert, moe_w_gate, moe_w_up, moe_w_down):
    batch, seq, _ = x.shape
    depth = moe_norm.shape[0]
    h = x.reshape(batch * seq, D_MODEL)
    for layer in range(depth):
        i = layer // 2
        if layer % 2 == 0:
            h = _even_layer(h, batch, seq, ev_norm_mix[i], ev_w_in[i], ev_w_out[i], s5_lam_re[i], s5_lam_im[i], s5_log_dt[i],
                            s5_b_re[i], s5_b_im[i], s5_c_re[i], s5_c_im[i], s5_d[i], s5_w_glu[i], conv_w[i], conv_b[i])
        else:
            h = _odd_layer(h, batch, seq, od_norm_mix[i], od_w_in[i], od_w_out[i], moba_q_norm[i], moba_k_norm[i],
                           nsa_q_norm[i], nsa_kcmp_norm[i], nsa_ksel_norm[i], nsa_kwin_norm[i], cmp_pe_k[i], cmp_w1_k[i],
                           cmp_w2_k[i], cmp_pe_v[i], cmp_w1_v[i], cmp_w2_v[i])
        h = _moe(h, moe_norm[layer].reshape(1, D_MODEL), moe_w_group[layer], moe_b_group[layer], moe_w_expert[layer],
                 moe_b_expert[layer], moe_w_gate[layer], moe_w_up[layer], moe_w_down[layer])
    return h.reshape(batch, seq, D_MODEL)
```

```python
import functools
import math

import jax
import jax.numpy as jnp
import numpy as np
from jax import lax
from jax.experimental import pallas as pl
from jax.experimental.pallas import tpu as pltpu

D_MODEL = 1024
HEAD_DIM = 64
EPS = 1e-6
S5_WIDTH = 256
S5_GROUP = 16
S5_GROUPS = 16
S5_STATE = 64
S5_CHUNK = 16
S5_GROUP_BITS = 4
S5_STATE_BITS = 6
S5_ROW_TILE = 512
S5_COL_TILE = 512
S5_SCAN_TILE = 256
CONV_WIDTH = 768
CONV_K = 3
MOBA_HEADS = 4
NSA_HEADS = 12
NSA_KV_HEADS = 2
NSA_GROUP = 6
MOBA_W = 256
NSA_W = 768
KV_W = 128
MOBA_BLOCK = 256
MOBA_TOPK = 3
CMP_BLOCK = 32
CMP_STRIDE = 16
CMP_HIDDEN = 256
SEL_BLOCK = 64
SEL_TOPK = 8
WINDOW = 512
N_GROUPS = 4
EXPERTS_PER_GROUP = 4
N_EXPERTS = 16
EXPERT_FF = 256

LANES = 128
SUBLANES = 8
VMEM_LIMIT_BYTES = 56 * 1024 * 1024
ROW_TILE = 512
GATE_LANES = LANES
NEG = -float(2 ** 100)
F32 = jnp.float32
BF16 = jnp.bfloat16


def _params(*semantics):
    return pltpu.CompilerParams(dimension_semantics=semantics, vmem_limit_bytes=VMEM_LIMIT_BYTES)


def _dot(a, b):
    return jnp.dot(a, b, preferred_element_type=F32)


def _dot_nt(a, b):
    return lax.dot_general(a, b, (((1,), (1,)), ((), ())), preferred_element_type=F32)


def _split(x):
    hi = x.astype(BF16)
    lo = (x - hi.astype(F32)).astype(BF16)
    return hi, lo


def _dot_x2(x, w):
    hi, lo = _split(x)
    return _dot(hi, w) + _dot(lo, w)


def _dot_x3(x, w_hi, w_lo):
    hi, lo = _split(x)
    return _dot(hi, w_hi) + (_dot(hi, w_lo) + _dot(lo, w_hi))


def _rms(x, gain):
    return x * lax.rsqrt(jnp.mean(x * x, axis=-1, keepdims=True) + EPS) * gain


def _gelu(x):
    return 0.5 * x * (1.0 + jnp.tanh(math.sqrt(2.0 / math.pi) * (x + 0.044715 * (x * x * x))))


def _sigmoid(x):
    return 1.0 / (1.0 + jnp.exp(-x))


def _full(shape):
    n = len(shape)
    return pl.BlockSpec(shape, lambda *_: (0,) * n)


def _even_in_kernel(tiles_per_seq, x_ref, g_ref, w_ref, cw_ref, cb_ref, u_ref, ub_ref, yb_ref, carry_ref):
    i = pl.program_id(0)
    xn = _rms(x_ref[...], g_ref[...]).astype(BF16)
    u = _dot(xn, w_ref[:, 0:S5_WIDTH])
    u_ref[...] = u
    ub_ref[...] = u.astype(BF16)
    o = S5_WIDTH
    xc = _dot(xn, w_ref[:, o:o + CONV_WIDTH])
    gb = _dot(xn, w_ref[:, o + CONV_WIDTH:o + 2 * CONV_WIDTH])
    gc = _dot(xn, w_ref[:, o + 2 * CONV_WIDTH:o + 3 * CONV_WIDTH])
    z = gc * xc
    tm = z.shape[0]

    @pl.when(i % tiles_per_seq == 0)
    def _():
        carry_ref[...] = jnp.zeros_like(carry_ref)

    row = lax.broadcasted_iota(jnp.int32, z.shape, 0)
    prev1 = carry_ref[SUBLANES - 1:SUBLANES, :]
    prev2 = carry_ref[SUBLANES - 2:SUBLANES - 1, :]
    z1 = jnp.where(row == 0, prev1, pltpu.roll(z, 1, 0))
    z2 = jnp.where(row == 0, prev2, jnp.where(row == 1, prev1, pltpu.roll(z, 2, 0)))
    y = cw_ref[0:1, :] * z2 + cw_ref[1:2, :] * z1 + cw_ref[2:3, :] * z + cb_ref[...]
    yb_ref[...] = (gb * y).astype(BF16)
    carry_ref[...] = z[tm - SUBLANES:tm, :]


def _even_in(x2, gain, w_in, conv_w, conv_b, seq, tm=ROW_TILE):
    t = x2.shape[0]
    n_in = w_in.shape[1]
    return pl.pallas_call(
        functools.partial(_even_in_kernel, seq // tm),
        grid=(t // tm,),
        in_specs=[pl.BlockSpec((tm, D_MODEL), lambda i: (i, 0)), _full((1, D_MODEL)),
                  _full((D_MODEL, n_in)), _full((CONV_K, CONV_WIDTH)), _full((1, CONV_WIDTH))],
        out_specs=[pl.BlockSpec((tm, S5_WIDTH), lambda i: (i, 0)), pl.BlockSpec((tm, S5_WIDTH), lambda i: (i, 0)),
                   pl.BlockSpec((tm, CONV_WIDTH), lambda i: (i, 0))],
        out_shape=[jax.ShapeDtypeStruct((t, S5_WIDTH), F32), jax.ShapeDtypeStruct((t, S5_WIDTH), BF16),
                   jax.ShapeDtypeStruct((t, CONV_WIDTH), BF16)],
        scratch_shapes=[pltpu.VMEM((SUBLANES, CONV_WIDTH), F32)],
        compiler_params=_params("arbitrary"),
        name="even_in",
    )(x2, gain, w_in, conv_w, conv_b)


def _s5_weights(lam_re, lam_im, log_dt, b_re, b_im, c_re, c_im):
    g, p, hg, ck = S5_GROUPS, S5_STATE, S5_GROUP, S5_CHUNK
    lr, li = lam_re.astype(F32), lam_im.astype(F32)
    dt = jnp.exp(log_dt.astype(F32))[:, None]
    mag = jnp.exp(lr * dt)
    a_re, a_im = mag * jnp.cos(li * dt), mag * jnp.sin(li * dt)
    den = lr * lr + li * li
    f_re = ((a_re - 1.0) * lr + a_im * li) / den
    f_im = (a_im * lr - (a_re - 1.0) * li) / den
    br, bi = b_re.astype(F32), b_im.astype(F32)
    bb_re = f_re[..., None] * br - f_im[..., None] * bi
    bb_im = f_re[..., None] * bi + f_im[..., None] * br
    pw_re, pw_im = [jnp.ones_like(a_re)], [jnp.zeros_like(a_im)]
    for _ in range(ck):
        r, m = pw_re[-1], pw_im[-1]
        pw_re.append(r * a_re - m * a_im)
        pw_im.append(r * a_im + m * a_re)
    pw_re, pw_im = jnp.stack(pw_re), jnp.stack(pw_im)
    cr, ci = c_re.astype(F32), c_im.astype(F32)
    rev_re, rev_im = pw_re[ck - 1::-1][:ck], pw_im[ck - 1::-1][:ck]
    ws_re = rev_re[:, :, :, None] * bb_re[None] - rev_im[:, :, :, None] * bb_im[None]
    ws_im = rev_re[:, :, :, None] * bb_im[None] + rev_im[:, :, :, None] * bb_re[None]
    ca_re = cr[None] * pw_re[1:, :, None, :] - ci[None] * pw_im[1:, :, None, :]
    ca_im = cr[None] * pw_im[1:, :, None, :] + ci[None] * pw_re[1:, :, None, :]
    cb_re = jnp.einsum('ghp,kgp,gpj->kghj', cr, pw_re[:ck], bb_re) - jnp.einsum('ghp,kgp,gpj->kghj', cr, pw_im[:ck], bb_im) \
        - jnp.einsum('ghp,kgp,gpj->kghj', ci, pw_re[:ck], bb_im) - jnp.einsum('ghp,kgp,gpj->kghj', ci, pw_im[:ck], bb_re)
    lag = np.arange(ck)[None, :] - np.arange(ck)[:, None]
    tz = cb_re[np.clip(lag, 0, ck - 1)]
    tz = jnp.where((lag >= 0)[:, :, None, None, None], tz, 0.0)
    cw = ck * g * hg
    ws = jnp.stack([ws_re, ws_im]).transpose(1, 2, 4, 0, 3).reshape(cw, 2 * p)
    wc = jnp.stack([ca_re, -ca_im]).transpose(0, 2, 4, 1, 3).reshape(2 * g * p, ck * hg)
    tzc = tz.transpose(0, 2, 4, 1, 3).reshape(cw, ck * hg)
    return ws.astype(BF16), wc.astype(BF16), tzc.astype(BF16), pw_re[ck].reshape(1, g * p), pw_im[ck].reshape(1, g * p)


def _group_expand(compact, expand, row_shift, col_shift, col0):
    full = _dot(compact, expand)
    row = lax.broadcasted_iota(jnp.int32, full.shape, 0)
    col = col0 + lax.broadcasted_iota(jnp.int32, full.shape, 1)
    same = ((row >> row_shift) & (S5_GROUPS - 1)) == ((col >> col_shift) & (S5_GROUPS - 1))
    return jnp.where(same, full, 0.0).astype(BF16)


def _s5_state_kernel(u_ref, ws_ref, e_ref, s_ref, w_scr):
    @pl.when(pl.program_id(1) == 0)
    def _():
        w_scr[...] = _group_expand(ws_ref[...], e_ref[...], S5_GROUP_BITS, S5_STATE_BITS,
                                   pl.program_id(0) * w_scr.shape[1])

    s_ref[...] = _dot(u_ref[...], w_scr[...])


def _s5_scan_kernel(s_ref, are_ref, aim_ref, xprev_ref, st_ref):
    @pl.when(pl.program_id(0) == 0)
    def _():
        st_ref[...] = jnp.zeros_like(st_ref)

    a_re, a_im = are_ref[...], aim_ref[...]
    nb, n = s_ref.shape[0], s_ref.shape[1]
    half = a_re.shape[1]

    def body(c, carry):
        out = []
        for b in range(nb):
            xr, xi = carry[2 * b], carry[2 * b + 1]
            xprev_ref[b, pl.ds(c, 1), 0:half] = xr
            xprev_ref[b, pl.ds(c, 1), half:2 * half] = xi
            s = s_ref[b, pl.ds(c, 1), :]
            out += [a_re * xr - a_im * xi + s[:, 0:half], a_re * xi + a_im * xr + s[:, half:2 * half]]
        return tuple(out)

    init = tuple(st_ref[b:b + 1, o:o + half] for b in range(nb) for o in (0, half))
    final = lax.fori_loop(0, n, body, init, unroll=4)
    for b in range(nb):
        st_ref[b:b + 1, 0:half] = final[2 * b]
        st_ref[b:b + 1, half:2 * half] = final[2 * b + 1]


def _s5_out_kernel(u_ref, xp_ref, tz_ref, wc_ref, e_ref, y_ref, tz_scr, wc_scr):
    @pl.when(pl.program_id(1) == 0)
    def _():
        col0 = pl.program_id(0) * tz_scr.shape[1]
        tz_scr[...] = _group_expand(tz_ref[...], e_ref[...], S5_GROUP_BITS, S5_GROUP_BITS, col0)
        wc_scr[...] = _group_expand(wc_ref[...], e_ref[...], S5_STATE_BITS, S5_GROUP_BITS, col0)

    y_ref[...] = _dot(u_ref[...], tz_scr[...]) + _dot(xp_ref[...].astype(BF16), wc_scr[...])


def _expand_matrix(outer, inner):
    e = np.zeros((outer, inner, outer, S5_GROUPS, inner), np.float32)
    for x in range(outer):
        for y in range(inner):
            e[x, y, x, :, y] = 1.0
    return jnp.asarray(e.reshape(outer * inner, outer * S5_GROUPS * inner), BF16)


def _s5_mixer_pre(ub, batch, seq, lam_re, lam_im, log_dt, b_re, b_im, c_re, c_im):
    ws, wc, tz, a16_re, a16_im = _s5_weights(lam_re, lam_im, log_dt, b_re, b_im, c_re, c_im)
    nc = seq // S5_CHUNK
    rows = batch * nc
    cw = S5_CHUNK * S5_WIDTH
    sw = 2 * S5_GROUPS * S5_STATE
    assert S5_GROUP == 1 << S5_GROUP_BITS and S5_STATE == 1 << S5_STATE_BITS
    tr = min(rows, S5_ROW_TILE)
    tn = S5_COL_TILE
    uc = ub.reshape(rows, cw)
    e_state = _expand_matrix(2, S5_STATE)
    e_out = _expand_matrix(S5_CHUNK, S5_GROUP)
    s = pl.pallas_call(
        _s5_state_kernel,
        grid=(sw // tn, rows // tr),
        in_specs=[pl.BlockSpec((tr, cw), lambda j, i: (i, 0)), _full(ws.shape),
                  pl.BlockSpec((e_state.shape[0], tn), lambda j, i: (0, j))],
        out_specs=pl.BlockSpec((tr, tn), lambda j, i: (i, j)),
        out_shape=jax.ShapeDtypeStruct((rows, sw), F32),
        scratch_shapes=[pltpu.VMEM((cw, tn), BF16)],
        compiler_params=_params("arbitrary", "arbitrary"),
        name="s5_state",
    )(uc, ws, e_state)
    tc = min(nc, S5_SCAN_TILE)
    xprev = pl.pallas_call(
        _s5_scan_kernel,
        grid=(nc // tc,),
        in_specs=[pl.BlockSpec((batch, tc, sw), lambda i: (0, i, 0)), _full((1, sw // 2)), _full((1, sw // 2))],
        out_specs=pl.BlockSpec((batch, tc, sw), lambda i: (0, i, 0)),
        out_shape=jax.ShapeDtypeStruct((batch, nc, sw), F32),
        scratch_shapes=[pltpu.VMEM((batch, sw), F32)],
        compiler_params=_params("arbitrary"),
        name="s5_scan",
    )(s.reshape(batch, nc, sw), a16_re, a16_im)
    y = pl.pallas_call(
        _s5_out_kernel,
        grid=(cw // tn, rows // tr),
        in_specs=[pl.BlockSpec((tr, cw), lambda j, i: (i, 0)), pl.BlockSpec((tr, sw), lambda j, i: (i, 0)),
                  _full(tz.shape), _full(wc.shape), pl.BlockSpec((e_out.shape[0], tn), lambda j, i: (0, j))],
        out_specs=pl.BlockSpec((tr, tn), lambda j, i: (i, j)),
        out_shape=jax.ShapeDtypeStruct((rows, cw), F32),
        scratch_shapes=[pltpu.VMEM((cw, tn), BF16), pltpu.VMEM((sw, tn), BF16)],
        compiler_params=_params("arbitrary", "arbitrary"),
        name="s5_out",
    )(uc, xprev.reshape(rows, sw), tz, wc, e_out)
    return y.reshape(batch * seq, S5_WIDTH)


def _even_out_kernel(ypre_ref, u_ref, yb_ref, x_ref, d_ref, wglu_ref, wout_ref, o_ref):
    y = _gelu(ypre_ref[...] + d_ref[...] * u_ref[...])
    y = y * _sigmoid(_dot(y.astype(BF16), wglu_ref[...]))
    o_ref[...] = (x_ref[...] + _dot(y.astype(BF16), wout_ref[0:S5_WIDTH, :])
                  + _dot(yb_ref[...], wout_ref[S5_WIDTH:D_MODEL, :]))


def _even_out(ypre, u, yb, x2, d, w_glu, w_out, tm=ROW_TILE):
    t = x2.shape[0]
    row = lambda w: pl.BlockSpec((tm, w), lambda i: (i, 0))
    return pl.pallas_call(
        _even_out_kernel,
        grid=(t // tm,),
        in_specs=[row(S5_WIDTH), row(S5_WIDTH), row(CONV_WIDTH), row(D_MODEL), _full((1, S5_WIDTH)),
                  _full((S5_WIDTH, S5_WIDTH)), _full((D_MODEL, D_MODEL))],
        out_specs=row(D_MODEL),
        out_shape=jax.ShapeDtypeStruct((t, D_MODEL), F32),
        compiler_params=_params("arbitrary"),
        name="even_out",
    )(ypre, u, yb, x2, d, w_glu, w_out)


def _first_max(v, pos, width, axis=-1):
    m = jnp.max(v, axis=axis, keepdims=True)
    idx = jnp.min(jnp.where(v == m, pos, width), axis=axis, keepdims=True)
    return m, idx


MOE_TM = 1024
MOE_ALIGN = 16
MOE_SLOTS = 1152
MOE_WIN = 288
ROUTER_LANES = LANES


def _moe_router_kernel(h_ref, g_ref, wr_hi_ref, wr_lo_ref, br_ref, gate_ref, grp_ref, cnt_ref):
    xn = _rms(h_ref[...], g_ref[...])
    logits = _dot_x3(xn, wr_hi_ref[...], wr_lo_ref[...]) + br_ref[...]
    lane = lax.broadcasted_iota(jnp.int32, logits.shape, 1)
    width = logits.shape[1]
    is_g = lane < N_GROUPS
    gl = jnp.where(is_g, logits, -jnp.inf)
    gm, gi = _first_max(gl, lane, width)
    gw = 1.0 / jnp.sum(jnp.where(is_g, jnp.exp(gl - gm), 0.0), axis=-1, keepdims=True)
    lo = N_GROUPS + gi * EXPERTS_PER_GROUP
    in_grp = (lane >= lo) & (lane < lo + EXPERTS_PER_GROUP)
    el = jnp.where(in_grp, logits, -jnp.inf)
    m1, i1 = _first_max(el, lane, width)
    m2, i2 = _first_max(jnp.where(lane == i1, -jnp.inf, el), lane, width)
    p2 = jnp.exp(m2 - m1)
    w1 = gw / (1.0 + p2)
    w2 = gw * p2 / (1.0 + p2)
    gate_ref[...] = jnp.where(lane == i1, w1, 0.0) + jnp.where(lane == i2, w2, 0.0)
    grp = (lane == gi).astype(F32)
    grp_ref[...] = grp.astype(BF16)
    cnt_ref[0] = jnp.broadcast_to(jnp.sum(grp, axis=0, keepdims=True), cnt_ref.shape[1:])


def _moe_expert_kernel(base_ref, nwin_ref, h_ref, g_ref, gate_ref, grp_ref, wg_ref, wu_ref, wd_ref, o_ref,
                       xs_ref, gs_ref, ys_ref, pt_ref):
    i, g = pl.program_id(0), pl.program_id(1)
    tm = h_ref.shape[0]

    @pl.when((i == 0) & (g == 0))
    def _():
        xs_ref[...] = jnp.zeros_like(xs_ref)
        gs_ref[...] = jnp.zeros_like(gs_ref)
        ys_ref[...] = jnp.zeros_like(ys_ref)

    @pl.when(g == 0)
    def _():
        xn = _rms(h_ref[...], g_ref[...]).astype(BF16)
        grp = grp_ref[...]
        earlier = (lax.broadcasted_iota(jnp.int32, (tm, tm), 0) > lax.broadcasted_iota(jnp.int32, (tm, tm), 1)).astype(BF16)
        rank = _dot(earlier, grp)
        lane = lax.broadcasted_iota(jnp.int32, rank.shape, 1)
        for k in range(N_GROUPS):
            rank = rank + jnp.where(lane == k, base_ref[i * N_GROUPS + k].astype(F32), 0.0)
        slot = jnp.sum(grp.astype(F32) * rank, axis=-1, keepdims=True).astype(jnp.int32)
        pt = (lax.broadcasted_iota(jnp.int32, (tm, MOE_SLOTS), 1) == slot).astype(BF16)
        pt_ref[...] = pt
        gather = lambda x: lax.dot_general(pt, x, (((0,), (0,)), ((), ())), preferred_element_type=F32)
        xs_ref[0:MOE_SLOTS, :] = gather(xn).astype(BF16)
        gate_hi, gate_lo = _split(gate_ref[...])
        gs_ref[0:MOE_SLOTS, :] = gather(gate_hi) + gather(gate_lo)

    def window(w, carry):
        r0 = pl.multiple_of(base_ref[i * N_GROUPS + g] + w * MOE_WIN, MOE_ALIGN)
        x = xs_ref[pl.ds(r0, MOE_WIN), :]
        gate = gs_ref[pl.ds(r0, MOE_WIN), :]
        lane = lax.broadcasted_iota(jnp.int32, gate.shape, 1)
        y = None
        for j in range(EXPERTS_PER_GROUP):
            ge = jnp.sum(jnp.where(lane == g * EXPERTS_PER_GROUP + (j + N_GROUPS), gate, 0.0), axis=-1, keepdims=True)
            h1 = _dot(x, wg_ref[j])
            h3 = _dot(x, wu_ref[j])
            act = (h1 * _sigmoid(h1)) * h3 * ge
            yj = _dot(act.astype(BF16), wd_ref[j])
            y = yj if y is None else y + yj
        ys_ref[pl.ds(r0, MOE_WIN), :] = y.astype(BF16)
        return carry

    lax.fori_loop(0, nwin_ref[i * N_GROUPS + g], window, 0)

    @pl.when(g == N_GROUPS - 1)
    def _():
        o_ref[...] = h_ref[...] + _dot(pt_ref[...], ys_ref[0:MOE_SLOTS, :])


def _moe(h2, gain, w_group, b_group, w_expert, b_expert, w_gate, w_up, w_down):
    t = h2.shape[0]
    tm, rw = MOE_TM, ROUTER_LANES
    assert t % tm == 0 and MOE_SLOTS >= tm + N_GROUPS * (MOE_ALIGN - 1) and MOE_WIN % MOE_ALIGN == 0
    n_tiles = t // tm
    wr = jnp.zeros((D_MODEL, rw), F32).at[:, 0:N_GROUPS].set(w_group).at[:, N_GROUPS:N_GROUPS + N_EXPERTS].set(w_expert)
    br = jnp.zeros((1, rw), F32).at[0, 0:N_GROUPS].set(b_group).at[0, N_GROUPS:N_GROUPS + N_EXPERTS].set(b_expert)
    wr_hi = wr.astype(BF16)
    wr_lo = (wr - wr_hi.astype(F32)).astype(BF16)
    row = lambda width: pl.BlockSpec((tm, width), lambda i: (i, 0))
    gates, grp, cnt = pl.pallas_call(
        _moe_router_kernel,
        grid=(n_tiles,),
        in_specs=[row(D_MODEL), _full((1, D_MODEL)), _full((D_MODEL, rw)), _full((D_MODEL, rw)), _full((1, rw))],
        out_specs=[row(rw), row(rw), pl.BlockSpec((1, SUBLANES, rw), lambda i: (i, 0, 0))],
        out_shape=[jax.ShapeDtypeStruct((t, rw), F32), jax.ShapeDtypeStruct((t, rw), BF16),
                   jax.ShapeDtypeStruct((n_tiles, SUBLANES, rw), F32)],
        compiler_params=_params("arbitrary"),
        name="moe_router",
    )(h2, gain, wr_hi, wr_lo, br)
    n = cnt[:, 0, 0:N_GROUPS].astype(jnp.int32)
    padded = (n + (MOE_ALIGN - 1)) // MOE_ALIGN * MOE_ALIGN
    base = (jnp.cumsum(padded, axis=1) - padded).reshape(-1)
    nwin = ((padded + (MOE_WIN - 1)) // MOE_WIN).reshape(-1)
    tile = lambda width: pl.BlockSpec((tm, width), lambda i, g, *_: (i, 0))
    experts = lambda shape: pl.BlockSpec((EXPERTS_PER_GROUP,) + shape, lambda i, g, *_: (g, 0, 0))
    slots = MOE_SLOTS + MOE_WIN
    return pl.pallas_call(
        _moe_expert_kernel,
        grid_spec=pltpu.PrefetchScalarGridSpec(
            num_scalar_prefetch=2,
            grid=(n_tiles, N_GROUPS),
            in_specs=[tile(D_MODEL), pl.BlockSpec((1, D_MODEL), lambda i, g, *_: (0, 0)), tile(rw), tile(rw),
                      experts((D_MODEL, EXPERT_FF)), experts((D_MODEL, EXPERT_FF)), experts((EXPERT_FF, D_MODEL))],
            out_specs=tile(D_MODEL),
            scratch_shapes=[pltpu.VMEM((slots, D_MODEL), BF16), pltpu.VMEM((slots, rw), F32),
                            pltpu.VMEM((slots, D_MODEL), BF16), pltpu.VMEM((tm, MOE_SLOTS), BF16)]),
        out_shape=jax.ShapeDtypeStruct((t, D_MODEL), F32),
        compiler_params=_params("arbitrary", "arbitrary"),
        name="moe",
    )(base, nwin, h2, gain, gates, grp, w_gate.astype(BF16), w_up.astype(BF16), w_down.astype(BF16))


ODD_SPLITS = (MOBA_W, MOBA_W, MOBA_W, NSA_W, KV_W, KV_W, KV_W, KV_W, KV_W, KV_W, GATE_LANES)
ODD_IN_PAD = sum(ODD_SPLITS)


def _head_rms(x, hsum, gain):
    w = x.shape[1]
    ss = jnp.concatenate([_dot_x2(x[:, o:o + hsum.shape[0]] * x[:, o:o + hsum.shape[0]], hsum)
                          for o in range(0, w, hsum.shape[0])], axis=1) if w > hsum.shape[0] else _dot_x2(x * x, hsum)
    return x * lax.rsqrt(ss * (1.0 / HEAD_DIM) + EPS) * gain


def _odd_in_kernel(tiles_per_seq, x_ref, g_ref, w_ref, hsum_ref, gq_ref, gk_ref, gnq_ref, gks_ref, gkw_ref,
                   qm_ref, kam_ref, kmean_ref, vam_ref, qs_ref, kvc_ref, kas_ref, vas_ref, kvw_ref, gt_ref):
    xn = _rms(x_ref[...], g_ref[...]).astype(BF16)
    offs = np.cumsum((0,) + ODD_SPLITS)
    col = lambda j: _dot(xn, w_ref[:, int(offs[j]):int(offs[j + 1])])
    head = lambda x, h: x[:, h * HEAD_DIM:(h + 1) * HEAD_DIM]
    hsum = hsum_ref[...]
    hsum128 = hsum_ref[0:KV_W, 0:KV_W]
    tm = x_ref.shape[0]
    pos = (pl.program_id(0) % tiles_per_seq) * tm + lax.broadcasted_iota(jnp.int32, (tm, HEAD_DIM), 0)
    lane = lax.broadcasted_iota(jnp.int32, (tm, HEAD_DIM), 1)
    ones_col = (lane == 0).astype(BF16)

    qm = _head_rms(col(0), hsum, gq_ref[...])
    km = _head_rms(col(1), hsum, gk_ref[...])
    for j in range(tm // MOBA_BLOCK):
        kmean_ref[0, j:j + 1, :] = jnp.mean(km[j * MOBA_BLOCK:(j + 1) * MOBA_BLOCK, :], axis=0, keepdims=True)
    km = km.astype(BF16)
    vm = col(2).astype(BF16)
    moba_id = (lane == pos // MOBA_BLOCK).astype(BF16)
    for h in range(MOBA_HEADS):
        qm_ref[0, h] = head(qm, h)
        kam_ref[0, h] = jnp.concatenate([head(km, h), moba_id], axis=1)
        vam_ref[0, h] = jnp.concatenate([head(vm, h), ones_col], axis=1)

    qd = (_head_rms(col(3), hsum, gnq_ref[...]) * (HEAD_DIM ** -0.5)).astype(BF16)
    for hk in range(NSA_KV_HEADS):
        for j in range(tm // NSA_TQ):
            for g in range(NSA_GROUP):
                qs_ref[0, hk, j, g * NSA_TQ:(g + 1) * NSA_TQ, :] = head(qd, hk * NSA_GROUP + g)[j * NSA_TQ:(j + 1) * NSA_TQ, :]
    kvc_ref[0] = col(4)
    kvc_ref[1] = col(5)
    ks = _head_rms(col(6), hsum128, gks_ref[...]).astype(BF16)
    vs = col(7).astype(BF16)
    kw = _head_rms(col(8), hsum128, gkw_ref[...]).astype(BF16)
    vw = col(9).astype(BF16)
    sel_id = (lane == (pos // SEL_BLOCK) % SEL_LANES).astype(BF16)
    for hk in range(NSA_KV_HEADS):
        kas_ref[0, hk] = jnp.concatenate([head(ks, hk), sel_id], axis=1)
        vas_ref[0, hk] = jnp.concatenate([head(vs, hk), ones_col], axis=1)
        kvw_ref[0, hk] = jnp.concatenate([head(kw, hk), head(vw, hk)], axis=1)
    gt_ref[...] = _sigmoid(col(10))


def _odd_in(h2, batch, seq, gain, w_in, moba_q_norm, moba_k_norm, nsa_q_norm, nsa_ksel_norm, nsa_kwin_norm, tm=ROW_TILE):
    t = h2.shape[0]
    assert MOBA_LANES == HEAD_DIM and SEL_LANES == HEAD_DIM and seq % tm == 0 and tm % MOBA_BLOCK == 0
    tps = seq // tm
    w = jnp.pad(w_in, ((0, 0), (0, ODD_IN_PAD - w_in.shape[1]))).astype(BF16)
    hsum = jnp.asarray(np.kron(np.eye(MOBA_W // HEAD_DIM), np.ones((HEAD_DIM, HEAD_DIM))), BF16)
    tile = lambda g, width: jnp.tile(g.astype(F32), width // HEAD_DIM).reshape(1, width)
    row = lambda width: pl.BlockSpec((tm, width), lambda i: (i, 0))
    heads = lambda n, width: pl.BlockSpec((1, n, tm, width), lambda i: (i // tps, 0, i % tps, 0))
    nmb = tm // MOBA_BLOCK
    nqt = tm // NSA_TQ
    rows = NSA_GROUP * NSA_TQ
    sds = jax.ShapeDtypeStruct
    out_specs = [heads(MOBA_HEADS, HEAD_DIM), heads(MOBA_HEADS, LANES), pl.BlockSpec((1, nmb, MOBA_W), lambda i: (i, 0, 0)),
                 heads(MOBA_HEADS, LANES),
                 pl.BlockSpec((1, NSA_KV_HEADS, nqt, rows, HEAD_DIM), lambda i: (i // tps, 0, i % tps, 0, 0)),
                 pl.BlockSpec((2, tm, KV_W), lambda i: (0, i, 0)),
                 heads(NSA_KV_HEADS, LANES), heads(NSA_KV_HEADS, LANES), heads(NSA_KV_HEADS, LANES),
                 row(GATE_LANES)]
    out_shape = [sds((batch, MOBA_HEADS, seq, HEAD_DIM), F32), sds((batch, MOBA_HEADS, seq, LANES), BF16),
                 sds((t // tm, nmb, MOBA_W), F32), sds((batch, MOBA_HEADS, seq, LANES), BF16),
                 sds((batch, NSA_KV_HEADS, seq // NSA_TQ, rows, HEAD_DIM), BF16),
                 sds((2, t, KV_W), F32), sds((batch, NSA_KV_HEADS, seq, LANES), BF16),
                 sds((batch, NSA_KV_HEADS, seq, LANES), BF16), sds((batch, NSA_KV_HEADS, seq, LANES), BF16),
                 sds((t, GATE_LANES), F32)]
    return pl.pallas_call(
        functools.partial(_odd_in_kernel, tps),
        grid=(t // tm,),
        in_specs=[row(D_MODEL), _full((1, D_MODEL)), _full((D_MODEL, ODD_IN_PAD)), _full((MOBA_W, MOBA_W)),
                  _full((1, MOBA_W)), _full((1, MOBA_W)), _full((1, NSA_W)), _full((1, KV_W)), _full((1, KV_W))],
        out_specs=out_specs,
        out_shape=out_shape,
        compiler_params=_params("arbitrary"),
        name="odd_in",
    )(h2, gain, w, hsum, tile(moba_q_norm, MOBA_W), tile(moba_k_norm, MOBA_W), tile(nsa_q_norm, NSA_W),
      tile(nsa_ksel_norm, KV_W), tile(nsa_kwin_norm, KV_W))


def _compress_kernel(c_ref, w1_ref, w2_ref, pe_ref, g_ref, o_ref):
    kind = pl.program_id(0)
    n16 = c_ref.shape[1] // CMP_STRIDE
    half = CMP_STRIDE * HEAD_DIM
    peb = _dot(pe_ref[0], w1_ref[0])[0:1, :]
    xs = [c_ref[0, pl.ds(s, n16, stride=CMP_STRIDE), :].astype(BF16) for s in range(CMP_STRIDE)]
    for h in range(NSA_KV_HEADS):
        first = second = None
        for s in range(CMP_STRIDE):
            x = xs[s][:, h * HEAD_DIM:(h + 1) * HEAD_DIM]
            a = _dot(x, w1_ref[0, s * HEAD_DIM:(s + 1) * HEAD_DIM, :])
            b = _dot(x, w1_ref[0, half + s * HEAD_DIM:half + (s + 1) * HEAD_DIM, :])
            first = a if first is None else first + a
            second = b if second is None else second + b
        hid = _gelu(first + pltpu.roll(second, n16 - 1, 0) + peb)
        out = _dot(hid.astype(BF16), w2_ref[0])
        o_ref[0, 0, h] = jnp.where(kind == 0, _rms(out, g_ref[...]), out).astype(BF16)


def _compress(kvc, batch, seq, pe_k, w1_k, w2_k, pe_v, w1_v, w2_v, kcmp_norm):
    n16 = seq // CMP_STRIDE
    half = CMP_STRIDE * HEAD_DIM
    w1 = jnp.stack([w1_k, w1_v]).astype(BF16)
    w2 = jnp.stack([w2_k, w2_v]).astype(BF16)
    pe = jnp.stack([pe_k, pe_v]).reshape(2, 1, 2 * half)
    pe = jnp.broadcast_to(pe, (2, SUBLANES, 2 * half)).astype(BF16)
    return pl.pallas_call(
        _compress_kernel,
        grid=(2, batch),
        in_specs=[pl.BlockSpec((1, seq, KV_W), lambda k, b: (k, b, 0)),
                  pl.BlockSpec((1, 2 * half, CMP_HIDDEN), lambda k, b: (k, 0, 0)),
                  pl.BlockSpec((1, CMP_HIDDEN, HEAD_DIM), lambda k, b: (k, 0, 0)),
                  pl.BlockSpec((1, SUBLANES, 2 * half), lambda k, b: (k, 0, 0)),
                  _full((1, HEAD_DIM))],
        out_specs=pl.BlockSpec((1, 1, NSA_KV_HEADS, n16, HEAD_DIM), lambda k, b: (k, b, 0, 0, 0)),
        out_shape=jax.ShapeDtypeStruct((2, batch, NSA_KV_HEADS, n16, HEAD_DIM), BF16),
        compiler_params=_params("arbitrary", "arbitrary"),
        name="nsa_compress",
    )(kvc, w1, w2, pe, kcmp_norm.astype(F32).reshape(1, HEAD_DIM))


M_INIT = -1e30


def _softmax_init(m_ref, acc_ref):
    m_ref[...] = jnp.full(m_ref.shape, M_INIT, F32)
    acc_ref[...] = jnp.zeros(acc_ref.shape, F32)


def _softmax_step(s, v_aug, m_ref, acc_ref):
    m_old = m_ref[...]
    m_new = jnp.maximum(m_old, jnp.max(s, axis=-1, keepdims=True))
    alpha = jnp.exp(m_old - m_new)
    p = jnp.exp(s - jnp.tile(m_new, (1, s.shape[1] // LANES)))
    acc_ref[...] = alpha * acc_ref[...] + _dot(p.astype(BF16), v_aug)
    m_ref[...] = m_new


def _past_keys_loop(n_keys, tile, step, riders=(), parts=2):
    n_full = n_keys // tile

    def body(j, carry):
        step(pl.multiple_of(j * tile, tile), tile)
        return carry

    extras = []
    for k, (with_tile, alone) in enumerate(riders):
        def both(with_tile=with_tile, k=k):
            out = with_tile()
            step(k * tile, tile)
            return out

        extras.append(lax.cond(n_full > k, both, alone))
    lax.fori_loop(len(riders), n_full, body, 0)
    rest = n_keys - n_full * tile
    start = pl.multiple_of(n_full * tile, tile)
    part = tile // parts
    n_parts = (rest + part - 1) // part
    for k in range(1, parts + 1):
        @pl.when(n_parts == k)
        def _(k=k):
            done = 0
            for size in (tile, tile // 2, tile // 4):
                if size % part == 0 and k * part - done >= size:
                    step(pl.multiple_of(start + done, part), size)
                    done += size
            assert done == k * part

    return extras


def _softmax_result(acc_ref):
    acc = acc_ref[...]
    return acc[:, 0:HEAD_DIM] * (1.0 / acc[:, HEAD_DIM:HEAD_DIM + 1])


def _pick_top(score, pos, width, k):
    sel = jnp.zeros(score.shape, jnp.bool_)
    for _ in range(k):
        m, idx = _first_max(score, pos, width, axis=0)
        hit = (pos == idx) & (m > -jnp.inf)
        sel = sel | hit
        score = jnp.where(pos == idx, -jnp.inf, score)
    return sel


MOBA_LANES = 64
MOBA_TK = 2048


MOBA_TQ = 4 * MOBA_BLOCK


def _moba_kernel(q_ref, ka_ref, va_ref, kmean_ref, o_ref, qa_ref, m_ref, acc_ref):
    i0 = pl.program_id(2) * (MOBA_TQ // MOBA_BLOCK)
    q = q_ref[0, 0]
    q_hi, q_lo = _split(q)
    km_hi, km_lo = _split(kmean_ref[0, 0])
    gate = _dot_nt(km_hi, q_hi) + (_dot_nt(km_lo, q_hi) + _dot_nt(km_hi, q_lo))
    blk = lax.broadcasted_iota(jnp.int32, gate.shape, 0)
    cur = i0 + lax.broadcasted_iota(jnp.int32, gate.shape, 1) // MOBA_BLOCK
    sel = _pick_top(jnp.where(blk < cur, gate, -jnp.inf), blk, gate.shape[0], MOBA_TOPK)
    qs = (q * (HEAD_DIM ** -0.5)).astype(BF16)
    past = jnp.where(sel & (blk < i0), 0.0, NEG).T[:, 0:MOBA_LANES]
    own = jnp.where((blk == cur) | (sel & (blk >= i0)), 0.0, NEG).T[:, 0:MOBA_LANES]
    qa_ref[0] = jnp.concatenate([qs, past.astype(BF16)], axis=1)
    qa_ref[1] = jnp.concatenate([qs, own.astype(BF16)], axis=1)
    _softmax_init(m_ref, acc_ref)

    def step(start, size):
        s = _dot_nt(qa_ref[0], ka_ref[0, 0, pl.ds(start, size), :])
        _softmax_step(s, va_ref[0, 0, pl.ds(start, size), :], m_ref, acc_ref)

    _past_keys_loop(i0 * MOBA_BLOCK, MOBA_TK, step)
    start = pl.multiple_of(i0 * MOBA_BLOCK, MOBA_TQ)
    s = _dot_nt(qa_ref[1], ka_ref[0, 0, pl.ds(start, MOBA_TQ), :])
    qpos = lax.broadcasted_iota(jnp.int32, s.shape, 0)
    kpos = lax.broadcasted_iota(jnp.int32, s.shape, 1)
    hidden = (qpos // MOBA_BLOCK == kpos // MOBA_BLOCK) & (kpos > qpos)
    _softmax_step(jnp.where(hidden, NEG, s), va_ref[0, 0, pl.ds(start, MOBA_TQ), :], m_ref, acc_ref)
    o_ref[0, 0] = _softmax_result(acc_ref).astype(BF16)


def _moba(qm, ka, kmean, va, batch, seq):
    nmb = seq // MOBA_BLOCK
    assert nmb <= MOBA_LANES and seq % MOBA_TK == 0 and seq % MOBA_TQ == 0
    kmean = kmean.reshape(batch, nmb, MOBA_HEADS, HEAD_DIM).transpose(0, 2, 1, 3)
    kmean = jnp.pad(kmean, ((0, 0), (0, 0), (0, LANES - nmb), (0, 0)))
    return pl.pallas_call(
        _moba_kernel,
        grid=(batch, MOBA_HEADS, seq // MOBA_TQ),
        in_specs=[pl.BlockSpec((1, 1, MOBA_TQ, HEAD_DIM), lambda b, h, i: (b, h, i, 0)),
                  pl.BlockSpec((1, 1, seq, LANES), lambda b, h, i: (b, h, 0, 0)),
                  pl.BlockSpec((1, 1, seq, LANES), lambda b, h, i: (b, h, 0, 0)),
                  pl.BlockSpec((1, 1, LANES, HEAD_DIM), lambda b, h, i: (b, h, 0, 0))],
        out_specs=pl.BlockSpec((1, 1, MOBA_TQ, HEAD_DIM), lambda b, h, i: (b, h, i, 0)),
        out_shape=jax.ShapeDtypeStruct((batch, MOBA_HEADS, seq, HEAD_DIM), BF16),
        scratch_shapes=[pltpu.VMEM((2, MOBA_TQ, LANES), BF16), pltpu.VMEM((MOBA_TQ, LANES), F32),
                        pltpu.VMEM((MOBA_TQ, LANES), F32)],
        compiler_params=_params("arbitrary", "arbitrary", "arbitrary"),
        name="moba",
    )(qm, ka, va, kmean)


NSA_TQ = 2 * SEL_BLOCK
NSA_TK = 2048
SEL_LANES = 64
SUPER_KEYS = SEL_LANES * SEL_BLOCK
CMP_WIDTH_STEP = 256


def _nsa_kernel(n_super, q_ref, kc_ref, vc_ref, ka_ref, va_ref, kvw_ref, gt_ref, e_ref, band_ref, o_ref,
                qa_ref, m_ref, acc_ref):
    qi = pl.program_id(2)
    tq = NSA_TQ
    s0 = qi * tq
    q = q_ref[0, 0, 0]

    def compressed(width):
        sc = _dot_nt(q, kc_ref[0, 0, 0:width, :])
        rq = lax.broadcasted_iota(jnp.int32, sc.shape, 0) & (tq - 1)
        n = lax.broadcasted_iota(jnp.int32, sc.shape, 1)
        vis = n * CMP_STRIDE + (CMP_BLOCK - 1) <= s0 + rq
        sc = jnp.where(vis, sc, NEG)
        pc = jnp.where(vis, jnp.exp(sc - jnp.max(sc, axis=-1, keepdims=True)), 0.0)
        pc = pc * (1.0 / jnp.maximum(jnp.sum(pc, axis=-1, keepdims=True), 1e-30))
        imp = pc[0:tq]
        for g in range(1, NSA_GROUP):
            imp = imp + pc[g * tq:(g + 1) * tq]
        return _dot(pc.astype(BF16), vc_ref[0, 0, 0:width, :]), _dot_x2(imp, band_ref[0:width, :])

    n16 = kc_ref.shape[2]
    widths = list(range(CMP_WIDTH_STEP, n16, CMP_WIDTH_STEP)) + [n16]
    o_c, pslc = lax.switch((s0 + tq - 1) // (CMP_WIDTH_STEP * CMP_STRIDE),
                           [functools.partial(compressed, wd) for wd in widths])

    wlen = WINDOW + tq

    def window(kvw, masked):
        sw = masked(_dot_nt(q, kvw[:, 0:HEAD_DIM]))
        pw = jnp.exp(sw - jnp.max(sw, axis=-1, keepdims=True))
        return _dot(pw.astype(BF16), kvw)[:, HEAD_DIM:2 * HEAD_DIM] * (1.0 / jnp.sum(pw, axis=-1, keepdims=True))

    def window_interior():
        def masked(sw):
            rq = lax.broadcasted_iota(jnp.int32, (sw.shape[0], tq), 0) & (tq - 1)
            c = lax.broadcasted_iota(jnp.int32, (sw.shape[0], tq), 1)
            return jnp.concatenate([jnp.where(c > rq, sw[:, 0:tq], NEG), sw[:, tq:WINDOW],
                                    jnp.where(c <= rq, sw[:, WINDOW:wlen], NEG)], axis=1)

        return window(kvw_ref[0, 0, pl.ds(pl.multiple_of(s0 - WINDOW, tq), wlen), :], masked)

    def window_start():
        def masked(sw):
            kabs = lax.broadcasted_iota(jnp.int32, sw.shape, 1)
            t = s0 + (lax.broadcasted_iota(jnp.int32, sw.shape, 0) & (tq - 1))
            return jnp.where((kabs <= t) & (kabs > t - WINDOW), sw, NEG)

        return window(kvw_ref[0, 0, 0:wlen, :], masked)

    pslc_t = pslc.T
    nb = pslc_t.shape[0]
    blk = lax.broadcasted_iota(jnp.int32, pslc_t.shape, 0)
    cur = (s0 + lax.broadcasted_iota(jnp.int32, pslc_t.shape, 1)) // SEL_BLOCK
    c0 = s0 // SEL_BLOCK
    elig = (blk >= 1) & (blk <= cur - 2)
    sel = _pick_top(jnp.where(elig, pslc_t, -jnp.inf), blk, nb, SEL_TOPK - 3)
    sel = sel | (blk == 0) | (blk == cur - 1)
    past = jnp.where(sel & (blk < c0), 0.0, NEG).T.astype(BF16)
    for st in range(n_super):
        b = past[:, st * SEL_LANES:(st + 1) * SEL_LANES]
        qa_ref[st] = jnp.concatenate([q, jnp.concatenate([b] * NSA_GROUP, axis=0)], axis=1)

    _softmax_init(m_ref, acc_ref)
    d0 = pl.multiple_of(s0, tq)
    s = _dot_nt(q, ka_ref[0, 0, pl.ds(d0, tq), :][:, 0:HEAD_DIM])
    qpos = lax.broadcasted_iota(jnp.int32, s.shape, 0) & (tq - 1)
    kpos = lax.broadcasted_iota(jnp.int32, s.shape, 1)
    _softmax_step(jnp.where(kpos <= qpos, s, NEG), va_ref[0, 0, pl.ds(d0, tq), :], m_ref, acc_ref)

    def step(start, size):
        s = _dot_nt(qa_ref[start // SUPER_KEYS], ka_ref[0, 0, pl.ds(start, size), :])
        _softmax_step(s, va_ref[0, 0, pl.ds(start, size), :], m_ref, acc_ref)

    (o_w,) = _past_keys_loop(s0, NSA_TK, step, parts=4, riders=(
        (window_interior, lambda: lax.cond(s0 >= WINDOW, window_interior, window_start)),))
    o_s = _softmax_result(acc_ref)

    w = NSA_GROUP * HEAD_DIM
    gexp = _dot_x2(gt_ref[...], e_ref[0])
    wide = lambda x: jnp.concatenate([x[g * tq:(g + 1) * tq] for g in range(NSA_GROUP)], axis=1)
    o_ref[...] = (gexp[:, 0:w] * wide(o_c) + gexp[:, w:2 * w] * wide(o_s) + gexp[:, 2 * w:3 * w] * wide(o_w)).astype(BF16)


def _nsa(qs, kcmp, vcmp, ka, va, kvw, gates, batch, seq):
    tq = NSA_TQ
    nq = seq // tq
    nb = seq // SEL_BLOCK
    n16 = seq // CMP_STRIDE
    assert seq % SUPER_KEYS == 0 and tq == 2 * SEL_BLOCK and WINDOW % tq == 0 and NSA_TK >= WINDOW
    n_super = seq // SUPER_KEYS
    rows = NSA_GROUP * tq
    e = np.zeros((NSA_KV_HEADS, GATE_LANES, 3 * NSA_GROUP * HEAD_DIM), np.float32)
    for br in range(3):
        for hk in range(NSA_KV_HEADS):
            for g in range(NSA_GROUP):
                c = (br * NSA_GROUP + g) * HEAD_DIM
                e[hk, br * NSA_HEADS + hk * NSA_GROUP + g, c:c + HEAD_DIM] = 1.0
    nn, jj = np.arange(n16)[:, None], np.arange(nb)[None, :]
    band = ((nn >= 4 * jj - 1) & (nn <= 4 * jj + 3)).astype(np.float32)
    resident = lambda width: pl.BlockSpec((1, 1, seq, width), lambda b, h, i: (b, h, 0, 0))
    w = NSA_GROUP * HEAD_DIM
    return pl.pallas_call(
        functools.partial(_nsa_kernel, n_super),
        grid=(batch, NSA_KV_HEADS, nq),
        in_specs=[pl.BlockSpec((1, 1, 1, rows, HEAD_DIM), lambda b, h, i: (b, h, i, 0, 0)),
                  pl.BlockSpec((1, 1, n16, HEAD_DIM), lambda b, h, i: (b, h, 0, 0)),
                  pl.BlockSpec((1, 1, n16, HEAD_DIM), lambda b, h, i: (b, h, 0, 0)),
                  resident(LANES), resident(LANES), resident(LANES),
                  pl.BlockSpec((tq, GATE_LANES), lambda b, h, i: (b * nq + i, 0)),
                  pl.BlockSpec((1, GATE_LANES, 3 * w), lambda b, h, i: (h, 0, 0)),
                  pl.BlockSpec((n16, nb), lambda b, h, i: (0, 0))],
        out_specs=pl.BlockSpec((tq, w), lambda b, h, i: (b * nq + i, h)),
        out_shape=jax.ShapeDtypeStruct((batch * seq, NSA_W), BF16),
        scratch_shapes=[pltpu.VMEM((n_super, rows, LANES), BF16), pltpu.VMEM((rows, LANES), F32),
                        pltpu.VMEM((rows, LANES), F32)],
        compiler_params=_params("arbitrary", "arbitrary", "arbitrary"),
        name="nsa",
    )(qs, kcmp, vcmp, ka, va, kvw, gates, jnp.asarray(e, BF16), jnp.asarray(band, BF16))


def _odd_out_kernel(om_ref, on_ref, h_ref, w_ref, o_ref):
    acc = h_ref[...] + _dot(on_ref[...], w_ref[MOBA_W:D_MODEL, :])
    for h in range(MOBA_HEADS):
        acc = acc + _dot(om_ref[0, h], w_ref[h * HEAD_DIM:(h + 1) * HEAD_DIM, :])
    o_ref[...] = acc


def _odd_out(o_moba, o_nsa, h2, w_out, seq, tm=ROW_TILE):
    t = h2.shape[0]
    tps = seq // tm
    row = lambda w: pl.BlockSpec((tm, w), lambda i: (i, 0))
    return pl.pallas_call(
        _odd_out_kernel,
        grid=(t // tm,),
        in_specs=[pl.BlockSpec((1, MOBA_HEADS, tm, HEAD_DIM), lambda i: (i // tps, 0, i % tps, 0)),
                  row(NSA_W), row(D_MODEL), _full((D_MODEL, D_MODEL))],
        out_specs=row(D_MODEL),
        out_shape=jax.ShapeDtypeStruct((t, D_MODEL), F32),
        compiler_params=_params("arbitrary"),
        name="odd_out",
    )(o_moba, o_nsa, h2, w_out)


def _even_layer(h2, batch, seq, norm, w_in, w_out, lam_re, lam_im, log_dt, b_re, b_im, c_re, c_im, d, w_glu, conv_w, conv_b):
    u, ub, yb = _even_in(h2, norm.reshape(1, D_MODEL), w_in.astype(BF16), conv_w, conv_b.reshape(1, CONV_WIDTH), seq)
    ypre = _s5_mixer_pre(ub, batch, seq, lam_re, lam_im, log_dt, b_re, b_im, c_re, c_im)
    return _even_out(ypre, u, yb, h2, d.reshape(1, S5_WIDTH), w_glu.astype(BF16), w_out.astype(BF16))


def _odd_layer(h2, batch, seq, norm, w_in, w_out, moba_q_norm, moba_k_norm, nsa_q_norm, nsa_kcmp_norm, nsa_ksel_norm,
               nsa_kwin_norm, cmp_pe_k, cmp_w1_k, cmp_w2_k, cmp_pe_v, cmp_w1_v, cmp_w2_v):
    qm, kam, kmean, vam, qs, kvc, kas, vas, kvw, gates = _odd_in(
        h2, batch, seq, norm.reshape(1, D_MODEL), w_in, moba_q_norm, moba_k_norm, nsa_q_norm, nsa_ksel_norm, nsa_kwin_norm)
    cmp = _compress(kvc, batch, seq, cmp_pe_k, cmp_w1_k, cmp_w2_k, cmp_pe_v, cmp_w1_v, cmp_w2_v, nsa_kcmp_norm)
    o_moba = _moba(qm, kam, kmean, vam, batch, seq)
    o_nsa = _nsa(qs, cmp[0], cmp[1], kas, vas, kvw, gates, batch, seq)
    return _odd_out(o_moba, o_nsa, h2, w_out.astype(BF16), seq)


def kernel(x, ev_norm_mix, ev_w_in, ev_w_out, s5_lam_re, s5_lam_im, s5_log_dt, s5_b_re, s5_b_im, s5_c_re, s5_c_im, s5_d, s5_w_glu, conv_w, conv_b, od_norm_mix, od_w_in, od_w_out, moba_q_norm, moba_k_norm, nsa_q_norm, nsa_kcmp_norm, nsa_ksel_norm, nsa_kwin_norm, cmp_pe_k, cmp_w1_k, cmp_w2_k, cmp_pe_v, cmp_w1_v, cmp_w2_v, moe_norm, moe_w_group, moe_b_group, moe_w_expert, moe_b_expert, moe_w_gate, moe_w_up, moe_w_down):
    batch, seq, _ = x.shape
    depth = moe_norm.shape[0]
    h = x.reshape(batch * seq, D_MODEL)
    for layer in range(depth):
        i = layer // 2
        if layer % 2 == 0:
            h = _even_layer(h, batch, seq, ev_norm_mix[i], ev_w_in[i], ev_w_out[i], s5_lam_re[i], s5_lam_im[i], s5_log_dt[i],
                            s5_b_re[i], s5_b_im[i], s5_c_re[i], s5_c_im[i], s5_d[i], s5_w_glu[i], conv_w[i], conv_b[i])
        else:
            h = _odd_layer(h, batch, seq, od_norm_mix[i], od_w_in[i], od_w_out[i], moba_q_norm[i], moba_k_norm[i],
                           nsa_q_norm[i], nsa_kcmp_norm[i], nsa_ksel_norm[i], nsa_kwin_norm[i], cmp_pe_k[i], cmp_w1_k[i],
                           cmp_w2_k[i], cmp_pe_v[i], cmp_w1_v[i], cmp_w2_v[i])
        h = _moe(h, moe_norm[layer].reshape(1, D_MODEL), moe_w_group[layer], moe_b_group[layer], moe_w_expert[layer],
                 moe_b_expert[layer], moe_w_gate[layer], moe_w_up[layer], moe_w_down[layer])
    return h.reshape(batch, seq, D_MODEL)
```

```python
import functools
import math

import jax
import jax.numpy as jnp
import numpy as np
from jax import lax
from jax.experimental import pallas as pl
from jax.experimental.pallas import tpu as pltpu

D_MODEL = 1024
HEAD_DIM = 64
EPS = 1e-6
S5_WIDTH = 256
S5_GROUP = 16
S5_GROUPS = 16
S5_STATE = 64
S5_CHUNK = 16
S5_GROUP_BITS = 4
S5_STATE_BITS = 6
S5_ROW_TILE = 512
S5_COL_TILE = 512
S5_SCAN_TILE = 256
CONV_WIDTH = 768
CONV_K = 3
MOBA_HEADS = 4
NSA_HEADS = 12
NSA_KV_HEADS = 2
NSA_GROUP = 6
MOBA_W = 256
NSA_W = 768
KV_W = 128
MOBA_BLOCK = 256
MOBA_TOPK = 3
CMP_BLOCK = 32
CMP_STRIDE = 16
CMP_HIDDEN = 256
SEL_BLOCK = 64
SEL_TOPK = 8
WINDOW = 512
N_GROUPS = 4
EXPERTS_PER_GROUP = 4
N_EXPERTS = 16
EXPERT_FF = 256

LANES = 128
SUBLANES = 8
VMEM_LIMIT_BYTES = 56 * 1024 * 1024
ROW_TILE = 512
GATE_LANES = LANES
NEG = -float(2 ** 100)
F32 = jnp.float32
BF16 = jnp.bfloat16


def _params(*semantics):
    return pltpu.CompilerParams(dimension_semantics=semantics, vmem_limit_bytes=VMEM_LIMIT_BYTES)


def _dot(a, b):
    return jnp.dot(a, b, preferred_element_type=F32)


def _dot_nt(a, b):
    return lax.dot_general(a, b, (((1,), (1,)), ((), ())), preferred_element_type=F32)


def _split(x):
    hi = x.astype(BF16)
    lo = (x - hi.astype(F32)).astype(BF16)
    return hi, lo


def _dot_x2(x, w):
    hi, lo = _split(x)
    return _dot(hi, w) + _dot(lo, w)


def _dot_x3(x, w_hi, w_lo):
    hi, lo = _split(x)
    return _dot(hi, w_hi) + (_dot(hi, w_lo) + _dot(lo, w_hi))


def _rms(x, gain):
    return x * lax.rsqrt(jnp.mean(x * x, axis=-1, keepdims=True) + EPS) * gain


def _gelu(x):
    return 0.5 * x * (1.0 + jnp.tanh(math.sqrt(2.0 / math.pi) * (x + 0.044715 * (x * x * x))))


def _sigmoid(x):
    return 1.0 / (1.0 + jnp.exp(-x))


def _full(shape):
    n = len(shape)
    return pl.BlockSpec(shape, lambda *_: (0,) * n)


def _even_in_kernel(tiles_per_seq, x_ref, g_ref, w_ref, cw_ref, cb_ref, u_ref, ub_ref, yb_ref, carry_ref):
    i = pl.program_id(0)
    xn = _rms(x_ref[...], g_ref[...]).astype(BF16)
    u = _dot(xn, w_ref[:, 0:S5_WIDTH])
    u_ref[...] = u
    ub_ref[...] = u.astype(BF16)
    o = S5_WIDTH
    xc = _dot(xn, w_ref[:, o:o + CONV_WIDTH])
    gb = _dot(xn, w_ref[:, o + CONV_WIDTH:o + 2 * CONV_WIDTH])
    gc = _dot(xn, w_ref[:, o + 2 * CONV_WIDTH:o + 3 * CONV_WIDTH])
    z = gc * xc
    tm = z.shape[0]

    @pl.when(i % tiles_per_seq == 0)
    def _():
        carry_ref[...] = jnp.zeros_like(carry_ref)

    row = lax.broadcasted_iota(jnp.int32, z.shape, 0)
    prev1 = carry_ref[SUBLANES - 1:SUBLANES, :]
    prev2 = carry_ref[SUBLANES - 2:SUBLANES - 1, :]
    z1 = jnp.where(row == 0, prev1, pltpu.roll(z, 1, 0))
    z2 = jnp.where(row == 0, prev2, jnp.where(row == 1, prev1, pltpu.roll(z, 2, 0)))
    y = cw_ref[0:1, :] * z2 + cw_ref[1:2, :] * z1 + cw_ref[2:3, :] * z + cb_ref[...]
    yb_ref[...] = (gb * y).astype(BF16)
    carry_ref[...] = z[tm - SUBLANES:tm, :]


def _even_in(x2, gain, w_in, conv_w, conv_b, seq, tm=ROW_TILE):
    t = x2.shape[0]
    n_in = w_in.shape[1]
    return pl.pallas_call(
        functools.partial(_even_in_kernel, seq // tm),
        grid=(t // tm,),
        in_specs=[pl.BlockSpec((tm, D_MODEL), lambda i: (i, 0)), _full((1, D_MODEL)),
                  _full((D_MODEL, n_in)), _full((CONV_K, CONV_WIDTH)), _full((1, CONV_WIDTH))],
        out_specs=[pl.BlockSpec((tm, S5_WIDTH), lambda i: (i, 0)), pl.BlockSpec((tm, S5_WIDTH), lambda i: (i, 0)),
                   pl.BlockSpec((tm, CONV_WIDTH), lambda i: (i, 0))],
        out_shape=[jax.ShapeDtypeStruct((t, S5_WIDTH), F32), jax.ShapeDtypeStruct((t, S5_WIDTH), BF16),
                   jax.ShapeDtypeStruct((t, CONV_WIDTH), BF16)],
        scratch_shapes=[pltpu.VMEM((SUBLANES, CONV_WIDTH), F32)],
        compiler_params=_params("arbitrary"),
        name="even_in",
    )(x2, gain, w_in, conv_w, conv_b)


def _s5_weights(lam_re, lam_im, log_dt, b_re, b_im, c_re, c_im):
    g, p, hg, ck = S5_GROUPS, S5_STATE, S5_GROUP, S5_CHUNK
    lr, li = lam_re.astype(F32), lam_im.astype(F32)
    dt = jnp.exp(log_dt.astype(F32))[:, None]
    mag = jnp.exp(lr * dt)
    a_re, a_im = mag * jnp.cos(li * dt), mag * jnp.sin(li * dt)
    den = lr * lr + li * li
    f_re = ((a_re - 1.0) * lr + a_im * li) / den
    f_im = (a_im * lr - (a_re - 1.0) * li) / den
    br, bi = b_re.astype(F32), b_im.astype(F32)
    bb_re = f_re[..., None] * br - f_im[..., None] * bi
    bb_im = f_re[..., None] * bi + f_im[..., None] * br
    pw_re, pw_im = [jnp.ones_like(a_re)], [jnp.zeros_like(a_im)]
    for _ in range(ck):
        r, m = pw_re[-1], pw_im[-1]
        pw_re.append(r * a_re - m * a_im)
        pw_im.append(r * a_im + m * a_re)
    pw_re, pw_im = jnp.stack(pw_re), jnp.stack(pw_im)
    cr, ci = c_re.astype(F32), c_im.astype(F32)
    rev_re, rev_im = pw_re[ck - 1::-1][:ck], pw_im[ck - 1::-1][:ck]
    ws_re = rev_re[:, :, :, None] * bb_re[None] - rev_im[:, :, :, None] * bb_im[None]
    ws_im = rev_re[:, :, :, None] * bb_im[None] + rev_im[:, :, :, None] * bb_re[None]
    ca_re = cr[None] * pw_re[1:, :, None, :] - ci[None] * pw_im[1:, :, None, :]
    ca_im = cr[None] * pw_im[1:, :, None, :] + ci[None] * pw_re[1:, :, None, :]
    cb_re = jnp.einsum('ghp,kgp,gpj->kghj', cr, pw_re[:ck], bb_re) - jnp.einsum('ghp,kgp,gpj->kghj', cr, pw_im[:ck], bb_im) \
        - jnp.einsum('ghp,kgp,gpj->kghj', ci, pw_re[:ck], bb_im) - jnp.einsum('ghp,kgp,gpj->kghj', ci, pw_im[:ck], bb_re)
    lag = np.arange(ck)[None, :] - np.arange(ck)[:, None]
    tz = cb_re[np.clip(lag, 0, ck - 1)]
    tz = jnp.where((lag >= 0)[:, :, None, None, None], tz, 0.0)
    cw = ck * g * hg
    ws = jnp.stack([ws_re, ws_im]).transpose(1, 2, 4, 0, 3).reshape(cw, 2 * p)
    wc = jnp.stack([ca_re, -ca_im]).transpose(0, 2, 4, 1, 3).reshape(2 * g * p, ck * hg)
    tzc = tz.transpose(0, 2, 4, 1, 3).reshape(cw, ck * hg)
    return ws.astype(BF16), wc.astype(BF16), tzc.astype(BF16), pw_re[ck].reshape(1, g * p), pw_im[ck].reshape(1, g * p)


def _group_expand(compact, expand, row_shift, col_shift, col0):
    full = _dot(compact, expand)
    row = lax.broadcasted_iota(jnp.int32, full.shape, 0)
    col = col0 + lax.broadcasted_iota(jnp.int32, full.shape, 1)
    same = ((row >> row_shift) & (S5_GROUPS - 1)) == ((col >> col_shift) & (S5_GROUPS - 1))
    return jnp.where(same, full, 0.0).astype(BF16)


def _s5_state_kernel(u_ref, ws_ref, e_ref, s_ref, w_scr):
    @pl.when(pl.program_id(1) == 0)
    def _():
        w_scr[...] = _group_expand(ws_ref[...], e_ref[...], S5_GROUP_BITS, S5_STATE_BITS,
                                   pl.program_id(0) * w_scr.shape[1])

    s_ref[...] = _dot(u_ref[...], w_scr[...])


def _s5_scan_kernel(s_ref, are_ref, aim_ref, xprev_ref, st_ref):
    @pl.when(pl.program_id(0) == 0)
    def _():
        st_ref[...] = jnp.zeros_like(st_ref)

    a_re, a_im = are_ref[...], aim_ref[...]
    nb, n = s_ref.shape[0], s_ref.shape[1]
    half = a_re.shape[1]

    def body(c, carry):
        out = []
        for b in range(nb):
            xr, xi = carry[2 * b], carry[2 * b + 1]
            xprev_ref[b, pl.ds(c, 1), 0:half] = xr
            xprev_ref[b, pl.ds(c, 1), half:2 * half] = xi
            s = s_ref[b, pl.ds(c, 1), :]
            out += [a_re * xr - a_im * xi + s[:, 0:half], a_re * xi + a_im * xr + s[:, half:2 * half]]
        return tuple(out)

    init = tuple(st_ref[b:b + 1, o:o + half] for b in range(nb) for o in (0, half))
    final = lax.fori_loop(0, n, body, init, unroll=4)
    for b in range(nb):
        st_ref[b:b + 1, 0:half] = final[2 * b]
        st_ref[b:b + 1, half:2 * half] = final[2 * b + 1]


def _s5_out_kernel(u_ref, xp_ref, tz_ref, wc_ref, e_ref, y_ref, tz_scr, wc_scr):
    @pl.when(pl.program_id(1) == 0)
    def _():
        col0 = pl.program_id(0) * tz_scr.shape[1]
        tz_scr[...] = _group_expand(tz_ref[...], e_ref[...], S5_GROUP_BITS, S5_GROUP_BITS, col0)
        wc_scr[...] = _group_expand(wc_ref[...], e_ref[...], S5_STATE_BITS, S5_GROUP_BITS, col0)

    y_ref[...] = _dot(u_ref[...], tz_scr[...]) + _dot(xp_ref[...].astype(BF16), wc_scr[...])


def _expand_matrix(outer, inner):
    e = np.zeros((outer, inner, outer, S5_GROUPS, inner), np.float32)
    for x in range(outer):
        for y in range(inner):
            e[x, y, x, :, y] = 1.0
    return jnp.asarray(e.reshape(outer * inner, outer * S5_GROUPS * inner), BF16)


def _s5_mixer_pre(ub, batch, seq, lam_re, lam_im, log_dt, b_re, b_im, c_re, c_im):
    ws, wc, tz, a16_re, a16_im = _s5_weights(lam_re, lam_im, log_dt, b_re, b_im, c_re, c_im)
    nc = seq // S5_CHUNK
    rows = batch * nc
    cw = S5_CHUNK * S5_WIDTH
    sw = 2 * S5_GROUPS * S5_STATE
    assert S5_GROUP == 1 << S5_GROUP_BITS and S5_STATE == 1 << S5_STATE_BITS
    tr = min(rows, S5_ROW_TILE)
    tn = S5_COL_TILE
    uc = ub.reshape(rows, cw)
    e_state = _expand_matrix(2, S5_STATE)
    e_out = _expand_matrix(S5_CHUNK, S5_GROUP)
    s = pl.pallas_call(
        _s5_state_kernel,
        grid=(sw // tn, rows // tr),
        in_specs=[pl.BlockSpec((tr, cw), lambda j, i: (i, 0)), _full(ws.shape),
                  pl.BlockSpec((e_state.shape[0], tn), lambda j, i: (0, j))],
        out_specs=pl.BlockSpec((tr, tn), lambda j, i: (i, j)),
        out_shape=jax.ShapeDtypeStruct((rows, sw), F32),
        scratch_shapes=[pltpu.VMEM((cw, tn), BF16)],
        compiler_params=_params("arbitrary", "arbitrary"),
        name="s5_state",
    )(uc, ws, e_state)
    tc = min(nc, S5_SCAN_TILE)
    xprev = pl.pallas_call(
        _s5_scan_kernel,
        grid=(nc // tc,),
        in_specs=[pl.BlockSpec((batch, tc, sw), lambda i: (0, i, 0)), _full((1, sw // 2)), _full((1, sw // 2))],
        out_specs=pl.BlockSpec((batch, tc, sw), lambda i: (0, i, 0)),
        out_shape=jax.ShapeDtypeStruct((batch, nc, sw), F32),
        scratch_shapes=[pltpu.VMEM((batch, sw), F32)],
        compiler_params=_params("arbitrary"),
        name="s5_scan",
    )(s.reshape(batch, nc, sw), a16_re, a16_im)
    y = pl.pallas_call(
        _s5_out_kernel,
        grid=(cw // tn, rows // tr),
        in_specs=[pl.BlockSpec((tr, cw), lambda j, i: (i, 0)), pl.BlockSpec((tr, sw), lambda j, i: (i, 0)),
                  _full(tz.shape), _full(wc.shape), pl.BlockSpec((e_out.shape[0], tn), lambda j, i: (0, j))],
        out_specs=pl.BlockSpec((tr, tn), lambda j, i: (i, j)),
        out_shape=jax.ShapeDtypeStruct((rows, cw), F32),
        scratch_shapes=[pltpu.VMEM((cw, tn), BF16), pltpu.VMEM((sw, tn), BF16)],
        compiler_params=_params("arbitrary", "arbitrary"),
        name="s5_out",
    )(uc, xprev.reshape(rows, sw), tz, wc, e_out)
    return y.reshape(batch * seq, S5_WIDTH)


def _even_out_kernel(ypre_ref, u_ref, yb_ref, x_ref, d_ref, wglu_ref, wout_ref, o_ref):
    y = _gelu(ypre_ref[...] + d_ref[...] * u_ref[...])
    y = y * _sigmoid(_dot(y.astype(BF16), wglu_ref[...]))
    o_ref[...] = (x_ref[...] + _dot(y.astype(BF16), wout_ref[0:S5_WIDTH, :])
                  + _dot(yb_ref[...], wout_ref[S5_WIDTH:D_MODEL, :]))


def _even_out(ypre, u, yb, x2, d, w_glu, w_out, tm=ROW_TILE):
    t = x2.shape[0]
    row = lambda w: pl.BlockSpec((tm, w), lambda i: (i, 0))
    return pl.pallas_call(
        _even_out_kernel,
        grid=(t // tm,),
        in_specs=[row(S5_WIDTH), row(S5_WIDTH), row(CONV_WIDTH), row(D_MODEL), _full((1, S5_WIDTH)),
                  _full((S5_WIDTH, S5_WIDTH)), _full((D_MODEL, D_MODEL))],
        out_specs=row(D_MODEL),
        out_shape=jax.ShapeDtypeStruct((t, D_MODEL), F32),
        compiler_params=_params("arbitrary"),
        name="even_out",
    )(ypre, u, yb, x2, d, w_glu, w_out)


def _first_max(v, pos, width, axis=-1):
    m = jnp.max(v, axis=axis, keepdims=True)
    idx = jnp.min(jnp.where(v == m, pos, width), axis=axis, keepdims=True)
    return m, idx


MOE_TM = 1024
MOE_ALIGN = 16
MOE_SLOTS = 1152
MOE_WIN = 320
ROUTER_LANES = LANES


def _moe_router_kernel(h_ref, g_ref, wr_hi_ref, wr_lo_ref, br_ref, gate_ref, grp_ref, cnt_ref):
    xn = _rms(h_ref[...], g_ref[...])
    logits = _dot_x3(xn, wr_hi_ref[...], wr_lo_ref[...]) + br_ref[...]
    lane = lax.broadcasted_iota(jnp.int32, logits.shape, 1)
    width = logits.shape[1]
    is_g = lane < N_GROUPS
    gl = jnp.where(is_g, logits, -jnp.inf)
    gm, gi = _first_max(gl, lane, width)
    gw = 1.0 / jnp.sum(jnp.where(is_g, jnp.exp(gl - gm), 0.0), axis=-1, keepdims=True)
    lo = N_GROUPS + gi * EXPERTS_PER_GROUP
    in_grp = (lane >= lo) & (lane < lo + EXPERTS_PER_GROUP)
    el = jnp.where(in_grp, logits, -jnp.inf)
    m1, i1 = _first_max(el, lane, width)
    m2, i2 = _first_max(jnp.where(lane == i1, -jnp.inf, el), lane, width)
    p2 = jnp.exp(m2 - m1)
    w1 = gw / (1.0 + p2)
    w2 = gw * p2 / (1.0 + p2)
    gate_ref[...] = jnp.where(lane == i1, w1, 0.0) + jnp.where(lane == i2, w2, 0.0)
    grp = (lane == gi).astype(F32)
    grp_ref[...] = grp.astype(BF16)
    cnt_ref[0] = jnp.broadcast_to(jnp.sum(grp, axis=0, keepdims=True), cnt_ref.shape[1:])


def _moe_expert_kernel(base_ref, nwin_ref, h_ref, g_ref, gate_ref, grp_ref, wg_ref, wu_ref, wd_ref, o_ref,
                       xs_ref, gs_ref, ys_ref, pt_ref):
    i, g = pl.program_id(0), pl.program_id(1)
    tm = h_ref.shape[0]

    @pl.when((i == 0) & (g == 0))
    def _():
        xs_ref[...] = jnp.zeros_like(xs_ref)
        gs_ref[...] = jnp.zeros_like(gs_ref)
        ys_ref[...] = jnp.zeros_like(ys_ref)

    @pl.when(g == 0)
    def _():
        xn = _rms(h_ref[...], g_ref[...]).astype(BF16)
        grp = grp_ref[...]
        earlier = (lax.broadcasted_iota(jnp.int32, (tm, tm), 0) > lax.broadcasted_iota(jnp.int32, (tm, tm), 1)).astype(BF16)
        rank = _dot(earlier, grp)
        lane = lax.broadcasted_iota(jnp.int32, rank.shape, 1)
        for k in range(N_GROUPS):
            rank = rank + jnp.where(lane == k, base_ref[i * N_GROUPS + k].astype(F32), 0.0)
        slot = jnp.sum(grp.astype(F32) * rank, axis=-1, keepdims=True).astype(jnp.int32)
        pt = (lax.broadcasted_iota(jnp.int32, (tm, MOE_SLOTS), 1) == slot).astype(BF16)
        pt_ref[...] = pt
        gather = lambda x: lax.dot_general(pt, x, (((0,), (0,)), ((), ())), preferred_element_type=F32)
        xs_ref[0:MOE_SLOTS, :] = gather(xn).astype(BF16)
        gate_hi, gate_lo = _split(gate_ref[...])
        gs_ref[0:MOE_SLOTS, :] = gather(gate_hi) + gather(gate_lo)

    def window(w, carry):
        r0 = pl.multiple_of(base_ref[i * N_GROUPS + g] + w * MOE_WIN, MOE_ALIGN)
        x = xs_ref[pl.ds(r0, MOE_WIN), :]
        gate = gs_ref[pl.ds(r0, MOE_WIN), :]
        lane = lax.broadcasted_iota(jnp.int32, gate.shape, 1)
        y = None
        for j in range(EXPERTS_PER_GROUP):
            ge = jnp.sum(jnp.where(lane == g * EXPERTS_PER_GROUP + (j + N_GROUPS), gate, 0.0), axis=-1, keepdims=True)
            h1 = _dot(x, wg_ref[j])
            h3 = _dot(x, wu_ref[j])
            act = (h1 * _sigmoid(h1)) * h3 * ge
            yj = _dot(act.astype(BF16), wd_ref[j])
            y = yj if y is None else y + yj
        ys_ref[pl.ds(r0, MOE_WIN), :] = y.astype(BF16)
        return carry

    lax.fori_loop(0, nwin_ref[i * N_GROUPS + g], window, 0)

    @pl.when(g == N_GROUPS - 1)
    def _():
        o_ref[...] = h_ref[...] + _dot(pt_ref[...], ys_ref[0:MOE_SLOTS, :])


def _moe(h2, gain, w_group, b_group, w_expert, b_expert, w_gate, w_up, w_down):
    t = h2.shape[0]
    tm, rw = MOE_TM, ROUTER_LANES
    assert t % tm == 0 and MOE_SLOTS >= tm + N_GROUPS * (MOE_ALIGN - 1) and MOE_WIN % MOE_ALIGN == 0
    n_tiles = t // tm
    wr = jnp.zeros((D_MODEL, rw), F32).at[:, 0:N_GROUPS].set(w_group).at[:, N_GROUPS:N_GROUPS + N_EXPERTS].set(w_expert)
    br = jnp.zeros((1, rw), F32).at[0, 0:N_GROUPS].set(b_group).at[0, N_GROUPS:N_GROUPS + N_EXPERTS].set(b_expert)
    wr_hi = wr.astype(BF16)
    wr_lo = (wr - wr_hi.astype(F32)).astype(BF16)
    row = lambda width: pl.BlockSpec((tm, width), lambda i: (i, 0))
    gates, grp, cnt = pl.pallas_call(
        _moe_router_kernel,
        grid=(n_tiles,),
        in_specs=[row(D_MODEL), _full((1, D_MODEL)), _full((D_MODEL, rw)), _full((D_MODEL, rw)), _full((1, rw))],
        out_specs=[row(rw), row(rw), pl.BlockSpec((1, SUBLANES, rw), lambda i: (i, 0, 0))],
        out_shape=[jax.ShapeDtypeStruct((t, rw), F32), jax.ShapeDtypeStruct((t, rw), BF16),
                   jax.ShapeDtypeStruct((n_tiles, SUBLANES, rw), F32)],
        compiler_params=_params("arbitrary"),
        name="moe_router",
    )(h2, gain, wr_hi, wr_lo, br)
    n = cnt[:, 0, 0:N_GROUPS].astype(jnp.int32)
    padded = (n + (MOE_ALIGN - 1)) // MOE_ALIGN * MOE_ALIGN
    base = (jnp.cumsum(padded, axis=1) - padded).reshape(-1)
    nwin = ((padded + (MOE_WIN - 1)) // MOE_WIN).reshape(-1)
    tile = lambda width: pl.BlockSpec((tm, width), lambda i, g, *_: (i, 0))
    experts = lambda shape: pl.BlockSpec((EXPERTS_PER_GROUP,) + shape, lambda i, g, *_: (g, 0, 0))
    slots = MOE_SLOTS + MOE_WIN
    return pl.pallas_call(
        _moe_expert_kernel,
        grid_spec=pltpu.PrefetchScalarGridSpec(
            num_scalar_prefetch=2,
            grid=(n_tiles, N_GROUPS),
            in_specs=[tile(D_MODEL), pl.BlockSpec((1, D_MODEL), lambda i, g, *_: (0, 0)), tile(rw), tile(rw),
                      experts((D_MODEL, EXPERT_FF)), experts((D_MODEL, EXPERT_FF)), experts((EXPERT_FF, D_MODEL))],
            out_specs=tile(D_MODEL),
            scratch_shapes=[pltpu.VMEM((slots, D_MODEL), BF16), pltpu.VMEM((slots, rw), F32),
                            pltpu.VMEM((slots, D_MODEL), BF16), pltpu.VMEM((tm, MOE_SLOTS), BF16)]),
        out_shape=jax.ShapeDtypeStruct((t, D_MODEL), F32),
        compiler_params=_params("arbitrary", "arbitrary"),
        name="moe",
    )(base, nwin, h2, gain, gates, grp, w_gate.astype(BF16), w_up.astype(BF16), w_down.astype(BF16))


ODD_SPLITS = (MOBA_W, MOBA_W, MOBA_W, NSA_W, KV_W, KV_W, KV_W, KV_W, KV_W, KV_W, GATE_LANES)
ODD_IN_PAD = sum(ODD_SPLITS)


def _head_rms(x, hsum, gain):
    w = x.shape[1]
    ss = jnp.concatenate([_dot_x2(x[:, o:o + hsum.shape[0]] * x[:, o:o + hsum.shape[0]], hsum)
                          for o in range(0, w, hsum.shape[0])], axis=1) if w > hsum.shape[0] else _dot_x2(x * x, hsum)
    return x * lax.rsqrt(ss * (1.0 / HEAD_DIM) + EPS) * gain


def _odd_in_kernel(tiles_per_seq, x_ref, g_ref, w_ref, hsum_ref, gq_ref, gk_ref, gnq_ref, gks_ref, gkw_ref,
                   qm_ref, kam_ref, kmean_ref, vam_ref, qs_ref, kvc_ref, kas_ref, vas_ref, kvw_ref, gt_ref):
    xn = _rms(x_ref[...], g_ref[...]).astype(BF16)
    offs = np.cumsum((0,) + ODD_SPLITS)
    col = lambda j: _dot(xn, w_ref[:, int(offs[j]):int(offs[j + 1])])
    head = lambda x, h: x[:, h * HEAD_DIM:(h + 1) * HEAD_DIM]
    hsum = hsum_ref[...]
    hsum128 = hsum_ref[0:KV_W, 0:KV_W]
    tm = x_ref.shape[0]
    pos = (pl.program_id(0) % tiles_per_seq) * tm + lax.broadcasted_iota(jnp.int32, (tm, HEAD_DIM), 0)
    lane = lax.broadcasted_iota(jnp.int32, (tm, HEAD_DIM), 1)
    ones_col = (lane == 0).astype(BF16)

    qm = _head_rms(col(0), hsum, gq_ref[...])
    km = _head_rms(col(1), hsum, gk_ref[...])
    for j in range(tm // MOBA_BLOCK):
        kmean_ref[0, j:j + 1, :] = jnp.mean(km[j * MOBA_BLOCK:(j + 1) * MOBA_BLOCK, :], axis=0, keepdims=True)
    km = km.astype(BF16)
    vm = col(2).astype(BF16)
    moba_id = (lane == pos // MOBA_BLOCK).astype(BF16)
    for h in range(MOBA_HEADS):
        qm_ref[0, h] = head(qm, h)
        kam_ref[0, h] = jnp.concatenate([head(km, h), moba_id], axis=1)
        vam_ref[0, h] = jnp.concatenate([head(vm, h), ones_col], axis=1)

    qd = (_head_rms(col(3), hsum, gnq_ref[...]) * (HEAD_DIM ** -0.5)).astype(BF16)
    for hk in range(NSA_KV_HEADS):
        for j in range(tm // NSA_TQ):
            for g in range(NSA_GROUP):
                qs_ref[0, hk, j, g * NSA_TQ:(g + 1) * NSA_TQ, :] = head(qd, hk * NSA_GROUP + g)[j * NSA_TQ:(j + 1) * NSA_TQ, :]
    kvc_ref[0] = col(4)
    kvc_ref[1] = col(5)
    ks = _head_rms(col(6), hsum128, gks_ref[...]).astype(BF16)
    vs = col(7).astype(BF16)
    kw = _head_rms(col(8), hsum128, gkw_ref[...]).astype(BF16)
    vw = col(9).astype(BF16)
    sel_id = (lane == (pos // SEL_BLOCK) % SEL_LANES).astype(BF16)
    for hk in range(NSA_KV_HEADS):
        kas_ref[0, hk] = jnp.concatenate([head(ks, hk), sel_id], axis=1)
        vas_ref[0, hk] = jnp.concatenate([head(vs, hk), ones_col], axis=1)
        kvw_ref[0, hk] = jnp.concatenate([head(kw, hk), head(vw, hk)], axis=1)
    gt_ref[...] = _sigmoid(col(10))


def _odd_in(h2, batch, seq, gain, w_in, moba_q_norm, moba_k_norm, nsa_q_norm, nsa_ksel_norm, nsa_kwin_norm, tm=ROW_TILE):
    t = h2.shape[0]
    assert MOBA_LANES == HEAD_DIM and SEL_LANES == HEAD_DIM and seq % tm == 0 and tm % MOBA_BLOCK == 0
    tps = seq // tm
    w = jnp.pad(w_in, ((0, 0), (0, ODD_IN_PAD - w_in.shape[1]))).astype(BF16)
    hsum = jnp.asarray(np.kron(np.eye(MOBA_W // HEAD_DIM), np.ones((HEAD_DIM, HEAD_DIM))), BF16)
    tile = lambda g, width: jnp.tile(g.astype(F32), width // HEAD_DIM).reshape(1, width)
    row = lambda width: pl.BlockSpec((tm, width), lambda i: (i, 0))
    heads = lambda n, width: pl.BlockSpec((1, n, tm, width), lambda i: (i // tps, 0, i % tps, 0))
    nmb = tm // MOBA_BLOCK
    nqt = tm // NSA_TQ
    rows = NSA_GROUP * NSA_TQ
    sds = jax.ShapeDtypeStruct
    out_specs = [heads(MOBA_HEADS, HEAD_DIM), heads(MOBA_HEADS, LANES), pl.BlockSpec((1, nmb, MOBA_W), lambda i: (i, 0, 0)),
                 heads(MOBA_HEADS, LANES),
                 pl.BlockSpec((1, NSA_KV_HEADS, nqt, rows, HEAD_DIM), lambda i: (i // tps, 0, i % tps, 0, 0)),
                 pl.BlockSpec((2, tm, KV_W), lambda i: (0, i, 0)),
                 heads(NSA_KV_HEADS, LANES), heads(NSA_KV_HEADS, LANES), heads(NSA_KV_HEADS, LANES),
                 row(GATE_LANES)]
    out_shape = [sds((batch, MOBA_HEADS, seq, HEAD_DIM), F32), sds((batch, MOBA_HEADS, seq, LANES), BF16),
                 sds((t // tm, nmb, MOBA_W), F32), sds((batch, MOBA_HEADS, seq, LANES), BF16),
                 sds((batch, NSA_KV_HEADS, seq // NSA_TQ, rows, HEAD_DIM), BF16),
                 sds((2, t, KV_W), F32), sds((batch, NSA_KV_HEADS, seq, LANES), BF16),
                 sds((batch, NSA_KV_HEADS, seq, LANES), BF16), sds((batch, NSA_KV_HEADS, seq, LANES), BF16),
                 sds((t, GATE_LANES), F32)]
    return pl.pallas_call(
        functools.partial(_odd_in_kernel, tps),
        grid=(t // tm,),
        in_specs=[row(D_MODEL), _full((1, D_MODEL)), _full((D_MODEL, ODD_IN_PAD)), _full((MOBA_W, MOBA_W)),
                  _full((1, MOBA_W)), _full((1, MOBA_W)), _full((1, NSA_W)), _full((1, KV_W)), _full((1, KV_W))],
        out_specs=out_specs,
        out_shape=out_shape,
        compiler_params=_params("arbitrary"),
        name="odd_in",
    )(h2, gain, w, hsum, tile(moba_q_norm, MOBA_W), tile(moba_k_norm, MOBA_W), tile(nsa_q_norm, NSA_W),
      tile(nsa_ksel_norm, KV_W), tile(nsa_kwin_norm, KV_W))


def _compress_kernel(c_ref, w1_ref, w2_ref, pe_ref, g_ref, o_ref):
    kind = pl.program_id(0)
    n16 = c_ref.shape[1] // CMP_STRIDE
    half = CMP_STRIDE * HEAD_DIM
    peb = _dot(pe_ref[0], w1_ref[0])[0:1, :]
    xs = [c_ref[0, pl.ds(s, n16, stride=CMP_STRIDE), :].astype(BF16) for s in range(CMP_STRIDE)]
    for h in range(NSA_KV_HEADS):
        first = second = None
        for s in range(CMP_STRIDE):
            x = xs[s][:, h * HEAD_DIM:(h + 1) * HEAD_DIM]
            a = _dot(x, w1_ref[0, s * HEAD_DIM:(s + 1) * HEAD_DIM, :])
            b = _dot(x, w1_ref[0, half + s * HEAD_DIM:half + (s + 1) * HEAD_DIM, :])
            first = a if first is None else first + a
            second = b if second is None else second + b
        hid = _gelu(first + pltpu.roll(second, n16 - 1, 0) + peb)
        out = _dot(hid.astype(BF16), w2_ref[0])
        o_ref[0, 0, h] = jnp.where(kind == 0, _rms(out, g_ref[...]), out).astype(BF16)


def _compress(kvc, batch, seq, pe_k, w1_k, w2_k, pe_v, w1_v, w2_v, kcmp_norm):
    n16 = seq // CMP_STRIDE
    half = CMP_STRIDE * HEAD_DIM
    w1 = jnp.stack([w1_k, w1_v]).astype(BF16)
    w2 = jnp.stack([w2_k, w2_v]).astype(BF16)
    pe = jnp.stack([pe_k, pe_v]).reshape(2, 1, 2 * half)
    pe = jnp.broadcast_to(pe, (2, SUBLANES, 2 * half)).astype(BF16)
    return pl.pallas_call(
        _compress_kernel,
        grid=(2, batch),
        in_specs=[pl.BlockSpec((1, seq, KV_W), lambda k, b: (k, b, 0)),
                  pl.BlockSpec((1, 2 * half, CMP_HIDDEN), lambda k, b: (k, 0, 0)),
                  pl.BlockSpec((1, CMP_HIDDEN, HEAD_DIM), lambda k, b: (k, 0, 0)),
                  pl.BlockSpec((1, SUBLANES, 2 * half), lambda k, b: (k, 0, 0)),
                  _full((1, HEAD_DIM))],
        out_specs=pl.BlockSpec((1, 1, NSA_KV_HEADS, n16, HEAD_DIM), lambda k, b: (k, b, 0, 0, 0)),
        out_shape=jax.ShapeDtypeStruct((2, batch, NSA_KV_HEADS, n16, HEAD_DIM), BF16),
        compiler_params=_params("arbitrary", "arbitrary"),
        name="nsa_compress",
    )(kvc, w1, w2, pe, kcmp_norm.astype(F32).reshape(1, HEAD_DIM))


M_INIT = -1e30


def _softmax_init(m_ref, acc_ref):
    m_ref[...] = jnp.full(m_ref.shape, M_INIT, F32)
    acc_ref[...] = jnp.zeros(acc_ref.shape, F32)


def _softmax_step(s, v_aug, m_ref, acc_ref):
    m_old = m_ref[...]
    m_new = jnp.maximum(m_old, jnp.max(s, axis=-1, keepdims=True))
    alpha = jnp.exp(m_old - m_new)
    p = jnp.exp(s - jnp.tile(m_new, (1, s.shape[1] // LANES)))
    acc_ref[...] = alpha * acc_ref[...] + _dot(p.astype(BF16), v_aug)
    m_ref[...] = m_new


def _past_keys_loop(n_keys, tile, step, riders=(), parts=2):
    n_full = n_keys // tile

    def body(j, carry):
        step(pl.multiple_of(j * tile, tile), tile)
        return carry

    extras = []
    for k, (with_tile, alone) in enumerate(riders):
        def both(with_tile=with_tile, k=k):
            out = with_tile()
            step(k * tile, tile)
            return out

        extras.append(lax.cond(n_full > k, both, alone))
    lax.fori_loop(len(riders), n_full, body, 0)
    rest = n_keys - n_full * tile
    start = pl.multiple_of(n_full * tile, tile)
    part = tile // parts
    n_parts = (rest + part - 1) // part
    for k in range(1, parts + 1):
        @pl.when(n_parts == k)
        def _(k=k):
            done = 0
            for size in (tile, tile // 2, tile // 4):
                if size % part == 0 and k * part - done >= size:
                    step(pl.multiple_of(start + done, part), size)
                    done += size
            assert done == k * part

    return extras


def _softmax_result(acc_ref):
    acc = acc_ref[...]
    return acc[:, 0:HEAD_DIM] * (1.0 / acc[:, HEAD_DIM:HEAD_DIM + 1])


def _pick_top(score, pos, width, k):
    sel = jnp.zeros(score.shape, jnp.bool_)
    for _ in range(k):
        m, idx = _first_max(score, pos, width, axis=0)
        hit = (pos == idx) & (m > -jnp.inf)
        sel = sel | hit
        score = jnp.where(pos == idx, -jnp.inf, score)
    return sel


MOBA_LANES = 64
MOBA_TK = 2048


MOBA_TQ = 4 * MOBA_BLOCK


def _moba_kernel(q_ref, ka_ref, va_ref, kmean_ref, o_ref, qa_ref, m_ref, acc_ref):
    i0 = pl.program_id(2) * (MOBA_TQ // MOBA_BLOCK)
    q = q_ref[0, 0]
    q_hi, q_lo = _split(q)
    km_hi, km_lo = _split(kmean_ref[0, 0])
    gate = _dot_nt(km_hi, q_hi) + (_dot_nt(km_lo, q_hi) + _dot_nt(km_hi, q_lo))
    blk = lax.broadcasted_iota(jnp.int32, gate.shape, 0)
    cur = i0 + lax.broadcasted_iota(jnp.int32, gate.shape, 1) // MOBA_BLOCK
    sel = _pick_top(jnp.where(blk < cur, gate, -jnp.inf), blk, gate.shape[0], MOBA_TOPK)
    qs = (q * (HEAD_DIM ** -0.5)).astype(BF16)
    past = jnp.where(sel & (blk < i0), 0.0, NEG).T[:, 0:MOBA_LANES]
    own = jnp.where((blk == cur) | (sel & (blk >= i0)), 0.0, NEG).T[:, 0:MOBA_LANES]
    qa_ref[0] = jnp.concatenate([qs, past.astype(BF16)], axis=1)
    qa_ref[1] = jnp.concatenate([qs, own.astype(BF16)], axis=1)
    _softmax_init(m_ref, acc_ref)

    def step(start, size):
        s = _dot_nt(qa_ref[0], ka_ref[0, 0, pl.ds(start, size), :])
        _softmax_step(s, va_ref[0, 0, pl.ds(start, size), :], m_ref, acc_ref)

    _past_keys_loop(i0 * MOBA_BLOCK, MOBA_TK, step)
    start = pl.multiple_of(i0 * MOBA_BLOCK, MOBA_TQ)
    s = _dot_nt(qa_ref[1], ka_ref[0, 0, pl.ds(start, MOBA_TQ), :])
    qpos = lax.broadcasted_iota(jnp.int32, s.shape, 0)
    kpos = lax.broadcasted_iota(jnp.int32, s.shape, 1)
    hidden = (qpos // MOBA_BLOCK == kpos // MOBA_BLOCK) & (kpos > qpos)
    _softmax_step(jnp.where(hidden, NEG, s), va_ref[0, 0, pl.ds(start, MOBA_TQ), :], m_ref, acc_ref)
    o_ref[0, 0] = _softmax_result(acc_ref).astype(BF16)


def _moba(qm, ka, kmean, va, batch, seq):
    nmb = seq // MOBA_BLOCK
    assert nmb <= MOBA_LANES and seq % MOBA_TK == 0 and seq % MOBA_TQ == 0
    kmean = kmean.reshape(batch, nmb, MOBA_HEADS, HEAD_DIM).transpose(0, 2, 1, 3)
    kmean = jnp.pad(kmean, ((0, 0), (0, 0), (0, LANES - nmb), (0, 0)))
    return pl.pallas_call(
        _moba_kernel,
        grid=(batch, MOBA_HEADS, seq // MOBA_TQ),
        in_specs=[pl.BlockSpec((1, 1, MOBA_TQ, HEAD_DIM), lambda b, h, i: (b, h, i, 0)),
                  pl.BlockSpec((1, 1, seq, LANES), lambda b, h, i: (b, h, 0, 0)),
                  pl.BlockSpec((1, 1, seq, LANES), lambda b, h, i: (b, h, 0, 0)),
                  pl.BlockSpec((1, 1, LANES, HEAD_DIM), lambda b, h, i: (b, h, 0, 0))],
        out_specs=pl.BlockSpec((1, 1, MOBA_TQ, HEAD_DIM), lambda b, h, i: (b, h, i, 0)),
        out_shape=jax.ShapeDtypeStruct((batch, MOBA_HEADS, seq, HEAD_DIM), BF16),
        scratch_shapes=[pltpu.VMEM((2, MOBA_TQ, LANES), BF16), pltpu.VMEM((MOBA_TQ, LANES), F32),
                        pltpu.VMEM((MOBA_TQ, LANES), F32)],
        compiler_params=_params("arbitrary", "arbitrary", "arbitrary"),
        name="moba",
    )(qm, ka, va, kmean)


NSA_TQ = 2 * SEL_BLOCK
NSA_TK = 2048
SEL_LANES = 64
SUPER_KEYS = SEL_LANES * SEL_BLOCK
CMP_WIDTH_STEP = 256


def _nsa_kernel(n_super, q_ref, kc_ref, vc_ref, ka_ref, va_ref, kvw_ref, gt_ref, e_ref, band_ref, o_ref,
                qa_ref, m_ref, acc_ref):
    qi = pl.program_id(2)
    tq = NSA_TQ
    s0 = qi * tq
    q = q_ref[0, 0, 0]

    def compressed(width):
        sc = _dot_nt(q, kc_ref[0, 0, 0:width, :])
        rq = lax.broadcasted_iota(jnp.int32, (sc.shape[0], LANES), 0) & (tq - 1)
        last = jnp.tile((s0 + rq - (CMP_BLOCK - 1)) >> (CMP_STRIDE.bit_length() - 1), (1, width // LANES))
        sc = jnp.where(lax.broadcasted_iota(jnp.int32, sc.shape, 1) <= last, sc, NEG)
        mx = jnp.max(sc, axis=-1, keepdims=True)
        pc = jnp.exp(sc - mx)
        pc = pc * jnp.where(mx > 0.5 * NEG, 1.0 / jnp.sum(pc, axis=-1, keepdims=True), 0.0)
        imp = pc[0:tq]
        for g in range(1, NSA_GROUP):
            imp = imp + pc[g * tq:(g + 1) * tq]
        return _dot(pc.astype(BF16), vc_ref[0, 0, 0:width, :]), _dot_x2(imp, band_ref[0:width, :])

    n16 = kc_ref.shape[2]
    widths = list(range(CMP_WIDTH_STEP, n16, CMP_WIDTH_STEP)) + [n16]
    o_c, pslc = lax.switch((s0 + tq - 1) // (CMP_WIDTH_STEP * CMP_STRIDE),
                           [functools.partial(compressed, wd) for wd in widths])

    wlen = WINDOW + tq

    def window(kvw, masked):
        sw = masked(_dot_nt(q, kvw[:, 0:HEAD_DIM]))
        pw = jnp.exp(sw - jnp.max(sw, axis=-1, keepdims=True))
        return _dot(pw.astype(BF16), kvw)[:, HEAD_DIM:2 * HEAD_DIM] * (1.0 / jnp.sum(pw, axis=-1, keepdims=True))

    def window_interior():
        def masked(sw):
            rq = lax.broadcasted_iota(jnp.int32, (sw.shape[0], tq), 0) & (tq - 1)
            c = lax.broadcasted_iota(jnp.int32, (sw.shape[0], tq), 1)
            return jnp.concatenate([jnp.where(c > rq, sw[:, 0:tq], NEG), sw[:, tq:WINDOW],
                                    jnp.where(c <= rq, sw[:, WINDOW:wlen], NEG)], axis=1)

        return window(kvw_ref[0, 0, pl.ds(pl.multiple_of(s0 - WINDOW, tq), wlen), :], masked)

    def window_start():
        def masked(sw):
            kabs = lax.broadcasted_iota(jnp.int32, sw.shape, 1)
            t = s0 + (lax.broadcasted_iota(jnp.int32, sw.shape, 0) & (tq - 1))
            return jnp.where((kabs <= t) & (kabs > t - WINDOW), sw, NEG)

        return window(kvw_ref[0, 0, 0:wlen, :], masked)

    pslc_t = pslc.T
    nb = pslc_t.shape[0]
    blk = lax.broadcasted_iota(jnp.int32, pslc_t.shape, 0)
    cur = (s0 + lax.broadcasted_iota(jnp.int32, pslc_t.shape, 1)) // SEL_BLOCK
    c0 = s0 // SEL_BLOCK
    elig = (blk >= 1) & (blk <= cur - 2)
    sel = _pick_top(jnp.where(elig, pslc_t, -jnp.inf), blk, nb, SEL_TOPK - 3)
    sel = sel | (blk == 0) | (blk == cur - 1)
    past = jnp.where(sel & (blk < c0), 0.0, NEG).T.astype(BF16)
    for st in range(n_super):
        b = past[:, st * SEL_LANES:(st + 1) * SEL_LANES]
        qa_ref[st] = jnp.concatenate([q, jnp.concatenate([b] * NSA_GROUP, axis=0)], axis=1)

    _softmax_init(m_ref, acc_ref)
    d0 = pl.multiple_of(s0, tq)
    s = _dot_nt(q, ka_ref[0, 0, pl.ds(d0, tq), :][:, 0:HEAD_DIM])
    qpos = lax.broadcasted_iota(jnp.int32, s.shape, 0) & (tq - 1)
    kpos = lax.broadcasted_iota(jnp.int32, s.shape, 1)
    _softmax_step(jnp.where(kpos <= qpos, s, NEG), va_ref[0, 0, pl.ds(d0, tq), :], m_ref, acc_ref)

    def step(start, size):
        s = _dot_nt(qa_ref[start // SUPER_KEYS], ka_ref[0, 0, pl.ds(start, size), :])
        _softmax_step(s, va_ref[0, 0, pl.ds(start, size), :], m_ref, acc_ref)

    (o_w,) = _past_keys_loop(s0, NSA_TK, step, parts=4, riders=(
        (window_interior, lambda: lax.cond(s0 >= WINDOW, window_interior, window_start)),))
    o_s = _softmax_result(acc_ref)

    w = NSA_GROUP * HEAD_DIM
    gexp = _dot_x2(gt_ref[...], e_ref[0])
    wide = lambda x: jnp.concatenate([x[g * tq:(g + 1) * tq] for g in range(NSA_GROUP)], axis=1)
    o_ref[...] = (gexp[:, 0:w] * wide(o_c) + gexp[:, w:2 * w] * wide(o_s) + gexp[:, 2 * w:3 * w] * wide(o_w)).astype(BF16)


def _nsa(qs, kcmp, vcmp, ka, va, kvw, gates, batch, seq):
    tq = NSA_TQ
    nq = seq // tq
    nb = seq // SEL_BLOCK
    n16 = seq // CMP_STRIDE
    assert seq % SUPER_KEYS == 0 and tq == 2 * SEL_BLOCK and WINDOW % tq == 0 and NSA_TK >= WINDOW
    n_super = seq // SUPER_KEYS
    rows = NSA_GROUP * tq
    e = np.zeros((NSA_KV_HEADS, GATE_LANES, 3 * NSA_GROUP * HEAD_DIM), np.float32)
    for br in range(3):
        for hk in range(NSA_KV_HEADS):
            for g in range(NSA_GROUP):
                c = (br * NSA_GROUP + g) * HEAD_DIM
                e[hk, br * NSA_HEADS + hk * NSA_GROUP + g, c:c + HEAD_DIM] = 1.0
    nn, jj = np.arange(n16)[:, None], np.arange(nb)[None, :]
    band = ((nn >= 4 * jj - 1) & (nn <= 4 * jj + 3)).astype(np.float32)
    resident = lambda width: pl.BlockSpec((1, 1, seq, width), lambda b, h, i: (b, h, 0, 0))
    w = NSA_GROUP * HEAD_DIM
    return pl.pallas_call(
        functools.partial(_nsa_kernel, n_super),
        grid=(batch, NSA_KV_HEADS, nq),
        in_specs=[pl.BlockSpec((1, 1, 1, rows, HEAD_DIM), lambda b, h, i: (b, h, i, 0, 0)),
                  pl.BlockSpec((1, 1, n16, HEAD_DIM), lambda b, h, i: (b, h, 0, 0)),
                  pl.BlockSpec((1, 1, n16, HEAD_DIM), lambda b, h, i: (b, h, 0, 0)),
                  resident(LANES), resident(LANES), resident(LANES),
                  pl.BlockSpec((tq, GATE_LANES), lambda b, h, i: (b * nq + i, 0)),
                  pl.BlockSpec((1, GATE_LANES, 3 * w), lambda b, h, i: (h, 0, 0)),
                  pl.BlockSpec((n16, nb), lambda b, h, i: (0, 0))],
        out_specs=pl.BlockSpec((tq, w), lambda b, h, i: (b * nq + i, h)),
        out_shape=jax.ShapeDtypeStruct((batch * seq, NSA_W), BF16),
        scratch_shapes=[pltpu.VMEM((n_super, rows, LANES), BF16), pltpu.VMEM((rows, LANES), F32),
                        pltpu.VMEM((rows, LANES), F32)],
        compiler_params=_params("arbitrary", "arbitrary", "arbitrary"),
        name="nsa",
    )(qs, kcmp, vcmp, ka, va, kvw, gates, jnp.asarray(e, BF16), jnp.asarray(band, BF16))


def _odd_out_kernel(om_ref, on_ref, h_ref, w_ref, o_ref):
    acc = h_ref[...] + _dot(on_ref[...], w_ref[MOBA_W:D_MODEL, :])
    for h in range(MOBA_HEADS):
        acc = acc + _dot(om_ref[0, h], w_ref[h * HEAD_DIM:(h + 1) * HEAD_DIM, :])
    o_ref[...] = acc


def _odd_out(o_moba, o_nsa, h2, w_out, seq, tm=ROW_TILE):
    t = h2.shape[0]
    tps = seq // tm
    row = lambda w: pl.BlockSpec((tm, w), lambda i: (i, 0))
    return pl.pallas_call(
        _odd_out_kernel,
        grid=(t // tm,),
        in_specs=[pl.BlockSpec((1, MOBA_HEADS, tm, HEAD_DIM), lambda i: (i // tps, 0, i % tps, 0)),
                  row(NSA_W), row(D_MODEL), _full((D_MODEL, D_MODEL))],
        out_specs=row(D_MODEL),
        out_shape=jax.ShapeDtypeStruct((t, D_MODEL), F32),
        compiler_params=_params("arbitrary"),
        name="odd_out",
    )(o_moba, o_nsa, h2, w_out)


def _even_layer(h2, batch, seq, norm, w_in, w_out, lam_re, lam_im, log_dt, b_re, b_im, c_re, c_im, d, w_glu, conv_w, conv_b):
    u, ub, yb = _even_in(h2, norm.reshape(1, D_MODEL), w_in.astype(BF16), conv_w, conv_b.reshape(1, CONV_WIDTH), seq)
    ypre = _s5_mixer_pre(ub, batch, seq, lam_re, lam_im, log_dt, b_re, b_im, c_re, c_im)
    return _even_out(ypre, u, yb, h2, d.reshape(1, S5_WIDTH), w_glu.astype(BF16), w_out.astype(BF16))


def _odd_layer(h2, batch, seq, norm, w_in, w_out, moba_q_norm, moba_k_norm, nsa_q_norm, nsa_kcmp_norm, nsa_ksel_norm,
               nsa_kwin_norm, cmp_pe_k, cmp_w1_k, cmp_w2_k, cmp_pe_v, cmp_w1_v, cmp_w2_v):
    qm, kam, kmean, vam, qs, kvc, kas, vas, kvw, gates = _odd_in(
        h2, batch, seq, norm.reshape(1, D_MODEL), w_in, moba_q_norm, moba_k_norm, nsa_q_norm, nsa_ksel_norm, nsa_kwin_norm)
    cmp = _compress(kvc, batch, seq, cmp_pe_k, cmp_w1_k, cmp_w2_k, cmp_pe_v, cmp_w1_v, cmp_w2_v, nsa_kcmp_norm)
    o_moba = _moba(qm, kam, kmean, vam, batch, seq)
    o_nsa = _nsa(qs, cmp[0], cmp[1], kas, vas, kvw, gates, batch, seq)
    return _odd_out(o_moba, o_nsa, h2, w_out.astype(BF16), seq)


def kernel(x, ev_norm_mix, ev_w_in, ev_w_out, s5_lam_re, s5_lam_im, s5_log_dt, s5_b_re, s5_b_im, s5_c_re, s5_c_im, s5_d, s5_w_glu, conv_w, conv_b, od_norm_mix, od_w_in, od_w_out, moba_q_norm, moba_k_norm, nsa_q_norm, nsa_kcmp_norm, nsa_ksel_norm, nsa_kwin_norm, cmp_pe_k, cmp_w1_k, cmp_w2_k, cmp_pe_v, cmp_w1_v, cmp_w2_v, moe_norm, moe_w_group, moe_b_group, moe_w_expert, moe_b_expert, moe_w_gate, moe_w_up, moe_w_down):
    batch, seq, _ = x.shape
    depth = moe_norm.shape[0]
    h = x.reshape(batch * seq, D_MODEL)
    for layer in range(depth):
        i = layer // 2
        if layer % 2 == 0:
            h = _even_layer(h, batch, seq, ev_norm_mix[i], ev_w_in[i], ev_w_out[i], s5_lam_re[i], s5_lam_im[i], s5_log_dt[i],
                            s5_b_re[i], s5_b_im[i], s5_c_re[i], s5_c_im[i], s5_d[i], s5_w_glu[i], conv_w[i], conv_b[i])
        else:
            h = _odd_layer(h, batch, seq, od_norm_mix[i], od_w_in[i], od_w_out[i], moba_q_norm[i], moba_k_norm[i],
                           nsa_q_norm[i], nsa_kcmp_norm[i], nsa_ksel_norm[i], nsa_kwin_norm[i], cmp_pe_k[i], cmp_w1_k[i],
                           cmp_w2_k[i], cmp_pe_v[i], cmp_w1_v[i], cmp_w2_v[i])
        h = _moe(h, moe_norm[layer].reshape(1, D_MODEL), moe_w_group[layer], moe_b_group[layer], moe_w_expert[layer],
                 moe_b_expert[layer], moe_w_gate[layer], moe_w_up[layer], moe_w_down[layer])
    return h.reshape(batch, seq, D_MODEL)
```

```python
import functools
import math

import jax
import jax.numpy as jnp
import numpy as np
from jax import lax
from jax.experimental import pallas as pl
from jax.experimental.pallas import tpu as pltpu

D_MODEL = 1024
HEAD_DIM = 64
EPS = 1e-6
S5_WIDTH = 256
S5_GROUP = 16
S5_GROUPS = 16
S5_STATE = 64
S5_CHUNK = 16
S5_GROUP_BITS = 4
S5_STATE_BITS = 6
S5_ROW_TILE = 512
S5_COL_TILE = 512
S5_SCAN_TILE = 256
CONV_WIDTH = 768
CONV_K = 3
MOBA_HEADS = 4
NSA_HEADS = 12
NSA_KV_HEADS = 2
NSA_GROUP = 6
MOBA_W = 256
NSA_W = 768
KV_W = 128
MOBA_BLOCK = 256
MOBA_TOPK = 3
CMP_BLOCK = 32
CMP_STRIDE = 16
CMP_HIDDEN = 256
SEL_BLOCK = 64
SEL_TOPK = 8
WINDOW = 512
N_GROUPS = 4
EXPERTS_PER_GROUP = 4
N_EXPERTS = 16
EXPERT_FF = 256

LANES = 128
SUBLANES = 8
VMEM_LIMIT_BYTES = 56 * 1024 * 1024
ROW_TILE = 1024
GATE_LANES = LANES
NEG = -float(2 ** 100)
F32 = jnp.float32
BF16 = jnp.bfloat16


def _params(*semantics):
    return pltpu.CompilerParams(dimension_semantics=semantics, vmem_limit_bytes=VMEM_LIMIT_BYTES)


def _dot(a, b):
    return jnp.dot(a, b, preferred_element_type=F32)


def _dot_nt(a, b):
    return lax.dot_general(a, b, (((1,), (1,)), ((), ())), preferred_element_type=F32)


def _split(x):
    hi = x.astype(BF16)
    lo = (x - hi.astype(F32)).astype(BF16)
    return hi, lo


def _dot_x2(x, w):
    hi, lo = _split(x)
    return _dot(hi, w) + _dot(lo, w)


def _dot_x3(x, w_hi, w_lo):
    hi, lo = _split(x)
    return _dot(hi, w_hi) + (_dot(hi, w_lo) + _dot(lo, w_hi))


def _rms(x, gain):
    return x * lax.rsqrt(jnp.mean(x * x, axis=-1, keepdims=True) + EPS) * gain


def _gelu(x):
    return 0.5 * x * (1.0 + jnp.tanh(math.sqrt(2.0 / math.pi) * (x + 0.044715 * (x * x * x))))


def _sigmoid(x):
    return 1.0 / (1.0 + jnp.exp(-x))


def _full(shape):
    n = len(shape)
    return pl.BlockSpec(shape, lambda *_: (0,) * n)


def _even_in_kernel(tiles_per_seq, x_ref, g_ref, w_ref, cw_ref, cb_ref, u_ref, ub_ref, yb_ref, carry_ref):
    i = pl.program_id(0)
    xn = _rms(x_ref[...], g_ref[...]).astype(BF16)
    u = _dot(xn, w_ref[:, 0:S5_WIDTH])
    u_ref[...] = u
    ub_ref[...] = u.astype(BF16)
    o = S5_WIDTH
    xc = _dot(xn, w_ref[:, o:o + CONV_WIDTH])
    gb = _dot(xn, w_ref[:, o + CONV_WIDTH:o + 2 * CONV_WIDTH])
    gc = _dot(xn, w_ref[:, o + 2 * CONV_WIDTH:o + 3 * CONV_WIDTH])
    z = gc * xc
    tm = z.shape[0]

    @pl.when(i % tiles_per_seq == 0)
    def _():
        carry_ref[...] = jnp.zeros_like(carry_ref)

    row = lax.broadcasted_iota(jnp.int32, z.shape, 0)
    prev1 = carry_ref[SUBLANES - 1:SUBLANES, :]
    prev2 = carry_ref[SUBLANES - 2:SUBLANES - 1, :]
    z1 = jnp.where(row == 0, prev1, pltpu.roll(z, 1, 0))
    z2 = jnp.where(row == 0, prev2, jnp.where(row == 1, prev1, pltpu.roll(z, 2, 0)))
    y = cw_ref[0:1, :] * z2 + cw_ref[1:2, :] * z1 + cw_ref[2:3, :] * z + cb_ref[...]
    yb_ref[...] = (gb * y).astype(BF16)
    carry_ref[...] = z[tm - SUBLANES:tm, :]


def _even_in(x2, gain, w_in, conv_w, conv_b, seq, tm=ROW_TILE):
    t = x2.shape[0]
    n_in = w_in.shape[1]
    return pl.pallas_call(
        functools.partial(_even_in_kernel, seq // tm),
        grid=(t // tm,),
        in_specs=[pl.BlockSpec((tm, D_MODEL), lambda i: (i, 0)), _full((1, D_MODEL)),
                  _full((D_MODEL, n_in)), _full((CONV_K, CONV_WIDTH)), _full((1, CONV_WIDTH))],
        out_specs=[pl.BlockSpec((tm, S5_WIDTH), lambda i: (i, 0)), pl.BlockSpec((tm, S5_WIDTH), lambda i: (i, 0)),
                   pl.BlockSpec((tm, CONV_WIDTH), lambda i: (i, 0))],
        out_shape=[jax.ShapeDtypeStruct((t, S5_WIDTH), F32), jax.ShapeDtypeStruct((t, S5_WIDTH), BF16),
                   jax.ShapeDtypeStruct((t, CONV_WIDTH), BF16)],
        scratch_shapes=[pltpu.VMEM((SUBLANES, CONV_WIDTH), F32)],
        compiler_params=_params("arbitrary"),
        name="even_in",
    )(x2, gain, w_in, conv_w, conv_b)


def _s5_weights(lam_re, lam_im, log_dt, b_re, b_im, c_re, c_im):
    g, p, hg, ck = S5_GROUPS, S5_STATE, S5_GROUP, S5_CHUNK
    lr, li = lam_re.astype(F32), lam_im.astype(F32)
    dt = jnp.exp(log_dt.astype(F32))[:, None]
    mag = jnp.exp(lr * dt)
    a_re, a_im = mag * jnp.cos(li * dt), mag * jnp.sin(li * dt)
    den = lr * lr + li * li
    f_re = ((a_re - 1.0) * lr + a_im * li) / den
    f_im = (a_im * lr - (a_re - 1.0) * li) / den
    br, bi = b_re.astype(F32), b_im.astype(F32)
    bb_re = f_re[..., None] * br - f_im[..., None] * bi
    bb_im = f_re[..., None] * bi + f_im[..., None] * br
    pw_re, pw_im = [jnp.ones_like(a_re)], [jnp.zeros_like(a_im)]
    for _ in range(ck):
        r, m = pw_re[-1], pw_im[-1]
        pw_re.append(r * a_re - m * a_im)
        pw_im.append(r * a_im + m * a_re)
    pw_re, pw_im = jnp.stack(pw_re), jnp.stack(pw_im)
    cr, ci = c_re.astype(F32), c_im.astype(F32)
    rev_re, rev_im = pw_re[ck - 1::-1][:ck], pw_im[ck - 1::-1][:ck]
    ws_re = rev_re[:, :, :, None] * bb_re[None] - rev_im[:, :, :, None] * bb_im[None]
    ws_im = rev_re[:, :, :, None] * bb_im[None] + rev_im[:, :, :, None] * bb_re[None]
    ca_re = cr[None] * pw_re[1:, :, None, :] - ci[None] * pw_im[1:, :, None, :]
    ca_im = cr[None] * pw_im[1:, :, None, :] + ci[None] * pw_re[1:, :, None, :]
    cb_re = jnp.einsum('ghp,kgp,gpj->kghj', cr, pw_re[:ck], bb_re) - jnp.einsum('ghp,kgp,gpj->kghj', cr, pw_im[:ck], bb_im) \
        - jnp.einsum('ghp,kgp,gpj->kghj', ci, pw_re[:ck], bb_im) - jnp.einsum('ghp,kgp,gpj->kghj', ci, pw_im[:ck], bb_re)
    lag = np.arange(ck)[None, :] - np.arange(ck)[:, None]
    tz = cb_re[np.clip(lag, 0, ck - 1)]
    tz = jnp.where((lag >= 0)[:, :, None, None, None], tz, 0.0)
    cw = ck * g * hg
    ws = jnp.stack([ws_re, ws_im]).transpose(1, 2, 4, 0, 3).reshape(cw, 2 * p)
    wc = jnp.stack([ca_re, -ca_im]).transpose(0, 2, 4, 1, 3).reshape(2 * g * p, ck * hg)
    tzc = tz.transpose(0, 2, 4, 1, 3).reshape(cw, ck * hg)
    return ws.astype(BF16), wc.astype(BF16), tzc.astype(BF16), pw_re[ck].reshape(1, g * p), pw_im[ck].reshape(1, g * p)


def _group_expand(compact, expand, row_shift, col_shift, col0):
    full = _dot(compact, expand)
    row = lax.broadcasted_iota(jnp.int32, full.shape, 0)
    col = col0 + lax.broadcasted_iota(jnp.int32, full.shape, 1)
    same = ((row >> row_shift) & (S5_GROUPS - 1)) == ((col >> col_shift) & (S5_GROUPS - 1))
    return jnp.where(same, full, 0.0).astype(BF16)


def _s5_state_kernel(u_ref, ws_ref, e_ref, s_ref, w_scr):
    @pl.when(pl.program_id(1) == 0)
    def _():
        w_scr[...] = _group_expand(ws_ref[...], e_ref[...], S5_GROUP_BITS, S5_STATE_BITS,
                                   pl.program_id(0) * w_scr.shape[1])

    s_ref[...] = _dot(u_ref[...], w_scr[...])


def _s5_scan_kernel(s_ref, are_ref, aim_ref, xprev_ref, st_ref):
    @pl.when(pl.program_id(0) == 0)
    def _():
        st_ref[...] = jnp.zeros_like(st_ref)

    a_re, a_im = are_ref[...], aim_ref[...]
    nb, n = s_ref.shape[0], s_ref.shape[1]
    half = a_re.shape[1]

    def body(c, carry):
        out = []
        for b in range(nb):
            xr, xi = carry[2 * b], carry[2 * b + 1]
            xprev_ref[b, pl.ds(c, 1), 0:half] = xr
            xprev_ref[b, pl.ds(c, 1), half:2 * half] = xi
            s = s_ref[b, pl.ds(c, 1), :]
            out += [a_re * xr - a_im * xi + s[:, 0:half], a_re * xi + a_im * xr + s[:, half:2 * half]]
        return tuple(out)

    init = tuple(st_ref[b:b + 1, o:o + half] for b in range(nb) for o in (0, half))
    final = lax.fori_loop(0, n, body, init, unroll=4)
    for b in range(nb):
        st_ref[b:b + 1, 0:half] = final[2 * b]
        st_ref[b:b + 1, half:2 * half] = final[2 * b + 1]


def _s5_out_kernel(u_ref, xp_ref, tz_ref, wc_ref, e_ref, y_ref, tz_scr, wc_scr):
    @pl.when(pl.program_id(1) == 0)
    def _():
        col0 = pl.program_id(0) * tz_scr.shape[1]
        tz_scr[...] = _group_expand(tz_ref[...], e_ref[...], S5_GROUP_BITS, S5_GROUP_BITS, col0)
        wc_scr[...] = _group_expand(wc_ref[...], e_ref[...], S5_STATE_BITS, S5_GROUP_BITS, col0)

    y_ref[...] = _dot(u_ref[...], tz_scr[...]) + _dot(xp_ref[...].astype(BF16), wc_scr[...])


def _expand_matrix(outer, inner):
    e = np.zeros((outer, inner, outer, S5_GROUPS, inner), np.float32)
    for x in range(outer):
        for y in range(inner):
            e[x, y, x, :, y] = 1.0
    return jnp.asarray(e.reshape(outer * inner, outer * S5_GROUPS * inner), BF16)


def _s5_mixer_pre(ub, batch, seq, lam_re, lam_im, log_dt, b_re, b_im, c_re, c_im):
    ws, wc, tz, a16_re, a16_im = _s5_weights(lam_re, lam_im, log_dt, b_re, b_im, c_re, c_im)
    nc = seq // S5_CHUNK
    rows = batch * nc
    cw = S5_CHUNK * S5_WIDTH
    sw = 2 * S5_GROUPS * S5_STATE
    assert S5_GROUP == 1 << S5_GROUP_BITS and S5_STATE == 1 << S5_STATE_BITS
    tr = min(rows, S5_ROW_TILE)
    tn = S5_COL_TILE
    uc = ub.reshape(rows, cw)
    e_state = _expand_matrix(2, S5_STATE)
    e_out = _expand_matrix(S5_CHUNK, S5_GROUP)
    s = pl.pallas_call(
        _s5_state_kernel,
        grid=(sw // tn, rows // tr),
        in_specs=[pl.BlockSpec((tr, cw), lambda j, i: (i, 0)), _full(ws.shape),
                  pl.BlockSpec((e_state.shape[0], tn), lambda j, i: (0, j))],
        out_specs=pl.BlockSpec((tr, tn), lambda j, i: (i, j)),
        out_shape=jax.ShapeDtypeStruct((rows, sw), F32),
        scratch_shapes=[pltpu.VMEM((cw, tn), BF16)],
        compiler_params=_params("arbitrary", "arbitrary"),
        name="s5_state",
    )(uc, ws, e_state)
    tc = min(nc, S5_SCAN_TILE)
    xprev = pl.pallas_call(
        _s5_scan_kernel,
        grid=(nc // tc,),
        in_specs=[pl.BlockSpec((batch, tc, sw), lambda i: (0, i, 0)), _full((1, sw // 2)), _full((1, sw // 2))],
        out_specs=pl.BlockSpec((batch, tc, sw), lambda i: (0, i, 0)),
        out_shape=jax.ShapeDtypeStruct((batch, nc, sw), F32),
        scratch_shapes=[pltpu.VMEM((batch, sw), F32)],
        compiler_params=_params("arbitrary"),
        name="s5_scan",
    )(s.reshape(batch, nc, sw), a16_re, a16_im)
    y = pl.pallas_call(
        _s5_out_kernel,
        grid=(cw // tn, rows // tr),
        in_specs=[pl.BlockSpec((tr, cw), lambda j, i: (i, 0)), pl.BlockSpec((tr, sw), lambda j, i: (i, 0)),
                  _full(tz.shape), _full(wc.shape), pl.BlockSpec((e_out.shape[0], tn), lambda j, i: (0, j))],
        out_specs=pl.BlockSpec((tr, tn), lambda j, i: (i, j)),
        out_shape=jax.ShapeDtypeStruct((rows, cw), F32),
        scratch_shapes=[pltpu.VMEM((cw, tn), BF16), pltpu.VMEM((sw, tn), BF16)],
        compiler_params=_params("arbitrary", "arbitrary"),
        name="s5_out",
    )(uc, xprev.reshape(rows, sw), tz, wc, e_out)
    return y.reshape(batch * seq, S5_WIDTH)


def _even_out_kernel(ypre_ref, u_ref, yb_ref, x_ref, d_ref, wglu_ref, wout_ref, o_ref):
    y = _gelu(ypre_ref[...] + d_ref[...] * u_ref[...])
    y = y * _sigmoid(_dot(y.astype(BF16), wglu_ref[...]))
    o_ref[...] = (x_ref[...] + _dot(y.astype(BF16), wout_ref[0:S5_WIDTH, :])
                  + _dot(yb_ref[...], wout_ref[S5_WIDTH:D_MODEL, :]))


def _even_out(ypre, u, yb, x2, d, w_glu, w_out, tm=ROW_TILE):
    t = x2.shape[0]
    row = lambda w: pl.BlockSpec((tm, w), lambda i: (i, 0))
    return pl.pallas_call(
        _even_out_kernel,
        grid=(t // tm,),
        in_specs=[row(S5_WIDTH), row(S5_WIDTH), row(CONV_WIDTH), row(D_MODEL), _full((1, S5_WIDTH)),
                  _full((S5_WIDTH, S5_WIDTH)), _full((D_MODEL, D_MODEL))],
        out_specs=row(D_MODEL),
        out_shape=jax.ShapeDtypeStruct((t, D_MODEL), F32),
        compiler_params=_params("arbitrary"),
        name="even_out",
    )(ypre, u, yb, x2, d, w_glu, w_out)


def _first_max(v, pos, width, axis=-1):
    m = jnp.max(v, axis=axis, keepdims=True)
    idx = jnp.min(jnp.where(v == m, pos, width), axis=axis, keepdims=True)
    return m, idx


MOE_TM = 1024
MOE_ALIGN = 16
MOE_SLOTS = 1152
MOE_WIN = 320
ROUTER_LANES = LANES


def _moe_router_kernel(h_ref, g_ref, wr_hi_ref, wr_lo_ref, br_ref, gate_ref, grp_ref, cnt_ref):
    xn = _rms(h_ref[...], g_ref[...])
    logits = _dot_x3(xn, wr_hi_ref[...], wr_lo_ref[...]) + br_ref[...]
    lane = lax.broadcasted_iota(jnp.int32, logits.shape, 1)
    width = logits.shape[1]
    is_g = lane < N_GROUPS
    gl = jnp.where(is_g, logits, -jnp.inf)
    gm, gi = _first_max(gl, lane, width)
    gw = 1.0 / jnp.sum(jnp.where(is_g, jnp.exp(gl - gm), 0.0), axis=-1, keepdims=True)
    lo = N_GROUPS + gi * EXPERTS_PER_GROUP
    in_grp = (lane >= lo) & (lane < lo + EXPERTS_PER_GROUP)
    el = jnp.where(in_grp, logits, -jnp.inf)
    m1, i1 = _first_max(el, lane, width)
    m2, i2 = _first_max(jnp.where(lane == i1, -jnp.inf, el), lane, width)
    p2 = jnp.exp(m2 - m1)
    w1 = gw / (1.0 + p2)
    w2 = gw * p2 / (1.0 + p2)
    gate_ref[...] = jnp.where(lane == i1, w1, 0.0) + jnp.where(lane == i2, w2, 0.0)
    grp = (lane == gi).astype(F32)
    grp_ref[...] = grp.astype(BF16)
    cnt_ref[0] = jnp.broadcast_to(jnp.sum(grp, axis=0, keepdims=True), cnt_ref.shape[1:])


def _moe_expert_kernel(base_ref, nwin_ref, h_ref, g_ref, gate_ref, grp_ref, wg_ref, wu_ref, wd_ref, o_ref,
                       xs_ref, gs_ref, ys_ref, pt_ref):
    i, g = pl.program_id(0), pl.program_id(1)
    tm = h_ref.shape[0]

    @pl.when((i == 0) & (g == 0))
    def _():
        xs_ref[...] = jnp.zeros_like(xs_ref)
        gs_ref[...] = jnp.zeros_like(gs_ref)
        ys_ref[...] = jnp.zeros_like(ys_ref)

    @pl.when(g == 0)
    def _():
        xn = _rms(h_ref[...], g_ref[...]).astype(BF16)
        grp = grp_ref[...]
        earlier = (lax.broadcasted_iota(jnp.int32, (tm, tm), 0) > lax.broadcasted_iota(jnp.int32, (tm, tm), 1)).astype(BF16)
        rank = _dot(earlier, grp)
        lane = lax.broadcasted_iota(jnp.int32, rank.shape, 1)
        for k in range(N_GROUPS):
            rank = rank + jnp.where(lane == k, base_ref[i * N_GROUPS + k].astype(F32), 0.0)
        slot = jnp.sum(grp.astype(F32) * rank, axis=-1, keepdims=True).astype(jnp.int32)
        pt = (lax.broadcasted_iota(jnp.int32, (tm, MOE_SLOTS), 1) == slot).astype(BF16)
        pt_ref[...] = pt
        gather = lambda x: lax.dot_general(pt, x, (((0,), (0,)), ((), ())), preferred_element_type=F32)
        xs_ref[0:MOE_SLOTS, :] = gather(xn).astype(BF16)
        gate_hi, gate_lo = _split(gate_ref[...])
        gs_ref[0:MOE_SLOTS, :] = gather(gate_hi) + gather(gate_lo)

    def window(w, carry):
        r0 = pl.multiple_of(base_ref[i * N_GROUPS + g] + w * MOE_WIN, MOE_ALIGN)
        x = xs_ref[pl.ds(r0, MOE_WIN), :]
        gate = gs_ref[pl.ds(r0, MOE_WIN), :]
        lane = lax.broadcasted_iota(jnp.int32, gate.shape, 1)
        y = None
        for j in range(EXPERTS_PER_GROUP):
            ge = jnp.sum(jnp.where(lane == g * EXPERTS_PER_GROUP + (j + N_GROUPS), gate, 0.0), axis=-1, keepdims=True)
            h1 = _dot(x, wg_ref[j])
            h3 = _dot(x, wu_ref[j])
            act = (h1 * _sigmoid(h1)) * h3 * ge
            yj = _dot(act.astype(BF16), wd_ref[j])
            y = yj if y is None else y + yj
        ys_ref[pl.ds(r0, MOE_WIN), :] = y.astype(BF16)
        return carry

    lax.fori_loop(0, nwin_ref[i * N_GROUPS + g], window, 0)

    @pl.when(g == N_GROUPS - 1)
    def _():
        o_ref[...] = h_ref[...] + _dot(pt_ref[...], ys_ref[0:MOE_SLOTS, :])


def _moe(h2, gain, w_group, b_group, w_expert, b_expert, w_gate, w_up, w_down):
    t = h2.shape[0]
    tm, rw = MOE_TM, ROUTER_LANES
    assert t % tm == 0 and MOE_SLOTS >= tm + N_GROUPS * (MOE_ALIGN - 1) and MOE_WIN % MOE_ALIGN == 0
    n_tiles = t // tm
    wr = jnp.zeros((D_MODEL, rw), F32).at[:, 0:N_GROUPS].set(w_group).at[:, N_GROUPS:N_GROUPS + N_EXPERTS].set(w_expert)
    br = jnp.zeros((1, rw), F32).at[0, 0:N_GROUPS].set(b_group).at[0, N_GROUPS:N_GROUPS + N_EXPERTS].set(b_expert)
    wr_hi = wr.astype(BF16)
    wr_lo = (wr - wr_hi.astype(F32)).astype(BF16)
    row = lambda width: pl.BlockSpec((tm, width), lambda i: (i, 0))
    gates, grp, cnt = pl.pallas_call(
        _moe_router_kernel,
        grid=(n_tiles,),
        in_specs=[row(D_MODEL), _full((1, D_MODEL)), _full((D_MODEL, rw)), _full((D_MODEL, rw)), _full((1, rw))],
        out_specs=[row(rw), row(rw), pl.BlockSpec((1, SUBLANES, rw), lambda i: (i, 0, 0))],
        out_shape=[jax.ShapeDtypeStruct((t, rw), F32), jax.ShapeDtypeStruct((t, rw), BF16),
                   jax.ShapeDtypeStruct((n_tiles, SUBLANES, rw), F32)],
        compiler_params=_params("arbitrary"),
        name="moe_router",
    )(h2, gain, wr_hi, wr_lo, br)
    n = cnt[:, 0, 0:N_GROUPS].astype(jnp.int32)
    padded = (n + (MOE_ALIGN - 1)) // MOE_ALIGN * MOE_ALIGN
    base = (jnp.cumsum(padded, axis=1) - padded).reshape(-1)
    nwin = ((padded + (MOE_WIN - 1)) // MOE_WIN).reshape(-1)
    tile = lambda width: pl.BlockSpec((tm, width), lambda i, g, *_: (i, 0))
    experts = lambda shape: pl.BlockSpec((EXPERTS_PER_GROUP,) + shape, lambda i, g, *_: (g, 0, 0))
    slots = MOE_SLOTS + MOE_WIN
    return pl.pallas_call(
        _moe_expert_kernel,
        grid_spec=pltpu.PrefetchScalarGridSpec(
            num_scalar_prefetch=2,
            grid=(n_tiles, N_GROUPS),
            in_specs=[tile(D_MODEL), pl.BlockSpec((1, D_MODEL), lambda i, g, *_: (0, 0)), tile(rw), tile(rw),
                      experts((D_MODEL, EXPERT_FF)), experts((D_MODEL, EXPERT_FF)), experts((EXPERT_FF, D_MODEL))],
            out_specs=tile(D_MODEL),
            scratch_shapes=[pltpu.VMEM((slots, D_MODEL), BF16), pltpu.VMEM((slots, rw), F32),
                            pltpu.VMEM((slots, D_MODEL), BF16), pltpu.VMEM((tm, MOE_SLOTS), BF16)]),
        out_shape=jax.ShapeDtypeStruct((t, D_MODEL), F32),
        compiler_params=_params("arbitrary", "arbitrary"),
        name="moe",
    )(base, nwin, h2, gain, gates, grp, w_gate.astype(BF16), w_up.astype(BF16), w_down.astype(BF16))


ODD_SPLITS = (MOBA_W, MOBA_W, MOBA_W, NSA_W, KV_W, KV_W, KV_W, KV_W, KV_W, KV_W, GATE_LANES)
ODD_IN_PAD = sum(ODD_SPLITS)


def _head_rms(x, hsum, gain):
    w = x.shape[1]
    ss = jnp.concatenate([_dot_x2(x[:, o:o + hsum.shape[0]] * x[:, o:o + hsum.shape[0]], hsum)
                          for o in range(0, w, hsum.shape[0])], axis=1) if w > hsum.shape[0] else _dot_x2(x * x, hsum)
    return x * lax.rsqrt(ss * (1.0 / HEAD_DIM) + EPS) * gain


def _odd_in_kernel(tiles_per_seq, x_ref, g_ref, w_ref, hsum_ref, gq_ref, gk_ref, gnq_ref, gks_ref, gkw_ref,
                   qm_ref, kam_ref, kmean_ref, vam_ref, qs_ref, kvc_ref, kas_ref, vas_ref, kvw_ref, gt_ref):
    xn = _rms(x_ref[...], g_ref[...]).astype(BF16)
    offs = np.cumsum((0,) + ODD_SPLITS)
    col = lambda j: _dot(xn, w_ref[:, int(offs[j]):int(offs[j + 1])])
    head = lambda x, h: x[:, h * HEAD_DIM:(h + 1) * HEAD_DIM]
    hsum = hsum_ref[...]
    hsum128 = hsum_ref[0:KV_W, 0:KV_W]
    tm = x_ref.shape[0]
    pos = (pl.program_id(0) % tiles_per_seq) * tm + lax.broadcasted_iota(jnp.int32, (tm, HEAD_DIM), 0)
    lane = lax.broadcasted_iota(jnp.int32, (tm, HEAD_DIM), 1)
    ones_col = (lane == 0).astype(BF16)

    qm = _head_rms(col(0), hsum, gq_ref[...])
    km = _head_rms(col(1), hsum, gk_ref[...])
    for j in range(tm // MOBA_BLOCK):
        kmean_ref[0, j:j + 1, :] = jnp.mean(km[j * MOBA_BLOCK:(j + 1) * MOBA_BLOCK, :], axis=0, keepdims=True)
    km = km.astype(BF16)
    vm = col(2).astype(BF16)
    moba_id = (lane == pos // MOBA_BLOCK).astype(BF16)
    for h in range(MOBA_HEADS):
        qm_ref[0, h] = head(qm, h)
        kam_ref[0, h] = jnp.concatenate([head(km, h), moba_id], axis=1)
        vam_ref[0, h] = jnp.concatenate([head(vm, h), ones_col], axis=1)

    qd = (_head_rms(col(3), hsum, gnq_ref[...]) * (HEAD_DIM ** -0.5)).astype(BF16)
    for hk in range(NSA_KV_HEADS):
        for j in range(tm // NSA_TQ):
            for g in range(NSA_GROUP):
                qs_ref[0, hk, j, g * NSA_TQ:(g + 1) * NSA_TQ, :] = head(qd, hk * NSA_GROUP + g)[j * NSA_TQ:(j + 1) * NSA_TQ, :]
    kvc_ref[0] = col(4)
    kvc_ref[1] = col(5)
    ks = _head_rms(col(6), hsum128, gks_ref[...]).astype(BF16)
    vs = col(7).astype(BF16)
    kw = _head_rms(col(8), hsum128, gkw_ref[...]).astype(BF16)
    vw = col(9).astype(BF16)
    sel_id = (lane == (pos // SEL_BLOCK) % SEL_LANES).astype(BF16)
    for hk in range(NSA_KV_HEADS):
        kas_ref[0, hk] = jnp.concatenate([head(ks, hk), sel_id], axis=1)
        vas_ref[0, hk] = jnp.concatenate([head(vs, hk), ones_col], axis=1)
        kvw_ref[0, hk] = jnp.concatenate([head(kw, hk), head(vw, hk)], axis=1)
    gt_ref[...] = _sigmoid(col(10))


def _odd_in(h2, batch, seq, gain, w_in, moba_q_norm, moba_k_norm, nsa_q_norm, nsa_ksel_norm, nsa_kwin_norm, tm=ROW_TILE):
    t = h2.shape[0]
    assert MOBA_LANES == HEAD_DIM and SEL_LANES == HEAD_DIM and seq % tm == 0 and tm % MOBA_BLOCK == 0
    tps = seq // tm
    w = jnp.pad(w_in, ((0, 0), (0, ODD_IN_PAD - w_in.shape[1]))).astype(BF16)
    hsum = jnp.asarray(np.kron(np.eye(MOBA_W // HEAD_DIM), np.ones((HEAD_DIM, HEAD_DIM))), BF16)
    tile = lambda g, width: jnp.tile(g.astype(F32), width // HEAD_DIM).reshape(1, width)
    row = lambda width: pl.BlockSpec((tm, width), lambda i: (i, 0))
    heads = lambda n, width: pl.BlockSpec((1, n, tm, width), lambda i: (i // tps, 0, i % tps, 0))
    nmb = tm // MOBA_BLOCK
    nqt = tm // NSA_TQ
    rows = NSA_GROUP * NSA_TQ
    sds = jax.ShapeDtypeStruct
    out_specs = [heads(MOBA_HEADS, HEAD_DIM), heads(MOBA_HEADS, LANES), pl.BlockSpec((1, nmb, MOBA_W), lambda i: (i, 0, 0)),
                 heads(MOBA_HEADS, LANES),
                 pl.BlockSpec((1, NSA_KV_HEADS, nqt, rows, HEAD_DIM), lambda i: (i // tps, 0, i % tps, 0, 0)),
                 pl.BlockSpec((2, tm, KV_W), lambda i: (0, i, 0)),
                 heads(NSA_KV_HEADS, LANES), heads(NSA_KV_HEADS, LANES), heads(NSA_KV_HEADS, LANES),
                 row(GATE_LANES)]
    out_shape = [sds((batch, MOBA_HEADS, seq, HEAD_DIM), F32), sds((batch, MOBA_HEADS, seq, LANES), BF16),
                 sds((t // tm, nmb, MOBA_W), F32), sds((batch, MOBA_HEADS, seq, LANES), BF16),
                 sds((batch, NSA_KV_HEADS, seq // NSA_TQ, rows, HEAD_DIM), BF16),
                 sds((2, t, KV_W), F32), sds((batch, NSA_KV_HEADS, seq, LANES), BF16),
                 sds((batch, NSA_KV_HEADS, seq, LANES), BF16), sds((batch, NSA_KV_HEADS, seq, LANES), BF16),
                 sds((t, GATE_LANES), F32)]
    return pl.pallas_call(
        functools.partial(_odd_in_kernel, tps),
        grid=(t // tm,),
        in_specs=[row(D_MODEL), _full((1, D_MODEL)), _full((D_MODEL, ODD_IN_PAD)), _full((MOBA_W, MOBA_W)),
                  _full((1, MOBA_W)), _full((1, MOBA_W)), _full((1, NSA_W)), _full((1, KV_W)), _full((1, KV_W))],
        out_specs=out_specs,
        out_shape=out_shape,
        compiler_params=_params("arbitrary"),
        name="odd_in",
    )(h2, gain, w, hsum, tile(moba_q_norm, MOBA_W), tile(moba_k_norm, MOBA_W), tile(nsa_q_norm, NSA_W),
      tile(nsa_ksel_norm, KV_W), tile(nsa_kwin_norm, KV_W))


def _compress_kernel(c_ref, w1_ref, w2_ref, pe_ref, g_ref, o_ref):
    kind = pl.program_id(0)
    n16 = c_ref.shape[1] // CMP_STRIDE
    half = CMP_STRIDE * HEAD_DIM
    peb = _dot(pe_ref[0], w1_ref[0])[0:1, :]
    xs = [c_ref[0, pl.ds(s, n16, stride=CMP_STRIDE), :].astype(BF16) for s in range(CMP_STRIDE)]
    for h in range(NSA_KV_HEADS):
        first = second = None
        for s in range(CMP_STRIDE):
            x = xs[s][:, h * HEAD_DIM:(h + 1) * HEAD_DIM]
            a = _dot(x, w1_ref[0, s * HEAD_DIM:(s + 1) * HEAD_DIM, :])
            b = _dot(x, w1_ref[0, half + s * HEAD_DIM:half + (s + 1) * HEAD_DIM, :])
            first = a if first is None else first + a
            second = b if second is None else second + b
        hid = _gelu(first + pltpu.roll(second, n16 - 1, 0) + peb)
        out = _dot(hid.astype(BF16), w2_ref[0])
        o_ref[0, 0, h] = jnp.where(kind == 0, _rms(out, g_ref[...]), out).astype(BF16)


def _compress(kvc, batch, seq, pe_k, w1_k, w2_k, pe_v, w1_v, w2_v, kcmp_norm):
    n16 = seq // CMP_STRIDE
    half = CMP_STRIDE * HEAD_DIM
    w1 = jnp.stack([w1_k, w1_v]).astype(BF16)
    w2 = jnp.stack([w2_k, w2_v]).astype(BF16)
    pe = jnp.stack([pe_k, pe_v]).reshape(2, 1, 2 * half)
    pe = jnp.broadcast_to(pe, (2, SUBLANES, 2 * half)).astype(BF16)
    return pl.pallas_call(
        _compress_kernel,
        grid=(2, batch),
        in_specs=[pl.BlockSpec((1, seq, KV_W), lambda k, b: (k, b, 0)),
                  pl.BlockSpec((1, 2 * half, CMP_HIDDEN), lambda k, b: (k, 0, 0)),
                  pl.BlockSpec((1, CMP_HIDDEN, HEAD_DIM), lambda k, b: (k, 0, 0)),
                  pl.BlockSpec((1, SUBLANES, 2 * half), lambda k, b: (k, 0, 0)),
                  _full((1, HEAD_DIM))],
        out_specs=pl.BlockSpec((1, 1, NSA_KV_HEADS, n16, HEAD_DIM), lambda k, b: (k, b, 0, 0, 0)),
        out_shape=jax.ShapeDtypeStruct((2, batch, NSA_KV_HEADS, n16, HEAD_DIM), BF16),
        compiler_params=_params("arbitrary", "arbitrary"),
        name="nsa_compress",
    )(kvc, w1, w2, pe, kcmp_norm.astype(F32).reshape(1, HEAD_DIM))


M_INIT = -1e30


def _softmax_init(m_ref, acc_ref):
    m_ref[...] = jnp.full(m_ref.shape, M_INIT, F32)
    acc_ref[...] = jnp.zeros(acc_ref.shape, F32)


def _softmax_step(s, v_aug, m_ref, acc_ref):
    m_old = m_ref[...]
    m_new = jnp.maximum(m_old, jnp.max(s, axis=-1, keepdims=True))
    alpha = jnp.exp(m_old - m_new)
    p = jnp.exp(s - jnp.tile(m_new, (1, s.shape[1] // LANES)))
    acc_ref[...] = alpha * acc_ref[...] + _dot(p.astype(BF16), v_aug)
    m_ref[...] = m_new


def _past_keys_loop(n_keys, tile, step, riders=(), parts=2):
    n_full = n_keys // tile

    def body(j, carry):
        step(pl.multiple_of(j * tile, tile), tile)
        return carry

    extras = []
    for k, (with_tile, alone) in enumerate(riders):
        def both(with_tile=with_tile, k=k):
            out = with_tile()
            step(k * tile, tile)
            return out

        extras.append(lax.cond(n_full > k, both, alone))
    lax.fori_loop(len(riders), n_full, body, 0)
    rest = n_keys - n_full * tile
    start = pl.multiple_of(n_full * tile, tile)
    part = tile // parts
    n_parts = (rest + part - 1) // part
    for k in range(1, parts + 1):
        @pl.when(n_parts == k)
        def _(k=k):
            done = 0
            for size in (tile, tile // 2, tile // 4):
                if size % part == 0 and k * part - done >= size:
                    step(pl.multiple_of(start + done, part), size)
                    done += size
            assert done == k * part

    return extras


def _softmax_result(acc_ref):
    acc = acc_ref[...]
    return acc[:, 0:HEAD_DIM] * (1.0 / acc[:, HEAD_DIM:HEAD_DIM + 1])


def _pick_top(score, pos, width, k):
    sel = jnp.zeros(score.shape, jnp.bool_)
    for _ in range(k):
        m, idx = _first_max(score, pos, width, axis=0)
        hit = (pos == idx) & (m > -jnp.inf)
        sel = sel | hit
        score = jnp.where(pos == idx, -jnp.inf, score)
    return sel


MOBA_LANES = 64
MOBA_TK = 2048


MOBA_TQ = 4 * MOBA_BLOCK


def _moba_kernel(q_ref, ka_ref, va_ref, kmean_ref, o_ref, qa_ref, m_ref, acc_ref):
    i0 = pl.program_id(2) * (MOBA_TQ // MOBA_BLOCK)
    q = q_ref[0, 0]
    q_hi, q_lo = _split(q)
    km_hi, km_lo = _split(kmean_ref[0, 0])
    gate = _dot_nt(km_hi, q_hi) + (_dot_nt(km_lo, q_hi) + _dot_nt(km_hi, q_lo))
    blk = lax.broadcasted_iota(jnp.int32, gate.shape, 0)
    cur = i0 + lax.broadcasted_iota(jnp.int32, gate.shape, 1) // MOBA_BLOCK
    sel = _pick_top(jnp.where(blk < cur, gate, -jnp.inf), blk, gate.shape[0], MOBA_TOPK)
    qs = (q * (HEAD_DIM ** -0.5)).astype(BF16)
    past = jnp.where(sel & (blk < i0), 0.0, NEG).T[:, 0:MOBA_LANES]
    own = jnp.where((blk == cur) | (sel & (blk >= i0)), 0.0, NEG).T[:, 0:MOBA_LANES]
    qa_ref[0] = jnp.concatenate([qs, past.astype(BF16)], axis=1)
    qa_ref[1] = jnp.concatenate([qs, own.astype(BF16)], axis=1)
    _softmax_init(m_ref, acc_ref)

    def step(start, size):
        s = _dot_nt(qa_ref[0], ka_ref[0, 0, pl.ds(start, size), :])
        _softmax_step(s, va_ref[0, 0, pl.ds(start, size), :], m_ref, acc_ref)

    _past_keys_loop(i0 * MOBA_BLOCK, MOBA_TK, step)
    start = pl.multiple_of(i0 * MOBA_BLOCK, MOBA_TQ)
    s = _dot_nt(qa_ref[1], ka_ref[0, 0, pl.ds(start, MOBA_TQ), :])
    qpos = lax.broadcasted_iota(jnp.int32, s.shape, 0)
    kpos = lax.broadcasted_iota(jnp.int32, s.shape, 1)
    hidden = (qpos // MOBA_BLOCK == kpos // MOBA_BLOCK) & (kpos > qpos)
    _softmax_step(jnp.where(hidden, NEG, s), va_ref[0, 0, pl.ds(start, MOBA_TQ), :], m_ref, acc_ref)
    o_ref[0, 0] = _softmax_result(acc_ref).astype(BF16)


def _moba(qm, ka, kmean, va, batch, seq):
    nmb = seq // MOBA_BLOCK
    assert nmb <= MOBA_LANES and seq % MOBA_TK == 0 and seq % MOBA_TQ == 0
    kmean = kmean.reshape(batch, nmb, MOBA_HEADS, HEAD_DIM).transpose(0, 2, 1, 3)
    kmean = jnp.pad(kmean, ((0, 0), (0, 0), (0, LANES - nmb), (0, 0)))
    return pl.pallas_call(
        _moba_kernel,
        grid=(batch, MOBA_HEADS, seq // MOBA_TQ),
        in_specs=[pl.BlockSpec((1, 1, MOBA_TQ, HEAD_DIM), lambda b, h, i: (b, h, i, 0)),
                  pl.BlockSpec((1, 1, seq, LANES), lambda b, h, i: (b, h, 0, 0)),
                  pl.BlockSpec((1, 1, seq, LANES), lambda b, h, i: (b, h, 0, 0)),
                  pl.BlockSpec((1, 1, LANES, HEAD_DIM), lambda b, h, i: (b, h, 0, 0))],
        out_specs=pl.BlockSpec((1, 1, MOBA_TQ, HEAD_DIM), lambda b, h, i: (b, h, i, 0)),
        out_shape=jax.ShapeDtypeStruct((batch, MOBA_HEADS, seq, HEAD_DIM), BF16),
        scratch_shapes=[pltpu.VMEM((2, MOBA_TQ, LANES), BF16), pltpu.VMEM((MOBA_TQ, LANES), F32),
                        pltpu.VMEM((MOBA_TQ, LANES), F32)],
        compiler_params=_params("arbitrary", "arbitrary", "arbitrary"),
        name="moba",
    )(qm, ka, va, kmean)


NSA_TQ = 2 * SEL_BLOCK
NSA_TK = 2048
SEL_LANES = 64
SUPER_KEYS = SEL_LANES * SEL_BLOCK
CMP_WIDTH_STEP = 256


def _nsa_kernel(n_super, q_ref, kc_ref, vc_ref, ka_ref, va_ref, kvw_ref, gt_ref, e_ref, band_ref, o_ref,
                qa_ref, m_ref, acc_ref):
    qi = pl.program_id(2)
    tq = NSA_TQ
    s0 = qi * tq
    q = q_ref[0, 0, 0]

    def compressed(width):
        sc = _dot_nt(q, kc_ref[0, 0, 0:width, :])
        rq = lax.broadcasted_iota(jnp.int32, (sc.shape[0], LANES), 0) & (tq - 1)
        last = jnp.tile((s0 + rq - (CMP_BLOCK - 1)) >> (CMP_STRIDE.bit_length() - 1), (1, width // LANES))
        sc = jnp.where(lax.broadcasted_iota(jnp.int32, sc.shape, 1) <= last, sc, NEG)
        mx = jnp.max(sc, axis=-1, keepdims=True)
        pc = jnp.exp(sc - mx)
        pc = pc * jnp.where(mx > 0.5 * NEG, 1.0 / jnp.sum(pc, axis=-1, keepdims=True), 0.0)
        imp = pc[0:tq]
        for g in range(1, NSA_GROUP):
            imp = imp + pc[g * tq:(g + 1) * tq]
        return _dot(pc.astype(BF16), vc_ref[0, 0, 0:width, :]), _dot_x2(imp, band_ref[0:width, :])

    n16 = kc_ref.shape[2]
    widths = list(range(CMP_WIDTH_STEP, n16, CMP_WIDTH_STEP)) + [n16]
    o_c, pslc = lax.switch((s0 + tq - 1) // (CMP_WIDTH_STEP * CMP_STRIDE),
                           [functools.partial(compressed, wd) for wd in widths])

    wlen = WINDOW + tq

    def window(kvw, masked):
        sw = masked(_dot_nt(q, kvw[:, 0:HEAD_DIM]))
        pw = jnp.exp(sw - jnp.max(sw, axis=-1, keepdims=True))
        return _dot(pw.astype(BF16), kvw)[:, HEAD_DIM:2 * HEAD_DIM] * (1.0 / jnp.sum(pw, axis=-1, keepdims=True))

    def window_interior():
        def masked(sw):
            rq = lax.broadcasted_iota(jnp.int32, (sw.shape[0], tq), 0) & (tq - 1)
            c = lax.broadcasted_iota(jnp.int32, (sw.shape[0], tq), 1)
            return jnp.concatenate([jnp.where(c > rq, sw[:, 0:tq], NEG), sw[:, tq:WINDOW],
                                    jnp.where(c <= rq, sw[:, WINDOW:wlen], NEG)], axis=1)

        return window(kvw_ref[0, 0, pl.ds(pl.multiple_of(s0 - WINDOW, tq), wlen), :], masked)

    def window_start():
        def masked(sw):
            kabs = lax.broadcasted_iota(jnp.int32, sw.shape, 1)
            t = s0 + (lax.broadcasted_iota(jnp.int32, sw.shape, 0) & (tq - 1))
            return jnp.where((kabs <= t) & (kabs > t - WINDOW), sw, NEG)

        return window(kvw_ref[0, 0, 0:wlen, :], masked)

    pslc_t = pslc.T
    nb = pslc_t.shape[0]
    blk = lax.broadcasted_iota(jnp.int32, pslc_t.shape, 0)
    cur = (s0 + lax.broadcasted_iota(jnp.int32, pslc_t.shape, 1)) // SEL_BLOCK
    c0 = s0 // SEL_BLOCK
    elig = (blk >= 1) & (blk <= cur - 2)
    sel = _pick_top(jnp.where(elig, pslc_t, -jnp.inf), blk, nb, SEL_TOPK - 3)
    sel = sel | (blk == 0) | (blk == cur - 1)
    past = jnp.where(sel & (blk < c0), 0.0, NEG).T.astype(BF16)
    for st in range(n_super):
        b = past[:, st * SEL_LANES:(st + 1) * SEL_LANES]
        qa_ref[st] = jnp.concatenate([q, jnp.concatenate([b] * NSA_GROUP, axis=0)], axis=1)

    _softmax_init(m_ref, acc_ref)
    d0 = pl.multiple_of(s0, tq)
    s = _dot_nt(q, ka_ref[0, 0, pl.ds(d0, tq), :][:, 0:HEAD_DIM])
    qpos = lax.broadcasted_iota(jnp.int32, s.shape, 0) & (tq - 1)
    kpos = lax.broadcasted_iota(jnp.int32, s.shape, 1)
    _softmax_step(jnp.where(kpos <= qpos, s, NEG), va_ref[0, 0, pl.ds(d0, tq), :], m_ref, acc_ref)

    def step(start, size):
        s = _dot_nt(qa_ref[start // SUPER_KEYS], ka_ref[0, 0, pl.ds(start, size), :])
        _softmax_step(s, va_ref[0, 0, pl.ds(start, size), :], m_ref, acc_ref)

    (o_w,) = _past_keys_loop(s0, NSA_TK, step, parts=4, riders=(
        (window_interior, lambda: lax.cond(s0 >= WINDOW, window_interior, window_start)),))
    o_s = _softmax_result(acc_ref)

    w = NSA_GROUP * HEAD_DIM
    gexp = _dot_x2(gt_ref[...], e_ref[0])
    wide = lambda x: jnp.concatenate([x[g * tq:(g + 1) * tq] for g in range(NSA_GROUP)], axis=1)
    o_ref[...] = (gexp[:, 0:w] * wide(o_c) + gexp[:, w:2 * w] * wide(o_s) + gexp[:, 2 * w:3 * w] * wide(o_w)).astype(BF16)


def _nsa(qs, kcmp, vcmp, ka, va, kvw, gates, batch, seq):
    tq = NSA_TQ
    nq = seq // tq
    nb = seq // SEL_BLOCK
    n16 = seq // CMP_STRIDE
    assert seq % SUPER_KEYS == 0 and tq == 2 * SEL_BLOCK and WINDOW % tq == 0 and NSA_TK >= WINDOW
    n_super = seq // SUPER_KEYS
    rows = NSA_GROUP * tq
    e = np.zeros((NSA_KV_HEADS, GATE_LANES, 3 * NSA_GROUP * HEAD_DIM), np.float32)
    for br in range(3):
        for hk in range(NSA_KV_HEADS):
            for g in range(NSA_GROUP):
                c = (br * NSA_GROUP + g) * HEAD_DIM
                e[hk, br * NSA_HEADS + hk * NSA_GROUP + g, c:c + HEAD_DIM] = 1.0
    nn, jj = np.arange(n16)[:, None], np.arange(nb)[None, :]
    band = ((nn >= 4 * jj - 1) & (nn <= 4 * jj + 3)).astype(np.float32)
    resident = lambda width: pl.BlockSpec((1, 1, seq, width), lambda b, h, i: (b, h, 0, 0))
    w = NSA_GROUP * HEAD_DIM
    return pl.pallas_call(
        functools.partial(_nsa_kernel, n_super),
        grid=(batch, NSA_KV_HEADS, nq),
        in_specs=[pl.BlockSpec((1, 1, 1, rows, HEAD_DIM), lambda b, h, i: (b, h, i, 0, 0)),
                  pl.BlockSpec((1, 1, n16, HEAD_DIM), lambda b, h, i: (b, h, 0, 0)),
                  pl.BlockSpec((1, 1, n16, HEAD_DIM), lambda b, h, i: (b, h, 0, 0)),
                  resident(LANES), resident(LANES), resident(LANES),
                  pl.BlockSpec((tq, GATE_LANES), lambda b, h, i: (b * nq + i, 0)),
                  pl.BlockSpec((1, GATE_LANES, 3 * w), lambda b, h, i: (h, 0, 0)),
                  pl.BlockSpec((n16, nb), lambda b, h, i: (0, 0))],
        out_specs=pl.BlockSpec((tq, w), lambda b, h, i: (b * nq + i, h)),
        out_shape=jax.ShapeDtypeStruct((batch * seq, NSA_W), BF16),
        scratch_shapes=[pltpu.VMEM((n_super, rows, LANES), BF16), pltpu.VMEM((rows, LANES), F32),
                        pltpu.VMEM((rows, LANES), F32)],
        compiler_params=_params("arbitrary", "arbitrary", "arbitrary"),
        name="nsa",
    )(qs, kcmp, vcmp, ka, va, kvw, gates, jnp.asarray(e, BF16), jnp.asarray(band, BF16))


def _odd_out_kernel(om_ref, on_ref, h_ref, w_ref, o_ref):
    acc = h_ref[...] + _dot(on_ref[...], w_ref[MOBA_W:D_MODEL, :])
    for h in range(MOBA_HEADS):
        acc = acc + _dot(om_ref[0, h], w_ref[h * HEAD_DIM:(h + 1) * HEAD_DIM, :])
    o_ref[...] = acc


def _odd_out(o_moba, o_nsa, h2, w_out, seq, tm=ROW_TILE):
    t = h2.shape[0]
    tps = seq // tm
    row = lambda w: pl.BlockSpec((tm, w), lambda i: (i, 0))
    return pl.pallas_call(
        _odd_out_kernel,
        grid=(t // tm,),
        in_specs=[pl.BlockSpec((1, MOBA_HEADS, tm, HEAD_DIM), lambda i: (i // tps, 0, i % tps, 0)),
                  row(NSA_W), row(D_MODEL), _full((D_MODEL, D_MODEL))],
        out_specs=row(D_MODEL),
        out_shape=jax.ShapeDtypeStruct((t, D_MODEL), F32),
        compiler_params=_params("arbitrary"),
        name="odd_out",
    )(o_moba, o_nsa, h2, w_out)


def _even_layer(h2, batch, seq, norm, w_in, w_out, lam_re, lam_im, log_dt, b_re, b_im, c_re, c_im, d, w_glu, conv_w, conv_b):
    u, ub, yb = _even_in(h2, norm.reshape(1, D_MODEL), w_in.astype(BF16), conv_w, conv_b.reshape(1, CONV_WIDTH), seq)
    ypre = _s5_mixer_pre(ub, batch, seq, lam_re, lam_im, log_dt, b_re, b_im, c_re, c_im)
    return _even_out(ypre, u, yb, h2, d.reshape(1, S5_WIDTH), w_glu.astype(BF16), w_out.astype(BF16))


def _odd_layer(h2, batch, seq, norm, w_in, w_out, moba_q_norm, moba_k_norm, nsa_q_norm, nsa_kcmp_norm, nsa_ksel_norm,
               nsa_kwin_norm, cmp_pe_k, cmp_w1_k, cmp_w2_k, cmp_pe_v, cmp_w1_v, cmp_w2_v):
    qm, kam, kmean, vam, qs, kvc, kas, vas, kvw, gates = _odd_in(
        h2, batch, seq, norm.reshape(1, D_MODEL), w_in, moba_q_norm, moba_k_norm, nsa_q_norm, nsa_ksel_norm, nsa_kwin_norm)
    cmp = _compress(kvc, batch, seq, cmp_pe_k, cmp_w1_k, cmp_w2_k, cmp_pe_v, cmp_w1_v, cmp_w2_v, nsa_kcmp_norm)
    o_moba = _moba(qm, kam, kmean, vam, batch, seq)
    o_nsa = _nsa(qs, cmp[0], cmp[1], kas, vas, kvw, gates, batch, seq)
    return _odd_out(o_moba, o_nsa, h2, w_out.astype(BF16), seq)


def kernel(x, ev_norm_mix, ev_w_in, ev_w_out, s5_lam_re, s5_lam_im, s5_log_dt, s5_b_re, s5_b_im, s5_c_re, s5_c_im, s5_d, s5_w_glu, conv_w, conv_b, od_norm_mix, od_w_in, od_w_out, moba_q_norm, moba_k_norm, nsa_q_norm, nsa_kcmp_norm, nsa_ksel_norm, nsa_kwin_norm, cmp_pe_k, cmp_w1_k, cmp_w2_k, cmp_pe_v, cmp_w1_v, cmp_w2_v, moe_norm, moe_w_group, moe_b_group, moe_w_expert, moe_b_expert, moe_w_gate, moe_w_up, moe_w_down):
    batch, seq, _ = x.shape
    depth = moe_norm.shape[0]
    h = x.reshape(batch * seq, D_MODEL)
    for layer in range(depth):
        i = layer // 2
        if layer % 2 == 0:
            h = _even_layer(h, batch, seq, ev_norm_mix[i], ev_w_in[i], ev_w_out[i], s5_lam_re[i], s5_lam_im[i], s5_log_dt[i],
                            s5_b_re[i], s5_b_im[i], s5_c_re[i], s5_c_im[i], s5_d[i], s5_w_glu[i], conv_w[i], conv_b[i])
        else:
            h = _odd_layer(h, batch, seq, od_norm_mix[i], od_w_in[i], od_w_out[i], moba_q_norm[i], moba_k_norm[i],
                           nsa_q_norm[i], nsa_kcmp_norm[i], nsa_ksel_norm[i], nsa_kwin_norm[i], cmp_pe_k[i], cmp_w1_k[i],
                           cmp_w2_k[i], cmp_pe_v[i], cmp_w1_v[i], cmp_w2_v[i])
        h = _moe(h, moe_norm[layer].reshape(1, D_MODEL), moe_w_group[layer], moe_b_group[layer], moe_w_expert[layer],
                 moe_b_expert[layer], moe_w_gate[layer], moe_w_up[layer], moe_w_down[layer])
    return h.reshape(batch, seq, D_MODEL)
```

```python
import functools
import math

import jax
import jax.numpy as jnp
import numpy as np
from jax import lax
from jax.experimental import pallas as pl
from jax.experimental.pallas import tpu as pltpu

D_MODEL = 1024
HEAD_DIM = 64
EPS = 1e-6
S5_WIDTH = 256
S5_GROUP = 16
S5_GROUPS = 16
S5_STATE = 64
S5_CHUNK = 16
S5_GROUP_BITS = 4
S5_STATE_BITS = 6
S5_ROW_TILE = 512
S5_COL_TILE = 512
S5_SCAN_TILE = 256
CONV_WIDTH = 768
CONV_K = 3
MOBA_HEADS = 4
NSA_HEADS = 12
NSA_KV_HEADS = 2
NSA_GROUP = 6
MOBA_W = 256
NSA_W = 768
KV_W = 128
MOBA_BLOCK = 256
MOBA_TOPK = 3
CMP_BLOCK = 32
CMP_STRIDE = 16
CMP_HIDDEN = 256
SEL_BLOCK = 64
SEL_TOPK = 8
WINDOW = 512
N_GROUPS = 4
EXPERTS_PER_GROUP = 4
N_EXPERTS = 16
EXPERT_FF = 256

LANES = 128
SUBLANES = 8
VMEM_LIMIT_BYTES = 56 * 1024 * 1024
ROW_TILE = 1024
GATE_LANES = LANES
NEG = -float(2 ** 100)
F32 = jnp.float32
BF16 = jnp.bfloat16


def _params(*semantics):
    return pltpu.CompilerParams(dimension_semantics=semantics, vmem_limit_bytes=VMEM_LIMIT_BYTES)


def _dot(a, b):
    return jnp.dot(a, b, preferred_element_type=F32)


def _dot_nt(a, b):
    return lax.dot_general(a, b, (((1,), (1,)), ((), ())), preferred_element_type=F32)


def _split(x):
    hi = x.astype(BF16)
    lo = (x - hi.astype(F32)).astype(BF16)
    return hi, lo


def _dot_x2(x, w):
    hi, lo = _split(x)
    return _dot(hi, w) + _dot(lo, w)


def _dot_x3(x, w_hilo):
    n = w_hilo.shape[1] // 2
    hi, lo = _split(x)
    both = _dot(hi, w_hilo)
    return both[:, 0:n] + (both[:, n:2 * n] + _dot(lo, w_hilo[:, 0:n]))


def _rms(x, gain):
    return x * lax.rsqrt(jnp.mean(x * x, axis=-1, keepdims=True) + EPS) * gain


def _gelu(x):
    return 0.5 * x * (1.0 + jnp.tanh(math.sqrt(2.0 / math.pi) * (x + 0.044715 * (x * x * x))))


def _sigmoid(x):
    return 1.0 / (1.0 + jnp.exp(-x))


def _full(shape):
    n = len(shape)
    return pl.BlockSpec(shape, lambda *_: (0,) * n)


def _even_in_kernel(tiles_per_seq, x_ref, g_ref, w_ref, cw_ref, cb_ref, u_ref, ub_ref, yb_ref, carry_ref):
    i = pl.program_id(0)
    xn = _rms(x_ref[...], g_ref[...]).astype(BF16)
    u = _dot(xn, w_ref[:, 0:S5_WIDTH])
    u_ref[...] = u
    ub_ref[...] = u.astype(BF16)
    o = S5_WIDTH
    xc = _dot(xn, w_ref[:, o:o + CONV_WIDTH])
    gb = _dot(xn, w_ref[:, o + CONV_WIDTH:o + 2 * CONV_WIDTH])
    gc = _dot(xn, w_ref[:, o + 2 * CONV_WIDTH:o + 3 * CONV_WIDTH])
    z = gc * xc
    tm = z.shape[0]

    @pl.when(i % tiles_per_seq == 0)
    def _():
        carry_ref[...] = jnp.zeros_like(carry_ref)

    row = lax.broadcasted_iota(jnp.int32, z.shape, 0)
    prev1 = carry_ref[SUBLANES - 1:SUBLANES, :]
    prev2 = carry_ref[SUBLANES - 2:SUBLANES - 1, :]
    z1 = jnp.where(row == 0, prev1, pltpu.roll(z, 1, 0))
    z2 = jnp.where(row == 0, prev2, jnp.where(row == 1, prev1, pltpu.roll(z, 2, 0)))
    y = cw_ref[0:1, :] * z2 + cw_ref[1:2, :] * z1 + cw_ref[2:3, :] * z + cb_ref[...]
    yb_ref[...] = (gb * y).astype(BF16)
    carry_ref[...] = z[tm - SUBLANES:tm, :]


def _even_in(x2, gain, w_in, conv_w, conv_b, seq, tm=ROW_TILE):
    t = x2.shape[0]
    n_in = w_in.shape[1]
    return pl.pallas_call(
        functools.partial(_even_in_kernel, seq // tm),
        grid=(t // tm,),
        in_specs=[pl.BlockSpec((tm, D_MODEL), lambda i: (i, 0)), _full((1, D_MODEL)),
                  _full((D_MODEL, n_in)), _full((CONV_K, CONV_WIDTH)), _full((1, CONV_WIDTH))],
        out_specs=[pl.BlockSpec((tm, S5_WIDTH), lambda i: (i, 0)), pl.BlockSpec((tm, S5_WIDTH), lambda i: (i, 0)),
                   pl.BlockSpec((tm, CONV_WIDTH), lambda i: (i, 0))],
        out_shape=[jax.ShapeDtypeStruct((t, S5_WIDTH), F32), jax.ShapeDtypeStruct((t, S5_WIDTH), BF16),
                   jax.ShapeDtypeStruct((t, CONV_WIDTH), BF16)],
        scratch_shapes=[pltpu.VMEM((SUBLANES, CONV_WIDTH), F32)],
        compiler_params=_params("arbitrary"),
        name="even_in",
    )(x2, gain, w_in, conv_w, conv_b)


def _s5_weights(lam_re, lam_im, log_dt, b_re, b_im, c_re, c_im):
    g, p, hg, ck = S5_GROUPS, S5_STATE, S5_GROUP, S5_CHUNK
    lr, li = lam_re.astype(F32), lam_im.astype(F32)
    dt = jnp.exp(log_dt.astype(F32))[:, None]
    mag = jnp.exp(lr * dt)
    a_re, a_im = mag * jnp.cos(li * dt), mag * jnp.sin(li * dt)
    den = lr * lr + li * li
    f_re = ((a_re - 1.0) * lr + a_im * li) / den
    f_im = (a_im * lr - (a_re - 1.0) * li) / den
    br, bi = b_re.astype(F32), b_im.astype(F32)
    bb_re = f_re[..., None] * br - f_im[..., None] * bi
    bb_im = f_re[..., None] * bi + f_im[..., None] * br
    pw_re, pw_im = [jnp.ones_like(a_re)], [jnp.zeros_like(a_im)]
    for _ in range(ck):
        r, m = pw_re[-1], pw_im[-1]
        pw_re.append(r * a_re - m * a_im)
        pw_im.append(r * a_im + m * a_re)
    pw_re, pw_im = jnp.stack(pw_re), jnp.stack(pw_im)
    cr, ci = c_re.astype(F32), c_im.astype(F32)
    rev_re, rev_im = pw_re[ck - 1::-1][:ck], pw_im[ck - 1::-1][:ck]
    ws_re = rev_re[:, :, :, None] * bb_re[None] - rev_im[:, :, :, None] * bb_im[None]
    ws_im = rev_re[:, :, :, None] * bb_im[None] + rev_im[:, :, :, None] * bb_re[None]
    ca_re = cr[None] * pw_re[1:, :, None, :] - ci[None] * pw_im[1:, :, None, :]
    ca_im = cr[None] * pw_im[1:, :, None, :] + ci[None] * pw_re[1:, :, None, :]
    cb_re = jnp.einsum('ghp,kgp,gpj->kghj', cr, pw_re[:ck], bb_re) - jnp.einsum('ghp,kgp,gpj->kghj', cr, pw_im[:ck], bb_im) \
        - jnp.einsum('ghp,kgp,gpj->kghj', ci, pw_re[:ck], bb_im) - jnp.einsum('ghp,kgp,gpj->kghj', ci, pw_im[:ck], bb_re)
    lag = np.arange(ck)[None, :] - np.arange(ck)[:, None]
    tz = cb_re[np.clip(lag, 0, ck - 1)]
    tz = jnp.where((lag >= 0)[:, :, None, None, None], tz, 0.0)
    cw = ck * g * hg
    ws = jnp.stack([ws_re, ws_im]).transpose(1, 2, 4, 0, 3).reshape(cw, 2 * p)
    wc = jnp.stack([ca_re, -ca_im]).transpose(0, 2, 4, 1, 3).reshape(2 * g * p, ck * hg)
    tzc = tz.transpose(0, 2, 4, 1, 3).reshape(cw, ck * hg)
    return ws.astype(BF16), wc.astype(BF16), tzc.astype(BF16), pw_re[ck].reshape(1, g * p), pw_im[ck].reshape(1, g * p)


def _group_expand(compact, expand, row_shift, col_shift, col0):
    full = _dot(compact, expand)
    row = lax.broadcasted_iota(jnp.int32, full.shape, 0)
    col = col0 + lax.broadcasted_iota(jnp.int32, full.shape, 1)
    same = ((row >> row_shift) & (S5_GROUPS - 1)) == ((col >> col_shift) & (S5_GROUPS - 1))
    return jnp.where(same, full, 0.0).astype(BF16)


def _s5_state_kernel(u_ref, ws_ref, e_ref, s_ref, w_scr):
    @pl.when(pl.program_id(1) == 0)
    def _():
        w_scr[...] = _group_expand(ws_ref[...], e_ref[...], S5_GROUP_BITS, S5_STATE_BITS,
                                   pl.program_id(0) * w_scr.shape[1])

    s_ref[...] = _dot(u_ref[...], w_scr[...])


def _s5_scan_kernel(s_ref, are_ref, aim_ref, xprev_ref, st_ref):
    @pl.when(pl.program_id(0) == 0)
    def _():
        st_ref[...] = jnp.zeros_like(st_ref)

    a_re, a_im = are_ref[...], aim_ref[...]
    nb, n = s_ref.shape[0], s_ref.shape[1]
    half = a_re.shape[1]

    def body(c, carry):
        out = []
        for b in range(nb):
            xr, xi = carry[2 * b], carry[2 * b + 1]
            xprev_ref[b, pl.ds(c, 1), 0:half] = xr
            xprev_ref[b, pl.ds(c, 1), half:2 * half] = xi
            s = s_ref[b, pl.ds(c, 1), :]
            out += [a_re * xr - a_im * xi + s[:, 0:half], a_re * xi + a_im * xr + s[:, half:2 * half]]
        return tuple(out)

    init = tuple(st_ref[b:b + 1, o:o + half] for b in range(nb) for o in (0, half))
    final = lax.fori_loop(0, n, body, init, unroll=4)
    for b in range(nb):
        st_ref[b:b + 1, 0:half] = final[2 * b]
        st_ref[b:b + 1, half:2 * half] = final[2 * b + 1]


def _s5_out_kernel(u_ref, xp_ref, tz_ref, wc_ref, e_ref, y_ref, tz_scr, wc_scr):
    @pl.when(pl.program_id(1) == 0)
    def _():
        col0 = pl.program_id(0) * tz_scr.shape[1]
        tz_scr[...] = _group_expand(tz_ref[...], e_ref[...], S5_GROUP_BITS, S5_GROUP_BITS, col0)
        wc_scr[...] = _group_expand(wc_ref[...], e_ref[...], S5_STATE_BITS, S5_GROUP_BITS, col0)

    y_ref[...] = _dot(u_ref[...], tz_scr[...]) + _dot(xp_ref[...].astype(BF16), wc_scr[...])


def _expand_matrix(outer, inner):
    e = np.zeros((outer, inner, outer, S5_GROUPS, inner), np.float32)
    for x in range(outer):
        for y in range(inner):
            e[x, y, x, :, y] = 1.0
    return jnp.asarray(e.reshape(outer * inner, outer * S5_GROUPS * inner), BF16)


def _s5_mixer_pre(ub, batch, seq, lam_re, lam_im, log_dt, b_re, b_im, c_re, c_im):
    ws, wc, tz, a16_re, a16_im = _s5_weights(lam_re, lam_im, log_dt, b_re, b_im, c_re, c_im)
    nc = seq // S5_CHUNK
    rows = batch * nc
    cw = S5_CHUNK * S5_WIDTH
    sw = 2 * S5_GROUPS * S5_STATE
    assert S5_GROUP == 1 << S5_GROUP_BITS and S5_STATE == 1 << S5_STATE_BITS
    tr = min(rows, S5_ROW_TILE)
    tn = S5_COL_TILE
    uc = ub.reshape(rows, cw)
    e_state = _expand_matrix(2, S5_STATE)
    e_out = _expand_matrix(S5_CHUNK, S5_GROUP)
    s = pl.pallas_call(
        _s5_state_kernel,
        grid=(sw // tn, rows // tr),
        in_specs=[pl.BlockSpec((tr, cw), lambda j, i: (i, 0)), _full(ws.shape),
                  pl.BlockSpec((e_state.shape[0], tn), lambda j, i: (0, j))],
        out_specs=pl.BlockSpec((tr, tn), lambda j, i: (i, j)),
        out_shape=jax.ShapeDtypeStruct((rows, sw), F32),
        scratch_shapes=[pltpu.VMEM((cw, tn), BF16)],
        compiler_params=_params("arbitrary", "arbitrary"),
        name="s5_state",
    )(uc, ws, e_state)
    tc = min(nc, S5_SCAN_TILE)
    xprev = pl.pallas_call(
        _s5_scan_kernel,
        grid=(nc // tc,),
        in_specs=[pl.BlockSpec((batch, tc, sw), lambda i: (0, i, 0)), _full((1, sw // 2)), _full((1, sw // 2))],
        out_specs=pl.BlockSpec((batch, tc, sw), lambda i: (0, i, 0)),
        out_shape=jax.ShapeDtypeStruct((batch, nc, sw), F32),
        scratch_shapes=[pltpu.VMEM((batch, sw), F32)],
        compiler_params=_params("arbitrary"),
        name="s5_scan",
    )(s.reshape(batch, nc, sw), a16_re, a16_im)
    y = pl.pallas_call(
        _s5_out_kernel,
        grid=(cw // tn, rows // tr),
        in_specs=[pl.BlockSpec((tr, cw), lambda j, i: (i, 0)), pl.BlockSpec((tr, sw), lambda j, i: (i, 0)),
                  _full(tz.shape), _full(wc.shape), pl.BlockSpec((e_out.shape[0], tn), lambda j, i: (0, j))],
        out_specs=pl.BlockSpec((tr, tn), lambda j, i: (i, j)),
        out_shape=jax.ShapeDtypeStruct((rows, cw), F32),
        scratch_shapes=[pltpu.VMEM((cw, tn), BF16), pltpu.VMEM((sw, tn), BF16)],
        compiler_params=_params("arbitrary", "arbitrary"),
        name="s5_out",
    )(uc, xprev.reshape(rows, sw), tz, wc, e_out)
    return y.reshape(batch * seq, S5_WIDTH)


def _even_out_kernel(ypre_ref, u_ref, yb_ref, x_ref, d_ref, wglu_ref, wout_ref, o_ref):
    y = _gelu(ypre_ref[...] + d_ref[...] * u_ref[...])
    y = y * _sigmoid(_dot(y.astype(BF16), wglu_ref[...]))
    o_ref[...] = (x_ref[...] + _dot(y.astype(BF16), wout_ref[0:S5_WIDTH, :])
                  + _dot(yb_ref[...], wout_ref[S5_WIDTH:D_MODEL, :]))


def _even_out(ypre, u, yb, x2, d, w_glu, w_out, tm=ROW_TILE):
    t = x2.shape[0]
    row = lambda w: pl.BlockSpec((tm, w), lambda i: (i, 0))
    return pl.pallas_call(
        _even_out_kernel,
        grid=(t // tm,),
        in_specs=[row(S5_WIDTH), row(S5_WIDTH), row(CONV_WIDTH), row(D_MODEL), _full((1, S5_WIDTH)),
                  _full((S5_WIDTH, S5_WIDTH)), _full((D_MODEL, D_MODEL))],
        out_specs=row(D_MODEL),
        out_shape=jax.ShapeDtypeStruct((t, D_MODEL), F32),
        compiler_params=_params("arbitrary"),
        name="even_out",
    )(ypre, u, yb, x2, d, w_glu, w_out)


def _first_max(v, pos, width, axis=-1):
    m = jnp.max(v, axis=axis, keepdims=True)
    idx = jnp.min(jnp.where(v == m, pos, width), axis=axis, keepdims=True)
    return m, idx


MOE_TM = 1024
MOE_ALIGN = 16
MOE_SLOTS = 1152
MOE_WIN = 320
ROUTER_LANES = LANES


def _moe_router_kernel(h_ref, g_ref, wr_ref, br_ref, gate_ref, grp_ref, cnt_ref):
    xn = _rms(h_ref[...], g_ref[...])
    logits = _dot_x3(xn, wr_ref[...]) + br_ref[...]
    lane = lax.broadcasted_iota(jnp.int32, logits.shape, 1).astype(F32)
    width = float(logits.shape[1])
    is_g = lane < N_GROUPS
    gl = jnp.where(is_g, logits, -jnp.inf)
    gm, gi = _first_max(gl, lane, width)
    gw = 1.0 / jnp.sum(jnp.where(is_g, jnp.exp(gl - gm), 0.0), axis=-1, keepdims=True)
    lo = N_GROUPS + gi * EXPERTS_PER_GROUP
    in_grp = (lane >= lo) & (lane < lo + EXPERTS_PER_GROUP)
    el = jnp.where(in_grp, logits, -jnp.inf)
    m1, i1 = _first_max(el, lane, width)
    m2, i2 = _first_max(jnp.where(lane == i1, -jnp.inf, el), lane, width)
    p2 = jnp.exp(m2 - m1)
    w1 = gw / (1.0 + p2)
    w2 = gw * p2 / (1.0 + p2)
    gate_ref[...] = jnp.where(lane == i1, w1, 0.0) + jnp.where(lane == i2, w2, 0.0)
    grp = (lane == gi).astype(F32)
    grp_ref[...] = grp.astype(BF16)
    cnt_ref[0] = jnp.broadcast_to(jnp.sum(grp, axis=0, keepdims=True), cnt_ref.shape[1:])


def _moe_expert_kernel(base_ref, nwin_ref, h_ref, g_ref, gate_ref, grp_ref, wg_ref, wu_ref, wd_ref, o_ref,
                       xs_ref, gs_ref, ys_ref, pt_ref):
    i, g = pl.program_id(0), pl.program_id(1)
    tm = h_ref.shape[0]

    @pl.when((i == 0) & (g == 0))
    def _():
        xs_ref[...] = jnp.zeros_like(xs_ref)
        gs_ref[...] = jnp.zeros_like(gs_ref)
        ys_ref[...] = jnp.zeros_like(ys_ref)

    @pl.when(g == 0)
    def _():
        xn = _rms(h_ref[...], g_ref[...]).astype(BF16)
        grp = grp_ref[...]
        earlier = (lax.broadcasted_iota(jnp.int32, (tm, tm), 0) > lax.broadcasted_iota(jnp.int32, (tm, tm), 1)).astype(BF16)
        rank = _dot(earlier, grp)
        lane = lax.broadcasted_iota(jnp.int32, rank.shape, 1)
        for k in range(N_GROUPS):
            rank = rank + jnp.where(lane == k, base_ref[i * N_GROUPS + k].astype(F32), 0.0)
        slot = jnp.sum(grp.astype(F32) * rank, axis=-1, keepdims=True).astype(jnp.int32)
        pt = (lax.broadcasted_iota(jnp.int32, (tm, MOE_SLOTS), 1) == slot).astype(BF16)
        pt_ref[...] = pt
        gather = lambda x: lax.dot_general(pt, x, (((0,), (0,)), ((), ())), preferred_element_type=F32)
        xs_ref[0:MOE_SLOTS, :] = gather(xn).astype(BF16)
        both = gather(jnp.concatenate(_split(gate_ref[...]), axis=1))
        gs_ref[0:MOE_SLOTS, :] = both[:, 0:ROUTER_LANES] + both[:, ROUTER_LANES:2 * ROUTER_LANES]

    def window(w, carry):
        r0 = pl.multiple_of(base_ref[i * N_GROUPS + g] + w * MOE_WIN, MOE_ALIGN)
        x = xs_ref[pl.ds(r0, MOE_WIN), :]
        gate = gs_ref[pl.ds(r0, MOE_WIN), :]
        lane = lax.broadcasted_iota(jnp.int32, gate.shape, 1)
        y = None
        for j in range(EXPERTS_PER_GROUP):
            ge = jnp.sum(jnp.where(lane == g * EXPERTS_PER_GROUP + (j + N_GROUPS), gate, 0.0), axis=-1, keepdims=True)
            h1 = _dot(x, wg_ref[j])
            h3 = _dot(x, wu_ref[j])
            act = (h1 * _sigmoid(h1)) * h3 * ge
            yj = _dot(act.astype(BF16), wd_ref[j])
            y = yj if y is None else y + yj
        ys_ref[pl.ds(r0, MOE_WIN), :] = y.astype(BF16)
        return carry

    lax.fori_loop(0, nwin_ref[i * N_GROUPS + g], window, 0)

    @pl.when(g == N_GROUPS - 1)
    def _():
        o_ref[...] = h_ref[...] + _dot(pt_ref[...], ys_ref[0:MOE_SLOTS, :])


def _moe(h2, gain, w_group, b_group, w_expert, b_expert, w_gate, w_up, w_down):
    t = h2.shape[0]
    tm, rw = MOE_TM, ROUTER_LANES
    assert t % tm == 0 and MOE_SLOTS >= tm + N_GROUPS * (MOE_ALIGN - 1) and MOE_WIN % MOE_ALIGN == 0
    n_tiles = t // tm
    wr = jnp.zeros((D_MODEL, rw), F32).at[:, 0:N_GROUPS].set(w_group).at[:, N_GROUPS:N_GROUPS + N_EXPERTS].set(w_expert)
    br = jnp.zeros((1, rw), F32).at[0, 0:N_GROUPS].set(b_group).at[0, N_GROUPS:N_GROUPS + N_EXPERTS].set(b_expert)
    wr_hilo = jnp.concatenate(_split(wr), axis=1)
    row = lambda width: pl.BlockSpec((tm, width), lambda i: (i, 0))
    gates, grp, cnt = pl.pallas_call(
        _moe_router_kernel,
        grid=(n_tiles,),
        in_specs=[row(D_MODEL), _full((1, D_MODEL)), _full((D_MODEL, 2 * rw)), _full((1, rw))],
        out_specs=[row(rw), row(rw), pl.BlockSpec((1, SUBLANES, rw), lambda i: (i, 0, 0))],
        out_shape=[jax.ShapeDtypeStruct((t, rw), F32), jax.ShapeDtypeStruct((t, rw), BF16),
                   jax.ShapeDtypeStruct((n_tiles, SUBLANES, rw), F32)],
        compiler_params=_params("arbitrary"),
        name="moe_router",
    )(h2, gain, wr_hilo, br)
    n = cnt[:, 0, 0:N_GROUPS].astype(jnp.int32)
    padded = (n + (MOE_ALIGN - 1)) // MOE_ALIGN * MOE_ALIGN
    base = (jnp.cumsum(padded, axis=1) - padded).reshape(-1)
    nwin = ((padded + (MOE_WIN - 1)) // MOE_WIN).reshape(-1)
    tile = lambda width: pl.BlockSpec((tm, width), lambda i, g, *_: (i, 0))
    experts = lambda shape: pl.BlockSpec((EXPERTS_PER_GROUP,) + shape, lambda i, g, *_: (g, 0, 0))
    slots = MOE_SLOTS + MOE_WIN
    return pl.pallas_call(
        _moe_expert_kernel,
        grid_spec=pltpu.PrefetchScalarGridSpec(
            num_scalar_prefetch=2,
            grid=(n_tiles, N_GROUPS),
            in_specs=[tile(D_MODEL), pl.BlockSpec((1, D_MODEL), lambda i, g, *_: (0, 0)), tile(rw), tile(rw),
                      experts((D_MODEL, EXPERT_FF)), experts((D_MODEL, EXPERT_FF)), experts((EXPERT_FF, D_MODEL))],
            out_specs=tile(D_MODEL),
            scratch_shapes=[pltpu.VMEM((slots, D_MODEL), BF16), pltpu.VMEM((slots, rw), F32),
                            pltpu.VMEM((slots, D_MODEL), BF16), pltpu.VMEM((tm, MOE_SLOTS), BF16)]),
        out_shape=jax.ShapeDtypeStruct((t, D_MODEL), F32),
        compiler_params=_params("arbitrary", "arbitrary"),
        name="moe",
    )(base, nwin, h2, gain, gates, grp, w_gate.astype(BF16), w_up.astype(BF16), w_down.astype(BF16))


ODD_SPLITS = (MOBA_W, MOBA_W, MOBA_W, NSA_W, KV_W, KV_W, KV_W, KV_W, KV_W, KV_W, GATE_LANES)
ODD_IN_PAD = sum(ODD_SPLITS)


def _head_rms(x, hsum, gain):
    w = x.shape[1]
    ss = jnp.concatenate([_dot_x2(x[:, o:o + hsum.shape[0]] * x[:, o:o + hsum.shape[0]], hsum)
                          for o in range(0, w, hsum.shape[0])], axis=1) if w > hsum.shape[0] else _dot_x2(x * x, hsum)
    return x * lax.rsqrt(ss * (1.0 / HEAD_DIM) + EPS) * gain


def _odd_in_kernel(tiles_per_seq, x_ref, g_ref, w_ref, hsum_ref, gq_ref, gk_ref, gnq_ref, gks_ref, gkw_ref,
                   qm_ref, kam_ref, kmean_ref, vam_ref, qs_ref, kvc_ref, kas_ref, vas_ref, kvw_ref, gt_ref):
    xn = _rms(x_ref[...], g_ref[...]).astype(BF16)
    offs = np.cumsum((0,) + ODD_SPLITS)
    col = lambda j: _dot(xn, w_ref[:, int(offs[j]):int(offs[j + 1])])
    head = lambda x, h: x[:, h * HEAD_DIM:(h + 1) * HEAD_DIM]
    hsum = hsum_ref[...]
    hsum128 = hsum_ref[0:KV_W, 0:KV_W]
    tm = x_ref.shape[0]
    pos = (pl.program_id(0) % tiles_per_seq) * tm + lax.broadcasted_iota(jnp.int32, (tm, HEAD_DIM), 0)
    lane = lax.broadcasted_iota(jnp.int32, (tm, HEAD_DIM), 1)
    ones_col = (lane == 0).astype(BF16)

    qm = _head_rms(col(0), hsum, gq_ref[...])
    km = _head_rms(col(1), hsum, gk_ref[...])
    for j in range(tm // MOBA_BLOCK):
        kmean_ref[0, j:j + 1, :] = jnp.mean(km[j * MOBA_BLOCK:(j + 1) * MOBA_BLOCK, :], axis=0, keepdims=True)
    km = km.astype(BF16)
    vm = col(2).astype(BF16)
    moba_id = (lane == pos // MOBA_BLOCK).astype(BF16)
    for h in range(MOBA_HEADS):
        qm_ref[0, h] = head(qm, h)
        kam_ref[0, h] = jnp.concatenate([head(km, h), moba_id], axis=1)
        vam_ref[0, h] = jnp.concatenate([head(vm, h), ones_col], axis=1)

    qd = (_head_rms(col(3), hsum, gnq_ref[...]) * (HEAD_DIM ** -0.5)).astype(BF16)
    for hk in range(NSA_KV_HEADS):
        for j in range(tm // NSA_TQ):
            for g in range(NSA_GROUP):
                qs_ref[0, hk, j, g * NSA_TQ:(g + 1) * NSA_TQ, :] = head(qd, hk * NSA_GROUP + g)[j * NSA_TQ:(j + 1) * NSA_TQ, :]
    kvc_ref[0] = col(4)
    kvc_ref[1] = col(5)
    ks = _head_rms(col(6), hsum128, gks_ref[...]).astype(BF16)
    vs = col(7).astype(BF16)
    kw = _head_rms(col(8), hsum128, gkw_ref[...]).astype(BF16)
    vw = col(9).astype(BF16)
    sel_id = (lane == (pos // SEL_BLOCK) % SEL_LANES).astype(BF16)
    for hk in range(NSA_KV_HEADS):
        kas_ref[0, hk] = jnp.concatenate([head(ks, hk), sel_id], axis=1)
        vas_ref[0, hk] = jnp.concatenate([head(vs, hk), ones_col], axis=1)
        kvw_ref[0, hk] = jnp.concatenate([head(kw, hk), head(vw, hk)], axis=1)
    gt_ref[...] = _sigmoid(col(10))


def _odd_in(h2, batch, seq, gain, w_in, moba_q_norm, moba_k_norm, nsa_q_norm, nsa_ksel_norm, nsa_kwin_norm, tm=ROW_TILE):
    t = h2.shape[0]
    assert MOBA_LANES == HEAD_DIM and SEL_LANES == HEAD_DIM and seq % tm == 0 and tm % MOBA_BLOCK == 0
    tps = seq // tm
    w = jnp.pad(w_in, ((0, 0), (0, ODD_IN_PAD - w_in.shape[1]))).astype(BF16)
    hsum = jnp.asarray(np.kron(np.eye(MOBA_W // HEAD_DIM), np.ones((HEAD_DIM, HEAD_DIM))), BF16)
    tile = lambda g, width: jnp.tile(g.astype(F32), width // HEAD_DIM).reshape(1, width)
    row = lambda width: pl.BlockSpec((tm, width), lambda i: (i, 0))
    heads = lambda n, width: pl.BlockSpec((1, n, tm, width), lambda i: (i // tps, 0, i % tps, 0))
    nmb = tm // MOBA_BLOCK
    nqt = tm // NSA_TQ
    rows = NSA_GROUP * NSA_TQ
    sds = jax.ShapeDtypeStruct
    out_specs = [heads(MOBA_HEADS, HEAD_DIM), heads(MOBA_HEADS, LANES), pl.BlockSpec((1, nmb, MOBA_W), lambda i: (i, 0, 0)),
                 heads(MOBA_HEADS, LANES),
                 pl.BlockSpec((1, NSA_KV_HEADS, nqt, rows, HEAD_DIM), lambda i: (i // tps, 0, i % tps, 0, 0)),
                 pl.BlockSpec((2, tm, KV_W), lambda i: (0, i, 0)),
                 heads(NSA_KV_HEADS, LANES), heads(NSA_KV_HEADS, LANES), heads(NSA_KV_HEADS, LANES),
                 row(GATE_LANES)]
    out_shape = [sds((batch, MOBA_HEADS, seq, HEAD_DIM), F32), sds((batch, MOBA_HEADS, seq, LANES), BF16),
                 sds((t // tm, nmb, MOBA_W), F32), sds((batch, MOBA_HEADS, seq, LANES), BF16),
                 sds((batch, NSA_KV_HEADS, seq // NSA_TQ, rows, HEAD_DIM), BF16),
                 sds((2, t, KV_W), F32), sds((batch, NSA_KV_HEADS, seq, LANES), BF16),
                 sds((batch, NSA_KV_HEADS, seq, LANES), BF16), sds((batch, NSA_KV_HEADS, seq, LANES), BF16),
                 sds((t, GATE_LANES), F32)]
    return pl.pallas_call(
        functools.partial(_odd_in_kernel, tps),
        grid=(t // tm,),
        in_specs=[row(D_MODEL), _full((1, D_MODEL)), _full((D_MODEL, ODD_IN_PAD)), _full((MOBA_W, MOBA_W)),
                  _full((1, MOBA_W)), _full((1, MOBA_W)), _full((1, NSA_W)), _full((1, KV_W)), _full((1, KV_W))],
        out_specs=out_specs,
        out_shape=out_shape,
        compiler_params=_params("arbitrary"),
        name="odd_in",
    )(h2, gain, w, hsum, tile(moba_q_norm, MOBA_W), tile(moba_k_norm, MOBA_W), tile(nsa_q_norm, NSA_W),
      tile(nsa_ksel_norm, KV_W), tile(nsa_kwin_norm, KV_W))


def _compress_kernel(c_ref, w1_ref, w2_ref, pe_ref, g_ref, o_ref):
    kind = pl.program_id(0)
    n16 = c_ref.shape[1] // CMP_STRIDE
    half = CMP_STRIDE * HEAD_DIM
    peb = _dot(pe_ref[0], w1_ref[0])[0:1, :]
    xs = [c_ref[0, pl.ds(s, n16, stride=CMP_STRIDE), :].astype(BF16) for s in range(CMP_STRIDE)]
    for h in range(NSA_KV_HEADS):
        first = second = None
        for s in range(CMP_STRIDE):
            x = xs[s][:, h * HEAD_DIM:(h + 1) * HEAD_DIM]
            a = _dot(x, w1_ref[0, s * HEAD_DIM:(s + 1) * HEAD_DIM, :])
            b = _dot(x, w1_ref[0, half + s * HEAD_DIM:half + (s + 1) * HEAD_DIM, :])
            first = a if first is None else first + a
            second = b if second is None else second + b
        hid = _gelu(first + pltpu.roll(second, n16 - 1, 0) + peb)
        out = _dot(hid.astype(BF16), w2_ref[0])
        o_ref[0, 0, h] = jnp.where(kind == 0, _rms(out, g_ref[...]), out).astype(BF16)


def _compress(kvc, batch, seq, pe_k, w1_k, w2_k, pe_v, w1_v, w2_v, kcmp_norm):
    n16 = seq // CMP_STRIDE
    half = CMP_STRIDE * HEAD_DIM
    w1 = jnp.stack([w1_k, w1_v]).astype(BF16)
    w2 = jnp.stack([w2_k, w2_v]).astype(BF16)
    pe = jnp.stack([pe_k, pe_v]).reshape(2, 1, 2 * half)
    pe = jnp.broadcast_to(pe, (2, SUBLANES, 2 * half)).astype(BF16)
    return pl.pallas_call(
        _compress_kernel,
        grid=(2, batch),
        in_specs=[pl.BlockSpec((1, seq, KV_W), lambda k, b: (k, b, 0)),
                  pl.BlockSpec((1, 2 * half, CMP_HIDDEN), lambda k, b: (k, 0, 0)),
                  pl.BlockSpec((1, CMP_HIDDEN, HEAD_DIM), lambda k, b: (k, 0, 0)),
                  pl.BlockSpec((1, SUBLANES, 2 * half), lambda k, b: (k, 0, 0)),
                  _full((1, HEAD_DIM))],
        out_specs=pl.BlockSpec((1, 1, NSA_KV_HEADS, n16, HEAD_DIM), lambda k, b: (k, b, 0, 0, 0)),
        out_shape=jax.ShapeDtypeStruct((2, batch, NSA_KV_HEADS, n16, HEAD_DIM), BF16),
        compiler_params=_params("arbitrary", "arbitrary"),
        name="nsa_compress",
    )(kvc, w1, w2, pe, kcmp_norm.astype(F32).reshape(1, HEAD_DIM))


M_INIT = -1e30


def _softmax_init(m_ref, acc_ref):
    m_ref[...] = jnp.full(m_ref.shape, M_INIT, F32)
    acc_ref[...] = jnp.zeros(acc_ref.shape, F32)


def _softmax_step(s, v_aug, m_ref, acc_ref):
    m_old = m_ref[...]
    m_new = jnp.maximum(m_old, jnp.max(s, axis=-1, keepdims=True))
    alpha = jnp.exp(m_old - m_new)
    p = jnp.exp(s - jnp.tile(m_new, (1, s.shape[1] // LANES)))
    acc_ref[...] = alpha * acc_ref[...] + _dot(p.astype(BF16), v_aug)
    m_ref[...] = m_new


def _past_keys_loop(n_keys, tile, step, riders=(), parts=2):
    n_full = n_keys // tile

    def body(j, carry):
        step(pl.multiple_of(j * tile, tile), tile)
        return carry

    extras = []
    for k, (with_tile, alone) in enumerate(riders):
        def both(with_tile=with_tile, k=k):
            out = with_tile()
            step(k * tile, tile)
            return out

        extras.append(lax.cond(n_full > k, both, alone))
    lax.fori_loop(len(riders), n_full, body, 0)
    rest = n_keys - n_full * tile
    start = pl.multiple_of(n_full * tile, tile)
    part = tile // parts
    n_parts = (rest + part - 1) // part
    for k in range(1, parts + 1):
        @pl.when(n_parts == k)
        def _(k=k):
            done = 0
            for size in (tile, tile // 2, tile // 4):
                if size % part == 0 and k * part - done >= size:
                    step(pl.multiple_of(start + done, part), size)
                    done += size
            assert done == k * part

    return extras


def _softmax_result(acc_ref):
    acc = acc_ref[...]
    return acc[:, 0:HEAD_DIM] * (1.0 / acc[:, HEAD_DIM:HEAD_DIM + 1])


def _pick_top(score, pos, width, k):
    sel = jnp.zeros(score.shape, jnp.bool_)
    for _ in range(k):
        m, idx = _first_max(score, pos, width, axis=0)
        hit = (pos == idx) & (m > -jnp.inf)
        sel = sel | hit
        score = jnp.where(pos == idx, -jnp.inf, score)
    return sel


MOBA_LANES = 64
MOBA_TK = 2048


MOBA_TQ = 4 * MOBA_BLOCK


def _moba_kernel(q_ref, ka_ref, va_ref, kmean_ref, o_ref, qa_ref, m_ref, acc_ref):
    i0 = pl.program_id(2) * (MOBA_TQ // MOBA_BLOCK)
    q = q_ref[0, 0]
    q_hi, q_lo = _split(q)
    km_hi, km_lo = _split(kmean_ref[0, 0])
    gate = _dot_nt(km_hi, q_hi) + (_dot_nt(km_lo, q_hi) + _dot_nt(km_hi, q_lo))
    blk = lax.broadcasted_iota(jnp.int32, gate.shape, 0)
    cur = i0 + lax.broadcasted_iota(jnp.int32, gate.shape, 1) // MOBA_BLOCK
    sel = _pick_top(jnp.where(blk < cur, gate, -jnp.inf), blk, gate.shape[0], MOBA_TOPK)
    qs = (q * (HEAD_DIM ** -0.5)).astype(BF16)
    past = jnp.where(sel & (blk < i0), 0.0, NEG).T[:, 0:MOBA_LANES]
    own = jnp.where((blk == cur) | (sel & (blk >= i0)), 0.0, NEG).T[:, 0:MOBA_LANES]
    qa_ref[0] = jnp.concatenate([qs, past.astype(BF16)], axis=1)
    qa_ref[1] = jnp.concatenate([qs, own.astype(BF16)], axis=1)
    _softmax_init(m_ref, acc_ref)

    def step(start, size):
        s = _dot_nt(qa_ref[0], ka_ref[0, 0, pl.ds(start, size), :])
        _softmax_step(s, va_ref[0, 0, pl.ds(start, size), :], m_ref, acc_ref)

    _past_keys_loop(i0 * MOBA_BLOCK, MOBA_TK, step)
    start = pl.multiple_of(i0 * MOBA_BLOCK, MOBA_TQ)
    s = _dot_nt(qa_ref[1], ka_ref[0, 0, pl.ds(start, MOBA_TQ), :])
    qpos = lax.broadcasted_iota(jnp.int32, s.shape, 0)
    kpos = lax.broadcasted_iota(jnp.int32, s.shape, 1)
    hidden = (qpos // MOBA_BLOCK == kpos // MOBA_BLOCK) & (kpos > qpos)
    _softmax_step(jnp.where(hidden, NEG, s), va_ref[0, 0, pl.ds(start, MOBA_TQ), :], m_ref, acc_ref)
    o_ref[0, 0] = _softmax_result(acc_ref).astype(BF16)


def _moba(qm, ka, kmean, va, batch, seq):
    nmb = seq // MOBA_BLOCK
    assert nmb <= MOBA_LANES and seq % MOBA_TK == 0 and seq % MOBA_TQ == 0
    kmean = kmean.reshape(batch, nmb, MOBA_HEADS, HEAD_DIM).transpose(0, 2, 1, 3)
    kmean = jnp.pad(kmean, ((0, 0), (0, 0), (0, LANES - nmb), (0, 0)))
    return pl.pallas_call(
        _moba_kernel,
        grid=(batch, MOBA_HEADS, seq // MOBA_TQ),
        in_specs=[pl.BlockSpec((1, 1, MOBA_TQ, HEAD_DIM), lambda b, h, i: (b, h, i, 0)),
                  pl.BlockSpec((1, 1, seq, LANES), lambda b, h, i: (b, h, 0, 0)),
                  pl.BlockSpec((1, 1, seq, LANES), lambda b, h, i: (b, h, 0, 0)),
                  pl.BlockSpec((1, 1, LANES, HEAD_DIM), lambda b, h, i: (b, h, 0, 0))],
        out_specs=pl.BlockSpec((1, 1, MOBA_TQ, HEAD_DIM), lambda b, h, i: (b, h, i, 0)),
        out_shape=jax.ShapeDtypeStruct((batch, MOBA_HEADS, seq, HEAD_DIM), BF16),
        scratch_shapes=[pltpu.VMEM((2, MOBA_TQ, LANES), BF16), pltpu.VMEM((MOBA_TQ, LANES), F32),
                        pltpu.VMEM((MOBA_TQ, LANES), F32)],
        compiler_params=_params("arbitrary", "arbitrary", "arbitrary"),
        name="moba",
    )(qm, ka, va, kmean)


NSA_TQ = 2 * SEL_BLOCK
NSA_TK = 2048
SEL_LANES = 64
SUPER_KEYS = SEL_LANES * SEL_BLOCK
CMP_WIDTH_STEP = 256


def _nsa_kernel(n_super, q_ref, kc_ref, vc_ref, ka_ref, va_ref, kvw_ref, gt_ref, e_ref, band_ref, o_ref,
                qa_ref, m_ref, acc_ref):
    qi = pl.program_id(2)
    tq = NSA_TQ
    s0 = qi * tq
    q = q_ref[0, 0, 0]

    def compressed(width):
        sc = _dot_nt(q, kc_ref[0, 0, 0:width, :])
        rq = lax.broadcasted_iota(jnp.int32, (sc.shape[0], LANES), 0) & (tq - 1)
        last = jnp.tile((s0 + rq - (CMP_BLOCK - 1)) >> (CMP_STRIDE.bit_length() - 1), (1, width // LANES))
        sc = jnp.where(lax.broadcasted_iota(jnp.int32, sc.shape, 1) <= last, sc, NEG)
        mx = jnp.max(sc, axis=-1, keepdims=True)
        pc = jnp.exp(sc - mx)
        pc = pc * jnp.where(mx > 0.5 * NEG, 1.0 / jnp.sum(pc, axis=-1, keepdims=True), 0.0)
        imp = pc[0:tq]
        for g in range(1, NSA_GROUP):
            imp = imp + pc[g * tq:(g + 1) * tq]
        return _dot(pc.astype(BF16), vc_ref[0, 0, 0:width, :]), _dot_x2(imp, band_ref[0:width, :])

    n16 = kc_ref.shape[2]
    widths = list(range(CMP_WIDTH_STEP, n16, CMP_WIDTH_STEP)) + [n16]
    o_c, pslc = lax.switch((s0 + tq - 1) // (CMP_WIDTH_STEP * CMP_STRIDE),
                           [functools.partial(compressed, wd) for wd in widths])

    wlen = WINDOW + tq

    def window(kvw, masked):
        sw = masked(_dot_nt(q, kvw[:, 0:HEAD_DIM]))
        pw = jnp.exp(sw - jnp.max(sw, axis=-1, keepdims=True))
        return _dot(pw.astype(BF16), kvw)[:, HEAD_DIM:2 * HEAD_DIM] * (1.0 / jnp.sum(pw, axis=-1, keepdims=True))

    def window_interior():
        def masked(sw):
            rq = lax.broadcasted_iota(jnp.int32, (sw.shape[0], tq), 0) & (tq - 1)
            c = lax.broadcasted_iota(jnp.int32, (sw.shape[0], tq), 1)
            return jnp.concatenate([jnp.where(c > rq, sw[:, 0:tq], NEG), sw[:, tq:WINDOW],
                                    jnp.where(c <= rq, sw[:, WINDOW:wlen], NEG)], axis=1)

        return window(kvw_ref[0, 0, pl.ds(pl.multiple_of(s0 - WINDOW, tq), wlen), :], masked)

    def window_start():
        def masked(sw):
            kabs = lax.broadcasted_iota(jnp.int32, sw.shape, 1)
            t = s0 + (lax.broadcasted_iota(jnp.int32, sw.shape, 0) & (tq - 1))
            return jnp.where((kabs <= t) & (kabs > t - WINDOW), sw, NEG)

        return window(kvw_ref[0, 0, 0:wlen, :], masked)

    pslc_t = pslc.T
    nb = pslc_t.shape[0]
    blk = lax.broadcasted_iota(jnp.int32, pslc_t.shape, 0)
    cur = (s0 + lax.broadcasted_iota(jnp.int32, pslc_t.shape, 1)) // SEL_BLOCK
    c0 = s0 // SEL_BLOCK
    elig = (blk >= 1) & (blk <= cur - 2)
    sel = _pick_top(jnp.where(elig, pslc_t, -jnp.inf), blk, nb, SEL_TOPK - 3)
    sel = sel | (blk == 0) | (blk == cur - 1)
    past = jnp.where(sel & (blk < c0), 0.0, NEG).T.astype(BF16)
    for st in range(n_super):
        b = past[:, st * SEL_LANES:(st + 1) * SEL_LANES]
        qa_ref[st] = jnp.concatenate([q, jnp.concatenate([b] * NSA_GROUP, axis=0)], axis=1)

    _softmax_init(m_ref, acc_ref)
    d0 = pl.multiple_of(s0, tq)
    s = _dot_nt(q, ka_ref[0, 0, pl.ds(d0, tq), :][:, 0:HEAD_DIM])
    qpos = lax.broadcasted_iota(jnp.int32, s.shape, 0) & (tq - 1)
    kpos = lax.broadcasted_iota(jnp.int32, s.shape, 1)
    _softmax_step(jnp.where(kpos <= qpos, s, NEG), va_ref[0, 0, pl.ds(d0, tq), :], m_ref, acc_ref)

    def step(start, size):
        s = _dot_nt(qa_ref[start // SUPER_KEYS], ka_ref[0, 0, pl.ds(start, size), :])
        _softmax_step(s, va_ref[0, 0, pl.ds(start, size), :], m_ref, acc_ref)

    (o_w,) = _past_keys_loop(s0, NSA_TK, step, parts=4, riders=(
        (window_interior, lambda: lax.cond(s0 >= WINDOW, window_interior, window_start)),))
    o_s = _softmax_result(acc_ref)

    w = NSA_GROUP * HEAD_DIM
    gexp = _dot_x2(gt_ref[...], e_ref[0])
    wide = lambda x: jnp.concatenate([x[g * tq:(g + 1) * tq] for g in range(NSA_GROUP)], axis=1)
    o_ref[...] = (gexp[:, 0:w] * wide(o_c) + gexp[:, w:2 * w] * wide(o_s) + gexp[:, 2 * w:3 * w] * wide(o_w)).astype(BF16)


def _nsa(qs, kcmp, vcmp, ka, va, kvw, gates, batch, seq):
    tq = NSA_TQ
    nq = seq // tq
    nb = seq // SEL_BLOCK
    n16 = seq // CMP_STRIDE
    assert seq % SUPER_KEYS == 0 and tq == 2 * SEL_BLOCK and WINDOW % tq == 0 and NSA_TK >= WINDOW
    n_super = seq // SUPER_KEYS
    rows = NSA_GROUP * tq
    e = np.zeros((NSA_KV_HEADS, GATE_LANES, 3 * NSA_GROUP * HEAD_DIM), np.float32)
    for br in range(3):
        for hk in range(NSA_KV_HEADS):
            for g in range(NSA_GROUP):
                c = (br * NSA_GROUP + g) * HEAD_DIM
                e[hk, br * NSA_HEADS + hk * NSA_GROUP + g, c:c + HEAD_DIM] = 1.0
    nn, jj = np.arange(n16)[:, None], np.arange(nb)[None, :]
    band = ((nn >= 4 * jj - 1) & (nn <= 4 * jj + 3)).astype(np.float32)
    resident = lambda width: pl.BlockSpec((1, 1, seq, width), lambda b, h, i: (b, h, 0, 0))
    w = NSA_GROUP * HEAD_DIM
    return pl.pallas_call(
        functools.partial(_nsa_kernel, n_super),
        grid=(batch, NSA_KV_HEADS, nq),
        in_specs=[pl.BlockSpec((1, 1, 1, rows, HEAD_DIM), lambda b, h, i: (b, h, i, 0, 0)),
                  pl.BlockSpec((1, 1, n16, HEAD_DIM), lambda b, h, i: (b, h, 0, 0)),
                  pl.BlockSpec((1, 1, n16, HEAD_DIM), lambda b, h, i: (b, h, 0, 0)),
                  resident(LANES), resident(LANES), resident(LANES),
                  pl.BlockSpec((tq, GATE_LANES), lambda b, h, i: (b * nq + i, 0)),
                  pl.BlockSpec((1, GATE_LANES, 3 * w), lambda b, h, i: (h, 0, 0)),
                  pl.BlockSpec((n16, nb), lambda b, h, i: (0, 0))],
        out_specs=pl.BlockSpec((tq, w), lambda b, h, i: (b * nq + i, h)),
        out_shape=jax.ShapeDtypeStruct((batch * seq, NSA_W), BF16),
        scratch_shapes=[pltpu.VMEM((n_super, rows, LANES), BF16), pltpu.VMEM((rows, LANES), F32),
                        pltpu.VMEM((rows, LANES), F32)],
        compiler_params=_params("arbitrary", "arbitrary", "arbitrary"),
        name="nsa",
    )(qs, kcmp, vcmp, ka, va, kvw, gates, jnp.asarray(e, BF16), jnp.asarray(band, BF16))


def _odd_out_kernel(om_ref, on_ref, h_ref, w_ref, o_ref):
    acc = h_ref[...] + _dot(on_ref[...], w_ref[MOBA_W:D_MODEL, :])
    for h in range(MOBA_HEADS):
        acc = acc + _dot(om_ref[0, h], w_ref[h * HEAD_DIM:(h + 1) * HEAD_DIM, :])
    o_ref[...] = acc


def _odd_out(o_moba, o_nsa, h2, w_out, seq, tm=ROW_TILE):
    t = h2.shape[0]
    tps = seq // tm
    row = lambda w: pl.BlockSpec((tm, w), lambda i: (i, 0))
    return pl.pallas_call(
        _odd_out_kernel,
        grid=(t // tm,),
        in_specs=[pl.BlockSpec((1, MOBA_HEADS, tm, HEAD_DIM), lambda i: (i // tps, 0, i % tps, 0)),
                  row(NSA_W), row(D_MODEL), _full((D_MODEL, D_MODEL))],
        out_specs=row(D_MODEL),
        out_shape=jax.ShapeDtypeStruct((t, D_MODEL), F32),
        compiler_params=_params("arbitrary"),
        name="odd_out",
    )(o_moba, o_nsa, h2, w_out)


def _even_layer(h2, batch, seq, norm, w_in, w_out, lam_re, lam_im, log_dt, b_re, b_im, c_re, c_im, d, w_glu, conv_w, conv_b):
    u, ub, yb = _even_in(h2, norm.reshape(1, D_MODEL), w_in.astype(BF16), conv_w, conv_b.reshape(1, CONV_WIDTH), seq)
    ypre = _s5_mixer_pre(ub, batch, seq, lam_re, lam_im, log_dt, b_re, b_im, c_re, c_im)
    return _even_out(ypre, u, yb, h2, d.reshape(1, S5_WIDTH), w_glu.astype(BF16), w_out.astype(BF16))


def _odd_layer(h2, batch, seq, norm, w_in, w_out, moba_q_norm, moba_k_norm, nsa_q_norm, nsa_kcmp_norm, nsa_ksel_norm,
               nsa_kwin_norm, cmp_pe_k, cmp_w1_k, cmp_w2_k, cmp_pe_v, cmp_w1_v, cmp_w2_v):
    qm, kam, kmean, vam, qs, kvc, kas, vas, kvw, gates = _odd_in(
        h2, batch, seq, norm.reshape(1, D_MODEL), w_in, moba_q_norm, moba_k_norm, nsa_q_norm, nsa_ksel_norm, nsa_kwin_norm)
    cmp = _compress(kvc, batch, seq, cmp_pe_k, cmp_w1_k, cmp_w2_k, cmp_pe_v, cmp_w1_v, cmp_w2_v, nsa_kcmp_norm)
    o_moba = _moba(qm, kam, kmean, vam, batch, seq)
    o_nsa = _nsa(qs, cmp[0], cmp[1], kas, vas, kvw, gates, batch, seq)
    return _odd_out(o_moba, o_nsa, h2, w_out.astype(BF16), seq)


def kernel(x, ev_norm_mix, ev_w_in, ev_w_out, s5_lam_re, s5_lam_im, s5_log_dt, s5_b_re, s5_b_im, s5_c_re, s5_c_im, s5_d, s5_w_glu, conv_w, conv_b, od_norm_mix, od_w_in, od_w_out, moba_q_norm, moba_k_norm, nsa_q_norm, nsa_kcmp_norm, nsa_ksel_norm, nsa_kwin_norm, cmp_pe_k, cmp_w1_k, cmp_w2_k, cmp_pe_v, cmp_w1_v, cmp_w2_v, moe_norm, moe_w_group, moe_b_group, moe_w_expert, moe_b_expert, moe_w_gate, moe_w_up, moe_w_down):
    batch, seq, _ = x.shape
    depth = moe_norm.shape[0]
    h = x.reshape(batch * seq, D_MODEL)
    for layer in range(depth):
        i = layer // 2
        if layer % 2 == 0:
            h = _even_layer(h, batch, seq, ev_norm_mix[i], ev_w_in[i], ev_w_out[i], s5_lam_re[i], s5_lam_im[i], s5_log_dt[i],
                            s5_b_re[i], s5_b_im[i], s5_c_re[i], s5_c_im[i], s5_d[i], s5_w_glu[i], conv_w[i], conv_b[i])
        else:
            h = _odd_layer(h, batch, seq, od_norm_mix[i], od_w_in[i], od_w_out[i], moba_q_norm[i], moba_k_norm[i],
                           nsa_q_norm[i], nsa_kcmp_norm[i], nsa_ksel_norm[i], nsa_kwin_norm[i], cmp_pe_k[i], cmp_w1_k[i],
                           cmp_w2_k[i], cmp_pe_v[i], cmp_w1_v[i], cmp_w2_v[i])
        h = _moe(h, moe_norm[layer].reshape(1, D_MODEL), moe_w_group[layer], moe_b_group[layer], moe_w_expert[layer],
                 moe_b_expert[layer], moe_w_gate[layer], moe_w_up[layer], moe_w_down[layer])
    return h.reshape(batch, seq, D_MODEL)
```

```python
import functools
import math

import jax
import jax.numpy as jnp
import numpy as np
from jax import lax
from jax.experimental import pallas as pl
from jax.experimental.pallas import tpu as pltpu

D_MODEL = 1024
HEAD_DIM = 64
EPS = 1e-6
S5_WIDTH = 256
S5_GROUP = 16
S5_GROUPS = 16
S5_STATE = 64
S5_CHUNK = 16
S5_GROUP_BITS = 4
S5_STATE_BITS = 6
S5_ROW_TILE = 512
S5_COL_TILE = 512
S5_SCAN_TILE = 256
CONV_WIDTH = 768
CONV_K = 3
MOBA_HEADS = 4
NSA_HEADS = 12
NSA_KV_HEADS = 2
NSA_GROUP = 6
MOBA_W = 256
NSA_W = 768
KV_W = 128
MOBA_BLOCK = 256
MOBA_TOPK = 3
CMP_BLOCK = 32
CMP_STRIDE = 16
CMP_HIDDEN = 256
SEL_BLOCK = 64
SEL_TOPK = 8
WINDOW = 512
N_GROUPS = 4
EXPERTS_PER_GROUP = 4
N_EXPERTS = 16
EXPERT_FF = 256

LANES = 128
SUBLANES = 8
VMEM_LIMIT_BYTES = 56 * 1024 * 1024
ROW_TILE = 1024
GATE_LANES = LANES
NEG = -float(2 ** 100)
F32 = jnp.float32
BF16 = jnp.bfloat16


def _params(*semantics):
    return pltpu.CompilerParams(dimension_semantics=semantics, vmem_limit_bytes=VMEM_LIMIT_BYTES)


def _dot(a, b):
    return jnp.dot(a, b, preferred_element_type=F32)


def _dot_nt(a, b):
    return lax.dot_general(a, b, (((1,), (1,)), ((), ())), preferred_element_type=F32)


def _split(x):
    hi = x.astype(BF16)
    lo = (x - hi.astype(F32)).astype(BF16)
    return hi, lo


def _dot_x2(x, w):
    hi, lo = _split(x)
    return _dot(hi, w) + _dot(lo, w)


def _dot_x3(x, w_hilo):
    n = w_hilo.shape[1] // 2
    hi, lo = _split(x)
    both = _dot(hi, w_hilo)
    return both[:, 0:n] + (both[:, n:2 * n] + _dot(lo, w_hilo[:, 0:n]))


def _rms(x, gain):
    return x * lax.rsqrt(jnp.mean(x * x, axis=-1, keepdims=True) + EPS) * gain


def _gelu(x):
    return 0.5 * x * (1.0 + jnp.tanh(math.sqrt(2.0 / math.pi) * (x + 0.044715 * (x * x * x))))


def _sigmoid(x):
    return 1.0 / (1.0 + jnp.exp(-x))


def _full(shape):
    n = len(shape)
    return pl.BlockSpec(shape, lambda *_: (0,) * n)


def _even_in_kernel(tiles_per_seq, x_ref, g_ref, w_ref, cw_ref, cb_ref, u_ref, ub_ref, yb_ref, carry_ref):
    i = pl.program_id(0)
    xn = _rms(x_ref[...], g_ref[...]).astype(BF16)
    u = _dot(xn, w_ref[:, 0:S5_WIDTH])
    u_ref[...] = u
    ub_ref[...] = u.astype(BF16)
    o = S5_WIDTH
    xc = _dot(xn, w_ref[:, o:o + CONV_WIDTH])
    gb = _dot(xn, w_ref[:, o + CONV_WIDTH:o + 2 * CONV_WIDTH])
    gc = _dot(xn, w_ref[:, o + 2 * CONV_WIDTH:o + 3 * CONV_WIDTH])
    z = gc * xc
    tm = z.shape[0]

    @pl.when(i % tiles_per_seq == 0)
    def _():
        carry_ref[...] = jnp.zeros_like(carry_ref)

    row = lax.broadcasted_iota(jnp.int32, z.shape, 0)
    prev1 = carry_ref[SUBLANES - 1:SUBLANES, :]
    prev2 = carry_ref[SUBLANES - 2:SUBLANES - 1, :]
    z1 = jnp.where(row == 0, prev1, pltpu.roll(z, 1, 0))
    z2 = jnp.where(row == 0, prev2, jnp.where(row == 1, prev1, pltpu.roll(z, 2, 0)))
    y = cw_ref[0:1, :] * z2 + cw_ref[1:2, :] * z1 + cw_ref[2:3, :] * z + cb_ref[...]
    yb_ref[...] = (gb * y).astype(BF16)
    carry_ref[...] = z[tm - SUBLANES:tm, :]


def _even_in(x2, gain, w_in, conv_w, conv_b, seq, tm=ROW_TILE):
    t = x2.shape[0]
    n_in = w_in.shape[1]
    return pl.pallas_call(
        functools.partial(_even_in_kernel, seq // tm),
        grid=(t // tm,),
        in_specs=[pl.BlockSpec((tm, D_MODEL), lambda i: (i, 0)), _full((1, D_MODEL)),
                  _full((D_MODEL, n_in)), _full((CONV_K, CONV_WIDTH)), _full((1, CONV_WIDTH))],
        out_specs=[pl.BlockSpec((tm, S5_WIDTH), lambda i: (i, 0)), pl.BlockSpec((tm, S5_WIDTH), lambda i: (i, 0)),
                   pl.BlockSpec((tm, CONV_WIDTH), lambda i: (i, 0))],
        out_shape=[jax.ShapeDtypeStruct((t, S5_WIDTH), F32), jax.ShapeDtypeStruct((t, S5_WIDTH), BF16),
                   jax.ShapeDtypeStruct((t, CONV_WIDTH), BF16)],
        scratch_shapes=[pltpu.VMEM((SUBLANES, CONV_WIDTH), F32)],
        compiler_params=_params("arbitrary"),
        name="even_in",
    )(x2, gain, w_in, conv_w, conv_b)


def _s5_weights(lam_re, lam_im, log_dt, b_re, b_im, c_re, c_im):
    g, p, hg, ck = S5_GROUPS, S5_STATE, S5_GROUP, S5_CHUNK
    lr, li = lam_re.astype(F32), lam_im.astype(F32)
    dt = jnp.exp(log_dt.astype(F32))[:, None]
    mag = jnp.exp(lr * dt)
    a_re, a_im = mag * jnp.cos(li * dt), mag * jnp.sin(li * dt)
    den = lr * lr + li * li
    f_re = ((a_re - 1.0) * lr + a_im * li) / den
    f_im = (a_im * lr - (a_re - 1.0) * li) / den
    br, bi = b_re.astype(F32), b_im.astype(F32)
    bb_re = f_re[..., None] * br - f_im[..., None] * bi
    bb_im = f_re[..., None] * bi + f_im[..., None] * br
    pw_re, pw_im = [jnp.ones_like(a_re)], [jnp.zeros_like(a_im)]
    for _ in range(ck):
        r, m = pw_re[-1], pw_im[-1]
        pw_re.append(r * a_re - m * a_im)
        pw_im.append(r * a_im + m * a_re)
    pw_re, pw_im = jnp.stack(pw_re), jnp.stack(pw_im)
    cr, ci = c_re.astype(F32), c_im.astype(F32)
    rev_re, rev_im = pw_re[ck - 1::-1][:ck], pw_im[ck - 1::-1][:ck]
    ws_re = rev_re[:, :, :, None] * bb_re[None] - rev_im[:, :, :, None] * bb_im[None]
    ws_im = rev_re[:, :, :, None] * bb_im[None] + rev_im[:, :, :, None] * bb_re[None]
    ca_re = cr[None] * pw_re[1:, :, None, :] - ci[None] * pw_im[1:, :, None, :]
    ca_im = cr[None] * pw_im[1:, :, None, :] + ci[None] * pw_re[1:, :, None, :]
    cb_re = jnp.einsum('ghp,kgp,gpj->kghj', cr, pw_re[:ck], bb_re) - jnp.einsum('ghp,kgp,gpj->kghj', cr, pw_im[:ck], bb_im) \
        - jnp.einsum('ghp,kgp,gpj->kghj', ci, pw_re[:ck], bb_im) - jnp.einsum('ghp,kgp,gpj->kghj', ci, pw_im[:ck], bb_re)
    lag = np.arange(ck)[None, :] - np.arange(ck)[:, None]
    tz = cb_re[np.clip(lag, 0, ck - 1)]
    tz = jnp.where((lag >= 0)[:, :, None, None, None], tz, 0.0)
    cw = ck * g * hg
    ws = jnp.stack([ws_re, ws_im]).transpose(1, 2, 4, 0, 3).reshape(cw, 2 * p)
    wc = jnp.stack([ca_re, -ca_im]).transpose(0, 2, 4, 1, 3).reshape(2 * g * p, ck * hg)
    tzc = tz.transpose(0, 2, 4, 1, 3).reshape(cw, ck * hg)
    return ws.astype(BF16), wc.astype(BF16), tzc.astype(BF16), pw_re[ck].reshape(1, g * p), pw_im[ck].reshape(1, g * p)


def _group_expand(compact, expand, row_shift, col_shift, col0):
    full = _dot(compact, expand)
    row = lax.broadcasted_iota(jnp.int32, full.shape, 0)
    col = col0 + lax.broadcasted_iota(jnp.int32, full.shape, 1)
    same = ((row >> row_shift) & (S5_GROUPS - 1)) == ((col >> col_shift) & (S5_GROUPS - 1))
    return jnp.where(same, full, 0.0).astype(BF16)


def _s5_state_kernel(u_ref, ws_ref, e_ref, s_ref, w_scr):
    @pl.when(pl.program_id(1) == 0)
    def _():
        w_scr[...] = _group_expand(ws_ref[...], e_ref[...], S5_GROUP_BITS, S5_STATE_BITS,
                                   pl.program_id(0) * w_scr.shape[1])

    s_ref[...] = _dot(u_ref[...], w_scr[...])


def _s5_scan_kernel(s_ref, are_ref, aim_ref, xprev_ref, st_ref):
    @pl.when(pl.program_id(0) == 0)
    def _():
        st_ref[...] = jnp.zeros_like(st_ref)

    a_re, a_im = are_ref[...], aim_ref[...]
    nb, n = s_ref.shape[0], s_ref.shape[1]
    half = a_re.shape[1]

    def body(c, carry):
        out = []
        for b in range(nb):
            xr, xi = carry[2 * b], carry[2 * b + 1]
            xprev_ref[b, pl.ds(c, 1), 0:half] = xr
            xprev_ref[b, pl.ds(c, 1), half:2 * half] = xi
            s = s_ref[b, pl.ds(c, 1), :]
            out += [a_re * xr - a_im * xi + s[:, 0:half], a_re * xi + a_im * xr + s[:, half:2 * half]]
        return tuple(out)

    init = tuple(st_ref[b:b + 1, o:o + half] for b in range(nb) for o in (0, half))
    final = lax.fori_loop(0, n, body, init, unroll=4)
    for b in range(nb):
        st_ref[b:b + 1, 0:half] = final[2 * b]
        st_ref[b:b + 1, half:2 * half] = final[2 * b + 1]


def _s5_out_kernel(u_ref, xp_ref, tz_ref, wc_ref, e_ref, y_ref, tz_scr, wc_scr):
    @pl.when(pl.program_id(1) == 0)
    def _():
        col0 = pl.program_id(0) * tz_scr.shape[1]
        tz_scr[...] = _group_expand(tz_ref[...], e_ref[...], S5_GROUP_BITS, S5_GROUP_BITS, col0)
        wc_scr[...] = _group_expand(wc_ref[...], e_ref[...], S5_STATE_BITS, S5_GROUP_BITS, col0)

    y_ref[...] = _dot(u_ref[...], tz_scr[...]) + _dot(xp_ref[...].astype(BF16), wc_scr[...])


def _expand_matrix(outer, inner):
    e = np.zeros((outer, inner, outer, S5_GROUPS, inner), np.float32)
    for x in range(outer):
        for y in range(inner):
            e[x, y, x, :, y] = 1.0
    return jnp.asarray(e.reshape(outer * inner, outer * S5_GROUPS * inner), BF16)


def _s5_mixer_pre(ub, batch, seq, lam_re, lam_im, log_dt, b_re, b_im, c_re, c_im):
    ws, wc, tz, a16_re, a16_im = _s5_weights(lam_re, lam_im, log_dt, b_re, b_im, c_re, c_im)
    nc = seq // S5_CHUNK
    rows = batch * nc
    cw = S5_CHUNK * S5_WIDTH
    sw = 2 * S5_GROUPS * S5_STATE
    assert S5_GROUP == 1 << S5_GROUP_BITS and S5_STATE == 1 << S5_STATE_BITS
    tr = min(rows, S5_ROW_TILE)
    tn = S5_COL_TILE
    uc = ub.reshape(rows, cw)
    e_state = _expand_matrix(2, S5_STATE)
    e_out = _expand_matrix(S5_CHUNK, S5_GROUP)
    s = pl.pallas_call(
        _s5_state_kernel,
        grid=(sw // tn, rows // tr),
        in_specs=[pl.BlockSpec((tr, cw), lambda j, i: (i, 0)), _full(ws.shape),
                  pl.BlockSpec((e_state.shape[0], tn), lambda j, i: (0, j))],
        out_specs=pl.BlockSpec((tr, tn), lambda j, i: (i, j)),
        out_shape=jax.ShapeDtypeStruct((rows, sw), F32),
        scratch_shapes=[pltpu.VMEM((cw, tn), BF16)],
        compiler_params=_params("arbitrary", "arbitrary"),
        name="s5_state",
    )(uc, ws, e_state)
    tc = min(nc, S5_SCAN_TILE)
    xprev = pl.pallas_call(
        _s5_scan_kernel,
        grid=(nc // tc,),
        in_specs=[pl.BlockSpec((batch, tc, sw), lambda i: (0, i, 0)), _full((1, sw // 2)), _full((1, sw // 2))],
        out_specs=pl.BlockSpec((batch, tc, sw), lambda i: (0, i, 0)),
        out_shape=jax.ShapeDtypeStruct((batch, nc, sw), F32),
        scratch_shapes=[pltpu.VMEM((batch, sw), F32)],
        compiler_params=_params("arbitrary"),
        name="s5_scan",
    )(s.reshape(batch, nc, sw), a16_re, a16_im)
    y = pl.pallas_call(
        _s5_out_kernel,
        grid=(cw // tn, rows // tr),
        in_specs=[pl.BlockSpec((tr, cw), lambda j, i: (i, 0)), pl.BlockSpec((tr, sw), lambda j, i: (i, 0)),
                  _full(tz.shape), _full(wc.shape), pl.BlockSpec((e_out.shape[0], tn), lambda j, i: (0, j))],
        out_specs=pl.BlockSpec((tr, tn), lambda j, i: (i, j)),
        out_shape=jax.ShapeDtypeStruct((rows, cw), F32),
        scratch_shapes=[pltpu.VMEM((cw, tn), BF16), pltpu.VMEM((sw, tn), BF16)],
        compiler_params=_params("arbitrary", "arbitrary"),
        name="s5_out",
    )(uc, xprev.reshape(rows, sw), tz, wc, e_out)
    return y.reshape(batch * seq, S5_WIDTH)


def _even_out_kernel(ypre_ref, u_ref, yb_ref, x_ref, d_ref, wglu_ref, wout_ref, o_ref):
    y = _gelu(ypre_ref[...] + d_ref[...] * u_ref[...])
    y = y * _sigmoid(_dot(y.astype(BF16), wglu_ref[...]))
    o_ref[...] = (x_ref[...] + _dot(y.astype(BF16), wout_ref[0:S5_WIDTH, :])
                  + _dot(yb_ref[...], wout_ref[S5_WIDTH:D_MODEL, :]))


def _even_out(ypre, u, yb, x2, d, w_glu, w_out, tm=ROW_TILE):
    t = x2.shape[0]
    row = lambda w: pl.BlockSpec((tm, w), lambda i: (i, 0))
    return pl.pallas_call(
        _even_out_kernel,
        grid=(t // tm,),
        in_specs=[row(S5_WIDTH), row(S5_WIDTH), row(CONV_WIDTH), row(D_MODEL), _full((1, S5_WIDTH)),
                  _full((S5_WIDTH, S5_WIDTH)), _full((D_MODEL, D_MODEL))],
        out_specs=row(D_MODEL),
        out_shape=jax.ShapeDtypeStruct((t, D_MODEL), F32),
        compiler_params=_params("arbitrary"),
        name="even_out",
    )(ypre, u, yb, x2, d, w_glu, w_out)


def _first_max(v, pos, width, axis=-1):
    m = jnp.max(v, axis=axis, keepdims=True)
    idx = jnp.min(jnp.where(v == m, pos, width), axis=axis, keepdims=True)
    return m, idx


MOE_TM = 1024
MOE_ALIGN = 16
MOE_SLOTS = 1152
MOE_WIN = 320
ROUTER_LANES = LANES


def _moe_router_kernel(h_ref, g_ref, wr_ref, br_ref, gate_ref, grp_ref, cnt_ref):
    xn = _rms(h_ref[...], g_ref[...])
    logits = _dot_x3(xn, wr_ref[...]) + br_ref[...]
    lane = lax.broadcasted_iota(jnp.int32, logits.shape, 1).astype(F32)
    width = float(logits.shape[1])
    is_g = lane < N_GROUPS
    gl = jnp.where(is_g, logits, -jnp.inf)
    gm, gi = _first_max(gl, lane, width)
    gw = 1.0 / jnp.sum(jnp.where(is_g, jnp.exp(gl - gm), 0.0), axis=-1, keepdims=True)
    lo = N_GROUPS + gi * EXPERTS_PER_GROUP
    in_grp = (lane >= lo) & (lane < lo + EXPERTS_PER_GROUP)
    el = jnp.where(in_grp, logits, -jnp.inf)
    m1, i1 = _first_max(el, lane, width)
    m2, i2 = _first_max(jnp.where(lane == i1, -jnp.inf, el), lane, width)
    p2 = jnp.exp(m2 - m1)
    w1 = gw / (1.0 + p2)
    w2 = gw * p2 / (1.0 + p2)
    gate_ref[...] = jnp.where(lane == i1, w1, 0.0) + jnp.where(lane == i2, w2, 0.0)
    grp = (lane == gi).astype(F32)
    grp_ref[...] = grp.astype(BF16)
    cnt_ref[0] = jnp.broadcast_to(jnp.sum(grp, axis=0, keepdims=True), cnt_ref.shape[1:])


def _moe_expert_kernel(base_ref, nwin_ref, h_ref, g_ref, gate_ref, grp_ref, wg_ref, wu_ref, wd_ref, o_ref,
                       xs_ref, gs_ref, ys_ref, pt_ref):
    i, g = pl.program_id(0), pl.program_id(1)
    tm = h_ref.shape[0]

    @pl.when((i == 0) & (g == 0))
    def _():
        xs_ref[...] = jnp.zeros_like(xs_ref)
        gs_ref[...] = jnp.zeros_like(gs_ref)
        ys_ref[...] = jnp.zeros_like(ys_ref)

    @pl.when(g == 0)
    def _():
        xn = _rms(h_ref[...], g_ref[...]).astype(BF16)
        grp = grp_ref[...]
        earlier = (lax.broadcasted_iota(jnp.int32, (tm, tm), 0) > lax.broadcasted_iota(jnp.int32, (tm, tm), 1)).astype(BF16)
        rank = _dot(earlier, grp)
        lane = lax.broadcasted_iota(jnp.int32, rank.shape, 1)
        for k in range(N_GROUPS):
            rank = rank + jnp.where(lane == k, base_ref[i * N_GROUPS + k].astype(F32), 0.0)
        slot = jnp.sum(grp.astype(F32) * rank, axis=-1, keepdims=True).astype(jnp.int32)
        pt = (lax.broadcasted_iota(jnp.int32, (tm, MOE_SLOTS), 1) == slot).astype(BF16)
        pt_ref[...] = pt
        gather = lambda x: lax.dot_general(pt, x, (((0,), (0,)), ((), ())), preferred_element_type=F32)
        xs_ref[0:MOE_SLOTS, :] = gather(xn).astype(BF16)
        both = gather(jnp.concatenate(_split(gate_ref[...]), axis=1))
        gs_ref[0:MOE_SLOTS, :] = both[:, 0:ROUTER_LANES] + both[:, ROUTER_LANES:2 * ROUTER_LANES]

    def window(w, carry):
        r0 = pl.multiple_of(base_ref[i * N_GROUPS + g] + w * MOE_WIN, MOE_ALIGN)
        x = xs_ref[pl.ds(r0, MOE_WIN), :]
        gate = gs_ref[pl.ds(r0, MOE_WIN), :]
        lane = lax.broadcasted_iota(jnp.int32, gate.shape, 1)
        y = None
        for j in range(EXPERTS_PER_GROUP):
            ge = jnp.sum(jnp.where(lane == g * EXPERTS_PER_GROUP + (j + N_GROUPS), gate, 0.0), axis=-1, keepdims=True)
            h1 = _dot(x, wg_ref[j])
            h3 = _dot(x, wu_ref[j])
            act = (h1 * _sigmoid(h1)) * h3 * ge
            yj = _dot(act.astype(BF16), wd_ref[j])
            y = yj if y is None else y + yj
        ys_ref[pl.ds(r0, MOE_WIN), :] = y.astype(BF16)
        return carry

    lax.fori_loop(0, nwin_ref[i * N_GROUPS + g], window, 0)

    @pl.when(g == N_GROUPS - 1)
    def _():
        o_ref[...] = h_ref[...] + _dot(pt_ref[...], ys_ref[0:MOE_SLOTS, :])


def _moe(h2, gain, w_group, b_group, w_expert, b_expert, w_gate, w_up, w_down):
    t = h2.shape[0]
    tm, rw = MOE_TM, ROUTER_LANES
    assert t % tm == 0 and MOE_SLOTS >= tm + N_GROUPS * (MOE_ALIGN - 1) and MOE_WIN % MOE_ALIGN == 0
    n_tiles = t // tm
    wr = jnp.zeros((D_MODEL, rw), F32).at[:, 0:N_GROUPS].set(w_group).at[:, N_GROUPS:N_GROUPS + N_EXPERTS].set(w_expert)
    br = jnp.zeros((1, rw), F32).at[0, 0:N_GROUPS].set(b_group).at[0, N_GROUPS:N_GROUPS + N_EXPERTS].set(b_expert)
    wr_hilo = jnp.concatenate(_split(wr), axis=1)
    row = lambda width: pl.BlockSpec((tm, width), lambda i: (i, 0))
    gates, grp, cnt = pl.pallas_call(
        _moe_router_kernel,
        grid=(n_tiles,),
        in_specs=[row(D_MODEL), _full((1, D_MODEL)), _full((D_MODEL, 2 * rw)), _full((1, rw))],
        out_specs=[row(rw), row(rw), pl.BlockSpec((1, SUBLANES, rw), lambda i: (i, 0, 0))],
        out_shape=[jax.ShapeDtypeStruct((t, rw), F32), jax.ShapeDtypeStruct((t, rw), BF16),
                   jax.ShapeDtypeStruct((n_tiles, SUBLANES, rw), F32)],
        compiler_params=_params("arbitrary"),
        name="moe_router",
    )(h2, gain, wr_hilo, br)
    n = cnt[:, 0, 0:N_GROUPS].astype(jnp.int32)
    padded = (n + (MOE_ALIGN - 1)) // MOE_ALIGN * MOE_ALIGN
    base = (jnp.cumsum(padded, axis=1) - padded).reshape(-1)
    nwin = ((padded + (MOE_WIN - 1)) // MOE_WIN).reshape(-1)
    tile = lambda width: pl.BlockSpec((tm, width), lambda i, g, *_: (i, 0))
    experts = lambda shape: pl.BlockSpec((EXPERTS_PER_GROUP,) + shape, lambda i, g, *_: (g, 0, 0))
    slots = MOE_SLOTS + MOE_WIN
    return pl.pallas_call(
        _moe_expert_kernel,
        grid_spec=pltpu.PrefetchScalarGridSpec(
            num_scalar_prefetch=2,
            grid=(n_tiles, N_GROUPS),
            in_specs=[tile(D_MODEL), pl.BlockSpec((1, D_MODEL), lambda i, g, *_: (0, 0)), tile(rw), tile(rw),
                      experts((D_MODEL, EXPERT_FF)), experts((D_MODEL, EXPERT_FF)), experts((EXPERT_FF, D_MODEL))],
            out_specs=tile(D_MODEL),
            scratch_shapes=[pltpu.VMEM((slots, D_MODEL), BF16), pltpu.VMEM((slots, rw), F32),
                            pltpu.VMEM((slots, D_MODEL), BF16), pltpu.VMEM((tm, MOE_SLOTS), BF16)]),
        out_shape=jax.ShapeDtypeStruct((t, D_MODEL), F32),
        compiler_params=_params("arbitrary", "arbitrary"),
        name="moe",
    )(base, nwin, h2, gain, gates, grp, w_gate.astype(BF16), w_up.astype(BF16), w_down.astype(BF16))


ODD_SPLITS = (MOBA_W, MOBA_W, MOBA_W, NSA_W, KV_W, KV_W, KV_W, KV_W, KV_W, KV_W, GATE_LANES)
ODD_IN_PAD = sum(ODD_SPLITS)


def _head_rms(x, hsum, gain):
    w = x.shape[1]
    ss = jnp.concatenate([_dot_x2(x[:, o:o + hsum.shape[0]] * x[:, o:o + hsum.shape[0]], hsum)
                          for o in range(0, w, hsum.shape[0])], axis=1) if w > hsum.shape[0] else _dot_x2(x * x, hsum)
    return x * lax.rsqrt(ss * (1.0 / HEAD_DIM) + EPS) * gain


def _odd_in_kernel(tiles_per_seq, x_ref, g_ref, w_ref, hsum_ref, gq_ref, gk_ref, gnq_ref, gks_ref, gkw_ref,
                   qm_ref, kam_ref, kmean_ref, vam_ref, qs_ref, kvc_ref, kas_ref, vas_ref, kvw_ref, gt_ref):
    xn = _rms(x_ref[...], g_ref[...]).astype(BF16)
    offs = np.cumsum((0,) + ODD_SPLITS)
    col = lambda j: _dot(xn, w_ref[:, int(offs[j]):int(offs[j + 1])])
    head = lambda x, h: x[:, h * HEAD_DIM:(h + 1) * HEAD_DIM]
    hsum = hsum_ref[...]
    hsum128 = hsum_ref[0:KV_W, 0:KV_W]
    tm = x_ref.shape[0]
    pos = (pl.program_id(0) % tiles_per_seq) * tm + lax.broadcasted_iota(jnp.int32, (tm, HEAD_DIM), 0)
    lane = lax.broadcasted_iota(jnp.int32, (tm, HEAD_DIM), 1)
    ones_col = (lane == 0).astype(BF16)

    qm = _head_rms(col(0), hsum, gq_ref[...])
    km = _head_rms(col(1), hsum, gk_ref[...])
    for j in range(tm // MOBA_BLOCK):
        kmean_ref[0, j:j + 1, :] = jnp.mean(km[j * MOBA_BLOCK:(j + 1) * MOBA_BLOCK, :], axis=0, keepdims=True)
    km = km.astype(BF16)
    vm = col(2).astype(BF16)
    moba_id = (lane == pos // MOBA_BLOCK).astype(BF16)
    for h in range(MOBA_HEADS):
        qm_ref[0, h] = head(qm, h)
        kam_ref[0, h] = jnp.concatenate([head(km, h), moba_id], axis=1)
        vam_ref[0, h] = jnp.concatenate([head(vm, h), ones_col], axis=1)

    qd = (_head_rms(col(3), hsum, gnq_ref[...]) * (HEAD_DIM ** -0.5)).astype(BF16)
    for hk in range(NSA_KV_HEADS):
        for j in range(tm // NSA_TQ):
            for g in range(NSA_GROUP):
                qs_ref[0, hk, j, g * NSA_TQ:(g + 1) * NSA_TQ, :] = head(qd, hk * NSA_GROUP + g)[j * NSA_TQ:(j + 1) * NSA_TQ, :]
    kvc_ref[0] = col(4)
    kvc_ref[1] = col(5)
    ks = _head_rms(col(6), hsum128, gks_ref[...]).astype(BF16)
    vs = col(7).astype(BF16)
    kw = _head_rms(col(8), hsum128, gkw_ref[...]).astype(BF16)
    vw = col(9).astype(BF16)
    sel_id = (lane == (pos // SEL_BLOCK) % SEL_LANES).astype(BF16)
    for hk in range(NSA_KV_HEADS):
        kas_ref[0, hk] = jnp.concatenate([head(ks, hk), sel_id], axis=1)
        vas_ref[0, hk] = jnp.concatenate([head(vs, hk), ones_col], axis=1)
        kvw_ref[0, hk] = jnp.concatenate([head(kw, hk), head(vw, hk)], axis=1)
    gt_ref[...] = _sigmoid(col(10))


def _odd_in(h2, batch, seq, gain, w_in, moba_q_norm, moba_k_norm, nsa_q_norm, nsa_ksel_norm, nsa_kwin_norm, tm=ROW_TILE):
    t = h2.shape[0]
    assert MOBA_LANES == HEAD_DIM and SEL_LANES == HEAD_DIM and seq % tm == 0 and tm % MOBA_BLOCK == 0
    tps = seq // tm
    w = jnp.pad(w_in, ((0, 0), (0, ODD_IN_PAD - w_in.shape[1]))).astype(BF16)
    hsum = jnp.asarray(np.kron(np.eye(MOBA_W // HEAD_DIM), np.ones((HEAD_DIM, HEAD_DIM))), BF16)
    tile = lambda g, width: jnp.tile(g.astype(F32), width // HEAD_DIM).reshape(1, width)
    row = lambda width: pl.BlockSpec((tm, width), lambda i: (i, 0))
    heads = lambda n, width: pl.BlockSpec((1, n, tm, width), lambda i: (i // tps, 0, i % tps, 0))
    nmb = tm // MOBA_BLOCK
    nqt = tm // NSA_TQ
    rows = NSA_GROUP * NSA_TQ
    sds = jax.ShapeDtypeStruct
    out_specs = [heads(MOBA_HEADS, HEAD_DIM), heads(MOBA_HEADS, LANES), pl.BlockSpec((1, nmb, MOBA_W), lambda i: (i, 0, 0)),
                 heads(MOBA_HEADS, LANES),
                 pl.BlockSpec((1, NSA_KV_HEADS, nqt, rows, HEAD_DIM), lambda i: (i // tps, 0, i % tps, 0, 0)),
                 pl.BlockSpec((2, tm, KV_W), lambda i: (0, i, 0)),
                 heads(NSA_KV_HEADS, LANES), heads(NSA_KV_HEADS, LANES), heads(NSA_KV_HEADS, LANES),
                 row(GATE_LANES)]
    out_shape = [sds((batch, MOBA_HEADS, seq, HEAD_DIM), F32), sds((batch, MOBA_HEADS, seq, LANES), BF16),
                 sds((t // tm, nmb, MOBA_W), F32), sds((batch, MOBA_HEADS, seq, LANES), BF16),
                 sds((batch, NSA_KV_HEADS, seq // NSA_TQ, rows, HEAD_DIM), BF16),
                 sds((2, t, KV_W), F32), sds((batch, NSA_KV_HEADS, seq, LANES), BF16),
                 sds((batch, NSA_KV_HEADS, seq, LANES), BF16), sds((batch, NSA_KV_HEADS, seq, LANES), BF16),
                 sds((t, GATE_LANES), F32)]
    return pl.pallas_call(
        functools.partial(_odd_in_kernel, tps),
        grid=(t // tm,),
        in_specs=[row(D_MODEL), _full((1, D_MODEL)), _full((D_MODEL, ODD_IN_PAD)), _full((MOBA_W, MOBA_W)),
                  _full((1, MOBA_W)), _full((1, MOBA_W)), _full((1, NSA_W)), _full((1, KV_W)), _full((1, KV_W))],
        out_specs=out_specs,
        out_shape=out_shape,
        compiler_params=_params("arbitrary"),
        name="odd_in",
    )(h2, gain, w, hsum, tile(moba_q_norm, MOBA_W), tile(moba_k_norm, MOBA_W), tile(nsa_q_norm, NSA_W),
      tile(nsa_ksel_norm, KV_W), tile(nsa_kwin_norm, KV_W))


def _compress_kernel(c_ref, w1_ref, w2_ref, pe_ref, g_ref, o_ref):
    kind = pl.program_id(0)
    n16 = c_ref.shape[1] // CMP_STRIDE
    half = CMP_STRIDE * HEAD_DIM
    peb = _dot(pe_ref[0], w1_ref[0])[0:1, :]
    xs = [c_ref[0, pl.ds(s, n16, stride=CMP_STRIDE), :].astype(BF16) for s in range(CMP_STRIDE)]
    for h in range(NSA_KV_HEADS):
        first = second = None
        for s in range(CMP_STRIDE):
            x = xs[s][:, h * HEAD_DIM:(h + 1) * HEAD_DIM]
            a = _dot(x, w1_ref[0, s * HEAD_DIM:(s + 1) * HEAD_DIM, :])
            b = _dot(x, w1_ref[0, half + s * HEAD_DIM:half + (s + 1) * HEAD_DIM, :])
            first = a if first is None else first + a
            second = b if second is None else second + b
        hid = _gelu(first + pltpu.roll(second, n16 - 1, 0) + peb)
        out = _dot(hid.astype(BF16), w2_ref[0])
        o_ref[0, 0, h] = jnp.where(kind == 0, _rms(out, g_ref[...]), out).astype(BF16)


def _compress(kvc, batch, seq, pe_k, w1_k, w2_k, pe_v, w1_v, w2_v, kcmp_norm):
    n16 = seq // CMP_STRIDE
    half = CMP_STRIDE * HEAD_DIM
    w1 = jnp.stack([w1_k, w1_v]).astype(BF16)
    w2 = jnp.stack([w2_k, w2_v]).astype(BF16)
    pe = jnp.stack([pe_k, pe_v]).reshape(2, 1, 2 * half)
    pe = jnp.broadcast_to(pe, (2, SUBLANES, 2 * half)).astype(BF16)
    return pl.pallas_call(
        _compress_kernel,
        grid=(2, batch),
        in_specs=[pl.BlockSpec((1, seq, KV_W), lambda k, b: (k, b, 0)),
                  pl.BlockSpec((1, 2 * half, CMP_HIDDEN), lambda k, b: (k, 0, 0)),
                  pl.BlockSpec((1, CMP_HIDDEN, HEAD_DIM), lambda k, b: (k, 0, 0)),
                  pl.BlockSpec((1, SUBLANES, 2 * half), lambda k, b: (k, 0, 0)),
                  _full((1, HEAD_DIM))],
        out_specs=pl.BlockSpec((1, 1, NSA_KV_HEADS, n16, HEAD_DIM), lambda k, b: (k, b, 0, 0, 0)),
        out_shape=jax.ShapeDtypeStruct((2, batch, NSA_KV_HEADS, n16, HEAD_DIM), BF16),
        compiler_params=_params("arbitrary", "arbitrary"),
        name="nsa_compress",
    )(kvc, w1, w2, pe, kcmp_norm.astype(F32).reshape(1, HEAD_DIM))


M_INIT = -1e30


def _softmax_init(m_ref, acc_ref):
    m_ref[...] = jnp.full(m_ref.shape, M_INIT, F32)
    acc_ref[...] = jnp.zeros(acc_ref.shape, F32)


def _softmax_step(s, v_aug, m_ref, acc_ref):
    m_old = m_ref[...]
    m_new = jnp.maximum(m_old, jnp.max(s, axis=-1, keepdims=True))
    alpha = jnp.exp(m_old - m_new)
    p = jnp.exp(s - jnp.tile(m_new, (1, s.shape[1] // LANES)))
    acc_ref[...] = alpha * acc_ref[...] + _dot(p.astype(BF16), v_aug)
    m_ref[...] = m_new


def _past_keys_loop(n_keys, tile, step, riders=(), parts=2):
    n_full = n_keys // tile

    def body(j, carry):
        step(pl.multiple_of(j * tile, tile), tile)
        return carry

    extras = []
    for k, (with_tile, alone) in enumerate(riders):
        def both(with_tile=with_tile, k=k):
            out = with_tile()
            step(k * tile, tile)
            return out

        extras.append(lax.cond(n_full > k, both, alone))
    lax.fori_loop(len(riders), n_full, body, 0)
    rest = n_keys - n_full * tile
    start = pl.multiple_of(n_full * tile, tile)
    part = tile // parts
    n_parts = (rest + part - 1) // part
    for k in range(1, parts + 1):
        @pl.when(n_parts == k)
        def _(k=k):
            done = 0
            for size in (tile, tile // 2, tile // 4):
                if size % part == 0 and k * part - done >= size:
                    step(pl.multiple_of(start + done, part), size)
                    done += size
            assert done == k * part

    return extras


def _softmax_result(acc_ref):
    acc = acc_ref[...]
    return acc[:, 0:HEAD_DIM] * (1.0 / acc[:, HEAD_DIM:HEAD_DIM + 1])


def _pick_top(score, pos, width, k):
    sel = jnp.zeros(score.shape, jnp.bool_)
    for _ in range(k):
        m, idx = _first_max(score, pos, width, axis=0)
        hit = (pos == idx) & (m > -jnp.inf)
        sel = sel | hit
        score = jnp.where(pos == idx, -jnp.inf, score)
    return sel


MOBA_LANES = 64
MOBA_TK = 2048


MOBA_TQ = 4 * MOBA_BLOCK


def _moba_kernel(q_ref, ka_ref, va_ref, kmean_ref, o_ref, qa_ref, m_ref, acc_ref):
    i0 = pl.program_id(2) * (MOBA_TQ // MOBA_BLOCK)
    q = q_ref[0, 0]
    q_hi, q_lo = _split(q)
    km_hi, km_lo = _split(kmean_ref[0, 0])
    gate = _dot_nt(km_hi, q_hi) + (_dot_nt(km_lo, q_hi) + _dot_nt(km_hi, q_lo))
    blk = lax.broadcasted_iota(jnp.int32, gate.shape, 0)
    cur = i0 + lax.broadcasted_iota(jnp.int32, gate.shape, 1) // MOBA_BLOCK
    sel = _pick_top(jnp.where(blk < cur, gate, -jnp.inf), blk, gate.shape[0], MOBA_TOPK)
    qs = (q * (HEAD_DIM ** -0.5)).astype(BF16)
    past = jnp.where(sel & (blk < i0), 0.0, NEG).T[:, 0:MOBA_LANES]
    own = jnp.where((blk == cur) | (sel & (blk >= i0)), 0.0, NEG).T[:, 0:MOBA_LANES]
    qa_ref[0] = jnp.concatenate([qs, past.astype(BF16)], axis=1)
    qa_ref[1] = jnp.concatenate([qs, own.astype(BF16)], axis=1)
    _softmax_init(m_ref, acc_ref)

    def step(start, size):
        s = _dot_nt(qa_ref[0], ka_ref[0, 0, pl.ds(start, size), :])
        _softmax_step(s, va_ref[0, 0, pl.ds(start, size), :], m_ref, acc_ref)

    _past_keys_loop(i0 * MOBA_BLOCK, MOBA_TK, step)
    start = pl.multiple_of(i0 * MOBA_BLOCK, MOBA_TQ)
    s = _dot_nt(qa_ref[1], ka_ref[0, 0, pl.ds(start, MOBA_TQ), :])
    qpos = lax.broadcasted_iota(jnp.int32, s.shape, 0)
    kpos = lax.broadcasted_iota(jnp.int32, s.shape, 1)
    hidden = (qpos // MOBA_BLOCK == kpos // MOBA_BLOCK) & (kpos > qpos)
    _softmax_step(jnp.where(hidden, NEG, s), va_ref[0, 0, pl.ds(start, MOBA_TQ), :], m_ref, acc_ref)
    o_ref[0, 0] = _softmax_result(acc_ref).astype(BF16)


def _moba(qm, ka, kmean, va, batch, seq):
    nmb = seq // MOBA_BLOCK
    assert nmb <= MOBA_LANES and seq % MOBA_TK == 0 and seq % MOBA_TQ == 0
    kmean = kmean.reshape(batch, nmb, MOBA_HEADS, HEAD_DIM).transpose(0, 2, 1, 3)
    kmean = jnp.pad(kmean, ((0, 0), (0, 0), (0, LANES - nmb), (0, 0)))
    return pl.pallas_call(
        _moba_kernel,
        grid=(batch, MOBA_HEADS, seq // MOBA_TQ),
        in_specs=[pl.BlockSpec((1, 1, MOBA_TQ, HEAD_DIM), lambda b, h, i: (b, h, i, 0)),
                  pl.BlockSpec((1, 1, seq, LANES), lambda b, h, i: (b, h, 0, 0)),
                  pl.BlockSpec((1, 1, seq, LANES), lambda b, h, i: (b, h, 0, 0)),
                  pl.BlockSpec((1, 1, LANES, HEAD_DIM), lambda b, h, i: (b, h, 0, 0))],
        out_specs=pl.BlockSpec((1, 1, MOBA_TQ, HEAD_DIM), lambda b, h, i: (b, h, i, 0)),
        out_shape=jax.ShapeDtypeStruct((batch, MOBA_HEADS, seq, HEAD_DIM), BF16),
        scratch_shapes=[pltpu.VMEM((2, MOBA_TQ, LANES), BF16), pltpu.VMEM((MOBA_TQ, LANES), F32),
                        pltpu.VMEM((MOBA_TQ, LANES), F32)],
        compiler_params=_params("arbitrary", "arbitrary", "arbitrary"),
        name="moba",
    )(qm, ka, va, kmean)


NSA_TQ = 2 * SEL_BLOCK
NSA_TK = 2048
SEL_LANES = 64
SUPER_KEYS = SEL_LANES * SEL_BLOCK
CMP_WIDTH_STEP = 256


def _nsa_kernel(n_super, q_ref, kc_ref, vc_ref, ka_ref, va_ref, kvw_ref, gt_ref, e_ref, band_ref, o_ref,
                qa_ref, m_ref, acc_ref):
    qi = pl.program_id(2)
    tq = NSA_TQ
    s0 = qi * tq
    q = q_ref[0, 0, 0]

    def compressed(width):
        sc = _dot_nt(q, kc_ref[0, 0, 0:width, :])
        rq = lax.broadcasted_iota(jnp.int32, (sc.shape[0], LANES), 0) & (tq - 1)
        last = jnp.tile((s0 + rq - (CMP_BLOCK - 1)) >> (CMP_STRIDE.bit_length() - 1), (1, width // LANES))
        sc = jnp.where(lax.broadcasted_iota(jnp.int32, sc.shape, 1) <= last, sc, NEG)
        mx = jnp.max(sc, axis=-1, keepdims=True)
        e = jnp.exp(sc - mx)
        r = jnp.where(mx > 0.5 * NEG, 1.0 / jnp.sum(e, axis=-1, keepdims=True), 0.0)
        imp = e[0:tq] * r[0:tq]
        for g in range(1, NSA_GROUP):
            imp = imp + e[g * tq:(g + 1) * tq] * r[g * tq:(g + 1) * tq]
        return _dot(e.astype(BF16), vc_ref[0, 0, 0:width, :]) * r, _dot_x2(imp, band_ref[0:width, :])

    n16 = kc_ref.shape[2]
    widths = list(range(CMP_WIDTH_STEP, n16, CMP_WIDTH_STEP)) + [n16]
    o_c, pslc = lax.switch((s0 + tq - 1) // (CMP_WIDTH_STEP * CMP_STRIDE),
                           [functools.partial(compressed, wd) for wd in widths])

    wlen = WINDOW + tq

    def window(kvw, masked):
        sw = masked(_dot_nt(q, kvw[:, 0:HEAD_DIM]))
        pw = jnp.exp(sw - jnp.max(sw, axis=-1, keepdims=True))
        return _dot(pw.astype(BF16), kvw)[:, HEAD_DIM:2 * HEAD_DIM] * (1.0 / jnp.sum(pw, axis=-1, keepdims=True))

    def window_interior():
        def masked(sw):
            rq = lax.broadcasted_iota(jnp.int32, (sw.shape[0], tq), 0) & (tq - 1)
            c = lax.broadcasted_iota(jnp.int32, (sw.shape[0], tq), 1)
            return jnp.concatenate([jnp.where(c > rq, sw[:, 0:tq], NEG), sw[:, tq:WINDOW],
                                    jnp.where(c <= rq, sw[:, WINDOW:wlen], NEG)], axis=1)

        return window(kvw_ref[0, 0, pl.ds(pl.multiple_of(s0 - WINDOW, tq), wlen), :], masked)

    def window_start():
        def masked(sw):
            kabs = lax.broadcasted_iota(jnp.int32, sw.shape, 1)
            t = s0 + (lax.broadcasted_iota(jnp.int32, sw.shape, 0) & (tq - 1))
            return jnp.where((kabs <= t) & (kabs > t - WINDOW), sw, NEG)

        return window(kvw_ref[0, 0, 0:wlen, :], masked)

    pslc_t = pslc.T
    nb = pslc_t.shape[0]
    blk = lax.broadcasted_iota(jnp.int32, pslc_t.shape, 0)
    cur = (s0 + lax.broadcasted_iota(jnp.int32, pslc_t.shape, 1)) // SEL_BLOCK
    c0 = s0 // SEL_BLOCK
    elig = (blk >= 1) & (blk <= cur - 2)
    sel = _pick_top(jnp.where(elig, pslc_t, -jnp.inf), blk, nb, SEL_TOPK - 3)
    sel = sel | (blk == 0) | (blk == cur - 1)
    past = jnp.where(sel & (blk < c0), 0.0, NEG).T.astype(BF16)
    for st in range(n_super):
        b = past[:, st * SEL_LANES:(st + 1) * SEL_LANES]
        qa_ref[st] = jnp.concatenate([q, jnp.concatenate([b] * NSA_GROUP, axis=0)], axis=1)

    _softmax_init(m_ref, acc_ref)
    d0 = pl.multiple_of(s0, tq)
    s = _dot_nt(q, ka_ref[0, 0, pl.ds(d0, tq), :][:, 0:HEAD_DIM])
    qpos = lax.broadcasted_iota(jnp.int32, s.shape, 0) & (tq - 1)
    kpos = lax.broadcasted_iota(jnp.int32, s.shape, 1)
    _softmax_step(jnp.where(kpos <= qpos, s, NEG), va_ref[0, 0, pl.ds(d0, tq), :], m_ref, acc_ref)

    def step(start, size):
        s = _dot_nt(qa_ref[start // SUPER_KEYS], ka_ref[0, 0, pl.ds(start, size), :])
        _softmax_step(s, va_ref[0, 0, pl.ds(start, size), :], m_ref, acc_ref)

    (o_w,) = _past_keys_loop(s0, NSA_TK, step, parts=4, riders=(
        (window_interior, lambda: lax.cond(s0 >= WINDOW, window_interior, window_start)),))
    o_s = _softmax_result(acc_ref)

    w = NSA_GROUP * HEAD_DIM
    gexp = _dot_x2(gt_ref[...], e_ref[0])
    wide = lambda x: jnp.concatenate([x[g * tq:(g + 1) * tq] for g in range(NSA_GROUP)], axis=1)
    o_ref[...] = (gexp[:, 0:w] * wide(o_c) + gexp[:, w:2 * w] * wide(o_s) + gexp[:, 2 * w:3 * w] * wide(o_w)).astype(BF16)


def _nsa(qs, kcmp, vcmp, ka, va, kvw, gates, batch, seq):
    tq = NSA_TQ
    nq = seq // tq
    nb = seq // SEL_BLOCK
    n16 = seq // CMP_STRIDE
    assert seq % SUPER_KEYS == 0 and tq == 2 * SEL_BLOCK and WINDOW % tq == 0 and NSA_TK >= WINDOW
    n_super = seq // SUPER_KEYS
    rows = NSA_GROUP * tq
    e = np.zeros((NSA_KV_HEADS, GATE_LANES, 3 * NSA_GROUP * HEAD_DIM), np.float32)
    for br in range(3):
        for hk in range(NSA_KV_HEADS):
            for g in range(NSA_GROUP):
                c = (br * NSA_GROUP + g) * HEAD_DIM
                e[hk, br * NSA_HEADS + hk * NSA_GROUP + g, c:c + HEAD_DIM] = 1.0
    nn, jj = np.arange(n16)[:, None], np.arange(nb)[None, :]
    band = ((nn >= 4 * jj - 1) & (nn <= 4 * jj + 3)).astype(np.float32)
    resident = lambda width: pl.BlockSpec((1, 1, seq, width), lambda b, h, i: (b, h, 0, 0))
    w = NSA_GROUP * HEAD_DIM
    return pl.pallas_call(
        functools.partial(_nsa_kernel, n_super),
        grid=(batch, NSA_KV_HEADS, nq),
        in_specs=[pl.BlockSpec((1, 1, 1, rows, HEAD_DIM), lambda b, h, i: (b, h, i, 0, 0)),
                  pl.BlockSpec((1, 1, n16, HEAD_DIM), lambda b, h, i: (b, h, 0, 0)),
                  pl.BlockSpec((1, 1, n16, HEAD_DIM), lambda b, h, i: (b, h, 0, 0)),
                  resident(LANES), resident(LANES), resident(LANES),
                  pl.BlockSpec((tq, GATE_LANES), lambda b, h, i: (b * nq + i, 0)),
                  pl.BlockSpec((1, GATE_LANES, 3 * w), lambda b, h, i: (h, 0, 0)),
                  pl.BlockSpec((n16, nb), lambda b, h, i: (0, 0))],
        out_specs=pl.BlockSpec((tq, w), lambda b, h, i: (b * nq + i, h)),
        out_shape=jax.ShapeDtypeStruct((batch * seq, NSA_W), BF16),
        scratch_shapes=[pltpu.VMEM((n_super, rows, LANES), BF16), pltpu.VMEM((rows, LANES), F32),
                        pltpu.VMEM((rows, LANES), F32)],
        compiler_params=_params("arbitrary", "arbitrary", "arbitrary"),
        name="nsa",
    )(qs, kcmp, vcmp, ka, va, kvw, gates, jnp.asarray(e, BF16), jnp.asarray(band, BF16))


def _odd_out_kernel(om_ref, on_ref, h_ref, w_ref, o_ref):
    acc = h_ref[...] + _dot(on_ref[...], w_ref[MOBA_W:D_MODEL, :])
    for h in range(MOBA_HEADS):
        acc = acc + _dot(om_ref[0, h], w_ref[h * HEAD_DIM:(h + 1) * HEAD_DIM, :])
    o_ref[...] = acc


def _odd_out(o_moba, o_nsa, h2, w_out, seq, tm=ROW_TILE):
    t = h2.shape[0]
    tps = seq // tm
    row = lambda w: pl.BlockSpec((tm, w), lambda i: (i, 0))
    return pl.pallas_call(
        _odd_out_kernel,
        grid=(t // tm,),
        in_specs=[pl.BlockSpec((1, MOBA_HEADS, tm, HEAD_DIM), lambda i: (i // tps, 0, i % tps, 0)),
                  row(NSA_W), row(D_MODEL), _full((D_MODEL, D_MODEL))],
        out_specs=row(D_MODEL),
        out_shape=jax.ShapeDtypeStruct((t, D_MODEL), F32),
        compiler_params=_params("arbitrary"),
        name="odd_out",
    )(o_moba, o_nsa, h2, w_out)


def _even_layer(h2, batch, seq, norm, w_in, w_out, lam_re, lam_im, log_dt, b_re, b_im, c_re, c_im, d, w_glu, conv_w, conv_b):
    u, ub, yb = _even_in(h2, norm.reshape(1, D_MODEL), w_in.astype(BF16), conv_w, conv_b.reshape(1, CONV_WIDTH), seq)
    ypre = _s5_mixer_pre(ub, batch, seq, lam_re, lam_im, log_dt, b_re, b_im, c_re, c_im)
    return _even_out(ypre, u, yb, h2, d.reshape(1, S5_WIDTH), w_glu.astype(BF16), w_out.astype(BF16))


def _odd_layer(h2, batch, seq, norm, w_in, w_out, moba_q_norm, moba_k_norm, nsa_q_norm, nsa_kcmp_norm, nsa_ksel_norm,
               nsa_kwin_norm, cmp_pe_k, cmp_w1_k, cmp_w2_k, cmp_pe_v, cmp_w1_v, cmp_w2_v):
    qm, kam, kmean, vam, qs, kvc, kas, vas, kvw, gates = _odd_in(
        h2, batch, seq, norm.reshape(1, D_MODEL), w_in, moba_q_norm, moba_k_norm, nsa_q_norm, nsa_ksel_norm, nsa_kwin_norm)
    cmp = _compress(kvc, batch, seq, cmp_pe_k, cmp_w1_k, cmp_w2_k, cmp_pe_v, cmp_w1_v, cmp_w2_v, nsa_kcmp_norm)
    o_moba = _moba(qm, kam, kmean, vam, batch, seq)
    o_nsa = _nsa(qs, cmp[0], cmp[1], kas, vas, kvw, gates, batch, seq)
    return _odd_out(o_moba, o_nsa, h2, w_out.astype(BF16), seq)


def kernel(x, ev_norm_mix, ev_w_in, ev_w_out, s5_lam_re, s5_lam_im, s5_log_dt, s5_b_re, s5_b_im, s5_c_re, s5_c_im, s5_d, s5_w_glu, conv_w, conv_b, od_norm_mix, od_w_in, od_w_out, moba_q_norm, moba_k_norm, nsa_q_norm, nsa_kcmp_norm, nsa_ksel_norm, nsa_kwin_norm, cmp_pe_k, cmp_w1_k, cmp_w2_k, cmp_pe_v, cmp_w1_v, cmp_w2_v, moe_norm, moe_w_group, moe_b_group, moe_w_expert, moe_b_expert, moe_w_gate, moe_w_up, moe_w_down):
    batch, seq, _ = x.shape
    depth = moe_norm.shape[0]
    h = x.reshape(batch * seq, D_MODEL)
    for layer in range(depth):
        i = layer // 2
        if layer % 2 == 0:
            h = _even_layer(h, batch, seq, ev_norm_mix[i], ev_w_in[i], ev_w_out[i], s5_lam_re[i], s5_lam_im[i], s5_log_dt[i],
                            s5_b_re[i], s5_b_im[i], s5_c_re[i], s5_c_im[i], s5_d[i], s5_w_glu[i], conv_w[i], conv_b[i])
        else:
            h = _odd_layer(h, batch, seq, od_norm_mix[i], od_w_in[i], od_w_out[i], moba_q_norm[i], moba_k_norm[i],
                           nsa_q_norm[i], nsa_kcmp_norm[i], nsa_ksel_norm[i], nsa_kwin_norm[i], cmp_pe_k[i], cmp_w1_k[i],
                           cmp_w2_k[i], cmp_pe_v[i], cmp_w1_v[i], cmp_w2_v[i])
        h = _moe(h, moe_norm[layer].reshape(1, D_MODEL), moe_w_group[layer], moe_b_group[layer], moe_w_expert[layer],
                 moe_b_expert[layer], moe_w_gate[layer], moe_w_up[layer], moe_w_down[layer])
    return h.reshape(batch, seq, D_MODEL)
```

```python
import functools
import math

import jax
import jax.numpy as jnp
import numpy as np
from jax import lax
from jax.experimental import pallas as pl
from jax.experimental.pallas import tpu as pltpu

D_MODEL = 1024
HEAD_DIM = 64
EPS = 1e-6
S5_WIDTH = 256
S5_GROUP = 16
S5_GROUPS = 16
S5_STATE = 64
S5_CHUNK = 16
S5_GROUP_BITS = 4
S5_STATE_BITS = 6
S5_ROW_TILE = 512
S5_COL_TILE = 512
S5_SCAN_TILE = 256
CONV_WIDTH = 768
CONV_K = 3
MOBA_HEADS = 4
NSA_HEADS = 12
NSA_KV_HEADS = 2
NSA_GROUP = 6
MOBA_W = 256
NSA_W = 768
KV_W = 128
MOBA_BLOCK = 256
MOBA_TOPK = 3
CMP_BLOCK = 32
CMP_STRIDE = 16
CMP_HIDDEN = 256
SEL_BLOCK = 64
SEL_TOPK = 8
WINDOW = 512
N_GROUPS = 4
EXPERTS_PER_GROUP = 4
N_EXPERTS = 16
EXPERT_FF = 256

LANES = 128
SUBLANES = 8
VMEM_LIMIT_BYTES = 56 * 1024 * 1024
ROW_TILE = 1024
GATE_LANES = LANES
NEG = -float(2 ** 100)
Q_SCALE = HEAD_DIM ** -0.5 * math.log2(math.e)
F32 = jnp.float32
BF16 = jnp.bfloat16


def _params(*semantics):
    return pltpu.CompilerParams(dimension_semantics=semantics, vmem_limit_bytes=VMEM_LIMIT_BYTES)


def _dot(a, b):
    return jnp.dot(a, b, preferred_element_type=F32)


def _dot_nt(a, b):
    return lax.dot_general(a, b, (((1,), (1,)), ((), ())), preferred_element_type=F32)


def _split(x):
    hi = x.astype(BF16)
    lo = (x - hi.astype(F32)).astype(BF16)
    return hi, lo


def _dot_x2(x, w):
    hi, lo = _split(x)
    return _dot(hi, w) + _dot(lo, w)


def _dot_x3(x, w_hilo):
    n = w_hilo.shape[1] // 2
    hi, lo = _split(x)
    both = _dot(hi, w_hilo)
    return both[:, 0:n] + (both[:, n:2 * n] + _dot(lo, w_hilo[:, 0:n]))


def _rms(x, gain):
    return x * lax.rsqrt(jnp.mean(x * x, axis=-1, keepdims=True) + EPS) * gain


def _gelu(x):
    return 0.5 * x * (1.0 + jnp.tanh(math.sqrt(2.0 / math.pi) * (x + 0.044715 * (x * x * x))))


def _sigmoid(x):
    return 1.0 / (1.0 + jnp.exp(-x))


def _full(shape):
    n = len(shape)
    return pl.BlockSpec(shape, lambda *_: (0,) * n)


def _even_in_kernel(tiles_per_seq, x_ref, g_ref, w_ref, cw_ref, cb_ref, u_ref, ub_ref, yb_ref, carry_ref):
    i = pl.program_id(0)
    xn = _rms(x_ref[...], g_ref[...]).astype(BF16)
    u = _dot(xn, w_ref[:, 0:S5_WIDTH])
    u_ref[...] = u
    ub_ref[...] = u.astype(BF16)
    o = S5_WIDTH
    xc = _dot(xn, w_ref[:, o:o + CONV_WIDTH])
    gb = _dot(xn, w_ref[:, o + CONV_WIDTH:o + 2 * CONV_WIDTH])
    gc = _dot(xn, w_ref[:, o + 2 * CONV_WIDTH:o + 3 * CONV_WIDTH])
    z = gc * xc
    tm = z.shape[0]

    @pl.when(i % tiles_per_seq == 0)
    def _():
        carry_ref[...] = jnp.zeros_like(carry_ref)

    row = lax.broadcasted_iota(jnp.int32, z.shape, 0)
    prev1 = carry_ref[SUBLANES - 1:SUBLANES, :]
    prev2 = carry_ref[SUBLANES - 2:SUBLANES - 1, :]
    z1 = jnp.where(row == 0, prev1, pltpu.roll(z, 1, 0))
    z2 = jnp.where(row == 0, prev2, jnp.where(row == 1, prev1, pltpu.roll(z, 2, 0)))
    y = cw_ref[0:1, :] * z2 + cw_ref[1:2, :] * z1 + cw_ref[2:3, :] * z + cb_ref[...]
    yb_ref[...] = (gb * y).astype(BF16)
    carry_ref[...] = z[tm - SUBLANES:tm, :]


def _even_in(x2, gain, w_in, conv_w, conv_b, seq, tm=ROW_TILE):
    t = x2.shape[0]
    n_in = w_in.shape[1]
    return pl.pallas_call(
        functools.partial(_even_in_kernel, seq // tm),
        grid=(t // tm,),
        in_specs=[pl.BlockSpec((tm, D_MODEL), lambda i: (i, 0)), _full((1, D_MODEL)),
                  _full((D_MODEL, n_in)), _full((CONV_K, CONV_WIDTH)), _full((1, CONV_WIDTH))],
        out_specs=[pl.BlockSpec((tm, S5_WIDTH), lambda i: (i, 0)), pl.BlockSpec((tm, S5_WIDTH), lambda i: (i, 0)),
                   pl.BlockSpec((tm, CONV_WIDTH), lambda i: (i, 0))],
        out_shape=[jax.ShapeDtypeStruct((t, S5_WIDTH), F32), jax.ShapeDtypeStruct((t, S5_WIDTH), BF16),
                   jax.ShapeDtypeStruct((t, CONV_WIDTH), BF16)],
        scratch_shapes=[pltpu.VMEM((SUBLANES, CONV_WIDTH), F32)],
        compiler_params=_params("arbitrary"),
        name="even_in",
    )(x2, gain, w_in, conv_w, conv_b)


def _s5_weights(lam_re, lam_im, log_dt, b_re, b_im, c_re, c_im):
    g, p, hg, ck = S5_GROUPS, S5_STATE, S5_GROUP, S5_CHUNK
    lr, li = lam_re.astype(F32), lam_im.astype(F32)
    dt = jnp.exp(log_dt.astype(F32))[:, None]
    mag = jnp.exp(lr * dt)
    a_re, a_im = mag * jnp.cos(li * dt), mag * jnp.sin(li * dt)
    den = lr * lr + li * li
    f_re = ((a_re - 1.0) * lr + a_im * li) / den
    f_im = (a_im * lr - (a_re - 1.0) * li) / den
    br, bi = b_re.astype(F32), b_im.astype(F32)
    bb_re = f_re[..., None] * br - f_im[..., None] * bi
    bb_im = f_re[..., None] * bi + f_im[..., None] * br
    pw_re, pw_im = [jnp.ones_like(a_re)], [jnp.zeros_like(a_im)]
    for _ in range(ck):
        r, m = pw_re[-1], pw_im[-1]
        pw_re.append(r * a_re - m * a_im)
        pw_im.append(r * a_im + m * a_re)
    pw_re, pw_im = jnp.stack(pw_re), jnp.stack(pw_im)
    cr, ci = c_re.astype(F32), c_im.astype(F32)
    rev_re, rev_im = pw_re[ck - 1::-1][:ck], pw_im[ck - 1::-1][:ck]
    ws_re = rev_re[:, :, :, None] * bb_re[None] - rev_im[:, :, :, None] * bb_im[None]
    ws_im = rev_re[:, :, :, None] * bb_im[None] + rev_im[:, :, :, None] * bb_re[None]
    ca_re = cr[None] * pw_re[1:, :, None, :] - ci[None] * pw_im[1:, :, None, :]
    ca_im = cr[None] * pw_im[1:, :, None, :] + ci[None] * pw_re[1:, :, None, :]
    cb_re = jnp.einsum('ghp,kgp,gpj->kghj', cr, pw_re[:ck], bb_re) - jnp.einsum('ghp,kgp,gpj->kghj', cr, pw_im[:ck], bb_im) \
        - jnp.einsum('ghp,kgp,gpj->kghj', ci, pw_re[:ck], bb_im) - jnp.einsum('ghp,kgp,gpj->kghj', ci, pw_im[:ck], bb_re)
    lag = np.arange(ck)[None, :] - np.arange(ck)[:, None]
    tz = cb_re[np.clip(lag, 0, ck - 1)]
    tz = jnp.where((lag >= 0)[:, :, None, None, None], tz, 0.0)
    cw = ck * g * hg
    ws = jnp.stack([ws_re, ws_im]).transpose(1, 2, 4, 0, 3).reshape(cw, 2 * p)
    wc = jnp.stack([ca_re, -ca_im]).transpose(0, 2, 4, 1, 3).reshape(2 * g * p, ck * hg)
    tzc = tz.transpose(0, 2, 4, 1, 3).reshape(cw, ck * hg)
    return ws.astype(BF16), wc.astype(BF16), tzc.astype(BF16), pw_re[ck].reshape(1, g * p), pw_im[ck].reshape(1, g * p)


def _group_expand(compact, expand, row_shift, col_shift, col0):
    full = _dot(compact, expand)
    row = lax.broadcasted_iota(jnp.int32, full.shape, 0)
    col = col0 + lax.broadcasted_iota(jnp.int32, full.shape, 1)
    same = ((row >> row_shift) & (S5_GROUPS - 1)) == ((col >> col_shift) & (S5_GROUPS - 1))
    return jnp.where(same, full, 0.0).astype(BF16)


def _s5_state_kernel(u_ref, ws_ref, e_ref, s_ref, w_scr):
    @pl.when(pl.program_id(1) == 0)
    def _():
        w_scr[...] = _group_expand(ws_ref[...], e_ref[...], S5_GROUP_BITS, S5_STATE_BITS,
                                   pl.program_id(0) * w_scr.shape[1])

    s_ref[...] = _dot(u_ref[...], w_scr[...])


def _s5_scan_kernel(s_ref, are_ref, aim_ref, xprev_ref, st_ref):
    @pl.when(pl.program_id(0) == 0)
    def _():
        st_ref[...] = jnp.zeros_like(st_ref)

    a_re, a_im = are_ref[...], aim_ref[...]
    nb, n = s_ref.shape[0], s_ref.shape[1]
    half = a_re.shape[1]

    def body(c, carry):
        out = []
        for b in range(nb):
            xr, xi = carry[2 * b], carry[2 * b + 1]
            xprev_ref[b, pl.ds(c, 1), 0:half] = xr
            xprev_ref[b, pl.ds(c, 1), half:2 * half] = xi
            s = s_ref[b, pl.ds(c, 1), :]
            out += [a_re * xr - a_im * xi + s[:, 0:half], a_re * xi + a_im * xr + s[:, half:2 * half]]
        return tuple(out)

    init = tuple(st_ref[b:b + 1, o:o + half] for b in range(nb) for o in (0, half))
    final = lax.fori_loop(0, n, body, init, unroll=4)
    for b in range(nb):
        st_ref[b:b + 1, 0:half] = final[2 * b]
        st_ref[b:b + 1, half:2 * half] = final[2 * b + 1]


def _s5_out_kernel(u_ref, xp_ref, tz_ref, wc_ref, e_ref, y_ref, tz_scr, wc_scr):
    @pl.when(pl.program_id(1) == 0)
    def _():
        col0 = pl.program_id(0) * tz_scr.shape[1]
        tz_scr[...] = _group_expand(tz_ref[...], e_ref[...], S5_GROUP_BITS, S5_GROUP_BITS, col0)
        wc_scr[...] = _group_expand(wc_ref[...], e_ref[...], S5_STATE_BITS, S5_GROUP_BITS, col0)

    y_ref[...] = _dot(u_ref[...], tz_scr[...]) + _dot(xp_ref[...].astype(BF16), wc_scr[...])


def _expand_matrix(outer, inner):
    e = np.zeros((outer, inner, outer, S5_GROUPS, inner), np.float32)
    for x in range(outer):
        for y in range(inner):
            e[x, y, x, :, y] = 1.0
    return jnp.asarray(e.reshape(outer * inner, outer * S5_GROUPS * inner), BF16)


def _s5_mixer_pre(ub, batch, seq, lam_re, lam_im, log_dt, b_re, b_im, c_re, c_im):
    ws, wc, tz, a16_re, a16_im = _s5_weights(lam_re, lam_im, log_dt, b_re, b_im, c_re, c_im)
    nc = seq // S5_CHUNK
    rows = batch * nc
    cw = S5_CHUNK * S5_WIDTH
    sw = 2 * S5_GROUPS * S5_STATE
    assert S5_GROUP == 1 << S5_GROUP_BITS and S5_STATE == 1 << S5_STATE_BITS
    tr = min(rows, S5_ROW_TILE)
    tn = S5_COL_TILE
    uc = ub.reshape(rows, cw)
    e_state = _expand_matrix(2, S5_STATE)
    e_out = _expand_matrix(S5_CHUNK, S5_GROUP)
    s = pl.pallas_call(
        _s5_state_kernel,
        grid=(sw // tn, rows // tr),
        in_specs=[pl.BlockSpec((tr, cw), lambda j, i: (i, 0)), _full(ws.shape),
                  pl.BlockSpec((e_state.shape[0], tn), lambda j, i: (0, j))],
        out_specs=pl.BlockSpec((tr, tn), lambda j, i: (i, j)),
        out_shape=jax.ShapeDtypeStruct((rows, sw), F32),
        scratch_shapes=[pltpu.VMEM((cw, tn), BF16)],
        compiler_params=_params("arbitrary", "arbitrary"),
        name="s5_state",
    )(uc, ws, e_state)
    tc = min(nc, S5_SCAN_TILE)
    xprev = pl.pallas_call(
        _s5_scan_kernel,
        grid=(nc // tc,),
        in_specs=[pl.BlockSpec((batch, tc, sw), lambda i: (0, i, 0)), _full((1, sw // 2)), _full((1, sw // 2))],
        out_specs=pl.BlockSpec((batch, tc, sw), lambda i: (0, i, 0)),
        out_shape=jax.ShapeDtypeStruct((batch, nc, sw), F32),
        scratch_shapes=[pltpu.VMEM((batch, sw), F32)],
        compiler_params=_params("arbitrary"),
        name="s5_scan",
    )(s.reshape(batch, nc, sw), a16_re, a16_im)
    y = pl.pallas_call(
        _s5_out_kernel,
        grid=(cw // tn, rows // tr),
        in_specs=[pl.BlockSpec((tr, cw), lambda j, i: (i, 0)), pl.BlockSpec((tr, sw), lambda j, i: (i, 0)),
                  _full(tz.shape), _full(wc.shape), pl.BlockSpec((e_out.shape[0], tn), lambda j, i: (0, j))],
        out_specs=pl.BlockSpec((tr, tn), lambda j, i: (i, j)),
        out_shape=jax.ShapeDtypeStruct((rows, cw), F32),
        scratch_shapes=[pltpu.VMEM((cw, tn), BF16), pltpu.VMEM((sw, tn), BF16)],
        compiler_params=_params("arbitrary", "arbitrary"),
        name="s5_out",
    )(uc, xprev.reshape(rows, sw), tz, wc, e_out)
    return y.reshape(batch * seq, S5_WIDTH)


def _even_out_kernel(ypre_ref, u_ref, yb_ref, x_ref, d_ref, wglu_ref, wout_ref, o_ref):
    y = _gelu(ypre_ref[...] + d_ref[...] * u_ref[...])
    y = y * _sigmoid(_dot(y.astype(BF16), wglu_ref[...]))
    o_ref[...] = (x_ref[...] + _dot(y.astype(BF16), wout_ref[0:S5_WIDTH, :])
                  + _dot(yb_ref[...], wout_ref[S5_WIDTH:D_MODEL, :]))


def _even_out(ypre, u, yb, x2, d, w_glu, w_out, tm=ROW_TILE):
    t = x2.shape[0]
    row = lambda w: pl.BlockSpec((tm, w), lambda i: (i, 0))
    return pl.pallas_call(
        _even_out_kernel,
        grid=(t // tm,),
        in_specs=[row(S5_WIDTH), row(S5_WIDTH), row(CONV_WIDTH), row(D_MODEL), _full((1, S5_WIDTH)),
                  _full((S5_WIDTH, S5_WIDTH)), _full((D_MODEL, D_MODEL))],
        out_specs=row(D_MODEL),
        out_shape=jax.ShapeDtypeStruct((t, D_MODEL), F32),
        compiler_params=_params("arbitrary"),
        name="even_out",
    )(ypre, u, yb, x2, d, w_glu, w_out)


def _first_max(v, pos, width, axis=-1):
    m = jnp.max(v, axis=axis, keepdims=True)
    idx = jnp.min(jnp.where(v == m, pos, width), axis=axis, keepdims=True)
    return m, idx


MOE_TM = 1024
MOE_ALIGN = 16
MOE_SLOTS = 1152
MOE_WIN = 320
ROUTER_LANES = LANES


def _moe_router_kernel(h_ref, g_ref, wr_ref, br_ref, gate_ref, grp_ref, cnt_ref):
    xn = _rms(h_ref[...], g_ref[...])
    logits = _dot_x3(xn, wr_ref[...]) + br_ref[...]
    lane = lax.broadcasted_iota(jnp.int32, logits.shape, 1).astype(F32)
    width = float(logits.shape[1])
    is_g = lane < N_GROUPS
    gl = jnp.where(is_g, logits, -jnp.inf)
    gm, gi = _first_max(gl, lane, width)
    gw = 1.0 / jnp.sum(jnp.where(is_g, jnp.exp(gl - gm), 0.0), axis=-1, keepdims=True)
    lo = N_GROUPS + gi * EXPERTS_PER_GROUP
    in_grp = (lane >= lo) & (lane < lo + EXPERTS_PER_GROUP)
    el = jnp.where(in_grp, logits, -jnp.inf)
    m1, i1 = _first_max(el, lane, width)
    m2, i2 = _first_max(jnp.where(lane == i1, -jnp.inf, el), lane, width)
    p2 = jnp.exp(m2 - m1)
    w1 = gw / (1.0 + p2)
    w2 = gw * p2 / (1.0 + p2)
    gate_ref[...] = jnp.where(lane == i1, w1, 0.0) + jnp.where(lane == i2, w2, 0.0)
    grp = (lane == gi).astype(F32)
    grp_ref[...] = grp.astype(BF16)
    cnt_ref[0] = jnp.broadcast_to(jnp.sum(grp, axis=0, keepdims=True), cnt_ref.shape[1:])


def _moe_expert_kernel(base_ref, nwin_ref, h_ref, g_ref, gate_ref, grp_ref, wg_ref, wu_ref, wd_ref, o_ref,
                       xs_ref, gs_ref, ys_ref, pt_ref):
    i, g = pl.program_id(0), pl.program_id(1)
    tm = h_ref.shape[0]

    @pl.when((i == 0) & (g == 0))
    def _():
        xs_ref[...] = jnp.zeros_like(xs_ref)
        gs_ref[...] = jnp.zeros_like(gs_ref)
        ys_ref[...] = jnp.zeros_like(ys_ref)

    @pl.when(g == 0)
    def _():
        xn = _rms(h_ref[...], g_ref[...]).astype(BF16)
        grp = grp_ref[...]
        earlier = (lax.broadcasted_iota(jnp.int32, (tm, tm), 0) > lax.broadcasted_iota(jnp.int32, (tm, tm), 1)).astype(BF16)
        rank = _dot(earlier, grp)
        lane = lax.broadcasted_iota(jnp.int32, rank.shape, 1)
        for k in range(N_GROUPS):
            rank = rank + jnp.where(lane == k, base_ref[i * N_GROUPS + k].astype(F32), 0.0)
        slot = jnp.sum(grp.astype(F32) * rank, axis=-1, keepdims=True).astype(jnp.int32)
        pt = (lax.broadcasted_iota(jnp.int32, (tm, MOE_SLOTS), 1) == slot).astype(BF16)
        pt_ref[...] = pt
        gather = lambda x: lax.dot_general(pt, x, (((0,), (0,)), ((), ())), preferred_element_type=F32)
        xs_ref[0:MOE_SLOTS, :] = gather(xn).astype(BF16)
        both = gather(jnp.concatenate(_split(gate_ref[...]), axis=1))
        gs_ref[0:MOE_SLOTS, :] = both[:, 0:ROUTER_LANES] + both[:, ROUTER_LANES:2 * ROUTER_LANES]

    def window(w, carry):
        r0 = pl.multiple_of(base_ref[i * N_GROUPS + g] + w * MOE_WIN, MOE_ALIGN)
        x = xs_ref[pl.ds(r0, MOE_WIN), :]
        gate = gs_ref[pl.ds(r0, MOE_WIN), :]
        lane = lax.broadcasted_iota(jnp.int32, gate.shape, 1)
        y = None
        for j in range(EXPERTS_PER_GROUP):
            ge = jnp.sum(jnp.where(lane == g * EXPERTS_PER_GROUP + (j + N_GROUPS), gate, 0.0), axis=-1, keepdims=True)
            h1 = _dot(x, wg_ref[j])
            h3 = _dot(x, wu_ref[j])
            act = (h1 * _sigmoid(h1)) * h3 * ge
            yj = _dot(act.astype(BF16), wd_ref[j])
            y = yj if y is None else y + yj
        ys_ref[pl.ds(r0, MOE_WIN), :] = y.astype(BF16)
        return carry

    lax.fori_loop(0, nwin_ref[i * N_GROUPS + g], window, 0)

    @pl.when(g == N_GROUPS - 1)
    def _():
        o_ref[...] = h_ref[...] + _dot(pt_ref[...], ys_ref[0:MOE_SLOTS, :])


def _moe(h2, gain, w_group, b_group, w_expert, b_expert, w_gate, w_up, w_down):
    t = h2.shape[0]
    tm, rw = MOE_TM, ROUTER_LANES
    assert t % tm == 0 and MOE_SLOTS >= tm + N_GROUPS * (MOE_ALIGN - 1) and MOE_WIN % MOE_ALIGN == 0
    n_tiles = t // tm
    wr = jnp.zeros((D_MODEL, rw), F32).at[:, 0:N_GROUPS].set(w_group).at[:, N_GROUPS:N_GROUPS + N_EXPERTS].set(w_expert)
    br = jnp.zeros((1, rw), F32).at[0, 0:N_GROUPS].set(b_group).at[0, N_GROUPS:N_GROUPS + N_EXPERTS].set(b_expert)
    wr_hilo = jnp.concatenate(_split(wr), axis=1)
    row = lambda width: pl.BlockSpec((tm, width), lambda i: (i, 0))
    gates, grp, cnt = pl.pallas_call(
        _moe_router_kernel,
        grid=(n_tiles,),
        in_specs=[row(D_MODEL), _full((1, D_MODEL)), _full((D_MODEL, 2 * rw)), _full((1, rw))],
        out_specs=[row(rw), row(rw), pl.BlockSpec((1, SUBLANES, rw), lambda i: (i, 0, 0))],
        out_shape=[jax.ShapeDtypeStruct((t, rw), F32), jax.ShapeDtypeStruct((t, rw), BF16),
                   jax.ShapeDtypeStruct((n_tiles, SUBLANES, rw), F32)],
        compiler_params=_params("arbitrary"),
        name="moe_router",
    )(h2, gain, wr_hilo, br)
    n = cnt[:, 0, 0:N_GROUPS].astype(jnp.int32)
    padded = (n + (MOE_ALIGN - 1)) // MOE_ALIGN * MOE_ALIGN
    base = (jnp.cumsum(padded, axis=1) - padded).reshape(-1)
    nwin = ((padded + (MOE_WIN - 1)) // MOE_WIN).reshape(-1)
    tile = lambda width: pl.BlockSpec((tm, width), lambda i, g, *_: (i, 0))
    experts = lambda shape: pl.BlockSpec((EXPERTS_PER_GROUP,) + shape, lambda i, g, *_: (g, 0, 0))
    slots = MOE_SLOTS + MOE_WIN
    return pl.pallas_call(
        _moe_expert_kernel,
        grid_spec=pltpu.PrefetchScalarGridSpec(
            num_scalar_prefetch=2,
            grid=(n_tiles, N_GROUPS),
            in_specs=[tile(D_MODEL), pl.BlockSpec((1, D_MODEL), lambda i, g, *_: (0, 0)), tile(rw), tile(rw),
                      experts((D_MODEL, EXPERT_FF)), experts((D_MODEL, EXPERT_FF)), experts((EXPERT_FF, D_MODEL))],
            out_specs=tile(D_MODEL),
            scratch_shapes=[pltpu.VMEM((slots, D_MODEL), BF16), pltpu.VMEM((slots, rw), F32),
                            pltpu.VMEM((slots, D_MODEL), BF16), pltpu.VMEM((tm, MOE_SLOTS), BF16)]),
        out_shape=jax.ShapeDtypeStruct((t, D_MODEL), F32),
        compiler_params=_params("arbitrary", "arbitrary"),
        name="moe",
    )(base, nwin, h2, gain, gates, grp, w_gate.astype(BF16), w_up.astype(BF16), w_down.astype(BF16))


ODD_SPLITS = (MOBA_W, MOBA_W, MOBA_W, NSA_W, KV_W, KV_W, KV_W, KV_W, KV_W, KV_W, GATE_LANES)
ODD_IN_PAD = sum(ODD_SPLITS)


def _head_rms(x, hsum, gain):
    w = x.shape[1]
    ss = jnp.concatenate([_dot_x2(x[:, o:o + hsum.shape[0]] * x[:, o:o + hsum.shape[0]], hsum)
                          for o in range(0, w, hsum.shape[0])], axis=1) if w > hsum.shape[0] else _dot_x2(x * x, hsum)
    return x * lax.rsqrt(ss * (1.0 / HEAD_DIM) + EPS) * gain


def _odd_in_kernel(tiles_per_seq, x_ref, g_ref, w_ref, hsum_ref, gq_ref, gk_ref, gnq_ref, gks_ref, gkw_ref,
                   qm_ref, kam_ref, kmean_ref, vam_ref, qs_ref, kvc_ref, kas_ref, vas_ref, kvw_ref, gt_ref):
    xn = _rms(x_ref[...], g_ref[...]).astype(BF16)
    offs = np.cumsum((0,) + ODD_SPLITS)
    col = lambda j: _dot(xn, w_ref[:, int(offs[j]):int(offs[j + 1])])
    head = lambda x, h: x[:, h * HEAD_DIM:(h + 1) * HEAD_DIM]
    hsum = hsum_ref[...]
    hsum128 = hsum_ref[0:KV_W, 0:KV_W]
    tm = x_ref.shape[0]
    pos = (pl.program_id(0) % tiles_per_seq) * tm + lax.broadcasted_iota(jnp.int32, (tm, HEAD_DIM), 0)
    lane = lax.broadcasted_iota(jnp.int32, (tm, HEAD_DIM), 1)
    ones_col = (lane == 0).astype(BF16)

    qm = _head_rms(col(0), hsum, gq_ref[...])
    km = _head_rms(col(1), hsum, gk_ref[...])
    for j in range(tm // MOBA_BLOCK):
        kmean_ref[0, j:j + 1, :] = jnp.mean(km[j * MOBA_BLOCK:(j + 1) * MOBA_BLOCK, :], axis=0, keepdims=True)
    km = km.astype(BF16)
    vm = col(2).astype(BF16)
    moba_id = (lane == pos // MOBA_BLOCK).astype(BF16)
    for h in range(MOBA_HEADS):
        qm_ref[0, h] = head(qm, h)
        kam_ref[0, h] = jnp.concatenate([head(km, h), moba_id], axis=1)
        vam_ref[0, h] = jnp.concatenate([head(vm, h), ones_col], axis=1)

    qd = (_head_rms(col(3), hsum, gnq_ref[...]) * Q_SCALE).astype(BF16)
    for hk in range(NSA_KV_HEADS):
        for j in range(tm // NSA_TQ):
            for g in range(NSA_GROUP):
                qs_ref[0, hk, j, g * NSA_TQ:(g + 1) * NSA_TQ, :] = head(qd, hk * NSA_GROUP + g)[j * NSA_TQ:(j + 1) * NSA_TQ, :]
    kvc_ref[0] = col(4)
    kvc_ref[1] = col(5)
    ks = _head_rms(col(6), hsum128, gks_ref[...]).astype(BF16)
    vs = col(7).astype(BF16)
    kw = _head_rms(col(8), hsum128, gkw_ref[...]).astype(BF16)
    vw = col(9).astype(BF16)
    sel_id = (lane == (pos // SEL_BLOCK) % SEL_LANES).astype(BF16)
    for hk in range(NSA_KV_HEADS):
        kas_ref[0, hk] = jnp.concatenate([head(ks, hk), sel_id], axis=1)
        vas_ref[0, hk] = jnp.concatenate([head(vs, hk), ones_col], axis=1)
        kvw_ref[0, hk] = jnp.concatenate([head(kw, hk), head(vw, hk)], axis=1)
    gt_ref[...] = _sigmoid(col(10))


def _odd_in(h2, batch, seq, gain, w_in, moba_q_norm, moba_k_norm, nsa_q_norm, nsa_ksel_norm, nsa_kwin_norm, tm=ROW_TILE):
    t = h2.shape[0]
    assert MOBA_LANES == HEAD_DIM and SEL_LANES == HEAD_DIM and seq % tm == 0 and tm % MOBA_BLOCK == 0
    tps = seq // tm
    w = jnp.pad(w_in, ((0, 0), (0, ODD_IN_PAD - w_in.shape[1]))).astype(BF16)
    hsum = jnp.asarray(np.kron(np.eye(MOBA_W // HEAD_DIM), np.ones((HEAD_DIM, HEAD_DIM))), BF16)
    tile = lambda g, width: jnp.tile(g.astype(F32), width // HEAD_DIM).reshape(1, width)
    row = lambda width: pl.BlockSpec((tm, width), lambda i: (i, 0))
    heads = lambda n, width: pl.BlockSpec((1, n, tm, width), lambda i: (i // tps, 0, i % tps, 0))
    nmb = tm // MOBA_BLOCK
    nqt = tm // NSA_TQ
    rows = NSA_GROUP * NSA_TQ
    sds = jax.ShapeDtypeStruct
    out_specs = [heads(MOBA_HEADS, HEAD_DIM), heads(MOBA_HEADS, LANES), pl.BlockSpec((1, nmb, MOBA_W), lambda i: (i, 0, 0)),
                 heads(MOBA_HEADS, LANES),
                 pl.BlockSpec((1, NSA_KV_HEADS, nqt, rows, HEAD_DIM), lambda i: (i // tps, 0, i % tps, 0, 0)),
                 pl.BlockSpec((2, tm, KV_W), lambda i: (0, i, 0)),
                 heads(NSA_KV_HEADS, LANES), heads(NSA_KV_HEADS, LANES), heads(NSA_KV_HEADS, LANES),
                 row(GATE_LANES)]
    out_shape = [sds((batch, MOBA_HEADS, seq, HEAD_DIM), F32), sds((batch, MOBA_HEADS, seq, LANES), BF16),
                 sds((t // tm, nmb, MOBA_W), F32), sds((batch, MOBA_HEADS, seq, LANES), BF16),
                 sds((batch, NSA_KV_HEADS, seq // NSA_TQ, rows, HEAD_DIM), BF16),
                 sds((2, t, KV_W), F32), sds((batch, NSA_KV_HEADS, seq, LANES), BF16),
                 sds((batch, NSA_KV_HEADS, seq, LANES), BF16), sds((batch, NSA_KV_HEADS, seq, LANES), BF16),
                 sds((t, GATE_LANES), F32)]
    return pl.pallas_call(
        functools.partial(_odd_in_kernel, tps),
        grid=(t // tm,),
        in_specs=[row(D_MODEL), _full((1, D_MODEL)), _full((D_MODEL, ODD_IN_PAD)), _full((MOBA_W, MOBA_W)),
                  _full((1, MOBA_W)), _full((1, MOBA_W)), _full((1, NSA_W)), _full((1, KV_W)), _full((1, KV_W))],
        out_specs=out_specs,
        out_shape=out_shape,
        compiler_params=_params("arbitrary"),
        name="odd_in",
    )(h2, gain, w, hsum, tile(moba_q_norm, MOBA_W), tile(moba_k_norm, MOBA_W), tile(nsa_q_norm, NSA_W),
      tile(nsa_ksel_norm, KV_W), tile(nsa_kwin_norm, KV_W))


def _compress_kernel(c_ref, w1_ref, w2_ref, pe_ref, g_ref, o_ref):
    kind = pl.program_id(0)
    n16 = c_ref.shape[1] // CMP_STRIDE
    half = CMP_STRIDE * HEAD_DIM
    peb = _dot(pe_ref[0], w1_ref[0])[0:1, :]
    xs = [c_ref[0, pl.ds(s, n16, stride=CMP_STRIDE), :].astype(BF16) for s in range(CMP_STRIDE)]
    for h in range(NSA_KV_HEADS):
        first = second = None
        for s in range(CMP_STRIDE):
            x = xs[s][:, h * HEAD_DIM:(h + 1) * HEAD_DIM]
            a = _dot(x, w1_ref[0, s * HEAD_DIM:(s + 1) * HEAD_DIM, :])
            b = _dot(x, w1_ref[0, half + s * HEAD_DIM:half + (s + 1) * HEAD_DIM, :])
            first = a if first is None else first + a
            second = b if second is None else second + b
        hid = _gelu(first + pltpu.roll(second, n16 - 1, 0) + peb)
        out = _dot(hid.astype(BF16), w2_ref[0])
        o_ref[0, 0, h] = jnp.where(kind == 0, _rms(out, g_ref[...]), out).astype(BF16)


def _compress(kvc, batch, seq, pe_k, w1_k, w2_k, pe_v, w1_v, w2_v, kcmp_norm):
    n16 = seq // CMP_STRIDE
    half = CMP_STRIDE * HEAD_DIM
    w1 = jnp.stack([w1_k, w1_v]).astype(BF16)
    w2 = jnp.stack([w2_k, w2_v]).astype(BF16)
    pe = jnp.stack([pe_k, pe_v]).reshape(2, 1, 2 * half)
    pe = jnp.broadcast_to(pe, (2, SUBLANES, 2 * half)).astype(BF16)
    return pl.pallas_call(
        _compress_kernel,
        grid=(2, batch),
        in_specs=[pl.BlockSpec((1, seq, KV_W), lambda k, b: (k, b, 0)),
                  pl.BlockSpec((1, 2 * half, CMP_HIDDEN), lambda k, b: (k, 0, 0)),
                  pl.BlockSpec((1, CMP_HIDDEN, HEAD_DIM), lambda k, b: (k, 0, 0)),
                  pl.BlockSpec((1, SUBLANES, 2 * half), lambda k, b: (k, 0, 0)),
                  _full((1, HEAD_DIM))],
        out_specs=pl.BlockSpec((1, 1, NSA_KV_HEADS, n16, HEAD_DIM), lambda k, b: (k, b, 0, 0, 0)),
        out_shape=jax.ShapeDtypeStruct((2, batch, NSA_KV_HEADS, n16, HEAD_DIM), BF16),
        compiler_params=_params("arbitrary", "arbitrary"),
        name="nsa_compress",
    )(kvc, w1, w2, pe, kcmp_norm.astype(F32).reshape(1, HEAD_DIM))


M_INIT = -1e30


def _softmax_init(m_ref, acc_ref):
    m_ref[...] = jnp.full(m_ref.shape, M_INIT, F32)
    acc_ref[...] = jnp.zeros(acc_ref.shape, F32)


def _softmax_step(s, v_aug, m_ref, acc_ref):
    m_old = m_ref[...]
    m_new = jnp.maximum(m_old, jnp.max(s, axis=-1, keepdims=True))
    alpha = jnp.exp2(m_old - m_new)
    p = jnp.exp2(s - jnp.tile(m_new, (1, s.shape[1] // LANES)))
    acc_ref[...] = alpha * acc_ref[...] + _dot(p.astype(BF16), v_aug)
    m_ref[...] = m_new


def _past_keys_loop(n_keys, tile, step, riders=(), parts=2):
    n_full = n_keys // tile

    def body(j, carry):
        step(pl.multiple_of(j * tile, tile), tile)
        return carry

    extras = []
    for k, (with_tile, alone) in enumerate(riders):
        def both(with_tile=with_tile, k=k):
            out = with_tile()
            step(k * tile, tile)
            return out

        extras.append(lax.cond(n_full > k, both, alone))
    lax.fori_loop(len(riders), n_full, body, 0)
    rest = n_keys - n_full * tile
    start = pl.multiple_of(n_full * tile, tile)
    part = tile // parts
    n_parts = (rest + part - 1) // part
    for k in range(1, parts + 1):
        @pl.when(n_parts == k)
        def _(k=k):
            done = 0
            for size in (tile, tile // 2, tile // 4):
                if size % part == 0 and k * part - done >= size:
                    step(pl.multiple_of(start + done, part), size)
                    done += size
            assert done == k * part

    return extras


def _softmax_result(acc_ref):
    acc = acc_ref[...]
    return acc[:, 0:HEAD_DIM] * (1.0 / acc[:, HEAD_DIM:HEAD_DIM + 1])


def _pick_top(score, pos, width, k):
    sel = jnp.zeros(score.shape, jnp.bool_)
    for _ in range(k):
        m, idx = _first_max(score, pos, width, axis=0)
        hit = (pos == idx) & (m > -jnp.inf)
        sel = sel | hit
        score = jnp.where(pos == idx, -jnp.inf, score)
    return sel


MOBA_LANES = 64
MOBA_TK = 2048


MOBA_TQ = 4 * MOBA_BLOCK


def _moba_kernel(q_ref, ka_ref, va_ref, kmean_ref, o_ref, qa_ref, m_ref, acc_ref):
    i0 = pl.program_id(2) * (MOBA_TQ // MOBA_BLOCK)
    q = q_ref[0, 0]
    q_hi, q_lo = _split(q)
    km_hi, km_lo = _split(kmean_ref[0, 0])
    gate = _dot_nt(km_hi, q_hi) + (_dot_nt(km_lo, q_hi) + _dot_nt(km_hi, q_lo))
    blk = lax.broadcasted_iota(jnp.int32, gate.shape, 0)
    cur = i0 + lax.broadcasted_iota(jnp.int32, gate.shape, 1) // MOBA_BLOCK
    sel = _pick_top(jnp.where(blk < cur, gate, -jnp.inf), blk, gate.shape[0], MOBA_TOPK)
    qs = (q * Q_SCALE).astype(BF16)
    past = jnp.where(sel & (blk < i0), 0.0, NEG).T[:, 0:MOBA_LANES]
    own = jnp.where((blk == cur) | (sel & (blk >= i0)), 0.0, NEG).T[:, 0:MOBA_LANES]
    qa_ref[0] = jnp.concatenate([qs, past.astype(BF16)], axis=1)
    qa_ref[1] = jnp.concatenate([qs, own.astype(BF16)], axis=1)
    _softmax_init(m_ref, acc_ref)

    def step(start, size):
        s = _dot_nt(qa_ref[0], ka_ref[0, 0, pl.ds(start, size), :])
        _softmax_step(s, va_ref[0, 0, pl.ds(start, size), :], m_ref, acc_ref)

    _past_keys_loop(i0 * MOBA_BLOCK, MOBA_TK, step)
    start = pl.multiple_of(i0 * MOBA_BLOCK, MOBA_TQ)
    s = _dot_nt(qa_ref[1], ka_ref[0, 0, pl.ds(start, MOBA_TQ), :])
    qpos = lax.broadcasted_iota(jnp.int32, s.shape, 0)
    kpos = lax.broadcasted_iota(jnp.int32, s.shape, 1)
    hidden = (qpos // MOBA_BLOCK == kpos // MOBA_BLOCK) & (kpos > qpos)
    _softmax_step(jnp.where(hidden, NEG, s), va_ref[0, 0, pl.ds(start, MOBA_TQ), :], m_ref, acc_ref)
    o_ref[0, 0] = _softmax_result(acc_ref).astype(BF16)


def _moba(qm, ka, kmean, va, batch, seq):
    nmb = seq // MOBA_BLOCK
    assert nmb <= MOBA_LANES and seq % MOBA_TK == 0 and seq % MOBA_TQ == 0
    kmean = kmean.reshape(batch, nmb, MOBA_HEADS, HEAD_DIM).transpose(0, 2, 1, 3)
    kmean = jnp.pad(kmean, ((0, 0), (0, 0), (0, LANES - nmb), (0, 0)))
    return pl.pallas_call(
        _moba_kernel,
        grid=(batch, MOBA_HEADS, seq // MOBA_TQ),
        in_specs=[pl.BlockSpec((1, 1, MOBA_TQ, HEAD_DIM), lambda b, h, i: (b, h, i, 0)),
                  pl.BlockSpec((1, 1, seq, LANES), lambda b, h, i: (b, h, 0, 0)),
                  pl.BlockSpec((1, 1, seq, LANES), lambda b, h, i: (b, h, 0, 0)),
                  pl.BlockSpec((1, 1, LANES, HEAD_DIM), lambda b, h, i: (b, h, 0, 0))],
        out_specs=pl.BlockSpec((1, 1, MOBA_TQ, HEAD_DIM), lambda b, h, i: (b, h, i, 0)),
        out_shape=jax.ShapeDtypeStruct((batch, MOBA_HEADS, seq, HEAD_DIM), BF16),
        scratch_shapes=[pltpu.VMEM((2, MOBA_TQ, LANES), BF16), pltpu.VMEM((MOBA_TQ, LANES), F32),
                        pltpu.VMEM((MOBA_TQ, LANES), F32)],
        compiler_params=_params("arbitrary", "arbitrary", "arbitrary"),
        name="moba",
    )(qm, ka, va, kmean)


NSA_TQ = 2 * SEL_BLOCK
NSA_TK = 2048
SEL_LANES = 64
SUPER_KEYS = SEL_LANES * SEL_BLOCK
CMP_WIDTH_STEP = 256


def _nsa_kernel(n_super, q_ref, kc_ref, vc_ref, ka_ref, va_ref, kvw_ref, gt_ref, e_ref, band_ref, o_ref,
                qa_ref, m_ref, acc_ref):
    qi = pl.program_id(2)
    tq = NSA_TQ
    s0 = qi * tq
    q = q_ref[0, 0, 0]

    def compressed(width):
        sc = _dot_nt(q, kc_ref[0, 0, 0:width, :])
        rq = lax.broadcasted_iota(jnp.int32, (sc.shape[0], LANES), 0) & (tq - 1)
        last = jnp.tile((s0 + rq - (CMP_BLOCK - 1)) >> (CMP_STRIDE.bit_length() - 1), (1, width // LANES))
        sc = jnp.where(lax.broadcasted_iota(jnp.int32, sc.shape, 1) <= last, sc, NEG)
        mx = jnp.max(sc, axis=-1, keepdims=True)
        e = jnp.exp2(sc - mx)
        r = jnp.where(mx > 0.5 * NEG, 1.0 / jnp.sum(e, axis=-1, keepdims=True), 0.0)
        imp = e[0:tq] * r[0:tq]
        for g in range(1, NSA_GROUP):
            imp = imp + e[g * tq:(g + 1) * tq] * r[g * tq:(g + 1) * tq]
        return _dot(e.astype(BF16), vc_ref[0, 0, 0:width, :]) * r, _dot_x2(imp, band_ref[0:width, :])

    n16 = kc_ref.shape[2]
    widths = list(range(CMP_WIDTH_STEP, n16, CMP_WIDTH_STEP)) + [n16]
    o_c, pslc = lax.switch((s0 + tq - 1) // (CMP_WIDTH_STEP * CMP_STRIDE),
                           [functools.partial(compressed, wd) for wd in widths])

    wlen = WINDOW + tq

    def window(kvw, masked):
        sw = masked(_dot_nt(q, kvw[:, 0:HEAD_DIM]))
        pw = jnp.exp2(sw - jnp.max(sw, axis=-1, keepdims=True))
        return _dot(pw.astype(BF16), kvw)[:, HEAD_DIM:2 * HEAD_DIM] * (1.0 / jnp.sum(pw, axis=-1, keepdims=True))

    def window_interior():
        def masked(sw):
            rq = lax.broadcasted_iota(jnp.int32, (sw.shape[0], tq), 0) & (tq - 1)
            c = lax.broadcasted_iota(jnp.int32, (sw.shape[0], tq), 1)
            return jnp.concatenate([jnp.where(c > rq, sw[:, 0:tq], NEG), sw[:, tq:WINDOW],
                                    jnp.where(c <= rq, sw[:, WINDOW:wlen], NEG)], axis=1)

        return window(kvw_ref[0, 0, pl.ds(pl.multiple_of(s0 - WINDOW, tq), wlen), :], masked)

    def window_start():
        def masked(sw):
            kabs = lax.broadcasted_iota(jnp.int32, sw.shape, 1)
            t = s0 + (lax.broadcasted_iota(jnp.int32, sw.shape, 0) & (tq - 1))
            return jnp.where((kabs <= t) & (kabs > t - WINDOW), sw, NEG)

        return window(kvw_ref[0, 0, 0:wlen, :], masked)

    pslc_t = pslc.T
    nb = pslc_t.shape[0]
    blk = lax.broadcasted_iota(jnp.int32, pslc_t.shape, 0)
    cur = (s0 + lax.broadcasted_iota(jnp.int32, pslc_t.shape, 1)) // SEL_BLOCK
    c0 = s0 // SEL_BLOCK
    elig = (blk >= 1) & (blk <= cur - 2)
    sel = _pick_top(jnp.where(elig, pslc_t, -jnp.inf), blk, nb, SEL_TOPK - 3)
    sel = sel | (blk == 0) | (blk == cur - 1)
    past = jnp.where(sel & (blk < c0), 0.0, NEG).T.astype(BF16)
    for st in range(n_super):
        b = past[:, st * SEL_LANES:(st + 1) * SEL_LANES]
        qa_ref[st] = jnp.concatenate([q, jnp.concatenate([b] * NSA_GROUP, axis=0)], axis=1)

    _softmax_init(m_ref, acc_ref)
    d0 = pl.multiple_of(s0, tq)
    s = _dot_nt(q, ka_ref[0, 0, pl.ds(d0, tq), :][:, 0:HEAD_DIM])
    qpos = lax.broadcasted_iota(jnp.int32, s.shape, 0) & (tq - 1)
    kpos = lax.broadcasted_iota(jnp.int32, s.shape, 1)
    _softmax_step(jnp.where(kpos <= qpos, s, NEG), va_ref[0, 0, pl.ds(d0, tq), :], m_ref, acc_ref)

    def step(start, size):
        s = _dot_nt(qa_ref[start // SUPER_KEYS], ka_ref[0, 0, pl.ds(start, size), :])
        _softmax_step(s, va_ref[0, 0, pl.ds(start, size), :], m_ref, acc_ref)

    (o_w,) = _past_keys_loop(s0, NSA_TK, step, parts=4, riders=(
        (window_interior, lambda: lax.cond(s0 >= WINDOW, window_interior, window_start)),))
    o_s = _softmax_result(acc_ref)

    w = NSA_GROUP * HEAD_DIM
    gexp = _dot_x2(gt_ref[...], e_ref[0])
    wide = lambda x: jnp.concatenate([x[g * tq:(g + 1) * tq] for g in range(NSA_GROUP)], axis=1)
    o_ref[...] = (gexp[:, 0:w] * wide(o_c) + gexp[:, w:2 * w] * wide(o_s) + gexp[:, 2 * w:3 * w] * wide(o_w)).astype(BF16)


def _nsa(qs, kcmp, vcmp, ka, va, kvw, gates, batch, seq):
    tq = NSA_TQ
    nq = seq // tq
    nb = seq // SEL_BLOCK
    n16 = seq // CMP_STRIDE
    assert seq % SUPER_KEYS == 0 and tq == 2 * SEL_BLOCK and WINDOW % tq == 0 and NSA_TK >= WINDOW
    n_super = seq // SUPER_KEYS
    rows = NSA_GROUP * tq
    e = np.zeros((NSA_KV_HEADS, GATE_LANES, 3 * NSA_GROUP * HEAD_DIM), np.float32)
    for br in range(3):
        for hk in range(NSA_KV_HEADS):
            for g in range(NSA_GROUP):
                c = (br * NSA_GROUP + g) * HEAD_DIM
                e[hk, br * NSA_HEADS + hk * NSA_GROUP + g, c:c + HEAD_DIM] = 1.0
    nn, jj = np.arange(n16)[:, None], np.arange(nb)[None, :]
    band = ((nn >= 4 * jj - 1) & (nn <= 4 * jj + 3)).astype(np.float32)
    resident = lambda width: pl.BlockSpec((1, 1, seq, width), lambda b, h, i: (b, h, 0, 0))
    w = NSA_GROUP * HEAD_DIM
    return pl.pallas_call(
        functools.partial(_nsa_kernel, n_super),
        grid=(batch, NSA_KV_HEADS, nq),
        in_specs=[pl.BlockSpec((1, 1, 1, rows, HEAD_DIM), lambda b, h, i: (b, h, i, 0, 0)),
                  pl.BlockSpec((1, 1, n16, HEAD_DIM), lambda b, h, i: (b, h, 0, 0)),
                  pl.BlockSpec((1, 1, n16, HEAD_DIM), lambda b, h, i: (b, h, 0, 0)),
                  resident(LANES), resident(LANES), resident(LANES),
                  pl.BlockSpec((tq, GATE_LANES), lambda b, h, i: (b * nq + i, 0)),
                  pl.BlockSpec((1, GATE_LANES, 3 * w), lambda b, h, i: (h, 0, 0)),
                  pl.BlockSpec((n16, nb), lambda b, h, i: (0, 0))],
        out_specs=pl.BlockSpec((tq, w), lambda b, h, i: (b * nq + i, h)),
        out_shape=jax.ShapeDtypeStruct((batch * seq, NSA_W), BF16),
        scratch_shapes=[pltpu.VMEM((n_super, rows, LANES), BF16), pltpu.VMEM((rows, LANES), F32),
                        pltpu.VMEM((rows, LANES), F32)],
        compiler_params=_params("arbitrary", "arbitrary", "arbitrary"),
        name="nsa",
    )(qs, kcmp, vcmp, ka, va, kvw, gates, jnp.asarray(e, BF16), jnp.asarray(band, BF16))


def _odd_out_kernel(om_ref, on_ref, h_ref, w_ref, o_ref):
    acc = h_ref[...] + _dot(on_ref[...], w_ref[MOBA_W:D_MODEL, :])
    for h in range(MOBA_HEADS):
        acc = acc + _dot(om_ref[0, h], w_ref[h * HEAD_DIM:(h + 1) * HEAD_DIM, :])
    o_ref[...] = acc


def _odd_out(o_moba, o_nsa, h2, w_out, seq, tm=ROW_TILE):
    t = h2.shape[0]
    tps = seq // tm
    row = lambda w: pl.BlockSpec((tm, w), lambda i: (i, 0))
    return pl.pallas_call(
        _odd_out_kernel,
        grid=(t // tm,),
        in_specs=[pl.BlockSpec((1, MOBA_HEADS, tm, HEAD_DIM), lambda i: (i // tps, 0, i % tps, 0)),
                  row(NSA_W), row(D_MODEL), _full((D_MODEL, D_MODEL))],
        out_specs=row(D_MODEL),
        out_shape=jax.ShapeDtypeStruct((t, D_MODEL), F32),
        compiler_params=_params("arbitrary"),
        name="odd_out",
    )(o_moba, o_nsa, h2, w_out)


def _even_layer(h2, batch, seq, norm, w_in, w_out, lam_re, lam_im, log_dt, b_re, b_im, c_re, c_im, d, w_glu, conv_w, conv_b):
    u, ub, yb = _even_in(h2, norm.reshape(1, D_MODEL), w_in.astype(BF16), conv_w, conv_b.reshape(1, CONV_WIDTH), seq)
    ypre = _s5_mixer_pre(ub, batch, seq, lam_re, lam_im, log_dt, b_re, b_im, c_re, c_im)
    return _even_out(ypre, u, yb, h2, d.reshape(1, S5_WIDTH), w_glu.astype(BF16), w_out.astype(BF16))


def _odd_layer(h2, batch, seq, norm, w_in, w_out, moba_q_norm, moba_k_norm, nsa_q_norm, nsa_kcmp_norm, nsa_ksel_norm,
               nsa_kwin_norm, cmp_pe_k, cmp_w1_k, cmp_w2_k, cmp_pe_v, cmp_w1_v, cmp_w2_v):
    qm, kam, kmean, vam, qs, kvc, kas, vas, kvw, gates = _odd_in(
        h2, batch, seq, norm.reshape(1, D_MODEL), w_in, moba_q_norm, moba_k_norm, nsa_q_norm, nsa_ksel_norm, nsa_kwin_norm)
    cmp = _compress(kvc, batch, seq, cmp_pe_k, cmp_w1_k, cmp_w2_k, cmp_pe_v, cmp_w1_v, cmp_w2_v, nsa_kcmp_norm)
    o_moba = _moba(qm, kam, kmean, vam, batch, seq)
    o_nsa = _nsa(qs, cmp[0], cmp[1], kas, vas, kvw, gates, batch, seq)
    return _odd_out(o_moba, o_nsa, h2, w_out.astype(BF16), seq)


def kernel(x, ev_norm_mix, ev_w_in, ev_w_out, s5_lam_re, s5_lam_im, s5_log_dt, s5_b_re, s5_b_im, s5_c_re, s5_c_im, s5_d, s5_w_glu, conv_w, conv_b, od_norm_mix, od_w_in, od_w_out, moba_q_norm, moba_k_norm, nsa_q_norm, nsa_kcmp_norm, nsa_ksel_norm, nsa_kwin_norm, cmp_pe_k, cmp_w1_k, cmp_w2_k, cmp_pe_v, cmp_w1_v, cmp_w2_v, moe_norm, moe_w_group, moe_b_group, moe_w_expert, moe_b_expert, moe_w_gate, moe_w_up, moe_w_down):
    batch, seq, _ = x.shape
    depth = moe_norm.shape[0]
    h = x.reshape(batch * seq, D_MODEL)
    for layer in range(depth):
        i = layer // 2
        if layer % 2 == 0:
            h = _even_layer(h, batch, seq, ev_norm_mix[i], ev_w_in[i], ev_w_out[i], s5_lam_re[i], s5_lam_im[i], s5_log_dt[i],
                            s5_b_re[i], s5_b_im[i], s5_c_re[i], s5_c_im[i], s5_d[i], s5_w_glu[i], conv_w[i], conv_b[i])
        else:
            h = _odd_layer(h, batch, seq, od_norm_mix[i], od_w_in[i], od_w_out[i], moba_q_norm[i], moba_k_norm[i],
                           nsa_q_norm[i], nsa_kcmp_norm[i], nsa_ksel_norm[i], nsa_kwin_norm[i], cmp_pe_k[i], cmp_w1_k[i],
                           cmp_w2_k[i], cmp_pe_v[i], cmp_w1_v[i], cmp_w2_v[i])
        h = _moe(h, moe_norm[layer].reshape(1, D_MODEL), moe_w_group[layer], moe_b_group[layer], moe_w_expert[layer],
                 moe_b_expert[layer], moe_w_gate[layer], moe_w_up[layer], moe_w_down[layer])
    return h.reshape(batch, seq, D_MODEL)
```

```python
import functools
import math

import jax
import jax.numpy as jnp
import numpy as np
from jax import lax
from jax.experimental import pallas as pl
from jax.experimental.pallas import tpu as pltpu

D_MODEL = 1024
HEAD_DIM = 64
EPS = 1e-6
S5_WIDTH = 256
S5_GROUP = 16
S5_GROUPS = 16
S5_STATE = 64
S5_CHUNK = 16
S5_GROUP_BITS = 4
S5_STATE_BITS = 6
S5_ROW_TILE = 512
S5_COL_TILE = 512
S5_SCAN_TILE = 256
CONV_WIDTH = 768
CONV_K = 3
MOBA_HEADS = 4
NSA_HEADS = 12
NSA_KV_HEADS = 2
NSA_GROUP = 6
MOBA_W = 256
NSA_W = 768
KV_W = 128
MOBA_BLOCK = 256
MOBA_TOPK = 3
CMP_BLOCK = 32
CMP_STRIDE = 16
CMP_HIDDEN = 256
SEL_BLOCK = 64
SEL_TOPK = 8
WINDOW = 512
N_GROUPS = 4
EXPERTS_PER_GROUP = 4
N_EXPERTS = 16
EXPERT_FF = 256

LANES = 128
SUBLANES = 8
VMEM_LIMIT_BYTES = 56 * 1024 * 1024
ROW_TILE = 1024
GATE_LANES = LANES
NEG = -float(2 ** 100)
Q_SCALE = HEAD_DIM ** -0.5 * math.log2(math.e)
F32 = jnp.float32
BF16 = jnp.bfloat16


def _params(*semantics):
    return pltpu.CompilerParams(dimension_semantics=semantics, vmem_limit_bytes=VMEM_LIMIT_BYTES)


def _dot(a, b):
    return jnp.dot(a, b, preferred_element_type=F32)


def _dot_nt(a, b):
    return lax.dot_general(a, b, (((1,), (1,)), ((), ())), preferred_element_type=F32)


def _split(x):
    hi = x.astype(BF16)
    lo = (x - hi.astype(F32)).astype(BF16)
    return hi, lo


def _dot_x2(x, w):
    hi, lo = _split(x)
    return _dot(hi, w) + _dot(lo, w)


def _dot_x3(x, w_hilo):
    n = w_hilo.shape[1] // 2
    hi, lo = _split(x)
    both = _dot(hi, w_hilo)
    return both[:, 0:n] + (both[:, n:2 * n] + _dot(lo, w_hilo[:, 0:n]))


def _rms(x, gain):
    return x * lax.rsqrt(jnp.mean(x * x, axis=-1, keepdims=True) + EPS) * gain


def _gelu(x):
    return 0.5 * x * (1.0 + jnp.tanh(math.sqrt(2.0 / math.pi) * (x + 0.044715 * (x * x * x))))


def _sigmoid(x):
    return 1.0 / (1.0 + jnp.exp(-x))


def _full(shape):
    n = len(shape)
    return pl.BlockSpec(shape, lambda *_: (0,) * n)


def _even_in_kernel(tiles_per_seq, x_ref, g_ref, w_ref, cw_ref, cb_ref, u_ref, ub_ref, yb_ref, carry_ref):
    i = pl.program_id(0)
    xn = _rms(x_ref[...], g_ref[...]).astype(BF16)
    u = _dot(xn, w_ref[:, 0:S5_WIDTH])
    u_ref[...] = u
    ub_ref[...] = u.astype(BF16)
    o = S5_WIDTH
    xc = _dot(xn, w_ref[:, o:o + CONV_WIDTH])
    gb = _dot(xn, w_ref[:, o + CONV_WIDTH:o + 2 * CONV_WIDTH])
    gc = _dot(xn, w_ref[:, o + 2 * CONV_WIDTH:o + 3 * CONV_WIDTH])
    z = gc * xc
    tm = z.shape[0]

    @pl.when(i % tiles_per_seq == 0)
    def _():
        carry_ref[...] = jnp.zeros_like(carry_ref)

    row = lax.broadcasted_iota(jnp.int32, z.shape, 0)
    prev1 = carry_ref[SUBLANES - 1:SUBLANES, :]
    prev2 = carry_ref[SUBLANES - 2:SUBLANES - 1, :]
    z1 = jnp.where(row == 0, prev1, pltpu.roll(z, 1, 0))
    z2 = jnp.where(row == 0, prev2, jnp.where(row == 1, prev1, pltpu.roll(z, 2, 0)))
    y = cw_ref[0:1, :] * z2 + cw_ref[1:2, :] * z1 + cw_ref[2:3, :] * z + cb_ref[...]
    yb_ref[...] = (gb * y).astype(BF16)
    carry_ref[...] = z[tm - SUBLANES:tm, :]


def _even_in(x2, gain, w_in, conv_w, conv_b, seq, tm=ROW_TILE):
    t = x2.shape[0]
    n_in = w_in.shape[1]
    return pl.pallas_call(
        functools.partial(_even_in_kernel, seq // tm),
        grid=(t // tm,),
        in_specs=[pl.BlockSpec((tm, D_MODEL), lambda i: (i, 0)), _full((1, D_MODEL)),
                  _full((D_MODEL, n_in)), _full((CONV_K, CONV_WIDTH)), _full((1, CONV_WIDTH))],
        out_specs=[pl.BlockSpec((tm, S5_WIDTH), lambda i: (i, 0)), pl.BlockSpec((tm, S5_WIDTH), lambda i: (i, 0)),
                   pl.BlockSpec((tm, CONV_WIDTH), lambda i: (i, 0))],
        out_shape=[jax.ShapeDtypeStruct((t, S5_WIDTH), F32), jax.ShapeDtypeStruct((t, S5_WIDTH), BF16),
                   jax.ShapeDtypeStruct((t, CONV_WIDTH), BF16)],
        scratch_shapes=[pltpu.VMEM((SUBLANES, CONV_WIDTH), F32)],
        compiler_params=_params("arbitrary"),
        name="even_in",
    )(x2, gain, w_in, conv_w, conv_b)


def _s5_weights(lam_re, lam_im, log_dt, b_re, b_im, c_re, c_im):
    g, p, hg, ck = S5_GROUPS, S5_STATE, S5_GROUP, S5_CHUNK
    lr, li = lam_re.astype(F32), lam_im.astype(F32)
    dt = jnp.exp(log_dt.astype(F32))[:, None]
    mag = jnp.exp(lr * dt)
    a_re, a_im = mag * jnp.cos(li * dt), mag * jnp.sin(li * dt)
    den = lr * lr + li * li
    f_re = ((a_re - 1.0) * lr + a_im * li) / den
    f_im = (a_im * lr - (a_re - 1.0) * li) / den
    br, bi = b_re.astype(F32), b_im.astype(F32)
    bb_re = f_re[..., None] * br - f_im[..., None] * bi
    bb_im = f_re[..., None] * bi + f_im[..., None] * br
    pw_re, pw_im = [jnp.ones_like(a_re)], [jnp.zeros_like(a_im)]
    for _ in range(ck):
        r, m = pw_re[-1], pw_im[-1]
        pw_re.append(r * a_re - m * a_im)
        pw_im.append(r * a_im + m * a_re)
    pw_re, pw_im = jnp.stack(pw_re), jnp.stack(pw_im)
    cr, ci = c_re.astype(F32), c_im.astype(F32)
    rev_re, rev_im = pw_re[ck - 1::-1][:ck], pw_im[ck - 1::-1][:ck]
    ws_re = rev_re[:, :, :, None] * bb_re[None] - rev_im[:, :, :, None] * bb_im[None]
    ws_im = rev_re[:, :, :, None] * bb_im[None] + rev_im[:, :, :, None] * bb_re[None]
    ca_re = cr[None] * pw_re[1:, :, None, :] - ci[None] * pw_im[1:, :, None, :]
    ca_im = cr[None] * pw_im[1:, :, None, :] + ci[None] * pw_re[1:, :, None, :]
    cb_re = jnp.einsum('ghp,kgp,gpj->kghj', cr, pw_re[:ck], bb_re) - jnp.einsum('ghp,kgp,gpj->kghj', cr, pw_im[:ck], bb_im) \
        - jnp.einsum('ghp,kgp,gpj->kghj', ci, pw_re[:ck], bb_im) - jnp.einsum('ghp,kgp,gpj->kghj', ci, pw_im[:ck], bb_re)
    lag = np.arange(ck)[None, :] - np.arange(ck)[:, None]
    tz = cb_re[np.clip(lag, 0, ck - 1)]
    tz = jnp.where((lag >= 0)[:, :, None, None, None], tz, 0.0)
    cw = ck * g * hg
    ws = jnp.stack([ws_re, ws_im]).transpose(1, 2, 4, 0, 3).reshape(cw, 2 * p)
    wc = jnp.stack([ca_re, -ca_im]).transpose(0, 2, 4, 1, 3).reshape(2 * g * p, ck * hg)
    tzc = tz.transpose(0, 2, 4, 1, 3).reshape(cw, ck * hg)
    return ws.astype(BF16), wc.astype(BF16), tzc.astype(BF16), pw_re[ck].reshape(1, g * p), pw_im[ck].reshape(1, g * p)


def _group_expand(compact, expand, row_shift, col_shift, col0):
    full = _dot(compact, expand)
    row = lax.broadcasted_iota(jnp.int32, full.shape, 0)
    col = col0 + lax.broadcasted_iota(jnp.int32, full.shape, 1)
    same = ((row >> row_shift) & (S5_GROUPS - 1)) == ((col >> col_shift) & (S5_GROUPS - 1))
    return jnp.where(same, full, 0.0).astype(BF16)


def _s5_state_kernel(u_ref, ws_ref, e_ref, s_ref, w_scr):
    @pl.when(pl.program_id(1) == 0)
    def _():
        w_scr[...] = _group_expand(ws_ref[...], e_ref[...], S5_GROUP_BITS, S5_STATE_BITS,
                                   pl.program_id(0) * w_scr.shape[1])

    s_ref[...] = _dot(u_ref[...], w_scr[...])


def _s5_scan_kernel(s_ref, are_ref, aim_ref, xprev_ref, st_ref):
    @pl.when(pl.program_id(0) == 0)
    def _():
        st_ref[...] = jnp.zeros_like(st_ref)

    a_re, a_im = are_ref[...], aim_ref[...]
    nb, n = s_ref.shape[0], s_ref.shape[1]
    half = a_re.shape[1]

    def body(c, carry):
        out = []
        for b in range(nb):
            xr, xi = carry[2 * b], carry[2 * b + 1]
            xprev_ref[b, pl.ds(c, 1), 0:half] = xr
            xprev_ref[b, pl.ds(c, 1), half:2 * half] = xi
            s = s_ref[b, pl.ds(c, 1), :]
            out += [a_re * xr - a_im * xi + s[:, 0:half], a_re * xi + a_im * xr + s[:, half:2 * half]]
        return tuple(out)

    init = tuple(st_ref[b:b + 1, o:o + half] for b in range(nb) for o in (0, half))
    final = lax.fori_loop(0, n, body, init, unroll=4)
    for b in range(nb):
        st_ref[b:b + 1, 0:half] = final[2 * b]
        st_ref[b:b + 1, half:2 * half] = final[2 * b + 1]


def _s5_out_kernel(u_ref, xp_ref, tz_ref, wc_ref, e_ref, y_ref, tz_scr, wc_scr):
    @pl.when(pl.program_id(1) == 0)
    def _():
        col0 = pl.program_id(0) * tz_scr.shape[1]
        tz_scr[...] = _group_expand(tz_ref[...], e_ref[...], S5_GROUP_BITS, S5_GROUP_BITS, col0)
        wc_scr[...] = _group_expand(wc_ref[...], e_ref[...], S5_STATE_BITS, S5_GROUP_BITS, col0)

    y_ref[...] = _dot(u_ref[...], tz_scr[...]) + _dot(xp_ref[...].astype(BF16), wc_scr[...])


def _expand_matrix(outer, inner):
    e = np.zeros((outer, inner, outer, S5_GROUPS, inner), np.float32)
    for x in range(outer):
        for y in range(inner):
            e[x, y, x, :, y] = 1.0
    return jnp.asarray(e.reshape(outer * inner, outer * S5_GROUPS * inner), BF16)


def _s5_mixer_pre(ub, batch, seq, lam_re, lam_im, log_dt, b_re, b_im, c_re, c_im):
    ws, wc, tz, a16_re, a16_im = _s5_weights(lam_re, lam_im, log_dt, b_re, b_im, c_re, c_im)
    nc = seq // S5_CHUNK
    rows = batch * nc
    cw = S5_CHUNK * S5_WIDTH
    sw = 2 * S5_GROUPS * S5_STATE
    assert S5_GROUP == 1 << S5_GROUP_BITS and S5_STATE == 1 << S5_STATE_BITS
    tr = min(rows, S5_ROW_TILE)
    tn = S5_COL_TILE
    uc = ub.reshape(rows, cw)
    e_state = _expand_matrix(2, S5_STATE)
    e_out = _expand_matrix(S5_CHUNK, S5_GROUP)
    s = pl.pallas_call(
        _s5_state_kernel,
        grid=(sw // tn, rows // tr),
        in_specs=[pl.BlockSpec((tr, cw), lambda j, i: (i, 0)), _full(ws.shape),
                  pl.BlockSpec((e_state.shape[0], tn), lambda j, i: (0, j))],
        out_specs=pl.BlockSpec((tr, tn), lambda j, i: (i, j)),
        out_shape=jax.ShapeDtypeStruct((rows, sw), F32),
        scratch_shapes=[pltpu.VMEM((cw, tn), BF16)],
        compiler_params=_params("arbitrary", "arbitrary"),
        name="s5_state",
    )(uc, ws, e_state)
    tc = min(nc, S5_SCAN_TILE)
    xprev = pl.pallas_call(
        _s5_scan_kernel,
        grid=(nc // tc,),
        in_specs=[pl.BlockSpec((batch, tc, sw), lambda i: (0, i, 0)), _full((1, sw // 2)), _full((1, sw // 2))],
        out_specs=pl.BlockSpec((batch, tc, sw), lambda i: (0, i, 0)),
        out_shape=jax.ShapeDtypeStruct((batch, nc, sw), F32),
        scratch_shapes=[pltpu.VMEM((batch, sw), F32)],
        compiler_params=_params("arbitrary"),
        name="s5_scan",
    )(s.reshape(batch, nc, sw), a16_re, a16_im)
    y = pl.pallas_call(
        _s5_out_kernel,
        grid=(cw // tn, rows // tr),
        in_specs=[pl.BlockSpec((tr, cw), lambda j, i: (i, 0)), pl.BlockSpec((tr, sw), lambda j, i: (i, 0)),
                  _full(tz.shape), _full(wc.shape), pl.BlockSpec((e_out.shape[0], tn), lambda j, i: (0, j))],
        out_specs=pl.BlockSpec((tr, tn), lambda j, i: (i, j)),
        out_shape=jax.ShapeDtypeStruct((rows, cw), F32),
        scratch_shapes=[pltpu.VMEM((cw, tn), BF16), pltpu.VMEM((sw, tn), BF16)],
        compiler_params=_params("arbitrary", "arbitrary"),
        name="s5_out",
    )(uc, xprev.reshape(rows, sw), tz, wc, e_out)
    return y.reshape(batch * seq, S5_WIDTH)


def _even_out_kernel(ypre_ref, u_ref, yb_ref, x_ref, d_ref, wglu_ref, wout_ref, mg_ref, wr_ref, br_ref,
                     o_ref, gate_ref, grp_ref, cnt_ref):
    y = _gelu(ypre_ref[...] + d_ref[...] * u_ref[...])
    y = y * _sigmoid(_dot(y.astype(BF16), wglu_ref[...]))
    h = (x_ref[...] + _dot(y.astype(BF16), wout_ref[0:S5_WIDTH, :]) + _dot(yb_ref[...], wout_ref[S5_WIDTH:D_MODEL, :]))
    o_ref[...] = h
    _route(h, mg_ref, wr_ref, br_ref, gate_ref, grp_ref, cnt_ref)


def _even_out(ypre, u, yb, x2, d, w_glu, w_out, router, tm=ROW_TILE):
    t = x2.shape[0]
    row = lambda w: pl.BlockSpec((tm, w), lambda i: (i, 0))
    r_in, r_out, r_shape = _router_specs(t, tm)
    h, *routing = pl.pallas_call(
        _even_out_kernel,
        grid=(t // tm,),
        in_specs=[row(S5_WIDTH), row(S5_WIDTH), row(CONV_WIDTH), row(D_MODEL), _full((1, S5_WIDTH)),
                  _full((S5_WIDTH, S5_WIDTH)), _full((D_MODEL, D_MODEL))] + r_in,
        out_specs=[row(D_MODEL)] + r_out,
        out_shape=[jax.ShapeDtypeStruct((t, D_MODEL), F32)] + r_shape,
        compiler_params=_params("arbitrary"),
        name="even_out",
    )(ypre, u, yb, x2, d, w_glu, w_out, *router)
    return h, routing


def _first_max(v, pos, width, axis=-1):
    m = jnp.max(v, axis=axis, keepdims=True)
    idx = jnp.min(jnp.where(v == m, pos, width), axis=axis, keepdims=True)
    return m, idx


MOE_TM = 1024
MOE_ALIGN = 16
MOE_SLOTS = 1152
MOE_WIN = 320
ROUTER_LANES = LANES


def _route(h, g_ref, wr_ref, br_ref, gate_ref, grp_ref, cnt_ref):
    xn = _rms(h, g_ref[...])
    logits = _dot_x3(xn, wr_ref[...]) + br_ref[...]
    lane = lax.broadcasted_iota(jnp.int32, logits.shape, 1).astype(F32)
    width = float(logits.shape[1])
    is_g = lane < N_GROUPS
    gl = jnp.where(is_g, logits, -jnp.inf)
    gm, gi = _first_max(gl, lane, width)
    gw = 1.0 / jnp.sum(jnp.where(is_g, jnp.exp(gl - gm), 0.0), axis=-1, keepdims=True)
    lo = N_GROUPS + gi * EXPERTS_PER_GROUP
    in_grp = (lane >= lo) & (lane < lo + EXPERTS_PER_GROUP)
    el = jnp.where(in_grp, logits, -jnp.inf)
    m1, i1 = _first_max(el, lane, width)
    m2, i2 = _first_max(jnp.where(lane == i1, -jnp.inf, el), lane, width)
    p2 = jnp.exp(m2 - m1)
    w1 = gw / (1.0 + p2)
    w2 = gw * p2 / (1.0 + p2)
    gate_ref[...] = jnp.where(lane == i1, w1, 0.0) + jnp.where(lane == i2, w2, 0.0)
    grp = (lane == gi).astype(F32)
    grp_ref[...] = grp.astype(BF16)
    cnt_ref[0] = jnp.broadcast_to(jnp.sum(grp, axis=0, keepdims=True), cnt_ref.shape[1:])


def _moe_expert_kernel(base_ref, nwin_ref, h_ref, g_ref, gate_ref, grp_ref, wg_ref, wu_ref, wd_ref, o_ref,
                       xs_ref, gs_ref, ys_ref, pt_ref):
    i, g = pl.program_id(0), pl.program_id(1)
    tm = h_ref.shape[0]

    @pl.when((i == 0) & (g == 0))
    def _():
        xs_ref[...] = jnp.zeros_like(xs_ref)
        gs_ref[...] = jnp.zeros_like(gs_ref)
        ys_ref[...] = jnp.zeros_like(ys_ref)

    @pl.when(g == 0)
    def _():
        xn = _rms(h_ref[...], g_ref[...]).astype(BF16)
        grp = grp_ref[...]
        earlier = (lax.broadcasted_iota(jnp.int32, (tm, tm), 0) > lax.broadcasted_iota(jnp.int32, (tm, tm), 1)).astype(BF16)
        rank = _dot(earlier, grp)
        lane = lax.broadcasted_iota(jnp.int32, rank.shape, 1)
        for k in range(N_GROUPS):
            rank = rank + jnp.where(lane == k, base_ref[i * N_GROUPS + k].astype(F32), 0.0)
        slot = jnp.sum(grp.astype(F32) * rank, axis=-1, keepdims=True).astype(jnp.int32)
        pt = (lax.broadcasted_iota(jnp.int32, (tm, MOE_SLOTS), 1) == slot).astype(BF16)
        pt_ref[...] = pt
        gather = lambda x: lax.dot_general(pt, x, (((0,), (0,)), ((), ())), preferred_element_type=F32)
        xs_ref[0:MOE_SLOTS, :] = gather(xn).astype(BF16)
        both = gather(jnp.concatenate(_split(gate_ref[...]), axis=1))
        gs_ref[0:MOE_SLOTS, :] = both[:, 0:ROUTER_LANES] + both[:, ROUTER_LANES:2 * ROUTER_LANES]

    def window(w, carry):
        r0 = pl.multiple_of(base_ref[i * N_GROUPS + g] + w * MOE_WIN, MOE_ALIGN)
        x = xs_ref[pl.ds(r0, MOE_WIN), :]
        gate = gs_ref[pl.ds(r0, MOE_WIN), :]
        lane = lax.broadcasted_iota(jnp.int32, gate.shape, 1)
        y = None
        for j in range(EXPERTS_PER_GROUP):
            ge = jnp.sum(jnp.where(lane == g * EXPERTS_PER_GROUP + (j + N_GROUPS), gate, 0.0), axis=-1, keepdims=True)
            h1 = _dot(x, wg_ref[j])
            h3 = _dot(x, wu_ref[j])
            act = (h1 * _sigmoid(h1)) * h3 * ge
            yj = _dot(act.astype(BF16), wd_ref[j])
            y = yj if y is None else y + yj
        ys_ref[pl.ds(r0, MOE_WIN), :] = y.astype(BF16)
        return carry

    lax.fori_loop(0, nwin_ref[i * N_GROUPS + g], window, 0)

    @pl.when(g == N_GROUPS - 1)
    def _():
        o_ref[...] = h_ref[...] + _dot(pt_ref[...], ys_ref[0:MOE_SLOTS, :])


def _router_operands(moe_gain, w_group, b_group, w_expert, b_expert):
    rw = ROUTER_LANES
    wr = jnp.zeros((D_MODEL, rw), F32).at[:, 0:N_GROUPS].set(w_group).at[:, N_GROUPS:N_GROUPS + N_EXPERTS].set(w_expert)
    br = jnp.zeros((1, rw), F32).at[0, 0:N_GROUPS].set(b_group).at[0, N_GROUPS:N_GROUPS + N_EXPERTS].set(b_expert)
    return moe_gain.reshape(1, D_MODEL), jnp.concatenate(_split(wr), axis=1), br


def _router_specs(t, tm):
    assert tm == MOE_TM and t % tm == 0
    rw = ROUTER_LANES
    row = lambda width: pl.BlockSpec((tm, width), lambda i: (i, 0))
    in_specs = [_full((1, D_MODEL)), _full((D_MODEL, 2 * rw)), _full((1, rw))]
    out_specs = [row(rw), row(rw), pl.BlockSpec((1, SUBLANES, rw), lambda i: (i, 0, 0))]
    out_shape = [jax.ShapeDtypeStruct((t, rw), F32), jax.ShapeDtypeStruct((t, rw), BF16),
                 jax.ShapeDtypeStruct((t // tm, SUBLANES, rw), F32)]
    return in_specs, out_specs, out_shape


def _moe(h2, gain, routing, w_gate, w_up, w_down):
    t = h2.shape[0]
    tm, rw = MOE_TM, ROUTER_LANES
    assert t % tm == 0 and MOE_SLOTS >= tm + N_GROUPS * (MOE_ALIGN - 1) and MOE_WIN % MOE_ALIGN == 0
    n_tiles = t // tm
    gates, grp, cnt = routing
    n = cnt[:, 0, 0:N_GROUPS].astype(jnp.int32)
    padded = (n + (MOE_ALIGN - 1)) // MOE_ALIGN * MOE_ALIGN
    base = (jnp.cumsum(padded, axis=1) - padded).reshape(-1)
    nwin = ((padded + (MOE_WIN - 1)) // MOE_WIN).reshape(-1)
    tile = lambda width: pl.BlockSpec((tm, width), lambda i, g, *_: (i, 0))
    experts = lambda shape: pl.BlockSpec((EXPERTS_PER_GROUP,) + shape, lambda i, g, *_: (g, 0, 0))
    slots = MOE_SLOTS + MOE_WIN
    return pl.pallas_call(
        _moe_expert_kernel,
        grid_spec=pltpu.PrefetchScalarGridSpec(
            num_scalar_prefetch=2,
            grid=(n_tiles, N_GROUPS),
            in_specs=[tile(D_MODEL), pl.BlockSpec((1, D_MODEL), lambda i, g, *_: (0, 0)), tile(rw), tile(rw),
                      experts((D_MODEL, EXPERT_FF)), experts((D_MODEL, EXPERT_FF)), experts((EXPERT_FF, D_MODEL))],
            out_specs=tile(D_MODEL),
            scratch_shapes=[pltpu.VMEM((slots, D_MODEL), BF16), pltpu.VMEM((slots, rw), F32),
                            pltpu.VMEM((slots, D_MODEL), BF16), pltpu.VMEM((tm, MOE_SLOTS), BF16)]),
        out_shape=jax.ShapeDtypeStruct((t, D_MODEL), F32),
        compiler_params=_params("arbitrary", "arbitrary"),
        name="moe",
    )(base, nwin, h2, gain, gates, grp, w_gate.astype(BF16), w_up.astype(BF16), w_down.astype(BF16))


ODD_SPLITS = (MOBA_W, MOBA_W, MOBA_W, NSA_W, KV_W, KV_W, KV_W, KV_W, KV_W, KV_W, GATE_LANES)
ODD_IN_PAD = sum(ODD_SPLITS)


def _head_rms(x, hsum, gain):
    w = x.shape[1]
    ss = jnp.concatenate([_dot_x2(x[:, o:o + hsum.shape[0]] * x[:, o:o + hsum.shape[0]], hsum)
                          for o in range(0, w, hsum.shape[0])], axis=1) if w > hsum.shape[0] else _dot_x2(x * x, hsum)
    return x * lax.rsqrt(ss * (1.0 / HEAD_DIM) + EPS) * gain


def _odd_in_kernel(tiles_per_seq, x_ref, g_ref, w_ref, hsum_ref, gq_ref, gk_ref, gnq_ref, gks_ref, gkw_ref,
                   qm_ref, kam_ref, kmean_ref, vam_ref, qs_ref, kvc_ref, kas_ref, vas_ref, kvw_ref, gt_ref):
    xn = _rms(x_ref[...], g_ref[...]).astype(BF16)
    offs = np.cumsum((0,) + ODD_SPLITS)
    col = lambda j: _dot(xn, w_ref[:, int(offs[j]):int(offs[j + 1])])
    head = lambda x, h: x[:, h * HEAD_DIM:(h + 1) * HEAD_DIM]
    hsum = hsum_ref[...]
    hsum128 = hsum_ref[0:KV_W, 0:KV_W]
    tm = x_ref.shape[0]
    pos = (pl.program_id(0) % tiles_per_seq) * tm + lax.broadcasted_iota(jnp.int32, (tm, HEAD_DIM), 0)
    lane = lax.broadcasted_iota(jnp.int32, (tm, HEAD_DIM), 1)
    ones_col = (lane == 0).astype(BF16)

    qm = _head_rms(col(0), hsum, gq_ref[...])
    km = _head_rms(col(1), hsum, gk_ref[...])
    for j in range(tm // MOBA_BLOCK):
        kmean_ref[0, j:j + 1, :] = jnp.mean(km[j * MOBA_BLOCK:(j + 1) * MOBA_BLOCK, :], axis=0, keepdims=True)
    km = km.astype(BF16)
    vm = col(2).astype(BF16)
    moba_id = (lane == pos // MOBA_BLOCK).astype(BF16)
    for h in range(MOBA_HEADS):
        qm_ref[0, h] = head(qm, h)
        kam_ref[0, h] = jnp.concatenate([head(km, h), moba_id], axis=1)
        vam_ref[0, h] = jnp.concatenate([head(vm, h), ones_col], axis=1)

    qd = (_head_rms(col(3), hsum, gnq_ref[...]) * Q_SCALE).astype(BF16)
    for hk in range(NSA_KV_HEADS):
        for j in range(tm // NSA_TQ):
            for g in range(NSA_GROUP):
                qs_ref[0, hk, j, g * NSA_TQ:(g + 1) * NSA_TQ, :] = head(qd, hk * NSA_GROUP + g)[j * NSA_TQ:(j + 1) * NSA_TQ, :]
    kvc_ref[0] = col(4)
    kvc_ref[1] = col(5)
    ks = _head_rms(col(6), hsum128, gks_ref[...]).astype(BF16)
    vs = col(7).astype(BF16)
    kw = _head_rms(col(8), hsum128, gkw_ref[...]).astype(BF16)
    vw = col(9).astype(BF16)
    sel_id = (lane == (pos // SEL_BLOCK) % SEL_LANES).astype(BF16)
    for hk in range(NSA_KV_HEADS):
        kas_ref[0, hk] = jnp.concatenate([head(ks, hk), sel_id], axis=1)
        vas_ref[0, hk] = jnp.concatenate([head(vs, hk), ones_col], axis=1)
        kvw_ref[0, hk] = jnp.concatenate([head(kw, hk), head(vw, hk)], axis=1)
    gt_ref[...] = _sigmoid(col(10))


def _odd_in(h2, batch, seq, gain, w_in, moba_q_norm, moba_k_norm, nsa_q_norm, nsa_ksel_norm, nsa_kwin_norm, tm=ROW_TILE):
    t = h2.shape[0]
    assert MOBA_LANES == HEAD_DIM and SEL_LANES == HEAD_DIM and seq % tm == 0 and tm % MOBA_BLOCK == 0
    tps = seq // tm
    w = jnp.pad(w_in, ((0, 0), (0, ODD_IN_PAD - w_in.shape[1]))).astype(BF16)
    hsum = jnp.asarray(np.kron(np.eye(MOBA_W // HEAD_DIM), np.ones((HEAD_DIM, HEAD_DIM))), BF16)
    tile = lambda g, width: jnp.tile(g.astype(F32), width // HEAD_DIM).reshape(1, width)
    row = lambda width: pl.BlockSpec((tm, width), lambda i: (i, 0))
    heads = lambda n, width: pl.BlockSpec((1, n, tm, width), lambda i: (i // tps, 0, i % tps, 0))
    nmb = tm // MOBA_BLOCK
    nqt = tm // NSA_TQ
    rows = NSA_GROUP * NSA_TQ
    sds = jax.ShapeDtypeStruct
    out_specs = [heads(MOBA_HEADS, HEAD_DIM), heads(MOBA_HEADS, LANES), pl.BlockSpec((1, nmb, MOBA_W), lambda i: (i, 0, 0)),
                 heads(MOBA_HEADS, LANES),
                 pl.BlockSpec((1, NSA_KV_HEADS, nqt, rows, HEAD_DIM), lambda i: (i // tps, 0, i % tps, 0, 0)),
                 pl.BlockSpec((2, tm, KV_W), lambda i: (0, i, 0)),
                 heads(NSA_KV_HEADS, LANES), heads(NSA_KV_HEADS, LANES), heads(NSA_KV_HEADS, LANES),
                 row(GATE_LANES)]
    out_shape = [sds((batch, MOBA_HEADS, seq, HEAD_DIM), F32), sds((batch, MOBA_HEADS, seq, LANES), BF16),
                 sds((t // tm, nmb, MOBA_W), F32), sds((batch, MOBA_HEADS, seq, LANES), BF16),
                 sds((batch, NSA_KV_HEADS, seq // NSA_TQ, rows, HEAD_DIM), BF16),
                 sds((2, t, KV_W), F32), sds((batch, NSA_KV_HEADS, seq, LANES), BF16),
                 sds((batch, NSA_KV_HEADS, seq, LANES), BF16), sds((batch, NSA_KV_HEADS, seq, LANES), BF16),
                 sds((t, GATE_LANES), F32)]
    return pl.pallas_call(
        functools.partial(_odd_in_kernel, tps),
        grid=(t // tm,),
        in_specs=[row(D_MODEL), _full((1, D_MODEL)), _full((D_MODEL, ODD_IN_PAD)), _full((MOBA_W, MOBA_W)),
                  _full((1, MOBA_W)), _full((1, MOBA_W)), _full((1, NSA_W)), _full((1, KV_W)), _full((1, KV_W))],
        out_specs=out_specs,
        out_shape=out_shape,
        compiler_params=_params("arbitrary"),
        name="odd_in",
    )(h2, gain, w, hsum, tile(moba_q_norm, MOBA_W), tile(moba_k_norm, MOBA_W), tile(nsa_q_norm, NSA_W),
      tile(nsa_ksel_norm, KV_W), tile(nsa_kwin_norm, KV_W))


def _compress_kernel(c_ref, w1_ref, w2_ref, pe_ref, g_ref, o_ref):
    kind = pl.program_id(0)
    n16 = c_ref.shape[1] // CMP_STRIDE
    half = CMP_STRIDE * HEAD_DIM
    peb = _dot(pe_ref[0], w1_ref[0])[0:1, :]
    xs = [c_ref[0, pl.ds(s, n16, stride=CMP_STRIDE), :].astype(BF16) for s in range(CMP_STRIDE)]
    for h in range(NSA_KV_HEADS):
        first = second = None
        for s in range(CMP_STRIDE):
            x = xs[s][:, h * HEAD_DIM:(h + 1) * HEAD_DIM]
            a = _dot(x, w1_ref[0, s * HEAD_DIM:(s + 1) * HEAD_DIM, :])
            b = _dot(x, w1_ref[0, half + s * HEAD_DIM:half + (s + 1) * HEAD_DIM, :])
            first = a if first is None else first + a
            second = b if second is None else second + b
        hid = _gelu(first + pltpu.roll(second, n16 - 1, 0) + peb)
        out = _dot(hid.astype(BF16), w2_ref[0])
        o_ref[0, 0, h] = jnp.where(kind == 0, _rms(out, g_ref[...]), out).astype(BF16)


def _compress(kvc, batch, seq, pe_k, w1_k, w2_k, pe_v, w1_v, w2_v, kcmp_norm):
    n16 = seq // CMP_STRIDE
    half = CMP_STRIDE * HEAD_DIM
    w1 = jnp.stack([w1_k, w1_v]).astype(BF16)
    w2 = jnp.stack([w2_k, w2_v]).astype(BF16)
    pe = jnp.stack([pe_k, pe_v]).reshape(2, 1, 2 * half)
    pe = jnp.broadcast_to(pe, (2, SUBLANES, 2 * half)).astype(BF16)
    return pl.pallas_call(
        _compress_kernel,
        grid=(2, batch),
        in_specs=[pl.BlockSpec((1, seq, KV_W), lambda k, b: (k, b, 0)),
                  pl.BlockSpec((1, 2 * half, CMP_HIDDEN), lambda k, b: (k, 0, 0)),
                  pl.BlockSpec((1, CMP_HIDDEN, HEAD_DIM), lambda k, b: (k, 0, 0)),
                  pl.BlockSpec((1, SUBLANES, 2 * half), lambda k, b: (k, 0, 0)),
                  _full((1, HEAD_DIM))],
        out_specs=pl.BlockSpec((1, 1, NSA_KV_HEADS, n16, HEAD_DIM), lambda k, b: (k, b, 0, 0, 0)),
        out_shape=jax.ShapeDtypeStruct((2, batch, NSA_KV_HEADS, n16, HEAD_DIM), BF16),
        compiler_params=_params("arbitrary", "arbitrary"),
        name="nsa_compress",
    )(kvc, w1, w2, pe, kcmp_norm.astype(F32).reshape(1, HEAD_DIM))


M_INIT = -1e30


def _softmax_init(m_ref, acc_ref):
    m_ref[...] = jnp.full(m_ref.shape, M_INIT, F32)
    acc_ref[...] = jnp.zeros(acc_ref.shape, F32)


def _softmax_step(s, v_aug, m_ref, acc_ref):
    m_old = m_ref[...]
    m_new = jnp.maximum(m_old, jnp.max(s, axis=-1, keepdims=True))
    alpha = jnp.exp2(m_old - m_new)
    p = jnp.exp2(s - jnp.tile(m_new, (1, s.shape[1] // LANES)))
    acc_ref[...] = alpha * acc_ref[...] + _dot(p.astype(BF16), v_aug)
    m_ref[...] = m_new


def _past_keys_loop(n_keys, tile, step, riders=(), parts=2):
    n_full = n_keys // tile

    def body(j, carry):
        step(pl.multiple_of(j * tile, tile), tile)
        return carry

    extras = []
    for k, (with_tile, alone) in enumerate(riders):
        def both(with_tile=with_tile, k=k):
            out = with_tile()
            step(k * tile, tile)
            return out

        extras.append(lax.cond(n_full > k, both, alone))
    lax.fori_loop(len(riders), n_full, body, 0)
    rest = n_keys - n_full * tile
    start = pl.multiple_of(n_full * tile, tile)
    part = tile // parts
    n_parts = (rest + part - 1) // part
    for k in range(1, parts + 1):
        @pl.when(n_parts == k)
        def _(k=k):
            done = 0
            for size in (tile, tile // 2, tile // 4):
                if size % part == 0 and k * part - done >= size:
                    step(pl.multiple_of(start + done, part), size)
                    done += size
            assert done == k * part

    return extras


def _softmax_result(acc_ref):
    acc = acc_ref[...]
    return acc[:, 0:HEAD_DIM] * (1.0 / acc[:, HEAD_DIM:HEAD_DIM + 1])


def _pick_top(score, pos, width, k):
    sel = jnp.zeros(score.shape, jnp.bool_)
    for _ in range(k):
        m, idx = _first_max(score, pos, width, axis=0)
        hit = (pos == idx) & (m > -jnp.inf)
        sel = sel | hit
        score = jnp.where(pos == idx, -jnp.inf, score)
    return sel


MOBA_LANES = 64
MOBA_TK = 2048


MOBA_TQ = 4 * MOBA_BLOCK


def _moba_kernel(q_ref, ka_ref, va_ref, kmean_ref, o_ref, qa_ref, m_ref, acc_ref):
    i0 = pl.program_id(2) * (MOBA_TQ // MOBA_BLOCK)
    q = q_ref[0, 0]
    q_hi, q_lo = _split(q)
    km_hi, km_lo = _split(kmean_ref[0, 0])
    gate = _dot_nt(km_hi, q_hi) + (_dot_nt(km_lo, q_hi) + _dot_nt(km_hi, q_lo))
    blk = lax.broadcasted_iota(jnp.int32, gate.shape, 0)
    cur = i0 + lax.broadcasted_iota(jnp.int32, gate.shape, 1) // MOBA_BLOCK
    sel = _pick_top(jnp.where(blk < cur, gate, -jnp.inf), blk, gate.shape[0], MOBA_TOPK)
    qs = (q * Q_SCALE).astype(BF16)
    past = jnp.where(sel & (blk < i0), 0.0, NEG).T[:, 0:MOBA_LANES]
    own = jnp.where((blk == cur) | (sel & (blk >= i0)), 0.0, NEG).T[:, 0:MOBA_LANES]
    qa_ref[0] = jnp.concatenate([qs, past.astype(BF16)], axis=1)
    qa_ref[1] = jnp.concatenate([qs, own.astype(BF16)], axis=1)
    _softmax_init(m_ref, acc_ref)

    def step(start, size):
        s = _dot_nt(qa_ref[0], ka_ref[0, 0, pl.ds(start, size), :])
        _softmax_step(s, va_ref[0, 0, pl.ds(start, size), :], m_ref, acc_ref)

    _past_keys_loop(i0 * MOBA_BLOCK, MOBA_TK, step)
    start = pl.multiple_of(i0 * MOBA_BLOCK, MOBA_TQ)
    s = _dot_nt(qa_ref[1], ka_ref[0, 0, pl.ds(start, MOBA_TQ), :])
    qpos = lax.broadcasted_iota(jnp.int32, s.shape, 0)
    kpos = lax.broadcasted_iota(jnp.int32, s.shape, 1)
    hidden = (qpos // MOBA_BLOCK == kpos // MOBA_BLOCK) & (kpos > qpos)
    _softmax_step(jnp.where(hidden, NEG, s), va_ref[0, 0, pl.ds(start, MOBA_TQ), :], m_ref, acc_ref)
    o_ref[0, 0] = _softmax_result(acc_ref).astype(BF16)


def _moba(qm, ka, kmean, va, batch, seq):
    nmb = seq // MOBA_BLOCK
    assert nmb <= MOBA_LANES and seq % MOBA_TK == 0 and seq % MOBA_TQ == 0
    kmean = kmean.reshape(batch, nmb, MOBA_HEADS, HEAD_DIM).transpose(0, 2, 1, 3)
    kmean = jnp.pad(kmean, ((0, 0), (0, 0), (0, LANES - nmb), (0, 0)))
    return pl.pallas_call(
        _moba_kernel,
        grid=(batch, MOBA_HEADS, seq // MOBA_TQ),
        in_specs=[pl.BlockSpec((1, 1, MOBA_TQ, HEAD_DIM), lambda b, h, i: (b, h, i, 0)),
                  pl.BlockSpec((1, 1, seq, LANES), lambda b, h, i: (b, h, 0, 0)),
                  pl.BlockSpec((1, 1, seq, LANES), lambda b, h, i: (b, h, 0, 0)),
                  pl.BlockSpec((1, 1, LANES, HEAD_DIM), lambda b, h, i: (b, h, 0, 0))],
        out_specs=pl.BlockSpec((1, 1, MOBA_TQ, HEAD_DIM), lambda b, h, i: (b, h, i, 0)),
        out_shape=jax.ShapeDtypeStruct((batch, MOBA_HEADS, seq, HEAD_DIM), BF16),
        scratch_shapes=[pltpu.VMEM((2, MOBA_TQ, LANES), BF16), pltpu.VMEM((MOBA_TQ, LANES), F32),
                        pltpu.VMEM((MOBA_TQ, LANES), F32)],
        compiler_params=_params("arbitrary", "arbitrary", "arbitrary"),
        name="moba",
    )(qm, ka, va, kmean)


NSA_TQ = 2 * SEL_BLOCK
NSA_TK = 2048
SEL_LANES = 64
SUPER_KEYS = SEL_LANES * SEL_BLOCK
CMP_WIDTH_STEP = 256


def _nsa_kernel(n_super, q_ref, kc_ref, vc_ref, ka_ref, va_ref, kvw_ref, gt_ref, e_ref, band_ref, o_ref,
                qa_ref, m_ref, acc_ref):
    qi = pl.program_id(2)
    tq = NSA_TQ
    s0 = qi * tq
    q = q_ref[0, 0, 0]

    def compressed(width):
        sc = _dot_nt(q, kc_ref[0, 0, 0:width, :])
        rq = lax.broadcasted_iota(jnp.int32, (sc.shape[0], LANES), 0) & (tq - 1)
        last = jnp.tile((s0 + rq - (CMP_BLOCK - 1)) >> (CMP_STRIDE.bit_length() - 1), (1, width // LANES))
        sc = jnp.where(lax.broadcasted_iota(jnp.int32, sc.shape, 1) <= last, sc, NEG)
        mx = jnp.max(sc, axis=-1, keepdims=True)
        e = jnp.exp2(sc - mx)
        r = jnp.where(mx > 0.5 * NEG, 1.0 / jnp.sum(e, axis=-1, keepdims=True), 0.0)
        imp = e[0:tq] * r[0:tq]
        for g in range(1, NSA_GROUP):
            imp = imp + e[g * tq:(g + 1) * tq] * r[g * tq:(g + 1) * tq]
        return _dot(e.astype(BF16), vc_ref[0, 0, 0:width, :]) * r, _dot_x2(imp, band_ref[0:width, :])

    n16 = kc_ref.shape[2]
    widths = list(range(CMP_WIDTH_STEP, n16, CMP_WIDTH_STEP)) + [n16]
    o_c, pslc = lax.switch((s0 + tq - 1) // (CMP_WIDTH_STEP * CMP_STRIDE),
                           [functools.partial(compressed, wd) for wd in widths])

    wlen = WINDOW + tq

    def window(kvw, masked):
        sw = masked(_dot_nt(q, kvw[:, 0:HEAD_DIM]))
        pw = jnp.exp2(sw - jnp.max(sw, axis=-1, keepdims=True))
        return _dot(pw.astype(BF16), kvw)[:, HEAD_DIM:2 * HEAD_DIM] * (1.0 / jnp.sum(pw, axis=-1, keepdims=True))

    def window_interior():
        def masked(sw):
            rq = lax.broadcasted_iota(jnp.int32, (sw.shape[0], tq), 0) & (tq - 1)
            c = lax.broadcasted_iota(jnp.int32, (sw.shape[0], tq), 1)
            return jnp.concatenate([jnp.where(c > rq, sw[:, 0:tq], NEG), sw[:, tq:WINDOW],
                                    jnp.where(c <= rq, sw[:, WINDOW:wlen], NEG)], axis=1)

        return window(kvw_ref[0, 0, pl.ds(pl.multiple_of(s0 - WINDOW, tq), wlen), :], masked)

    def window_start():
        def masked(sw):
            kabs = lax.broadcasted_iota(jnp.int32, sw.shape, 1)
            t = s0 + (lax.broadcasted_iota(jnp.int32, sw.shape, 0) & (tq - 1))
            return jnp.where((kabs <= t) & (kabs > t - WINDOW), sw, NEG)

        return window(kvw_ref[0, 0, 0:wlen, :], masked)

    pslc_t = pslc.T
    nb = pslc_t.shape[0]
    blk = lax.broadcasted_iota(jnp.int32, pslc_t.shape, 0)
    cur = (s0 + lax.broadcasted_iota(jnp.int32, pslc_t.shape, 1)) // SEL_BLOCK
    c0 = s0 // SEL_BLOCK
    elig = (blk >= 1) & (blk <= cur - 2)
    sel = _pick_top(jnp.where(elig, pslc_t, -jnp.inf), blk, nb, SEL_TOPK - 3)
    sel = sel | (blk == 0) | (blk == cur - 1)
    past = jnp.where(sel & (blk < c0), 0.0, NEG).T.astype(BF16)
    for st in range(n_super):
        b = past[:, st * SEL_LANES:(st + 1) * SEL_LANES]
        qa_ref[st] = jnp.concatenate([q, jnp.concatenate([b] * NSA_GROUP, axis=0)], axis=1)

    _softmax_init(m_ref, acc_ref)
    d0 = pl.multiple_of(s0, tq)
    s = _dot_nt(q, ka_ref[0, 0, pl.ds(d0, tq), :][:, 0:HEAD_DIM])
    qpos = lax.broadcasted_iota(jnp.int32, s.shape, 0) & (tq - 1)
    kpos = lax.broadcasted_iota(jnp.int32, s.shape, 1)
    _softmax_step(jnp.where(kpos <= qpos, s, NEG), va_ref[0, 0, pl.ds(d0, tq), :], m_ref, acc_ref)

    def step(start, size):
        s = _dot_nt(qa_ref[start // SUPER_KEYS], ka_ref[0, 0, pl.ds(start, size), :])
        _softmax_step(s, va_ref[0, 0, pl.ds(start, size), :], m_ref, acc_ref)

    (o_w,) = _past_keys_loop(s0, NSA_TK, step, parts=4, riders=(
        (window_interior, lambda: lax.cond(s0 >= WINDOW, window_interior, window_start)),))
    o_s = _softmax_result(acc_ref)

    w = NSA_GROUP * HEAD_DIM
    gexp = _dot_x2(gt_ref[...], e_ref[0])
    wide = lambda x: jnp.concatenate([x[g * tq:(g + 1) * tq] for g in range(NSA_GROUP)], axis=1)
    o_ref[...] = (gexp[:, 0:w] * wide(o_c) + gexp[:, w:2 * w] * wide(o_s) + gexp[:, 2 * w:3 * w] * wide(o_w)).astype(BF16)


def _nsa(qs, kcmp, vcmp, ka, va, kvw, gates, batch, seq):
    tq = NSA_TQ
    nq = seq // tq
    nb = seq // SEL_BLOCK
    n16 = seq // CMP_STRIDE
    assert seq % SUPER_KEYS == 0 and tq == 2 * SEL_BLOCK and WINDOW % tq == 0 and NSA_TK >= WINDOW
    n_super = seq // SUPER_KEYS
    rows = NSA_GROUP * tq
    e = np.zeros((NSA_KV_HEADS, GATE_LANES, 3 * NSA_GROUP * HEAD_DIM), np.float32)
    for br in range(3):
        for hk in range(NSA_KV_HEADS):
            for g in range(NSA_GROUP):
                c = (br * NSA_GROUP + g) * HEAD_DIM
                e[hk, br * NSA_HEADS + hk * NSA_GROUP + g, c:c + HEAD_DIM] = 1.0
    nn, jj = np.arange(n16)[:, None], np.arange(nb)[None, :]
    band = ((nn >= 4 * jj - 1) & (nn <= 4 * jj + 3)).astype(np.float32)
    resident = lambda width: pl.BlockSpec((1, 1, seq, width), lambda b, h, i: (b, h, 0, 0))
    w = NSA_GROUP * HEAD_DIM
    return pl.pallas_call(
        functools.partial(_nsa_kernel, n_super),
        grid=(batch, NSA_KV_HEADS, nq),
        in_specs=[pl.BlockSpec((1, 1, 1, rows, HEAD_DIM), lambda b, h, i: (b, h, i, 0, 0)),
                  pl.BlockSpec((1, 1, n16, HEAD_DIM), lambda b, h, i: (b, h, 0, 0)),
                  pl.BlockSpec((1, 1, n16, HEAD_DIM), lambda b, h, i: (b, h, 0, 0)),
                  resident(LANES), resident(LANES), resident(LANES),
                  pl.BlockSpec((tq, GATE_LANES), lambda b, h, i: (b * nq + i, 0)),
                  pl.BlockSpec((1, GATE_LANES, 3 * w), lambda b, h, i: (h, 0, 0)),
                  pl.BlockSpec((n16, nb), lambda b, h, i: (0, 0))],
        out_specs=pl.BlockSpec((tq, w), lambda b, h, i: (b * nq + i, h)),
        out_shape=jax.ShapeDtypeStruct((batch * seq, NSA_W), BF16),
        scratch_shapes=[pltpu.VMEM((n_super, rows, LANES), BF16), pltpu.VMEM((rows, LANES), F32),
                        pltpu.VMEM((rows, LANES), F32)],
        compiler_params=_params("arbitrary", "arbitrary", "arbitrary"),
        name="nsa",
    )(qs, kcmp, vcmp, ka, va, kvw, gates, jnp.asarray(e, BF16), jnp.asarray(band, BF16))


def _odd_out_kernel(om_ref, on_ref, h_ref, w_ref, mg_ref, wr_ref, br_ref, o_ref, gate_ref, grp_ref, cnt_ref):
    acc = h_ref[...] + _dot(on_ref[...], w_ref[MOBA_W:D_MODEL, :])
    for h in range(MOBA_HEADS):
        acc = acc + _dot(om_ref[0, h], w_ref[h * HEAD_DIM:(h + 1) * HEAD_DIM, :])
    o_ref[...] = acc
    _route(acc, mg_ref, wr_ref, br_ref, gate_ref, grp_ref, cnt_ref)


def _odd_out(o_moba, o_nsa, h2, w_out, router, seq, tm=ROW_TILE):
    t = h2.shape[0]
    tps = seq // tm
    row = lambda w: pl.BlockSpec((tm, w), lambda i: (i, 0))
    r_in, r_out, r_shape = _router_specs(t, tm)
    h, *routing = pl.pallas_call(
        _odd_out_kernel,
        grid=(t // tm,),
        in_specs=[pl.BlockSpec((1, MOBA_HEADS, tm, HEAD_DIM), lambda i: (i // tps, 0, i % tps, 0)),
                  row(NSA_W), row(D_MODEL), _full((D_MODEL, D_MODEL))] + r_in,
        out_specs=[row(D_MODEL)] + r_out,
        out_shape=[jax.ShapeDtypeStruct((t, D_MODEL), F32)] + r_shape,
        compiler_params=_params("arbitrary"),
        name="odd_out",
    )(o_moba, o_nsa, h2, w_out, *router)
    return h, routing


def _even_layer(h2, batch, seq, router, norm, w_in, w_out, lam_re, lam_im, log_dt, b_re, b_im, c_re, c_im, d, w_glu,
                conv_w, conv_b):
    u, ub, yb = _even_in(h2, norm.reshape(1, D_MODEL), w_in.astype(BF16), conv_w, conv_b.reshape(1, CONV_WIDTH), seq)
    ypre = _s5_mixer_pre(ub, batch, seq, lam_re, lam_im, log_dt, b_re, b_im, c_re, c_im)
    return _even_out(ypre, u, yb, h2, d.reshape(1, S5_WIDTH), w_glu.astype(BF16), w_out.astype(BF16), router)


def _odd_layer(h2, batch, seq, router, norm, w_in, w_out, moba_q_norm, moba_k_norm, nsa_q_norm, nsa_kcmp_norm, nsa_ksel_norm,
               nsa_kwin_norm, cmp_pe_k, cmp_w1_k, cmp_w2_k, cmp_pe_v, cmp_w1_v, cmp_w2_v):
    qm, kam, kmean, vam, qs, kvc, kas, vas, kvw, gates = _odd_in(
        h2, batch, seq, norm.reshape(1, D_MODEL), w_in, moba_q_norm, moba_k_norm, nsa_q_norm, nsa_ksel_norm, nsa_kwin_norm)
    cmp = _compress(kvc, batch, seq, cmp_pe_k, cmp_w1_k, cmp_w2_k, cmp_pe_v, cmp_w1_v, cmp_w2_v, nsa_kcmp_norm)
    o_moba = _moba(qm, kam, kmean, vam, batch, seq)
    o_nsa = _nsa(qs, cmp[0], cmp[1], kas, vas, kvw, gates, batch, seq)
    return _odd_out(o_moba, o_nsa, h2, w_out.astype(BF16), router, seq)


def kernel(x, ev_norm_mix, ev_w_in, ev_w_out, s5_lam_re, s5_lam_im, s5_log_dt, s5_b_re, s5_b_im, s5_c_re, s5_c_im, s5_d, s5_w_glu, conv_w, conv_b, od_norm_mix, od_w_in, od_w_out, moba_q_norm, moba_k_norm, nsa_q_norm, nsa_kcmp_norm, nsa_ksel_norm, nsa_kwin_norm, cmp_pe_k, cmp_w1_k, cmp_w2_k, cmp_pe_v, cmp_w1_v, cmp_w2_v, moe_norm, moe_w_group, moe_b_group, moe_w_expert, moe_b_expert, moe_w_gate, moe_w_up, moe_w_down):
    batch, seq, _ = x.shape
    depth = moe_norm.shape[0]
    h = x.reshape(batch * seq, D_MODEL)
    for layer in range(depth):
        i = layer // 2
        router = _router_operands(moe_norm[layer], moe_w_group[layer], moe_b_group[layer], moe_w_expert[layer],
                                  moe_b_expert[layer])
        if layer % 2 == 0:
            h, routing = _even_layer(h, batch, seq, router, ev_norm_mix[i], ev_w_in[i], ev_w_out[i], s5_lam_re[i],
                                     s5_lam_im[i], s5_log_dt[i], s5_b_re[i], s5_b_im[i], s5_c_re[i], s5_c_im[i], s5_d[i],
                                     s5_w_glu[i], conv_w[i], conv_b[i])
        else:
            h, routing = _odd_layer(h, batch, seq, router, od_norm_mix[i], od_w_in[i], od_w_out[i], moba_q_norm[i],
                                    moba_k_norm[i], nsa_q_norm[i], nsa_kcmp_norm[i], nsa_ksel_norm[i], nsa_kwin_norm[i],
                                    cmp_pe_k[i], cmp_w1_k[i], cmp_w2_k[i], cmp_pe_v[i], cmp_w1_v[i], cmp_w2_v[i])
        h = _moe(h, moe_norm[layer].reshape(1, D_MODEL), routing, moe_w_gate[layer], moe_w_up[layer], moe_w_down[layer])
    return h.reshape(batch, seq, D_MODEL)
```

```python
import functools
import math

import jax
import jax.numpy as jnp
import numpy as np
from jax import lax
from jax.experimental import pallas as pl
from jax.experimental.pallas import tpu as pltpu

D_MODEL = 1024
HEAD_DIM = 64
EPS = 1e-6
S5_WIDTH = 256
S5_GROUP = 16
S5_GROUPS = 16
S5_STATE = 64
S5_CHUNK = 16
S5_GROUP_BITS = 4
S5_STATE_BITS = 6
S5_ROW_TILE = 512
S5_COL_TILE = 512
S5_SCAN_TILE = 256
CONV_WIDTH = 768
CONV_K = 3
MOBA_HEADS = 4
NSA_HEADS = 12
NSA_KV_HEADS = 2
NSA_GROUP = 6
MOBA_W = 256
NSA_W = 768
KV_W = 128
MOBA_BLOCK = 256
MOBA_TOPK = 3
CMP_BLOCK = 32
CMP_STRIDE = 16
CMP_HIDDEN = 256
SEL_BLOCK = 64
SEL_TOPK = 8
WINDOW = 512
N_GROUPS = 4
EXPERTS_PER_GROUP = 4
N_EXPERTS = 16
EXPERT_FF = 256

LANES = 128
SUBLANES = 8
VMEM_LIMIT_BYTES = 56 * 1024 * 1024
ROW_TILE = 1024
GATE_LANES = LANES
NEG = -float(2 ** 100)
Q_SCALE = HEAD_DIM ** -0.5 * math.log2(math.e)
F32 = jnp.float32
BF16 = jnp.bfloat16


def _params(*semantics):
    return pltpu.CompilerParams(dimension_semantics=semantics, vmem_limit_bytes=VMEM_LIMIT_BYTES)


def _dot(a, b):
    return jnp.dot(a, b, preferred_element_type=F32)


def _dot_nt(a, b):
    return lax.dot_general(a, b, (((1,), (1,)), ((), ())), preferred_element_type=F32)


def _split(x):
    hi = x.astype(BF16)
    lo = (x - hi.astype(F32)).astype(BF16)
    return hi, lo


def _dot_x2(x, w):
    hi, lo = _split(x)
    return _dot(hi, w) + _dot(lo, w)


def _dot_x3(x, w_hilo):
    n = w_hilo.shape[1] // 2
    hi, lo = _split(x)
    both = _dot(hi, w_hilo)
    return both[:, 0:n] + (both[:, n:2 * n] + _dot(lo, w_hilo[:, 0:n]))


def _rms(x, gain):
    return x * lax.rsqrt(jnp.mean(x * x, axis=-1, keepdims=True) + EPS) * gain


def _gelu(x):
    return 0.5 * x * (1.0 + jnp.tanh(math.sqrt(2.0 / math.pi) * (x + 0.044715 * (x * x * x))))


def _sigmoid(x):
    return 1.0 / (1.0 + jnp.exp(-x))


def _full(shape):
    n = len(shape)
    return pl.BlockSpec(shape, lambda *_: (0,) * n)


def _even_in_kernel(tiles_per_seq, x_ref, g_ref, w_ref, cw_ref, cb_ref, u_ref, ub_ref, yb_ref, carry_ref):
    i = pl.program_id(0)
    xn = _rms(x_ref[...], g_ref[...]).astype(BF16)
    u = _dot(xn, w_ref[:, 0:S5_WIDTH])
    u_ref[...] = u
    ub_ref[...] = u.astype(BF16)
    o = S5_WIDTH
    xc = _dot(xn, w_ref[:, o:o + CONV_WIDTH])
    gb = _dot(xn, w_ref[:, o + CONV_WIDTH:o + 2 * CONV_WIDTH])
    gc = _dot(xn, w_ref[:, o + 2 * CONV_WIDTH:o + 3 * CONV_WIDTH])
    z = gc * xc
    tm = z.shape[0]

    @pl.when(i % tiles_per_seq == 0)
    def _():
        carry_ref[...] = jnp.zeros_like(carry_ref)

    row = lax.broadcasted_iota(jnp.int32, z.shape, 0)
    prev1 = carry_ref[SUBLANES - 1:SUBLANES, :]
    prev2 = carry_ref[SUBLANES - 2:SUBLANES - 1, :]
    z1 = jnp.where(row == 0, prev1, pltpu.roll(z, 1, 0))
    z2 = jnp.where(row == 0, prev2, jnp.where(row == 1, prev1, pltpu.roll(z, 2, 0)))
    y = cw_ref[0:1, :] * z2 + cw_ref[1:2, :] * z1 + cw_ref[2:3, :] * z + cb_ref[...]
    yb_ref[...] = (gb * y).astype(BF16)
    carry_ref[...] = z[tm - SUBLANES:tm, :]


def _even_in(x2, gain, w_in, conv_w, conv_b, seq, tm=ROW_TILE):
    t = x2.shape[0]
    n_in = w_in.shape[1]
    return pl.pallas_call(
        functools.partial(_even_in_kernel, seq // tm),
        grid=(t // tm,),
        in_specs=[pl.BlockSpec((tm, D_MODEL), lambda i: (i, 0)), _full((1, D_MODEL)),
                  _full((D_MODEL, n_in)), _full((CONV_K, CONV_WIDTH)), _full((1, CONV_WIDTH))],
        out_specs=[pl.BlockSpec((tm, S5_WIDTH), lambda i: (i, 0)), pl.BlockSpec((tm, S5_WIDTH), lambda i: (i, 0)),
                   pl.BlockSpec((tm, CONV_WIDTH), lambda i: (i, 0))],
        out_shape=[jax.ShapeDtypeStruct((t, S5_WIDTH), F32), jax.ShapeDtypeStruct((t, S5_WIDTH), BF16),
                   jax.ShapeDtypeStruct((t, CONV_WIDTH), BF16)],
        scratch_shapes=[pltpu.VMEM((SUBLANES, CONV_WIDTH), F32)],
        compiler_params=_params("arbitrary"),
        name="even_in",
    )(x2, gain, w_in, conv_w, conv_b)


def _s5_weights(lam_re, lam_im, log_dt, b_re, b_im, c_re, c_im):
    g, p, hg, ck = S5_GROUPS, S5_STATE, S5_GROUP, S5_CHUNK
    lr, li = lam_re.astype(F32), lam_im.astype(F32)
    dt = jnp.exp(log_dt.astype(F32))[:, None]
    mag = jnp.exp(lr * dt)
    a_re, a_im = mag * jnp.cos(li * dt), mag * jnp.sin(li * dt)
    den = lr * lr + li * li
    f_re = ((a_re - 1.0) * lr + a_im * li) / den
    f_im = (a_im * lr - (a_re - 1.0) * li) / den
    br, bi = b_re.astype(F32), b_im.astype(F32)
    bb_re = f_re[..., None] * br - f_im[..., None] * bi
    bb_im = f_re[..., None] * bi + f_im[..., None] * br
    pw_re, pw_im = [jnp.ones_like(a_re)], [jnp.zeros_like(a_im)]
    for _ in range(ck):
        r, m = pw_re[-1], pw_im[-1]
        pw_re.append(r * a_re - m * a_im)
        pw_im.append(r * a_im + m * a_re)
    pw_re, pw_im = jnp.stack(pw_re), jnp.stack(pw_im)
    cr, ci = c_re.astype(F32), c_im.astype(F32)
    rev_re, rev_im = pw_re[ck - 1::-1][:ck], pw_im[ck - 1::-1][:ck]
    ws_re = rev_re[:, :, :, None] * bb_re[None] - rev_im[:, :, :, None] * bb_im[None]
    ws_im = rev_re[:, :, :, None] * bb_im[None] + rev_im[:, :, :, None] * bb_re[None]
    ca_re = cr[None] * pw_re[1:, :, None, :] - ci[None] * pw_im[1:, :, None, :]
    ca_im = cr[None] * pw_im[1:, :, None, :] + ci[None] * pw_re[1:, :, None, :]
    cb_re = jnp.einsum('ghp,kgp,gpj->kghj', cr, pw_re[:ck], bb_re) - jnp.einsum('ghp,kgp,gpj->kghj', cr, pw_im[:ck], bb_im) \
        - jnp.einsum('ghp,kgp,gpj->kghj', ci, pw_re[:ck], bb_im) - jnp.einsum('ghp,kgp,gpj->kghj', ci, pw_im[:ck], bb_re)
    lag = np.arange(ck)[None, :] - np.arange(ck)[:, None]
    tz = cb_re[np.clip(lag, 0, ck - 1)]
    tz = jnp.where((lag >= 0)[:, :, None, None, None], tz, 0.0)
    cw = ck * g * hg
    ws = jnp.stack([ws_re, ws_im]).transpose(1, 2, 4, 0, 3).reshape(cw, 2 * p)
    wc = jnp.stack([ca_re, -ca_im]).transpose(0, 2, 4, 1, 3).reshape(2 * g * p, ck * hg)
    tzc = tz.transpose(0, 2, 4, 1, 3).reshape(cw, ck * hg)
    return ws.astype(BF16), wc.astype(BF16), tzc.astype(BF16), pw_re[ck].reshape(1, g * p), pw_im[ck].reshape(1, g * p)


def _group_expand(compact, expand, row_shift, col_shift, col0):
    full = _dot(compact, expand)
    row = lax.broadcasted_iota(jnp.int32, full.shape, 0)
    col = col0 + lax.broadcasted_iota(jnp.int32, full.shape, 1)
    same = ((row >> row_shift) & (S5_GROUPS - 1)) == ((col >> col_shift) & (S5_GROUPS - 1))
    return jnp.where(same, full, 0.0).astype(BF16)


def _s5_state_kernel(u_ref, ws_ref, e_ref, s_ref, w_scr):
    @pl.when(pl.program_id(1) == 0)
    def _():
        w_scr[...] = _group_expand(ws_ref[...], e_ref[...], S5_GROUP_BITS, S5_STATE_BITS,
                                   pl.program_id(0) * w_scr.shape[1])

    s_ref[...] = _dot(u_ref[...], w_scr[...])


def _s5_scan_kernel(s_ref, are_ref, aim_ref, xprev_ref, st_ref):
    @pl.when(pl.program_id(0) == 0)
    def _():
        st_ref[...] = jnp.zeros_like(st_ref)

    a_re, a_im = are_ref[...], aim_ref[...]
    nb, n = s_ref.shape[0], s_ref.shape[1]
    half = a_re.shape[1]

    def body(c, carry):
        out = []
        for b in range(nb):
            xr, xi = carry[2 * b], carry[2 * b + 1]
            xprev_ref[b, pl.ds(c, 1), 0:half] = xr
            xprev_ref[b, pl.ds(c, 1), half:2 * half] = xi
            s = s_ref[b, pl.ds(c, 1), :]
            out += [a_re * xr - a_im * xi + s[:, 0:half], a_re * xi + a_im * xr + s[:, half:2 * half]]
        return tuple(out)

    init = tuple(st_ref[b:b + 1, o:o + half] for b in range(nb) for o in (0, half))
    final = lax.fori_loop(0, n, body, init, unroll=4)
    for b in range(nb):
        st_ref[b:b + 1, 0:half] = final[2 * b]
        st_ref[b:b + 1, half:2 * half] = final[2 * b + 1]


def _s5_out_kernel(u_ref, xp_ref, tz_ref, wc_ref, e_ref, y_ref, tz_scr, wc_scr):
    @pl.when(pl.program_id(1) == 0)
    def _():
        col0 = pl.program_id(0) * tz_scr.shape[1]
        tz_scr[...] = _group_expand(tz_ref[...], e_ref[...], S5_GROUP_BITS, S5_GROUP_BITS, col0)
        wc_scr[...] = _group_expand(wc_ref[...], e_ref[...], S5_STATE_BITS, S5_GROUP_BITS, col0)

    y_ref[...] = _dot(u_ref[...], tz_scr[...]) + _dot(xp_ref[...].astype(BF16), wc_scr[...])


def _expand_matrix(outer, inner):
    e = np.zeros((outer, inner, outer, S5_GROUPS, inner), np.float32)
    for x in range(outer):
        for y in range(inner):
            e[x, y, x, :, y] = 1.0
    return jnp.asarray(e.reshape(outer * inner, outer * S5_GROUPS * inner), BF16)


def _s5_mixer_pre(ub, batch, seq, lam_re, lam_im, log_dt, b_re, b_im, c_re, c_im):
    ws, wc, tz, a16_re, a16_im = _s5_weights(lam_re, lam_im, log_dt, b_re, b_im, c_re, c_im)
    nc = seq // S5_CHUNK
    rows = batch * nc
    cw = S5_CHUNK * S5_WIDTH
    sw = 2 * S5_GROUPS * S5_STATE
    assert S5_GROUP == 1 << S5_GROUP_BITS and S5_STATE == 1 << S5_STATE_BITS
    tr = min(rows, S5_ROW_TILE)
    tn = S5_COL_TILE
    uc = ub.reshape(rows, cw)
    e_state = _expand_matrix(2, S5_STATE)
    e_out = _expand_matrix(S5_CHUNK, S5_GROUP)
    s = pl.pallas_call(
        _s5_state_kernel,
        grid=(sw // tn, rows // tr),
        in_specs=[pl.BlockSpec((tr, cw), lambda j, i: (i, 0)), _full(ws.shape),
                  pl.BlockSpec((e_state.shape[0], tn), lambda j, i: (0, j))],
        out_specs=pl.BlockSpec((tr, tn), lambda j, i: (i, j)),
        out_shape=jax.ShapeDtypeStruct((rows, sw), F32),
        scratch_shapes=[pltpu.VMEM((cw, tn), BF16)],
        compiler_params=_params("arbitrary", "arbitrary"),
        name="s5_state",
    )(uc, ws, e_state)
    tc = min(nc, S5_SCAN_TILE)
    xprev = pl.pallas_call(
        _s5_scan_kernel,
        grid=(nc // tc,),
        in_specs=[pl.BlockSpec((batch, tc, sw), lambda i: (0, i, 0)), _full((1, sw // 2)), _full((1, sw // 2))],
        out_specs=pl.BlockSpec((batch, tc, sw), lambda i: (0, i, 0)),
        out_shape=jax.ShapeDtypeStruct((batch, nc, sw), F32),
        scratch_shapes=[pltpu.VMEM((batch, sw), F32)],
        compiler_params=_params("arbitrary"),
        name="s5_scan",
    )(s.reshape(batch, nc, sw), a16_re, a16_im)
    y = pl.pallas_call(
        _s5_out_kernel,
        grid=(cw // tn, rows // tr),
        in_specs=[pl.BlockSpec((tr, cw), lambda j, i: (i, 0)), pl.BlockSpec((tr, sw), lambda j, i: (i, 0)),
                  _full(tz.shape), _full(wc.shape), pl.BlockSpec((e_out.shape[0], tn), lambda j, i: (0, j))],
        out_specs=pl.BlockSpec((tr, tn), lambda j, i: (i, j)),
        out_shape=jax.ShapeDtypeStruct((rows, cw), F32),
        scratch_shapes=[pltpu.VMEM((cw, tn), BF16), pltpu.VMEM((sw, tn), BF16)],
        compiler_params=_params("arbitrary", "arbitrary"),
        name="s5_out",
    )(uc, xprev.reshape(rows, sw), tz, wc, e_out)
    return y.reshape(batch * seq, S5_WIDTH)


def _even_out_kernel(ypre_ref, u_ref, yb_ref, x_ref, d_ref, wglu_ref, wout_ref, mg_ref, wr_ref, br_ref,
                     o_ref, gate_ref, grp_ref, cnt_ref):
    y = _gelu(ypre_ref[...] + d_ref[...] * u_ref[...])
    y = y * _sigmoid(_dot(y.astype(BF16), wglu_ref[...]))
    h = (x_ref[...] + _dot(y.astype(BF16), wout_ref[0:S5_WIDTH, :]) + _dot(yb_ref[...], wout_ref[S5_WIDTH:D_MODEL, :]))
    o_ref[...] = h
    _route(h, mg_ref, wr_ref, br_ref, gate_ref, grp_ref, cnt_ref)


def _even_out(ypre, u, yb, x2, d, w_glu, w_out, router, tm=ROW_TILE):
    t = x2.shape[0]
    row = lambda w: pl.BlockSpec((tm, w), lambda i: (i, 0))
    r_in, r_out, r_shape = _router_specs(t, tm)
    h, *routing = pl.pallas_call(
        _even_out_kernel,
        grid=(t // tm,),
        in_specs=[row(S5_WIDTH), row(S5_WIDTH), row(CONV_WIDTH), row(D_MODEL), _full((1, S5_WIDTH)),
                  _full((S5_WIDTH, S5_WIDTH)), _full((D_MODEL, D_MODEL))] + r_in,
        out_specs=[row(D_MODEL)] + r_out,
        out_shape=[jax.ShapeDtypeStruct((t, D_MODEL), F32)] + r_shape,
        compiler_params=_params("arbitrary"),
        name="even_out",
    )(ypre, u, yb, x2, d, w_glu, w_out, *router)
    return h, routing


def _first_max(v, pos, width, axis=-1):
    m = jnp.max(v, axis=axis, keepdims=True)
    idx = jnp.min(jnp.where(v == m, pos, width), axis=axis, keepdims=True)
    return m, idx


MOE_TM = 1024
MOE_ALIGN = 16
MOE_SLOTS = 1152
MOE_WIN = 320
ROUTER_LANES = LANES


def _route(h, g_ref, wr_ref, br_ref, gate_ref, grp_ref, cnt_ref):
    xn = _rms(h, g_ref[...])
    logits = _dot_x3(xn, wr_ref[...]) + br_ref[...]
    lane = lax.broadcasted_iota(jnp.int32, logits.shape, 1).astype(F32)
    width = float(logits.shape[1])
    is_g = lane < N_GROUPS
    gl = jnp.where(is_g, logits, -jnp.inf)
    gm, gi = _first_max(gl, lane, width)
    gw = 1.0 / jnp.sum(jnp.where(is_g, jnp.exp(gl - gm), 0.0), axis=-1, keepdims=True)
    lo = N_GROUPS + gi * EXPERTS_PER_GROUP
    in_grp = (lane >= lo) & (lane < lo + EXPERTS_PER_GROUP)
    el = jnp.where(in_grp, logits, -jnp.inf)
    m1, i1 = _first_max(el, lane, width)
    m2, i2 = _first_max(jnp.where(lane == i1, -jnp.inf, el), lane, width)
    p2 = jnp.exp(m2 - m1)
    w1 = gw / (1.0 + p2)
    w2 = gw * p2 / (1.0 + p2)
    gate_ref[...] = jnp.where(lane == i1, w1, 0.0) + jnp.where(lane == i2, w2, 0.0)
    grp = (lane == gi).astype(F32)
    grp_ref[...] = grp.astype(BF16)
    cnt_ref[0] = jnp.broadcast_to(jnp.sum(grp, axis=0, keepdims=True), cnt_ref.shape[1:])


def _moe_expert_kernel(base_ref, nwin_ref, h_ref, g_ref, gate_ref, grp_ref, wg_ref, wu_ref, wd_ref, o_ref,
                       xs_ref, gs_ref, ys_ref, pt_ref):
    i, g = pl.program_id(0), pl.program_id(1)
    tm = h_ref.shape[0]

    @pl.when((i == 0) & (g == 0))
    def _():
        xs_ref[...] = jnp.zeros_like(xs_ref)
        gs_ref[...] = jnp.zeros_like(gs_ref)
        ys_ref[...] = jnp.zeros_like(ys_ref)

    @pl.when(g == 0)
    def _():
        xn = _rms(h_ref[...], g_ref[...]).astype(BF16)
        grp = grp_ref[...]
        earlier = (lax.broadcasted_iota(jnp.int32, (tm, tm), 0) > lax.broadcasted_iota(jnp.int32, (tm, tm), 1)).astype(BF16)
        rank = _dot(earlier, grp)
        lane = lax.broadcasted_iota(jnp.int32, rank.shape, 1)
        for k in range(N_GROUPS):
            rank = rank + jnp.where(lane == k, base_ref[i * N_GROUPS + k].astype(F32), 0.0)
        slot = jnp.sum(grp.astype(F32) * rank, axis=-1, keepdims=True).astype(jnp.int32)
        pt = (lax.broadcasted_iota(jnp.int32, (tm, MOE_SLOTS), 1) == slot).astype(BF16)
        pt_ref[...] = pt
        gather = lambda x: lax.dot_general(pt, x, (((0,), (0,)), ((), ())), preferred_element_type=F32)
        xs_ref[0:MOE_SLOTS, :] = gather(xn).astype(BF16)
        both = gather(jnp.concatenate(_split(gate_ref[...]), axis=1))
        gs_ref[0:MOE_SLOTS, :] = both[:, 0:ROUTER_LANES] + both[:, ROUTER_LANES:2 * ROUTER_LANES]

    def window(w, carry):
        r0 = pl.multiple_of(base_ref[i * N_GROUPS + g] + w * MOE_WIN, MOE_ALIGN)
        x = xs_ref[pl.ds(r0, MOE_WIN), :]
        gate = gs_ref[pl.ds(r0, MOE_WIN), :]
        lane = lax.broadcasted_iota(jnp.int32, gate.shape, 1)
        y = None
        for j in range(EXPERTS_PER_GROUP):
            ge = jnp.sum(jnp.where(lane == g * EXPERTS_PER_GROUP + (j + N_GROUPS), gate, 0.0), axis=-1, keepdims=True)
            h1 = _dot(x, wg_ref[j])
            h3 = _dot(x, wu_ref[j])
            act = (h1 * _sigmoid(h1)) * h3 * ge
            yj = _dot(act.astype(BF16), wd_ref[j])
            y = yj if y is None else y + yj
        ys_ref[pl.ds(r0, MOE_WIN), :] = y.astype(BF16)
        return carry

    lax.fori_loop(0, nwin_ref[i * N_GROUPS + g], window, 0)

    @pl.when(g == N_GROUPS - 1)
    def _():
        o_ref[...] = h_ref[...] + _dot(pt_ref[...], ys_ref[0:MOE_SLOTS, :])


def _router_operands(moe_gain, w_group, b_group, w_expert, b_expert):
    rw = ROUTER_LANES
    wr = jnp.zeros((D_MODEL, rw), F32).at[:, 0:N_GROUPS].set(w_group).at[:, N_GROUPS:N_GROUPS + N_EXPERTS].set(w_expert)
    br = jnp.zeros((1, rw), F32).at[0, 0:N_GROUPS].set(b_group).at[0, N_GROUPS:N_GROUPS + N_EXPERTS].set(b_expert)
    return moe_gain.reshape(1, D_MODEL), jnp.concatenate(_split(wr), axis=1), br


def _router_specs(t, tm):
    assert tm == MOE_TM and t % tm == 0
    rw = ROUTER_LANES
    row = lambda width: pl.BlockSpec((tm, width), lambda i: (i, 0))
    in_specs = [_full((1, D_MODEL)), _full((D_MODEL, 2 * rw)), _full((1, rw))]
    out_specs = [row(rw), row(rw), pl.BlockSpec((1, SUBLANES, rw), lambda i: (i, 0, 0))]
    out_shape = [jax.ShapeDtypeStruct((t, rw), F32), jax.ShapeDtypeStruct((t, rw), BF16),
                 jax.ShapeDtypeStruct((t // tm, SUBLANES, rw), F32)]
    return in_specs, out_specs, out_shape


def _moe(h2, gain, routing, w_gate, w_up, w_down):
    t = h2.shape[0]
    tm, rw = MOE_TM, ROUTER_LANES
    assert t % tm == 0 and MOE_SLOTS >= tm + N_GROUPS * (MOE_ALIGN - 1) and MOE_WIN % MOE_ALIGN == 0
    n_tiles = t // tm
    gates, grp, cnt = routing
    n = cnt[:, 0, 0:N_GROUPS].astype(jnp.int32)
    padded = (n + (MOE_ALIGN - 1)) // MOE_ALIGN * MOE_ALIGN
    base = (jnp.cumsum(padded, axis=1) - padded).reshape(-1)
    nwin = ((padded + (MOE_WIN - 1)) // MOE_WIN).reshape(-1)
    tile = lambda width: pl.BlockSpec((tm, width), lambda i, g, *_: (i, 0))
    experts = lambda shape: pl.BlockSpec((EXPERTS_PER_GROUP,) + shape, lambda i, g, *_: (g, 0, 0))
    slots = MOE_SLOTS + MOE_WIN
    return pl.pallas_call(
        _moe_expert_kernel,
        grid_spec=pltpu.PrefetchScalarGridSpec(
            num_scalar_prefetch=2,
            grid=(n_tiles, N_GROUPS),
            in_specs=[tile(D_MODEL), pl.BlockSpec((1, D_MODEL), lambda i, g, *_: (0, 0)), tile(rw), tile(rw),
                      experts((D_MODEL, EXPERT_FF)), experts((D_MODEL, EXPERT_FF)), experts((EXPERT_FF, D_MODEL))],
            out_specs=tile(D_MODEL),
            scratch_shapes=[pltpu.VMEM((slots, D_MODEL), BF16), pltpu.VMEM((slots, rw), F32),
                            pltpu.VMEM((slots, D_MODEL), BF16), pltpu.VMEM((tm, MOE_SLOTS), BF16)]),
        out_shape=jax.ShapeDtypeStruct((t, D_MODEL), F32),
        compiler_params=_params("arbitrary", "arbitrary"),
        name="moe",
    )(base, nwin, h2, gain, gates, grp, w_gate.astype(BF16), w_up.astype(BF16), w_down.astype(BF16))


ODD_SPLITS = (MOBA_W, MOBA_W, MOBA_W, NSA_W, KV_W, KV_W, KV_W, KV_W, KV_W, KV_W, GATE_LANES)
ODD_IN_PAD = sum(ODD_SPLITS)


def _head_rms(x, hsum, gain):
    w = x.shape[1]
    ss = jnp.concatenate([_dot_x2(x[:, o:o + hsum.shape[0]] * x[:, o:o + hsum.shape[0]], hsum)
                          for o in range(0, w, hsum.shape[0])], axis=1) if w > hsum.shape[0] else _dot_x2(x * x, hsum)
    return x * lax.rsqrt(ss * (1.0 / HEAD_DIM) + EPS) * gain


def _odd_in_kernel(tiles_per_seq, x_ref, g_ref, w_ref, hsum_ref, gq_ref, gk_ref, gnq_ref, gks_ref, gkw_ref,
                   qm_ref, kam_ref, kmean_ref, vam_ref, qs_ref, kvc_ref, kas_ref, vas_ref, kvw_ref, gt_ref):
    xn = _rms(x_ref[...], g_ref[...]).astype(BF16)
    offs = np.cumsum((0,) + ODD_SPLITS)
    col = lambda j: _dot(xn, w_ref[:, int(offs[j]):int(offs[j + 1])])
    head = lambda x, h: x[:, h * HEAD_DIM:(h + 1) * HEAD_DIM]
    hsum = hsum_ref[...]
    hsum128 = hsum_ref[0:KV_W, 0:KV_W]
    tm = x_ref.shape[0]
    pos = (pl.program_id(0) % tiles_per_seq) * tm + lax.broadcasted_iota(jnp.int32, (tm, HEAD_DIM), 0)
    lane = lax.broadcasted_iota(jnp.int32, (tm, HEAD_DIM), 1)
    ones_col = (lane == 0).astype(BF16)

    qm = _head_rms(col(0), hsum, gq_ref[...])
    km = _head_rms(col(1), hsum, gk_ref[...])
    for j in range(tm // MOBA_BLOCK):
        kmean_ref[0, j:j + 1, :] = jnp.mean(km[j * MOBA_BLOCK:(j + 1) * MOBA_BLOCK, :], axis=0, keepdims=True)
    km = km.astype(BF16)
    vm = col(2).astype(BF16)
    moba_id = (lane == pos // MOBA_BLOCK).astype(BF16)
    for h in range(MOBA_HEADS):
        qm_ref[0, h] = head(qm, h)
        kam_ref[0, h] = jnp.concatenate([head(km, h), moba_id], axis=1)
        vam_ref[0, h] = jnp.concatenate([head(vm, h), ones_col], axis=1)

    qd = (_head_rms(col(3), hsum, gnq_ref[...]) * Q_SCALE).astype(BF16)
    for hk in range(NSA_KV_HEADS):
        for j in range(tm // NSA_TQ):
            for g in range(NSA_GROUP):
                qs_ref[0, hk, j, g * NSA_TQ:(g + 1) * NSA_TQ, :] = head(qd, hk * NSA_GROUP + g)[j * NSA_TQ:(j + 1) * NSA_TQ, :]
    kvc_ref[0] = col(4)
    kvc_ref[1] = col(5)
    ks = _head_rms(col(6), hsum128, gks_ref[...]).astype(BF16)
    vs = col(7).astype(BF16)
    kw = _head_rms(col(8), hsum128, gkw_ref[...]).astype(BF16)
    vw = col(9).astype(BF16)
    sel_id = (lane == (pos // SEL_BLOCK) % SEL_LANES).astype(BF16)
    for hk in range(NSA_KV_HEADS):
        kas_ref[0, hk] = jnp.concatenate([head(ks, hk), sel_id], axis=1)
        vas_ref[0, hk] = jnp.concatenate([head(vs, hk), ones_col], axis=1)
        kvw_ref[0, hk] = jnp.concatenate([head(kw, hk), head(vw, hk)], axis=1)
    gt_ref[...] = _sigmoid(col(10))


def _odd_in(h2, batch, seq, gain, w_in, moba_q_norm, moba_k_norm, nsa_q_norm, nsa_ksel_norm, nsa_kwin_norm, tm=ROW_TILE):
    t = h2.shape[0]
    assert MOBA_LANES == HEAD_DIM and SEL_LANES == HEAD_DIM and seq % tm == 0 and tm % MOBA_BLOCK == 0
    tps = seq // tm
    w = jnp.pad(w_in, ((0, 0), (0, ODD_IN_PAD - w_in.shape[1]))).astype(BF16)
    hsum = jnp.asarray(np.kron(np.eye(MOBA_W // HEAD_DIM), np.ones((HEAD_DIM, HEAD_DIM))), BF16)
    tile = lambda g, width: jnp.tile(g.astype(F32), width // HEAD_DIM).reshape(1, width)
    row = lambda width: pl.BlockSpec((tm, width), lambda i: (i, 0))
    heads = lambda n, width: pl.BlockSpec((1, n, tm, width), lambda i: (i // tps, 0, i % tps, 0))
    nmb = tm // MOBA_BLOCK
    nqt = tm // NSA_TQ
    rows = NSA_GROUP * NSA_TQ
    sds = jax.ShapeDtypeStruct
    out_specs = [heads(MOBA_HEADS, HEAD_DIM), heads(MOBA_HEADS, LANES), pl.BlockSpec((1, nmb, MOBA_W), lambda i: (i, 0, 0)),
                 heads(MOBA_HEADS, LANES),
                 pl.BlockSpec((1, NSA_KV_HEADS, nqt, rows, HEAD_DIM), lambda i: (i // tps, 0, i % tps, 0, 0)),
                 pl.BlockSpec((2, tm, KV_W), lambda i: (0, i, 0)),
                 heads(NSA_KV_HEADS, LANES), heads(NSA_KV_HEADS, LANES), heads(NSA_KV_HEADS, LANES),
                 row(GATE_LANES)]
    out_shape = [sds((batch, MOBA_HEADS, seq, HEAD_DIM), F32), sds((batch, MOBA_HEADS, seq, LANES), BF16),
                 sds((t // tm, nmb, MOBA_W), F32), sds((batch, MOBA_HEADS, seq, LANES), BF16),
                 sds((batch, NSA_KV_HEADS, seq // NSA_TQ, rows, HEAD_DIM), BF16),
                 sds((2, t, KV_W), F32), sds((batch, NSA_KV_HEADS, seq, LANES), BF16),
                 sds((batch, NSA_KV_HEADS, seq, LANES), BF16), sds((batch, NSA_KV_HEADS, seq, LANES), BF16),
                 sds((t, GATE_LANES), F32)]
    return pl.pallas_call(
        functools.partial(_odd_in_kernel, tps),
        grid=(t // tm,),
        in_specs=[row(D_MODEL), _full((1, D_MODEL)), _full((D_MODEL, ODD_IN_PAD)), _full((MOBA_W, MOBA_W)),
                  _full((1, MOBA_W)), _full((1, MOBA_W)), _full((1, NSA_W)), _full((1, KV_W)), _full((1, KV_W))],
        out_specs=out_specs,
        out_shape=out_shape,
        compiler_params=_params("arbitrary"),
        name="odd_in",
    )(h2, gain, w, hsum, tile(moba_q_norm, MOBA_W), tile(moba_k_norm, MOBA_W), tile(nsa_q_norm, NSA_W),
      tile(nsa_ksel_norm, KV_W), tile(nsa_kwin_norm, KV_W))


def _compress_kernel(c_ref, w1_ref, w2_ref, pe_ref, g_ref, o_ref):
    kind = pl.program_id(0)
    n16 = c_ref.shape[1] // CMP_STRIDE
    half = CMP_STRIDE * HEAD_DIM
    peb = _dot(pe_ref[0], w1_ref[0])[0:1, :]
    xs = [c_ref[0, pl.ds(s, n16, stride=CMP_STRIDE), :].astype(BF16) for s in range(CMP_STRIDE)]
    for h in range(NSA_KV_HEADS):
        first = second = None
        for s in range(CMP_STRIDE):
            x = xs[s][:, h * HEAD_DIM:(h + 1) * HEAD_DIM]
            a = _dot(x, w1_ref[0, s * HEAD_DIM:(s + 1) * HEAD_DIM, :])
            b = _dot(x, w1_ref[0, half + s * HEAD_DIM:half + (s + 1) * HEAD_DIM, :])
            first = a if first is None else first + a
            second = b if second is None else second + b
        hid = _gelu(first + pltpu.roll(second, n16 - 1, 0) + peb)
        out = _dot(hid.astype(BF16), w2_ref[0])
        o_ref[0, 0, h] = jnp.where(kind == 0, _rms(out, g_ref[...]), out).astype(BF16)


def _compress(kvc, batch, seq, pe_k, w1_k, w2_k, pe_v, w1_v, w2_v, kcmp_norm):
    n16 = seq // CMP_STRIDE
    half = CMP_STRIDE * HEAD_DIM
    w1 = jnp.stack([w1_k, w1_v]).astype(BF16)
    w2 = jnp.stack([w2_k, w2_v]).astype(BF16)
    pe = jnp.stack([pe_k, pe_v]).reshape(2, 1, 2 * half)
    pe = jnp.broadcast_to(pe, (2, SUBLANES, 2 * half)).astype(BF16)
    return pl.pallas_call(
        _compress_kernel,
        grid=(2, batch),
        in_specs=[pl.BlockSpec((1, seq, KV_W), lambda k, b: (k, b, 0)),
                  pl.BlockSpec((1, 2 * half, CMP_HIDDEN), lambda k, b: (k, 0, 0)),
                  pl.BlockSpec((1, CMP_HIDDEN, HEAD_DIM), lambda k, b: (k, 0, 0)),
                  pl.BlockSpec((1, SUBLANES, 2 * half), lambda k, b: (k, 0, 0)),
                  _full((1, HEAD_DIM))],
        out_specs=pl.BlockSpec((1, 1, NSA_KV_HEADS, n16, HEAD_DIM), lambda k, b: (k, b, 0, 0, 0)),
        out_shape=jax.ShapeDtypeStruct((2, batch, NSA_KV_HEADS, n16, HEAD_DIM), BF16),
        compiler_params=_params("arbitrary", "arbitrary"),
        name="nsa_compress",
    )(kvc, w1, w2, pe, kcmp_norm.astype(F32).reshape(1, HEAD_DIM))


M_INIT = -1e30


def _softmax_init(m_ref, acc_ref):
    m_ref[...] = jnp.full(m_ref.shape, M_INIT, F32)
    acc_ref[...] = jnp.zeros(acc_ref.shape, F32)


def _softmax_step(s, v_aug, m_ref, acc_ref):
    m_old = m_ref[...]
    m_new = jnp.maximum(m_old, jnp.max(s, axis=-1, keepdims=True))
    alpha = jnp.exp2(m_old - m_new)
    p = jnp.exp2(s - jnp.tile(m_new, (1, s.shape[1] // LANES)))
    acc_ref[...] = alpha * acc_ref[...] + _dot(p.astype(BF16), v_aug)
    m_ref[...] = m_new


def _past_keys_loop(n_keys, tile, step, riders=(), parts=2):
    n_full = n_keys // tile

    def body(j, carry):
        step(pl.multiple_of(j * tile, tile), tile)
        return carry

    extras = []
    for k, (with_tile, alone) in enumerate(riders):
        def both(with_tile=with_tile, k=k):
            out = with_tile()
            step(k * tile, tile)
            return out

        extras.append(lax.cond(n_full > k, both, alone))
    lax.fori_loop(len(riders), n_full, body, 0)
    rest = n_keys - n_full * tile
    start = pl.multiple_of(n_full * tile, tile)
    part = tile // parts
    n_parts = (rest + part - 1) // part
    for k in range(1, parts + 1):
        @pl.when(n_parts == k)
        def _(k=k):
            done = 0
            for size in (tile, tile // 2, tile // 4):
                if size % part == 0 and k * part - done >= size:
                    step(pl.multiple_of(start + done, part), size)
                    done += size
            assert done == k * part

    return extras


def _softmax_result(acc_ref):
    acc = acc_ref[...]
    return acc[:, 0:HEAD_DIM] * (1.0 / acc[:, HEAD_DIM:HEAD_DIM + 1])


def _pick_top(score, pos, width, k):
    sel = jnp.zeros(score.shape, jnp.bool_)
    for _ in range(k):
        m, idx = _first_max(score, pos, width, axis=0)
        hit = (pos == idx) & (m > -jnp.inf)
        sel = sel | hit
        score = jnp.where(pos == idx, -jnp.inf, score)
    return sel


MOBA_LANES = 64
MOBA_TK = 2048


MOBA_TQ = 4 * MOBA_BLOCK


def _moba_kernel(q_ref, ka_ref, va_ref, kmean_ref, o_ref, qa_ref, m_ref, acc_ref):
    i0 = pl.program_id(2) * (MOBA_TQ // MOBA_BLOCK)
    q = q_ref[0, 0]
    q_hi, q_lo = _split(q)
    km_hi, km_lo = _split(kmean_ref[0, 0])
    gate = _dot_nt(km_hi, q_hi) + (_dot_nt(km_lo, q_hi) + _dot_nt(km_hi, q_lo))
    blk = lax.broadcasted_iota(jnp.int32, gate.shape, 0)
    cur = i0 + lax.broadcasted_iota(jnp.int32, gate.shape, 1) // MOBA_BLOCK
    sel = _pick_top(jnp.where(blk < cur, gate, -jnp.inf), blk, gate.shape[0], MOBA_TOPK)
    qs = (q * Q_SCALE).astype(BF16)
    past = jnp.where(sel & (blk < i0), 0.0, NEG).T[:, 0:MOBA_LANES]
    own = jnp.where((blk == cur) | (sel & (blk >= i0)), 0.0, NEG).T[:, 0:MOBA_LANES]
    qa_ref[0] = jnp.concatenate([qs, past.astype(BF16)], axis=1)
    qa_ref[1] = jnp.concatenate([qs, own.astype(BF16)], axis=1)
    _softmax_init(m_ref, acc_ref)

    def step(start, size):
        s = _dot_nt(qa_ref[0], ka_ref[0, 0, pl.ds(start, size), :])
        _softmax_step(s, va_ref[0, 0, pl.ds(start, size), :], m_ref, acc_ref)

    _past_keys_loop(i0 * MOBA_BLOCK, MOBA_TK, step)
    start = pl.multiple_of(i0 * MOBA_BLOCK, MOBA_TQ)
    s = _dot_nt(qa_ref[1], ka_ref[0, 0, pl.ds(start, MOBA_TQ), :])
    qpos = lax.broadcasted_iota(jnp.int32, s.shape, 0)
    kpos = lax.broadcasted_iota(jnp.int32, s.shape, 1)
    hidden = (qpos // MOBA_BLOCK == kpos // MOBA_BLOCK) & (kpos > qpos)
    _softmax_step(jnp.where(hidden, NEG, s), va_ref[0, 0, pl.ds(start, MOBA_TQ), :], m_ref, acc_ref)
    o_ref[0, 0] = _softmax_result(acc_ref).astype(BF16)


def _moba(qm, ka, kmean, va, batch, seq):
    nmb = seq // MOBA_BLOCK
    assert nmb <= MOBA_LANES and seq % MOBA_TK == 0 and seq % MOBA_TQ == 0
    kmean = kmean.reshape(batch, nmb, MOBA_HEADS, HEAD_DIM).transpose(0, 2, 1, 3)
    kmean = jnp.pad(kmean, ((0, 0), (0, 0), (0, LANES - nmb), (0, 0)))
    return pl.pallas_call(
        _moba_kernel,
        grid=(batch, MOBA_HEADS, seq // MOBA_TQ),
        in_specs=[pl.BlockSpec((1, 1, MOBA_TQ, HEAD_DIM), lambda b, h, i: (b, h, i, 0)),
                  pl.BlockSpec((1, 1, seq, LANES), lambda b, h, i: (b, h, 0, 0)),
                  pl.BlockSpec((1, 1, seq, LANES), lambda b, h, i: (b, h, 0, 0)),
                  pl.BlockSpec((1, 1, LANES, HEAD_DIM), lambda b, h, i: (b, h, 0, 0))],
        out_specs=pl.BlockSpec((1, 1, MOBA_TQ, HEAD_DIM), lambda b, h, i: (b, h, i, 0)),
        out_shape=jax.ShapeDtypeStruct((batch, MOBA_HEADS, seq, HEAD_DIM), BF16),
        scratch_shapes=[pltpu.VMEM((2, MOBA_TQ, LANES), BF16), pltpu.VMEM((MOBA_TQ, LANES), F32),
                        pltpu.VMEM((MOBA_TQ, LANES), F32)],
        compiler_params=_params("arbitrary", "arbitrary", "arbitrary"),
        name="moba",
    )(qm, ka, va, kmean)


NSA_TQ = 2 * SEL_BLOCK
NSA_TK = 2048
SEL_LANES = 64
SUPER_KEYS = SEL_LANES * SEL_BLOCK
CMP_WIDTH_STEP = 256


NSA_PER_STEP = 2


def _nsa_kernel(n_super, q_ref, kc_ref, vc_ref, ka_ref, va_ref, kvw_ref, gt_ref, e_ref, band_ref, o_ref,
                qa_ref, m_ref, acc_ref):
    for sub in range(NSA_PER_STEP):
        _nsa_tile(n_super, sub, q_ref, kc_ref, vc_ref, ka_ref, va_ref, kvw_ref, gt_ref, e_ref, band_ref, o_ref,
                  qa_ref, m_ref, acc_ref)


def _nsa_tile(n_super, sub, q_ref, kc_ref, vc_ref, ka_ref, va_ref, kvw_ref, gt_ref, e_ref, band_ref, o_ref,
              qa_ref, m_ref, acc_ref):
    qi = pl.program_id(2) * NSA_PER_STEP + sub
    tq = NSA_TQ
    s0 = qi * tq
    q = q_ref[0, 0, sub]

    def compressed(width):
        sc = _dot_nt(q, kc_ref[0, 0, 0:width, :])
        rq = lax.broadcasted_iota(jnp.int32, (sc.shape[0], LANES), 0) & (tq - 1)
        last = jnp.tile((s0 + rq - (CMP_BLOCK - 1)) >> (CMP_STRIDE.bit_length() - 1), (1, width // LANES))
        sc = jnp.where(lax.broadcasted_iota(jnp.int32, sc.shape, 1) <= last, sc, NEG)
        mx = jnp.max(sc, axis=-1, keepdims=True)
        e = jnp.exp2(sc - mx)
        r = jnp.where(mx > 0.5 * NEG, 1.0 / jnp.sum(e, axis=-1, keepdims=True), 0.0)
        imp = e[0:tq] * r[0:tq]
        for g in range(1, NSA_GROUP):
            imp = imp + e[g * tq:(g + 1) * tq] * r[g * tq:(g + 1) * tq]
        return _dot(e.astype(BF16), vc_ref[0, 0, 0:width, :]) * r, _dot_x2(imp, band_ref[0:width, :])

    n16 = kc_ref.shape[2]
    widths = list(range(CMP_WIDTH_STEP, n16, CMP_WIDTH_STEP)) + [n16]
    o_c, pslc = lax.switch((s0 + tq - 1) // (CMP_WIDTH_STEP * CMP_STRIDE),
                           [functools.partial(compressed, wd) for wd in widths])

    wlen = WINDOW + tq

    def window(kvw, masked):
        sw = masked(_dot_nt(q, kvw[:, 0:HEAD_DIM]))
        pw = jnp.exp2(sw - jnp.max(sw, axis=-1, keepdims=True))
        return _dot(pw.astype(BF16), kvw)[:, HEAD_DIM:2 * HEAD_DIM] * (1.0 / jnp.sum(pw, axis=-1, keepdims=True))

    def window_interior():
        def masked(sw):
            rq = lax.broadcasted_iota(jnp.int32, (sw.shape[0], tq), 0) & (tq - 1)
            c = lax.broadcasted_iota(jnp.int32, (sw.shape[0], tq), 1)
            return jnp.concatenate([jnp.where(c > rq, sw[:, 0:tq], NEG), sw[:, tq:WINDOW],
                                    jnp.where(c <= rq, sw[:, WINDOW:wlen], NEG)], axis=1)

        return window(kvw_ref[0, 0, pl.ds(pl.multiple_of(s0 - WINDOW, tq), wlen), :], masked)

    def window_start():
        def masked(sw):
            kabs = lax.broadcasted_iota(jnp.int32, sw.shape, 1)
            t = s0 + (lax.broadcasted_iota(jnp.int32, sw.shape, 0) & (tq - 1))
            return jnp.where((kabs <= t) & (kabs > t - WINDOW), sw, NEG)

        return window(kvw_ref[0, 0, 0:wlen, :], masked)

    pslc_t = pslc.T
    nb = pslc_t.shape[0]
    blk = lax.broadcasted_iota(jnp.int32, pslc_t.shape, 0)
    cur = (s0 + lax.broadcasted_iota(jnp.int32, pslc_t.shape, 1)) // SEL_BLOCK
    c0 = s0 // SEL_BLOCK
    elig = (blk >= 1) & (blk <= cur - 2)
    sel = _pick_top(jnp.where(elig, pslc_t, -jnp.inf), blk, nb, SEL_TOPK - 3)
    sel = sel | (blk == 0) | (blk == cur - 1)
    past = jnp.where(sel & (blk < c0), 0.0, NEG).T.astype(BF16)
    for st in range(n_super):
        b = past[:, st * SEL_LANES:(st + 1) * SEL_LANES]
        qa_ref[st] = jnp.concatenate([q, jnp.concatenate([b] * NSA_GROUP, axis=0)], axis=1)

    _softmax_init(m_ref, acc_ref)
    d0 = pl.multiple_of(s0, tq)
    s = _dot_nt(q, ka_ref[0, 0, pl.ds(d0, tq), :][:, 0:HEAD_DIM])
    qpos = lax.broadcasted_iota(jnp.int32, s.shape, 0) & (tq - 1)
    kpos = lax.broadcasted_iota(jnp.int32, s.shape, 1)
    _softmax_step(jnp.where(kpos <= qpos, s, NEG), va_ref[0, 0, pl.ds(d0, tq), :], m_ref, acc_ref)

    def step(start, size):
        s = _dot_nt(qa_ref[start // SUPER_KEYS], ka_ref[0, 0, pl.ds(start, size), :])
        _softmax_step(s, va_ref[0, 0, pl.ds(start, size), :], m_ref, acc_ref)

    (o_w,) = _past_keys_loop(s0, NSA_TK, step, parts=4, riders=(
        (window_interior, lambda: lax.cond(s0 >= WINDOW, window_interior, window_start)),))
    o_s = _softmax_result(acc_ref)

    w = NSA_GROUP * HEAD_DIM
    gexp = _dot_x2(gt_ref[sub * tq:(sub + 1) * tq, :], e_ref[0])
    wide = lambda x: jnp.concatenate([x[g * tq:(g + 1) * tq] for g in range(NSA_GROUP)], axis=1)
    o_ref[sub * tq:(sub + 1) * tq, :] = (gexp[:, 0:w] * wide(o_c) + gexp[:, w:2 * w] * wide(o_s) + gexp[:, 2 * w:3 * w] * wide(o_w)).astype(BF16)


def _nsa(qs, kcmp, vcmp, ka, va, kvw, gates, batch, seq):
    tq = NSA_TQ
    nq = seq // tq
    nb = seq // SEL_BLOCK
    n16 = seq // CMP_STRIDE
    assert seq % SUPER_KEYS == 0 and tq == 2 * SEL_BLOCK and WINDOW % tq == 0 and NSA_TK >= WINDOW
    assert nq % NSA_PER_STEP == 0
    steps = nq // NSA_PER_STEP
    n_super = seq // SUPER_KEYS
    rows = NSA_GROUP * tq
    e = np.zeros((NSA_KV_HEADS, GATE_LANES, 3 * NSA_GROUP * HEAD_DIM), np.float32)
    for br in range(3):
        for hk in range(NSA_KV_HEADS):
            for g in range(NSA_GROUP):
                c = (br * NSA_GROUP + g) * HEAD_DIM
                e[hk, br * NSA_HEADS + hk * NSA_GROUP + g, c:c + HEAD_DIM] = 1.0
    nn, jj = np.arange(n16)[:, None], np.arange(nb)[None, :]
    band = ((nn >= 4 * jj - 1) & (nn <= 4 * jj + 3)).astype(np.float32)
    resident = lambda width: pl.BlockSpec((1, 1, seq, width), lambda b, h, i: (b, h, 0, 0))
    w = NSA_GROUP * HEAD_DIM
    return pl.pallas_call(
        functools.partial(_nsa_kernel, n_super),
        grid=(batch, NSA_KV_HEADS, steps),
        in_specs=[pl.BlockSpec((1, 1, NSA_PER_STEP, rows, HEAD_DIM), lambda b, h, i: (b, h, i, 0, 0)),
                  pl.BlockSpec((1, 1, n16, HEAD_DIM), lambda b, h, i: (b, h, 0, 0)),
                  pl.BlockSpec((1, 1, n16, HEAD_DIM), lambda b, h, i: (b, h, 0, 0)),
                  resident(LANES), resident(LANES), resident(LANES),
                  pl.BlockSpec((NSA_PER_STEP * tq, GATE_LANES), lambda b, h, i: (b * steps + i, 0)),
                  pl.BlockSpec((1, GATE_LANES, 3 * w), lambda b, h, i: (h, 0, 0)),
                  pl.BlockSpec((n16, nb), lambda b, h, i: (0, 0))],
        out_specs=pl.BlockSpec((NSA_PER_STEP * tq, w), lambda b, h, i: (b * steps + i, h)),
        out_shape=jax.ShapeDtypeStruct((batch * seq, NSA_W), BF16),
        scratch_shapes=[pltpu.VMEM((n_super, rows, LANES), BF16), pltpu.VMEM((rows, LANES), F32),
                        pltpu.VMEM((rows, LANES), F32)],
        compiler_params=_params("arbitrary", "arbitrary", "arbitrary"),
        name="nsa",
    )(qs, kcmp, vcmp, ka, va, kvw, gates, jnp.asarray(e, BF16), jnp.asarray(band, BF16))


def _odd_out_kernel(om_ref, on_ref, h_ref, w_ref, mg_ref, wr_ref, br_ref, o_ref, gate_ref, grp_ref, cnt_ref):
    acc = h_ref[...] + _dot(on_ref[...], w_ref[MOBA_W:D_MODEL, :])
    for h in range(MOBA_HEADS):
        acc = acc + _dot(om_ref[0, h], w_ref[h * HEAD_DIM:(h + 1) * HEAD_DIM, :])
    o_ref[...] = acc
    _route(acc, mg_ref, wr_ref, br_ref, gate_ref, grp_ref, cnt_ref)


def _odd_out(o_moba, o_nsa, h2, w_out, router, seq, tm=ROW_TILE):
    t = h2.shape[0]
    tps = seq // tm
    row = lambda w: pl.BlockSpec((tm, w), lambda i: (i, 0))
    r_in, r_out, r_shape = _router_specs(t, tm)
    h, *routing = pl.pallas_call(
        _odd_out_kernel,
        grid=(t // tm,),
        in_specs=[pl.BlockSpec((1, MOBA_HEADS, tm, HEAD_DIM), lambda i: (i // tps, 0, i % tps, 0)),
                  row(NSA_W), row(D_MODEL), _full((D_MODEL, D_MODEL))] + r_in,
        out_specs=[row(D_MODEL)] + r_out,
        out_shape=[jax.ShapeDtypeStruct((t, D_MODEL), F32)] + r_shape,
        compiler_params=_params("arbitrary"),
        name="odd_out",
    )(o_moba, o_nsa, h2, w_out, *router)
    return h, routing


def _even_layer(h2, batch, seq, router, norm, w_in, w_out, lam_re, lam_im, log_dt, b_re, b_im, c_re, c_im, d, w_glu,
                conv_w, conv_b):
    u, ub, yb = _even_in(h2, norm.reshape(1, D_MODEL), w_in.astype(BF16), conv_w, conv_b.reshape(1, CONV_WIDTH), seq)
    ypre = _s5_mixer_pre(ub, batch, seq, lam_re, lam_im, log_dt, b_re, b_im, c_re, c_im)
    return _even_out(ypre, u, yb, h2, d.reshape(1, S5_WIDTH), w_glu.astype(BF16), w_out.astype(BF16), router)


def _odd_layer(h2, batch, seq, router, norm, w_in, w_out, moba_q_norm, moba_k_norm, nsa_q_norm, nsa_kcmp_norm, nsa_ksel_norm,
               nsa_kwin_norm, cmp_pe_k, cmp_w1_k, cmp_w2_k, cmp_pe_v, cmp_w1_v, cmp_w2_v):
    qm, kam, kmean, vam, qs, kvc, kas, vas, kvw, gates = _odd_in(
        h2, batch, seq, norm.reshape(1, D_MODEL), w_in, moba_q_norm, moba_k_norm, nsa_q_norm, nsa_ksel_norm, nsa_kwin_norm)
    cmp = _compress(kvc, batch, seq, cmp_pe_k, cmp_w1_k, cmp_w2_k, cmp_pe_v, cmp_w1_v, cmp_w2_v, nsa_kcmp_norm)
    o_moba = _moba(qm, kam, kmean, vam, batch, seq)
    o_nsa = _nsa(qs, cmp[0], cmp[1], kas, vas, kvw, gates, batch, seq)
    return _odd_out(o_moba, o_nsa, h2, w_out.astype(BF16), router, seq)


def kernel(x, ev_norm_mix, ev_w_in, ev_w_out, s5_lam_re, s5_lam_im, s5_log_dt, s5_b_re, s5_b_im, s5_c_re, s5_c_im, s5_d, s5_w_glu, conv_w, conv_b, od_norm_mix, od_w_in, od_w_out, moba_q_norm, moba_k_norm, nsa_q_norm, nsa_kcmp_norm, nsa_ksel_norm, nsa_kwin_norm, cmp_pe_k, cmp_w1_k, cmp_w2_k, cmp_pe_v, cmp_w1_v, cmp_w2_v, moe_norm, moe_w_group, moe_b_group, moe_w_expert, moe_b_expert, moe_w_gate, moe_w_up, moe_w_down):
    batch, seq, _ = x.shape
    depth = moe_norm.shape[0]
    h = x.reshape(batch * seq, D_MODEL)
    for layer in range(depth):
        i = layer // 2
        router = _router_operands(moe_norm[layer], moe_w_group[layer], moe_b_group[layer], moe_w_expert[layer],
                                  moe_b_expert[layer])
        if layer % 2 == 0:
            h, routing = _even_layer(h, batch, seq, router, ev_norm_mix[i], ev_w_in[i], ev_w_out[i], s5_lam_re[i],
                                     s5_lam_im[i], s5_log_dt[i], s5_b_re[i], s5_b_im[i], s5_c_re[i], s5_c_im[i], s5_d[i],
                                     s5_w_glu[i], conv_w[i], conv_b[i])
        else:
            h, routing = _odd_layer(h, batch, seq, router, od_norm_mix[i], od_w_in[i], od_w_out[i], moba_q_norm[i],
                                    moba_k_norm[i], nsa_q_norm[i], nsa_kcmp_norm[i], nsa_ksel_norm[i], nsa_kwin_norm[i],
                                    cmp_pe_k[i], cmp_w1_k[i], cmp_w2_k[i], cmp_pe_v[i], cmp_w1_v[i], cmp_w2_v[i])
        h = _moe(h, moe_norm[layer].reshape(1, D_MODEL), routing, moe_w_gate[layer], moe_w_up[layer], moe_w_down[layer])
    return h.reshape(batch, seq, D_MODEL)
```

```python
import functools
import math

import jax
import jax.numpy as jnp
import numpy as np
from jax import lax
from jax.experimental import pallas as pl
from jax.experimental.pallas import tpu as pltpu

D_MODEL = 1024
HEAD_DIM = 64
EPS = 1e-6
S5_WIDTH = 256
S5_GROUP = 16
S5_GROUPS = 16
S5_STATE = 64
S5_CHUNK = 16
S5_GROUP_BITS = 4
S5_STATE_BITS = 6
S5_ROW_TILE = 512
S5_COL_TILE = 512
S5_SCAN_TILE = 256
CONV_WIDTH = 768
CONV_K = 3
MOBA_HEADS = 4
NSA_HEADS = 12
NSA_KV_HEADS = 2
NSA_GROUP = 6
MOBA_W = 256
NSA_W = 768
KV_W = 128
MOBA_BLOCK = 256
MOBA_TOPK = 3
CMP_BLOCK = 32
CMP_STRIDE = 16
CMP_HIDDEN = 256
SEL_BLOCK = 64
SEL_TOPK = 8
WINDOW = 512
N_GROUPS = 4
EXPERTS_PER_GROUP = 4
N_EXPERTS = 16
EXPERT_FF = 256

LANES = 128
SUBLANES = 8
VMEM_LIMIT_BYTES = 56 * 1024 * 1024
ROW_TILE = 1024
GATE_LANES = LANES
NEG = -float(2 ** 100)
Q_SCALE = HEAD_DIM ** -0.5 * math.log2(math.e)
F32 = jnp.float32
BF16 = jnp.bfloat16


def _params(*semantics):
    return pltpu.CompilerParams(dimension_semantics=semantics, vmem_limit_bytes=VMEM_LIMIT_BYTES)


def _dot(a, b):
    return jnp.dot(a, b, preferred_element_type=F32)


def _dot_nt(a, b):
    return lax.dot_general(a, b, (((1,), (1,)), ((), ())), preferred_element_type=F32)


def _split(x):
    hi = x.astype(BF16)
    lo = (x - hi.astype(F32)).astype(BF16)
    return hi, lo


def _dot_x2(x, w):
    hi, lo = _split(x)
    return _dot(hi, w) + _dot(lo, w)


def _dot_x3(x, w_hilo):
    n = w_hilo.shape[1] // 2
    hi, lo = _split(x)
    both = _dot(hi, w_hilo)
    return both[:, 0:n] + (both[:, n:2 * n] + _dot(lo, w_hilo[:, 0:n]))


def _rms(x, gain):
    return x * lax.rsqrt(jnp.mean(x * x, axis=-1, keepdims=True) + EPS) * gain


def _gelu(x):
    return 0.5 * x * (1.0 + jnp.tanh(math.sqrt(2.0 / math.pi) * (x + 0.044715 * (x * x * x))))


def _sigmoid(x):
    return 1.0 / (1.0 + jnp.exp(-x))


def _full(shape):
    n = len(shape)
    return pl.BlockSpec(shape, lambda *_: (0,) * n)


def _even_in_kernel(tiles_per_seq, x_ref, g_ref, w_ref, cw_ref, cb_ref, u_ref, ub_ref, yb_ref, carry_ref):
    i = pl.program_id(0)
    xn = _rms(x_ref[...], g_ref[...]).astype(BF16)
    u = _dot(xn, w_ref[:, 0:S5_WIDTH])
    u_ref[...] = u
    ub_ref[...] = u.astype(BF16)
    o = S5_WIDTH
    xc = _dot(xn, w_ref[:, o:o + CONV_WIDTH])
    gb = _dot(xn, w_ref[:, o + CONV_WIDTH:o + 2 * CONV_WIDTH])
    gc = _dot(xn, w_ref[:, o + 2 * CONV_WIDTH:o + 3 * CONV_WIDTH])
    z = gc * xc
    tm = z.shape[0]

    @pl.when(i % tiles_per_seq == 0)
    def _():
        carry_ref[...] = jnp.zeros_like(carry_ref)

    row = lax.broadcasted_iota(jnp.int32, z.shape, 0)
    prev1 = carry_ref[SUBLANES - 1:SUBLANES, :]
    prev2 = carry_ref[SUBLANES - 2:SUBLANES - 1, :]
    z1 = jnp.where(row == 0, prev1, pltpu.roll(z, 1, 0))
    z2 = jnp.where(row == 0, prev2, jnp.where(row == 1, prev1, pltpu.roll(z, 2, 0)))
    y = cw_ref[0:1, :] * z2 + cw_ref[1:2, :] * z1 + cw_ref[2:3, :] * z + cb_ref[...]
    yb_ref[...] = (gb * y).astype(BF16)
    carry_ref[...] = z[tm - SUBLANES:tm, :]


def _even_in(x2, gain, w_in, conv_w, conv_b, seq, tm=ROW_TILE):
    t = x2.shape[0]
    n_in = w_in.shape[1]
    return pl.pallas_call(
        functools.partial(_even_in_kernel, seq // tm),
        grid=(t // tm,),
        in_specs=[pl.BlockSpec((tm, D_MODEL), lambda i: (i, 0)), _full((1, D_MODEL)),
                  _full((D_MODEL, n_in)), _full((CONV_K, CONV_WIDTH)), _full((1, CONV_WIDTH))],
        out_specs=[pl.BlockSpec((tm, S5_WIDTH), lambda i: (i, 0)), pl.BlockSpec((tm, S5_WIDTH), lambda i: (i, 0)),
                   pl.BlockSpec((tm, CONV_WIDTH), lambda i: (i, 0))],
        out_shape=[jax.ShapeDtypeStruct((t, S5_WIDTH), F32), jax.ShapeDtypeStruct((t, S5_WIDTH), BF16),
                   jax.ShapeDtypeStruct((t, CONV_WIDTH), BF16)],
        scratch_shapes=[pltpu.VMEM((SUBLANES, CONV_WIDTH), F32)],
        compiler_params=_params("arbitrary"),
        name="even_in",
    )(x2, gain, w_in, conv_w, conv_b)


def _s5_weights(lam_re, lam_im, log_dt, b_re, b_im, c_re, c_im):
    g, p, hg, ck = S5_GROUPS, S5_STATE, S5_GROUP, S5_CHUNK
    lr, li = lam_re.astype(F32), lam_im.astype(F32)
    dt = jnp.exp(log_dt.astype(F32))[:, None]
    mag = jnp.exp(lr * dt)
    a_re, a_im = mag * jnp.cos(li * dt), mag * jnp.sin(li * dt)
    den = lr * lr + li * li
    f_re = ((a_re - 1.0) * lr + a_im * li) / den
    f_im = (a_im * lr - (a_re - 1.0) * li) / den
    br, bi = b_re.astype(F32), b_im.astype(F32)
    bb_re = f_re[..., None] * br - f_im[..., None] * bi
    bb_im = f_re[..., None] * bi + f_im[..., None] * br
    pw_re, pw_im = [jnp.ones_like(a_re)], [jnp.zeros_like(a_im)]
    for _ in range(ck):
        r, m = pw_re[-1], pw_im[-1]
        pw_re.append(r * a_re - m * a_im)
        pw_im.append(r * a_im + m * a_re)
    pw_re, pw_im = jnp.stack(pw_re), jnp.stack(pw_im)
    cr, ci = c_re.astype(F32), c_im.astype(F32)
    rev_re, rev_im = pw_re[ck - 1::-1][:ck], pw_im[ck - 1::-1][:ck]
    ws_re = rev_re[:, :, :, None] * bb_re[None] - rev_im[:, :, :, None] * bb_im[None]
    ws_im = rev_re[:, :, :, None] * bb_im[None] + rev_im[:, :, :, None] * bb_re[None]
    ca_re = cr[None] * pw_re[1:, :, None, :] - ci[None] * pw_im[1:, :, None, :]
    ca_im = cr[None] * pw_im[1:, :, None, :] + ci[None] * pw_re[1:, :, None, :]
    cb_re = jnp.einsum('ghp,kgp,gpj->kghj', cr, pw_re[:ck], bb_re) - jnp.einsum('ghp,kgp,gpj->kghj', cr, pw_im[:ck], bb_im) \
        - jnp.einsum('ghp,kgp,gpj->kghj', ci, pw_re[:ck], bb_im) - jnp.einsum('ghp,kgp,gpj->kghj', ci, pw_im[:ck], bb_re)
    lag = np.arange(ck)[None, :] - np.arange(ck)[:, None]
    tz = cb_re[np.clip(lag, 0, ck - 1)]
    tz = jnp.where((lag >= 0)[:, :, None, None, None], tz, 0.0)
    cw = ck * g * hg
    ws = jnp.stack([ws_re, ws_im]).transpose(1, 2, 4, 0, 3).reshape(cw, 2 * p)
    wc = jnp.stack([ca_re, -ca_im]).transpose(0, 2, 4, 1, 3).reshape(2 * g * p, ck * hg)
    tzc = tz.transpose(0, 2, 4, 1, 3).reshape(cw, ck * hg)
    return ws.astype(BF16), wc.astype(BF16), tzc.astype(BF16), pw_re[ck].reshape(1, g * p), pw_im[ck].reshape(1, g * p)


def _group_expand(compact, expand, row_shift, col_shift, col0):
    full = _dot(compact, expand)
    row = lax.broadcasted_iota(jnp.int32, full.shape, 0)
    col = col0 + lax.broadcasted_iota(jnp.int32, full.shape, 1)
    same = ((row >> row_shift) & (S5_GROUPS - 1)) == ((col >> col_shift) & (S5_GROUPS - 1))
    return jnp.where(same, full, 0.0).astype(BF16)


def _s5_state_kernel(u_ref, ws_ref, e_ref, s_ref, w_scr):
    @pl.when(pl.program_id(1) == 0)
    def _():
        w_scr[...] = _group_expand(ws_ref[...], e_ref[...], S5_GROUP_BITS, S5_STATE_BITS,
                                   pl.program_id(0) * w_scr.shape[1])

    s_ref[...] = _dot(u_ref[...], w_scr[...])


def _s5_scan_kernel(s_ref, are_ref, aim_ref, xprev_ref, st_ref):
    @pl.when(pl.program_id(0) == 0)
    def _():
        st_ref[...] = jnp.zeros_like(st_ref)

    a_re, a_im = are_ref[...], aim_ref[...]
    nb, n = s_ref.shape[0], s_ref.shape[1]
    half = a_re.shape[1]

    def body(c, carry):
        out = []
        for b in range(nb):
            xr, xi = carry[2 * b], carry[2 * b + 1]
            xprev_ref[b, pl.ds(c, 1), 0:half] = xr
            xprev_ref[b, pl.ds(c, 1), half:2 * half] = xi
            s = s_ref[b, pl.ds(c, 1), :]
            out += [a_re * xr - a_im * xi + s[:, 0:half], a_re * xi + a_im * xr + s[:, half:2 * half]]
        return tuple(out)

    init = tuple(st_ref[b:b + 1, o:o + half] for b in range(nb) for o in (0, half))
    final = lax.fori_loop(0, n, body, init, unroll=4)
    for b in range(nb):
        st_ref[b:b + 1, 0:half] = final[2 * b]
        st_ref[b:b + 1, half:2 * half] = final[2 * b + 1]


def _s5_out_kernel(u_ref, xp_ref, tz_ref, wc_ref, e_ref, y_ref, tz_scr, wc_scr):
    @pl.when(pl.program_id(1) == 0)
    def _():
        col0 = pl.program_id(0) * tz_scr.shape[1]
        tz_scr[...] = _group_expand(tz_ref[...], e_ref[...], S5_GROUP_BITS, S5_GROUP_BITS, col0)
        wc_scr[...] = _group_expand(wc_ref[...], e_ref[...], S5_STATE_BITS, S5_GROUP_BITS, col0)

    y_ref[...] = _dot(u_ref[...], tz_scr[...]) + _dot(xp_ref[...].astype(BF16), wc_scr[...])


def _expand_matrix(outer, inner):
    e = np.zeros((outer, inner, outer, S5_GROUPS, inner), np.float32)
    for x in range(outer):
        for y in range(inner):
            e[x, y, x, :, y] = 1.0
    return jnp.asarray(e.reshape(outer * inner, outer * S5_GROUPS * inner), BF16)


def _s5_mixer_pre(ub, batch, seq, lam_re, lam_im, log_dt, b_re, b_im, c_re, c_im):
    ws, wc, tz, a16_re, a16_im = _s5_weights(lam_re, lam_im, log_dt, b_re, b_im, c_re, c_im)
    nc = seq // S5_CHUNK
    rows = batch * nc
    cw = S5_CHUNK * S5_WIDTH
    sw = 2 * S5_GROUPS * S5_STATE
    assert S5_GROUP == 1 << S5_GROUP_BITS and S5_STATE == 1 << S5_STATE_BITS
    tr = min(rows, S5_ROW_TILE)
    tn = S5_COL_TILE
    uc = ub.reshape(rows, cw)
    e_state = _expand_matrix(2, S5_STATE)
    e_out = _expand_matrix(S5_CHUNK, S5_GROUP)
    s = pl.pallas_call(
        _s5_state_kernel,
        grid=(sw // tn, rows // tr),
        in_specs=[pl.BlockSpec((tr, cw), lambda j, i: (i, 0)), _full(ws.shape),
                  pl.BlockSpec((e_state.shape[0], tn), lambda j, i: (0, j))],
        out_specs=pl.BlockSpec((tr, tn), lambda j, i: (i, j)),
        out_shape=jax.ShapeDtypeStruct((rows, sw), F32),
        scratch_shapes=[pltpu.VMEM((cw, tn), BF16)],
        compiler_params=_params("arbitrary", "arbitrary"),
        name="s5_state",
    )(uc, ws, e_state)
    tc = min(nc, S5_SCAN_TILE)
    xprev = pl.pallas_call(
        _s5_scan_kernel,
        grid=(nc // tc,),
        in_specs=[pl.BlockSpec((batch, tc, sw), lambda i: (0, i, 0)), _full((1, sw // 2)), _full((1, sw // 2))],
        out_specs=pl.BlockSpec((batch, tc, sw), lambda i: (0, i, 0)),
        out_shape=jax.ShapeDtypeStruct((batch, nc, sw), F32),
        scratch_shapes=[pltpu.VMEM((batch, sw), F32)],
        compiler_params=_params("arbitrary"),
        name="s5_scan",
    )(s.reshape(batch, nc, sw), a16_re, a16_im)
    y = pl.pallas_call(
        _s5_out_kernel,
        grid=(cw // tn, rows // tr),
        in_specs=[pl.BlockSpec((tr, cw), lambda j, i: (i, 0)), pl.BlockSpec((tr, sw), lambda j, i: (i, 0)),
                  _full(tz.shape), _full(wc.shape), pl.BlockSpec((e_out.shape[0], tn), lambda j, i: (0, j))],
        out_specs=pl.BlockSpec((tr, tn), lambda j, i: (i, j)),
        out_shape=jax.ShapeDtypeStruct((rows, cw), F32),
        scratch_shapes=[pltpu.VMEM((cw, tn), BF16), pltpu.VMEM((sw, tn), BF16)],
        compiler_params=_params("arbitrary", "arbitrary"),
        name="s5_out",
    )(uc, xprev.reshape(rows, sw), tz, wc, e_out)
    return y.reshape(batch * seq, S5_WIDTH)


def _even_out_kernel(ypre_ref, u_ref, yb_ref, x_ref, d_ref, wglu_ref, wout_ref, mg_ref, wr_ref, br_ref,
                     o_ref, gate_ref, grp_ref, cnt_ref):
    y = _gelu(ypre_ref[...] + d_ref[...] * u_ref[...])
    y = y * _sigmoid(_dot(y.astype(BF16), wglu_ref[...]))
    h = (x_ref[...] + _dot(y.astype(BF16), wout_ref[0:S5_WIDTH, :]) + _dot(yb_ref[...], wout_ref[S5_WIDTH:D_MODEL, :]))
    o_ref[...] = h
    _route(h, mg_ref, wr_ref, br_ref, gate_ref, grp_ref, cnt_ref)


def _even_out(ypre, u, yb, x2, d, w_glu, w_out, router, tm=ROW_TILE):
    t = x2.shape[0]
    row = lambda w: pl.BlockSpec((tm, w), lambda i: (i, 0))
    r_in, r_out, r_shape = _router_specs(t, tm)
    h, *routing = pl.pallas_call(
        _even_out_kernel,
        grid=(t // tm,),
        in_specs=[row(S5_WIDTH), row(S5_WIDTH), row(CONV_WIDTH), row(D_MODEL), _full((1, S5_WIDTH)),
                  _full((S5_WIDTH, S5_WIDTH)), _full((D_MODEL, D_MODEL))] + r_in,
        out_specs=[row(D_MODEL)] + r_out,
        out_shape=[jax.ShapeDtypeStruct((t, D_MODEL), F32)] + r_shape,
        compiler_params=_params("arbitrary"),
        name="even_out",
    )(ypre, u, yb, x2, d, w_glu, w_out, *router)
    return h, routing


def _first_max(v, pos, width, axis=-1):
    m = jnp.max(v, axis=axis, keepdims=True)
    idx = jnp.min(jnp.where(v == m, pos, width), axis=axis, keepdims=True)
    return m, idx


MOE_TM = 1024
MOE_ALIGN = 16
MOE_SLOTS = 1152
MOE_WIN = 320
ROUTER_LANES = LANES


def _route(h, g_ref, wr_ref, br_ref, gate_ref, grp_ref, cnt_ref):
    xn = _rms(h, g_ref[...])
    logits = _dot_x3(xn, wr_ref[...]) + br_ref[...]
    lane = lax.broadcasted_iota(jnp.int32, logits.shape, 1).astype(F32)
    width = float(logits.shape[1])
    is_g = lane < N_GROUPS
    gl = jnp.where(is_g, logits, -jnp.inf)
    gm, gi = _first_max(gl, lane, width)
    gw = 1.0 / jnp.sum(jnp.where(is_g, jnp.exp(gl - gm), 0.0), axis=-1, keepdims=True)
    lo = N_GROUPS + gi * EXPERTS_PER_GROUP
    in_grp = (lane >= lo) & (lane < lo + EXPERTS_PER_GROUP)
    el = jnp.where(in_grp, logits, -jnp.inf)
    m1, i1 = _first_max(el, lane, width)
    m2, i2 = _first_max(jnp.where(lane == i1, -jnp.inf, el), lane, width)
    p2 = jnp.exp(m2 - m1)
    w1 = gw / (1.0 + p2)
    w2 = gw * p2 / (1.0 + p2)
    gate_ref[...] = jnp.where(lane == i1, w1, 0.0) + jnp.where(lane == i2, w2, 0.0)
    grp = (lane == gi).astype(F32)
    grp_ref[...] = grp.astype(BF16)
    cnt_ref[0] = jnp.broadcast_to(jnp.sum(grp, axis=0, keepdims=True), cnt_ref.shape[1:])


def _moe_expert_kernel(base_ref, nwin_ref, h_ref, g_ref, gate_ref, grp_ref, wg_ref, wu_ref, wd_ref, o_ref,
                       xs_ref, gs_ref, ys_ref, pt_ref):
    i, g = pl.program_id(0), pl.program_id(1)
    tm = h_ref.shape[0]

    @pl.when((i == 0) & (g == 0))
    def _():
        xs_ref[...] = jnp.zeros_like(xs_ref)
        gs_ref[...] = jnp.zeros_like(gs_ref)
        ys_ref[...] = jnp.zeros_like(ys_ref)

    @pl.when(g == 0)
    def _():
        xn = _rms(h_ref[...], g_ref[...]).astype(BF16)
        grp = grp_ref[...]
        earlier = (lax.broadcasted_iota(jnp.int32, (tm, tm), 0) > lax.broadcasted_iota(jnp.int32, (tm, tm), 1)).astype(BF16)
        rank = _dot(earlier, grp)
        lane = lax.broadcasted_iota(jnp.int32, rank.shape, 1)
        for k in range(N_GROUPS):
            rank = rank + jnp.where(lane == k, base_ref[i * N_GROUPS + k].astype(F32), 0.0)
        slot = jnp.sum(grp.astype(F32) * rank, axis=-1, keepdims=True).astype(jnp.int32)
        pt = (lax.broadcasted_iota(jnp.int32, (tm, MOE_SLOTS), 1) == slot).astype(BF16)
        pt_ref[...] = pt
        gather = lambda x: lax.dot_general(pt, x, (((0,), (0,)), ((), ())), preferred_element_type=F32)
        xs_ref[0:MOE_SLOTS, :] = gather(xn).astype(BF16)
        both = gather(jnp.concatenate(_split(gate_ref[...]), axis=1))
        gs_ref[0:MOE_SLOTS, :] = both[:, 0:ROUTER_LANES] + both[:, ROUTER_LANES:2 * ROUTER_LANES]

    def window(w, carry):
        r0 = pl.multiple_of(base_ref[i * N_GROUPS + g] + w * MOE_WIN, MOE_ALIGN)
        x = xs_ref[pl.ds(r0, MOE_WIN), :]
        gate = gs_ref[pl.ds(r0, MOE_WIN), :]
        lane = lax.broadcasted_iota(jnp.int32, gate.shape, 1)
        y = None
        for j in range(EXPERTS_PER_GROUP):
            ge = jnp.sum(jnp.where(lane == g * EXPERTS_PER_GROUP + (j + N_GROUPS), gate, 0.0), axis=-1, keepdims=True)
            h1 = _dot(x, wg_ref[j])
            h3 = _dot(x, wu_ref[j])
            act = (h1 * _sigmoid(h1)) * h3 * ge
            yj = _dot(act.astype(BF16), wd_ref[j])
            y = yj if y is None else y + yj
        ys_ref[pl.ds(r0, MOE_WIN), :] = y.astype(BF16)
        return carry

    lax.fori_loop(0, nwin_ref[i * N_GROUPS + g], window, 0)

    @pl.when(g == N_GROUPS - 1)
    def _():
        o_ref[...] = h_ref[...] + _dot(pt_ref[...], ys_ref[0:MOE_SLOTS, :])


def _router_operands(moe_gain, w_group, b_group, w_expert, b_expert):
    rw = ROUTER_LANES
    wr = jnp.zeros((D_MODEL, rw), F32).at[:, 0:N_GROUPS].set(w_group).at[:, N_GROUPS:N_GROUPS + N_EXPERTS].set(w_expert)
    br = jnp.zeros((1, rw), F32).at[0, 0:N_GROUPS].set(b_group).at[0, N_GROUPS:N_GROUPS + N_EXPERTS].set(b_expert)
    return moe_gain.reshape(1, D_MODEL), jnp.concatenate(_split(wr), axis=1), br


def _router_specs(t, tm):
    assert tm == MOE_TM and t % tm == 0
    rw = ROUTER_LANES
    row = lambda width: pl.BlockSpec((tm, width), lambda i: (i, 0))
    in_specs = [_full((1, D_MODEL)), _full((D_MODEL, 2 * rw)), _full((1, rw))]
    out_specs = [row(rw), row(rw), pl.BlockSpec((1, SUBLANES, rw), lambda i: (i, 0, 0))]
    out_shape = [jax.ShapeDtypeStruct((t, rw), F32), jax.ShapeDtypeStruct((t, rw), BF16),
                 jax.ShapeDtypeStruct((t // tm, SUBLANES, rw), F32)]
    return in_specs, out_specs, out_shape


def _moe(h2, gain, routing, w_gate, w_up, w_down):
    t = h2.shape[0]
    tm, rw = MOE_TM, ROUTER_LANES
    assert t % tm == 0 and MOE_SLOTS >= tm + N_GROUPS * (MOE_ALIGN - 1) and MOE_WIN % MOE_ALIGN == 0
    n_tiles = t // tm
    gates, grp, cnt = routing
    n = cnt[:, 0, 0:N_GROUPS].astype(jnp.int32)
    padded = (n + (MOE_ALIGN - 1)) // MOE_ALIGN * MOE_ALIGN
    base = (jnp.cumsum(padded, axis=1) - padded).reshape(-1)
    nwin = ((padded + (MOE_WIN - 1)) // MOE_WIN).reshape(-1)
    tile = lambda width: pl.BlockSpec((tm, width), lambda i, g, *_: (i, 0))
    experts = lambda shape: pl.BlockSpec((EXPERTS_PER_GROUP,) + shape, lambda i, g, *_: (g, 0, 0))
    slots = MOE_SLOTS + MOE_WIN
    return pl.pallas_call(
        _moe_expert_kernel,
        grid_spec=pltpu.PrefetchScalarGridSpec(
            num_scalar_prefetch=2,
            grid=(n_tiles, N_GROUPS),
            in_specs=[tile(D_MODEL), pl.BlockSpec((1, D_MODEL), lambda i, g, *_: (0, 0)), tile(rw), tile(rw),
                      experts((D_MODEL, EXPERT_FF)), experts((D_MODEL, EXPERT_FF)), experts((EXPERT_FF, D_MODEL))],
            out_specs=tile(D_MODEL),
            scratch_shapes=[pltpu.VMEM((slots, D_MODEL), BF16), pltpu.VMEM((slots, rw), F32),
                            pltpu.VMEM((slots, D_MODEL), BF16), pltpu.VMEM((tm, MOE_SLOTS), BF16)]),
        out_shape=jax.ShapeDtypeStruct((t, D_MODEL), F32),
        compiler_params=_params("arbitrary", "arbitrary"),
        name="moe",
    )(base, nwin, h2, gain, gates, grp, w_gate.astype(BF16), w_up.astype(BF16), w_down.astype(BF16))


ODD_SPLITS = (MOBA_W, MOBA_W, MOBA_W, NSA_W, KV_W, KV_W, KV_W, KV_W, KV_W, KV_W, GATE_LANES)
ODD_IN_PAD = sum(ODD_SPLITS)


def _head_rms(x, hsum, gain):
    w = x.shape[1]
    ss = jnp.concatenate([_dot_x2(x[:, o:o + hsum.shape[0]] * x[:, o:o + hsum.shape[0]], hsum)
                          for o in range(0, w, hsum.shape[0])], axis=1) if w > hsum.shape[0] else _dot_x2(x * x, hsum)
    return x * lax.rsqrt(ss * (1.0 / HEAD_DIM) + EPS) * gain


def _odd_in_kernel(tiles_per_seq, x_ref, g_ref, w_ref, hsum_ref, gq_ref, gk_ref, gnq_ref, gks_ref, gkw_ref,
                   qm_ref, kam_ref, kmean_ref, vam_ref, qs_ref, kvc_ref, kas_ref, vas_ref, kvw_ref, gt_ref):
    xn = _rms(x_ref[...], g_ref[...]).astype(BF16)
    offs = np.cumsum((0,) + ODD_SPLITS)
    col = lambda j: _dot(xn, w_ref[:, int(offs[j]):int(offs[j + 1])])
    head = lambda x, h: x[:, h * HEAD_DIM:(h + 1) * HEAD_DIM]
    hsum = hsum_ref[...]
    hsum128 = hsum_ref[0:KV_W, 0:KV_W]
    tm = x_ref.shape[0]
    pos = (pl.program_id(0) % tiles_per_seq) * tm + lax.broadcasted_iota(jnp.int32, (tm, HEAD_DIM), 0)
    lane = lax.broadcasted_iota(jnp.int32, (tm, HEAD_DIM), 1)
    ones_col = (lane == 0).astype(BF16)

    qm = _head_rms(col(0), hsum, gq_ref[...])
    km = _head_rms(col(1), hsum, gk_ref[...])
    for j in range(tm // MOBA_BLOCK):
        kmean_ref[0, j:j + 1, :] = jnp.mean(km[j * MOBA_BLOCK:(j + 1) * MOBA_BLOCK, :], axis=0, keepdims=True)
    km = km.astype(BF16)
    vm = col(2).astype(BF16)
    moba_id = (lane == pos // MOBA_BLOCK).astype(BF16)
    for h in range(MOBA_HEADS):
        qm_ref[0, h] = head(qm, h)
        kam_ref[0, h] = jnp.concatenate([head(km, h), moba_id], axis=1)
        vam_ref[0, h] = jnp.concatenate([head(vm, h), ones_col], axis=1)

    qd = (_head_rms(col(3), hsum, gnq_ref[...]) * Q_SCALE).astype(BF16)
    for hk in range(NSA_KV_HEADS):
        for j in range(tm // NSA_TQ):
            for g in range(NSA_GROUP):
                qs_ref[0, hk, j, g * NSA_TQ:(g + 1) * NSA_TQ, :] = head(qd, hk * NSA_GROUP + g)[j * NSA_TQ:(j + 1) * NSA_TQ, :]
    kvc_ref[0] = col(4)
    kvc_ref[1] = col(5)
    ks = _head_rms(col(6), hsum128, gks_ref[...]).astype(BF16)
    vs = col(7).astype(BF16)
    kw = _head_rms(col(8), hsum128, gkw_ref[...]).astype(BF16)
    vw = col(9).astype(BF16)
    sel_id = (lane == (pos // SEL_BLOCK) % SEL_LANES).astype(BF16)
    for hk in range(NSA_KV_HEADS):
        kas_ref[0, hk] = jnp.concatenate([head(ks, hk), sel_id], axis=1)
        vas_ref[0, hk] = jnp.concatenate([head(vs, hk), ones_col], axis=1)
        kvw_ref[0, hk] = jnp.concatenate([head(kw, hk), head(vw, hk)], axis=1)
    gt_ref[...] = _sigmoid(col(10))


def _odd_in(h2, batch, seq, gain, w_in, moba_q_norm, moba_k_norm, nsa_q_norm, nsa_ksel_norm, nsa_kwin_norm, tm=ROW_TILE):
    t = h2.shape[0]
    assert MOBA_LANES == HEAD_DIM and SEL_LANES == HEAD_DIM and seq % tm == 0 and tm % MOBA_BLOCK == 0
    tps = seq // tm
    w = jnp.pad(w_in, ((0, 0), (0, ODD_IN_PAD - w_in.shape[1]))).astype(BF16)
    hsum = jnp.asarray(np.kron(np.eye(MOBA_W // HEAD_DIM), np.ones((HEAD_DIM, HEAD_DIM))), BF16)
    tile = lambda g, width: jnp.tile(g.astype(F32), width // HEAD_DIM).reshape(1, width)
    row = lambda width: pl.BlockSpec((tm, width), lambda i: (i, 0))
    heads = lambda n, width: pl.BlockSpec((1, n, tm, width), lambda i: (i // tps, 0, i % tps, 0))
    nmb = tm // MOBA_BLOCK
    nqt = tm // NSA_TQ
    rows = NSA_GROUP * NSA_TQ
    sds = jax.ShapeDtypeStruct
    out_specs = [heads(MOBA_HEADS, HEAD_DIM), heads(MOBA_HEADS, LANES), pl.BlockSpec((1, nmb, MOBA_W), lambda i: (i, 0, 0)),
                 heads(MOBA_HEADS, LANES),
                 pl.BlockSpec((1, NSA_KV_HEADS, nqt, rows, HEAD_DIM), lambda i: (i // tps, 0, i % tps, 0, 0)),
                 pl.BlockSpec((2, tm, KV_W), lambda i: (0, i, 0)),
                 heads(NSA_KV_HEADS, LANES), heads(NSA_KV_HEADS, LANES), heads(NSA_KV_HEADS, LANES),
                 row(GATE_LANES)]
    out_shape = [sds((batch, MOBA_HEADS, seq, HEAD_DIM), F32), sds((batch, MOBA_HEADS, seq, LANES), BF16),
                 sds((t // tm, nmb, MOBA_W), F32), sds((batch, MOBA_HEADS, seq, LANES), BF16),
                 sds((batch, NSA_KV_HEADS, seq // NSA_TQ, rows, HEAD_DIM), BF16),
                 sds((2, t, KV_W), F32), sds((batch, NSA_KV_HEADS, seq, LANES), BF16),
                 sds((batch, NSA_KV_HEADS, seq, LANES), BF16), sds((batch, NSA_KV_HEADS, seq, LANES), BF16),
                 sds((t, GATE_LANES), F32)]
    return pl.pallas_call(
        functools.partial(_odd_in_kernel, tps),
        grid=(t // tm,),
        in_specs=[row(D_MODEL), _full((1, D_MODEL)), _full((D_MODEL, ODD_IN_PAD)), _full((MOBA_W, MOBA_W)),
                  _full((1, MOBA_W)), _full((1, MOBA_W)), _full((1, NSA_W)), _full((1, KV_W)), _full((1, KV_W))],
        out_specs=out_specs,
        out_shape=out_shape,
        compiler_params=_params("arbitrary"),
        name="odd_in",
    )(h2, gain, w, hsum, tile(moba_q_norm, MOBA_W), tile(moba_k_norm, MOBA_W), tile(nsa_q_norm, NSA_W),
      tile(nsa_ksel_norm, KV_W), tile(nsa_kwin_norm, KV_W))


def _compress_kernel(c_ref, w1_ref, w2_ref, pe_ref, g_ref, o_ref):
    kind = pl.program_id(0)
    n16 = c_ref.shape[1] // CMP_STRIDE
    half = CMP_STRIDE * HEAD_DIM
    peb = _dot(pe_ref[0], w1_ref[0])[0:1, :]
    xs = [c_ref[0, pl.ds(s, n16, stride=CMP_STRIDE), :].astype(BF16) for s in range(CMP_STRIDE)]
    for h in range(NSA_KV_HEADS):
        first = second = None
        for s in range(CMP_STRIDE):
            x = xs[s][:, h * HEAD_DIM:(h + 1) * HEAD_DIM]
            a = _dot(x, w1_ref[0, s * HEAD_DIM:(s + 1) * HEAD_DIM, :])
            b = _dot(x, w1_ref[0, half + s * HEAD_DIM:half + (s + 1) * HEAD_DIM, :])
            first = a if first is None else first + a
            second = b if second is None else second + b
        hid = _gelu(first + pltpu.roll(second, n16 - 1, 0) + peb)
        out = _dot(hid.astype(BF16), w2_ref[0])
        o_ref[0, 0, h] = jnp.where(kind == 0, _rms(out, g_ref[...]), out).astype(BF16)


def _compress(kvc, batch, seq, pe_k, w1_k, w2_k, pe_v, w1_v, w2_v, kcmp_norm):
    n16 = seq // CMP_STRIDE
    half = CMP_STRIDE * HEAD_DIM
    w1 = jnp.stack([w1_k, w1_v]).astype(BF16)
    w2 = jnp.stack([w2_k, w2_v]).astype(BF16)
    pe = jnp.stack([pe_k, pe_v]).reshape(2, 1, 2 * half)
    pe = jnp.broadcast_to(pe, (2, SUBLANES, 2 * half)).astype(BF16)
    return pl.pallas_call(
        _compress_kernel,
        grid=(2, batch),
        in_specs=[pl.BlockSpec((1, seq, KV_W), lambda k, b: (k, b, 0)),
                  pl.BlockSpec((1, 2 * half, CMP_HIDDEN), lambda k, b: (k, 0, 0)),
                  pl.BlockSpec((1, CMP_HIDDEN, HEAD_DIM), lambda k, b: (k, 0, 0)),
                  pl.BlockSpec((1, SUBLANES, 2 * half), lambda k, b: (k, 0, 0)),
                  _full((1, HEAD_DIM))],
        out_specs=pl.BlockSpec((1, 1, NSA_KV_HEADS, n16, HEAD_DIM), lambda k, b: (k, b, 0, 0, 0)),
        out_shape=jax.ShapeDtypeStruct((2, batch, NSA_KV_HEADS, n16, HEAD_DIM), BF16),
        compiler_params=_params("arbitrary", "arbitrary"),
        name="nsa_compress",
    )(kvc, w1, w2, pe, kcmp_norm.astype(F32).reshape(1, HEAD_DIM))


M_INIT = -1e30


def _softmax_init(m_ref, acc_ref):
    m_ref[...] = jnp.full(m_ref.shape, M_INIT, F32)
    acc_ref[...] = jnp.zeros(acc_ref.shape, F32)


def _softmax_step(s, v_aug, m_ref, acc_ref):
    m_old = m_ref[...]
    m_new = jnp.maximum(m_old, jnp.max(s, axis=-1, keepdims=True))
    alpha = jnp.exp2(m_old - m_new)
    p = jnp.exp2(s - jnp.tile(m_new, (1, s.shape[1] // LANES)))
    acc_ref[...] = alpha * acc_ref[...] + _dot(p.astype(BF16), v_aug)
    m_ref[...] = m_new


def _past_keys_loop(n_keys, tile, step, riders=(), parts=2):
    n_full = n_keys // tile

    def body(j, carry):
        step(pl.multiple_of(j * tile, tile), tile)
        return carry

    extras = []
    for k, (with_tile, alone) in enumerate(riders):
        def both(with_tile=with_tile, k=k):
            out = with_tile()
            step(k * tile, tile)
            return out

        extras.append(lax.cond(n_full > k, both, alone))
    lax.fori_loop(len(riders), n_full, body, 0)
    rest = n_keys - n_full * tile
    start = pl.multiple_of(n_full * tile, tile)
    part = tile // parts
    n_parts = (rest + part - 1) // part
    for k in range(1, parts + 1):
        @pl.when(n_parts == k)
        def _(k=k):
            done = 0
            for size in (tile, tile // 2, tile // 4):
                if size % part == 0 and k * part - done >= size:
                    step(pl.multiple_of(start + done, part), size)
                    done += size
            assert done == k * part

    return extras


def _softmax_result(acc_ref):
    acc = acc_ref[...]
    return acc[:, 0:HEAD_DIM] * (1.0 / acc[:, HEAD_DIM:HEAD_DIM + 1])


def _pick_top(score, pos, width, k):
    sel = jnp.zeros(score.shape, jnp.bool_)
    for _ in range(k):
        m, idx = _first_max(score, pos, width, axis=0)
        hit = (pos == idx) & (m > -jnp.inf)
        sel = sel | hit
        score = jnp.where(pos == idx, -jnp.inf, score)
    return sel


MOBA_LANES = 64
MOBA_TK = 2048


MOBA_TQ = 4 * MOBA_BLOCK


def _moba_kernel(q_ref, ka_ref, va_ref, kmean_ref, o_ref, qa_ref, m_ref, acc_ref):
    i0 = pl.program_id(2) * (MOBA_TQ // MOBA_BLOCK)
    q = q_ref[0, 0]
    q_hi, q_lo = _split(q)
    km_hi, km_lo = _split(kmean_ref[0, 0])
    gate = _dot_nt(km_hi, q_hi) + (_dot_nt(km_lo, q_hi) + _dot_nt(km_hi, q_lo))
    blk = lax.broadcasted_iota(jnp.int32, gate.shape, 0)
    cur = i0 + lax.broadcasted_iota(jnp.int32, gate.shape, 1) // MOBA_BLOCK
    sel = _pick_top(jnp.where(blk < cur, gate, -jnp.inf), blk, gate.shape[0], MOBA_TOPK)
    qs = (q * Q_SCALE).astype(BF16)
    past = jnp.where(sel & (blk < i0), 0.0, NEG).T[:, 0:MOBA_LANES]
    own = jnp.where((blk == cur) | (sel & (blk >= i0)), 0.0, NEG).T[:, 0:MOBA_LANES]
    qa_ref[0] = jnp.concatenate([qs, past.astype(BF16)], axis=1)
    qa_ref[1] = jnp.concatenate([qs, own.astype(BF16)], axis=1)
    _softmax_init(m_ref, acc_ref)

    def step(start, size):
        s = _dot_nt(qa_ref[0], ka_ref[0, 0, pl.ds(start, size), :])
        _softmax_step(s, va_ref[0, 0, pl.ds(start, size), :], m_ref, acc_ref)

    _past_keys_loop(i0 * MOBA_BLOCK, MOBA_TK, step)
    start = pl.multiple_of(i0 * MOBA_BLOCK, MOBA_TQ)
    s = _dot_nt(qa_ref[1], ka_ref[0, 0, pl.ds(start, MOBA_TQ), :])
    qpos = lax.broadcasted_iota(jnp.int32, s.shape, 0)
    kpos = lax.broadcasted_iota(jnp.int32, s.shape, 1)
    hidden = (qpos // MOBA_BLOCK == kpos // MOBA_BLOCK) & (kpos > qpos)
    _softmax_step(jnp.where(hidden, NEG, s), va_ref[0, 0, pl.ds(start, MOBA_TQ), :], m_ref, acc_ref)
    o_ref[0, 0] = _softmax_result(acc_ref).astype(BF16)


def _moba(qm, ka, kmean, va, batch, seq):
    nmb = seq // MOBA_BLOCK
    assert nmb <= MOBA_LANES and seq % MOBA_TK == 0 and seq % MOBA_TQ == 0
    kmean = kmean.reshape(batch, nmb, MOBA_HEADS, HEAD_DIM).transpose(0, 2, 1, 3)
    kmean = jnp.pad(kmean, ((0, 0), (0, 0), (0, LANES - nmb), (0, 0)))
    return pl.pallas_call(
        _moba_kernel,
        grid=(batch, MOBA_HEADS, seq // MOBA_TQ),
        in_specs=[pl.BlockSpec((1, 1, MOBA_TQ, HEAD_DIM), lambda b, h, i: (b, h, i, 0)),
                  pl.BlockSpec((1, 1, seq, LANES), lambda b, h, i: (b, h, 0, 0)),
                  pl.BlockSpec((1, 1, seq, LANES), lambda b, h, i: (b, h, 0, 0)),
                  pl.BlockSpec((1, 1, LANES, HEAD_DIM), lambda b, h, i: (b, h, 0, 0))],
        out_specs=pl.BlockSpec((1, 1, MOBA_TQ, HEAD_DIM), lambda b, h, i: (b, h, i, 0)),
        out_shape=jax.ShapeDtypeStruct((batch, MOBA_HEADS, seq, HEAD_DIM), BF16),
        scratch_shapes=[pltpu.VMEM((2, MOBA_TQ, LANES), BF16), pltpu.VMEM((MOBA_TQ, LANES), F32),
                        pltpu.VMEM((MOBA_TQ, LANES), F32)],
        compiler_params=_params("arbitrary", "arbitrary", "arbitrary"),
        name="moba",
    )(qm, ka, va, kmean)


NSA_TQ = 2 * SEL_BLOCK
NSA_TK = 2048
SEL_LANES = 64
SUPER_KEYS = SEL_LANES * SEL_BLOCK
CMP_WIDTH_STEP = 256


def _nsa_select_kernel(q_ref, kc_ref, vc_ref, band_ref, oc_ref, bias_ref):
    qi = pl.program_id(2)
    tq = NSA_TQ
    s0 = qi * tq
    q = q_ref[0, 0, 0]

    def compressed(width):
        sc = _dot_nt(q, kc_ref[0, 0, 0:width, :])
        rq = lax.broadcasted_iota(jnp.int32, (sc.shape[0], LANES), 0) & (tq - 1)
        last = jnp.tile((s0 + rq - (CMP_BLOCK - 1)) >> (CMP_STRIDE.bit_length() - 1), (1, width // LANES))
        sc = jnp.where(lax.broadcasted_iota(jnp.int32, sc.shape, 1) <= last, sc, NEG)
        mx = jnp.max(sc, axis=-1, keepdims=True)
        e = jnp.exp2(sc - mx)
        r = jnp.where(mx > 0.5 * NEG, 1.0 / jnp.sum(e, axis=-1, keepdims=True), 0.0)
        imp = e[0:tq] * r[0:tq]
        for g in range(1, NSA_GROUP):
            imp = imp + e[g * tq:(g + 1) * tq] * r[g * tq:(g + 1) * tq]
        return _dot(e.astype(BF16), vc_ref[0, 0, 0:width, :]) * r, _dot_x2(imp, band_ref[0:width, :])

    n16 = kc_ref.shape[2]
    widths = list(range(CMP_WIDTH_STEP, n16, CMP_WIDTH_STEP)) + [n16]
    o_c, pslc = lax.switch((s0 + tq - 1) // (CMP_WIDTH_STEP * CMP_STRIDE),
                           [functools.partial(compressed, wd) for wd in widths])

    pslc_t = pslc.T
    nb = pslc_t.shape[0]
    blk = lax.broadcasted_iota(jnp.int32, pslc_t.shape, 0)
    cur = (s0 + lax.broadcasted_iota(jnp.int32, pslc_t.shape, 1)) // SEL_BLOCK
    c0 = s0 // SEL_BLOCK
    elig = (blk >= 1) & (blk <= cur - 2)
    sel = _pick_top(jnp.where(elig, pslc_t, -jnp.inf), blk, nb, SEL_TOPK - 3)
    sel = sel | (blk == 0) | (blk == cur - 1)
    oc_ref[0, 0, 0] = o_c
    bias_ref[0, 0, 0] = jnp.where(sel & (blk < c0), 0.0, NEG).T.astype(BF16)


def _nsa_kernel(n_super, q_ref, oc_ref, bias_ref, ka_ref, va_ref, kvw_ref, gt_ref, e_ref, o_ref,
                qa_ref, m_ref, acc_ref):
    qi = pl.program_id(2)
    tq = NSA_TQ
    s0 = qi * tq
    q = q_ref[0, 0, 0]
    o_c = oc_ref[0, 0, 0]
    past = bias_ref[0, 0, 0]

    wlen = WINDOW + tq

    def window(kvw, masked):
        sw = masked(_dot_nt(q, kvw[:, 0:HEAD_DIM]))
        pw = jnp.exp2(sw - jnp.max(sw, axis=-1, keepdims=True))
        return _dot(pw.astype(BF16), kvw)[:, HEAD_DIM:2 * HEAD_DIM] * (1.0 / jnp.sum(pw, axis=-1, keepdims=True))

    def window_interior():
        def masked(sw):
            rq = lax.broadcasted_iota(jnp.int32, (sw.shape[0], tq), 0) & (tq - 1)
            c = lax.broadcasted_iota(jnp.int32, (sw.shape[0], tq), 1)
            return jnp.concatenate([jnp.where(c > rq, sw[:, 0:tq], NEG), sw[:, tq:WINDOW],
                                    jnp.where(c <= rq, sw[:, WINDOW:wlen], NEG)], axis=1)

        return window(kvw_ref[0, 0, pl.ds(pl.multiple_of(s0 - WINDOW, tq), wlen), :], masked)

    def window_start():
        def masked(sw):
            kabs = lax.broadcasted_iota(jnp.int32, sw.shape, 1)
            t = s0 + (lax.broadcasted_iota(jnp.int32, sw.shape, 0) & (tq - 1))
            return jnp.where((kabs <= t) & (kabs > t - WINDOW), sw, NEG)

        return window(kvw_ref[0, 0, 0:wlen, :], masked)

    for st in range(n_super):
        b = past[:, st * SEL_LANES:(st + 1) * SEL_LANES]
        qa_ref[st] = jnp.concatenate([q, jnp.concatenate([b] * NSA_GROUP, axis=0)], axis=1)

    _softmax_init(m_ref, acc_ref)
    d0 = pl.multiple_of(s0, tq)
    s = _dot_nt(q, ka_ref[0, 0, pl.ds(d0, tq), :][:, 0:HEAD_DIM])
    qpos = lax.broadcasted_iota(jnp.int32, s.shape, 0) & (tq - 1)
    kpos = lax.broadcasted_iota(jnp.int32, s.shape, 1)
    _softmax_step(jnp.where(kpos <= qpos, s, NEG), va_ref[0, 0, pl.ds(d0, tq), :], m_ref, acc_ref)

    def step(start, size):
        s = _dot_nt(qa_ref[start // SUPER_KEYS], ka_ref[0, 0, pl.ds(start, size), :])
        _softmax_step(s, va_ref[0, 0, pl.ds(start, size), :], m_ref, acc_ref)

    (o_w,) = _past_keys_loop(s0, NSA_TK, step, parts=4, riders=(
        (window_interior, lambda: lax.cond(s0 >= WINDOW, window_interior, window_start)),))
    o_s = _softmax_result(acc_ref)

    w = NSA_GROUP * HEAD_DIM
    gexp = _dot_x2(gt_ref[...], e_ref[0])
    wide = lambda x: jnp.concatenate([x[g * tq:(g + 1) * tq] for g in range(NSA_GROUP)], axis=1)
    o_ref[...] = (gexp[:, 0:w] * wide(o_c) + gexp[:, w:2 * w] * wide(o_s) + gexp[:, 2 * w:3 * w] * wide(o_w)).astype(BF16)


def _nsa(qs, kcmp, vcmp, ka, va, kvw, gates, batch, seq):
    tq = NSA_TQ
    nq = seq // tq
    nb = seq // SEL_BLOCK
    n16 = seq // CMP_STRIDE
    assert seq % SUPER_KEYS == 0 and tq == 2 * SEL_BLOCK and WINDOW % tq == 0 and NSA_TK >= WINDOW
    n_super = seq // SUPER_KEYS
    rows = NSA_GROUP * tq
    e = np.zeros((NSA_KV_HEADS, GATE_LANES, 3 * NSA_GROUP * HEAD_DIM), np.float32)
    for br in range(3):
        for hk in range(NSA_KV_HEADS):
            for g in range(NSA_GROUP):
                c = (br * NSA_GROUP + g) * HEAD_DIM
                e[hk, br * NSA_HEADS + hk * NSA_GROUP + g, c:c + HEAD_DIM] = 1.0
    nn, jj = np.arange(n16)[:, None], np.arange(nb)[None, :]
    band = ((nn >= 4 * jj - 1) & (nn <= 4 * jj + 3)).astype(np.float32)
    resident = lambda width: pl.BlockSpec((1, 1, seq, width), lambda b, h, i: (b, h, 0, 0))
    tile5 = lambda r, c: pl.BlockSpec((1, 1, 1, r, c), lambda b, h, i: (b, h, i, 0, 0))
    w = NSA_GROUP * HEAD_DIM
    oc, bias = pl.pallas_call(
        _nsa_select_kernel,
        grid=(batch, NSA_KV_HEADS, nq),
        in_specs=[tile5(rows, HEAD_DIM),
                  pl.BlockSpec((1, 1, n16, HEAD_DIM), lambda b, h, i: (b, h, 0, 0)),
                  pl.BlockSpec((1, 1, n16, HEAD_DIM), lambda b, h, i: (b, h, 0, 0)),
                  pl.BlockSpec((n16, nb), lambda b, h, i: (0, 0))],
        out_specs=[tile5(rows, HEAD_DIM), tile5(tq, nb)],
        out_shape=[jax.ShapeDtypeStruct((batch, NSA_KV_HEADS, nq, rows, HEAD_DIM), F32),
                   jax.ShapeDtypeStruct((batch, NSA_KV_HEADS, nq, tq, nb), BF16)],
        compiler_params=_params("arbitrary", "arbitrary", "arbitrary"),
        name="nsa_select",
    )(qs, kcmp, vcmp, jnp.asarray(band, BF16))
    return pl.pallas_call(
        functools.partial(_nsa_kernel, n_super),
        grid=(batch, NSA_KV_HEADS, nq),
        in_specs=[tile5(rows, HEAD_DIM), tile5(rows, HEAD_DIM), tile5(tq, nb),
                  resident(LANES), resident(LANES), resident(LANES),
                  pl.BlockSpec((tq, GATE_LANES), lambda b, h, i: (b * nq + i, 0)),
                  pl.BlockSpec((1, GATE_LANES, 3 * w), lambda b, h, i: (h, 0, 0))],
        out_specs=pl.BlockSpec((tq, w), lambda b, h, i: (b * nq + i, h)),
        out_shape=jax.ShapeDtypeStruct((batch * seq, NSA_W), BF16),
        scratch_shapes=[pltpu.VMEM((n_super, rows, LANES), BF16), pltpu.VMEM((rows, LANES), F32),
                        pltpu.VMEM((rows, LANES), F32)],
        compiler_params=_params("arbitrary", "arbitrary", "arbitrary"),
        name="nsa",
    )(qs, oc, bias, ka, va, kvw, gates, jnp.asarray(e, BF16))


def _odd_out_kernel(om_ref, on_ref, h_ref, w_ref, mg_ref, wr_ref, br_ref, o_ref, gate_ref, grp_ref, cnt_ref):
    acc = h_ref[...] + _dot(on_ref[...], w_ref[MOBA_W:D_MODEL, :])
    for h in range(MOBA_HEADS):
        acc = acc + _dot(om_ref[0, h], w_ref[h * HEAD_DIM:(h + 1) * HEAD_DIM, :])
    o_ref[...] = acc
    _route(acc, mg_ref, wr_ref, br_ref, gate_ref, grp_ref, cnt_ref)


def _odd_out(o_moba, o_nsa, h2, w_out, router, seq, tm=ROW_TILE):
    t = h2.shape[0]
    tps = seq // tm
    row = lambda w: pl.BlockSpec((tm, w), lambda i: (i, 0))
    r_in, r_out, r_shape = _router_specs(t, tm)
    h, *routing = pl.pallas_call(
        _odd_out_kernel,
        grid=(t // tm,),
        in_specs=[pl.BlockSpec((1, MOBA_HEADS, tm, HEAD_DIM), lambda i: (i // tps, 0, i % tps, 0)),
                  row(NSA_W), row(D_MODEL), _full((D_MODEL, D_MODEL))] + r_in,
        out_specs=[row(D_MODEL)] + r_out,
        out_shape=[jax.ShapeDtypeStruct((t, D_MODEL), F32)] + r_shape,
        compiler_params=_params("arbitrary"),
        name="odd_out",
    )(o_moba, o_nsa, h2, w_out, *router)
    return h, routing


def _even_layer(h2, batch, seq, router, norm, w_in, w_out, lam_re, lam_im, log_dt, b_re, b_im, c_re, c_im, d, w_glu,
                conv_w, conv_b):
    u, ub, yb = _even_in(h2, norm.reshape(1, D_MODEL), w_in.astype(BF16), conv_w, conv_b.reshape(1, CONV_WIDTH), seq)
    ypre = _s5_mixer_pre(ub, batch, seq, lam_re, lam_im, log_dt, b_re, b_im, c_re, c_im)
    return _even_out(ypre, u, yb, h2, d.reshape(1, S5_WIDTH), w_glu.astype(BF16), w_out.astype(BF16), router)


def _odd_layer(h2, batch, seq, router, norm, w_in, w_out, moba_q_norm, moba_k_norm, nsa_q_norm, nsa_kcmp_norm, nsa_ksel_norm,
               nsa_kwin_norm, cmp_pe_k, cmp_w1_k, cmp_w2_k, cmp_pe_v, cmp_w1_v, cmp_w2_v):
    qm, kam, kmean, vam, qs, kvc, kas, vas, kvw, gates = _odd_in(
        h2, batch, seq, norm.reshape(1, D_MODEL), w_in, moba_q_norm, moba_k_norm, nsa_q_norm, nsa_ksel_norm, nsa_kwin_norm)
    cmp = _compress(kvc, batch, seq, cmp_pe_k, cmp_w1_k, cmp_w2_k, cmp_pe_v, cmp_w1_v, cmp_w2_v, nsa_kcmp_norm)
    o_moba = _moba(qm, kam, kmean, vam, batch, seq)
    o_nsa = _nsa(qs, cmp[0], cmp[1], kas, vas, kvw, gates, batch, seq)
    return _odd_out(o_moba, o_nsa, h2, w_out.astype(BF16), router, seq)


def kernel(x, ev_norm_mix, ev_w_in, ev_w_out, s5_lam_re, s5_lam_im, s5_log_dt, s5_b_re, s5_b_im, s5_c_re, s5_c_im, s5_d, s5_w_glu, conv_w, conv_b, od_norm_mix, od_w_in, od_w_out, moba_q_norm, moba_k_norm, nsa_q_norm, nsa_kcmp_norm, nsa_ksel_norm, nsa_kwin_norm, cmp_pe_k, cmp_w1_k, cmp_w2_k, cmp_pe_v, cmp_w1_v, cmp_w2_v, moe_norm, moe_w_group, moe_b_group, moe_w_expert, moe_b_expert, moe_w_gate, moe_w_up, moe_w_down):
    batch, seq, _ = x.shape
    depth = moe_norm.shape[0]
    h = x.reshape(batch * seq, D_MODEL)
    for layer in range(depth):
        i = layer // 2
        router = _router_operands(moe_norm[layer], moe_w_group[layer], moe_b_group[layer], moe_w_expert[layer],
                                  moe_b_expert[layer])
        if layer % 2 == 0:
            h, routing = _even_layer(h, batch, seq, router, ev_norm_mix[i], ev_w_in[i], ev_w_out[i], s5_lam_re[i],
                                     s5_lam_im[i], s5_log_dt[i], s5_b_re[i], s5_b_im[i], s5_c_re[i], s5_c_im[i], s5_d[i],
                                     s5_w_glu[i], conv_w[i], conv_b[i])
        else:
            h, routing = _odd_layer(h, batch, seq, router, od_norm_mix[i], od_w_in[i], od_w_out[i], moba_q_norm[i],
                                    moba_k_norm[i], nsa_q_norm[i], nsa_kcmp_norm[i], nsa_ksel_norm[i], nsa_kwin_norm[i],
                                    cmp_pe_k[i], cmp_w1_k[i], cmp_w2_k[i], cmp_pe_v[i], cmp_w1_v[i], cmp_w2_v[i])
        h = _moe(h, moe_norm[layer].reshape(1, D_MODEL), routing, moe_w_gate[layer], moe_w_up[layer], moe_w_down[layer])
    return h.reshape(batch, seq, D_MODEL)
```

```python
import functools
import math

import jax
import jax.numpy as jnp
import numpy as np
from jax import lax
from jax.experimental import pallas as pl
from jax.experimental.pallas import tpu as pltpu

D_MODEL = 1024
HEAD_DIM = 64
EPS = 1e-6
S5_WIDTH = 256
S5_GROUP = 16
S5_GROUPS = 16
S5_STATE = 64
S5_CHUNK = 16
S5_GROUP_BITS = 4
S5_STATE_BITS = 6
S5_ROW_TILE = 512
S5_COL_TILE = 1024
S5_SCAN_TILE = 256
CONV_WIDTH = 768
CONV_K = 3
MOBA_HEADS = 4
NSA_HEADS = 12
NSA_KV_HEADS = 2
NSA_GROUP = 6
MOBA_W = 256
NSA_W = 768
KV_W = 128
MOBA_BLOCK = 256
MOBA_TOPK = 3
CMP_BLOCK = 32
CMP_STRIDE = 16
CMP_HIDDEN = 256
SEL_BLOCK = 64
SEL_TOPK = 8
WINDOW = 512
N_GROUPS = 4
EXPERTS_PER_GROUP = 4
N_EXPERTS = 16
EXPERT_FF = 256

LANES = 128
SUBLANES = 8
VMEM_LIMIT_BYTES = 56 * 1024 * 1024
ROW_TILE = 1024
GATE_LANES = LANES
NEG = -float(2 ** 100)
Q_SCALE = HEAD_DIM ** -0.5 * math.log2(math.e)
F32 = jnp.float32
BF16 = jnp.bfloat16


def _params(*semantics):
    return pltpu.CompilerParams(dimension_semantics=semantics, vmem_limit_bytes=VMEM_LIMIT_BYTES)


def _dot(a, b):
    return jnp.dot(a, b, preferred_element_type=F32)


def _dot_nt(a, b):
    return lax.dot_general(a, b, (((1,), (1,)), ((), ())), preferred_element_type=F32)


def _split(x):
    hi = x.astype(BF16)
    lo = (x - hi.astype(F32)).astype(BF16)
    return hi, lo


def _dot_x2(x, w):
    hi, lo = _split(x)
    return _dot(hi, w) + _dot(lo, w)


def _dot_x3(x, w_hilo):
    n = w_hilo.shape[1] // 2
    hi, lo = _split(x)
    both = _dot(hi, w_hilo)
    return both[:, 0:n] + (both[:, n:2 * n] + _dot(lo, w_hilo[:, 0:n]))


def _rms(x, gain):
    return x * lax.rsqrt(jnp.mean(x * x, axis=-1, keepdims=True) + EPS) * gain


def _gelu(x):
    return 0.5 * x * (1.0 + jnp.tanh(math.sqrt(2.0 / math.pi) * (x + 0.044715 * (x * x * x))))


def _sigmoid(x):
    return 1.0 / (1.0 + jnp.exp(-x))


def _full(shape):
    n = len(shape)
    return pl.BlockSpec(shape, lambda *_: (0,) * n)


def _even_in_kernel(tiles_per_seq, x_ref, g_ref, w_ref, cw_ref, cb_ref, u_ref, ub_ref, yb_ref, carry_ref):
    i = pl.program_id(0)
    xn = _rms(x_ref[...], g_ref[...]).astype(BF16)
    u = _dot(xn, w_ref[:, 0:S5_WIDTH])
    u_ref[...] = u
    ub_ref[...] = u.astype(BF16)
    o = S5_WIDTH
    xc = _dot(xn, w_ref[:, o:o + CONV_WIDTH])
    gb = _dot(xn, w_ref[:, o + CONV_WIDTH:o + 2 * CONV_WIDTH])
    gc = _dot(xn, w_ref[:, o + 2 * CONV_WIDTH:o + 3 * CONV_WIDTH])
    z = gc * xc
    tm = z.shape[0]

    @pl.when(i % tiles_per_seq == 0)
    def _():
        carry_ref[...] = jnp.zeros_like(carry_ref)

    row = lax.broadcasted_iota(jnp.int32, z.shape, 0)
    prev1 = carry_ref[SUBLANES - 1:SUBLANES, :]
    prev2 = carry_ref[SUBLANES - 2:SUBLANES - 1, :]
    z1 = jnp.where(row == 0, prev1, pltpu.roll(z, 1, 0))
    z2 = jnp.where(row == 0, prev2, jnp.where(row == 1, prev1, pltpu.roll(z, 2, 0)))
    y = cw_ref[0:1, :] * z2 + cw_ref[1:2, :] * z1 + cw_ref[2:3, :] * z + cb_ref[...]
    yb_ref[...] = (gb * y).astype(BF16)
    carry_ref[...] = z[tm - SUBLANES:tm, :]


def _even_in(x2, gain, w_in, conv_w, conv_b, seq, tm=ROW_TILE):
    t = x2.shape[0]
    n_in = w_in.shape[1]
    return pl.pallas_call(
        functools.partial(_even_in_kernel, seq // tm),
        grid=(t // tm,),
        in_specs=[pl.BlockSpec((tm, D_MODEL), lambda i: (i, 0)), _full((1, D_MODEL)),
                  _full((D_MODEL, n_in)), _full((CONV_K, CONV_WIDTH)), _full((1, CONV_WIDTH))],
        out_specs=[pl.BlockSpec((tm, S5_WIDTH), lambda i: (i, 0)), pl.BlockSpec((tm, S5_WIDTH), lambda i: (i, 0)),
                   pl.BlockSpec((tm, CONV_WIDTH), lambda i: (i, 0))],
        out_shape=[jax.ShapeDtypeStruct((t, S5_WIDTH), F32), jax.ShapeDtypeStruct((t, S5_WIDTH), BF16),
                   jax.ShapeDtypeStruct((t, CONV_WIDTH), BF16)],
        scratch_shapes=[pltpu.VMEM((SUBLANES, CONV_WIDTH), F32)],
        compiler_params=_params("arbitrary"),
        name="even_in",
    )(x2, gain, w_in, conv_w, conv_b)


def _s5_weights(lam_re, lam_im, log_dt, b_re, b_im, c_re, c_im):
    g, p, hg, ck = S5_GROUPS, S5_STATE, S5_GROUP, S5_CHUNK
    lr, li = lam_re.astype(F32), lam_im.astype(F32)
    dt = jnp.exp(log_dt.astype(F32))[:, None]
    mag = jnp.exp(lr * dt)
    a_re, a_im = mag * jnp.cos(li * dt), mag * jnp.sin(li * dt)
    den = lr * lr + li * li
    f_re = ((a_re - 1.0) * lr + a_im * li) / den
    f_im = (a_im * lr - (a_re - 1.0) * li) / den
    br, bi = b_re.astype(F32), b_im.astype(F32)
    bb_re = f_re[..., None] * br - f_im[..., None] * bi
    bb_im = f_re[..., None] * bi + f_im[..., None] * br
    pw_re, pw_im = [jnp.ones_like(a_re)], [jnp.zeros_like(a_im)]
    for _ in range(ck):
        r, m = pw_re[-1], pw_im[-1]
        pw_re.append(r * a_re - m * a_im)
        pw_im.append(r * a_im + m * a_re)
    pw_re, pw_im = jnp.stack(pw_re), jnp.stack(pw_im)
    cr, ci = c_re.astype(F32), c_im.astype(F32)
    rev_re, rev_im = pw_re[ck - 1::-1][:ck], pw_im[ck - 1::-1][:ck]
    ws_re = rev_re[:, :, :, None] * bb_re[None] - rev_im[:, :, :, None] * bb_im[None]
    ws_im = rev_re[:, :, :, None] * bb_im[None] + rev_im[:, :, :, None] * bb_re[None]
    ca_re = cr[None] * pw_re[1:, :, None, :] - ci[None] * pw_im[1:, :, None, :]
    ca_im = cr[None] * pw_im[1:, :, None, :] + ci[None] * pw_re[1:, :, None, :]
    cb_re = jnp.einsum('ghp,kgp,gpj->kghj', cr, pw_re[:ck], bb_re) - jnp.einsum('ghp,kgp,gpj->kghj', cr, pw_im[:ck], bb_im) \
        - jnp.einsum('ghp,kgp,gpj->kghj', ci, pw_re[:ck], bb_im) - jnp.einsum('ghp,kgp,gpj->kghj', ci, pw_im[:ck], bb_re)
    lag = np.arange(ck)[None, :] - np.arange(ck)[:, None]
    tz = cb_re[np.clip(lag, 0, ck - 1)]
    tz = jnp.where((lag >= 0)[:, :, None, None, None], tz, 0.0)
    cw = ck * g * hg
    ws = jnp.stack([ws_re, ws_im]).transpose(1, 2, 4, 0, 3).reshape(cw, 2 * p)
    wc = jnp.stack([ca_re, -ca_im]).transpose(0, 2, 4, 1, 3).reshape(2 * g * p, ck * hg)
    tzc = tz.transpose(0, 2, 4, 1, 3).reshape(cw, ck * hg)
    return ws.astype(BF16), wc.astype(BF16), tzc.astype(BF16), pw_re[ck].reshape(1, g * p), pw_im[ck].reshape(1, g * p)


def _group_expand(compact, expand, row_shift, col_shift, col0):
    full = _dot(compact, expand)
    row = lax.broadcasted_iota(jnp.int32, full.shape, 0)
    col = col0 + lax.broadcasted_iota(jnp.int32, full.shape, 1)
    same = ((row >> row_shift) & (S5_GROUPS - 1)) == ((col >> col_shift) & (S5_GROUPS - 1))
    return jnp.where(same, full, 0.0).astype(BF16)


def _s5_state_kernel(u_ref, ws_ref, e_ref, s_ref, w_scr):
    @pl.when(pl.program_id(1) == 0)
    def _():
        w_scr[...] = _group_expand(ws_ref[...], e_ref[...], S5_GROUP_BITS, S5_STATE_BITS,
                                   pl.program_id(0) * w_scr.shape[1])

    s_ref[...] = _dot(u_ref[...], w_scr[...])


def _s5_scan_kernel(s_ref, are_ref, aim_ref, xprev_ref, st_ref):
    @pl.when(pl.program_id(0) == 0)
    def _():
        st_ref[...] = jnp.zeros_like(st_ref)

    a_re, a_im = are_ref[...], aim_ref[...]
    nb, n = s_ref.shape[0], s_ref.shape[1]
    half = a_re.shape[1]

    def body(c, carry):
        out = []
        for b in range(nb):
            xr, xi = carry[2 * b], carry[2 * b + 1]
            xprev_ref[b, pl.ds(c, 1), 0:half] = xr
            xprev_ref[b, pl.ds(c, 1), half:2 * half] = xi
            s = s_ref[b, pl.ds(c, 1), :]
            out += [a_re * xr - a_im * xi + s[:, 0:half], a_re * xi + a_im * xr + s[:, half:2 * half]]
        return tuple(out)

    init = tuple(st_ref[b:b + 1, o:o + half] for b in range(nb) for o in (0, half))
    final = lax.fori_loop(0, n, body, init, unroll=4)
    for b in range(nb):
        st_ref[b:b + 1, 0:half] = final[2 * b]
        st_ref[b:b + 1, half:2 * half] = final[2 * b + 1]


def _s5_out_kernel(u_ref, xp_ref, tz_ref, wc_ref, e_ref, y_ref, tz_scr, wc_scr):
    @pl.when(pl.program_id(1) == 0)
    def _():
        col0 = pl.program_id(0) * tz_scr.shape[1]
        tz_scr[...] = _group_expand(tz_ref[...], e_ref[...], S5_GROUP_BITS, S5_GROUP_BITS, col0)
        wc_scr[...] = _group_expand(wc_ref[...], e_ref[...], S5_STATE_BITS, S5_GROUP_BITS, col0)

    y_ref[...] = _dot(u_ref[...], tz_scr[...]) + _dot(xp_ref[...].astype(BF16), wc_scr[...])


def _expand_matrix(outer, inner):
    e = np.zeros((outer, inner, outer, S5_GROUPS, inner), np.float32)
    for x in range(outer):
        for y in range(inner):
            e[x, y, x, :, y] = 1.0
    return jnp.asarray(e.reshape(outer * inner, outer * S5_GROUPS * inner), BF16)


def _s5_mixer_pre(ub, batch, seq, lam_re, lam_im, log_dt, b_re, b_im, c_re, c_im):
    ws, wc, tz, a16_re, a16_im = _s5_weights(lam_re, lam_im, log_dt, b_re, b_im, c_re, c_im)
    nc = seq // S5_CHUNK
    rows = batch * nc
    cw = S5_CHUNK * S5_WIDTH
    sw = 2 * S5_GROUPS * S5_STATE
    assert S5_GROUP == 1 << S5_GROUP_BITS and S5_STATE == 1 << S5_STATE_BITS
    tr = min(rows, S5_ROW_TILE)
    tn = S5_COL_TILE
    uc = ub.reshape(rows, cw)
    e_state = _expand_matrix(2, S5_STATE)
    e_out = _expand_matrix(S5_CHUNK, S5_GROUP)
    s = pl.pallas_call(
        _s5_state_kernel,
        grid=(sw // tn, rows // tr),
        in_specs=[pl.BlockSpec((tr, cw), lambda j, i: (i, 0)), _full(ws.shape),
                  pl.BlockSpec((e_state.shape[0], tn), lambda j, i: (0, j))],
        out_specs=pl.BlockSpec((tr, tn), lambda j, i: (i, j)),
        out_shape=jax.ShapeDtypeStruct((rows, sw), F32),
        scratch_shapes=[pltpu.VMEM((cw, tn), BF16)],
        compiler_params=_params("arbitrary", "arbitrary"),
        name="s5_state",
    )(uc, ws, e_state)
    tc = min(nc, S5_SCAN_TILE)
    xprev = pl.pallas_call(
        _s5_scan_kernel,
        grid=(nc // tc,),
        in_specs=[pl.BlockSpec((batch, tc, sw), lambda i: (0, i, 0)), _full((1, sw // 2)), _full((1, sw // 2))],
        out_specs=pl.BlockSpec((batch, tc, sw), lambda i: (0, i, 0)),
        out_shape=jax.ShapeDtypeStruct((batch, nc, sw), F32),
        scratch_shapes=[pltpu.VMEM((batch, sw), F32)],
        compiler_params=_params("arbitrary"),
        name="s5_scan",
    )(s.reshape(batch, nc, sw), a16_re, a16_im)
    y = pl.pallas_call(
        _s5_out_kernel,
        grid=(cw // tn, rows // tr),
        in_specs=[pl.BlockSpec((tr, cw), lambda j, i: (i, 0)), pl.BlockSpec((tr, sw), lambda j, i: (i, 0)),
                  _full(tz.shape), _full(wc.shape), pl.BlockSpec((e_out.shape[0], tn), lambda j, i: (0, j))],
        out_specs=pl.BlockSpec((tr, tn), lambda j, i: (i, j)),
        out_shape=jax.ShapeDtypeStruct((rows, cw), F32),
        scratch_shapes=[pltpu.VMEM((cw, tn), BF16), pltpu.VMEM((sw, tn), BF16)],
        compiler_params=_params("arbitrary", "arbitrary"),
        name="s5_out",
    )(uc, xprev.reshape(rows, sw), tz, wc, e_out)
    return y.reshape(batch * seq, S5_WIDTH)


def _even_out_kernel(ypre_ref, u_ref, yb_ref, x_ref, d_ref, wglu_ref, wout_ref, mg_ref, wr_ref, br_ref,
                     o_ref, gate_ref, grp_ref, cnt_ref):
    y = _gelu(ypre_ref[...] + d_ref[...] * u_ref[...])
    y = y * _sigmoid(_dot(y.astype(BF16), wglu_ref[...]))
    h = (x_ref[...] + _dot(y.astype(BF16), wout_ref[0:S5_WIDTH, :]) + _dot(yb_ref[...], wout_ref[S5_WIDTH:D_MODEL, :]))
    o_ref[...] = h
    _route(h, mg_ref, wr_ref, br_ref, gate_ref, grp_ref, cnt_ref)


def _even_out(ypre, u, yb, x2, d, w_glu, w_out, router, tm=ROW_TILE):
    t = x2.shape[0]
    row = lambda w: pl.BlockSpec((tm, w), lambda i: (i, 0))
    r_in, r_out, r_shape = _router_specs(t, tm)
    h, *routing = pl.pallas_call(
        _even_out_kernel,
        grid=(t // tm,),
        in_specs=[row(S5_WIDTH), row(S5_WIDTH), row(CONV_WIDTH), row(D_MODEL), _full((1, S5_WIDTH)),
                  _full((S5_WIDTH, S5_WIDTH)), _full((D_MODEL, D_MODEL))] + r_in,
        out_specs=[row(D_MODEL)] + r_out,
        out_shape=[jax.ShapeDtypeStruct((t, D_MODEL), F32)] + r_shape,
        compiler_params=_params("arbitrary"),
        name="even_out",
    )(ypre, u, yb, x2, d, w_glu, w_out, *router)
    return h, routing


def _first_max(v, pos, width, axis=-1):
    m = jnp.max(v, axis=axis, keepdims=True)
    idx = jnp.min(jnp.where(v == m, pos, width), axis=axis, keepdims=True)
    return m, idx


MOE_TM = 1024
MOE_ALIGN = 16
MOE_SLOTS = 1152
MOE_WIN = 320
ROUTER_LANES = LANES


def _route(h, g_ref, wr_ref, br_ref, gate_ref, grp_ref, cnt_ref):
    xn = _rms(h, g_ref[...])
    logits = _dot_x3(xn, wr_ref[...]) + br_ref[...]
    lane = lax.broadcasted_iota(jnp.int32, logits.shape, 1).astype(F32)
    width = float(logits.shape[1])
    is_g = lane < N_GROUPS
    gl = jnp.where(is_g, logits, -jnp.inf)
    gm, gi = _first_max(gl, lane, width)
    gw = 1.0 / jnp.sum(jnp.where(is_g, jnp.exp(gl - gm), 0.0), axis=-1, keepdims=True)
    lo = N_GROUPS + gi * EXPERTS_PER_GROUP
    in_grp = (lane >= lo) & (lane < lo + EXPERTS_PER_GROUP)
    el = jnp.where(in_grp, logits, -jnp.inf)
    m1, i1 = _first_max(el, lane, width)
    m2, i2 = _first_max(jnp.where(lane == i1, -jnp.inf, el), lane, width)
    p2 = jnp.exp(m2 - m1)
    w1 = gw / (1.0 + p2)
    w2 = gw * p2 / (1.0 + p2)
    gate_ref[...] = jnp.where(lane == i1, w1, 0.0) + jnp.where(lane == i2, w2, 0.0)
    grp = (lane == gi).astype(F32)
    grp_ref[...] = grp.astype(BF16)
    cnt_ref[0] = jnp.broadcast_to(jnp.sum(grp, axis=0, keepdims=True), cnt_ref.shape[1:])


def _moe_expert_kernel(base_ref, nwin_ref, h_ref, g_ref, gate_ref, grp_ref, wg_ref, wu_ref, wd_ref, o_ref,
                       xs_ref, gs_ref, ys_ref, pt_ref):
    i, g = pl.program_id(0), pl.program_id(1)
    tm = h_ref.shape[0]

    @pl.when((i == 0) & (g == 0))
    def _():
        xs_ref[...] = jnp.zeros_like(xs_ref)
        gs_ref[...] = jnp.zeros_like(gs_ref)
        ys_ref[...] = jnp.zeros_like(ys_ref)

    @pl.when(g == 0)
    def _():
        xn = _rms(h_ref[...], g_ref[...]).astype(BF16)
        grp = grp_ref[...]
        earlier = (lax.broadcasted_iota(jnp.int32, (tm, tm), 0) > lax.broadcasted_iota(jnp.int32, (tm, tm), 1)).astype(BF16)
        rank = _dot(earlier, grp)
        lane = lax.broadcasted_iota(jnp.int32, rank.shape, 1)
        for k in range(N_GROUPS):
            rank = rank + jnp.where(lane == k, base_ref[i * N_GROUPS + k].astype(F32), 0.0)
        slot = jnp.sum(grp.astype(F32) * rank, axis=-1, keepdims=True).astype(jnp.int32)
        pt = (lax.broadcasted_iota(jnp.int32, (tm, MOE_SLOTS), 1) == slot).astype(BF16)
        pt_ref[...] = pt
        gather = lambda x: lax.dot_general(pt, x, (((0,), (0,)), ((), ())), preferred_element_type=F32)
        xs_ref[0:MOE_SLOTS, :] = gather(xn).astype(BF16)
        both = gather(jnp.concatenate(_split(gate_ref[...]), axis=1))
        gs_ref[0:MOE_SLOTS, :] = both[:, 0:ROUTER_LANES] + both[:, ROUTER_LANES:2 * ROUTER_LANES]

    def window(w, carry):
        r0 = pl.multiple_of(base_ref[i * N_GROUPS + g] + w * MOE_WIN, MOE_ALIGN)
        x = xs_ref[pl.ds(r0, MOE_WIN), :]
        gate = gs_ref[pl.ds(r0, MOE_WIN), :]
        lane = lax.broadcasted_iota(jnp.int32, gate.shape, 1)
        y = None
        for j in range(EXPERTS_PER_GROUP):
            ge = jnp.sum(jnp.where(lane == g * EXPERTS_PER_GROUP + (j + N_GROUPS), gate, 0.0), axis=-1, keepdims=True)
            h1 = _dot(x, wg_ref[j])
            h3 = _dot(x, wu_ref[j])
            act = (h1 * _sigmoid(h1)) * h3 * ge
            yj = _dot(act.astype(BF16), wd_ref[j])
            y = yj if y is None else y + yj
        ys_ref[pl.ds(r0, MOE_WIN), :] = y.astype(BF16)
        return carry

    lax.fori_loop(0, nwin_ref[i * N_GROUPS + g], window, 0)

    @pl.when(g == N_GROUPS - 1)
    def _():
        o_ref[...] = h_ref[...] + _dot(pt_ref[...], ys_ref[0:MOE_SLOTS, :])


def _router_operands(moe_gain, w_group, b_group, w_expert, b_expert):
    rw = ROUTER_LANES
    wr = jnp.zeros((D_MODEL, rw), F32).at[:, 0:N_GROUPS].set(w_group).at[:, N_GROUPS:N_GROUPS + N_EXPERTS].set(w_expert)
    br = jnp.zeros((1, rw), F32).at[0, 0:N_GROUPS].set(b_group).at[0, N_GROUPS:N_GROUPS + N_EXPERTS].set(b_expert)
    return moe_gain.reshape(1, D_MODEL), jnp.concatenate(_split(wr), axis=1), br


def _router_specs(t, tm):
    assert tm == MOE_TM and t % tm == 0
    rw = ROUTER_LANES
    row = lambda width: pl.BlockSpec((tm, width), lambda i: (i, 0))
    in_specs = [_full((1, D_MODEL)), _full((D_MODEL, 2 * rw)), _full((1, rw))]
    out_specs = [row(rw), row(rw), pl.BlockSpec((1, SUBLANES, rw), lambda i: (i, 0, 0))]
    out_shape = [jax.ShapeDtypeStruct((t, rw), F32), jax.ShapeDtypeStruct((t, rw), BF16),
                 jax.ShapeDtypeStruct((t // tm, SUBLANES, rw), F32)]
    return in_specs, out_specs, out_shape


def _moe(h2, gain, routing, w_gate, w_up, w_down):
    t = h2.shape[0]
    tm, rw = MOE_TM, ROUTER_LANES
    assert t % tm == 0 and MOE_SLOTS >= tm + N_GROUPS * (MOE_ALIGN - 1) and MOE_WIN % MOE_ALIGN == 0
    n_tiles = t // tm
    gates, grp, cnt = routing
    n = cnt[:, 0, 0:N_GROUPS].astype(jnp.int32)
    padded = (n + (MOE_ALIGN - 1)) // MOE_ALIGN * MOE_ALIGN
    base = (jnp.cumsum(padded, axis=1) - padded).reshape(-1)
    nwin = ((padded + (MOE_WIN - 1)) // MOE_WIN).reshape(-1)
    tile = lambda width: pl.BlockSpec((tm, width), lambda i, g, *_: (i, 0))
    experts = lambda shape: pl.BlockSpec((EXPERTS_PER_GROUP,) + shape, lambda i, g, *_: (g, 0, 0))
    slots = MOE_SLOTS + MOE_WIN
    return pl.pallas_call(
        _moe_expert_kernel,
        grid_spec=pltpu.PrefetchScalarGridSpec(
            num_scalar_prefetch=2,
            grid=(n_tiles, N_GROUPS),
            in_specs=[tile(D_MODEL), pl.BlockSpec((1, D_MODEL), lambda i, g, *_: (0, 0)), tile(rw), tile(rw),
                      experts((D_MODEL, EXPERT_FF)), experts((D_MODEL, EXPERT_FF)), experts((EXPERT_FF, D_MODEL))],
            out_specs=tile(D_MODEL),
            scratch_shapes=[pltpu.VMEM((slots, D_MODEL), BF16), pltpu.VMEM((slots, rw), F32),
                            pltpu.VMEM((slots, D_MODEL), BF16), pltpu.VMEM((tm, MOE_SLOTS), BF16)]),
        out_shape=jax.ShapeDtypeStruct((t, D_MODEL), F32),
        compiler_params=_params("arbitrary", "arbitrary"),
        name="moe",
    )(base, nwin, h2, gain, gates, grp, w_gate.astype(BF16), w_up.astype(BF16), w_down.astype(BF16))


ODD_SPLITS = (MOBA_W, MOBA_W, MOBA_W, NSA_W, KV_W, KV_W, KV_W, KV_W, KV_W, KV_W, GATE_LANES)
ODD_IN_PAD = sum(ODD_SPLITS)


def _head_rms(x, hsum, gain):
    w = x.shape[1]
    ss = jnp.concatenate([_dot_x2(x[:, o:o + hsum.shape[0]] * x[:, o:o + hsum.shape[0]], hsum)
                          for o in range(0, w, hsum.shape[0])], axis=1) if w > hsum.shape[0] else _dot_x2(x * x, hsum)
    return x * lax.rsqrt(ss * (1.0 / HEAD_DIM) + EPS) * gain


def _odd_in_kernel(tiles_per_seq, x_ref, g_ref, w_ref, hsum_ref, gq_ref, gk_ref, gnq_ref, gks_ref, gkw_ref,
                   qm_ref, kam_ref, kmean_ref, vam_ref, qs_ref, kvc_ref, kas_ref, vas_ref, kvw_ref, gt_ref):
    xn = _rms(x_ref[...], g_ref[...]).astype(BF16)
    offs = np.cumsum((0,) + ODD_SPLITS)
    col = lambda j: _dot(xn, w_ref[:, int(offs[j]):int(offs[j + 1])])
    head = lambda x, h: x[:, h * HEAD_DIM:(h + 1) * HEAD_DIM]
    hsum = hsum_ref[...]
    hsum128 = hsum_ref[0:KV_W, 0:KV_W]
    tm = x_ref.shape[0]
    pos = (pl.program_id(0) % tiles_per_seq) * tm + lax.broadcasted_iota(jnp.int32, (tm, HEAD_DIM), 0)
    lane = lax.broadcasted_iota(jnp.int32, (tm, HEAD_DIM), 1)
    ones_col = (lane == 0).astype(BF16)

    qm = _head_rms(col(0), hsum, gq_ref[...])
    km = _head_rms(col(1), hsum, gk_ref[...])
    for j in range(tm // MOBA_BLOCK):
        kmean_ref[0, j:j + 1, :] = jnp.mean(km[j * MOBA_BLOCK:(j + 1) * MOBA_BLOCK, :], axis=0, keepdims=True)
    km = km.astype(BF16)
    vm = col(2).astype(BF16)
    moba_id = (lane == pos // MOBA_BLOCK).astype(BF16)
    for h in range(MOBA_HEADS):
        qm_ref[0, h] = head(qm, h)
        kam_ref[0, h] = jnp.concatenate([head(km, h), moba_id], axis=1)
        vam_ref[0, h] = jnp.concatenate([head(vm, h), ones_col], axis=1)

    qd = (_head_rms(col(3), hsum, gnq_ref[...]) * Q_SCALE).astype(BF16)
    for hk in range(NSA_KV_HEADS):
        for j in range(tm // NSA_TQ):
            for g in range(NSA_GROUP):
                qs_ref[0, hk, j, g * NSA_TQ:(g + 1) * NSA_TQ, :] = head(qd, hk * NSA_GROUP + g)[j * NSA_TQ:(j + 1) * NSA_TQ, :]
    kvc_ref[0] = col(4)
    kvc_ref[1] = col(5)
    ks = _head_rms(col(6), hsum128, gks_ref[...]).astype(BF16)
    vs = col(7).astype(BF16)
    kw = _head_rms(col(8), hsum128, gkw_ref[...]).astype(BF16)
    vw = col(9).astype(BF16)
    sel_id = (lane == (pos // SEL_BLOCK) % SEL_LANES).astype(BF16)
    for hk in range(NSA_KV_HEADS):
        kas_ref[0, hk] = jnp.concatenate([head(ks, hk), sel_id], axis=1)
        vas_ref[0, hk] = jnp.concatenate([head(vs, hk), ones_col], axis=1)
        kvw_ref[0, hk] = jnp.concatenate([head(kw, hk), head(vw, hk)], axis=1)
    gt_ref[...] = _sigmoid(col(10))


def _odd_in(h2, batch, seq, gain, w_in, moba_q_norm, moba_k_norm, nsa_q_norm, nsa_ksel_norm, nsa_kwin_norm, tm=ROW_TILE):
    t = h2.shape[0]
    assert MOBA_LANES == HEAD_DIM and SEL_LANES == HEAD_DIM and seq % tm == 0 and tm % MOBA_BLOCK == 0
    tps = seq // tm
    w = jnp.pad(w_in, ((0, 0), (0, ODD_IN_PAD - w_in.shape[1]))).astype(BF16)
    hsum = jnp.asarray(np.kron(np.eye(MOBA_W // HEAD_DIM), np.ones((HEAD_DIM, HEAD_DIM))), BF16)
    tile = lambda g, width: jnp.tile(g.astype(F32), width // HEAD_DIM).reshape(1, width)
    row = lambda width: pl.BlockSpec((tm, width), lambda i: (i, 0))
    heads = lambda n, width: pl.BlockSpec((1, n, tm, width), lambda i: (i // tps, 0, i % tps, 0))
    nmb = tm // MOBA_BLOCK
    nqt = tm // NSA_TQ
    rows = NSA_GROUP * NSA_TQ
    sds = jax.ShapeDtypeStruct
    out_specs = [heads(MOBA_HEADS, HEAD_DIM), heads(MOBA_HEADS, LANES), pl.BlockSpec((1, nmb, MOBA_W), lambda i: (i, 0, 0)),
                 heads(MOBA_HEADS, LANES),
                 pl.BlockSpec((1, NSA_KV_HEADS, nqt, rows, HEAD_DIM), lambda i: (i // tps, 0, i % tps, 0, 0)),
                 pl.BlockSpec((2, tm, KV_W), lambda i: (0, i, 0)),
                 heads(NSA_KV_HEADS, LANES), heads(NSA_KV_HEADS, LANES), heads(NSA_KV_HEADS, LANES),
                 row(GATE_LANES)]
    out_shape = [sds((batch, MOBA_HEADS, seq, HEAD_DIM), F32), sds((batch, MOBA_HEADS, seq, LANES), BF16),
                 sds((t // tm, nmb, MOBA_W), F32), sds((batch, MOBA_HEADS, seq, LANES), BF16),
                 sds((batch, NSA_KV_HEADS, seq // NSA_TQ, rows, HEAD_DIM), BF16),
                 sds((2, t, KV_W), F32), sds((batch, NSA_KV_HEADS, seq, LANES), BF16),
                 sds((batch, NSA_KV_HEADS, seq, LANES), BF16), sds((batch, NSA_KV_HEADS, seq, LANES), BF16),
                 sds((t, GATE_LANES), F32)]
    return pl.pallas_call(
        functools.partial(_odd_in_kernel, tps),
        grid=(t // tm,),
        in_specs=[row(D_MODEL), _full((1, D_MODEL)), _full((D_MODEL, ODD_IN_PAD)), _full((MOBA_W, MOBA_W)),
                  _full((1, MOBA_W)), _full((1, MOBA_W)), _full((1, NSA_W)), _full((1, KV_W)), _full((1, KV_W))],
        out_specs=out_specs,
        out_shape=out_shape,
        compiler_params=_params("arbitrary"),
        name="odd_in",
    )(h2, gain, w, hsum, tile(moba_q_norm, MOBA_W), tile(moba_k_norm, MOBA_W), tile(nsa_q_norm, NSA_W),
      tile(nsa_ksel_norm, KV_W), tile(nsa_kwin_norm, KV_W))


def _compress_kernel(c_ref, w1_ref, w2_ref, pe_ref, g_ref, o_ref):
    kind = pl.program_id(0)
    n16 = c_ref.shape[1] // CMP_STRIDE
    half = CMP_STRIDE * HEAD_DIM
    peb = _dot(pe_ref[0], w1_ref[0])[0:1, :]
    xs = [c_ref[0, pl.ds(s, n16, stride=CMP_STRIDE), :].astype(BF16) for s in range(CMP_STRIDE)]
    for h in range(NSA_KV_HEADS):
        first = second = None
        for s in range(CMP_STRIDE):
            x = xs[s][:, h * HEAD_DIM:(h + 1) * HEAD_DIM]
            a = _dot(x, w1_ref[0, s * HEAD_DIM:(s + 1) * HEAD_DIM, :])
            b = _dot(x, w1_ref[0, half + s * HEAD_DIM:half + (s + 1) * HEAD_DIM, :])
            first = a if first is None else first + a
            second = b if second is None else second + b
        hid = _gelu(first + pltpu.roll(second, n16 - 1, 0) + peb)
        out = _dot(hid.astype(BF16), w2_ref[0])
        o_ref[0, 0, h] = jnp.where(kind == 0, _rms(out, g_ref[...]), out).astype(BF16)


def _compress(kvc, batch, seq, pe_k, w1_k, w2_k, pe_v, w1_v, w2_v, kcmp_norm):
    n16 = seq // CMP_STRIDE
    half = CMP_STRIDE * HEAD_DIM
    w1 = jnp.stack([w1_k, w1_v]).astype(BF16)
    w2 = jnp.stack([w2_k, w2_v]).astype(BF16)
    pe = jnp.stack([pe_k, pe_v]).reshape(2, 1, 2 * half)
    pe = jnp.broadcast_to(pe, (2, SUBLANES, 2 * half)).astype(BF16)
    return pl.pallas_call(
        _compress_kernel,
        grid=(2, batch),
        in_specs=[pl.BlockSpec((1, seq, KV_W), lambda k, b: (k, b, 0)),
                  pl.BlockSpec((1, 2 * half, CMP_HIDDEN), lambda k, b: (k, 0, 0)),
                  pl.BlockSpec((1, CMP_HIDDEN, HEAD_DIM), lambda k, b: (k, 0, 0)),
                  pl.BlockSpec((1, SUBLANES, 2 * half), lambda k, b: (k, 0, 0)),
                  _full((1, HEAD_DIM))],
        out_specs=pl.BlockSpec((1, 1, NSA_KV_HEADS, n16, HEAD_DIM), lambda k, b: (k, b, 0, 0, 0)),
        out_shape=jax.ShapeDtypeStruct((2, batch, NSA_KV_HEADS, n16, HEAD_DIM), BF16),
        compiler_params=_params("arbitrary", "arbitrary"),
        name="nsa_compress",
    )(kvc, w1, w2, pe, kcmp_norm.astype(F32).reshape(1, HEAD_DIM))


M_INIT = -1e30


def _softmax_init(m_ref, acc_ref):
    m_ref[...] = jnp.full(m_ref.shape, M_INIT, F32)
    acc_ref[...] = jnp.zeros(acc_ref.shape, F32)


def _softmax_step(s, v_aug, m_ref, acc_ref):
    m_old = m_ref[...]
    m_new = jnp.maximum(m_old, jnp.max(s, axis=-1, keepdims=True))
    alpha = jnp.exp2(m_old - m_new)
    p = jnp.exp2(s - jnp.tile(m_new, (1, s.shape[1] // LANES)))
    acc_ref[...] = alpha * acc_ref[...] + _dot(p.astype(BF16), v_aug)
    m_ref[...] = m_new


def _past_keys_loop(n_keys, tile, step, riders=(), parts=2):
    n_full = n_keys // tile

    def body(j, carry):
        step(pl.multiple_of(j * tile, tile), tile)
        return carry

    extras = []
    for k, (with_tile, alone) in enumerate(riders):
        def both(with_tile=with_tile, k=k):
            out = with_tile()
            step(k * tile, tile)
            return out

        extras.append(lax.cond(n_full > k, both, alone))
    lax.fori_loop(len(riders), n_full, body, 0)
    rest = n_keys - n_full * tile
    start = pl.multiple_of(n_full * tile, tile)
    part = tile // parts
    n_parts = (rest + part - 1) // part
    for k in range(1, parts + 1):
        @pl.when(n_parts == k)
        def _(k=k):
            done = 0
            for size in (tile, tile // 2, tile // 4):
                if size % part == 0 and k * part - done >= size:
                    step(pl.multiple_of(start + done, part), size)
                    done += size
            assert done == k * part

    return extras


def _softmax_result(acc_ref):
    acc = acc_ref[...]
    return acc[:, 0:HEAD_DIM] * (1.0 / acc[:, HEAD_DIM:HEAD_DIM + 1])


def _pick_top(score, pos, width, k):
    sel = jnp.zeros(score.shape, jnp.bool_)
    for _ in range(k):
        m, idx = _first_max(score, pos, width, axis=0)
        hit = (pos == idx) & (m > -jnp.inf)
        sel = sel | hit
        score = jnp.where(pos == idx, -jnp.inf, score)
    return sel


MOBA_LANES = 64
MOBA_TK = 2048


MOBA_TQ = 4 * MOBA_BLOCK


def _moba_kernel(q_ref, ka_ref, va_ref, kmean_ref, o_ref, qa_ref, m_ref, acc_ref):
    i0 = pl.program_id(2) * (MOBA_TQ // MOBA_BLOCK)
    q = q_ref[0, 0]
    q_hi, q_lo = _split(q)
    km_hi, km_lo = _split(kmean_ref[0, 0])
    gate = _dot_nt(km_hi, q_hi) + (_dot_nt(km_lo, q_hi) + _dot_nt(km_hi, q_lo))
    blk = lax.broadcasted_iota(jnp.int32, gate.shape, 0)
    cur = i0 + lax.broadcasted_iota(jnp.int32, gate.shape, 1) // MOBA_BLOCK
    sel = _pick_top(jnp.where(blk < cur, gate, -jnp.inf), blk, gate.shape[0], MOBA_TOPK)
    qs = (q * Q_SCALE).astype(BF16)
    past = jnp.where(sel & (blk < i0), 0.0, NEG).T[:, 0:MOBA_LANES]
    own = jnp.where((blk == cur) | (sel & (blk >= i0)), 0.0, NEG).T[:, 0:MOBA_LANES]
    qa_ref[0] = jnp.concatenate([qs, past.astype(BF16)], axis=1)
    qa_ref[1] = jnp.concatenate([qs, own.astype(BF16)], axis=1)
    _softmax_init(m_ref, acc_ref)

    def step(start, size):
        s = _dot_nt(qa_ref[0], ka_ref[0, 0, pl.ds(start, size), :])
        _softmax_step(s, va_ref[0, 0, pl.ds(start, size), :], m_ref, acc_ref)

    _past_keys_loop(i0 * MOBA_BLOCK, MOBA_TK, step)
    start = pl.multiple_of(i0 * MOBA_BLOCK, MOBA_TQ)
    s = _dot_nt(qa_ref[1], ka_ref[0, 0, pl.ds(start, MOBA_TQ), :])
    qpos = lax.broadcasted_iota(jnp.int32, s.shape, 0)
    kpos = lax.broadcasted_iota(jnp.int32, s.shape, 1)
    hidden = (qpos // MOBA_BLOCK == kpos // MOBA_BLOCK) & (kpos > qpos)
    _softmax_step(jnp.where(hidden, NEG, s), va_ref[0, 0, pl.ds(start, MOBA_TQ), :], m_ref, acc_ref)
    o_ref[0, 0] = _softmax_result(acc_ref).astype(BF16)


def _moba(qm, ka, kmean, va, batch, seq):
    nmb = seq // MOBA_BLOCK
    assert nmb <= MOBA_LANES and seq % MOBA_TK == 0 and seq % MOBA_TQ == 0
    kmean = kmean.reshape(batch, nmb, MOBA_HEADS, HEAD_DIM).transpose(0, 2, 1, 3)
    kmean = jnp.pad(kmean, ((0, 0), (0, 0), (0, LANES - nmb), (0, 0)))
    return pl.pallas_call(
        _moba_kernel,
        grid=(batch, MOBA_HEADS, seq // MOBA_TQ),
        in_specs=[pl.BlockSpec((1, 1, MOBA_TQ, HEAD_DIM), lambda b, h, i: (b, h, i, 0)),
                  pl.BlockSpec((1, 1, seq, LANES), lambda b, h, i: (b, h, 0, 0)),
                  pl.BlockSpec((1, 1, seq, LANES), lambda b, h, i: (b, h, 0, 0)),
                  pl.BlockSpec((1, 1, LANES, HEAD_DIM), lambda b, h, i: (b, h, 0, 0))],
        out_specs=pl.BlockSpec((1, 1, MOBA_TQ, HEAD_DIM), lambda b, h, i: (b, h, i, 0)),
        out_shape=jax.ShapeDtypeStruct((batch, MOBA_HEADS, seq, HEAD_DIM), BF16),
        scratch_shapes=[pltpu.VMEM((2, MOBA_TQ, LANES), BF16), pltpu.VMEM((MOBA_TQ, LANES), F32),
                        pltpu.VMEM((MOBA_TQ, LANES), F32)],
        compiler_params=_params("arbitrary", "arbitrary", "arbitrary"),
        name="moba",
    )(qm, ka, va, kmean)


NSA_TQ = 2 * SEL_BLOCK
NSA_TK = 2048
SEL_LANES = 64
SUPER_KEYS = SEL_LANES * SEL_BLOCK
CMP_WIDTH_STEP = 256


def _nsa_kernel(n_super, q_ref, kc_ref, vc_ref, ka_ref, va_ref, kvw_ref, gt_ref, e_ref, band_ref, o_ref,
                qa_ref, m_ref, acc_ref):
    qi = pl.program_id(2)
    tq = NSA_TQ
    s0 = qi * tq
    q = q_ref[0, 0, 0]

    def compressed(width):
        sc = _dot_nt(q, kc_ref[0, 0, 0:width, :])
        rq = lax.broadcasted_iota(jnp.int32, (sc.shape[0], LANES), 0) & (tq - 1)
        last = jnp.tile((s0 + rq - (CMP_BLOCK - 1)) >> (CMP_STRIDE.bit_length() - 1), (1, width // LANES))
        sc = jnp.where(lax.broadcasted_iota(jnp.int32, sc.shape, 1) <= last, sc, NEG)
        mx = jnp.max(sc, axis=-1, keepdims=True)
        e = jnp.exp2(sc - mx)
        r = jnp.where(mx > 0.5 * NEG, 1.0 / jnp.sum(e, axis=-1, keepdims=True), 0.0)
        imp = e[0:tq] * r[0:tq]
        for g in range(1, NSA_GROUP):
            imp = imp + e[g * tq:(g + 1) * tq] * r[g * tq:(g + 1) * tq]
        return _dot(e.astype(BF16), vc_ref[0, 0, 0:width, :]) * r, _dot_x2(imp, band_ref[0:width, :])

    n16 = kc_ref.shape[2]
    widths = list(range(CMP_WIDTH_STEP, n16, CMP_WIDTH_STEP)) + [n16]
    o_c, pslc = lax.switch((s0 + tq - 1) // (CMP_WIDTH_STEP * CMP_STRIDE),
                           [functools.partial(compressed, wd) for wd in widths])

    wlen = WINDOW + tq

    def window(kvw, masked):
        sw = masked(_dot_nt(q, kvw[:, 0:HEAD_DIM]))
        pw = jnp.exp2(sw - jnp.max(sw, axis=-1, keepdims=True))
        return _dot(pw.astype(BF16), kvw)[:, HEAD_DIM:2 * HEAD_DIM] * (1.0 / jnp.sum(pw, axis=-1, keepdims=True))

    def window_interior():
        def masked(sw):
            rq = lax.broadcasted_iota(jnp.int32, (sw.shape[0], tq), 0) & (tq - 1)
            c = lax.broadcasted_iota(jnp.int32, (sw.shape[0], tq), 1)
            return jnp.concatenate([jnp.where(c > rq, sw[:, 0:tq], NEG), sw[:, tq:WINDOW],
                                    jnp.where(c <= rq, sw[:, WINDOW:wlen], NEG)], axis=1)

        return window(kvw_ref[0, 0, pl.ds(pl.multiple_of(s0 - WINDOW, tq), wlen), :], masked)

    def window_start():
        def masked(sw):
            kabs = lax.broadcasted_iota(jnp.int32, sw.shape, 1)
            t = s0 + (lax.broadcasted_iota(jnp.int32, sw.shape, 0) & (tq - 1))
            return jnp.where((kabs <= t) & (kabs > t - WINDOW), sw, NEG)

        return window(kvw_ref[0, 0, 0:wlen, :], masked)

    pslc_t = pslc.T
    nb = pslc_t.shape[0]
    blk = lax.broadcasted_iota(jnp.int32, pslc_t.shape, 0)
    cur = (s0 + lax.broadcasted_iota(jnp.int32, pslc_t.shape, 1)) // SEL_BLOCK
    c0 = s0 // SEL_BLOCK
    elig = (blk >= 1) & (blk <= cur - 2)
    sel = _pick_top(jnp.where(elig, pslc_t, -jnp.inf), blk, nb, SEL_TOPK - 3)
    sel = sel | (blk == 0) | (blk == cur - 1)
    past = jnp.where(sel & (blk < c0), 0.0, NEG).T.astype(BF16)
    for st in range(n_super):
        b = past[:, st * SEL_LANES:(st + 1) * SEL_LANES]
        qa_ref[st] = jnp.concatenate([q, jnp.concatenate([b] * NSA_GROUP, axis=0)], axis=1)

    _softmax_init(m_ref, acc_ref)
    d0 = pl.multiple_of(s0, tq)
    s = _dot_nt(q, ka_ref[0, 0, pl.ds(d0, tq), :][:, 0:HEAD_DIM])
    qpos = lax.broadcasted_iota(jnp.int32, s.shape, 0) & (tq - 1)
    kpos = lax.broadcasted_iota(jnp.int32, s.shape, 1)
    _softmax_step(jnp.where(kpos <= qpos, s, NEG), va_ref[0, 0, pl.ds(d0, tq), :], m_ref, acc_ref)

    def step(start, size):
        s = _dot_nt(qa_ref[start // SUPER_KEYS], ka_ref[0, 0, pl.ds(start, size), :])
        _softmax_step(s, va_ref[0, 0, pl.ds(start, size), :], m_ref, acc_ref)

    (o_w,) = _past_keys_loop(s0, NSA_TK, step, parts=4, riders=(
        (window_interior, lambda: lax.cond(s0 >= WINDOW, window_interior, window_start)),))
    o_s = _softmax_result(acc_ref)

    w = NSA_GROUP * HEAD_DIM
    gexp = _dot_x2(gt_ref[...], e_ref[0])
    wide = lambda x: jnp.concatenate([x[g * tq:(g + 1) * tq] for g in range(NSA_GROUP)], axis=1)
    o_ref[...] = (gexp[:, 0:w] * wide(o_c) + gexp[:, w:2 * w] * wide(o_s) + gexp[:, 2 * w:3 * w] * wide(o_w)).astype(BF16)


def _nsa(qs, kcmp, vcmp, ka, va, kvw, gates, batch, seq):
    tq = NSA_TQ
    nq = seq // tq
    nb = seq // SEL_BLOCK
    n16 = seq // CMP_STRIDE
    assert seq % SUPER_KEYS == 0 and tq == 2 * SEL_BLOCK and WINDOW % tq == 0 and NSA_TK >= WINDOW
    n_super = seq // SUPER_KEYS
    rows = NSA_GROUP * tq
    e = np.zeros((NSA_KV_HEADS, GATE_LANES, 3 * NSA_GROUP * HEAD_DIM), np.float32)
    for br in range(3):
        for hk in range(NSA_KV_HEADS):
            for g in range(NSA_GROUP):
                c = (br * NSA_GROUP + g) * HEAD_DIM
                e[hk, br * NSA_HEADS + hk * NSA_GROUP + g, c:c + HEAD_DIM] = 1.0
    nn, jj = np.arange(n16)[:, None], np.arange(nb)[None, :]
    band = ((nn >= 4 * jj - 1) & (nn <= 4 * jj + 3)).astype(np.float32)
    resident = lambda width: pl.BlockSpec((1, 1, seq, width), lambda b, h, i: (b, h, 0, 0))
    w = NSA_GROUP * HEAD_DIM
    return pl.pallas_call(
        functools.partial(_nsa_kernel, n_super),
        grid=(batch, NSA_KV_HEADS, nq),
        in_specs=[pl.BlockSpec((1, 1, 1, rows, HEAD_DIM), lambda b, h, i: (b, h, i, 0, 0)),
                  pl.BlockSpec((1, 1, n16, HEAD_DIM), lambda b, h, i: (b, h, 0, 0)),
                  pl.BlockSpec((1, 1, n16, HEAD_DIM), lambda b, h, i: (b, h, 0, 0)),
                  resident(LANES), resident(LANES), resident(LANES),
                  pl.BlockSpec((tq, GATE_LANES), lambda b, h, i: (b * nq + i, 0)),
                  pl.BlockSpec((1, GATE_LANES, 3 * w), lambda b, h, i: (h, 0, 0)),
                  pl.BlockSpec((n16, nb), lambda b, h, i: (0, 0))],
        out_specs=pl.BlockSpec((tq, w), lambda b, h, i: (b * nq + i, h)),
        out_shape=jax.ShapeDtypeStruct((batch * seq, NSA_W), BF16),
        scratch_shapes=[pltpu.VMEM((n_super, rows, LANES), BF16), pltpu.VMEM((rows, LANES), F32),
                        pltpu.VMEM((rows, LANES), F32)],
        compiler_params=_params("arbitrary", "arbitrary", "arbitrary"),
        name="nsa",
    )(qs, kcmp, vcmp, ka, va, kvw, gates, jnp.asarray(e, BF16), jnp.asarray(band, BF16))


def _odd_out_kernel(om_ref, on_ref, h_ref, w_ref, mg_ref, wr_ref, br_ref, o_ref, gate_ref, grp_ref, cnt_ref):
    acc = h_ref[...] + _dot(on_ref[...], w_ref[MOBA_W:D_MODEL, :])
    for h in range(MOBA_HEADS):
        acc = acc + _dot(om_ref[0, h], w_ref[h * HEAD_DIM:(h + 1) * HEAD_DIM, :])
    o_ref[...] = acc
    _route(acc, mg_ref, wr_ref, br_ref, gate_ref, grp_ref, cnt_ref)


def _odd_out(o_moba, o_nsa, h2, w_out, router, seq, tm=ROW_TILE):
    t = h2.shape[0]
    tps = seq // tm
    row = lambda w: pl.BlockSpec((tm, w), lambda i: (i, 0))
    r_in, r_out, r_shape = _router_specs(t, tm)
    h, *routing = pl.pallas_call(
        _odd_out_kernel,
        grid=(t // tm,),
        in_specs=[pl.BlockSpec((1, MOBA_HEADS, tm, HEAD_DIM), lambda i: (i // tps, 0, i % tps, 0)),
                  row(NSA_W), row(D_MODEL), _full((D_MODEL, D_MODEL))] + r_in,
        out_specs=[row(D_MODEL)] + r_out,
        out_shape=[jax.ShapeDtypeStruct((t, D_MODEL), F32)] + r_shape,
        compiler_params=_params("arbitrary"),
        name="odd_out",
    )(o_moba, o_nsa, h2, w_out, *router)
    return h, routing


def _even_layer(h2, batch, seq, router, norm, w_in, w_out, lam_re, lam_im, log_dt, b_re, b_im, c_re, c_im, d, w_glu,
                conv_w, conv_b):
    u, ub, yb = _even_in(h2, norm.reshape(1, D_MODEL), w_in.astype(BF16), conv_w, conv_b.reshape(1, CONV_WIDTH), seq)
    ypre = _s5_mixer_pre(ub, batch, seq, lam_re, lam_im, log_dt, b_re, b_im, c_re, c_im)
    return _even_out(ypre, u, yb, h2, d.reshape(1, S5_WIDTH), w_glu.astype(BF16), w_out.astype(BF16), router)


def _odd_layer(h2, batch, seq, router, norm, w_in, w_out, moba_q_norm, moba_k_norm, nsa_q_norm, nsa_kcmp_norm, nsa_ksel_norm,
               nsa_kwin_norm, cmp_pe_k, cmp_w1_k, cmp_w2_k, cmp_pe_v, cmp_w1_v, cmp_w2_v):
    qm, kam, kmean, vam, qs, kvc, kas, vas, kvw, gates = _odd_in(
        h2, batch, seq, norm.reshape(1, D_MODEL), w_in, moba_q_norm, moba_k_norm, nsa_q_norm, nsa_ksel_norm, nsa_kwin_norm)
    cmp = _compress(kvc, batch, seq, cmp_pe_k, cmp_w1_k, cmp_w2_k, cmp_pe_v, cmp_w1_v, cmp_w2_v, nsa_kcmp_norm)
    o_moba = _moba(qm, kam, kmean, vam, batch, seq)
    o_nsa = _nsa(qs, cmp[0], cmp[1], kas, vas, kvw, gates, batch, seq)
    return _odd_out(o_moba, o_nsa, h2, w_out.astype(BF16), router, seq)


def kernel(x, ev_norm_mix, ev_w_in, ev_w_out, s5_lam_re, s5_lam_im, s5_log_dt, s5_b_re, s5_b_im, s5_c_re, s5_c_im, s5_d, s5_w_glu, conv_w, conv_b, od_norm_mix, od_w_in, od_w_out, moba_q_norm, moba_k_norm, nsa_q_norm, nsa_kcmp_norm, nsa_ksel_norm, nsa_kwin_norm, cmp_pe_k, cmp_w1_k, cmp_w2_k, cmp_pe_v, cmp_w1_v, cmp_w2_v, moe_norm, moe_w_group, moe_b_group, moe_w_expert, moe_b_expert, moe_w_gate, moe_w_up, moe_w_down):
    batch, seq, _ = x.shape
    depth = moe_norm.shape[0]
    h = x.reshape(batch * seq, D_MODEL)
    for layer in range(depth):
        i = layer // 2
        router = _router_operands(moe_norm[layer], moe_w_group[layer], moe_b_group[layer], moe_w_expert[layer],
                                  moe_b_expert[layer])
        if layer % 2 == 0:
            h, routing = _even_layer(h, batch, seq, router, ev_norm_mix[i], ev_w_in[i], ev_w_out[i], s5_lam_re[i],
                                     s5_lam_im[i], s5_log_dt[i], s5_b_re[i], s5_b_im[i], s5_c_re[i], s5_c_im[i], s5_d[i],
                                     s5_w_glu[i], conv_w[i], conv_b[i])
        else:
            h, routing = _odd_layer(h, batch, seq, router, od_norm_mix[i], od_w_in[i], od_w_out[i], moba_q_norm[i],
                                    moba_k_norm[i], nsa_q_norm[i], nsa_kcmp_norm[i], nsa_ksel_norm[i], nsa_kwin_norm[i],
                                    cmp_pe_k[i], cmp_w1_k[i], cmp_w2_k[i], cmp_pe_v[i], cmp_w1_v[i], cmp_w2_v[i])
        h = _moe(h, moe_norm[layer].reshape(1, D_MODEL), routing, moe_w_gate[layer], moe_w_up[layer], moe_w_down[layer])
    return h.reshape(batch, seq, D_MODEL)
```
